```python
import math
import jax, jax.numpy as jnp
from jax import lax
import numpy as np

D_MODEL = 2048
BATCH = 8
SEQ = 4096
DEPTH = 1

CHUNK = 64
D_MIX = 2 * D_MODEL
D_SSD = D_MIX // 2
D_LRU = D_MIX - D_SSD
SSD_HEAD_DIM = 64
SSD_HEADS = D_SSD // SSD_HEAD_DIM
SSD_GROUPS = 8
SSD_STATE = 128
D_XBC = D_SSD + 2 * SSD_GROUPS * SSD_STATE
LRU_HEADS = 16
LRU_BLOCK = D_LRU // LRU_HEADS
LRU_C = 8.0
CONV_WIDTH = 4
D_FF = 4 * D_MODEL
D_IN = D_SSD + D_XBC + SSD_HEADS + 2 * D_LRU
EPS = 1e-6

kernel_name = "hybrid_ssd_rglru_parallel_block"


def rms_norm(x, g):
    xf = x.astype(jnp.float32)
    y = xf * lax.rsqrt(jnp.mean(xf * xf, axis=-1, keepdims=True) + EPS)
    return (y * g.astype(jnp.float32)).astype(x.dtype)


def causal_dwconv(x, w, b):
    c = x.shape[-1]
    y = lax.conv_general_dilated(
        x, w[:, None, :].astype(x.dtype), window_strides=(1,),
        padding=[(w.shape[0] - 1, 0)],
        dimension_numbers=("NWC", "WIO", "NWC"), feature_group_count=c)
    return y + b.astype(x.dtype)


def ssd_scan(xh, dt, a, bm, cm, d_skip):
    b_, l_, h_, p_ = xh.shape
    g_, n_ = bm.shape[2], bm.shape[3]
    k_ = h_ // g_
    c_ = l_ // CHUNK
    x6 = (xh * dt[..., None]).reshape(b_, c_, CHUNK, g_, k_, p_)
    adt = (dt * a).reshape(b_, c_, CHUNK, g_, k_)
    acs = jnp.cumsum(adt, axis=2)
    bc = bm.reshape(b_, c_, CHUNK, g_, n_)
    cc = cm.reshape(b_, c_, CHUNK, g_, n_)
    causal = jnp.tril(jnp.ones((CHUNK, CHUNK), dtype=bool))[None, None, :, :, None, None]
    seg = acs[:, :, :, None] - acs[:, :, None, :]
    decay = jnp.exp(jnp.where(causal, seg, -jnp.inf))
    scores = jnp.einsum("bclgn,bcsgn->bclsg", cc, bc)
    y_diag = jnp.einsum("bclsg,bclsgk,bcsgkp->bclgkp", scores, decay, x6)
    decay_to_end = jnp.exp(acs[:, :, -1:] - acs)
    states = jnp.einsum("bcsgn,bcsgk,bcsgkp->bcgkpn", bc, decay_to_end, x6)
    chunk_decay = jnp.exp(acs[:, :, -1])

    def step(h, inp):
        s, dcy = inp
        return h * dcy[..., None, None] + s, h

    h0 = jnp.zeros((b_, g_, k_, p_, n_), dtype=xh.dtype)
    _, prev = lax.scan(step, h0, (jnp.moveaxis(states, 1, 0), jnp.moveaxis(chunk_decay, 1, 0)))
    prev = jnp.moveaxis(prev, 0, 1)
    y_off = jnp.einsum("bclgn,bcgkpn,bclgk->bclgkp", cc, prev, jnp.exp(acs))
    y = (y_diag + y_off).reshape(b_, l_, h_, p_)
    return y + xh * d_skip[:, None]


def rg_lru(xl, w_a, b_a, w_x, b_x, lam):
    b_, l_, _ = xl.shape
    xb = xl.reshape(b_, l_, LRU_HEADS, LRU_BLOCK)
    r = jax.nn.sigmoid(jnp.einsum("blhi,hij->blhj", xb, w_a) + b_a).reshape(b_, l_, D_LRU)
    i = jax.nn.sigmoid(jnp.einsum("blhi,hij->blhj", xb, w_x) + b_x).reshape(b_, l_, D_LRU)
    log_a = -LRU_C * r * jax.nn.softplus(-lam)
    a = jnp.exp(log_a)
    u = jnp.sqrt(-jnp.expm1(2.0 * log_a)) * (i * xl)

    def combine(e1, e2):
        a1, b1 = e1
        a2, b2 = e2
        return a1 * a2, a2 * b1 + b2

    _, h = lax.associative_scan(combine, (a, u), axis=1)
    return h


def _fwd_setup_inputs(seed: int = 0) -> dict:
    key = jax.random.key(seed)
    ks = jax.random.split(key, 32)
    f32 = jnp.float32
    nrm = lambda k, s, sc: jax.random.normal(k, s, f32) * sc
    gain = lambda k, s: 1.0 + 0.02 * jax.random.normal(k, s, f32)
    L = DEPTH
    dt0 = jnp.exp(jax.random.uniform(ks[10], (L, SSD_HEADS), f32, math.log(1e-3), math.log(1e-1)))
    a0 = jax.random.uniform(ks[14], (L, D_LRU), f32, 0.9, 0.999)
    a_base = jnp.exp(jnp.log(a0) / LRU_C)
    return {
        "x": jax.random.normal(ks[0], (BATCH, SEQ, D_MODEL), f32),
        "pre_mix_norm": gain(ks[1], (L, D_MODEL)),
        "w_in": nrm(ks[2], (L, D_MODEL, D_IN), D_MODEL ** -0.5),
        "ssd_conv_w": nrm(ks[3], (L, CONV_WIDTH, D_XBC), CONV_WIDTH ** -0.5),
        "ssd_conv_b": nrm(ks[4], (L, D_XBC), 0.01),
        "ssd_dt_bias": dt0 + jnp.log(-jnp.expm1(-dt0)),
        "ssd_a_log": jnp.log(jax.random.uniform(ks[11], (L, SSD_HEADS), f32, 1.0, 16.0)),
        "ssd_d": gain(ks[12], (L, SSD_HEADS)),
        "ssd_norm": gain(ks[13], (L, D_SSD)),
        "lru_conv_w": nrm(ks[5], (L, CONV_WIDTH, D_LRU), CONV_WIDTH ** -0.5),
        "lru_conv_b": nrm(ks[6], (L, D_LRU), 0.01),
        "lru_w_a": nrm(ks[7], (L, LRU_HEADS, LRU_BLOCK, LRU_BLOCK), LRU_BLOCK ** -0.5),
        "lru_b_a": nrm(ks[8], (L, LRU_HEADS, LRU_BLOCK), 0.01),
        "lru_w_x": nrm(ks[9], (L, LRU_HEADS, LRU_BLOCK, LRU_BLOCK), LRU_BLOCK ** -0.5),
        "lru_b_x": nrm(ks[15], (L, LRU_HEADS, LRU_BLOCK), 0.01),
        "lru_lambda": jnp.log(a_base) - jnp.log1p(-a_base),
        "lru_norm": gain(ks[16], (L, D_LRU)),
        "w_out": nrm(ks[17], (L, D_MIX, D_MODEL), D_MIX ** -0.5),
        "post_mix_norm": gain(ks[18], (L, D_MODEL)),
        "pre_mlp_norm": gain(ks[19], (L, D_MODEL)),
        "w_mlp_in": nrm(ks[20], (L, D_MODEL, D_FF), D_MODEL ** -0.5),
        "w_mlp_out": nrm(ks[21], (L, D_FF, D_MODEL), D_FF ** -0.5),
        "post_mlp_norm": gain(ks[22], (L, D_MODEL)),
    }


def _fwd_reference(x, pre_mix_norm, w_in, ssd_conv_w, ssd_conv_b, ssd_dt_bias, ssd_a_log, ssd_d,
              ssd_norm, lru_conv_w, lru_conv_b, lru_w_a, lru_b_a, lru_w_x, lru_b_x, lru_lambda,
              lru_norm, w_out, post_mix_norm, pre_mlp_norm, w_mlp_in, w_mlp_out, post_mlp_norm):
    f32 = jnp.float32
    b_, l_, _ = x.shape
    split_at = np.cumsum([D_SSD, D_XBC, SSD_HEADS, D_LRU]).tolist()
    for li in range(DEPTH):
        h = rms_norm(x, pre_mix_norm[li])
        proj = h @ w_in[li].astype(h.dtype)
        z, xbc, dt_raw, gate_lru, x_lru = jnp.split(proj, split_at, axis=-1)

        xbc = jax.nn.silu(causal_dwconv(xbc, ssd_conv_w[li], ssd_conv_b[li])).astype(f32)
        xs, bm, cm = jnp.split(xbc, [D_SSD, D_SSD + SSD_GROUPS * SSD_STATE], axis=-1)
        dt = jax.nn.softplus(dt_raw.astype(f32) + ssd_dt_bias[li].astype(f32))
        a = -jnp.exp(ssd_a_log[li].astype(f32))
        y_ssd = ssd_scan(xs.reshape(b_, l_, SSD_HEADS, SSD_HEAD_DIM), dt, a,
                         bm.reshape(b_, l_, SSD_GROUPS, SSD_STATE),
                         cm.reshape(b_, l_, SSD_GROUPS, SSD_STATE), ssd_d[li].astype(f32))
        y_ssd = y_ssd.reshape(b_, l_, D_SSD) * jax.nn.silu(z.astype(f32))
        yg = y_ssd.reshape(b_, l_, SSD_GROUPS, D_SSD // SSD_GROUPS)
        yg = yg * lax.rsqrt(jnp.mean(yg * yg, axis=-1, keepdims=True) + EPS)
        y_ssd = (yg.reshape(b_, l_, D_SSD) * ssd_norm[li].astype(f32)).astype(x.dtype)

        xl = causal_dwconv(x_lru, lru_conv_w[li], lru_conv_b[li]).astype(f32)
        hl = rg_lru(xl, lru_w_a[li].astype(f32), lru_b_a[li].astype(f32),
                    lru_w_x[li].astype(f32), lru_b_x[li].astype(f32), lru_lambda[li].astype(f32))
        y_lru = rms_norm(hl * jax.nn.gelu(gate_lru.astype(f32)), lru_norm[li]).astype(x.dtype)

        mix = jnp.concatenate([y_ssd, y_lru], axis=-1) @ w_out[li].astype(x.dtype)
        x = x + rms_norm(mix, post_mix_norm[li])

        hm = rms_norm(x, pre_mlp_norm[li]) @ w_mlp_in[li].astype(x.dtype)
        hm = jnp.square(jax.nn.relu(hm)) @ w_mlp_out[li].astype(x.dtype)
        x = x + rms_norm(hm, post_mlp_norm[li])
    return x


import jax as _jax
import jax.numpy as _jnp

TWIN_FORMAT = 'train_step'
FWD_PARAMS = ['x', 'pre_mix_norm', 'w_in', 'ssd_conv_w', 'ssd_conv_b', 'ssd_dt_bias', 'ssd_a_log', 'ssd_d', 'ssd_norm', 'lru_conv_w', 'lru_conv_b', 'lru_w_a', 'lru_b_a', 'lru_w_x', 'lru_b_x', 'lru_lambda', 'lru_norm', 'w_out', 'post_mix_norm', 'pre_mlp_norm', 'w_mlp_in', 'w_mlp_out', 'post_mlp_norm']
TWIN_WEIGHTS = ['pre_mix_norm', 'w_in', 'ssd_conv_w', 'ssd_conv_b', 'ssd_dt_bias', 'ssd_a_log', 'ssd_d', 'ssd_norm', 'lru_conv_w', 'lru_conv_b', 'lru_w_a', 'lru_b_a', 'lru_w_x', 'lru_b_x', 'lru_lambda', 'lru_norm', 'w_out', 'post_mix_norm', 'pre_mlp_norm', 'w_mlp_in', 'w_mlp_out', 'post_mlp_norm']
TWIN_DIFF_INPUT = 'x'
TWIN_INPUTS = ['x', 'pre_mix_norm', 'w_in', 'ssd_conv_w', 'ssd_conv_b', 'ssd_dt_bias', 'ssd_a_log', 'ssd_d', 'ssd_norm', 'lru_conv_w', 'lru_conv_b', 'lru_w_a', 'lru_b_a', 'lru_w_x', 'lru_b_x', 'lru_lambda', 'lru_norm', 'w_out', 'post_mix_norm', 'pre_mlp_norm', 'w_mlp_in', 'w_mlp_out', 'post_mlp_norm', 'loss_target', 'm_pre_mix_norm', 'm_w_in', 'm_ssd_conv_w', 'm_ssd_conv_b', 'm_ssd_dt_bias', 'm_ssd_a_log', 'm_ssd_d', 'm_ssd_norm', 'm_lru_conv_w', 'm_lru_conv_b', 'm_lru_w_a', 'm_lru_b_a', 'm_lru_w_x', 'm_lru_b_x', 'm_lru_lambda', 'm_lru_norm', 'm_w_out', 'm_post_mix_norm', 'm_pre_mlp_norm', 'm_w_mlp_in', 'm_w_mlp_out', 'm_post_mlp_norm', 'v_pre_mix_norm', 'v_w_in', 'v_ssd_conv_w', 'v_ssd_conv_b', 'v_ssd_dt_bias', 'v_ssd_a_log', 'v_ssd_d', 'v_ssd_norm', 'v_lru_conv_w', 'v_lru_conv_b', 'v_lru_w_a', 'v_lru_b_a', 'v_lru_w_x', 'v_lru_b_x', 'v_lru_lambda', 'v_lru_norm', 'v_w_out', 'v_post_mix_norm', 'v_pre_mlp_norm', 'v_w_mlp_in', 'v_w_mlp_out', 'v_post_mlp_norm']
TWIN_OUTPUTS = ['loss', 'grad_x', 'grad_pre_mix_norm', 'grad_w_in', 'grad_ssd_conv_w', 'grad_ssd_conv_b', 'grad_ssd_dt_bias', 'grad_ssd_a_log', 'grad_ssd_d', 'grad_ssd_norm', 'grad_lru_conv_w', 'grad_lru_conv_b', 'grad_lru_w_a', 'grad_lru_b_a', 'grad_lru_w_x', 'grad_lru_b_x', 'grad_lru_lambda', 'grad_lru_norm', 'grad_w_out', 'grad_post_mix_norm', 'grad_pre_mlp_norm', 'grad_w_mlp_in', 'grad_w_mlp_out', 'grad_post_mlp_norm', 'delta_pre_mix_norm', 'delta_w_in', 'delta_ssd_conv_w', 'delta_ssd_conv_b', 'delta_ssd_dt_bias', 'delta_ssd_a_log', 'delta_ssd_d', 'delta_ssd_norm', 'delta_lru_conv_w', 'delta_lru_conv_b', 'delta_lru_w_a', 'delta_lru_b_a', 'delta_lru_w_x', 'delta_lru_b_x', 'delta_lru_lambda', 'delta_lru_norm', 'delta_w_out', 'delta_post_mix_norm', 'delta_pre_mlp_norm', 'delta_w_mlp_in', 'delta_w_mlp_out', 'delta_post_mlp_norm', 'new_m_pre_mix_norm', 'new_m_w_in', 'new_m_ssd_conv_w', 'new_m_ssd_conv_b', 'new_m_ssd_dt_bias', 'new_m_ssd_a_log', 'new_m_ssd_d', 'new_m_ssd_norm', 'new_m_lru_conv_w', 'new_m_lru_conv_b', 'new_m_lru_w_a', 'new_m_lru_b_a', 'new_m_lru_w_x', 'new_m_lru_b_x', 'new_m_lru_lambda', 'new_m_lru_norm', 'new_m_w_out', 'new_m_post_mix_norm', 'new_m_pre_mlp_norm', 'new_m_w_mlp_in', 'new_m_w_mlp_out', 'new_m_post_mlp_norm', 'new_v_pre_mix_norm', 'new_v_w_in', 'new_v_ssd_conv_w', 'new_v_ssd_conv_b', 'new_v_ssd_dt_bias', 'new_v_ssd_a_log', 'new_v_ssd_d', 'new_v_ssd_norm', 'new_v_lru_conv_w', 'new_v_lru_conv_b', 'new_v_lru_w_a', 'new_v_lru_b_a', 'new_v_lru_w_x', 'new_v_lru_b_x', 'new_v_lru_lambda', 'new_v_lru_norm', 'new_v_w_out', 'new_v_post_mix_norm', 'new_v_pre_mlp_norm', 'new_v_w_mlp_in', 'new_v_w_mlp_out', 'new_v_post_mlp_norm']
TWIN_LEAF_KINDS = {'loss': 'loss', 'grad_x': 'grad_x', 'grad_pre_mix_norm': 'grad_w', 'grad_w_in': 'grad_w', 'grad_ssd_conv_w': 'grad_w', 'grad_ssd_conv_b': 'grad_w', 'grad_ssd_dt_bias': 'grad_w', 'grad_ssd_a_log': 'grad_w', 'grad_ssd_d': 'grad_w', 'grad_ssd_norm': 'grad_w', 'grad_lru_conv_w': 'grad_w', 'grad_lru_conv_b': 'grad_w', 'grad_lru_w_a': 'grad_w', 'grad_lru_b_a': 'grad_w', 'grad_lru_w_x': 'grad_w', 'grad_lru_b_x': 'grad_w', 'grad_lru_lambda': 'grad_w', 'grad_lru_norm': 'grad_w', 'grad_w_out': 'grad_w', 'grad_post_mix_norm': 'grad_w', 'grad_pre_mlp_norm': 'grad_w', 'grad_w_mlp_in': 'grad_w', 'grad_w_mlp_out': 'grad_w', 'grad_post_mlp_norm': 'grad_w', 'delta_pre_mix_norm': 'delta_w', 'delta_w_in': 'delta_w', 'delta_ssd_conv_w': 'delta_w', 'delta_ssd_conv_b': 'delta_w', 'delta_ssd_dt_bias': 'delta_w', 'delta_ssd_a_log': 'delta_w', 'delta_ssd_d': 'delta_w', 'delta_ssd_norm': 'delta_w', 'delta_lru_conv_w': 'delta_w', 'delta_lru_conv_b': 'delta_w', 'delta_lru_w_a': 'delta_w', 'delta_lru_b_a': 'delta_w', 'delta_lru_w_x': 'delta_w', 'delta_lru_b_x': 'delta_w', 'delta_lru_lambda': 'delta_w', 'delta_lru_norm': 'delta_w', 'delta_w_out': 'delta_w', 'delta_post_mix_norm': 'delta_w', 'delta_pre_mlp_norm': 'delta_w', 'delta_w_mlp_in': 'delta_w', 'delta_w_mlp_out': 'delta_w', 'delta_post_mlp_norm': 'delta_w', 'new_m_pre_mix_norm': 'new_m', 'new_m_w_in': 'new_m', 'new_m_ssd_conv_w': 'new_m', 'new_m_ssd_conv_b': 'new_m', 'new_m_ssd_dt_bias': 'new_m', 'new_m_ssd_a_log': 'new_m', 'new_m_ssd_d': 'new_m', 'new_m_ssd_norm': 'new_m', 'new_m_lru_conv_w': 'new_m', 'new_m_lru_conv_b': 'new_m', 'new_m_lru_w_a': 'new_m', 'new_m_lru_b_a': 'new_m', 'new_m_lru_w_x': 'new_m', 'new_m_lru_b_x': 'new_m', 'new_m_lru_lambda': 'new_m', 'new_m_lru_norm': 'new_m', 'new_m_w_out': 'new_m', 'new_m_post_mix_norm': 'new_m', 'new_m_pre_mlp_norm': 'new_m', 'new_m_w_mlp_in': 'new_m', 'new_m_w_mlp_out': 'new_m', 'new_m_post_mlp_norm': 'new_m', 'new_v_pre_mix_norm': 'new_v', 'new_v_w_in': 'new_v', 'new_v_ssd_conv_w': 'new_v', 'new_v_ssd_conv_b': 'new_v', 'new_v_ssd_dt_bias': 'new_v', 'new_v_ssd_a_log': 'new_v', 'new_v_ssd_d': 'new_v', 'new_v_ssd_norm': 'new_v', 'new_v_lru_conv_w': 'new_v', 'new_v_lru_conv_b': 'new_v', 'new_v_lru_w_a': 'new_v', 'new_v_lru_b_a': 'new_v', 'new_v_lru_w_x': 'new_v', 'new_v_lru_b_x': 'new_v', 'new_v_lru_lambda': 'new_v', 'new_v_lru_norm': 'new_v', 'new_v_w_out': 'new_v', 'new_v_post_mix_norm': 'new_v', 'new_v_pre_mlp_norm': 'new_v', 'new_v_w_mlp_in': 'new_v', 'new_v_w_mlp_out': 'new_v', 'new_v_post_mlp_norm': 'new_v'}


def _forward(args):
    return _fwd_reference(*[args[k] for k in FWD_PARAMS])


def _output_shape():
    def fwd():
        inp = _fwd_setup_inputs(0)
        return _fwd_reference(*[inp[k] for k in FWD_PARAMS])
    out = _jax.eval_shape(fwd)
    return out.shape, out.dtype

N_MICROBATCH = 1
ADAM_LR = 0.001
ADAM_B1 = 0.9
ADAM_B2 = 0.999
ADAM_EPS = 1e-08
ADAM_WD = 0.01
ADAM_STEP = 10
PER_EXAMPLE_BATCH_AXIS = {'x': 0, 'loss_target': 0}
SHARED_INPUTS = []
_WEIGHT_DTYPES = {'pre_mix_norm': _jnp.float32, 'w_in': _jnp.float32, 'ssd_conv_w': _jnp.float32, 'ssd_conv_b': _jnp.float32, 'ssd_dt_bias': _jnp.float32, 'ssd_a_log': _jnp.float32, 'ssd_d': _jnp.float32, 'ssd_norm': _jnp.float32, 'lru_conv_w': _jnp.float32, 'lru_conv_b': _jnp.float32, 'lru_w_a': _jnp.float32, 'lru_b_a': _jnp.float32, 'lru_w_x': _jnp.float32, 'lru_b_x': _jnp.float32, 'lru_lambda': _jnp.float32, 'lru_norm': _jnp.float32, 'w_out': _jnp.float32, 'post_mix_norm': _jnp.float32, 'pre_mlp_norm': _jnp.float32, 'w_mlp_in': _jnp.float32, 'w_mlp_out': _jnp.float32, 'post_mlp_norm': _jnp.float32}
MOMENT_SCALE = {'pre_mix_norm': 3.965839e-01, 'w_in': 1.614968e-01, 'ssd_conv_w': 3.958604e-01, 'ssd_conv_b': 1.428390e+00, 'ssd_dt_bias': 5.308265e-01, 'ssd_a_log': 2.105018e+00, 'ssd_d': 1.606736e+00, 'ssd_norm': 8.913898e-01, 'lru_conv_w': 8.240854e-01, 'lru_conv_b': 1.401555e+01, 'lru_w_a': 3.551681e-01, 'lru_b_a': 2.323511e-01, 'lru_w_x': 6.536589e-01, 'lru_b_x': 2.042485e-01, 'lru_lambda': 3.714919e-01, 'lru_norm': 1.037743e+00, 'w_out': 1.316462e+00, 'post_mix_norm': 1.604036e+01, 'pre_mlp_norm': 4.340867e-01, 'w_mlp_in': 2.166453e-01, 'w_mlp_out': 1.365550e+00, 'post_mlp_norm': 1.643034e+01}


def _to_microbatches(a, axis):
    t = _jnp.moveaxis(a, axis, 0)
    t = t.reshape((N_MICROBATCH, t.shape[0] // N_MICROBATCH) + t.shape[1:])
    return _jnp.moveaxis(t, 1, axis + 1)


def setup_inputs(seed: int = 0) -> dict:
    inp = _fwd_setup_inputs(seed)
    key = _jax.random.fold_in(_jax.random.key(seed), 7919)
    shape, _ = _output_shape()
    out = dict(inp)
    out["loss_target"] = _jax.random.normal(_jax.random.fold_in(key, 0), shape, _jnp.float32)
    for i, name in enumerate(TWIN_WEIGHTS):
        w = inp[name].astype(_jnp.float32)
        if MOMENT_SCALE is None:
            s = _jnp.sqrt(_jnp.mean(_jnp.square(w)) + 1e-30)
        else:
            s = MOMENT_SCALE[name]
        km, kv = _jax.random.split(_jax.random.fold_in(key, i + 1))
        out[name] = w
        out["m_" + name] = s * _jax.random.normal(km, w.shape, _jnp.float32)
        out["v_" + name] = (s * s) * _jax.random.uniform(kv, w.shape, _jnp.float32, 0.5, 1.5)
    if N_MICROBATCH > 1:
        for name, axis in PER_EXAMPLE_BATCH_AXIS.items():
            out[name] = _to_microbatches(out[name], axis)
    return {'x': out['x'], 'pre_mix_norm': out['pre_mix_norm'], 'w_in': out['w_in'], 'ssd_conv_w': out['ssd_conv_w'], 'ssd_conv_b': out['ssd_conv_b'], 'ssd_dt_bias': out['ssd_dt_bias'], 'ssd_a_log': out['ssd_a_log'], 'ssd_d': out['ssd_d'], 'ssd_norm': out['ssd_norm'], 'lru_conv_w': out['lru_conv_w'], 'lru_conv_b': out['lru_conv_b'], 'lru_w_a': out['lru_w_a'], 'lru_b_a': out['lru_b_a'], 'lru_w_x': out['lru_w_x'], 'lru_b_x': out['lru_b_x'], 'lru_lambda': out['lru_lambda'], 'lru_norm': out['lru_norm'], 'w_out': out['w_out'], 'post_mix_norm': out['post_mix_norm'], 'pre_mlp_norm': out['pre_mlp_norm'], 'w_mlp_in': out['w_mlp_in'], 'w_mlp_out': out['w_mlp_out'], 'post_mlp_norm': out['post_mlp_norm'], 'loss_target': out['loss_target'], 'm_pre_mix_norm': out['m_pre_mix_norm'], 'm_w_in': out['m_w_in'], 'm_ssd_conv_w': out['m_ssd_conv_w'], 'm_ssd_conv_b': out['m_ssd_conv_b'], 'm_ssd_dt_bias': out['m_ssd_dt_bias'], 'm_ssd_a_log': out['m_ssd_a_log'], 'm_ssd_d': out['m_ssd_d'], 'm_ssd_norm': out['m_ssd_norm'], 'm_lru_conv_w': out['m_lru_conv_w'], 'm_lru_conv_b': out['m_lru_conv_b'], 'm_lru_w_a': out['m_lru_w_a'], 'm_lru_b_a': out['m_lru_b_a'], 'm_lru_w_x': out['m_lru_w_x'], 'm_lru_b_x': out['m_lru_b_x'], 'm_lru_lambda': out['m_lru_lambda'], 'm_lru_norm': out['m_lru_norm'], 'm_w_out': out['m_w_out'], 'm_post_mix_norm': out['m_post_mix_norm'], 'm_pre_mlp_norm': out['m_pre_mlp_norm'], 'm_w_mlp_in': out['m_w_mlp_in'], 'm_w_mlp_out': out['m_w_mlp_out'], 'm_post_mlp_norm': out['m_post_mlp_norm'], 'v_pre_mix_norm': out['v_pre_mix_norm'], 'v_w_in': out['v_w_in'], 'v_ssd_conv_w': out['v_ssd_conv_w'], 'v_ssd_conv_b': out['v_ssd_conv_b'], 'v_ssd_dt_bias': out['v_ssd_dt_bias'], 'v_ssd_a_log': out['v_ssd_a_log'], 'v_ssd_d': out['v_ssd_d'], 'v_ssd_norm': out['v_ssd_norm'], 'v_lru_conv_w': out['v_lru_conv_w'], 'v_lru_conv_b': out['v_lru_conv_b'], 'v_lru_w_a': out['v_lru_w_a'], 'v_lru_b_a': out['v_lru_b_a'], 'v_lru_w_x': out['v_lru_w_x'], 'v_lru_b_x': out['v_lru_b_x'], 'v_lru_lambda': out['v_lru_lambda'], 'v_lru_norm': out['v_lru_norm'], 'v_w_out': out['v_w_out'], 'v_post_mix_norm': out['v_post_mix_norm'], 'v_pre_mlp_norm': out['v_pre_mlp_norm'], 'v_w_mlp_in': out['v_w_mlp_in'], 'v_w_mlp_out': out['v_w_mlp_out'], 'v_post_mlp_norm': out['v_post_mlp_norm']}


def _loss(weights, diff, rest, loss_target):
    with _jax.named_scope("forward"):
        args = {**rest, TWIN_DIFF_INPUT: diff, **{k: w.astype(_WEIGHT_DTYPES[k]) for k, w in weights.items()}}
        y = _forward(args)
    with _jax.named_scope("loss_head"):
        err = _jnp.square(y.astype(_jnp.float32) - loss_target)
        return 0.5 * _jnp.sum(_jnp.mean(err, axis=-1)) if err.ndim else 0.5 * err


def _adamw(w, g, m, v):
    m = ADAM_B1 * m + (1.0 - ADAM_B1) * g
    v = ADAM_B2 * v + (1.0 - ADAM_B2) * _jnp.square(g)
    m_hat = m / (1.0 - ADAM_B1 ** ADAM_STEP)
    v_hat = v / (1.0 - ADAM_B2 ** ADAM_STEP)
    delta = -ADAM_LR * (m_hat / (_jnp.sqrt(v_hat) + ADAM_EPS) + ADAM_WD * w)
    return delta, m, v


def reference(x, pre_mix_norm, w_in, ssd_conv_w, ssd_conv_b, ssd_dt_bias, ssd_a_log, ssd_d, ssd_norm, lru_conv_w, lru_conv_b, lru_w_a, lru_b_a, lru_w_x, lru_b_x, lru_lambda, lru_norm, w_out, post_mix_norm, pre_mlp_norm, w_mlp_in, w_mlp_out, post_mlp_norm, loss_target, m_pre_mix_norm, m_w_in, m_ssd_conv_w, m_ssd_conv_b, m_ssd_dt_bias, m_ssd_a_log, m_ssd_d, m_ssd_norm, m_lru_conv_w, m_lru_conv_b, m_lru_w_a, m_lru_b_a, m_lru_w_x, m_lru_b_x, m_lru_lambda, m_lru_norm, m_w_out, m_post_mix_norm, m_pre_mlp_norm, m_w_mlp_in, m_w_mlp_out, m_post_mlp_norm, v_pre_mix_norm, v_w_in, v_ssd_conv_w, v_ssd_conv_b, v_ssd_dt_bias, v_ssd_a_log, v_ssd_d, v_ssd_norm, v_lru_conv_w, v_lru_conv_b, v_lru_w_a, v_lru_b_a, v_lru_w_x, v_lru_b_x, v_lru_lambda, v_lru_norm, v_w_out, v_post_mix_norm, v_pre_mlp_norm, v_w_mlp_in, v_w_mlp_out, v_post_mlp_norm):
    given = dict(x=x, pre_mix_norm=pre_mix_norm, w_in=w_in, ssd_conv_w=ssd_conv_w, ssd_conv_b=ssd_conv_b, ssd_dt_bias=ssd_dt_bias, ssd_a_log=ssd_a_log, ssd_d=ssd_d, ssd_norm=ssd_norm, lru_conv_w=lru_conv_w, lru_conv_b=lru_conv_b, lru_w_a=lru_w_a, lru_b_a=lru_b_a, lru_w_x=lru_w_x, lru_b_x=lru_b_x, lru_lambda=lru_lambda, lru_norm=lru_norm, w_out=w_out, post_mix_norm=post_mix_norm, pre_mlp_norm=pre_mlp_norm, w_mlp_in=w_mlp_in, w_mlp_out=w_mlp_out, post_mlp_norm=post_mlp_norm, loss_target=loss_target, m_pre_mix_norm=m_pre_mix_norm, m_w_in=m_w_in, m_ssd_conv_w=m_ssd_conv_w, m_ssd_conv_b=m_ssd_conv_b, m_ssd_dt_bias=m_ssd_dt_bias, m_ssd_a_log=m_ssd_a_log, m_ssd_d=m_ssd_d, m_ssd_norm=m_ssd_norm, m_lru_conv_w=m_lru_conv_w, m_lru_conv_b=m_lru_conv_b, m_lru_w_a=m_lru_w_a, m_lru_b_a=m_lru_b_a, m_lru_w_x=m_lru_w_x, m_lru_b_x=m_lru_b_x, m_lru_lambda=m_lru_lambda, m_lru_norm=m_lru_norm, m_w_out=m_w_out, m_post_mix_norm=m_post_mix_norm, m_pre_mlp_norm=m_pre_mlp_norm, m_w_mlp_in=m_w_mlp_in, m_w_mlp_out=m_w_mlp_out, m_post_mlp_norm=m_post_mlp_norm, v_pre_mix_norm=v_pre_mix_norm, v_w_in=v_w_in, v_ssd_conv_w=v_ssd_conv_w, v_ssd_conv_b=v_ssd_conv_b, v_ssd_dt_bias=v_ssd_dt_bias, v_ssd_a_log=v_ssd_a_log, v_ssd_d=v_ssd_d, v_ssd_norm=v_ssd_norm, v_lru_conv_w=v_lru_conv_w, v_lru_conv_b=v_lru_conv_b, v_lru_w_a=v_lru_w_a, v_lru_b_a=v_lru_b_a, v_lru_w_x=v_lru_w_x, v_lru_b_x=v_lru_b_x, v_lru_lambda=v_lru_lambda, v_lru_norm=v_lru_norm, v_w_out=v_w_out, v_post_mix_norm=v_post_mix_norm, v_pre_mlp_norm=v_pre_mlp_norm, v_w_mlp_in=v_w_mlp_in, v_w_mlp_out=v_w_mlp_out, v_post_mlp_norm=v_post_mlp_norm)
    weights = {n: given[n] for n in TWIN_WEIGHTS}
    shared = {n: given[n] for n in SHARED_INPUTS}
    per_example = {n: given[n] for n in ['x']}
    grad_fn = _jax.value_and_grad(_loss, argnums=(0, 1))

    def one_microbatch(ex, loss_target):
        ex = dict(ex)
        diff = ex.pop(TWIN_DIFF_INPUT)
        return grad_fn(weights, diff, {**shared, **ex}, loss_target)

    if N_MICROBATCH == 1:
        loss, (grad_w, grad_x) = one_microbatch(per_example, given["loss_target"])
    else:
        def body(carry, xs):
            loss_sum, grad_sum = carry
            l_k, (gw_k, gx_k) = one_microbatch(xs[0], xs[1])
            with _jax.named_scope("update"):
                return (loss_sum + l_k, _jax.tree.map(_jnp.add, grad_sum, gw_k)), gx_k

        init = (_jnp.zeros((), _jnp.float32), _jax.tree.map(_jnp.zeros_like, weights))
        (loss, grad_w), grad_x = _jax.lax.scan(body, init, (per_example, given["loss_target"]))
    with _jax.named_scope("update"):
        delta_w, new_m, new_v = {}, {}, {}
        for n in TWIN_WEIGHTS:
            delta_w[n], new_m[n], new_v[n] = _adamw(weights[n], grad_w[n], given["m_" + n], given["v_" + n])
    return (loss, grad_x, *[grad_w[n] for n in TWIN_WEIGHTS], *[delta_w[n] for n in TWIN_WEIGHTS],
            *[new_m[n] for n in TWIN_WEIGHTS], *[new_v[n] for n in TWIN_WEIGHTS])
```

```python
import functools

import jax
import jax.numpy as jnp
from jax import lax
from jax.experimental import pallas as pl
from jax.experimental.pallas import tpu as pltpu

F32 = jnp.float32
BF16 = jnp.bfloat16
MESH = pl.DeviceIdType.MESH

EPS = 1e-6
LRU_C = 8.0
ADAM_LR = 0.001
ADAM_B1 = 0.9
ADAM_B2 = 0.999
ADAM_EPS = 1e-08
ADAM_WD = 0.01
ADAM_STEP = 10

N_GROUPS = 8
HEADS_PER_GROUP = 4
HEAD_DIM = 64
GROUP_W = HEADS_PER_GROUP * HEAD_DIM
STATE = 128
LRU_HEADS = 16
LRU_BLOCK = 128
CONV_W = 4
SSD_CHUNK = 128
HALO = 8

VMEM_LIMIT = 48 * 1024 * 1024


def _params(sem=None):
    return pltpu.CompilerParams(dimension_semantics=sem, vmem_limit_bytes=VMEM_LIMIT)


@jax.custom_jvp
def _log1p(x):
    u = 1.0 + x
    d = u - 1.0
    return jnp.where(d == 0.0, x, jnp.log(u) * (x / jnp.where(d == 0.0, 1.0, d)))


@_log1p.defjvp
def _log1p_jvp(primals, tangents):
    (x,), (t,) = primals, tangents
    return _log1p(x), t / (1.0 + x)


@jax.custom_jvp
def _expm1(x):
    u = jnp.exp(x)
    lu = jnp.log(u)
    safe = jnp.where(lu == 0.0, 1.0, lu)
    y = (u - 1.0) * (x / safe)
    y = jnp.where(lu == 0.0, x, y)
    return jnp.where(u == 0.0, -1.0, y)


@_expm1.defjvp
def _expm1_jvp(primals, tangents):
    (x,), (t,) = primals, tangents
    return _expm1(x), t * jnp.exp(x)


def _softplus(x):
    return jnp.maximum(x, 0.0) + _log1p(jnp.exp(-jnp.abs(x)))


def _sigmoid(x):
    return 1.0 / (1.0 + jnp.exp(-x))


def _silu(x):
    return x * _sigmoid(x)


def _gelu(x):
    c = 0.7978845608028654
    return 0.5 * x * (1.0 + jnp.tanh(c * (x + 0.044715 * (x * x * x))))


def _rms(x, g):
    return x * lax.rsqrt(jnp.mean(x * x, axis=-1, keepdims=True) + EPS) * g


def _dot(a, b, dims):
    return lax.dot_general(a.astype(BF16), b.astype(BF16), (dims, ((), ())), preferred_element_type=F32)


_NN = ((1,), (0,))
_NT = ((1,), (1,))
_TN = ((0,), (0,))


def _matmul(a, b, *, mode, m, n, k, tm, tn, tk, out_dtypes, name, a_spec=None, b_spec=None,
            out_specs=None, out_shapes=None, extras=(), epilogue=None):
    tm, tn, tk = min(tm, m), min(tn, n), min(tk, k)
    assert m % tm == 0 and n % tn == 0 and k % tk == 0, (name, m, n, k, tm, tn, tk)
    nk = k // tk
    dims = {"nn": _NN, "nt": _NT, "tn": _TN}[mode]
    if a_spec is None:
        a_spec = pl.BlockSpec((tk, tm), lambda i, j, kk: (kk, i)) if mode == "tn" else pl.BlockSpec((tm, tk), lambda i, j, kk: (i, kk))
    if b_spec is None:
        b_spec = pl.BlockSpec((tn, tk), lambda i, j, kk: (j, kk)) if mode == "nt" else pl.BlockSpec((tk, tn), lambda i, j, kk: (kk, j))
    tile = pl.BlockSpec((tm, tn), lambda i, j, kk: (i, j))
    if out_specs is None:
        out_specs = [tile for _ in out_dtypes]
    if out_shapes is None:
        out_shapes = [jax.ShapeDtypeStruct((m, n), d) for d in out_dtypes]
    n_ex, n_out = len(extras), len(out_dtypes)

    def body(*refs):
        a_ref, b_ref = refs[0], refs[1]
        ex_refs = refs[2:2 + n_ex]
        o_refs = refs[2 + n_ex:2 + n_ex + n_out]
        acc = refs[-1]
        kk = pl.program_id(2)

        @pl.when(kk == 0)
        def _():
            acc[...] = jnp.zeros_like(acc)

        acc[...] += _dot(a_ref[...], b_ref[...], dims)

        @pl.when(kk == nk - 1)
        def _():
            r = acc[...]
            outs = epilogue(r, *[e[...] for e in ex_refs]) if epilogue is not None else (r,)
            for o_ref, o in zip(o_refs, outs):
                o_ref[...] = o.astype(o_ref.dtype)

    outs = pl.pallas_call(
        body, name=name, grid=(m // tm, n // tn, nk),
        in_specs=[a_spec, b_spec] + [tile for _ in extras],
        out_specs=out_specs, out_shape=out_shapes,
        scratch_shapes=[pltpu.VMEM((tm, tn), F32)],
        compiler_params=_params(("parallel", "parallel", "arbitrary")),
    )(a, b, *extras)
    return outs


def _rowwise(fn, rows, bcast, out_rows, out_acc, *, name, tr, sub):
    rows = [r if isinstance(r, tuple) else (r, 0, r.shape[1]) for r in rows]
    row_specs = []
    for arr, c0, w in rows:
        assert c0 % w == 0, (name, c0, w)
        row_specs.append((w, c0 // w))
    rows = [r[0] for r in rows]
    L = rows[0].shape[0]
    tr = min(tr, L)
    sub = min(sub, tr)
    assert L % tr == 0 and tr % sub == 0, (name, L, tr, sub)
    n_r, n_b, n_or, n_oa = len(rows), len(bcast), len(out_rows), len(out_acc)

    def body(*refs):
        r_refs = refs[:n_r]
        b_refs = refs[n_r:n_r + n_b]
        or_refs = refs[n_r + n_b:n_r + n_b + n_or]
        oa_refs = refs[n_r + n_b + n_or:]
        i = pl.program_id(0)

        @pl.when(i == 0)
        def _():
            for o in oa_refs:
                o[...] = jnp.zeros_like(o)

        bvals = [b[...] for b in b_refs]

        def step(s, carry):
            r0 = pl.multiple_of(s * sub, sub)
            tiles = [r[pl.ds(r0, sub), :] for r in r_refs]
            outs = fn(*tiles, *bvals)
            for o_ref, o in zip(or_refs, outs[:n_or]):
                o_ref[pl.ds(r0, sub), :] = o.astype(o_ref.dtype)
            for o_ref, o in zip(oa_refs, outs[n_or:]):
                o_ref[...] += o
            return carry

        if tr == sub:
            step(0, 0)
        else:
            lax.fori_loop(0, tr // sub, step, 0)

    def whole(shape):
        nd = len(shape)
        return pl.BlockSpec(shape, lambda i, _n=nd: (0,) * _n)

    outs = pl.pallas_call(
        body, name=name, grid=(L // tr,),
        in_specs=[pl.BlockSpec((tr, w), lambda i, _c=cb: (i, _c)) for w, cb in row_specs] + [whole(b.shape) for b in bcast],
        out_specs=[pl.BlockSpec((tr, c), lambda i: (i, 0)) for c, _ in out_rows] + [whole(s) for s in out_acc],
        out_shape=[jax.ShapeDtypeStruct((L, c), d) for c, d in out_rows] + [jax.ShapeDtypeStruct(s, F32) for s in out_acc],
        compiler_params=_params(("arbitrary",)),
    )(*rows, *bcast)
    return outs


def _colsum(x):
    return jnp.sum(x, axis=0, keepdims=True)


def _grouped(fn, rows, params, out_rows, out_acc, *, gw, name, tr):
    rows = [r if isinstance(r, tuple) else (r, 0) for r in rows]
    L = rows[0][0].shape[0]
    tr = min(tr, L)
    assert L % tr == 0
    G = None
    for p in params:
        G = p.shape[0] if p.ndim == 3 else p.shape[1] // gw
    cw = G * gw
    n_r, n_p, n_or, n_oa = len(rows), len(params), len(out_rows), len(out_acc)

    def pick(ref, g):
        return ref[g] if len(ref.shape) == 3 else ref[:, g * gw:(g + 1) * gw]

    def body(*refs):
        r_refs = refs[:n_r]
        p_refs = refs[n_r:n_r + n_p]
        or_refs = refs[n_r + n_p:n_r + n_p + n_or]
        oa_refs = refs[n_r + n_p + n_or:]

        @pl.when(pl.program_id(0) == 0)
        def _():
            for o in oa_refs:
                o[...] = jnp.zeros_like(o)

        for g in range(G):
            outs = fn(*[pick(r, g) for r in r_refs], *[pick(p, g) for p in p_refs])
            for o_ref, o in zip(or_refs, outs[:n_or]):
                o_ref[:, g * gw:(g + 1) * gw] = o.astype(o_ref.dtype)
            for o_ref, o in zip(oa_refs, outs[n_or:]):
                if len(o_ref.shape) == 3:
                    o_ref[g] += o
                else:
                    o_ref[:, g * gw:(g + 1) * gw] += o

    def whole(shape):
        nd = len(shape)
        return pl.BlockSpec(shape, lambda i, _n=nd: (0,) * _n)

    for _, c0 in rows:
        assert c0 % cw == 0
    return pl.pallas_call(
        body, name=name, grid=(L // tr,),
        in_specs=[pl.BlockSpec((tr, cw), lambda i, _c=c0 // cw: (i, _c)) for _, c0 in rows] + [whole(p.shape) for p in params],
        out_specs=[pl.BlockSpec((tr, cw), lambda i: (i, 0)) for _ in out_rows] + [whole(s) for s in out_acc],
        out_shape=[jax.ShapeDtypeStruct((L, cw), d) for d in out_rows] + [jax.ShapeDtypeStruct(s, F32) for s in out_acc],
        compiler_params=_params(("arbitrary",)),
    )(*[r[0] for r in rows], *params)


def _dsilu(p):
    s = _sigmoid(p)
    return s + p * s * (1.0 - s)


def _conv_fwd(x, c0, C, w, b, *, silu, name, tr=256, cb=512):
    L = x.shape[0]
    tr = min(tr, L)
    nb, hb = L // tr, tr // HALO
    assert L % tr == 0 and C % cb == 0 and c0 % cb == 0
    n_out = 2 if silu else 1

    def body(x_ref, h_ref, w_ref, b_ref, *rest):
        o_refs, xp = rest[:n_out], rest[n_out]
        xp[0:HALO, :] = h_ref[...]

        @pl.when(pl.program_id(0) == 0)
        def _():
            xp[0:HALO, :] = jnp.zeros((HALO, cb), F32)

        xp[HALO:, :] = x_ref[...]
        wv = w_ref[...]
        y = b_ref[...]
        for k in range(CONV_W):
            y = y + wv[k:k + 1, :] * xp[HALO - 3 + k:HALO - 3 + k + tr, :]
        o_refs[0][...] = y
        if silu:
            o_refs[1][...] = _silu(y)

    return pl.pallas_call(
        body, name=name, grid=(nb, C // cb),
        in_specs=[
            pl.BlockSpec((tr, cb), lambda i, j: (i, c0 // cb + j)),
            pl.BlockSpec((HALO, cb), lambda i, j: (jnp.maximum(i * hb - 1, 0), c0 // cb + j)),
            pl.BlockSpec((CONV_W, cb), lambda i, j: (0, j)),
            pl.BlockSpec((1, cb), lambda i, j: (0, j)),
        ],
        out_specs=[pl.BlockSpec((tr, cb), lambda i, j: (i, j)) for _ in range(n_out)],
        out_shape=[jax.ShapeDtypeStruct((L, C), F32) for _ in range(n_out)],
        scratch_shapes=[pltpu.VMEM((tr + HALO, cb), F32)],
        compiler_params=_params(("parallel", "parallel")),
    )(x, x, w, b)


def _conv_bwd(dact, dc0, pre, x, xc0, C, w, *, silu, name, tr=256, cb=512):
    L = x.shape[0]
    tr = min(tr, L)
    nb, hb = L // tr, tr // HALO
    last_h = L // HALO - 1

    def body(*refs):
        if silu:
            d_ref, dh_ref, p_ref, ph_ref, x_ref, xh_ref, w_ref, dx_ref, dwb_ref, dp, xp = refs
        else:
            d_ref, dh_ref, x_ref, xh_ref, w_ref, dx_ref, dwb_ref, dp, xp = refs
        i = pl.program_id(1)
        dm, dh = d_ref[...], dh_ref[...]
        if silu:
            dm = dm * _dsilu(p_ref[...])
            dh = dh * _dsilu(ph_ref[...])
        dp[0:tr, :] = dm
        dp[tr:, :] = dh

        @pl.when(i == nb - 1)
        def _():
            dp[tr:, :] = jnp.zeros((HALO, cb), F32)

        xp[0:HALO, :] = xh_ref[...]

        @pl.when(i == 0)
        def _():
            xp[0:HALO, :] = jnp.zeros((HALO, cb), F32)
            dwb_ref[...] = jnp.zeros_like(dwb_ref)

        xp[HALO:, :] = x_ref[...]
        wv = w_ref[...]
        dx = jnp.zeros((tr, cb), F32)
        for k in range(CONV_W):
            dx = dx + wv[k:k + 1, :] * dp[3 - k:3 - k + tr, :]
            dwb_ref[k:k + 1, :] += _colsum(dm * xp[HALO - 3 + k:HALO - 3 + k + tr, :])
        dx_ref[...] = dx
        dwb_ref[CONV_W:CONV_W + 1, :] += _colsum(dm)

    def main(c):
        return pl.BlockSpec((tr, cb), lambda j, i: (i, c // cb + j))

    def nxt(c):
        return pl.BlockSpec((HALO, cb), lambda j, i: (jnp.minimum((i + 1) * hb, last_h), c // cb + j))

    in_specs = [main(dc0), nxt(dc0)]
    args = [dact, dact]
    if silu:
        in_specs += [main(0), nxt(0)]
        args += [pre, pre]
    in_specs += [main(xc0), pl.BlockSpec((HALO, cb), lambda j, i: (jnp.maximum(i * hb - 1, 0), xc0 // cb + j)),
                 pl.BlockSpec((CONV_W, cb), lambda j, i: (0, j))]
    args += [x, x, w]
    return pl.pallas_call(
        body, name=name, grid=(C // cb, nb),
        in_specs=in_specs,
        out_specs=[pl.BlockSpec((tr, cb), lambda j, i: (i, j)), pl.BlockSpec((8, cb), lambda j, i: (0, j))],
        out_shape=[jax.ShapeDtypeStruct((L, C), F32), jax.ShapeDtypeStruct((8, C), F32)],
        scratch_shapes=[pltpu.VMEM((tr + HALO, cb), F32), pltpu.VMEM((tr + HALO, cb), F32)],
        compiler_params=_params(("parallel", "arbitrary")),
    )(*args)


def _ssd_f1(dtraw, dtb, alog):
    q = dtraw.shape[0]
    dt = _softplus(dtraw + dtb)
    adt = dt * (-jnp.exp(alog))
    tril = (lax.broadcasted_iota(jnp.int32, (q, q), 0) >= lax.broadcasted_iota(jnp.int32, (q, q), 1)).astype(F32)
    acs = lax.dot_general(tril, adt, (_NN, ((), ())), precision=lax.Precision.HIGHEST, preferred_element_type=F32)
    return dt, acs


def _ssd_group(g, x, bm, cm, dt, acs, drow, hp):
    q = x.shape[0]
    lane = lax.broadcasted_iota(jnp.int32, (1, 128), 1)
    sub = lax.broadcasted_iota(jnp.int32, (128, 1), 0)
    head_of = lax.broadcasted_iota(jnp.int32, (1, GROUP_W), 1) // HEAD_DIM
    is_last = (lax.broadcasted_iota(jnp.int32, (q, 1), 0) == q - 1).astype(F32)
    causal = lax.broadcasted_iota(jnp.int32, (q, q), 0) >= lax.broadcasted_iota(jnp.int32, (q, q), 1)
    acs_end = jnp.sum(acs * is_last, axis=0, keepdims=True)
    acs_t = acs.T
    dt_exp = jnp.zeros((q, GROUP_W), F32)
    acs_exp = jnp.zeros((q, GROUP_W), F32)
    end_exp = jnp.zeros((1, GROUP_W), F32)
    d_exp = jnp.zeros((1, GROUP_W), F32)
    heads = []
    for k in range(HEADS_PER_GROUP):
        h = HEADS_PER_GROUP * g + k
        oh = (lane == h).astype(F32)
        mk = (head_of == k).astype(F32)
        acs_col = jnp.sum(acs * oh, axis=1, keepdims=True)
        acs_row = jnp.sum(acs_t * (sub == h).astype(F32), axis=0, keepdims=True)
        dt_exp = dt_exp + jnp.sum(dt * oh, axis=1, keepdims=True) * mk
        acs_exp = acs_exp + acs_col * mk
        end_exp = end_exp + jnp.sum(acs_end * oh, axis=1, keepdims=True) * mk
        d_exp = d_exp + jnp.sum(drow * oh, axis=1, keepdims=True) * mk
        heads.append((acs_col, acs_row, mk))
    xdt = x * dt_exp
    states = _dot(bm, xdt * jnp.exp(end_exp - acs_exp), _TN)
    y = _dot(cm, hp, _NN) * jnp.exp(acs_exp) + x * d_exp
    scores = _dot(cm, bm, _NT)
    for acs_col, acs_row, mk in heads:
        dec = jnp.exp(jnp.where(causal, acs_col - acs_row, -jnp.inf))
        y = y + _dot(scores * dec, xdt * mk, _NN)
    return y, hp * jnp.exp(end_exp) + states


def _ssd_fwd(act, dtraw, dtb, alog, drow, *, name):
    L = act.shape[0]
    q = min(SSD_CHUNK, L)
    nc = L // q
    d_ssd = N_GROUPS * GROUP_W

    def body(act_ref, dt_ref, dtb_ref, alog_ref, drow_ref, y_ref, hst_ref, h):
        @pl.when(pl.program_id(0) == 0)
        def _():
            h[...] = jnp.zeros_like(h)

        dt, acs = _ssd_f1(dt_ref[...], dtb_ref[...], alog_ref[...])
        drow = drow_ref[...]
        for g in range(N_GROUPS):
            hp = h[g]
            hst_ref[0, g] = hp
            y, hn = _ssd_group(g, act_ref[:, g * GROUP_W:(g + 1) * GROUP_W],
                               act_ref[:, d_ssd + g * STATE:d_ssd + (g + 1) * STATE],
                               act_ref[:, d_ssd + (N_GROUPS + g) * STATE:d_ssd + (N_GROUPS + g + 1) * STATE],
                               dt, acs, drow, hp)
            y_ref[:, g * GROUP_W:(g + 1) * GROUP_W] = y
            h[g] = hn

    row = pl.BlockSpec((1, 128), lambda i: (0, 0))
    return pl.pallas_call(
        body, name=name, grid=(nc,),
        in_specs=[pl.BlockSpec((q, act.shape[1]), lambda i: (i, 0)), pl.BlockSpec((q, 128), lambda i: (i, 0)), row, row, row],
        out_specs=[pl.BlockSpec((q, d_ssd), lambda i: (i, 0)),
                   pl.BlockSpec((1, N_GROUPS, STATE, GROUP_W), lambda i: (i, 0, 0, 0))],
        out_shape=[jax.ShapeDtypeStruct((L, d_ssd), F32), jax.ShapeDtypeStruct((nc, N_GROUPS, STATE, GROUP_W), F32)],
        scratch_shapes=[pltpu.VMEM((N_GROUPS, STATE, GROUP_W), F32)],
        compiler_params=_params(("arbitrary",)),
    )(act, dtraw, dtb, alog, drow)


def _ssd_bwd(act, dtraw, dy, hst, dtb, alog, drow, *, name):
    L = act.shape[0]
    q = min(SSD_CHUNK, L)
    nc = L // q
    d_ssd = N_GROUPS * GROUP_W

    def body(act_ref, dt_ref, dy_ref, hst_ref, dtb_ref, alog_ref, drow_ref, dact_ref, ddt_ref, dpar_ref, dh):
        @pl.when(pl.program_id(0) == 0)
        def _():
            dh[...] = jnp.zeros_like(dh)
            dpar_ref[...] = jnp.zeros_like(dpar_ref)

        (dt, acs), f1_vjp = jax.vjp(_ssd_f1, dt_ref[...], dtb_ref[...], alog_ref[...])
        drow = drow_ref[...]
        ddt = jnp.zeros_like(dt)
        dacs = jnp.zeros_like(acs)
        ddrow = jnp.zeros_like(drow)
        for g in range(N_GROUPS):
            xs = slice(g * GROUP_W, (g + 1) * GROUP_W)
            bs = slice(d_ssd + g * STATE, d_ssd + (g + 1) * STATE)
            cs = slice(d_ssd + (N_GROUPS + g) * STATE, d_ssd + (N_GROUPS + g + 1) * STATE)
            _, f2_vjp = jax.vjp(functools.partial(_ssd_group, g), act_ref[:, xs], act_ref[:, bs], act_ref[:, cs],
                                dt, acs, drow, hst_ref[0, g])
            dx, dbm, dcm, ddt_g, dacs_g, ddrow_g, dhp = f2_vjp((dy_ref[:, xs], dh[g]))
            dact_ref[:, xs] = dx
            dact_ref[:, bs] = dbm
            dact_ref[:, cs] = dcm
            dh[g] = dhp
            ddt, dacs, ddrow = ddt + ddt_g, dacs + dacs_g, ddrow + ddrow_g
        ddtraw, ddtb, dalog = f1_vjp((ddt, dacs))
        ddt_ref[...] = ddtraw
        dpar_ref[0:1, :] += ddtb
        dpar_ref[1:2, :] += dalog
        dpar_ref[2:3, :] += ddrow

    row = pl.BlockSpec((1, 128), lambda i: (0, 0))
    rev = lambda i: (nc - 1 - i, 0)
    return pl.pallas_call(
        body, name=name, grid=(nc,),
        in_specs=[pl.BlockSpec((q, act.shape[1]), rev), pl.BlockSpec((q, 128), rev), pl.BlockSpec((q, d_ssd), rev),
                  pl.BlockSpec((1, N_GROUPS, STATE, GROUP_W), lambda i: (nc - 1 - i, 0, 0, 0)), row, row, row],
        out_specs=[pl.BlockSpec((q, act.shape[1]), rev), pl.BlockSpec((q, 128), rev), pl.BlockSpec((8, 128), lambda i: (0, 0))],
        out_shape=[jax.ShapeDtypeStruct(act.shape, F32), jax.ShapeDtypeStruct((L, 128), F32), jax.ShapeDtypeStruct((8, 128), F32)],
        scratch_shapes=[pltpu.VMEM((N_GROUPS, STATE, GROUP_W), F32)],
        compiler_params=_params(("arbitrary",)),
    )(act, dtraw, dy, hst, dtb, alog, drow)


def _gate_head(xl, wa, ba, wx, bx, lam):
    r = _sigmoid(_dot(xl, wa, _NN) + ba)
    i = _sigmoid(_dot(xl, wx, _NN) + bx)
    log_a = -LRU_C * r * _softplus(-lam)
    return jnp.exp(log_a), jnp.sqrt(-_expm1(2.0 * log_a)) * (i * xl)


def _gate_head_bwd(xl, da, du, wa, ba, wx, bx, lam):
    _, vjp = jax.vjp(_gate_head, xl, wa, ba, wx, bx, lam)
    dxl, dwa, dba, dwx, dbx, dlam = vjp((da, du))
    return dxl, dwa, dba, dwx, dbx, dlam


def _scan_tile(a, b, rows, reverse):
    for d in (1, 2, 4):
        if reverse:
            keep = rows < 8 - d
            a_sh, b_sh = pltpu.roll(a, 8 - d, 0), pltpu.roll(b, 8 - d, 0)
        else:
            keep = rows >= d
            a_sh, b_sh = pltpu.roll(a, d, 0), pltpu.roll(b, d, 0)
        b = b + a * jnp.where(keep, b_sh, 0.0)
        a = a * jnp.where(keep, a_sh, 1.0)
    return a, b


def _lru_scan_fwd(a, u, *, name, tr=512, cb=1024):
    L, C = a.shape
    tr, cb = min(tr, L), min(cb, C)

    def body(a_ref, u_ref, h_ref, hp_ref, carry):
        @pl.when(pl.program_id(1) == 0)
        def _():
            carry[...] = jnp.zeros_like(carry)

        rows = lax.broadcasted_iota(jnp.int32, (8, cb), 0)

        def tile(t, hc):
            r0 = pl.multiple_of(t * 8, 8)
            pa, hb = _scan_tile(a_ref[pl.ds(r0, 8), :], u_ref[pl.ds(r0, 8), :], rows, False)
            h = hb + pa * hc
            h_ref[pl.ds(r0, 8), :] = h
            hp_ref[pl.ds(r0, 8), :] = jnp.where(rows >= 1, pltpu.roll(h, 1, 0), hc)
            return h[7:8, :]

        carry[...] = lax.fori_loop(0, tr // 8, tile, carry[...])

    blk = pl.BlockSpec((tr, cb), lambda j, i: (i, j))
    return pl.pallas_call(
        body, name=name, grid=(C // cb, L // tr),
        in_specs=[blk, blk], out_specs=[blk, blk],
        out_shape=[jax.ShapeDtypeStruct((L, C), F32), jax.ShapeDtypeStruct((L, C), F32)],
        scratch_shapes=[pltpu.VMEM((1, cb), F32)],
        compiler_params=_params(("parallel", "arbitrary")),
    )(a, u)


def _lru_scan_bwd(a, hprev, dh, *, name, tr=512, cb=1024):
    L, C = a.shape
    tr, cb = min(tr, L), min(cb, C)
    nb = L // tr

    def body(a_ref, hp_ref, dh_ref, da_ref, du_ref, carry):
        @pl.when(pl.program_id(1) == 0)
        def _():
            carry[...] = jnp.zeros_like(carry)

        rows = lax.broadcasted_iota(jnp.int32, (8, cb), 0)

        def tile(t, gc):
            r0 = pl.multiple_of((tr // 8 - 1 - t) * 8, 8)
            av, dv = a_ref[pl.ds(r0, 8), :], dh_ref[pl.ds(r0, 8), :]
            pa, gb = _scan_tile(av, av * dv, rows, True)
            big = gb + pa * gc
            g = dv + jnp.where(rows < 7, pltpu.roll(big, 7, 0), gc)
            du_ref[pl.ds(r0, 8), :] = g
            da_ref[pl.ds(r0, 8), :] = g * hp_ref[pl.ds(r0, 8), :]
            return big[0:1, :]

        carry[...] = lax.fori_loop(0, tr // 8, tile, carry[...])

    blk = pl.BlockSpec((tr, cb), lambda j, i: (nb - 1 - i, j))
    return pl.pallas_call(
        body, name=name, grid=(C // cb, nb),
        in_specs=[blk, blk, blk], out_specs=[blk, blk],
        out_shape=[jax.ShapeDtypeStruct((L, C), F32), jax.ShapeDtypeStruct((L, C), F32)],
        scratch_shapes=[pltpu.VMEM((1, cb), F32)],
        compiler_params=_params(("parallel", "arbitrary")),
    )(a, hprev, dh)


def _ssd_gate(y, z, n):
    v = y * _silu(z)
    return v * lax.rsqrt(jnp.mean(v * v, axis=-1, keepdims=True) + EPS) * n


def _ssd_gate_bwd(y, z, dy, n):
    _, vjp = jax.vjp(_ssd_gate, y, z, n)
    return vjp(dy)


def _lru_out(hl, gate, n):
    return _rms(hl * _gelu(gate), n)


def _lru_out_bwd(hl, gate, dy, n):
    _, vjp = jax.vjp(_lru_out, hl, gate, n)
    return vjp(dy)


def _mid(x, mix, pm, pmlp):
    x1 = x + _rms(mix, pm)
    return x1, _rms(x1, pmlp)


def _mid_bwd(x, mix, dx1p, dh2, pm, pmlp):
    _, vjp = jax.vjp(_mid, x, mix, pm, pmlp)
    dx, dmix, dpm, dpmlp = vjp((dx1p, dh2))
    return dmix, dx, dpm, dpmlp


def _loss_bwd(hm2, x1, tgt, g):
    def lossf(hm2, x1, g):
        e = x1 + _rms(hm2, g) - tgt
        return 0.5 * jnp.sum(jnp.mean(e * e, axis=-1, keepdims=True), axis=0, keepdims=True)

    val, vjp = jax.vjp(lossf, hm2, x1, g)
    dhm2, dx1, dg = vjp(jnp.ones((1, 1), F32))
    return dhm2, dx1, dg, val * jnp.ones((1, 128), F32)


def _in_bwd(x, dh_a, dh_b, dx1, g):
    _, vjp = jax.vjp(_rms, x, g)
    dx, dg = vjp(dh_a + dh_b)
    return dx + dx1, dg


def _adamw(w, g, m, v):
    m = ADAM_B1 * m + (1.0 - ADAM_B1) * g
    v = ADAM_B2 * v + (1.0 - ADAM_B2) * (g * g)
    m_hat = m / (1.0 - ADAM_B1 ** ADAM_STEP)
    v_hat = v / (1.0 - ADAM_B2 ** ADAM_STEP)
    return -ADAM_LR * (m_hat / (jnp.sqrt(v_hat) + ADAM_EPS) + ADAM_WD * w), m, v


def _local_step(x, tgt, p):
    L, D = x.shape
    d_ssd, d_xbc, d_lru, d_mix, d_ff = 2048, 4096, 2048, 4096, 8192
    n_main = d_ssd + d_xbc + 2 * d_lru
    c_xbc, c_gate, c_xl = d_ssd, d_ssd + d_xbc, d_ssd + d_xbc + d_lru
    TR, SUB = 256, 16
    mm = dict(tm=1024, tn=1024, tk=1024)

    (h,) = _rowwise(lambda xt, g: (_rms(xt, g),), [x], [p["pre_mix_norm"]], [(D, BF16)], [], name="f_prenorm", tr=TR, sub=SUB)
    (proj,) = _matmul(h, p["w_main"], mode="nn", m=L, n=n_main, k=D, out_dtypes=[F32], name="f_inproj", **mm)
    (dtraw,) = _matmul(h, p["w_dt"], mode="nn", m=L, n=128, k=D, out_dtypes=[F32], name="f_dtproj", **mm)
    pre, act = _conv_fwd(proj, c_xbc, d_xbc, p["ssd_conv_w"], p["ssd_conv_b"], silu=True, name="f_ssdconv")
    (xl,) = _conv_fwd(proj, c_xl, d_lru, p["lru_conv_w"], p["lru_conv_b"], silu=False, name="f_lruconv")
    yraw, hst = _ssd_fwd(act, dtraw, p["dtb"], p["alog"], p["drow"], name="f_ssd")
    (yssd,) = _grouped(lambda y, z, n: (_ssd_gate(y, z, n),), [yraw, (proj, 0)], [p["ssd_norm"]], [BF16], [],
                       gw=GROUP_W, name="f_ssdgate", tr=TR)
    gate_p = [p["lru_w_a"], p["lru_b_a"], p["lru_w_x"], p["lru_b_x"], p["lru_lambda"]]
    a, u = _grouped(_gate_head, [xl], gate_p, [F32, F32], [], gw=LRU_BLOCK, name="f_lrugates", tr=TR)
    hl, hprev = _lru_scan_fwd(a, u, name="f_lruscan")
    (ylru,) = _rowwise(lambda ht, gt, n: (_lru_out(ht, gt, n),), [hl, (proj, c_gate, d_lru)], [p["lru_norm"]],
                       [(d_lru, BF16)], [], name="f_lruout", tr=TR, sub=SUB)
    ycat = jnp.concatenate([yssd, ylru], axis=1)
    (mix,) = _matmul(ycat, p["w_out"], mode="nn", m=L, n=D, k=d_mix, out_dtypes=[F32], name="f_outproj", **mm)
    x1, h2 = _rowwise(_mid, [x, mix], [p["post_mix_norm"], p["pre_mlp_norm"]], [(D, F32), (D, BF16)], [],
                      name="f_mid", tr=TR, sub=SUB)
    nb_mi = (d_ff // 4) // mm["tn"]
    hm, act2 = _matmul(h2, p["w_mi"], mode="nn", m=L, n=d_ff, k=D, out_dtypes=[F32, BF16], name="f_mlpin",
                       b_spec=pl.BlockSpec((None, mm["tk"], mm["tn"]), lambda i, j, kk: (j // nb_mi, kk, j % nb_mi)),
                       epilogue=lambda r: (r, jnp.square(jnp.maximum(r, 0.0))), **mm)
    (hm2,) = _matmul(act2, p["w_mo"], mode="nn", m=L, n=D, k=d_ff, out_dtypes=[F32], name="f_mlpout", **mm)

    dhm2, dx1p, d_post_mlp, loss = _rowwise(_loss_bwd, [hm2, x1, tgt], [p["post_mlp_norm"]], [(D, BF16), (D, F32)],
                                            [(1, D), (1, 128)], name="b_loss", tr=TR, sub=SUB)
    (dhm,) = _matmul(dhm2, p["w_mo"], mode="nt", m=L, n=d_ff, k=D, out_dtypes=[BF16], name="b_mlpout_dx", extras=[hm],
                     epilogue=lambda r, hmv: (r * (2.0 * jnp.maximum(hmv, 0.0)),), **mm)
    (dw_mo,) = _matmul(act2, dhm2, mode="tn", m=d_ff, n=D, k=L, out_dtypes=[BF16], name="b_mlpout_dw", **mm)
    kb_mi = (d_ff // 4) // mm["tk"]
    (dh2,) = _matmul(dhm, p["w_mi"], mode="nt", m=L, n=D, k=d_ff, out_dtypes=[F32], name="b_mlpin_dx",
                     b_spec=pl.BlockSpec((None, mm["tn"], mm["tk"]), lambda i, j, kk: (kk // kb_mi, j, kk % kb_mi)), **mm)
    (dw_mi,) = _matmul(h2, dhm, mode="tn", m=D, n=d_ff, k=L, out_dtypes=[BF16], name="b_mlpin_dw",
                       out_specs=[pl.BlockSpec((None, mm["tm"], mm["tn"]), lambda i, j, kk: (j // nb_mi, i, j % nb_mi))],
                       out_shapes=[jax.ShapeDtypeStruct((4, D, d_ff // 4), BF16)], **mm)
    dmix, dx1, d_post_mix, d_pre_mlp = _rowwise(_mid_bwd, [x, mix, dx1p, dh2], [p["post_mix_norm"], p["pre_mlp_norm"]],
                                                [(D, BF16), (D, F32)], [(1, D), (1, D)], name="b_mid", tr=TR, sub=SUB)
    (dycat,) = _matmul(dmix, p["w_out"], mode="nt", m=L, n=d_mix, k=D, out_dtypes=[F32], name="b_outproj_dx", **mm)
    (dw_out,) = _matmul(ycat, dmix, mode="tn", m=d_mix, n=D, k=L, out_dtypes=[BF16], name="b_outproj_dw", **mm)
    dhl, dgate, d_lru_norm = _rowwise(_lru_out_bwd, [hl, (proj, c_gate, d_lru), (dycat, d_ssd, d_lru)], [p["lru_norm"]],
                                      [(d_lru, F32), (d_lru, BF16)], [(1, d_lru)], name="b_lruout", tr=TR, sub=SUB)
    da, du = _lru_scan_bwd(a, hprev, dhl, name="b_lruscan")
    dxl, d_wa, d_ba, d_wx, d_bx, d_lam = _grouped(
        _gate_head_bwd, [xl, da, du], gate_p, [F32],
        [(LRU_HEADS, LRU_BLOCK, LRU_BLOCK), (1, d_lru), (LRU_HEADS, LRU_BLOCK, LRU_BLOCK), (1, d_lru), (1, d_lru)],
        gw=LRU_BLOCK, name="b_lrugates", tr=TR)
    dxlru, dwb_lru = _conv_bwd(dxl, 0, None, proj, c_xl, d_lru, p["lru_conv_w"], silu=False, name="b_lruconv")
    dyraw, dz, d_ssd_norm = _grouped(_ssd_gate_bwd, [yraw, (proj, 0), (dycat, 0)], [p["ssd_norm"]], [F32, BF16], [(1, d_ssd)],
                                     gw=GROUP_W, name="b_ssdgate", tr=TR)
    dact, ddtraw, dpar = _ssd_bwd(act, dtraw, dyraw, hst, p["dtb"], p["alog"], p["drow"], name="b_ssd")
    dxbc, dwb_ssd = _conv_bwd(dact, 0, pre, proj, c_xbc, d_xbc, p["ssd_conv_w"], silu=True, name="b_ssdconv")
    dproj = jnp.concatenate([dz, dxbc.astype(BF16), dgate, dxlru.astype(BF16)], axis=1)
    (dh_a,) = _matmul(dproj, p["w_main"], mode="nt", m=L, n=D, k=n_main, out_dtypes=[F32], name="b_inproj_dx", **mm)
    (dh_b,) = _matmul(ddtraw, p["w_dt"], mode="nt", m=L, n=D, k=128, out_dtypes=[F32], name="b_dtproj_dx", **mm)
    (dw_main,) = _matmul(h, dproj, mode="tn", m=D, n=n_main, k=L, out_dtypes=[BF16], name="b_inproj_dw", **mm)
    (dw_dt,) = _matmul(h, ddtraw, mode="tn", m=D, n=128, k=L, out_dtypes=[BF16], name="b_dtproj_dw", **mm)
    grad_x, d_pre_mix = _rowwise(_in_bwd, [x, dh_a, dh_b, dx1], [p["pre_mix_norm"]], [(D, F32)], [(1, D)],
                                 name="b_prenorm", tr=TR, sub=SUB)

    large = dict(w_main=dw_main, w_dt=dw_dt, w_out=dw_out, w_mi=dw_mi, w_mo=dw_mo)
    small = dict(loss=loss, pre_mix_norm=d_pre_mix, ssd_conv=dwb_ssd, ssd_par=dpar, ssd_norm=d_ssd_norm, lru_conv=dwb_lru,
                 lru_w_a=d_wa, lru_b_a=d_ba, lru_w_x=d_wx, lru_b_x=d_bx, lru_lambda=d_lam, lru_norm=d_lru_norm,
                 post_mix_norm=d_post_mix, pre_mlp_norm=d_pre_mlp, post_mlp_norm=d_post_mlp)
    return grad_x, large, small


ANY = pl.BlockSpec(memory_space=pl.ANY)


def _place():
    return lax.axis_index("x"), lax.axis_index("y"), lax.axis_index("c")


def _other_chips(x, y):
    return [(1 - x, y), (x, 1 - y), (1 - x, 1 - y)]


def _allgather8(blk, *, name):
    r, n = blk.shape

    def body(x_ref, out_ref, send_sems, recv_sems, local_sem):
        x, y, c = _place()
        me, sibling = (x, y, c), (x, y, 1 - c)
        chips = _other_chips(x, y)

        def rows(px, py, pc):
            return out_ref.at[pl.ds((4 * px + 2 * py + pc) * r, r), :]

        def copy(k, block, to, src=None):
            return pltpu.make_async_remote_copy(
                src_ref=rows(*block) if src is None else src, dst_ref=rows(*block),
                send_sem=send_sems.at[k], recv_sem=recv_sems.at[k], device_id=to, device_id_type=MESH)

        mine = pltpu.make_async_copy(x_ref, rows(*me), local_sem)
        mine.start()
        first = [copy(0, me, sibling, src=x_ref)]
        first += [copy(1 + k, me, (*chip, c), src=x_ref) for k, chip in enumerate(chips)]
        for cp in first:
            cp.start()
        passed = [copy(4 + k, (*chip, c), sibling) for k, chip in enumerate(chips)]
        for k, chip in enumerate(chips):
            copy(1 + k, (*chip, c), me).wait_recv()
            passed[k].start()
        copy(0, sibling, me).wait_recv()
        for k, chip in enumerate(chips):
            copy(4 + k, (*chip, 1 - c), me).wait_recv()
        for cp in first + passed:
            cp.wait_send()
        mine.wait()

    return pl.pallas_call(
        body, name=name,
        out_shape=jax.ShapeDtypeStruct((8 * r, n), blk.dtype),
        in_specs=[pl.BlockSpec(memory_space=pltpu.VMEM)], out_specs=pl.BlockSpec(memory_space=pltpu.VMEM),
        scratch_shapes=[pltpu.SemaphoreType.DMA((7,)), pltpu.SemaphoreType.DMA((7,)), pltpu.SemaphoreType.DMA],
        compiler_params=pltpu.CompilerParams(vmem_limit_bytes=VMEM_LIMIT),
    )(blk)


def _sum8(g, *, name, tr=232):
    _, r, n = g.shape
    tr = min(tr, r)
    assert r % tr == 0

    def body(g_ref, o_ref):
        s = g_ref[0]
        for k in range(1, 8):
            s = s + g_ref[k]
        o_ref[...] = s

    return pl.pallas_call(
        body, name=name, grid=(r // tr,),
        in_specs=[pl.BlockSpec((8, tr, n), lambda i: (0, i, 0))], out_specs=pl.BlockSpec((tr, n), lambda i: (i, 0)),
        out_shape=jax.ShapeDtypeStruct((r, n), g.dtype), compiler_params=_params(("parallel",)),
    )(g)


def _gather_weights(shards, *, name):
    n = len(shards)

    def body(*refs):
        s_refs, g_refs = refs[:n], refs[n:2 * n]
        send_sems, recv_sems, local_sems = refs[2 * n:]
        x, y, c = _place()
        j, sibling = 2 * x + y, (x, y, 1 - c)
        chips = _other_chips(x, y)

        def cp(w, k, src, dst, to):
            return pltpu.make_async_remote_copy(src_ref=src, dst_ref=dst, send_sem=send_sems.at[6 * w + k],
                                                recv_sem=recv_sems.at[6 * w + k], device_id=to, device_id_type=MESH)

        def part(w, chip_idx, hc):
            rh = s_refs[w].shape[0] // 2
            return g_refs[w].at[chip_idx, pl.ds(hc * rh, rh), :]

        local, sends = [], []
        for w in range(n):
            rh = s_refs[w].shape[0] // 2
            lc = pltpu.make_async_copy(s_refs[w], g_refs[w].at[j], local_sems.at[w])
            lc.start()
            local.append(lc)
            for k, chip in enumerate(chips):
                d = cp(w, k, s_refs[w].at[pl.ds(c * rh, rh), :], part(w, j, c), (*chip, c))
                d.start()
                sends.append(d)
        for w in range(n):
            for k, (px, py) in enumerate(chips):
                landed = part(w, 2 * px + py, c)
                cp(w, k, landed, landed, (px, py, c)).wait_recv()
                d = cp(w, 3 + k, landed, landed, sibling)
                d.start()
                sends.append(d)
        for w in range(n):
            for k, (px, py) in enumerate(chips):
                other = part(w, 2 * px + py, 1 - c)
                cp(w, 3 + k, other, other, sibling).wait_recv()
        for d in sends:
            d.wait_send()
        for lc in local:
            lc.wait()

    return pl.pallas_call(
        body, name=name,
        out_shape=[jax.ShapeDtypeStruct((4,) + s.shape, s.dtype) for s in shards],
        in_specs=[ANY] * n, out_specs=[ANY] * n,
        scratch_shapes=[pltpu.SemaphoreType.DMA((6 * n,)), pltpu.SemaphoreType.DMA((6 * n,)), pltpu.SemaphoreType.DMA((n,))],
    )(*shards)


def _rs_sibling(gs, *, name):
    n = len(gs)

    def body(*refs):
        g_refs, got_refs, own_refs = refs[:n], refs[n:2 * n], refs[2 * n:3 * n]
        send_sems, recv_sems, local_sems = refs[3 * n:]
        x, y, c = _place()
        sibling = (x, y, 1 - c)
        copies = []
        for w in range(n):
            rh = g_refs[w].shape[1] // 2
            lc = pltpu.make_async_copy(g_refs[w].at[:, pl.ds(c * rh, rh), :], own_refs[w], local_sems.at[w])
            d = pltpu.make_async_remote_copy(src_ref=g_refs[w].at[:, pl.ds((1 - c) * rh, rh), :], dst_ref=got_refs[w],
                                             send_sem=send_sems.at[w], recv_sem=recv_sems.at[w],
                                             device_id=sibling, device_id_type=MESH)
            lc.start()
            d.start()
            copies.append((lc, d))
        for lc, d in copies:
            d.wait_recv()
            d.wait_send()
            lc.wait()

    half = [jax.ShapeDtypeStruct((4, g.shape[1] // 2, g.shape[2]), g.dtype) for g in gs]
    outs = pl.pallas_call(
        body, name=name, out_shape=half + half, in_specs=[ANY] * n, out_specs=[ANY] * (2 * n),
        scratch_shapes=[pltpu.SemaphoreType.DMA((n,)), pltpu.SemaphoreType.DMA((n,)), pltpu.SemaphoreType.DMA((n,))],
    )(*gs)
    return outs[:n], outs[n:]


def _rs_chips(ss, *, name):
    n = len(ss)

    def body(*refs):
        s_refs, got_refs, own_refs = refs[:n], refs[n:4 * n], refs[4 * n:5 * n]
        send_sems, recv_sems, local_sems = refs[5 * n:]
        x, y, c = _place()
        j = 2 * x + y
        chips = _other_chips(x, y)
        copies, local = [], []
        for w in range(n):
            lc = pltpu.make_async_copy(s_refs[w].at[j], own_refs[w], local_sems.at[w])
            lc.start()
            local.append(lc)
            for k, (px, py) in enumerate(chips):
                d = pltpu.make_async_remote_copy(src_ref=s_refs[w].at[2 * px + py], dst_ref=got_refs[3 * w + k],
                                                 send_sem=send_sems.at[3 * w + k], recv_sem=recv_sems.at[3 * w + k],
                                                 device_id=(px, py, c), device_id_type=MESH)
                d.start()
                copies.append(d)
        for d in copies:
            d.wait_recv()
        for d in copies:
            d.wait_send()
        for lc in local:
            lc.wait()

    one = [jax.ShapeDtypeStruct(s.shape[1:], s.dtype) for s in ss]
    got = [o for o in one for _ in range(3)]
    outs = pl.pallas_call(
        body, name=name, out_shape=got + one, in_specs=[ANY] * n, out_specs=[ANY] * (4 * n),
        scratch_shapes=[pltpu.SemaphoreType.DMA((3 * n,)), pltpu.SemaphoreType.DMA((3 * n,)), pltpu.SemaphoreType.DMA((n,))],
    )(*ss)
    return [outs[3 * w:3 * w + 3] for w in range(n)], outs[3 * n:]


def _share_halves(ts, *, name):
    n = len(ts)

    def body(*refs):
        t_refs, o_refs = refs[:n], refs[n:2 * n]
        send_sems, recv_sems, local_sems = refs[2 * n:]
        x, y, c = _place()
        copies = []
        for w in range(n):
            lc = pltpu.make_async_copy(t_refs[w], o_refs[w].at[c], local_sems.at[w])
            d = pltpu.make_async_remote_copy(src_ref=t_refs[w], dst_ref=o_refs[w].at[c], send_sem=send_sems.at[w],
                                             recv_sem=recv_sems.at[w], device_id=(x, y, 1 - c), device_id_type=MESH)
            lc.start()
            d.start()
            copies.append((lc, d))
        for w, (lc, d) in enumerate(copies):
            pltpu.make_async_remote_copy(src_ref=t_refs[w], dst_ref=o_refs[w].at[1 - c], send_sem=send_sems.at[w],
                                         recv_sem=recv_sems.at[w], device_id=(x, y, 1 - c), device_id_type=MESH).wait_recv()
            d.wait_send()
            lc.wait()

    return pl.pallas_call(
        body, name=name, out_shape=[jax.ShapeDtypeStruct((2,) + t.shape, t.dtype) for t in ts],
        in_specs=[ANY] * n, out_specs=[ANY] * n,
        scratch_shapes=[pltpu.SemaphoreType.DMA((n,)), pltpu.SemaphoreType.DMA((n,)), pltpu.SemaphoreType.DMA((n,))],
    )(*ts)


def _pack(arrs):
    parts = []
    for v in arrs:
        f = v.reshape(-1)
        f = jnp.pad(f, (0, (-f.shape[0]) % 1024))
        parts.append(f.reshape(-1, 128))
    return jnp.concatenate(parts, axis=0)


def _unpack(packed, shapes):
    out, r0 = [], 0
    for s in shapes:
        size = 1
        for d in s:
            size *= d
        nr = (size + 1023) // 1024 * 8
        out.append(packed[r0:r0 + nr].reshape(-1)[:size].reshape(s))
        r0 += nr
    return out


SMALL_GRADS = [("loss", (1, 128)), ("pre_mix_norm", (1, 2048)), ("ssd_conv", (8, 4096)), ("ssd_par", (8, 128)),
               ("ssd_norm", (1, 2048)), ("lru_conv", (8, 2048)), ("lru_w_a", (16, 128, 128)), ("lru_b_a", (1, 2048)),
               ("lru_w_x", (16, 128, 128)), ("lru_b_x", (1, 2048)), ("lru_lambda", (1, 2048)), ("lru_norm", (1, 2048)),
               ("post_mix_norm", (1, 2048)), ("pre_mlp_norm", (1, 2048)), ("post_mlp_norm", (1, 2048))]

WEIGHTS = ['pre_mix_norm', 'w_in', 'ssd_conv_w', 'ssd_conv_b', 'ssd_dt_bias', 'ssd_a_log', 'ssd_d', 'ssd_norm', 'lru_conv_w',
           'lru_conv_b', 'lru_w_a', 'lru_b_a', 'lru_w_x', 'lru_b_x', 'lru_lambda', 'lru_norm', 'w_out', 'post_mix_norm',
           'pre_mlp_norm', 'w_mlp_in', 'w_mlp_out', 'post_mlp_norm']
LARGE = ['w_in', 'w_out', 'w_mlp_in', 'w_mlp_out']

D_SSD, D_XBC, DT_W = 2048, 4096, 32


def kernel(x, pre_mix_norm, w_in, ssd_conv_w, ssd_conv_b, ssd_dt_bias, ssd_a_log, ssd_d, ssd_norm, lru_conv_w, lru_conv_b, lru_w_a, lru_b_a, lru_w_x, lru_b_x, lru_lambda, lru_norm, w_out, post_mix_norm, pre_mlp_norm, w_mlp_in, w_mlp_out, post_mlp_norm, loss_target, m_pre_mix_norm, m_w_in, m_ssd_conv_w, m_ssd_conv_b, m_ssd_dt_bias, m_ssd_a_log, m_ssd_d, m_ssd_norm, m_lru_conv_w, m_lru_conv_b, m_lru_w_a, m_lru_b_a, m_lru_w_x, m_lru_b_x, m_lru_lambda, m_lru_norm, m_w_out, m_post_mix_norm, m_pre_mlp_norm, m_w_mlp_in, m_w_mlp_out, m_post_mlp_norm, v_pre_mix_norm, v_w_in, v_ssd_conv_w, v_ssd_conv_b, v_ssd_dt_bias, v_ssd_a_log, v_ssd_d, v_ssd_norm, v_lru_conv_w, v_lru_conv_b, v_lru_w_a, v_lru_b_a, v_lru_w_x, v_lru_b_x, v_lru_lambda, v_lru_norm, v_w_out, v_post_mix_norm, v_pre_mlp_norm, v_w_mlp_in, v_w_mlp_out, v_post_mlp_norm):
    a = dict(locals())
    j = 2 * lax.axis_index("x") + lax.axis_index("y")
    D = x.shape[-1]
    c_dt = D_SSD + D_XBC

    shards = []
    for name in LARGE:
        w = a[name][0]
        shards.append(_rowwise(lambda t: (t,), [w], [], [(w.shape[1], BF16)], [], name="cast_" + name, tr=256, sub=32)[0])
    g_in, g_out, g_mi, g_mo = _gather_weights(shards, name="gather_weights")
    w_full = jnp.transpose(g_in, (1, 0, 2)).reshape(D, -1)
    w_main = jnp.concatenate([w_full[:, :c_dt], w_full[:, c_dt + DT_W:]], axis=1)
    w_dt = jnp.pad(w_full[:, c_dt:c_dt + DT_W], ((0, 0), (0, 128 - DT_W)))
    taps = jnp.concatenate([ssd_conv_w[0].reshape(-1, 128), lru_conv_w[0].reshape(-1, 128)], axis=0)
    taps = _allgather8(taps, name="gather_taps").reshape(8, taps.shape[0], 128)[0::2]
    n_ssd = ssd_conv_w.shape[1] * ssd_conv_w.shape[2] // 128
    ssd_taps = taps[:, :n_ssd].reshape(4, CONV_W, -1).transpose(1, 0, 2).reshape(CONV_W, -1)
    lru_taps = taps[:, n_ssd:].reshape(4, CONV_W, -1).transpose(1, 0, 2).reshape(CONV_W, -1)

    def row128(v):
        return jnp.pad(v, ((0, 0), (0, 128 - v.shape[1])))

    p = dict(pre_mix_norm=pre_mix_norm, w_main=w_main, w_dt=w_dt, ssd_conv_w=ssd_taps, ssd_conv_b=ssd_conv_b,
             dtb=row128(ssd_dt_bias), alog=row128(ssd_a_log), drow=row128(ssd_d), ssd_norm=ssd_norm,
             lru_conv_w=lru_taps, lru_conv_b=lru_conv_b, lru_w_a=lru_w_a[0], lru_b_a=lru_b_a.reshape(1, -1),
             lru_w_x=lru_w_x[0], lru_b_x=lru_b_x.reshape(1, -1), lru_lambda=lru_lambda, lru_norm=lru_norm,
             w_out=g_out.reshape(-1, D), post_mix_norm=post_mix_norm, pre_mlp_norm=pre_mlp_norm, w_mi=g_mi,
             w_mo=g_mo.reshape(-1, D), post_mlp_norm=post_mlp_norm)
    grad_x, large, small = _local_step(x[0], loss_target[0], p)

    packed = _pack([small[n] for n, _ in SMALL_GRADS])
    total = _sum8(_allgather8(packed, name="gather_small").reshape(8, packed.shape[0], 128), name="sum_small")
    tot = dict(zip([n for n, _ in SMALL_GRADS], _unpack(total, [s for _, s in SMALL_GRADS])))
    loss = tot["loss"][0, 0]
    n_sc, n_lc = ssd_conv_w.shape[2], lru_conv_w.shape[2]
    grads = dict(
        pre_mix_norm=tot["pre_mix_norm"],
        ssd_conv_w=lax.dynamic_slice(tot["ssd_conv"][:CONV_W], (0, j * n_sc), (CONV_W, n_sc))[None],
        ssd_conv_b=tot["ssd_conv"][CONV_W:CONV_W + 1],
        ssd_dt_bias=tot["ssd_par"][0:1, :DT_W], ssd_a_log=tot["ssd_par"][1:2, :DT_W], ssd_d=tot["ssd_par"][2:3, :DT_W],
        ssd_norm=tot["ssd_norm"],
        lru_conv_w=lax.dynamic_slice(tot["lru_conv"][:CONV_W], (0, j * n_lc), (CONV_W, n_lc))[None],
        lru_conv_b=tot["lru_conv"][CONV_W:CONV_W + 1],
        lru_w_a=tot["lru_w_a"][None], lru_b_a=tot["lru_b_a"].reshape(lru_b_a.shape),
        lru_w_x=tot["lru_w_x"][None], lru_b_x=tot["lru_b_x"].reshape(lru_b_x.shape),
        lru_lambda=tot["lru_lambda"], lru_norm=tot["lru_norm"], post_mix_norm=tot["post_mix_norm"],
        pre_mlp_norm=tot["pre_mlp_norm"], post_mlp_norm=tot["post_mlp_norm"])

    dw_in = jnp.concatenate([large["w_main"][:, :c_dt], large["w_dt"][:, :DT_W], large["w_main"][:, c_dt:]], axis=1)
    gs = [dw_in.reshape(D, 4, -1).transpose(1, 0, 2), large["w_out"].reshape(4, -1, D), large["w_mi"],
          large["w_mo"].reshape(4, -1, D)]
    theirs, mine = _rs_sibling(gs, name="reduce_sibling")
    ss = []
    for name, t, o in zip(LARGE, theirs, mine):
        cc = t.shape[2]
        (s,) = _rowwise(lambda u, v: (u.astype(F32) + v.astype(F32),), [o.reshape(-1, cc), t.reshape(-1, cc)], [],
                        [(cc, BF16)], [], name="add_sibling_" + name, tr=256, sub=32)
        ss.append(s.reshape(t.shape))
    got, own = _rs_chips(ss, name="reduce_chips")
    totals = []
    for name, g3, o in zip(LARGE, got, own):
        cc = o.shape[1]
        totals.append(_rowwise(lambda s, r0, r1, r2: (((s.astype(F32) + r0.astype(F32)) + r1.astype(F32)) + r2.astype(F32),),
                               [o, g3[0], g3[1], g3[2]], [], [(cc, F32)], [], name="add_chips_" + name, tr=256, sub=32)[0])
    for name, g in zip(LARGE, _share_halves(totals, name="share_halves")):
        grads[name] = g.reshape(a[name].shape)

    delta, new_m, new_v = {}, {}, {}
    for name in LARGE:
        w = a[name][0]
        cc = w.shape[1]
        outs = _rowwise(_adamw, [w, grads[name][0], a["m_" + name][0], a["v_" + name][0]], [], [(cc, F32)] * 3, [],
                        name="adamw_" + name, tr=128, sub=8)
        delta[name], new_m[name], new_v[name] = [o[None] for o in outs]
    small_w = [n for n in WEIGHTS if n not in LARGE]
    packs = [_pack([d[n] for n in small_w]) for d in (a, grads, {n: a["m_" + n] for n in small_w}, {n: a["v_" + n] for n in small_w})]
    outs = _rowwise(_adamw, packs, [], [(128, F32)] * 3, [], name="adamw_small", tr=packs[0].shape[0], sub=8)
    for d, o in zip((delta, new_m, new_v), outs):
        d.update(zip(small_w, _unpack(o, [a[n].shape for n in small_w])))

    return (loss, grad_x[None], *[grads[n] for n in WEIGHTS], *[delta[n] for n in WEIGHTS],
            *[new_m[n] for n in WEIGHTS], *[new_v[n] for n in WEIGHTS])
```

```python
import functools

import jax
import jax.numpy as jnp
from jax import lax
from jax.experimental import pallas as pl
from jax.experimental.pallas import tpu as pltpu

F32 = jnp.float32
BF16 = jnp.bfloat16
MESH = pl.DeviceIdType.MESH

EPS = 1e-6
LRU_C = 8.0
ADAM_LR = 0.001
ADAM_B1 = 0.9
ADAM_B2 = 0.999
ADAM_EPS = 1e-08
ADAM_WD = 0.01
ADAM_STEP = 10

N_GROUPS = 8
HEADS_PER_GROUP = 4
HEAD_DIM = 64
GROUP_W = HEADS_PER_GROUP * HEAD_DIM
STATE = 128
LRU_HEADS = 16
LRU_BLOCK = 128
CONV_W = 4
SSD_CHUNK = 128
HALO = 8

VMEM_LIMIT = 48 * 1024 * 1024


def _params(sem=None):
    return pltpu.CompilerParams(dimension_semantics=sem, vmem_limit_bytes=VMEM_LIMIT)


@jax.custom_jvp
def _log1p(x):
    u = 1.0 + x
    d = u - 1.0
    return jnp.where(d == 0.0, x, jnp.log(u) * (x / jnp.where(d == 0.0, 1.0, d)))


@_log1p.defjvp
def _log1p_jvp(primals, tangents):
    (x,), (t,) = primals, tangents
    return _log1p(x), t / (1.0 + x)


@jax.custom_jvp
def _expm1(x):
    u = jnp.exp(x)
    lu = jnp.log(u)
    safe = jnp.where(lu == 0.0, 1.0, lu)
    y = (u - 1.0) * (x / safe)
    y = jnp.where(lu == 0.0, x, y)
    return jnp.where(u == 0.0, -1.0, y)


@_expm1.defjvp
def _expm1_jvp(primals, tangents):
    (x,), (t,) = primals, tangents
    return _expm1(x), t * jnp.exp(x)


def _softplus(x):
    return jnp.maximum(x, 0.0) + _log1p(jnp.exp(-jnp.abs(x)))


def _sigmoid(x):
    return 1.0 / (1.0 + jnp.exp(-x))


def _silu(x):
    return x * _sigmoid(x)


def _gelu(x):
    c = 0.7978845608028654
    return 0.5 * x * (1.0 + jnp.tanh(c * (x + 0.044715 * (x * x * x))))


def _rms(x, g):
    return x * lax.rsqrt(jnp.mean(x * x, axis=-1, keepdims=True) + EPS) * g


def _dot(a, b, dims):
    return lax.dot_general(a.astype(BF16), b.astype(BF16), (dims, ((), ())), preferred_element_type=F32)


_NN = ((1,), (0,))
_NT = ((1,), (1,))
_TN = ((0,), (0,))


def _matmul(a, b, *, mode, m, n, k, tm, tn, tk, out_dtypes, name, a_spec=None, b_spec=None,
            out_specs=None, out_shapes=None, extras=(), epilogue=None):
    tm, tn, tk = min(tm, m), min(tn, n), min(tk, k)
    assert m % tm == 0 and n % tn == 0 and k % tk == 0, (name, m, n, k, tm, tn, tk)
    nk = k // tk
    dims = {"nn": _NN, "nt": _NT, "tn": _TN}[mode]
    if a_spec is None:
        a_spec = pl.BlockSpec((tk, tm), lambda i, j, kk: (kk, i)) if mode == "tn" else pl.BlockSpec((tm, tk), lambda i, j, kk: (i, kk))
    if b_spec is None:
        b_spec = pl.BlockSpec((tn, tk), lambda i, j, kk: (j, kk)) if mode == "nt" else pl.BlockSpec((tk, tn), lambda i, j, kk: (kk, j))
    tile = pl.BlockSpec((tm, tn), lambda i, j, kk: (i, j))
    if out_specs is None:
        out_specs = [tile for _ in out_dtypes]
    if out_shapes is None:
        out_shapes = [jax.ShapeDtypeStruct((m, n), d) for d in out_dtypes]
    n_ex, n_out = len(extras), len(out_dtypes)

    def body(*refs):
        a_ref, b_ref = refs[0], refs[1]
        ex_refs = refs[2:2 + n_ex]
        o_refs = refs[2 + n_ex:2 + n_ex + n_out]
        acc = refs[-1]
        kk = pl.program_id(2)

        @pl.when(kk == 0)
        def _():
            acc[...] = jnp.zeros_like(acc)

        acc[...] += _dot(a_ref[...], b_ref[...], dims)

        @pl.when(kk == nk - 1)
        def _():
            r = acc[...]
            outs = epilogue(r, *[e[...] for e in ex_refs]) if epilogue is not None else (r,)
            for o_ref, o in zip(o_refs, outs):
                o_ref[...] = o.astype(o_ref.dtype)

    outs = pl.pallas_call(
        body, name=name, grid=(m // tm, n // tn, nk),
        in_specs=[a_spec, b_spec] + [tile for _ in extras],
        out_specs=out_specs, out_shape=out_shapes,
        scratch_shapes=[pltpu.VMEM((tm, tn), F32)],
        compiler_params=_params(("parallel", "parallel", "arbitrary")),
    )(a, b, *extras)
    return outs


def _rowwise(fn, rows, bcast, out_rows, out_acc, *, name, tr, sub):
    rows = [r if isinstance(r, tuple) else (r, 0, r.shape[1]) for r in rows]
    row_specs = []
    for arr, c0, w in rows:
        assert c0 % w == 0, (name, c0, w)
        row_specs.append((w, c0 // w))
    rows = [r[0] for r in rows]
    L = rows[0].shape[0]
    tr = min(tr, L)
    sub = min(sub, tr)
    assert L % tr == 0 and tr % sub == 0, (name, L, tr, sub)
    n_r, n_b, n_or, n_oa = len(rows), len(bcast), len(out_rows), len(out_acc)

    def body(*refs):
        r_refs = refs[:n_r]
        b_refs = refs[n_r:n_r + n_b]
        or_refs = refs[n_r + n_b:n_r + n_b + n_or]
        oa_refs = refs[n_r + n_b + n_or:]
        i = pl.program_id(0)

        @pl.when(i == 0)
        def _():
            for o in oa_refs:
                o[...] = jnp.zeros_like(o)

        bvals = [b[...] for b in b_refs]

        def step(s, carry):
            r0 = pl.multiple_of(s * sub, sub)
            tiles = [r[pl.ds(r0, sub), :] for r in r_refs]
            outs = fn(*tiles, *bvals)
            for o_ref, o in zip(or_refs, outs[:n_or]):
                o_ref[pl.ds(r0, sub), :] = o.astype(o_ref.dtype)
            for o_ref, o in zip(oa_refs, outs[n_or:]):
                o_ref[...] += o
            return carry

        if tr == sub:
            step(0, 0)
        else:
            lax.fori_loop(0, tr // sub, step, 0)

    def whole(shape):
        nd = len(shape)
        return pl.BlockSpec(shape, lambda i, _n=nd: (0,) * _n)

    outs = pl.pallas_call(
        body, name=name, grid=(L // tr,),
        in_specs=[pl.BlockSpec((tr, w), lambda i, _c=cb: (i, _c)) for w, cb in row_specs] + [whole(b.shape) for b in bcast],
        out_specs=[pl.BlockSpec((tr, c), lambda i: (i, 0)) for c, _ in out_rows] + [whole(s) for s in out_acc],
        out_shape=[jax.ShapeDtypeStruct((L, c), d) for c, d in out_rows] + [jax.ShapeDtypeStruct(s, F32) for s in out_acc],
        compiler_params=_params(("arbitrary",)),
    )(*rows, *bcast)
    return outs


def _colsum(x):
    return jnp.sum(x, axis=0, keepdims=True)


def _grouped(fn, rows, params, out_rows, out_acc, *, gw, name, tr):
    rows = [r if isinstance(r, tuple) else (r, 0) for r in rows]
    L = rows[0][0].shape[0]
    tr = min(tr, L)
    assert L % tr == 0
    G = None
    for p in params:
        G = p.shape[0] if p.ndim == 3 else p.shape[1] // gw
    cw = G * gw
    n_r, n_p, n_or, n_oa = len(rows), len(params), len(out_rows), len(out_acc)

    def pick(ref, g):
        return ref[g] if len(ref.shape) == 3 else ref[:, g * gw:(g + 1) * gw]

    def body(*refs):
        r_refs = refs[:n_r]
        p_refs = refs[n_r:n_r + n_p]
        or_refs = refs[n_r + n_p:n_r + n_p + n_or]
        oa_refs = refs[n_r + n_p + n_or:]

        @pl.when(pl.program_id(0) == 0)
        def _():
            for o in oa_refs:
                o[...] = jnp.zeros_like(o)

        for g in range(G):
            outs = fn(*[pick(r, g) for r in r_refs], *[pick(p, g) for p in p_refs])
            for o_ref, o in zip(or_refs, outs[:n_or]):
                o_ref[:, g * gw:(g + 1) * gw] = o.astype(o_ref.dtype)
            for o_ref, o in zip(oa_refs, outs[n_or:]):
                if len(o_ref.shape) == 3:
                    o_ref[g] += o
                else:
                    o_ref[:, g * gw:(g + 1) * gw] += o

    def whole(shape):
        nd = len(shape)
        return pl.BlockSpec(shape, lambda i, _n=nd: (0,) * _n)

    for _, c0 in rows:
        assert c0 % cw == 0
    return pl.pallas_call(
        body, name=name, grid=(L // tr,),
        in_specs=[pl.BlockSpec((tr, cw), lambda i, _c=c0 // cw: (i, _c)) for _, c0 in rows] + [whole(p.shape) for p in params],
        out_specs=[pl.BlockSpec((tr, cw), lambda i: (i, 0)) for _ in out_rows] + [whole(s) for s in out_acc],
        out_shape=[jax.ShapeDtypeStruct((L, cw), d) for d in out_rows] + [jax.ShapeDtypeStruct(s, F32) for s in out_acc],
        compiler_params=_params(("arbitrary",)),
    )(*[r[0] for r in rows], *params)


def _dsilu(p):
    s = _sigmoid(p)
    return s + p * s * (1.0 - s)


def _conv_fwd(x, c0, C, w, b, *, silu, name, tr=256, cb=512):
    L = x.shape[0]
    tr = min(tr, L)
    nb, hb = L // tr, tr // HALO
    assert L % tr == 0 and C % cb == 0 and c0 % cb == 0
    n_out = 2 if silu else 1

    def body(x_ref, h_ref, w_ref, b_ref, *rest):
        o_refs, xp = rest[:n_out], rest[n_out]
        xp[0:HALO, :] = h_ref[...]

        @pl.when(pl.program_id(0) == 0)
        def _():
            xp[0:HALO, :] = jnp.zeros((HALO, cb), F32)

        xp[HALO:, :] = x_ref[...]
        wv = w_ref[...]
        y = b_ref[...]
        for k in range(CONV_W):
            y = y + wv[k:k + 1, :] * xp[HALO - 3 + k:HALO - 3 + k + tr, :]
        o_refs[0][...] = y
        if silu:
            o_refs[1][...] = _silu(y)

    return pl.pallas_call(
        body, name=name, grid=(nb, C // cb),
        in_specs=[
            pl.BlockSpec((tr, cb), lambda i, j: (i, c0 // cb + j)),
            pl.BlockSpec((HALO, cb), lambda i, j: (jnp.maximum(i * hb - 1, 0), c0 // cb + j)),
            pl.BlockSpec((CONV_W, cb), lambda i, j: (0, j)),
            pl.BlockSpec((1, cb), lambda i, j: (0, j)),
        ],
        out_specs=[pl.BlockSpec((tr, cb), lambda i, j: (i, j)) for _ in range(n_out)],
        out_shape=[jax.ShapeDtypeStruct((L, C), F32) for _ in range(n_out)],
        scratch_shapes=[pltpu.VMEM((tr + HALO, cb), F32)],
        compiler_params=_params(("parallel", "parallel")),
    )(x, x, w, b)


def _conv_bwd(dact, dc0, pre, x, xc0, C, w, *, silu, name, tr=256, cb=512):
    L = x.shape[0]
    tr = min(tr, L)
    nb, hb = L // tr, tr // HALO
    last_h = L // HALO - 1

    def body(*refs):
        if silu:
            d_ref, dh_ref, p_ref, ph_ref, x_ref, xh_ref, w_ref, dx_ref, dwb_ref, dp, xp = refs
        else:
            d_ref, dh_ref, x_ref, xh_ref, w_ref, dx_ref, dwb_ref, dp, xp = refs
        i = pl.program_id(1)
        dm, dh = d_ref[...], dh_ref[...]
        if silu:
            dm = dm * _dsilu(p_ref[...])
            dh = dh * _dsilu(ph_ref[...])
        dp[0:tr, :] = dm
        dp[tr:, :] = dh

        @pl.when(i == nb - 1)
        def _():
            dp[tr:, :] = jnp.zeros((HALO, cb), F32)

        xp[0:HALO, :] = xh_ref[...]

        @pl.when(i == 0)
        def _():
            xp[0:HALO, :] = jnp.zeros((HALO, cb), F32)
            dwb_ref[...] = jnp.zeros_like(dwb_ref)

        xp[HALO:, :] = x_ref[...]
        wv = w_ref[...]
        dx = jnp.zeros((tr, cb), F32)
        for k in range(CONV_W):
            dx = dx + wv[k:k + 1, :] * dp[3 - k:3 - k + tr, :]
            dwb_ref[k:k + 1, :] += _colsum(dm * xp[HALO - 3 + k:HALO - 3 + k + tr, :])
        dx_ref[...] = dx
        dwb_ref[CONV_W:CONV_W + 1, :] += _colsum(dm)

    def main(c):
        return pl.BlockSpec((tr, cb), lambda j, i: (i, c // cb + j))

    def nxt(c):
        return pl.BlockSpec((HALO, cb), lambda j, i: (jnp.minimum((i + 1) * hb, last_h), c // cb + j))

    in_specs = [main(dc0), nxt(dc0)]
    args = [dact, dact]
    if silu:
        in_specs += [main(0), nxt(0)]
        args += [pre, pre]
    in_specs += [main(xc0), pl.BlockSpec((HALO, cb), lambda j, i: (jnp.maximum(i * hb - 1, 0), xc0 // cb + j)),
                 pl.BlockSpec((CONV_W, cb), lambda j, i: (0, j))]
    args += [x, x, w]
    return pl.pallas_call(
        body, name=name, grid=(C // cb, nb),
        in_specs=in_specs,
        out_specs=[pl.BlockSpec((tr, cb), lambda j, i: (i, j)), pl.BlockSpec((8, cb), lambda j, i: (0, j))],
        out_shape=[jax.ShapeDtypeStruct((L, C), F32), jax.ShapeDtypeStruct((8, C), F32)],
        scratch_shapes=[pltpu.VMEM((tr + HALO, cb), F32), pltpu.VMEM((tr + HALO, cb), F32)],
        compiler_params=_params(("parallel", "arbitrary")),
    )(*args)


def _ssd_f1(dtraw, dtb, alog):
    q = dtraw.shape[0]
    dt = _softplus(dtraw + dtb)
    adt = dt * (-jnp.exp(alog))
    tril = (lax.broadcasted_iota(jnp.int32, (q, q), 0) >= lax.broadcasted_iota(jnp.int32, (q, q), 1)).astype(F32)
    acs = lax.dot_general(tril, adt, (_NN, ((), ())), precision=lax.Precision.HIGHEST, preferred_element_type=F32)
    return dt, acs


def _ssd_group(g, x, bm, cm, dt, acs, drow, hp):
    q = x.shape[0]
    lane = lax.broadcasted_iota(jnp.int32, (1, 128), 1)
    sub = lax.broadcasted_iota(jnp.int32, (128, 1), 0)
    head_of = lax.broadcasted_iota(jnp.int32, (1, GROUP_W), 1) // HEAD_DIM
    is_last = (lax.broadcasted_iota(jnp.int32, (q, 1), 0) == q - 1).astype(F32)
    causal = lax.broadcasted_iota(jnp.int32, (q, q), 0) >= lax.broadcasted_iota(jnp.int32, (q, q), 1)
    acs_end = jnp.sum(acs * is_last, axis=0, keepdims=True)
    acs_t = acs.T
    dt_exp = jnp.zeros((q, GROUP_W), F32)
    acs_exp = jnp.zeros((q, GROUP_W), F32)
    end_exp = jnp.zeros((1, GROUP_W), F32)
    d_exp = jnp.zeros((1, GROUP_W), F32)
    heads = []
    for k in range(HEADS_PER_GROUP):
        h = HEADS_PER_GROUP * g + k
        oh = (lane == h).astype(F32)
        mk = (head_of == k).astype(F32)
        acs_col = jnp.sum(acs * oh, axis=1, keepdims=True)
        acs_row = jnp.sum(acs_t * (sub == h).astype(F32), axis=0, keepdims=True)
        dt_exp = dt_exp + jnp.sum(dt * oh, axis=1, keepdims=True) * mk
        acs_exp = acs_exp + acs_col * mk
        end_exp = end_exp + jnp.sum(acs_end * oh, axis=1, keepdims=True) * mk
        d_exp = d_exp + jnp.sum(drow * oh, axis=1, keepdims=True) * mk
        heads.append((acs_col, acs_row, mk))
    xdt = x * dt_exp
    states = _dot(bm, xdt * jnp.exp(end_exp - acs_exp), _TN)
    y = _dot(cm, hp, _NN) * jnp.exp(acs_exp) + x * d_exp
    scores = _dot(cm, bm, _NT)
    for acs_col, acs_row, mk in heads:
        dec = jnp.exp(jnp.where(causal, acs_col - acs_row, -jnp.inf))
        y = y + _dot(scores * dec, xdt * mk, _NN)
    return y, hp * jnp.exp(end_exp) + states


def _ssd_fwd(act, dtraw, dtb, alog, drow, *, name):
    L = act.shape[0]
    q = min(SSD_CHUNK, L)
    nc = L // q
    d_ssd = N_GROUPS * GROUP_W

    def body(act_ref, dt_ref, dtb_ref, alog_ref, drow_ref, y_ref, hst_ref, h):
        @pl.when(pl.program_id(0) == 0)
        def _():
            h[...] = jnp.zeros_like(h)

        dt, acs = _ssd_f1(dt_ref[...], dtb_ref[...], alog_ref[...])
        drow = drow_ref[...]
        for g in range(N_GROUPS):
            hp = h[g]
            hst_ref[0, g] = hp
            y, hn = _ssd_group(g, act_ref[:, g * GROUP_W:(g + 1) * GROUP_W],
                               act_ref[:, d_ssd + g * STATE:d_ssd + (g + 1) * STATE],
                               act_ref[:, d_ssd + (N_GROUPS + g) * STATE:d_ssd + (N_GROUPS + g + 1) * STATE],
                               dt, acs, drow, hp)
            y_ref[:, g * GROUP_W:(g + 1) * GROUP_W] = y
            h[g] = hn

    row = pl.BlockSpec((1, 128), lambda i: (0, 0))
    return pl.pallas_call(
        body, name=name, grid=(nc,),
        in_specs=[pl.BlockSpec((q, act.shape[1]), lambda i: (i, 0)), pl.BlockSpec((q, 128), lambda i: (i, 0)), row, row, row],
        out_specs=[pl.BlockSpec((q, d_ssd), lambda i: (i, 0)),
                   pl.BlockSpec((1, N_GROUPS, STATE, GROUP_W), lambda i: (i, 0, 0, 0))],
        out_shape=[jax.ShapeDtypeStruct((L, d_ssd), F32), jax.ShapeDtypeStruct((nc, N_GROUPS, STATE, GROUP_W), F32)],
        scratch_shapes=[pltpu.VMEM((N_GROUPS, STATE, GROUP_W), F32)],
        compiler_params=_params(("arbitrary",)),
    )(act, dtraw, dtb, alog, drow)


def _ssd_bwd(act, dtraw, dy, hst, dtb, alog, drow, *, name):
    L = act.shape[0]
    q = min(SSD_CHUNK, L)
    nc = L // q
    d_ssd = N_GROUPS * GROUP_W

    def body(act_ref, dt_ref, dy_ref, hst_ref, dtb_ref, alog_ref, drow_ref, dact_ref, ddt_ref, dpar_ref, dh):
        @pl.when(pl.program_id(0) == 0)
        def _():
            dh[...] = jnp.zeros_like(dh)
            dpar_ref[...] = jnp.zeros_like(dpar_ref)

        (dt, acs), f1_vjp = jax.vjp(_ssd_f1, dt_ref[...], dtb_ref[...], alog_ref[...])
        drow = drow_ref[...]
        ddt = jnp.zeros_like(dt)
        dacs = jnp.zeros_like(acs)
        ddrow = jnp.zeros_like(drow)
        for g in range(N_GROUPS):
            xs = slice(g * GROUP_W, (g + 1) * GROUP_W)
            bs = slice(d_ssd + g * STATE, d_ssd + (g + 1) * STATE)
            cs = slice(d_ssd + (N_GROUPS + g) * STATE, d_ssd + (N_GROUPS + g + 1) * STATE)
            _, f2_vjp = jax.vjp(functools.partial(_ssd_group, g), act_ref[:, xs], act_ref[:, bs], act_ref[:, cs],
                                dt, acs, drow, hst_ref[0, g])
            dx, dbm, dcm, ddt_g, dacs_g, ddrow_g, dhp = f2_vjp((dy_ref[:, xs], dh[g]))
            dact_ref[:, xs] = dx
            dact_ref[:, bs] = dbm
            dact_ref[:, cs] = dcm
            dh[g] = dhp
            ddt, dacs, ddrow = ddt + ddt_g, dacs + dacs_g, ddrow + ddrow_g
        ddtraw, ddtb, dalog = f1_vjp((ddt, dacs))
        ddt_ref[...] = ddtraw
        dpar_ref[0:1, :] += ddtb
        dpar_ref[1:2, :] += dalog
        dpar_ref[2:3, :] += ddrow

    row = pl.BlockSpec((1, 128), lambda i: (0, 0))
    rev = lambda i: (nc - 1 - i, 0)
    return pl.pallas_call(
        body, name=name, grid=(nc,),
        in_specs=[pl.BlockSpec((q, act.shape[1]), rev), pl.BlockSpec((q, 128), rev), pl.BlockSpec((q, d_ssd), rev),
                  pl.BlockSpec((1, N_GROUPS, STATE, GROUP_W), lambda i: (nc - 1 - i, 0, 0, 0)), row, row, row],
        out_specs=[pl.BlockSpec((q, act.shape[1]), rev), pl.BlockSpec((q, 128), rev), pl.BlockSpec((8, 128), lambda i: (0, 0))],
        out_shape=[jax.ShapeDtypeStruct(act.shape, F32), jax.ShapeDtypeStruct((L, 128), F32), jax.ShapeDtypeStruct((8, 128), F32)],
        scratch_shapes=[pltpu.VMEM((N_GROUPS, STATE, GROUP_W), F32)],
        compiler_params=_params(("arbitrary",)),
    )(act, dtraw, dy, hst, dtb, alog, drow)


def _gate_head(xl, wa, ba, wx, bx, lam):
    r = _sigmoid(_dot(xl, wa, _NN) + ba)
    i = _sigmoid(_dot(xl, wx, _NN) + bx)
    log_a = -LRU_C * r * _softplus(-lam)
    return jnp.exp(log_a), jnp.sqrt(-_expm1(2.0 * log_a)) * (i * xl)


def _gate_head_bwd(xl, da, du, wa, ba, wx, bx, lam):
    _, vjp = jax.vjp(_gate_head, xl, wa, ba, wx, bx, lam)
    dxl, dwa, dba, dwx, dbx, dlam = vjp((da, du))
    return dxl, dwa, dba, dwx, dbx, dlam


def _scan_tile(a, b, rows, reverse):
    for d in (1, 2, 4):
        if reverse:
            keep = rows < 8 - d
            a_sh, b_sh = pltpu.roll(a, 8 - d, 0), pltpu.roll(b, 8 - d, 0)
        else:
            keep = rows >= d
            a_sh, b_sh = pltpu.roll(a, d, 0), pltpu.roll(b, d, 0)
        b = b + a * jnp.where(keep, b_sh, 0.0)
        a = a * jnp.where(keep, a_sh, 1.0)
    return a, b


def _lru_scan_fwd(a, u, *, name, tr=512, cb=1024):
    L, C = a.shape
    tr, cb = min(tr, L), min(cb, C)

    def body(a_ref, u_ref, h_ref, hp_ref, carry):
        @pl.when(pl.program_id(1) == 0)
        def _():
            carry[...] = jnp.zeros_like(carry)

        rows = lax.broadcasted_iota(jnp.int32, (8, cb), 0)

        def tile(t, hc):
            r0 = pl.multiple_of(t * 8, 8)
            pa, hb = _scan_tile(a_ref[pl.ds(r0, 8), :], u_ref[pl.ds(r0, 8), :], rows, False)
            h = hb + pa * hc
            h_ref[pl.ds(r0, 8), :] = h
            hp_ref[pl.ds(r0, 8), :] = jnp.where(rows >= 1, pltpu.roll(h, 1, 0), hc)
            return h[7:8, :]

        carry[...] = lax.fori_loop(0, tr // 8, tile, carry[...])

    blk = pl.BlockSpec((tr, cb), lambda j, i: (i, j))
    return pl.pallas_call(
        body, name=name, grid=(C // cb, L // tr),
        in_specs=[blk, blk], out_specs=[blk, blk],
        out_shape=[jax.ShapeDtypeStruct((L, C), F32), jax.ShapeDtypeStruct((L, C), F32)],
        scratch_shapes=[pltpu.VMEM((1, cb), F32)],
        compiler_params=_params(("parallel", "arbitrary")),
    )(a, u)


def _lru_scan_bwd(a, hprev, dh, *, name, tr=512, cb=1024):
    L, C = a.shape
    tr, cb = min(tr, L), min(cb, C)
    nb = L // tr

    def body(a_ref, hp_ref, dh_ref, da_ref, du_ref, carry):
        @pl.when(pl.program_id(1) == 0)
        def _():
            carry[...] = jnp.zeros_like(carry)

        rows = lax.broadcasted_iota(jnp.int32, (8, cb), 0)

        def tile(t, gc):
            r0 = pl.multiple_of((tr // 8 - 1 - t) * 8, 8)
            av, dv = a_ref[pl.ds(r0, 8), :], dh_ref[pl.ds(r0, 8), :]
            pa, gb = _scan_tile(av, av * dv, rows, True)
            big = gb + pa * gc
            g = dv + jnp.where(rows < 7, pltpu.roll(big, 7, 0), gc)
            du_ref[pl.ds(r0, 8), :] = g
            da_ref[pl.ds(r0, 8), :] = g * hp_ref[pl.ds(r0, 8), :]
            return big[0:1, :]

        carry[...] = lax.fori_loop(0, tr // 8, tile, carry[...])

    blk = pl.BlockSpec((tr, cb), lambda j, i: (nb - 1 - i, j))
    return pl.pallas_call(
        body, name=name, grid=(C // cb, nb),
        in_specs=[blk, blk, blk], out_specs=[blk, blk],
        out_shape=[jax.ShapeDtypeStruct((L, C), F32), jax.ShapeDtypeStruct((L, C), F32)],
        scratch_shapes=[pltpu.VMEM((1, cb), F32)],
        compiler_params=_params(("parallel", "arbitrary")),
    )(a, hprev, dh)


def _ssd_gate(y, z, n):
    v = y * _silu(z)
    return v * lax.rsqrt(jnp.mean(v * v, axis=-1, keepdims=True) + EPS) * n


def _ssd_gate_bwd(y, z, dy, n):
    _, vjp = jax.vjp(_ssd_gate, y, z, n)
    return vjp(dy)


def _lru_out(hl, gate, n):
    return _rms(hl * _gelu(gate), n)


def _lru_out_bwd(hl, gate, dy, n):
    _, vjp = jax.vjp(_lru_out, hl, gate, n)
    return vjp(dy)


def _mid(x, mix, pm, pmlp):
    x1 = x + _rms(mix, pm)
    return x1, _rms(x1, pmlp)


def _mid_bwd(x, mix, dx1p, dh2, pm, pmlp):
    _, vjp = jax.vjp(_mid, x, mix, pm, pmlp)
    dx, dmix, dpm, dpmlp = vjp((dx1p, dh2))
    return dmix, dx, dpm, dpmlp


def _loss_bwd(hm2, x1, tgt, g):
    def lossf(hm2, x1, g):
        e = x1 + _rms(hm2, g) - tgt
        return 0.5 * jnp.sum(jnp.mean(e * e, axis=-1, keepdims=True), axis=0, keepdims=True)

    val, vjp = jax.vjp(lossf, hm2, x1, g)
    dhm2, dx1, dg = vjp(jnp.ones((1, 1), F32))
    return dhm2, dx1, dg, val * jnp.ones((1, 128), F32)


def _in_bwd(x, dh_a, dh_b, dx1, g):
    _, vjp = jax.vjp(_rms, x, g)
    dx, dg = vjp(dh_a + dh_b)
    return dx + dx1, dg


def _adamw(w, g, m, v):
    m = ADAM_B1 * m + (1.0 - ADAM_B1) * g
    v = ADAM_B2 * v + (1.0 - ADAM_B2) * (g * g)
    m_hat = m / (1.0 - ADAM_B1 ** ADAM_STEP)
    v_hat = v / (1.0 - ADAM_B2 ** ADAM_STEP)
    return -ADAM_LR * (m_hat / (jnp.sqrt(v_hat) + ADAM_EPS) + ADAM_WD * w), m, v


def _local_step(x, tgt, p):
    L, D = x.shape
    d_ssd, d_xbc, d_lru, d_mix, d_ff = 2048, 4096, 2048, 4096, 8192
    n_main = d_ssd + d_xbc + 2 * d_lru
    c_xbc, c_gate, c_xl = d_ssd, d_ssd + d_xbc, d_ssd + d_xbc + d_lru
    TR, SUB = 256, 16
    mm = dict(tm=1024, tn=1024, tk=1024)

    (h,) = _rowwise(lambda xt, g: (_rms(xt, g),), [x], [p["pre_mix_norm"]], [(D, BF16)], [], name="f_prenorm", tr=TR, sub=SUB)
    (proj,) = _matmul(h, p["w_main"], mode="nn", m=L, n=n_main, k=D, out_dtypes=[F32], name="f_inproj", **mm)
    (dtraw,) = _matmul(h, p["w_dt"], mode="nn", m=L, n=128, k=D, out_dtypes=[F32], name="f_dtproj", **mm)
    pre, act = _conv_fwd(proj, c_xbc, d_xbc, p["ssd_conv_w"], p["ssd_conv_b"], silu=True, name="f_ssdconv")
    (xl,) = _conv_fwd(proj, c_xl, d_lru, p["lru_conv_w"], p["lru_conv_b"], silu=False, name="f_lruconv")
    yraw, hst = _ssd_fwd(act, dtraw, p["dtb"], p["alog"], p["drow"], name="f_ssd")
    (yssd,) = _grouped(lambda y, z, n: (_ssd_gate(y, z, n),), [yraw, (proj, 0)], [p["ssd_norm"]], [BF16], [],
                       gw=GROUP_W, name="f_ssdgate", tr=TR)
    gate_p = [p["lru_w_a"], p["lru_b_a"], p["lru_w_x"], p["lru_b_x"], p["lru_lambda"]]
    a, u = _grouped(_gate_head, [xl], gate_p, [F32, F32], [], gw=LRU_BLOCK, name="f_lrugates", tr=TR)
    hl, hprev = _lru_scan_fwd(a, u, name="f_lruscan")
    (ylru,) = _rowwise(lambda ht, gt, n: (_lru_out(ht, gt, n),), [hl, (proj, c_gate, d_lru)], [p["lru_norm"]],
                       [(d_lru, BF16)], [], name="f_lruout", tr=TR, sub=SUB)
    ycat = jnp.concatenate([yssd, ylru], axis=1)
    (mix,) = _matmul(ycat, p["w_out"], mode="nn", m=L, n=D, k=d_mix, out_dtypes=[F32], name="f_outproj", **mm)
    x1, h2 = _rowwise(_mid, [x, mix], [p["post_mix_norm"], p["pre_mlp_norm"]], [(D, F32), (D, BF16)], [],
                      name="f_mid", tr=TR, sub=SUB)
    nb_mi = (d_ff // 4) // mm["tn"]
    hm, act2 = _matmul(h2, p["w_mi"], mode="nn", m=L, n=d_ff, k=D, out_dtypes=[F32, BF16], name="f_mlpin",
                       b_spec=pl.BlockSpec((None, mm["tk"], mm["tn"]), lambda i, j, kk: (j // nb_mi, kk, j % nb_mi)),
                       epilogue=lambda r: (r, jnp.square(jnp.maximum(r, 0.0))), **mm)
    (hm2,) = _matmul(act2, p["w_mo"], mode="nn", m=L, n=D, k=d_ff, out_dtypes=[F32], name="f_mlpout", **mm)

    dhm2, dx1p, d_post_mlp, loss = _rowwise(_loss_bwd, [hm2, x1, tgt], [p["post_mlp_norm"]], [(D, BF16), (D, F32)],
                                            [(1, D), (1, 128)], name="b_loss", tr=TR, sub=SUB)
    (dhm,) = _matmul(dhm2, p["w_mo"], mode="nt", m=L, n=d_ff, k=D, out_dtypes=[BF16], name="b_mlpout_dx", extras=[hm],
                     epilogue=lambda r, hmv: (r * (2.0 * jnp.maximum(hmv, 0.0)),), **mm)
    (dw_mo,) = _matmul(act2, dhm2, mode="tn", m=d_ff, n=D, k=L, out_dtypes=[BF16], name="b_mlpout_dw", **mm)
    kb_mi = (d_ff // 4) // mm["tk"]
    (dh2,) = _matmul(dhm, p["w_mi"], mode="nt", m=L, n=D, k=d_ff, out_dtypes=[F32], name="b_mlpin_dx",
                     b_spec=pl.BlockSpec((None, mm["tn"], mm["tk"]), lambda i, j, kk: (kk // kb_mi, j, kk % kb_mi)), **mm)
    (dw_mi,) = _matmul(h2, dhm, mode="tn", m=D, n=d_ff, k=L, out_dtypes=[BF16], name="b_mlpin_dw",
                       out_specs=[pl.BlockSpec((None, mm["tm"], mm["tn"]), lambda i, j, kk: (j // nb_mi, i, j % nb_mi))],
                       out_shapes=[jax.ShapeDtypeStruct((4, D, d_ff // 4), BF16)], **mm)
    dmix, dx1, d_post_mix, d_pre_mlp = _rowwise(_mid_bwd, [x, mix, dx1p, dh2], [p["post_mix_norm"], p["pre_mlp_norm"]],
                                                [(D, BF16), (D, F32)], [(1, D), (1, D)], name="b_mid", tr=TR, sub=SUB)
    (dycat,) = _matmul(dmix, p["w_out"], mode="nt", m=L, n=d_mix, k=D, out_dtypes=[F32], name="b_outproj_dx", **mm)
    (dw_out,) = _matmul(ycat, dmix, mode="tn", m=d_mix, n=D, k=L, out_dtypes=[BF16], name="b_outproj_dw", **mm)
    dhl, dgate, d_lru_norm = _rowwise(_lru_out_bwd, [hl, (proj, c_gate, d_lru), (dycat, d_ssd, d_lru)], [p["lru_norm"]],
                                      [(d_lru, F32), (d_lru, BF16)], [(1, d_lru)], name="b_lruout", tr=TR, sub=SUB)
    da, du = _lru_scan_bwd(a, hprev, dhl, name="b_lruscan")
    dxl, d_wa, d_ba, d_wx, d_bx, d_lam = _grouped(
        _gate_head_bwd, [xl, da, du], gate_p, [F32],
        [(LRU_HEADS, LRU_BLOCK, LRU_BLOCK), (1, d_lru), (LRU_HEADS, LRU_BLOCK, LRU_BLOCK), (1, d_lru), (1, d_lru)],
        gw=LRU_BLOCK, name="b_lrugates", tr=TR)
    dxlru, dwb_lru = _conv_bwd(dxl, 0, None, proj, c_xl, d_lru, p["lru_conv_w"], silu=False, name="b_lruconv")
    dyraw, dz, d_ssd_norm = _grouped(_ssd_gate_bwd, [yraw, (proj, 0), (dycat, 0)], [p["ssd_norm"]], [F32, BF16], [(1, d_ssd)],
                                     gw=GROUP_W, name="b_ssdgate", tr=TR)
    dact, ddtraw, dpar = _ssd_bwd(act, dtraw, dyraw, hst, p["dtb"], p["alog"], p["drow"], name="b_ssd")
    dxbc, dwb_ssd = _conv_bwd(dact, 0, pre, proj, c_xbc, d_xbc, p["ssd_conv_w"], silu=True, name="b_ssdconv")
    dproj = jnp.concatenate([dz, dxbc.astype(BF16), dgate, dxlru.astype(BF16)], axis=1)
    (dh_a,) = _matmul(dproj, p["w_main"], mode="nt", m=L, n=D, k=n_main, out_dtypes=[F32], name="b_inproj_dx", **mm)
    (dh_b,) = _matmul(ddtraw, p["w_dt"], mode="nt", m=L, n=D, k=128, out_dtypes=[F32], name="b_dtproj_dx", **mm)
    (dw_main,) = _matmul(h, dproj, mode="tn", m=D, n=n_main, k=L, out_dtypes=[BF16], name="b_inproj_dw", **mm)
    (dw_dt,) = _matmul(h, ddtraw, mode="tn", m=D, n=128, k=L, out_dtypes=[BF16], name="b_dtproj_dw", **mm)
    grad_x, d_pre_mix = _rowwise(_in_bwd, [x, dh_a, dh_b, dx1], [p["pre_mix_norm"]], [(D, F32)], [(1, D)],
                                 name="b_prenorm", tr=TR, sub=SUB)

    large = dict(w_main=dw_main, w_dt=dw_dt, w_out=dw_out, w_mi=dw_mi, w_mo=dw_mo)
    small = dict(loss=loss, pre_mix_norm=d_pre_mix, ssd_conv=dwb_ssd, ssd_par=dpar, ssd_norm=d_ssd_norm, lru_conv=dwb_lru,
                 lru_w_a=d_wa, lru_b_a=d_ba, lru_w_x=d_wx, lru_b_x=d_bx, lru_lambda=d_lam, lru_norm=d_lru_norm,
                 post_mix_norm=d_post_mix, pre_mlp_norm=d_pre_mlp, post_mlp_norm=d_post_mlp)
    return grad_x, large, small


ANY = pl.BlockSpec(memory_space=pl.ANY)


def _place():
    return lax.axis_index("x"), lax.axis_index("y"), lax.axis_index("c")


def _other_chips(x, y):
    return [(1 - x, y), (x, 1 - y), (1 - x, 1 - y)]


def _allgather8(blk, *, name):
    r, n = blk.shape

    def body(x_ref, out_ref, send_sems, recv_sems, local_sem):
        x, y, c = _place()
        me, sibling = (x, y, c), (x, y, 1 - c)
        chips = _other_chips(x, y)

        def rows(px, py, pc):
            return out_ref.at[pl.ds((4 * px + 2 * py + pc) * r, r), :]

        def copy(k, block, to, src=None):
            return pltpu.make_async_remote_copy(
                src_ref=rows(*block) if src is None else src, dst_ref=rows(*block),
                send_sem=send_sems.at[k], recv_sem=recv_sems.at[k], device_id=to, device_id_type=MESH)

        mine = pltpu.make_async_copy(x_ref, rows(*me), local_sem)
        mine.start()
        first = [copy(0, me, sibling, src=x_ref)]
        first += [copy(1 + k, me, (*chip, c), src=x_ref) for k, chip in enumerate(chips)]
        for cp in first:
            cp.start()
        passed = [copy(4 + k, (*chip, c), sibling) for k, chip in enumerate(chips)]
        for k, chip in enumerate(chips):
            copy(1 + k, (*chip, c), me).wait_recv()
            passed[k].start()
        copy(0, sibling, me).wait_recv()
        for k, chip in enumerate(chips):
            copy(4 + k, (*chip, 1 - c), me).wait_recv()
        for cp in first + passed:
            cp.wait_send()
        mine.wait()

    return pl.pallas_call(
        body, name=name,
        out_shape=jax.ShapeDtypeStruct((8 * r, n), blk.dtype),
        in_specs=[pl.BlockSpec(memory_space=pltpu.VMEM)], out_specs=pl.BlockSpec(memory_space=pltpu.VMEM),
        scratch_shapes=[pltpu.SemaphoreType.DMA((7,)), pltpu.SemaphoreType.DMA((7,)), pltpu.SemaphoreType.DMA],
        compiler_params=pltpu.CompilerParams(vmem_limit_bytes=VMEM_LIMIT),
    )(blk)


def _sum8(g, *, name, tr=232):
    _, r, n = g.shape
    tr = min(tr, r)
    assert r % tr == 0

    def body(g_ref, o_ref):
        s = g_ref[0]
        for k in range(1, 8):
            s = s + g_ref[k]
        o_ref[...] = s

    return pl.pallas_call(
        body, name=name, grid=(r // tr,),
        in_specs=[pl.BlockSpec((8, tr, n), lambda i: (0, i, 0))], out_specs=pl.BlockSpec((tr, n), lambda i: (i, 0)),
        out_shape=jax.ShapeDtypeStruct((r, n), g.dtype), compiler_params=_params(("parallel",)),
    )(g)


def _blocks(fn, ins, outs, *, grid, name, prefetch=None, aliases=None):
    n_in = len(ins)

    def body(*refs):
        if prefetch is not None:
            refs = refs[1:]
        res = fn(*[r[...] for r in refs[:n_in]])
        for o_ref, o in zip(refs[n_in:], res):
            o_ref[...] = o.astype(o_ref.dtype)

    in_specs = [pl.BlockSpec(b, m) for _, b, m in ins]
    out_specs = [pl.BlockSpec(b, m) for _, b, m in outs]
    kw = dict(name=name, out_shape=[s for s, _, _ in outs], input_output_aliases=aliases or {},
              compiler_params=_params(("arbitrary",) * len(grid)))
    arrs = [a for a, _, _ in ins]
    if prefetch is None:
        return pl.pallas_call(body, grid=grid, in_specs=in_specs, out_specs=out_specs, **kw)(*arrs)
    spec = pltpu.PrefetchScalarGridSpec(num_scalar_prefetch=1, grid=grid, in_specs=in_specs, out_specs=out_specs)
    return pl.pallas_call(body, grid_spec=spec, **kw)(prefetch, *arrs)


def _gather_weights(slots, *, name):
    n = len(slots)

    def body(*refs):
        i_refs, g_refs = refs[:n], refs[n:2 * n]
        send_sems, recv_sems = refs[2 * n:]
        x, y, c = _place()
        j, sibling = 2 * x + y, (x, y, 1 - c)
        chips = _other_chips(x, y)

        def cp(w, k, src, dst, to):
            return pltpu.make_async_remote_copy(src_ref=src, dst_ref=dst, send_sem=send_sems.at[6 * w + k],
                                                recv_sem=recv_sems.at[6 * w + k], device_id=to, device_id_type=MESH)

        def part(ref, chip_idx, hc):
            rh = ref.shape[1] // 2
            return ref.at[chip_idx, pl.ds(hc * rh, rh), :]

        sends = []
        for w in range(n):
            for k, chip in enumerate(chips):
                d = cp(w, k, part(i_refs[w], j, c), part(g_refs[w], j, c), (*chip, c))
                d.start()
                sends.append(d)
        for w in range(n):
            for k, (px, py) in enumerate(chips):
                landed = part(g_refs[w], 2 * px + py, c)
                cp(w, k, landed, landed, (px, py, c)).wait_recv()
                d = cp(w, 3 + k, landed, landed, sibling)
                d.start()
                sends.append(d)
        for w in range(n):
            for k, (px, py) in enumerate(chips):
                other = part(g_refs[w], 2 * px + py, 1 - c)
                cp(w, 3 + k, other, other, sibling).wait_recv()
        for d in sends:
            d.wait_send()

    return pl.pallas_call(
        body, name=name,
        out_shape=[jax.ShapeDtypeStruct(s.shape, s.dtype) for s in slots],
        in_specs=[ANY] * n, out_specs=[ANY] * n, input_output_aliases={w: w for w in range(n)},
        scratch_shapes=[pltpu.SemaphoreType.DMA((6 * n,)), pltpu.SemaphoreType.DMA((6 * n,))],
    )(*slots)


def _rs_sibling(gs, *, name):
    n = len(gs)

    def body(*refs):
        g_refs, got_refs = refs[:n], refs[n:2 * n]
        send_sems, recv_sems = refs[2 * n:]
        x, y, c = _place()
        copies = []
        for w in range(n):
            for s in range(4):
                d = pltpu.make_async_remote_copy(src_ref=g_refs[w].at[s, 1 - c], dst_ref=got_refs[w].at[s],
                                                 send_sem=send_sems.at[4 * w + s], recv_sem=recv_sems.at[4 * w + s],
                                                 device_id=(x, y, 1 - c), device_id_type=MESH)
                d.start()
                copies.append(d)
        for d in copies:
            d.wait_recv()
        for d in copies:
            d.wait_send()

    return pl.pallas_call(
        body, name=name, out_shape=[jax.ShapeDtypeStruct((4,) + g.shape[2:], g.dtype) for g in gs],
        in_specs=[ANY] * n, out_specs=[ANY] * n,
        scratch_shapes=[pltpu.SemaphoreType.DMA((4 * n,)), pltpu.SemaphoreType.DMA((4 * n,))],
    )(*gs)


def _rs_chips(ss, *, name):
    n = len(ss)

    def body(*refs):
        s_refs, got_refs = refs[:n], refs[n:4 * n]
        send_sems, recv_sems = refs[4 * n:]
        x, y, c = _place()
        chips = _other_chips(x, y)
        copies = []
        for w in range(n):
            for k, (px, py) in enumerate(chips):
                d = pltpu.make_async_remote_copy(src_ref=s_refs[w].at[2 * px + py], dst_ref=got_refs[3 * w + k],
                                                 send_sem=send_sems.at[3 * w + k], recv_sem=recv_sems.at[3 * w + k],
                                                 device_id=(px, py, c), device_id_type=MESH)
                d.start()
                copies.append(d)
        for d in copies:
            d.wait_recv()
        for d in copies:
            d.wait_send()

    got = [jax.ShapeDtypeStruct(s.shape[1:], s.dtype) for s in ss for _ in range(3)]
    outs = pl.pallas_call(
        body, name=name, out_shape=got, in_specs=[ANY] * n, out_specs=[ANY] * (3 * n),
        scratch_shapes=[pltpu.SemaphoreType.DMA((3 * n,)), pltpu.SemaphoreType.DMA((3 * n,))],
    )(*ss)
    return [outs[3 * w:3 * w + 3] for w in range(n)]


def _share_halves(ts, *, name):
    n = len(ts)

    def body(*refs):
        t_refs, o_refs = refs[:n], refs[n:2 * n]
        send_sems, recv_sems = refs[2 * n:]
        x, y, c = _place()

        def cp(w, src, dst):
            return pltpu.make_async_remote_copy(src_ref=src, dst_ref=dst, send_sem=send_sems.at[w], recv_sem=recv_sems.at[w],
                                                device_id=(x, y, 1 - c), device_id_type=MESH)

        copies = [cp(w, t_refs[w].at[c], o_refs[w].at[c]) for w in range(n)]
        for d in copies:
            d.start()
        for w in range(n):
            cp(w, t_refs[w].at[c], o_refs[w].at[1 - c]).wait_recv()
        for d in copies:
            d.wait_send()

    return pl.pallas_call(
        body, name=name, out_shape=[jax.ShapeDtypeStruct(t.shape, t.dtype) for t in ts],
        in_specs=[ANY] * n, out_specs=[ANY] * n, input_output_aliases={w: w for w in range(n)},
        scratch_shapes=[pltpu.SemaphoreType.DMA((n,)), pltpu.SemaphoreType.DMA((n,))],
    )(*ts)


def _pack(arrs):
    parts = []
    for v in arrs:
        f = v.reshape(-1)
        f = jnp.pad(f, (0, (-f.shape[0]) % 1024))
        parts.append(f.reshape(-1, 128))
    return jnp.concatenate(parts, axis=0)


def _unpack(packed, shapes):
    out, r0 = [], 0
    for s in shapes:
        size = 1
        for d in s:
            size *= d
        nr = (size + 1023) // 1024 * 8
        out.append(packed[r0:r0 + nr].reshape(-1)[:size].reshape(s))
        r0 += nr
    return out


SMALL_GRADS = [("loss", (1, 128)), ("pre_mix_norm", (1, 2048)), ("ssd_conv", (8, 4096)), ("ssd_par", (8, 128)),
               ("ssd_norm", (1, 2048)), ("lru_conv", (8, 2048)), ("lru_w_a", (16, 128, 128)), ("lru_b_a", (1, 2048)),
               ("lru_w_x", (16, 128, 128)), ("lru_b_x", (1, 2048)), ("lru_lambda", (1, 2048)), ("lru_norm", (1, 2048)),
               ("post_mix_norm", (1, 2048)), ("pre_mlp_norm", (1, 2048)), ("post_mlp_norm", (1, 2048))]

WEIGHTS = ['pre_mix_norm', 'w_in', 'ssd_conv_w', 'ssd_conv_b', 'ssd_dt_bias', 'ssd_a_log', 'ssd_d', 'ssd_norm', 'lru_conv_w',
           'lru_conv_b', 'lru_w_a', 'lru_b_a', 'lru_w_x', 'lru_b_x', 'lru_lambda', 'lru_norm', 'w_out', 'post_mix_norm',
           'pre_mlp_norm', 'w_mlp_in', 'w_mlp_out', 'post_mlp_norm']
LARGE = ['w_in', 'w_out', 'w_mlp_in', 'w_mlp_out']

D_SSD, D_XBC, DT_W = 2048, 4096, 32


def kernel(x, pre_mix_norm, w_in, ssd_conv_w, ssd_conv_b, ssd_dt_bias, ssd_a_log, ssd_d, ssd_norm, lru_conv_w, lru_conv_b, lru_w_a, lru_b_a, lru_w_x, lru_b_x, lru_lambda, lru_norm, w_out, post_mix_norm, pre_mlp_norm, w_mlp_in, w_mlp_out, post_mlp_norm, loss_target, m_pre_mix_norm, m_w_in, m_ssd_conv_w, m_ssd_conv_b, m_ssd_dt_bias, m_ssd_a_log, m_ssd_d, m_ssd_norm, m_lru_conv_w, m_lru_conv_b, m_lru_w_a, m_lru_b_a, m_lru_w_x, m_lru_b_x, m_lru_lambda, m_lru_norm, m_w_out, m_post_mix_norm, m_pre_mlp_norm, m_w_mlp_in, m_w_mlp_out, m_post_mlp_norm, v_pre_mix_norm, v_w_in, v_ssd_conv_w, v_ssd_conv_b, v_ssd_dt_bias, v_ssd_a_log, v_ssd_d, v_ssd_norm, v_lru_conv_w, v_lru_conv_b, v_lru_w_a, v_lru_b_a, v_lru_w_x, v_lru_b_x, v_lru_lambda, v_lru_norm, v_w_out, v_post_mix_norm, v_pre_mlp_norm, v_w_mlp_in, v_w_mlp_out, v_post_mlp_norm):
    a = dict(locals())
    j = 2 * lax.axis_index("x") + lax.axis_index("y")
    D = x.shape[-1]
    c_dt = D_SSD + D_XBC

    where = jnp.stack([j, lax.axis_index("c")]).astype(jnp.int32)
    TB = 256
    slots = []
    for name in LARGE:
        w = a[name][0]
        r, cc = w.shape
        slots.append(_blocks(lambda t: (t,), [(w, (TB, cc), lambda i, s: (i, 0))],
                             [(jax.ShapeDtypeStruct((4, r, cc), BF16), (None, TB, cc), lambda i, s: (s[0], i, 0))],
                             grid=(r // TB,), name="cast_" + name, prefetch=where)[0])
    g_in, g_out, g_mi, g_mo = _gather_weights(slots, name="gather_weights")
    w_full = jnp.transpose(g_in, (1, 0, 2)).reshape(D, -1)
    w_main = jnp.concatenate([w_full[:, :c_dt], w_full[:, c_dt + DT_W:]], axis=1)
    w_dt = jnp.pad(w_full[:, c_dt:c_dt + DT_W], ((0, 0), (0, 128 - DT_W)))
    taps = jnp.concatenate([ssd_conv_w[0].reshape(-1, 128), lru_conv_w[0].reshape(-1, 128)], axis=0)
    taps = _allgather8(taps, name="gather_taps").reshape(8, taps.shape[0], 128)[0::2]
    n_ssd = ssd_conv_w.shape[1] * ssd_conv_w.shape[2] // 128
    ssd_taps = taps[:, :n_ssd].reshape(4, CONV_W, -1).transpose(1, 0, 2).reshape(CONV_W, -1)
    lru_taps = taps[:, n_ssd:].reshape(4, CONV_W, -1).transpose(1, 0, 2).reshape(CONV_W, -1)

    def row128(v):
        return jnp.pad(v, ((0, 0), (0, 128 - v.shape[1])))

    p = dict(pre_mix_norm=pre_mix_norm, w_main=w_main, w_dt=w_dt, ssd_conv_w=ssd_taps, ssd_conv_b=ssd_conv_b,
             dtb=row128(ssd_dt_bias), alog=row128(ssd_a_log), drow=row128(ssd_d), ssd_norm=ssd_norm,
             lru_conv_w=lru_taps, lru_conv_b=lru_conv_b, lru_w_a=lru_w_a[0], lru_b_a=lru_b_a.reshape(1, -1),
             lru_w_x=lru_w_x[0], lru_b_x=lru_b_x.reshape(1, -1), lru_lambda=lru_lambda, lru_norm=lru_norm,
             w_out=g_out.reshape(-1, D), post_mix_norm=post_mix_norm, pre_mlp_norm=pre_mlp_norm, w_mi=g_mi,
             w_mo=g_mo.reshape(-1, D), post_mlp_norm=post_mlp_norm)
    grad_x, large, small = _local_step(x[0], loss_target[0], p)

    packed = _pack([small[n] for n, _ in SMALL_GRADS])
    total = _sum8(_allgather8(packed, name="gather_small").reshape(8, packed.shape[0], 128), name="sum_small")
    tot = dict(zip([n for n, _ in SMALL_GRADS], _unpack(total, [s for _, s in SMALL_GRADS])))
    loss = tot["loss"][0, 0]
    n_sc, n_lc = ssd_conv_w.shape[2], lru_conv_w.shape[2]
    grads = dict(
        pre_mix_norm=tot["pre_mix_norm"],
        ssd_conv_w=lax.dynamic_slice(tot["ssd_conv"][:CONV_W], (0, j * n_sc), (CONV_W, n_sc))[None],
        ssd_conv_b=tot["ssd_conv"][CONV_W:CONV_W + 1],
        ssd_dt_bias=tot["ssd_par"][0:1, :DT_W], ssd_a_log=tot["ssd_par"][1:2, :DT_W], ssd_d=tot["ssd_par"][2:3, :DT_W],
        ssd_norm=tot["ssd_norm"],
        lru_conv_w=lax.dynamic_slice(tot["lru_conv"][:CONV_W], (0, j * n_lc), (CONV_W, n_lc))[None],
        lru_conv_b=tot["lru_conv"][CONV_W:CONV_W + 1],
        lru_w_a=tot["lru_w_a"][None], lru_b_a=tot["lru_b_a"].reshape(lru_b_a.shape),
        lru_w_x=tot["lru_w_x"][None], lru_b_x=tot["lru_b_x"].reshape(lru_b_x.shape),
        lru_lambda=tot["lru_lambda"], lru_norm=tot["lru_norm"], post_mix_norm=tot["post_mix_norm"],
        pre_mlp_norm=tot["pre_mlp_norm"], post_mlp_norm=tot["post_mlp_norm"])

    dw_in = jnp.concatenate([large["w_main"][:, :c_dt], large["w_dt"][:, :DT_W], large["w_main"][:, c_dt:]], axis=1)
    gs = [dw_in.reshape(D, 4, -1).transpose(1, 0, 2), large["w_out"].reshape(4, -1, D), large["w_mi"],
          large["w_mo"].reshape(4, -1, D)]
    gs = [g.reshape(4, 2, g.shape[1] // 2, g.shape[2]) for g in gs]
    theirs = _rs_sibling(gs, name="reduce_sibling")
    ss = []
    for name, g, t in zip(LARGE, gs, theirs):
        _, rh, cc = t.shape
        ss.append(_blocks(lambda u, v: (u.astype(F32) + v.astype(F32),),
                          [(g, (None, None, TB, cc), lambda q, i, s: (q, s[1], i, 0)), (t, (None, TB, cc), lambda q, i, s: (q, i, 0))],
                          [(jax.ShapeDtypeStruct(t.shape, BF16), (None, TB, cc), lambda q, i, s: (q, i, 0))],
                          grid=(4, rh // TB), name="add_sibling_" + name, prefetch=where)[0])
    got = _rs_chips(ss, name="reduce_chips")
    totals = []
    for name, s4, g3 in zip(LARGE, ss, got):
        _, rh, cc = s4.shape
        row = ((TB, cc), lambda i, s: (i, 0))
        totals.append(_blocks(lambda o, r0, r1, r2: (((o.astype(F32) + r0.astype(F32)) + r1.astype(F32)) + r2.astype(F32),),
                              [(s4, (None, TB, cc), lambda i, s: (s[0], i, 0)), (g3[0],) + row, (g3[1],) + row, (g3[2],) + row],
                              [(jax.ShapeDtypeStruct((2, rh, cc), F32), (None, TB, cc), lambda i, s: (s[1], i, 0))],
                              grid=(rh // TB,), name="add_chips_" + name, prefetch=where)[0])
    for name, g in zip(LARGE, _share_halves(totals, name="share_halves")):
        grads[name] = g.reshape(a[name].shape)

    delta, new_m, new_v = {}, {}, {}
    for name in LARGE:
        w = a[name][0]
        cc = w.shape[1]
        outs = _rowwise(_adamw, [w, grads[name][0], a["m_" + name][0], a["v_" + name][0]], [], [(cc, F32)] * 3, [],
                        name="adamw_" + name, tr=128, sub=8)
        delta[name], new_m[name], new_v[name] = [o[None] for o in outs]
    small_w = [n for n in WEIGHTS if n not in LARGE]
    packs = [_pack([d[n] for n in small_w]) for d in (a, grads, {n: a["m_" + n] for n in small_w}, {n: a["v_" + n] for n in small_w})]
    outs = _rowwise(_adamw, packs, [], [(128, F32)] * 3, [], name="adamw_small", tr=packs[0].shape[0], sub=8)
    for d, o in zip((delta, new_m, new_v), outs):
        d.update(zip(small_w, _unpack(o, [a[n].shape for n in small_w])))

    return (loss, grad_x[None], *[grads[n] for n in WEIGHTS], *[delta[n] for n in WEIGHTS],
            *[new_m[n] for n in WEIGHTS], *[new_v[n] for n in WEIGHTS])
```

```python
import functools

import jax
import jax.numpy as jnp
from jax import lax
from jax.experimental import pallas as pl
from jax.experimental.pallas import tpu as pltpu

F32 = jnp.float32
BF16 = jnp.bfloat16
MESH = pl.DeviceIdType.MESH

EPS = 1e-6
LRU_C = 8.0
ADAM_LR = 0.001
ADAM_B1 = 0.9
ADAM_B2 = 0.999
ADAM_EPS = 1e-08
ADAM_WD = 0.01
ADAM_STEP = 10

N_GROUPS = 8
HEADS_PER_GROUP = 4
HEAD_DIM = 64
GROUP_W = HEADS_PER_GROUP * HEAD_DIM
STATE = 128
LRU_HEADS = 16
LRU_BLOCK = 128
CONV_W = 4
SSD_CHUNK = 128
HALO = 8

VMEM_LIMIT = 48 * 1024 * 1024


def _params(sem=None):
    return pltpu.CompilerParams(dimension_semantics=sem, vmem_limit_bytes=VMEM_LIMIT)


@jax.custom_jvp
def _log1p(x):
    u = 1.0 + x
    d = u - 1.0
    return jnp.where(d == 0.0, x, jnp.log(u) * (x / jnp.where(d == 0.0, 1.0, d)))


@_log1p.defjvp
def _log1p_jvp(primals, tangents):
    (x,), (t,) = primals, tangents
    return _log1p(x), t / (1.0 + x)


@jax.custom_jvp
def _expm1(x):
    u = jnp.exp(x)
    lu = jnp.log(u)
    safe = jnp.where(lu == 0.0, 1.0, lu)
    y = (u - 1.0) * (x / safe)
    y = jnp.where(lu == 0.0, x, y)
    return jnp.where(u == 0.0, -1.0, y)


@_expm1.defjvp
def _expm1_jvp(primals, tangents):
    (x,), (t,) = primals, tangents
    return _expm1(x), t * jnp.exp(x)


def _softplus(x):
    return jnp.maximum(x, 0.0) + _log1p(jnp.exp(-jnp.abs(x)))


def _sigmoid(x):
    return 1.0 / (1.0 + jnp.exp(-x))


def _silu(x):
    return x * _sigmoid(x)


def _gelu(x):
    c = 0.7978845608028654
    return 0.5 * x * (1.0 + jnp.tanh(c * (x + 0.044715 * (x * x * x))))


def _rms(x, g):
    return x * lax.rsqrt(jnp.mean(x * x, axis=-1, keepdims=True) + EPS) * g


def _dot(a, b, dims):
    return lax.dot_general(a.astype(BF16), b.astype(BF16), (dims, ((), ())), preferred_element_type=F32)


_NN = ((1,), (0,))
_NT = ((1,), (1,))
_TN = ((0,), (0,))


ANY = pl.BlockSpec(memory_space=pl.ANY)


class _Side:
    def __init__(self, ins, outs, aliases, n_sems, make):
        self.ins, self.outs, self.aliases, self.n_sems, self.make = ins, outs, aliases, n_sems, make


def _call(body, args, *, name, grid, in_specs, out_specs, out_shape, scratch_shapes=(), sem=None, side=None):
    in_specs, out_specs, out_shape, scratch_shapes = list(in_specs), list(out_specs), list(out_shape), list(scratch_shapes)
    if side is None:
        outs = pl.pallas_call(body, name=name, grid=grid, in_specs=in_specs, out_specs=out_specs, out_shape=out_shape,
                              scratch_shapes=scratch_shapes, compiler_params=_params(sem))(*args)
        return list(outs), []
    n_in, n_out, n_scr, si, so = len(in_specs), len(out_specs), len(scratch_shapes), len(side.ins), len(side.outs)

    def full(*refs):
        s_in = refs[n_in:n_in + si]
        o0 = n_in + si
        s_out = refs[o0 + n_out:o0 + n_out + so]
        scr = refs[o0 + n_out + so:o0 + n_out + so + n_scr]
        send_sems, recv_sems = refs[-2], refs[-1]
        ids = [pl.program_id(d) for d in range(len(grid))]
        first = functools.reduce(jnp.logical_and, [i == 0 for i in ids])
        last = functools.reduce(jnp.logical_and, [i == g - 1 for i, g in zip(ids, grid)])

        @pl.when(first)
        def _():
            for d in side.make(s_in, s_out, send_sems, recv_sems)[0]:
                d.start()

        body(*refs[:n_in], *refs[o0:o0 + n_out], *scr)

        @pl.when(last)
        def _():
            sends, recvs = side.make(s_in, s_out, send_sems, recv_sems)
            for d in recvs:
                d.wait_recv()
            for d in sends:
                d.wait_send()

    outs = pl.pallas_call(
        full, name=name, grid=grid, in_specs=in_specs + [ANY] * si, out_specs=out_specs + [ANY] * so,
        out_shape=out_shape + list(side.outs),
        scratch_shapes=scratch_shapes + [pltpu.SemaphoreType.DMA((side.n_sems,)), pltpu.SemaphoreType.DMA((side.n_sems,))],
        input_output_aliases={n_in + i: n_out + o for i, o in side.aliases.items()},
        compiler_params=_params(("arbitrary",) * len(grid)),
    )(*args, *side.ins)
    return list(outs[:n_out]), list(outs[n_out:])


def _matmul(a, b, *, mode, m, n, k, tm, tn, tk, out_dtypes, name, a_spec=None, b_spec=None,
            out_specs=None, out_shapes=None, extras=(), epilogue=None, side=None):
    tm, tn, tk = min(tm, m), min(tn, n), min(tk, k)
    assert m % tm == 0 and n % tn == 0 and k % tk == 0, (name, m, n, k, tm, tn, tk)
    nk = k // tk
    dims = {"nn": _NN, "nt": _NT, "tn": _TN}[mode]
    if a_spec is None:
        a_spec = pl.BlockSpec((tk, tm), lambda i, j, kk: (kk, i)) if mode == "tn" else pl.BlockSpec((tm, tk), lambda i, j, kk: (i, kk))
    if b_spec is None:
        b_spec = pl.BlockSpec((tn, tk), lambda i, j, kk: (j, kk)) if mode == "nt" else pl.BlockSpec((tk, tn), lambda i, j, kk: (kk, j))
    tile = pl.BlockSpec((tm, tn), lambda i, j, kk: (i, j))
    if out_specs is None:
        out_specs = [tile for _ in out_dtypes]
    if out_shapes is None:
        out_shapes = [jax.ShapeDtypeStruct((m, n), d) for d in out_dtypes]
    n_ex, n_out = len(extras), len(out_dtypes)

    def body(*refs):
        a_ref, b_ref = refs[0], refs[1]
        ex_refs = refs[2:2 + n_ex]
        o_refs = refs[2 + n_ex:2 + n_ex + n_out]
        acc = refs[-1]
        kk = pl.program_id(2)

        @pl.when(kk == 0)
        def _():
            acc[...] = jnp.zeros_like(acc)

        acc[...] += _dot(a_ref[...], b_ref[...], dims)

        @pl.when(kk == nk - 1)
        def _():
            r = acc[...]
            outs = epilogue(r, *[e[...] for e in ex_refs]) if epilogue is not None else (r,)
            for o_ref, o in zip(o_refs, outs):
                o_ref[...] = o.astype(o_ref.dtype)

    outs, side_outs = _call(
        body, (a, b, *extras), name=name, grid=(m // tm, n // tn, nk),
        in_specs=[a_spec, b_spec] + [tile for _ in extras], out_specs=out_specs, out_shape=out_shapes,
        scratch_shapes=[pltpu.VMEM((tm, tn), F32)], sem=("parallel", "parallel", "arbitrary"), side=side)
    return outs if side is None else (outs, side_outs)


def _rowwise(fn, rows, bcast, out_rows, out_acc, *, name, tr, sub):
    rows = [r if isinstance(r, tuple) else (r, 0, r.shape[1]) for r in rows]
    row_specs = []
    for arr, c0, w in rows:
        assert c0 % w == 0, (name, c0, w)
        row_specs.append((w, c0 // w))
    rows = [r[0] for r in rows]
    L = rows[0].shape[0]
    tr = min(tr, L)
    sub = min(sub, tr)
    assert L % tr == 0 and tr % sub == 0, (name, L, tr, sub)
    n_r, n_b, n_or, n_oa = len(rows), len(bcast), len(out_rows), len(out_acc)

    def body(*refs):
        r_refs = refs[:n_r]
        b_refs = refs[n_r:n_r + n_b]
        or_refs = refs[n_r + n_b:n_r + n_b + n_or]
        oa_refs = refs[n_r + n_b + n_or:]
        i = pl.program_id(0)

        @pl.when(i == 0)
        def _():
            for o in oa_refs:
                o[...] = jnp.zeros_like(o)

        bvals = [b[...] for b in b_refs]

        def step(s, carry):
            r0 = pl.multiple_of(s * sub, sub)
            tiles = [r[pl.ds(r0, sub), :] for r in r_refs]
            outs = fn(*tiles, *bvals)
            for o_ref, o in zip(or_refs, outs[:n_or]):
                o_ref[pl.ds(r0, sub), :] = o.astype(o_ref.dtype)
            for o_ref, o in zip(oa_refs, outs[n_or:]):
                o_ref[...] += o
            return carry

        if tr == sub:
            step(0, 0)
        else:
            lax.fori_loop(0, tr // sub, step, 0)

    def whole(shape):
        nd = len(shape)
        return pl.BlockSpec(shape, lambda i, _n=nd: (0,) * _n)

    outs = pl.pallas_call(
        body, name=name, grid=(L // tr,),
        in_specs=[pl.BlockSpec((tr, w), lambda i, _c=cb: (i, _c)) for w, cb in row_specs] + [whole(b.shape) for b in bcast],
        out_specs=[pl.BlockSpec((tr, c), lambda i: (i, 0)) for c, _ in out_rows] + [whole(s) for s in out_acc],
        out_shape=[jax.ShapeDtypeStruct((L, c), d) for c, d in out_rows] + [jax.ShapeDtypeStruct(s, F32) for s in out_acc],
        compiler_params=_params(("arbitrary",)),
    )(*rows, *bcast)
    return outs


def _colsum(x):
    return jnp.sum(x, axis=0, keepdims=True)


def _grouped(fn, rows, params, out_rows, out_acc, *, gw, name, tr):
    rows = [r if isinstance(r, tuple) else (r, 0) for r in rows]
    L = rows[0][0].shape[0]
    tr = min(tr, L)
    assert L % tr == 0
    G = None
    for p in params:
        G = p.shape[0] if p.ndim == 3 else p.shape[1] // gw
    cw = G * gw
    n_r, n_p, n_or, n_oa = len(rows), len(params), len(out_rows), len(out_acc)

    def pick(ref, g):
        return ref[g] if len(ref.shape) == 3 else ref[:, g * gw:(g + 1) * gw]

    def body(*refs):
        r_refs = refs[:n_r]
        p_refs = refs[n_r:n_r + n_p]
        or_refs = refs[n_r + n_p:n_r + n_p + n_or]
        oa_refs = refs[n_r + n_p + n_or:]

        @pl.when(pl.program_id(0) == 0)
        def _():
            for o in oa_refs:
                o[...] = jnp.zeros_like(o)

        for g in range(G):
            outs = fn(*[pick(r, g) for r in r_refs], *[pick(p, g) for p in p_refs])
            for o_ref, o in zip(or_refs, outs[:n_or]):
                o_ref[:, g * gw:(g + 1) * gw] = o.astype(o_ref.dtype)
            for o_ref, o in zip(oa_refs, outs[n_or:]):
                if len(o_ref.shape) == 3:
                    o_ref[g] += o
                else:
                    o_ref[:, g * gw:(g + 1) * gw] += o

    def whole(shape):
        nd = len(shape)
        return pl.BlockSpec(shape, lambda i, _n=nd: (0,) * _n)

    for _, c0 in rows:
        assert c0 % cw == 0
    return pl.pallas_call(
        body, name=name, grid=(L // tr,),
        in_specs=[pl.BlockSpec((tr, cw), lambda i, _c=c0 // cw: (i, _c)) for _, c0 in rows] + [whole(p.shape) for p in params],
        out_specs=[pl.BlockSpec((tr, cw), lambda i: (i, 0)) for _ in out_rows] + [whole(s) for s in out_acc],
        out_shape=[jax.ShapeDtypeStruct((L, cw), d) for d in out_rows] + [jax.ShapeDtypeStruct(s, F32) for s in out_acc],
        compiler_params=_params(("arbitrary",)),
    )(*[r[0] for r in rows], *params)


def _dsilu(p):
    s = _sigmoid(p)
    return s + p * s * (1.0 - s)


def _conv_fwd(x, c0, C, w, b, *, silu, name, tr=256, cb=512, side=None):
    L = x.shape[0]
    tr = min(tr, L)
    nb, hb = L // tr, tr // HALO
    assert L % tr == 0 and C % cb == 0 and c0 % cb == 0
    n_out = 2 if silu else 1

    def body(x_ref, h_ref, w_ref, b_ref, *rest):
        o_refs, xp = rest[:n_out], rest[n_out]
        xp[0:HALO, :] = h_ref[...]

        @pl.when(pl.program_id(0) == 0)
        def _():
            xp[0:HALO, :] = jnp.zeros((HALO, cb), F32)

        xp[HALO:, :] = x_ref[...]
        wv = w_ref[...]
        y = b_ref[...]
        for k in range(CONV_W):
            y = y + wv[k:k + 1, :] * xp[HALO - 3 + k:HALO - 3 + k + tr, :]
        o_refs[0][...] = y
        if silu:
            o_refs[1][...] = _silu(y)

    outs, side_outs = _call(
        body, (x, x, w, b), name=name, grid=(nb, C // cb),
        in_specs=[
            pl.BlockSpec((tr, cb), lambda i, j: (i, c0 // cb + j)),
            pl.BlockSpec((HALO, cb), lambda i, j: (jnp.maximum(i * hb - 1, 0), c0 // cb + j)),
            pl.BlockSpec((CONV_W, cb), lambda i, j: (0, j)),
            pl.BlockSpec((1, cb), lambda i, j: (0, j)),
        ],
        out_specs=[pl.BlockSpec((tr, cb), lambda i, j: (i, j)) for _ in range(n_out)],
        out_shape=[jax.ShapeDtypeStruct((L, C), F32) for _ in range(n_out)],
        scratch_shapes=[pltpu.VMEM((tr + HALO, cb), F32)], sem=("parallel", "parallel"), side=side)
    return outs if side is None else (outs, side_outs)


def _conv_bwd(dact, dc0, pre, x, xc0, C, w, *, silu, name, tr=256, cb=512):
    L = x.shape[0]
    tr = min(tr, L)
    nb, hb = L // tr, tr // HALO
    last_h = L // HALO - 1

    def body(*refs):
        if silu:
            d_ref, dh_ref, p_ref, ph_ref, x_ref, xh_ref, w_ref, dx_ref, dwb_ref, dp, xp = refs
        else:
            d_ref, dh_ref, x_ref, xh_ref, w_ref, dx_ref, dwb_ref, dp, xp = refs
        i = pl.program_id(1)
        dm, dh = d_ref[...], dh_ref[...]
        if silu:
            dm = dm * _dsilu(p_ref[...])
            dh = dh * _dsilu(ph_ref[...])
        dp[0:tr, :] = dm
        dp[tr:, :] = dh

        @pl.when(i == nb - 1)
        def _():
            dp[tr:, :] = jnp.zeros((HALO, cb), F32)

        xp[0:HALO, :] = xh_ref[...]

        @pl.when(i == 0)
        def _():
            xp[0:HALO, :] = jnp.zeros((HALO, cb), F32)
            dwb_ref[...] = jnp.zeros_like(dwb_ref)

        xp[HALO:, :] = x_ref[...]
        wv = w_ref[...]
        dx = jnp.zeros((tr, cb), F32)
        for k in range(CONV_W):
            dx = dx + wv[k:k + 1, :] * dp[3 - k:3 - k + tr, :]
            dwb_ref[k:k + 1, :] += _colsum(dm * xp[HALO - 3 + k:HALO - 3 + k + tr, :])
        dx_ref[...] = dx.astype(dx_ref.dtype)
        dwb_ref[CONV_W:CONV_W + 1, :] += _colsum(dm)

    def main(c):
        return pl.BlockSpec((tr, cb), lambda j, i: (i, c // cb + j))

    def nxt(c):
        return pl.BlockSpec((HALO, cb), lambda j, i: (jnp.minimum((i + 1) * hb, last_h), c // cb + j))

    in_specs = [main(dc0), nxt(dc0)]
    args = [dact, dact]
    if silu:
        in_specs += [main(0), nxt(0)]
        args += [pre, pre]
    in_specs += [main(xc0), pl.BlockSpec((HALO, cb), lambda j, i: (jnp.maximum(i * hb - 1, 0), xc0 // cb + j)),
                 pl.BlockSpec((CONV_W, cb), lambda j, i: (0, j))]
    args += [x, x, w]
    return pl.pallas_call(
        body, name=name, grid=(C // cb, nb),
        in_specs=in_specs,
        out_specs=[pl.BlockSpec((tr, cb), lambda j, i: (i, j)), pl.BlockSpec((8, cb), lambda j, i: (0, j))],
        out_shape=[jax.ShapeDtypeStruct((L, C), BF16), jax.ShapeDtypeStruct((8, C), F32)],
        scratch_shapes=[pltpu.VMEM((tr + HALO, cb), F32), pltpu.VMEM((tr + HALO, cb), F32)],
        compiler_params=_params(("parallel", "arbitrary")),
    )(*args)


def _ssd_f1(dtraw, dtb, alog):
    q = dtraw.shape[0]
    dt = _softplus(dtraw + dtb)
    adt = dt * (-jnp.exp(alog))
    tril = (lax.broadcasted_iota(jnp.int32, (q, q), 0) >= lax.broadcasted_iota(jnp.int32, (q, q), 1)).astype(F32)
    acs = lax.dot_general(tril, adt, (_NN, ((), ())), precision=lax.Precision.HIGHEST, preferred_element_type=F32)
    return dt, acs


def _ssd_group(g, x, bm, cm, dt, acs, drow, hp):
    q = x.shape[0]
    lane = lax.broadcasted_iota(jnp.int32, (1, 128), 1)
    sub = lax.broadcasted_iota(jnp.int32, (128, 1), 0)
    head_of = lax.broadcasted_iota(jnp.int32, (1, GROUP_W), 1) // HEAD_DIM
    is_last = (lax.broadcasted_iota(jnp.int32, (q, 1), 0) == q - 1).astype(F32)
    causal = lax.broadcasted_iota(jnp.int32, (q, q), 0) >= lax.broadcasted_iota(jnp.int32, (q, q), 1)
    acs_end = jnp.sum(acs * is_last, axis=0, keepdims=True)
    acs_t = acs.T
    dt_exp = jnp.zeros((q, GROUP_W), F32)
    acs_exp = jnp.zeros((q, GROUP_W), F32)
    end_exp = jnp.zeros((1, GROUP_W), F32)
    d_exp = jnp.zeros((1, GROUP_W), F32)
    heads = []
    for k in range(HEADS_PER_GROUP):
        h = HEADS_PER_GROUP * g + k
        oh = (lane == h).astype(F32)
        mk = (head_of == k).astype(F32)
        acs_col = jnp.sum(acs * oh, axis=1, keepdims=True)
        acs_row = jnp.sum(acs_t * (sub == h).astype(F32), axis=0, keepdims=True)
        dt_exp = dt_exp + jnp.sum(dt * oh, axis=1, keepdims=True) * mk
        acs_exp = acs_exp + acs_col * mk
        end_exp = end_exp + jnp.sum(acs_end * oh, axis=1, keepdims=True) * mk
        d_exp = d_exp + jnp.sum(drow * oh, axis=1, keepdims=True) * mk
        heads.append((acs_col, acs_row, mk))
    xdt = x * dt_exp
    states = _dot(bm, xdt * jnp.exp(end_exp - acs_exp), _TN)
    y = _dot(cm, hp, _NN) * jnp.exp(acs_exp) + x * d_exp
    scores = _dot(cm, bm, _NT)
    for acs_col, acs_row, mk in heads:
        dec = jnp.exp(jnp.where(causal, acs_col - acs_row, -jnp.inf))
        y = y + _dot(scores * dec, xdt * mk, _NN)
    return y, hp * jnp.exp(end_exp) + states


def _ssd_fwd(act, dtraw, dtb, alog, drow, *, name):
    L = act.shape[0]
    q = min(SSD_CHUNK, L)
    nc = L // q
    d_ssd = N_GROUPS * GROUP_W

    def body(act_ref, dt_ref, dtb_ref, alog_ref, drow_ref, y_ref, hst_ref, h):
        @pl.when(pl.program_id(0) == 0)
        def _():
            h[...] = jnp.zeros_like(h)

        dt, acs = _ssd_f1(dt_ref[...], dtb_ref[...], alog_ref[...])
        drow = drow_ref[...]
        for g in range(N_GROUPS):
            hp = h[g]
            hst_ref[0, g] = hp
            y, hn = _ssd_group(g, act_ref[:, g * GROUP_W:(g + 1) * GROUP_W],
                               act_ref[:, d_ssd + g * STATE:d_ssd + (g + 1) * STATE],
                               act_ref[:, d_ssd + (N_GROUPS + g) * STATE:d_ssd + (N_GROUPS + g + 1) * STATE],
                               dt, acs, drow, hp)
            y_ref[:, g * GROUP_W:(g + 1) * GROUP_W] = y
            h[g] = hn

    row = pl.BlockSpec((1, 128), lambda i: (0, 0))
    return pl.pallas_call(
        body, name=name, grid=(nc,),
        in_specs=[pl.BlockSpec((q, act.shape[1]), lambda i: (i, 0)), pl.BlockSpec((q, 128), lambda i: (i, 0)), row, row, row],
        out_specs=[pl.BlockSpec((q, d_ssd), lambda i: (i, 0)),
                   pl.BlockSpec((1, N_GROUPS, STATE, GROUP_W), lambda i: (i, 0, 0, 0))],
        out_shape=[jax.ShapeDtypeStruct((L, d_ssd), F32), jax.ShapeDtypeStruct((nc, N_GROUPS, STATE, GROUP_W), F32)],
        scratch_shapes=[pltpu.VMEM((N_GROUPS, STATE, GROUP_W), F32)],
        compiler_params=_params(("arbitrary",)),
    )(act, dtraw, dtb, alog, drow)


def _ssd_bwd(act, dtraw, dy, hst, dtb, alog, drow, *, name, side=None):
    L = act.shape[0]
    q = min(SSD_CHUNK, L)
    nc = L // q
    d_ssd = N_GROUPS * GROUP_W

    def body(act_ref, dt_ref, dy_ref, hst_ref, dtb_ref, alog_ref, drow_ref, dact_ref, ddt_ref, dpar_ref, dh):
        @pl.when(pl.program_id(0) == 0)
        def _():
            dh[...] = jnp.zeros_like(dh)
            dpar_ref[...] = jnp.zeros_like(dpar_ref)

        (dt, acs), f1_vjp = jax.vjp(_ssd_f1, dt_ref[...], dtb_ref[...], alog_ref[...])
        drow = drow_ref[...]
        ddt = jnp.zeros_like(dt)
        dacs = jnp.zeros_like(acs)
        ddrow = jnp.zeros_like(drow)
        for g in range(N_GROUPS):
            xs = slice(g * GROUP_W, (g + 1) * GROUP_W)
            bs = slice(d_ssd + g * STATE, d_ssd + (g + 1) * STATE)
            cs = slice(d_ssd + (N_GROUPS + g) * STATE, d_ssd + (N_GROUPS + g + 1) * STATE)
            _, f2_vjp = jax.vjp(functools.partial(_ssd_group, g), act_ref[:, xs], act_ref[:, bs], act_ref[:, cs],
                                dt, acs, drow, hst_ref[0, g])
            dx, dbm, dcm, ddt_g, dacs_g, ddrow_g, dhp = f2_vjp((dy_ref[:, xs], dh[g]))
            dact_ref[:, xs] = dx
            dact_ref[:, bs] = dbm
            dact_ref[:, cs] = dcm
            dh[g] = dhp
            ddt, dacs, ddrow = ddt + ddt_g, dacs + dacs_g, ddrow + ddrow_g
        ddtraw, ddtb, dalog = f1_vjp((ddt, dacs))
        ddt_ref[...] = ddtraw
        dpar_ref[0:1, :] += ddtb
        dpar_ref[1:2, :] += dalog
        dpar_ref[2:3, :] += ddrow

    row = pl.BlockSpec((1, 128), lambda i: (0, 0))
    rev = lambda i: (nc - 1 - i, 0)
    outs, side_outs = _call(
        body, (act, dtraw, dy, hst, dtb, alog, drow), name=name, grid=(nc,),
        in_specs=[pl.BlockSpec((q, act.shape[1]), rev), pl.BlockSpec((q, 128), rev), pl.BlockSpec((q, d_ssd), rev),
                  pl.BlockSpec((1, N_GROUPS, STATE, GROUP_W), lambda i: (nc - 1 - i, 0, 0, 0)), row, row, row],
        out_specs=[pl.BlockSpec((q, act.shape[1]), rev), pl.BlockSpec((q, 128), rev), pl.BlockSpec((8, 128), lambda i: (0, 0))],
        out_shape=[jax.ShapeDtypeStruct(act.shape, F32), jax.ShapeDtypeStruct((L, 128), F32), jax.ShapeDtypeStruct((8, 128), F32)],
        scratch_shapes=[pltpu.VMEM((N_GROUPS, STATE, GROUP_W), F32)], sem=("arbitrary",), side=side)
    return outs if side is None else (outs, side_outs)


def _gate_head(xl, wa, ba, wx, bx, lam):
    r = _sigmoid(_dot(xl, wa, _NN) + ba)
    i = _sigmoid(_dot(xl, wx, _NN) + bx)
    log_a = -LRU_C * r * _softplus(-lam)
    return jnp.exp(log_a), jnp.sqrt(-_expm1(2.0 * log_a)) * (i * xl)


def _gate_head_bwd(xl, da, du, wa, ba, wx, bx, lam):
    _, vjp = jax.vjp(_gate_head, xl, wa, ba, wx, bx, lam)
    dxl, dwa, dba, dwx, dbx, dlam = vjp((da, du))
    return dxl, dwa, dba, dwx, dbx, dlam


def _scan_tile(a, b, rows, reverse):
    for d in (1, 2, 4):
        if reverse:
            keep = rows < 8 - d
            a_sh, b_sh = pltpu.roll(a, 8 - d, 0), pltpu.roll(b, 8 - d, 0)
        else:
            keep = rows >= d
            a_sh, b_sh = pltpu.roll(a, d, 0), pltpu.roll(b, d, 0)
        b = b + a * jnp.where(keep, b_sh, 0.0)
        a = a * jnp.where(keep, a_sh, 1.0)
    return a, b


def _lru_scan_fwd(a, u, *, name, tr=512, cb=1024):
    L, C = a.shape
    tr, cb = min(tr, L), min(cb, C)

    def body(a_ref, u_ref, h_ref, hp_ref, carry):
        @pl.when(pl.program_id(1) == 0)
        def _():
            carry[...] = jnp.zeros_like(carry)

        rows = lax.broadcasted_iota(jnp.int32, (8, cb), 0)

        def tile(t, hc):
            r0 = pl.multiple_of(t * 8, 8)
            pa, hb = _scan_tile(a_ref[pl.ds(r0, 8), :], u_ref[pl.ds(r0, 8), :], rows, False)
            h = hb + pa * hc
            h_ref[pl.ds(r0, 8), :] = h
            hp_ref[pl.ds(r0, 8), :] = jnp.where(rows >= 1, pltpu.roll(h, 1, 0), hc)
            return h[7:8, :]

        carry[...] = lax.fori_loop(0, tr // 8, tile, carry[...])

    blk = pl.BlockSpec((tr, cb), lambda j, i: (i, j))
    return pl.pallas_call(
        body, name=name, grid=(C // cb, L // tr),
        in_specs=[blk, blk], out_specs=[blk, blk],
        out_shape=[jax.ShapeDtypeStruct((L, C), F32), jax.ShapeDtypeStruct((L, C), F32)],
        scratch_shapes=[pltpu.VMEM((1, cb), F32)],
        compiler_params=_params(("parallel", "arbitrary")),
    )(a, u)


def _lru_scan_bwd(a, hprev, dh, *, name, tr=512, cb=1024):
    L, C = a.shape
    tr, cb = min(tr, L), min(cb, C)
    nb = L // tr

    def body(a_ref, hp_ref, dh_ref, da_ref, du_ref, carry):
        @pl.when(pl.program_id(1) == 0)
        def _():
            carry[...] = jnp.zeros_like(carry)

        rows = lax.broadcasted_iota(jnp.int32, (8, cb), 0)

        def tile(t, gc):
            r0 = pl.multiple_of((tr // 8 - 1 - t) * 8, 8)
            av, dv = a_ref[pl.ds(r0, 8), :], dh_ref[pl.ds(r0, 8), :]
            pa, gb = _scan_tile(av, av * dv, rows, True)
            big = gb + pa * gc
            g = dv + jnp.where(rows < 7, pltpu.roll(big, 7, 0), gc)
            du_ref[pl.ds(r0, 8), :] = g
            da_ref[pl.ds(r0, 8), :] = g * hp_ref[pl.ds(r0, 8), :]
            return big[0:1, :]

        carry[...] = lax.fori_loop(0, tr // 8, tile, carry[...])

    blk = pl.BlockSpec((tr, cb), lambda j, i: (nb - 1 - i, j))
    return pl.pallas_call(
        body, name=name, grid=(C // cb, nb),
        in_specs=[blk, blk, blk], out_specs=[blk, blk],
        out_shape=[jax.ShapeDtypeStruct((L, C), F32), jax.ShapeDtypeStruct((L, C), F32)],
        scratch_shapes=[pltpu.VMEM((1, cb), F32)],
        compiler_params=_params(("parallel", "arbitrary")),
    )(a, hprev, dh)


def _ssd_gate(y, z, n):
    v = y * _silu(z)
    return v * lax.rsqrt(jnp.mean(v * v, axis=-1, keepdims=True) + EPS) * n


def _ssd_gate_bwd(y, z, dy, n):
    _, vjp = jax.vjp(_ssd_gate, y, z, n)
    return vjp(dy)


def _lru_out(hl, gate, n):
    return _rms(hl * _gelu(gate), n)


def _lru_out_bwd(hl, gate, dy, n):
    _, vjp = jax.vjp(_lru_out, hl, gate, n)
    return vjp(dy)


def _mid(x, mix, pm, pmlp):
    x1 = x + _rms(mix, pm)
    return x1, _rms(x1, pmlp)


def _mid_bwd(x, mix, dx1p, dh2, pm, pmlp):
    _, vjp = jax.vjp(_mid, x, mix, pm, pmlp)
    dx, dmix, dpm, dpmlp = vjp((dx1p, dh2))
    return dmix, dx, dpm, dpmlp


def _loss_bwd(hm2, x1, tgt, g):
    def lossf(hm2, x1, g):
        e = x1 + _rms(hm2, g) - tgt
        return 0.5 * jnp.sum(jnp.mean(e * e, axis=-1, keepdims=True), axis=0, keepdims=True)

    val, vjp = jax.vjp(lossf, hm2, x1, g)
    dhm2, dx1, dg = vjp(jnp.ones((1, 1), F32))
    return dhm2, dx1, dg, val * jnp.ones((1, 128), F32)


def _in_bwd(x, dh_a, dh_b, dx1, g):
    _, vjp = jax.vjp(_rms, x, g)
    dx, dg = vjp(dh_a + dh_b)
    return dx + dx1, dg


def _adamw(w, g, m, v):
    m = ADAM_B1 * m + (1.0 - ADAM_B1) * g
    v = ADAM_B2 * v + (1.0 - ADAM_B2) * (g * g)
    m_hat = m / (1.0 - ADAM_B1 ** ADAM_STEP)
    v_hat = v / (1.0 - ADAM_B2 ** ADAM_STEP)
    return -ADAM_LR * (m_hat / (jnp.sqrt(v_hat) + ADAM_EPS) + ADAM_WD * w), m, v


class _LocalPlan:
    def __init__(self, p):
        self.p, self.large = p, {}

    def weight(self, name):
        return self.p[name]

    def side(self, kernel_name):
        return None

    def done(self, kernel_name, side_outs):
        pass

    def grad(self, name, g):
        self.large[name] = g


def _local_step(x, tgt, p, plan):
    L, D = x.shape

    def carry(fn, *args, name, **kw):
        side = plan.side(name)
        if side is None:
            return fn(*args, name=name, **kw)
        outs, side_outs = fn(*args, name=name, side=side, **kw)
        plan.done(name, side_outs)
        return outs

    d_ssd, d_xbc, d_lru, d_mix, d_ff = 2048, 4096, 2048, 4096, 8192
    n_main = d_ssd + d_xbc + 2 * d_lru
    c_xbc, c_gate, c_xl = d_ssd, d_ssd + d_xbc, d_ssd + d_xbc + d_lru
    TR, SUB = 256, 16
    mm = dict(tm=1024, tn=1024, tk=1024)

    (h,) = _rowwise(lambda xt, g: (_rms(xt, g),), [x], [p["pre_mix_norm"]], [(D, BF16)], [], name="f_prenorm", tr=TR, sub=SUB)
    w_main, w_dt = plan.weight("w_main"), plan.weight("w_dt")
    (proj,) = carry(_matmul, h, w_main, mode="nn", m=L, n=n_main, k=D, out_dtypes=[F32], name="f_inproj", **mm)
    (dtraw,) = _matmul(h, w_dt, mode="nn", m=L, n=128, k=D, out_dtypes=[F32], name="f_dtproj", **mm)
    pre, act = carry(_conv_fwd, proj, c_xbc, d_xbc, p["ssd_conv_w"], p["ssd_conv_b"], silu=True, name="f_ssdconv")
    (xl,) = carry(_conv_fwd, proj, c_xl, d_lru, p["lru_conv_w"], p["lru_conv_b"], silu=False, name="f_lruconv")
    yraw, hst = _ssd_fwd(act, dtraw, p["dtb"], p["alog"], p["drow"], name="f_ssd")
    (yssd,) = _grouped(lambda y, z, n: (_ssd_gate(y, z, n),), [yraw, (proj, 0)], [p["ssd_norm"]], [BF16], [],
                       gw=GROUP_W, name="f_ssdgate", tr=TR)
    gate_p = [p["lru_w_a"], p["lru_b_a"], p["lru_w_x"], p["lru_b_x"], p["lru_lambda"]]
    a, u = _grouped(_gate_head, [xl], gate_p, [F32, F32], [], gw=LRU_BLOCK, name="f_lrugates", tr=TR)
    hl, hprev = _lru_scan_fwd(a, u, name="f_lruscan")
    (ylru,) = _rowwise(lambda ht, gt, n: (_lru_out(ht, gt, n),), [hl, (proj, c_gate, d_lru)], [p["lru_norm"]],
                       [(d_lru, BF16)], [], name="f_lruout", tr=TR, sub=SUB)
    ycat = jnp.concatenate([yssd, ylru], axis=1)
    w_out = plan.weight("w_out")
    (mix,) = _matmul(ycat, w_out, mode="nn", m=L, n=D, k=d_mix, out_dtypes=[F32], name="f_outproj", **mm)
    x1, h2 = _rowwise(_mid, [x, mix], [p["post_mix_norm"], p["pre_mlp_norm"]], [(D, F32), (D, BF16)], [],
                      name="f_mid", tr=TR, sub=SUB)
    nb_mi = (d_ff // 4) // mm["tn"]
    w_mi = plan.weight("w_mi")
    hm, act2 = _matmul(h2, w_mi, mode="nn", m=L, n=d_ff, k=D, out_dtypes=[BF16, BF16], name="f_mlpin",
                       b_spec=pl.BlockSpec((None, mm["tk"], mm["tn"]), lambda i, j, kk: (j // nb_mi, kk, j % nb_mi)),
                       epilogue=lambda r: (r, jnp.square(jnp.maximum(r, 0.0))), **mm)
    w_mo = plan.weight("w_mo")
    (hm2,) = _matmul(act2, w_mo, mode="nn", m=L, n=D, k=d_ff, out_dtypes=[F32], name="f_mlpout", **mm)

    dhm2, dx1p, d_post_mlp, loss = _rowwise(_loss_bwd, [hm2, x1, tgt], [p["post_mlp_norm"]], [(D, BF16), (D, F32)],
                                            [(1, D), (1, 128)], name="b_loss", tr=TR, sub=SUB)
    (dhm,) = _matmul(dhm2, w_mo, mode="nt", m=L, n=d_ff, k=D, out_dtypes=[BF16], name="b_mlpout_dx", extras=[hm],
                     epilogue=lambda r, hmv: (r * (2.0 * jnp.maximum(hmv.astype(F32), 0.0)),), **mm)
    (dw_mo,) = _matmul(act2, dhm2, mode="tn", m=d_ff, n=D, k=L, out_dtypes=[BF16], name="b_mlpout_dw", **mm)
    plan.grad("w_mlp_out", dw_mo.reshape(4, -1, D))
    kb_mi = (d_ff // 4) // mm["tk"]
    (dh2,) = carry(_matmul, dhm, w_mi, mode="nt", m=L, n=D, k=d_ff, out_dtypes=[F32], name="b_mlpin_dx",
                   b_spec=pl.BlockSpec((None, mm["tn"], mm["tk"]), lambda i, j, kk: (kk // kb_mi, j, kk % kb_mi)), **mm)
    (dw_mi,) = _matmul(h2, dhm, mode="tn", m=D, n=d_ff, k=L, out_dtypes=[BF16], name="b_mlpin_dw",
                       out_specs=[pl.BlockSpec((None, mm["tm"], mm["tn"]), lambda i, j, kk: (j // nb_mi, i, j % nb_mi))],
                       out_shapes=[jax.ShapeDtypeStruct((4, D, d_ff // 4), BF16)], **mm)
    plan.grad("w_mlp_in", dw_mi)
    dmix, dx1, d_post_mix, d_pre_mlp = _rowwise(_mid_bwd, [x, mix, dx1p, dh2], [p["post_mix_norm"], p["pre_mlp_norm"]],
                                                [(D, BF16), (D, F32)], [(1, D), (1, D)], name="b_mid", tr=TR, sub=SUB)
    (dw_out,) = _matmul(ycat, dmix, mode="tn", m=d_mix, n=D, k=L, out_dtypes=[BF16], name="b_outproj_dw", **mm)
    plan.grad("w_out", dw_out.reshape(4, -1, D))
    (dycat,) = _matmul(dmix, w_out, mode="nt", m=L, n=d_mix, k=D, out_dtypes=[F32], name="b_outproj_dx", **mm)
    dhl, dgate, d_lru_norm = _rowwise(_lru_out_bwd, [hl, (proj, c_gate, d_lru), (dycat, d_ssd, d_lru)], [p["lru_norm"]],
                                      [(d_lru, F32), (d_lru, BF16)], [(1, d_lru)], name="b_lruout", tr=TR, sub=SUB)
    da, du = _lru_scan_bwd(a, hprev, dhl, name="b_lruscan")
    dxl, d_wa, d_ba, d_wx, d_bx, d_lam = _grouped(
        _gate_head_bwd, [xl, da, du], gate_p, [F32],
        [(LRU_HEADS, LRU_BLOCK, LRU_BLOCK), (1, d_lru), (LRU_HEADS, LRU_BLOCK, LRU_BLOCK), (1, d_lru), (1, d_lru)],
        gw=LRU_BLOCK, name="b_lrugates", tr=TR)
    dxlru, dwb_lru = _conv_bwd(dxl, 0, None, proj, c_xl, d_lru, p["lru_conv_w"], silu=False, name="b_lruconv")
    dyraw, dz, d_ssd_norm = _grouped(_ssd_gate_bwd, [yraw, (proj, 0), (dycat, 0)], [p["ssd_norm"]], [F32, BF16], [(1, d_ssd)],
                                     gw=GROUP_W, name="b_ssdgate", tr=TR)
    dact, ddtraw, dpar = carry(_ssd_bwd, act, dtraw, dyraw, hst, p["dtb"], p["alog"], p["drow"], name="b_ssd")
    dxbc, dwb_ssd = _conv_bwd(dact, 0, pre, proj, c_xbc, d_xbc, p["ssd_conv_w"], silu=True, name="b_ssdconv")
    dproj = jnp.concatenate([dz, dxbc, dgate, dxlru], axis=1)
    (dw_main,) = _matmul(h, dproj, mode="tn", m=D, n=n_main, k=L, out_dtypes=[BF16], name="b_inproj_dw", **mm)
    (dw_dt,) = _matmul(h, ddtraw, mode="tn", m=D, n=128, k=L, out_dtypes=[BF16], name="b_dtproj_dw", **mm)
    plan.grad("w_in", (dw_main, dw_dt))
    (dh_a,) = carry(_matmul, dproj, w_main, mode="nt", m=L, n=D, k=n_main, out_dtypes=[F32], name="b_inproj_dx", **mm)
    (dh_b,) = _matmul(ddtraw, w_dt, mode="nt", m=L, n=D, k=128, out_dtypes=[F32], name="b_dtproj_dx", **mm)
    grad_x, d_pre_mix = _rowwise(_in_bwd, [x, dh_a, dh_b, dx1], [p["pre_mix_norm"]], [(D, F32)], [(1, D)],
                                 name="b_prenorm", tr=TR, sub=SUB)

    small = dict(loss=loss, pre_mix_norm=d_pre_mix, ssd_conv=dwb_ssd, ssd_par=dpar, ssd_norm=d_ssd_norm, lru_conv=dwb_lru,
                 lru_w_a=d_wa, lru_b_a=d_ba, lru_w_x=d_wx, lru_b_x=d_bx, lru_lambda=d_lam, lru_norm=d_lru_norm,
                 post_mix_norm=d_post_mix, pre_mlp_norm=d_pre_mlp, post_mlp_norm=d_post_mlp)
    return grad_x, small


def _place():
    return lax.axis_index("x"), lax.axis_index("y"), lax.axis_index("c")


def _other_chips(x, y):
    return [(1 - x, y), (x, 1 - y), (1 - x, 1 - y)]


def _allgather8(blk, *, name):
    r, n = blk.shape

    def body(x_ref, out_ref, send_sems, recv_sems, local_sem):
        x, y, c = _place()
        me, sibling = (x, y, c), (x, y, 1 - c)
        chips = _other_chips(x, y)

        def rows(px, py, pc):
            return out_ref.at[pl.ds((4 * px + 2 * py + pc) * r, r), :]

        def copy(k, block, to, src=None):
            return pltpu.make_async_remote_copy(
                src_ref=rows(*block) if src is None else src, dst_ref=rows(*block),
                send_sem=send_sems.at[k], recv_sem=recv_sems.at[k], device_id=to, device_id_type=MESH)

        mine = pltpu.make_async_copy(x_ref, rows(*me), local_sem)
        mine.start()
        first = [copy(0, me, sibling, src=x_ref)]
        first += [copy(1 + k, me, (*chip, c), src=x_ref) for k, chip in enumerate(chips)]
        for cp in first:
            cp.start()
        passed = [copy(4 + k, (*chip, c), sibling) for k, chip in enumerate(chips)]
        for k, chip in enumerate(chips):
            copy(1 + k, (*chip, c), me).wait_recv()
            passed[k].start()
        copy(0, sibling, me).wait_recv()
        for k, chip in enumerate(chips):
            copy(4 + k, (*chip, 1 - c), me).wait_recv()
        for cp in first + passed:
            cp.wait_send()
        mine.wait()

    return pl.pallas_call(
        body, name=name,
        out_shape=jax.ShapeDtypeStruct((8 * r, n), blk.dtype),
        in_specs=[pl.BlockSpec(memory_space=pltpu.VMEM)], out_specs=pl.BlockSpec(memory_space=pltpu.VMEM),
        scratch_shapes=[pltpu.SemaphoreType.DMA((7,)), pltpu.SemaphoreType.DMA((7,)), pltpu.SemaphoreType.DMA],
        compiler_params=pltpu.CompilerParams(vmem_limit_bytes=VMEM_LIMIT),
    )(blk)


def _sum8(g, *, name, tr=232):
    _, r, n = g.shape
    tr = min(tr, r)
    assert r % tr == 0

    def body(g_ref, o_ref):
        s = g_ref[0]
        for k in range(1, 8):
            s = s + g_ref[k]
        o_ref[...] = s

    return pl.pallas_call(
        body, name=name, grid=(r // tr,),
        in_specs=[pl.BlockSpec((8, tr, n), lambda i: (0, i, 0))], out_specs=pl.BlockSpec((tr, n), lambda i: (i, 0)),
        out_shape=jax.ShapeDtypeStruct((r, n), g.dtype), compiler_params=_params(("parallel",)),
    )(g)


def _blocks(fn, ins, outs, *, grid, name, prefetch=None, aliases=None):
    n_in = len(ins)

    def body(*refs):
        if prefetch is not None:
            refs = refs[1:]
        res = fn(*[r[...] for r in refs[:n_in]])
        for o_ref, o in zip(refs[n_in:], res):
            o_ref[...] = o.astype(o_ref.dtype)

    in_specs = [pl.BlockSpec(b, m) for _, b, m in ins]
    out_specs = [pl.BlockSpec(b, m) for _, b, m in outs]
    kw = dict(name=name, out_shape=[s for s, _, _ in outs], input_output_aliases=aliases or {},
              compiler_params=_params(("arbitrary",) * len(grid)))
    arrs = [a for a, _, _ in ins]
    if prefetch is None:
        return pl.pallas_call(body, grid=grid, in_specs=in_specs, out_specs=out_specs, **kw)(*arrs)
    spec = pltpu.PrefetchScalarGridSpec(num_scalar_prefetch=1, grid=grid, in_specs=in_specs, out_specs=out_specs)
    return pl.pallas_call(body, grid_spec=spec, **kw)(prefetch, *arrs)


def _gather_weights(slots, *, name):
    n = len(slots)

    def body(*refs):
        i_refs, g_refs = refs[:n], refs[n:2 * n]
        send_sems, recv_sems = refs[2 * n:]
        x, y, c = _place()
        j, sibling = 2 * x + y, (x, y, 1 - c)
        chips = _other_chips(x, y)

        def cp(w, k, src, dst, to):
            return pltpu.make_async_remote_copy(src_ref=src, dst_ref=dst, send_sem=send_sems.at[6 * w + k],
                                                recv_sem=recv_sems.at[6 * w + k], device_id=to, device_id_type=MESH)

        def part(ref, chip_idx, hc):
            rh = ref.shape[1] // 2
            return ref.at[chip_idx, pl.ds(hc * rh, rh), :]

        sends = []
        for w in range(n):
            for k, chip in enumerate(chips):
                d = cp(w, k, part(i_refs[w], j, c), part(g_refs[w], j, c), (*chip, c))
                d.start()
                sends.append(d)
        for w in range(n):
            for k, (px, py) in enumerate(chips):
                landed = part(g_refs[w], 2 * px + py, c)
                cp(w, k, landed, landed, (px, py, c)).wait_recv()
                d = cp(w, 3 + k, landed, landed, sibling)
                d.start()
                sends.append(d)
        for w in range(n):
            for k, (px, py) in enumerate(chips):
                other = part(g_refs[w], 2 * px + py, 1 - c)
                cp(w, 3 + k, other, other, sibling).wait_recv()
        for d in sends:
            d.wait_send()

    return pl.pallas_call(
        body, name=name,
        out_shape=[jax.ShapeDtypeStruct(s.shape, s.dtype) for s in slots],
        in_specs=[ANY] * n, out_specs=[ANY] * n, input_output_aliases={w: w for w in range(n)},
        scratch_shapes=[pltpu.SemaphoreType.DMA((6 * n,)), pltpu.SemaphoreType.DMA((6 * n,))],
    )(*slots)


def _rs_sibling(gs, *, name):
    n = len(gs)

    def body(*refs):
        g_refs, got_refs = refs[:n], refs[n:2 * n]
        send_sems, recv_sems = refs[2 * n:]
        x, y, c = _place()
        copies = []
        for w in range(n):
            for s in range(4):
                d = pltpu.make_async_remote_copy(src_ref=g_refs[w].at[s, 1 - c], dst_ref=got_refs[w].at[s],
                                                 send_sem=send_sems.at[4 * w + s], recv_sem=recv_sems.at[4 * w + s],
                                                 device_id=(x, y, 1 - c), device_id_type=MESH)
                d.start()
                copies.append(d)
        for d in copies:
            d.wait_recv()
        for d in copies:
            d.wait_send()

    return pl.pallas_call(
        body, name=name, out_shape=[jax.ShapeDtypeStruct((4,) + g.shape[2:], g.dtype) for g in gs],
        in_specs=[ANY] * n, out_specs=[ANY] * n,
        scratch_shapes=[pltpu.SemaphoreType.DMA((4 * n,)), pltpu.SemaphoreType.DMA((4 * n,))],
    )(*gs)


def _share_halves(ts, *, name):
    n = len(ts)

    def body(*refs):
        t_refs, o_refs = refs[:n], refs[n:2 * n]
        send_sems, recv_sems = refs[2 * n:]
        x, y, c = _place()

        def cp(w, src, dst):
            return pltpu.make_async_remote_copy(src_ref=src, dst_ref=dst, send_sem=send_sems.at[w], recv_sem=recv_sems.at[w],
                                                device_id=(x, y, 1 - c), device_id_type=MESH)

        copies = [cp(w, t_refs[w].at[c], o_refs[w].at[c]) for w in range(n)]
        for d in copies:
            d.start()
        for w in range(n):
            cp(w, t_refs[w].at[c], o_refs[w].at[1 - c]).wait_recv()
        for d in copies:
            d.wait_send()

    return pl.pallas_call(
        body, name=name, out_shape=[jax.ShapeDtypeStruct(t.shape, t.dtype) for t in ts],
        in_specs=[ANY] * n, out_specs=[ANY] * n, input_output_aliases={w: w for w in range(n)},
        scratch_shapes=[pltpu.SemaphoreType.DMA((n,)), pltpu.SemaphoreType.DMA((n,))],
    )(*ts)


def _pack(arrs):
    parts = []
    for v in arrs:
        f = v.reshape(-1)
        f = jnp.pad(f, (0, (-f.shape[0]) % 1024))
        parts.append(f.reshape(-1, 128))
    return jnp.concatenate(parts, axis=0)


def _unpack(packed, shapes):
    out, r0 = [], 0
    for s in shapes:
        size = 1
        for d in s:
            size *= d
        nr = (size + 1023) // 1024 * 8
        out.append(packed[r0:r0 + nr].reshape(-1)[:size].reshape(s))
        r0 += nr
    return out


SMALL_GRADS = [("loss", (1, 128)), ("pre_mix_norm", (1, 2048)), ("ssd_conv", (8, 4096)), ("ssd_par", (8, 128)),
               ("ssd_norm", (1, 2048)), ("lru_conv", (8, 2048)), ("lru_w_a", (16, 128, 128)), ("lru_b_a", (1, 2048)),
               ("lru_w_x", (16, 128, 128)), ("lru_b_x", (1, 2048)), ("lru_lambda", (1, 2048)), ("lru_norm", (1, 2048)),
               ("post_mix_norm", (1, 2048)), ("pre_mlp_norm", (1, 2048)), ("post_mlp_norm", (1, 2048))]

WEIGHTS = ['pre_mix_norm', 'w_in', 'ssd_conv_w', 'ssd_conv_b', 'ssd_dt_bias', 'ssd_a_log', 'ssd_d', 'ssd_norm', 'lru_conv_w',
           'lru_conv_b', 'lru_w_a', 'lru_b_a', 'lru_w_x', 'lru_b_x', 'lru_lambda', 'lru_norm', 'w_out', 'post_mix_norm',
           'pre_mlp_norm', 'w_mlp_in', 'w_mlp_out', 'post_mlp_norm']
LARGE = ['w_in', 'w_out', 'w_mlp_in', 'w_mlp_out']

D_SSD, D_XBC, DT_W = 2048, 4096, 32
TB = 256


def _half(ref, chip_idx, hc):
    rh = ref.shape[1] // 2
    return ref.at[chip_idx, pl.ds(hc * rh, rh), :]


def _job_gather_ici(i_ref, g_ref, send_sems, recv_sems, base):
    x, y, c = _place()
    j = 2 * x + y
    sends, recvs = [], []
    for k, (px, py) in enumerate(_other_chips(x, y)):
        kw = dict(send_sem=send_sems.at[base + k], recv_sem=recv_sems.at[base + k], device_id=(px, py, c), device_id_type=MESH)
        sends.append(pltpu.make_async_remote_copy(src_ref=_half(i_ref, j, c), dst_ref=_half(g_ref, j, c), **kw))
        landed = _half(g_ref, 2 * px + py, c)
        recvs.append(pltpu.make_async_remote_copy(src_ref=landed, dst_ref=landed, **kw))
    return sends, recvs


def _job_gather_sibling(i_ref, g_ref, send_sems, recv_sems, base):
    x, y, c = _place()
    sends, recvs = [], []
    for k, (px, py) in enumerate(_other_chips(x, y)):
        kw = dict(send_sem=send_sems.at[base + k], recv_sem=recv_sems.at[base + k], device_id=(x, y, 1 - c), device_id_type=MESH)
        sends.append(pltpu.make_async_remote_copy(src_ref=_half(i_ref, 2 * px + py, c), dst_ref=_half(g_ref, 2 * px + py, c), **kw))
        other = _half(g_ref, 2 * px + py, 1 - c)
        recvs.append(pltpu.make_async_remote_copy(src_ref=other, dst_ref=other, **kw))
    return sends, recvs


def _job_reduce_ici(s_ref, got_refs, send_sems, recv_sems, base):
    x, y, c = _place()
    sends = [pltpu.make_async_remote_copy(src_ref=s_ref.at[2 * px + py], dst_ref=got_refs[k], send_sem=send_sems.at[base + k],
                                          recv_sem=recv_sems.at[base + k], device_id=(px, py, c), device_id_type=MESH)
             for k, (px, py) in enumerate(_other_chips(x, y))]
    return sends, sends


class _DistPlan:
    def __init__(self, slots, w_main, w_dt, where, shapes):
        self.slots, self.w_main, self.w_dt, self.where, self.shapes = slots, w_main, w_dt, where, shapes
        self.pending, self.totals = [], {}

    def weight(self, name):
        d = self.w_main.shape[0]
        return {"w_main": lambda: self.w_main, "w_dt": lambda: self.w_dt, "w_out": lambda: self.slots["w_out"].reshape(-1, d),
                "w_mi": lambda: self.slots["w_mlp_in"], "w_mo": lambda: self.slots["w_mlp_out"].reshape(-1, d)}[name]()

    def _slot_side(self, jobs):
        names = []
        for n, _ in jobs:
            if n not in names:
                names.append(n)
        arrs = [self.slots[n] for n in names]

        def make(i_refs, o_refs, send_sems, recv_sems):
            sends, recvs = [], []
            for q, (n, job) in enumerate(jobs):
                s, r = job(i_refs[names.index(n)], o_refs[names.index(n)], send_sems, recv_sems, 3 * q)
                sends, recvs = sends + s, recvs + r
            return sends, recvs

        side = _Side(arrs, [jax.ShapeDtypeStruct(a.shape, a.dtype) for a in arrs], {i: i for i in range(len(arrs))}, 3 * len(jobs), make)
        return side, names

    def side(self, kernel_name):
        if kernel_name == "f_inproj":
            side, self._names = self._slot_side([("w_out", _job_gather_ici), ("w_mlp_in", _job_gather_ici)])
        elif kernel_name == "f_ssdconv":
            side, self._names = self._slot_side([("w_mlp_out", _job_gather_ici), ("w_out", _job_gather_sibling),
                                                 ("w_mlp_in", _job_gather_sibling)])
        elif kernel_name == "f_lruconv":
            side, self._names = self._slot_side([("w_mlp_out", _job_gather_sibling)])
        elif kernel_name in ("b_mlpin_dx", "b_ssd", "b_inproj_dx") and self.pending:
            sent = self._sent = list(self.pending)
            self.pending = []

            def make(i_refs, o_refs, send_sems, recv_sems):
                sends = []
                for q in range(len(sent)):
                    sends += _job_reduce_ici(i_refs[q], o_refs[3 * q:3 * q + 3], send_sems, recv_sems, 3 * q)[0]
                return sends, sends

            outs = [jax.ShapeDtypeStruct(s.shape[1:], s.dtype) for _, s in sent for _ in range(3)]
            side = _Side([s for _, s in sent], outs, {}, 3 * len(sent), make)
        else:
            return None
        return side

    def done(self, kernel_name, side_outs):
        if kernel_name.startswith("f_"):
            self.slots.update(zip(self._names, side_outs))
            return
        for q, (name, s4) in enumerate(self._sent):
            g3 = side_outs[3 * q:3 * q + 3]
            _, rh, cc = s4.shape
            row = ((TB, cc), lambda i, s: (i, 0))
            self.totals[name] = _blocks(
                lambda o, r0, r1, r2: (((o.astype(F32) + r0.astype(F32)) + r1.astype(F32)) + r2.astype(F32),),
                [(s4, (None, TB, cc), lambda i, s: (s[0], i, 0)), (g3[0],) + row, (g3[1],) + row, (g3[2],) + row],
                [(jax.ShapeDtypeStruct((2, rh, cc), F32), (None, TB, cc), lambda i, s: (s[1], i, 0))],
                grid=(rh // TB,), name="add_chips_" + name, prefetch=self.where)[0]

    def grad(self, name, g):
        if name == "w_in":
            dw_main, dw_dt = g
            c_dt = D_SSD + D_XBC
            g = jnp.concatenate([dw_main[:, :c_dt], dw_dt[:, :DT_W], dw_main[:, c_dt:]], axis=1)
            g = g.reshape(g.shape[0], 4, -1).transpose(1, 0, 2)
        g = g.reshape(4, 2, g.shape[1] // 2, g.shape[2])
        (t,) = _rs_sibling([g], name="reduce_sibling_" + name)
        _, rh, cc = t.shape
        s4 = _blocks(lambda u, v: (u.astype(F32) + v.astype(F32),),
                     [(g, (None, None, TB, cc), lambda q, i, s: (q, s[1], i, 0)), (t, (None, TB, cc), lambda q, i, s: (q, i, 0))],
                     [(jax.ShapeDtypeStruct(t.shape, BF16), (None, TB, cc), lambda q, i, s: (q, i, 0))],
                     grid=(4, rh // TB), name="add_sibling_" + name, prefetch=self.where)[0]
        self.pending.append((name, s4))

    def finish(self):
        assert not self.pending
        names = list(self.totals)
        outs = _share_halves([self.totals[n] for n in names], name="share_halves")
        return {n: o.reshape(self.shapes[n]) for n, o in zip(names, outs)}


def kernel(x, pre_mix_norm, w_in, ssd_conv_w, ssd_conv_b, ssd_dt_bias, ssd_a_log, ssd_d, ssd_norm, lru_conv_w, lru_conv_b, lru_w_a, lru_b_a, lru_w_x, lru_b_x, lru_lambda, lru_norm, w_out, post_mix_norm, pre_mlp_norm, w_mlp_in, w_mlp_out, post_mlp_norm, loss_target, m_pre_mix_norm, m_w_in, m_ssd_conv_w, m_ssd_conv_b, m_ssd_dt_bias, m_ssd_a_log, m_ssd_d, m_ssd_norm, m_lru_conv_w, m_lru_conv_b, m_lru_w_a, m_lru_b_a, m_lru_w_x, m_lru_b_x, m_lru_lambda, m_lru_norm, m_w_out, m_post_mix_norm, m_pre_mlp_norm, m_w_mlp_in, m_w_mlp_out, m_post_mlp_norm, v_pre_mix_norm, v_w_in, v_ssd_conv_w, v_ssd_conv_b, v_ssd_dt_bias, v_ssd_a_log, v_ssd_d, v_ssd_norm, v_lru_conv_w, v_lru_conv_b, v_lru_w_a, v_lru_b_a, v_lru_w_x, v_lru_b_x, v_lru_lambda, v_lru_norm, v_w_out, v_post_mix_norm, v_pre_mlp_norm, v_w_mlp_in, v_w_mlp_out, v_post_mlp_norm):
    a = dict(locals())
    j = 2 * lax.axis_index("x") + lax.axis_index("y")
    D = x.shape[-1]
    c_dt = D_SSD + D_XBC

    where = jnp.stack([j, lax.axis_index("c")]).astype(jnp.int32)
    slots = {}
    for name in LARGE:
        w = a[name][0]
        r, cc = w.shape
        slots[name] = _blocks(lambda t: (t,), [(w, (TB, cc), lambda i, s: (i, 0))],
                              [(jax.ShapeDtypeStruct((4, r, cc), BF16), (None, TB, cc), lambda i, s: (s[0], i, 0))],
                              grid=(r // TB,), name="cast_" + name, prefetch=where)[0]
    (g_in,) = _gather_weights([slots.pop("w_in")], name="gather_w_in")
    w_full = jnp.transpose(g_in, (1, 0, 2)).reshape(D, -1)
    w_main = jnp.concatenate([w_full[:, :c_dt], w_full[:, c_dt + DT_W:]], axis=1)
    w_dt = jnp.pad(w_full[:, c_dt:c_dt + DT_W], ((0, 0), (0, 128 - DT_W)))
    taps = jnp.concatenate([ssd_conv_w[0].reshape(-1, 128), lru_conv_w[0].reshape(-1, 128)], axis=0)
    taps = _allgather8(taps, name="gather_taps").reshape(8, taps.shape[0], 128)[0::2]
    n_ssd = ssd_conv_w.shape[1] * ssd_conv_w.shape[2] // 128
    ssd_taps = taps[:, :n_ssd].reshape(4, CONV_W, -1).transpose(1, 0, 2).reshape(CONV_W, -1)
    lru_taps = taps[:, n_ssd:].reshape(4, CONV_W, -1).transpose(1, 0, 2).reshape(CONV_W, -1)

    def row128(v):
        return jnp.pad(v, ((0, 0), (0, 128 - v.shape[1])))

    p = dict(pre_mix_norm=pre_mix_norm, ssd_conv_w=ssd_taps, ssd_conv_b=ssd_conv_b,
             dtb=row128(ssd_dt_bias), alog=row128(ssd_a_log), drow=row128(ssd_d), ssd_norm=ssd_norm,
             lru_conv_w=lru_taps, lru_conv_b=lru_conv_b, lru_w_a=lru_w_a[0], lru_b_a=lru_b_a.reshape(1, -1),
             lru_w_x=lru_w_x[0], lru_b_x=lru_b_x.reshape(1, -1), lru_lambda=lru_lambda, lru_norm=lru_norm,
             post_mix_norm=post_mix_norm, pre_mlp_norm=pre_mlp_norm, post_mlp_norm=post_mlp_norm)
    plan = _DistPlan(slots, w_main, w_dt, where, {n: a[n].shape for n in LARGE})
    grad_x, small = _local_step(x[0], loss_target[0], p, plan)

    packed = _pack([small[n] for n, _ in SMALL_GRADS])
    total = _sum8(_allgather8(packed, name="gather_small").reshape(8, packed.shape[0], 128), name="sum_small")
    tot = dict(zip([n for n, _ in SMALL_GRADS], _unpack(total, [s for _, s in SMALL_GRADS])))
    loss = tot["loss"][0, 0]
    n_sc, n_lc = ssd_conv_w.shape[2], lru_conv_w.shape[2]
    grads = dict(
        pre_mix_norm=tot["pre_mix_norm"],
        ssd_conv_w=lax.dynamic_slice(tot["ssd_conv"][:CONV_W], (0, j * n_sc), (CONV_W, n_sc))[None],
        ssd_conv_b=tot["ssd_conv"][CONV_W:CONV_W + 1],
        ssd_dt_bias=tot["ssd_par"][0:1, :DT_W], ssd_a_log=tot["ssd_par"][1:2, :DT_W], ssd_d=tot["ssd_par"][2:3, :DT_W],
        ssd_norm=tot["ssd_norm"],
        lru_conv_w=lax.dynamic_slice(tot["lru_conv"][:CONV_W], (0, j * n_lc), (CONV_W, n_lc))[None],
        lru_conv_b=tot["lru_conv"][CONV_W:CONV_W + 1],
        lru_w_a=tot["lru_w_a"][None], lru_b_a=tot["lru_b_a"].reshape(lru_b_a.shape),
        lru_w_x=tot["lru_w_x"][None], lru_b_x=tot["lru_b_x"].reshape(lru_b_x.shape),
        lru_lambda=tot["lru_lambda"], lru_norm=tot["lru_norm"], post_mix_norm=tot["post_mix_norm"],
        pre_mlp_norm=tot["pre_mlp_norm"], post_mlp_norm=tot["post_mlp_norm"])

    grads.update(plan.finish())

    delta, new_m, new_v = {}, {}, {}
    for name in LARGE:
        w = a[name][0]
        cc = w.shape[1]
        outs = _rowwise(_adamw, [w, grads[name][0], a["m_" + name][0], a["v_" + name][0]], [], [(cc, F32)] * 3, [],
                        name="adamw_" + name, tr=128, sub=8)
        delta[name], new_m[name], new_v[name] = [o[None] for o in outs]
    small_w = [n for n in WEIGHTS if n not in LARGE]
    packs = [_pack([d[n] for n in small_w]) for d in (a, grads, {n: a["m_" + n] for n in small_w}, {n: a["v_" + n] for n in small_w})]
    outs = _rowwise(_adamw, packs, [], [(128, F32)] * 3, [], name="adamw_small", tr=packs[0].shape[0], sub=8)
    for d, o in zip((delta, new_m, new_v), outs):
        d.update(zip(small_w, _unpack(o, [a[n].shape for n in small_w])))

    return (loss, grad_x[None], *[grads[n] for n in WEIGHTS], *[delta[n] for n in WEIGHTS],
            *[new_m[n] for n in WEIGHTS], *[new_v[n] for n in WEIGHTS])
```

```python
import functools

import jax
import jax.numpy as jnp
from jax import lax
from jax.experimental import pallas as pl
from jax.experimental.pallas import tpu as pltpu

F32 = jnp.float32
BF16 = jnp.bfloat16
MESH = pl.DeviceIdType.MESH

EPS = 1e-6
LRU_C = 8.0
ADAM_LR = 0.001
ADAM_B1 = 0.9
ADAM_B2 = 0.999
ADAM_EPS = 1e-08
ADAM_WD = 0.01
ADAM_STEP = 10

N_GROUPS = 8
HEADS_PER_GROUP = 4
HEAD_DIM = 64
GROUP_W = HEADS_PER_GROUP * HEAD_DIM
STATE = 128
LRU_HEADS = 16
LRU_BLOCK = 128
CONV_W = 4
SSD_CHUNK = 128
HALO = 8

VMEM_LIMIT = 48 * 1024 * 1024


def _params(sem=None):
    return pltpu.CompilerParams(dimension_semantics=sem, vmem_limit_bytes=VMEM_LIMIT)


@jax.custom_jvp
def _log1p(x):
    u = 1.0 + x
    d = u - 1.0
    return jnp.where(d == 0.0, x, jnp.log(u) * (x / jnp.where(d == 0.0, 1.0, d)))


@_log1p.defjvp
def _log1p_jvp(primals, tangents):
    (x,), (t,) = primals, tangents
    return _log1p(x), t / (1.0 + x)


@jax.custom_jvp
def _expm1(x):
    u = jnp.exp(x)
    lu = jnp.log(u)
    safe = jnp.where(lu == 0.0, 1.0, lu)
    y = (u - 1.0) * (x / safe)
    y = jnp.where(lu == 0.0, x, y)
    return jnp.where(u == 0.0, -1.0, y)


@_expm1.defjvp
def _expm1_jvp(primals, tangents):
    (x,), (t,) = primals, tangents
    return _expm1(x), t * jnp.exp(x)


def _softplus(x):
    return jnp.maximum(x, 0.0) + _log1p(jnp.exp(-jnp.abs(x)))


def _sigmoid(x):
    return 1.0 / (1.0 + jnp.exp(-x))


def _silu(x):
    return x * _sigmoid(x)


def _gelu(x):
    c = 0.7978845608028654
    return 0.5 * x * (1.0 + jnp.tanh(c * (x + 0.044715 * (x * x * x))))


def _rms(x, g):
    return x * lax.rsqrt(jnp.mean(x * x, axis=-1, keepdims=True) + EPS) * g


def _dot(a, b, dims):
    return lax.dot_general(a.astype(BF16), b.astype(BF16), (dims, ((), ())), preferred_element_type=F32)


_NN = ((1,), (0,))
_NT = ((1,), (1,))
_TN = ((0,), (0,))


ANY = pl.BlockSpec(memory_space=pl.ANY)


class _Side:
    def __init__(self, ins, outs, aliases, n_sems, make):
        self.ins, self.outs, self.aliases, self.n_sems, self.make = ins, outs, aliases, n_sems, make


def _call(body, args, *, name, grid, in_specs, out_specs, out_shape, scratch_shapes=(), sem=None, side=None):
    in_specs, out_specs, out_shape, scratch_shapes = list(in_specs), list(out_specs), list(out_shape), list(scratch_shapes)
    if side is None:
        outs = pl.pallas_call(body, name=name, grid=grid, in_specs=in_specs, out_specs=out_specs, out_shape=out_shape,
                              scratch_shapes=scratch_shapes, compiler_params=_params(sem))(*args)
        return list(outs), []
    n_in, n_out, n_scr, si, so = len(in_specs), len(out_specs), len(scratch_shapes), len(side.ins), len(side.outs)

    def full(*refs):
        s_in = refs[n_in:n_in + si]
        o0 = n_in + si
        s_out = refs[o0 + n_out:o0 + n_out + so]
        scr = refs[o0 + n_out + so:o0 + n_out + so + n_scr]
        send_sems, recv_sems = refs[-2], refs[-1]
        ids = [pl.program_id(d) for d in range(len(grid))]
        first = functools.reduce(jnp.logical_and, [i == 0 for i in ids])
        last = functools.reduce(jnp.logical_and, [i == g - 1 for i, g in zip(ids, grid)])

        @pl.when(first)
        def _():
            for d in side.make(s_in, s_out, send_sems, recv_sems)[0]:
                d.start()

        body(*refs[:n_in], *refs[o0:o0 + n_out], *scr)

        @pl.when(last)
        def _():
            sends, recvs = side.make(s_in, s_out, send_sems, recv_sems)
            for d in recvs:
                d.wait_recv()
            for d in sends:
                d.wait_send()

    outs = pl.pallas_call(
        full, name=name, grid=grid, in_specs=in_specs + [ANY] * si, out_specs=out_specs + [ANY] * so,
        out_shape=out_shape + list(side.outs),
        scratch_shapes=scratch_shapes + [pltpu.SemaphoreType.DMA((side.n_sems,)), pltpu.SemaphoreType.DMA((side.n_sems,))],
        input_output_aliases={n_in + i: n_out + o for i, o in side.aliases.items()},
        compiler_params=_params(("arbitrary",) * len(grid)),
    )(*args, *side.ins)
    return list(outs[:n_out]), list(outs[n_out:])


def _matmul(a, b, *, mode, m, n, k, tm, tn, tk, out_dtypes, name, a_spec=None, b_spec=None,
            out_specs=None, out_shapes=None, extras=(), epilogue=None, side=None):
    tm, tn, tk = min(tm, m), min(tn, n), min(tk, k)
    assert m % tm == 0 and n % tn == 0 and k % tk == 0, (name, m, n, k, tm, tn, tk)
    nk = k // tk
    dims = {"nn": _NN, "nt": _NT, "tn": _TN}[mode]
    if a_spec is None:
        a_spec = pl.BlockSpec((tk, tm), lambda i, j, kk: (kk, i)) if mode == "tn" else pl.BlockSpec((tm, tk), lambda i, j, kk: (i, kk))
    if b_spec is None:
        b_spec = pl.BlockSpec((tn, tk), lambda i, j, kk: (j, kk)) if mode == "nt" else pl.BlockSpec((tk, tn), lambda i, j, kk: (kk, j))
    tile = pl.BlockSpec((tm, tn), lambda i, j, kk: (i, j))
    if out_specs is None:
        out_specs = [tile for _ in out_dtypes]
    if out_shapes is None:
        out_shapes = [jax.ShapeDtypeStruct((m, n), d) for d in out_dtypes]
    n_ex, n_out = len(extras), len(out_dtypes)

    def body(*refs):
        a_ref, b_ref = refs[0], refs[1]
        ex_refs = refs[2:2 + n_ex]
        o_refs = refs[2 + n_ex:2 + n_ex + n_out]
        def finish(r):
            outs = epilogue(r, *[e[...] for e in ex_refs]) if epilogue is not None else (r,)
            for o_ref, o in zip(o_refs, outs):
                o_ref[...] = o.astype(o_ref.dtype)

        if nk == 1:
            finish(_dot(a_ref[...], b_ref[...], dims))
            return
        acc = refs[-1]
        kk = pl.program_id(2)

        @pl.when(kk == 0)
        def _():
            acc[...] = _dot(a_ref[...], b_ref[...], dims)

        @pl.when(kk > 0)
        def _():
            acc[...] += _dot(a_ref[...], b_ref[...], dims)

        @pl.when(kk == nk - 1)
        def _():
            finish(acc[...])

    outs, side_outs = _call(
        body, (a, b, *extras), name=name, grid=(m // tm, n // tn, nk),
        in_specs=[a_spec, b_spec] + [tile for _ in extras], out_specs=out_specs, out_shape=out_shapes,
        scratch_shapes=[] if nk == 1 else [pltpu.VMEM((tm, tn), F32)], sem=("parallel", "parallel", "arbitrary"), side=side)
    return outs if side is None else (outs, side_outs)


def _rowwise(fn, rows, bcast, out_rows, out_acc, *, name, tr, sub, side=None):
    rows = [r if isinstance(r, tuple) else (r, 0, r.shape[1]) for r in rows]
    row_specs = []
    for arr, c0, w in rows:
        assert c0 % w == 0, (name, c0, w)
        row_specs.append((w, c0 // w))
    rows = [r[0] for r in rows]
    L = rows[0].shape[0]
    tr = min(tr, L)
    sub = min(sub, tr)
    assert L % tr == 0 and tr % sub == 0, (name, L, tr, sub)
    n_r, n_b, n_or, n_oa = len(rows), len(bcast), len(out_rows), len(out_acc)

    def body(*refs):
        r_refs = refs[:n_r]
        b_refs = refs[n_r:n_r + n_b]
        or_refs = refs[n_r + n_b:n_r + n_b + n_or]
        oa_refs = refs[n_r + n_b + n_or:]
        i = pl.program_id(0)

        @pl.when(i == 0)
        def _():
            for o in oa_refs:
                o[...] = jnp.zeros_like(o)

        bvals = [b[...] for b in b_refs]

        def step(s, carry):
            r0 = pl.multiple_of(s * sub, sub)
            tiles = [r[pl.ds(r0, sub), :] for r in r_refs]
            outs = fn(*tiles, *bvals)
            for o_ref, o in zip(or_refs, outs[:n_or]):
                o_ref[pl.ds(r0, sub), :] = o.astype(o_ref.dtype)
            for o_ref, o in zip(oa_refs, outs[n_or:]):
                o_ref[...] += o
            return carry

        if tr == sub:
            step(0, 0)
        else:
            lax.fori_loop(0, tr // sub, step, 0)

    def whole(shape):
        nd = len(shape)
        return pl.BlockSpec(shape, lambda i, _n=nd: (0,) * _n)

    outs, side_outs = _call(
        body, (*rows, *bcast), name=name, grid=(L // tr,),
        in_specs=[pl.BlockSpec((tr, w), lambda i, _c=cb: (i, _c)) for w, cb in row_specs] + [whole(b.shape) for b in bcast],
        out_specs=[pl.BlockSpec((tr, c), lambda i: (i, 0)) for c, _ in out_rows] + [whole(s) for s in out_acc],
        out_shape=[jax.ShapeDtypeStruct((L, c), d) for c, d in out_rows] + [jax.ShapeDtypeStruct(s, F32) for s in out_acc],
        sem=("arbitrary",), side=side)
    return outs if side is None else (outs, side_outs)


def _colsum(x):
    return jnp.sum(x, axis=0, keepdims=True)


def _grouped(fn, rows, params, out_rows, out_acc, *, gw, name, tr):
    rows = [r if isinstance(r, tuple) else (r, 0) for r in rows]
    L = rows[0][0].shape[0]
    tr = min(tr, L)
    assert L % tr == 0
    G = None
    for p in params:
        G = p.shape[0] if p.ndim == 3 else p.shape[1] // gw
    cw = G * gw
    n_r, n_p, n_or, n_oa = len(rows), len(params), len(out_rows), len(out_acc)

    def pick(ref, g):
        return ref[g] if len(ref.shape) == 3 else ref[:, g * gw:(g + 1) * gw]

    def body(*refs):
        r_refs = refs[:n_r]
        p_refs = refs[n_r:n_r + n_p]
        or_refs = refs[n_r + n_p:n_r + n_p + n_or]
        oa_refs = refs[n_r + n_p + n_or:]

        @pl.when(pl.program_id(0) == 0)
        def _():
            for o in oa_refs:
                o[...] = jnp.zeros_like(o)

        for g in range(G):
            outs = fn(*[pick(r, g) for r in r_refs], *[pick(p, g) for p in p_refs])
            for o_ref, o in zip(or_refs, outs[:n_or]):
                o_ref[:, g * gw:(g + 1) * gw] = o.astype(o_ref.dtype)
            for o_ref, o in zip(oa_refs, outs[n_or:]):
                if len(o_ref.shape) == 3:
                    o_ref[g] += o
                else:
                    o_ref[:, g * gw:(g + 1) * gw] += o

    def whole(shape):
        nd = len(shape)
        return pl.BlockSpec(shape, lambda i, _n=nd: (0,) * _n)

    for _, c0 in rows:
        assert c0 % cw == 0
    return pl.pallas_call(
        body, name=name, grid=(L // tr,),
        in_specs=[pl.BlockSpec((tr, cw), lambda i, _c=c0 // cw: (i, _c)) for _, c0 in rows] + [whole(p.shape) for p in params],
        out_specs=[pl.BlockSpec((tr, cw), lambda i: (i, 0)) for _ in out_rows] + [whole(s) for s in out_acc],
        out_shape=[jax.ShapeDtypeStruct((L, cw), d) for d in out_rows] + [jax.ShapeDtypeStruct(s, F32) for s in out_acc],
        compiler_params=_params(("arbitrary",)),
    )(*[r[0] for r in rows], *params)


def _dsilu(p):
    s = _sigmoid(p)
    return s + p * s * (1.0 - s)


CONV_RC, CONV_CC = 32, 512


def _past_window(x_ref, head, r0, k, cs):
    if r0 == 0:
        return head[HALO - 3 + k:HALO - 3 + k + CONV_RC, cs]
    return x_ref[r0 - 3 + k:r0 - 3 + k + CONV_RC, cs]


def _conv_fwd(x, c0, C, w, b, *, silu, name, tr=512, cb=1024, side=None):
    L = x.shape[0]
    tr = min(tr, L)
    nb, hb = L // tr, tr // HALO
    assert L % tr == 0 and C % cb == 0 and c0 % cb == 0 and tr % CONV_RC == 0 and cb % CONV_CC == 0
    n_out = 2 if silu else 1

    def body(x_ref, h_ref, w_ref, b_ref, *rest):
        o_refs, head = rest[:n_out], rest[n_out]
        head[0:HALO, :] = h_ref[...]

        @pl.when(pl.program_id(0) == 0)
        def _():
            head[0:HALO, :] = jnp.zeros((HALO, cb), F32)

        head[HALO:, :] = x_ref[0:CONV_RC, :]
        for cc in range(cb // CONV_CC):
            cs = slice(cc * CONV_CC, (cc + 1) * CONV_CC)
            wv, bv = w_ref[:, cs], b_ref[:, cs]
            for r0 in range(0, tr, CONV_RC):
                y = bv
                for k in range(CONV_W):
                    y = y + wv[k:k + 1, :] * _past_window(x_ref, head, r0, k, cs)
                o_refs[0][r0:r0 + CONV_RC, cs] = y
                if silu:
                    o_refs[1][r0:r0 + CONV_RC, cs] = _silu(y)

    outs, side_outs = _call(
        body, (x, x, w, b), name=name, grid=(nb, C // cb),
        in_specs=[
            pl.BlockSpec((tr, cb), lambda i, j: (i, c0 // cb + j)),
            pl.BlockSpec((HALO, cb), lambda i, j: (jnp.maximum(i * hb - 1, 0), c0 // cb + j)),
            pl.BlockSpec((CONV_W, cb), lambda i, j: (0, j)),
            pl.BlockSpec((1, cb), lambda i, j: (0, j)),
        ],
        out_specs=[pl.BlockSpec((tr, cb), lambda i, j: (i, j)) for _ in range(n_out)],
        out_shape=[jax.ShapeDtypeStruct((L, C), F32) for _ in range(n_out)],
        scratch_shapes=[pltpu.VMEM((HALO + CONV_RC, cb), F32)], sem=("parallel", "parallel"), side=side)
    return outs if side is None else (outs, side_outs)


def _fold8(v):
    return (v[0:8] + v[8:16]) + (v[16:24] + v[24:32])


def _conv_bwd(dact, dc0, pre, x, xc0, C, w, *, silu, name, tr=512, cb=1024):
    L = x.shape[0]
    tr = min(tr, L)
    nb, hb = L // tr, tr // HALO
    last_h = L // HALO - 1
    assert L % tr == 0 and C % cb == 0 and tr % CONV_RC == 0 and cb % CONV_CC == 0

    def body(*refs):
        if silu:
            d_ref, dh_ref, p_ref, ph_ref, x_ref, xh_ref, w_ref, dx_ref, dwb_ref, dp, head = refs
        else:
            d_ref, dh_ref, x_ref, xh_ref, w_ref, dx_ref, dwb_ref, dp, head = refs
        i = pl.program_id(1)
        dp[tr:, :] = dh_ref[...] * _dsilu(ph_ref[...]) if silu else dh_ref[...]

        @pl.when(i == nb - 1)
        def _():
            dp[tr:, :] = jnp.zeros((HALO, cb), F32)

        head[0:HALO, :] = xh_ref[...]

        @pl.when(i == 0)
        def _():
            head[0:HALO, :] = jnp.zeros((HALO, cb), F32)
            dwb_ref[...] = jnp.zeros_like(dwb_ref)

        head[HALO:, :] = x_ref[0:CONV_RC, :]
        for cc in range(cb // CONV_CC):
            cs = slice(cc * CONV_CC, (cc + 1) * CONV_CC)
            for r0 in range(0, tr, CONV_RC):
                rs = slice(r0, r0 + CONV_RC)
                dp[rs, cs] = d_ref[rs, cs] * _dsilu(p_ref[rs, cs]) if silu else d_ref[rs, cs]
        for cc in range(cb // CONV_CC):
            cs = slice(cc * CONV_CC, (cc + 1) * CONV_CC)
            wv = w_ref[:, cs]
            acc = [jnp.zeros((8, CONV_CC), F32) for _ in range(CONV_W + 1)]
            for r0 in range(0, tr, CONV_RC):
                dm = dp[r0:r0 + CONV_RC, cs]
                dx = jnp.zeros((CONV_RC, CONV_CC), F32)
                for k in range(CONV_W):
                    dx = dx + wv[k:k + 1, :] * dp[r0 + 3 - k:r0 + 3 - k + CONV_RC, cs]
                    acc[k] = acc[k] + _fold8(dm * _past_window(x_ref, head, r0, k, cs))
                acc[CONV_W] = acc[CONV_W] + _fold8(dm)
                dx_ref[r0:r0 + CONV_RC, cs] = dx.astype(dx_ref.dtype)
            for k in range(CONV_W + 1):
                dwb_ref[k:k + 1, cs] += _colsum(acc[k])

    def main(c):
        return pl.BlockSpec((tr, cb), lambda j, i: (i, c // cb + j))

    def nxt(c):
        return pl.BlockSpec((HALO, cb), lambda j, i: (jnp.minimum((i + 1) * hb, last_h), c // cb + j))

    in_specs = [main(dc0), nxt(dc0)]
    args = [dact, dact]
    if silu:
        in_specs += [main(0), nxt(0)]
        args += [pre, pre]
    in_specs += [main(xc0), pl.BlockSpec((HALO, cb), lambda j, i: (jnp.maximum(i * hb - 1, 0), xc0 // cb + j)),
                 pl.BlockSpec((CONV_W, cb), lambda j, i: (0, j))]
    args += [x, x, w]
    return pl.pallas_call(
        body, name=name, grid=(C // cb, nb),
        in_specs=in_specs,
        out_specs=[pl.BlockSpec((tr, cb), lambda j, i: (i, j)), pl.BlockSpec((8, cb), lambda j, i: (0, j))],
        out_shape=[jax.ShapeDtypeStruct((L, C), BF16), jax.ShapeDtypeStruct((8, C), F32)],
        scratch_shapes=[pltpu.VMEM((tr + HALO, cb), F32), pltpu.VMEM((HALO + CONV_RC, cb), F32)],
        compiler_params=_params(("parallel", "arbitrary")),
    )(*args)


def _ssd_f1(dtraw, dtb, alog):
    q = dtraw.shape[0]
    dt = _softplus(dtraw + dtb)
    adt = dt * (-jnp.exp(alog))
    tril = (lax.broadcasted_iota(jnp.int32, (q, q), 0) >= lax.broadcasted_iota(jnp.int32, (q, q), 1)).astype(F32)
    acs = lax.dot_general(tril, adt, (_NN, ((), ())), precision=lax.Precision.HIGHEST, preferred_element_type=F32)
    return dt, acs


def _ssd_group(g, x, bm, cm, dt, acs, drow, hp):
    q = x.shape[0]
    lane = lax.broadcasted_iota(jnp.int32, (1, 128), 1)
    sub = lax.broadcasted_iota(jnp.int32, (128, 1), 0)
    head_of = lax.broadcasted_iota(jnp.int32, (1, GROUP_W), 1) // HEAD_DIM
    is_last = (lax.broadcasted_iota(jnp.int32, (q, 1), 0) == q - 1).astype(F32)
    causal = lax.broadcasted_iota(jnp.int32, (q, q), 0) >= lax.broadcasted_iota(jnp.int32, (q, q), 1)
    acs_end = jnp.sum(acs * is_last, axis=0, keepdims=True)
    acs_t = acs.T
    dt_exp = jnp.zeros((q, GROUP_W), F32)
    acs_exp = jnp.zeros((q, GROUP_W), F32)
    end_exp = jnp.zeros((1, GROUP_W), F32)
    d_exp = jnp.zeros((1, GROUP_W), F32)
    heads = []
    for k in range(HEADS_PER_GROUP):
        h = HEADS_PER_GROUP * g + k
        oh = (lane == h).astype(F32)
        mk = (head_of == k).astype(F32)
        acs_col = jnp.sum(acs * oh, axis=1, keepdims=True)
        acs_row = jnp.sum(acs_t * (sub == h).astype(F32), axis=0, keepdims=True)
        dt_exp = dt_exp + jnp.sum(dt * oh, axis=1, keepdims=True) * mk
        acs_exp = acs_exp + acs_col * mk
        end_exp = end_exp + jnp.sum(acs_end * oh, axis=1, keepdims=True) * mk
        d_exp = d_exp + jnp.sum(drow * oh, axis=1, keepdims=True) * mk
        heads.append((acs_col, acs_row, mk))
    xdt = x * dt_exp
    states = _dot(bm, xdt * jnp.exp(end_exp - acs_exp), _TN)
    y = _dot(cm, hp, _NN) * jnp.exp(acs_exp) + x * d_exp
    scores = _dot(cm, bm, _NT)
    for acs_col, acs_row, mk in heads:
        dec = jnp.exp(jnp.where(causal, acs_col - acs_row, -jnp.inf))
        y = y + _dot(scores * dec, xdt * mk, _NN)
    return y, hp * jnp.exp(end_exp) + states


def _ssd_fwd(act, dtraw, dtb, alog, drow, *, name):
    L = act.shape[0]
    q = min(SSD_CHUNK, L)
    nc = L // q
    d_ssd = N_GROUPS * GROUP_W

    def body(act_ref, dt_ref, dtb_ref, alog_ref, drow_ref, y_ref, hst_ref, h):
        @pl.when(pl.program_id(0) == 0)
        def _():
            h[...] = jnp.zeros_like(h)

        dt, acs = _ssd_f1(dt_ref[...], dtb_ref[...], alog_ref[...])
        drow = drow_ref[...]
        for g in range(N_GROUPS):
            hp = h[g]
            hst_ref[0, g] = hp
            y, hn = _ssd_group(g, act_ref[:, g * GROUP_W:(g + 1) * GROUP_W],
                               act_ref[:, d_ssd + g * STATE:d_ssd + (g + 1) * STATE],
                               act_ref[:, d_ssd + (N_GROUPS + g) * STATE:d_ssd + (N_GROUPS + g + 1) * STATE],
                               dt, acs, drow, hp)
            y_ref[:, g * GROUP_W:(g + 1) * GROUP_W] = y
            h[g] = hn

    row = pl.BlockSpec((1, 128), lambda i: (0, 0))
    return pl.pallas_call(
        body, name=name, grid=(nc,),
        in_specs=[pl.BlockSpec((q, act.shape[1]), lambda i: (i, 0)), pl.BlockSpec((q, 128), lambda i: (i, 0)), row, row, row],
        out_specs=[pl.BlockSpec((q, d_ssd), lambda i: (i, 0)),
                   pl.BlockSpec((1, N_GROUPS, STATE, GROUP_W), lambda i: (i, 0, 0, 0))],
        out_shape=[jax.ShapeDtypeStruct((L, d_ssd), F32), jax.ShapeDtypeStruct((nc, N_GROUPS, STATE, GROUP_W), F32)],
        scratch_shapes=[pltpu.VMEM((N_GROUPS, STATE, GROUP_W), F32)],
        compiler_params=_params(("arbitrary",)),
    )(act, dtraw, dtb, alog, drow)


def _ssd_bwd(act, dtraw, dy, hst, dtb, alog, drow, *, name, side=None):
    L = act.shape[0]
    q = min(SSD_CHUNK, L)
    nc = L // q
    d_ssd = N_GROUPS * GROUP_W

    def body(act_ref, dt_ref, dy_ref, hst_ref, dtb_ref, alog_ref, drow_ref, dact_ref, ddt_ref, dpar_ref, dh):
        @pl.when(pl.program_id(0) == 0)
        def _():
            dh[...] = jnp.zeros_like(dh)
            dpar_ref[...] = jnp.zeros_like(dpar_ref)

        (dt, acs), f1_vjp = jax.vjp(_ssd_f1, dt_ref[...], dtb_ref[...], alog_ref[...])
        drow = drow_ref[...]
        ddt = jnp.zeros_like(dt)
        dacs = jnp.zeros_like(acs)
        ddrow = jnp.zeros_like(drow)
        for g in range(N_GROUPS):
            xs = slice(g * GROUP_W, (g + 1) * GROUP_W)
            bs = slice(d_ssd + g * STATE, d_ssd + (g + 1) * STATE)
            cs = slice(d_ssd + (N_GROUPS + g) * STATE, d_ssd + (N_GROUPS + g + 1) * STATE)
            _, f2_vjp = jax.vjp(functools.partial(_ssd_group, g), act_ref[:, xs], act_ref[:, bs], act_ref[:, cs],
                                dt, acs, drow, hst_ref[0, g])
            dx, dbm, dcm, ddt_g, dacs_g, ddrow_g, dhp = f2_vjp((dy_ref[:, xs], dh[g]))
            dact_ref[:, xs] = dx
            dact_ref[:, bs] = dbm
            dact_ref[:, cs] = dcm
            dh[g] = dhp
            ddt, dacs, ddrow = ddt + ddt_g, dacs + dacs_g, ddrow + ddrow_g
        ddtraw, ddtb, dalog = f1_vjp((ddt, dacs))
        ddt_ref[...] = ddtraw
        dpar_ref[0:1, :] += ddtb
        dpar_ref[1:2, :] += dalog
        dpar_ref[2:3, :] += ddrow

    row = pl.BlockSpec((1, 128), lambda i: (0, 0))
    rev = lambda i: (nc - 1 - i, 0)
    outs, side_outs = _call(
        body, (act, dtraw, dy, hst, dtb, alog, drow), name=name, grid=(nc,),
        in_specs=[pl.BlockSpec((q, act.shape[1]), rev), pl.BlockSpec((q, 128), rev), pl.BlockSpec((q, d_ssd), rev),
                  pl.BlockSpec((1, N_GROUPS, STATE, GROUP_W), lambda i: (nc - 1 - i, 0, 0, 0)), row, row, row],
        out_specs=[pl.BlockSpec((q, act.shape[1]), rev), pl.BlockSpec((q, 128), rev), pl.BlockSpec((8, 128), lambda i: (0, 0))],
        out_shape=[jax.ShapeDtypeStruct(act.shape, F32), jax.ShapeDtypeStruct((L, 128), F32), jax.ShapeDtypeStruct((8, 128), F32)],
        scratch_shapes=[pltpu.VMEM((N_GROUPS, STATE, GROUP_W), F32)], sem=("arbitrary",), side=side)
    return outs if side is None else (outs, side_outs)


def _gate_head(xl, wa, ba, wx, bx, lam):
    r = _sigmoid(_dot(xl, wa, _NN) + ba)
    i = _sigmoid(_dot(xl, wx, _NN) + bx)
    log_a = -LRU_C * r * _softplus(-lam)
    return jnp.exp(log_a), jnp.sqrt(-_expm1(2.0 * log_a)) * (i * xl)


def _gate_head_bwd(xl, da, du, wa, ba, wx, bx, lam):
    _, vjp = jax.vjp(_gate_head, xl, wa, ba, wx, bx, lam)
    dxl, dwa, dba, dwx, dbx, dlam = vjp((da, du))
    return dxl, dwa, dba, dwx, dbx, dlam


def _scan_tile(a, b, rows, reverse):
    for d in (1, 2, 4):
        if reverse:
            keep = rows < 8 - d
            a_sh, b_sh = pltpu.roll(a, 8 - d, 0), pltpu.roll(b, 8 - d, 0)
        else:
            keep = rows >= d
            a_sh, b_sh = pltpu.roll(a, d, 0), pltpu.roll(b, d, 0)
        b = b + a * jnp.where(keep, b_sh, 0.0)
        a = a * jnp.where(keep, a_sh, 1.0)
    return a, b


def _lru_scan_fwd(a, u, *, name, tr=512, cb=1024):
    L, C = a.shape
    tr, cb = min(tr, L), min(cb, C)

    def body(a_ref, u_ref, h_ref, hp_ref, carry):
        @pl.when(pl.program_id(1) == 0)
        def _():
            carry[...] = jnp.zeros_like(carry)

        rows = lax.broadcasted_iota(jnp.int32, (8, cb), 0)

        def tile(t, hc):
            r0 = pl.multiple_of(t * 8, 8)
            pa, hb = _scan_tile(a_ref[pl.ds(r0, 8), :], u_ref[pl.ds(r0, 8), :], rows, False)
            h = hb + pa * hc
            h_ref[pl.ds(r0, 8), :] = h
            hp_ref[pl.ds(r0, 8), :] = jnp.where(rows >= 1, pltpu.roll(h, 1, 0), hc)
            return h[7:8, :]

        carry[...] = lax.fori_loop(0, tr // 8, tile, carry[...])

    blk = pl.BlockSpec((tr, cb), lambda j, i: (i, j))
    return pl.pallas_call(
        body, name=name, grid=(C // cb, L // tr),
        in_specs=[blk, blk], out_specs=[blk, blk],
        out_shape=[jax.ShapeDtypeStruct((L, C), F32), jax.ShapeDtypeStruct((L, C), F32)],
        scratch_shapes=[pltpu.VMEM((1, cb), F32)],
        compiler_params=_params(("parallel", "arbitrary")),
    )(a, u)


def _lru_scan_bwd(a, hprev, dh, *, name, tr=512, cb=1024):
    L, C = a.shape
    tr, cb = min(tr, L), min(cb, C)
    nb = L // tr

    def body(a_ref, hp_ref, dh_ref, da_ref, du_ref, carry):
        @pl.when(pl.program_id(1) == 0)
        def _():
            carry[...] = jnp.zeros_like(carry)

        rows = lax.broadcasted_iota(jnp.int32, (8, cb), 0)

        def tile(t, gc):
            r0 = pl.multiple_of((tr // 8 - 1 - t) * 8, 8)
            av, dv = a_ref[pl.ds(r0, 8), :], dh_ref[pl.ds(r0, 8), :]
            pa, gb = _scan_tile(av, av * dv, rows, True)
            big = gb + pa * gc
            g = dv + jnp.where(rows < 7, pltpu.roll(big, 7, 0), gc)
            du_ref[pl.ds(r0, 8), :] = g
            da_ref[pl.ds(r0, 8), :] = g * hp_ref[pl.ds(r0, 8), :]
            return big[0:1, :]

        carry[...] = lax.fori_loop(0, tr // 8, tile, carry[...])

    blk = pl.BlockSpec((tr, cb), lambda j, i: (nb - 1 - i, j))
    return pl.pallas_call(
        body, name=name, grid=(C // cb, nb),
        in_specs=[blk, blk, blk], out_specs=[blk, blk],
        out_shape=[jax.ShapeDtypeStruct((L, C), F32), jax.ShapeDtypeStruct((L, C), F32)],
        scratch_shapes=[pltpu.VMEM((1, cb), F32)],
        compiler_params=_params(("parallel", "arbitrary")),
    )(a, hprev, dh)


def _ssd_gate(y, z, n):
    v = y * _silu(z)
    return v * lax.rsqrt(jnp.mean(v * v, axis=-1, keepdims=True) + EPS) * n


def _ssd_gate_bwd(y, z, dy, n):
    _, vjp = jax.vjp(_ssd_gate, y, z, n)
    return vjp(dy)


def _lru_out(hl, gate, n):
    return _rms(hl * _gelu(gate), n)


def _lru_out_bwd(hl, gate, dy, n):
    _, vjp = jax.vjp(_lru_out, hl, gate, n)
    return vjp(dy)


def _mid(x, mix, pm, pmlp):
    x1 = x + _rms(mix, pm)
    return x1, _rms(x1, pmlp)


def _mid_bwd(x, mix, dx1p, dh2, pm, pmlp):
    _, vjp = jax.vjp(_mid, x, mix, pm, pmlp)
    dx, dmix, dpm, dpmlp = vjp((dx1p, dh2))
    return dmix, dx, dpm, dpmlp


def _loss_bwd(hm2, x1, tgt, g):
    def lossf(hm2, x1, g):
        e = x1 + _rms(hm2, g) - tgt
        return 0.5 * jnp.sum(jnp.mean(e * e, axis=-1, keepdims=True), axis=0, keepdims=True)

    val, vjp = jax.vjp(lossf, hm2, x1, g)
    dhm2, dx1, dg = vjp(jnp.ones((1, 1), F32))
    return dhm2, dx1, dg, val * jnp.ones((1, 128), F32)


def _in_bwd(x, dh_a, dh_b, dx1, g):
    _, vjp = jax.vjp(_rms, x, g)
    dx, dg = vjp(dh_a + dh_b)
    return dx + dx1, dg


def _adamw(w, g, m, v):
    m = ADAM_B1 * m + (1.0 - ADAM_B1) * g
    v = ADAM_B2 * v + (1.0 - ADAM_B2) * (g * g)
    m_hat = m / (1.0 - ADAM_B1 ** ADAM_STEP)
    v_hat = v / (1.0 - ADAM_B2 ** ADAM_STEP)
    return -ADAM_LR * (m_hat / (jnp.sqrt(v_hat) + ADAM_EPS) + ADAM_WD * w), m, v


class _LocalPlan:
    def __init__(self, p):
        self.p, self.large = p, {}

    def weight(self, name):
        return self.p[name]

    def side(self, kernel_name):
        return None

    def done(self, kernel_name, side_outs):
        pass

    def grad(self, name, g):
        self.large[name] = g


def _local_step(x, tgt, p, plan):
    L, D = x.shape

    def carry(fn, *args, name, **kw):
        side = plan.side(name)
        if side is None:
            return fn(*args, name=name, **kw)
        outs, side_outs = fn(*args, name=name, side=side, **kw)
        plan.done(name, side_outs)
        return outs

    d_ssd, d_xbc, d_lru, d_mix, d_ff = 2048, 4096, 2048, 4096, 8192
    n_main = d_ssd + d_xbc + 2 * d_lru
    c_xbc, c_gate, c_xl = d_ssd, d_ssd + d_xbc, d_ssd + d_xbc + d_lru
    TR, SUB = 256, 32
    mm = dict(tm=1024, tn=1024, tk=2048)

    (h,) = _rowwise(lambda xt, g: (_rms(xt, g),), [x], [p["pre_mix_norm"]], [(D, BF16)], [], name="f_prenorm", tr=TR, sub=SUB)
    w_main, w_dt = plan.weight("w_main"), plan.weight("w_dt")
    (proj,) = carry(_matmul, h, w_main, mode="nn", m=L, n=n_main, k=D, out_dtypes=[F32], name="f_inproj", **mm)
    (dtraw,) = _matmul(h, w_dt, mode="nn", m=L, n=128, k=D, out_dtypes=[F32], name="f_dtproj", **mm)
    pre, act = carry(_conv_fwd, proj, c_xbc, d_xbc, p["ssd_conv_w"], p["ssd_conv_b"], silu=True, name="f_ssdconv")
    (xl,) = carry(_conv_fwd, proj, c_xl, d_lru, p["lru_conv_w"], p["lru_conv_b"], silu=False, name="f_lruconv")
    yraw, hst = _ssd_fwd(act, dtraw, p["dtb"], p["alog"], p["drow"], name="f_ssd")
    (yssd,) = _grouped(lambda y, z, n: (_ssd_gate(y, z, n),), [yraw, (proj, 0)], [p["ssd_norm"]], [BF16], [],
                       gw=GROUP_W, name="f_ssdgate", tr=TR)
    gate_p = [p["lru_w_a"], p["lru_b_a"], p["lru_w_x"], p["lru_b_x"], p["lru_lambda"]]
    a, u = _grouped(_gate_head, [xl], gate_p, [F32, F32], [], gw=LRU_BLOCK, name="f_lrugates", tr=TR)
    hl, hprev = _lru_scan_fwd(a, u, name="f_lruscan")
    (ylru,) = _rowwise(lambda ht, gt, n: (_lru_out(ht, gt, n),), [hl, (proj, c_gate, d_lru)], [p["lru_norm"]],
                       [(d_lru, BF16)], [], name="f_lruout", tr=TR, sub=SUB)
    ycat = jnp.concatenate([yssd, ylru], axis=1)
    w_out = plan.weight("w_out")
    (mix,) = _matmul(ycat, w_out, mode="nn", m=L, n=D, k=d_mix, out_dtypes=[F32], name="f_outproj", **mm)
    x1, h2 = _rowwise(_mid, [x, mix], [p["post_mix_norm"], p["pre_mlp_norm"]], [(D, F32), (D, BF16)], [],
                      name="f_mid", tr=TR, sub=SUB)
    nb_mi = (d_ff // 4) // mm["tn"]
    w_mi = plan.weight("w_mi")
    hm, act2 = _matmul(h2, w_mi, mode="nn", m=L, n=d_ff, k=D, out_dtypes=[BF16, BF16], name="f_mlpin",
                       b_spec=pl.BlockSpec((None, mm["tk"], mm["tn"]), lambda i, j, kk: (j // nb_mi, kk, j % nb_mi)),
                       epilogue=lambda r: (r, jnp.square(jnp.maximum(r, 0.0))), **mm)
    w_mo = plan.weight("w_mo")
    (hm2,) = _matmul(act2, w_mo, mode="nn", m=L, n=D, k=d_ff, out_dtypes=[F32], name="f_mlpout", **mm)

    dhm2, dx1p, d_post_mlp, loss = _rowwise(_loss_bwd, [hm2, x1, tgt], [p["post_mlp_norm"]], [(D, BF16), (D, F32)],
                                            [(1, D), (1, 128)], name="b_loss", tr=TR, sub=SUB)
    (dhm,) = _matmul(dhm2, w_mo, mode="nt", m=L, n=d_ff, k=D, out_dtypes=[BF16], name="b_mlpout_dx", extras=[hm],
                     epilogue=lambda r, hmv: (r * (2.0 * jnp.maximum(hmv.astype(F32), 0.0)),), **mm)
    (dw_mo,) = _matmul(act2, dhm2, mode="tn", m=d_ff, n=D, k=L, out_dtypes=[BF16], name="b_mlpout_dw", **mm)
    plan.grad("w_mlp_out", dw_mo.reshape(4, -1, D))
    kb_mi = (d_ff // 4) // mm["tk"]
    (dh2,) = carry(_matmul, dhm, w_mi, mode="nt", m=L, n=D, k=d_ff, out_dtypes=[F32], name="b_mlpin_dx",
                   b_spec=pl.BlockSpec((None, mm["tn"], mm["tk"]), lambda i, j, kk: (kk // kb_mi, j, kk % kb_mi)), **mm)
    (dw_mi,) = _matmul(h2, dhm, mode="tn", m=D, n=d_ff, k=L, out_dtypes=[BF16], name="b_mlpin_dw",
                       out_specs=[pl.BlockSpec((None, mm["tm"], mm["tn"]), lambda i, j, kk: (j // nb_mi, i, j % nb_mi))],
                       out_shapes=[jax.ShapeDtypeStruct((4, D, d_ff // 4), BF16)], **mm)
    plan.grad("w_mlp_in", dw_mi)
    dmix, dx1, d_post_mix, d_pre_mlp = _rowwise(_mid_bwd, [x, mix, dx1p, dh2], [p["post_mix_norm"], p["pre_mlp_norm"]],
                                                [(D, BF16), (D, F32)], [(1, D), (1, D)], name="b_mid", tr=TR, sub=SUB)
    (dw_out,) = _matmul(ycat, dmix, mode="tn", m=d_mix, n=D, k=L, out_dtypes=[BF16], name="b_outproj_dw", **mm)
    plan.grad("w_out", dw_out.reshape(4, -1, D))
    (dycat,) = _matmul(dmix, w_out, mode="nt", m=L, n=d_mix, k=D, out_dtypes=[F32], name="b_outproj_dx", **mm)
    dhl, dgate, d_lru_norm = _rowwise(_lru_out_bwd, [hl, (proj, c_gate, d_lru), (dycat, d_ssd, d_lru)], [p["lru_norm"]],
                                      [(d_lru, F32), (d_lru, BF16)], [(1, d_lru)], name="b_lruout", tr=TR, sub=SUB)
    da, du = _lru_scan_bwd(a, hprev, dhl, name="b_lruscan")
    dxl, d_wa, d_ba, d_wx, d_bx, d_lam = _grouped(
        _gate_head_bwd, [xl, da, du], gate_p, [F32],
        [(LRU_HEADS, LRU_BLOCK, LRU_BLOCK), (1, d_lru), (LRU_HEADS, LRU_BLOCK, LRU_BLOCK), (1, d_lru), (1, d_lru)],
        gw=LRU_BLOCK, name="b_lrugates", tr=TR)
    dxlru, dwb_lru = _conv_bwd(dxl, 0, None, proj, c_xl, d_lru, p["lru_conv_w"], silu=False, name="b_lruconv")
    dyraw, dz, d_ssd_norm = _grouped(_ssd_gate_bwd, [yraw, (proj, 0), (dycat, 0)], [p["ssd_norm"]], [F32, BF16], [(1, d_ssd)],
                                     gw=GROUP_W, name="b_ssdgate", tr=TR)
    dact, ddtraw, dpar = carry(_ssd_bwd, act, dtraw, dyraw, hst, p["dtb"], p["alog"], p["drow"], name="b_ssd")
    dxbc, dwb_ssd = _conv_bwd(dact, 0, pre, proj, c_xbc, d_xbc, p["ssd_conv_w"], silu=True, name="b_ssdconv")
    dproj = jnp.concatenate([dz, dxbc, dgate, dxlru], axis=1)
    (dw_main,) = _matmul(h, dproj, mode="tn", m=D, n=n_main, k=L, out_dtypes=[BF16], name="b_inproj_dw", **mm)
    (dw_dt,) = _matmul(h, ddtraw, mode="tn", m=D, n=128, k=L, out_dtypes=[BF16], name="b_dtproj_dw", **mm)
    plan.grad("w_in", (dw_main, dw_dt))
    (dh_a,) = carry(_matmul, dproj, w_main, mode="nt", m=L, n=D, k=n_main, out_dtypes=[F32], name="b_inproj_dx", **mm)
    (dh_b,) = _matmul(ddtraw, w_dt, mode="nt", m=L, n=D, k=128, out_dtypes=[F32], name="b_dtproj_dx", **mm)
    grad_x, d_pre_mix = carry(_rowwise, _in_bwd, [x, dh_a, dh_b, dx1], [p["pre_mix_norm"]], [(D, F32)], [(1, D)],
                              name="b_prenorm", tr=TR, sub=SUB)

    small = dict(loss=loss, pre_mix_norm=d_pre_mix, ssd_conv=dwb_ssd, ssd_par=dpar, ssd_norm=d_ssd_norm, lru_conv=dwb_lru,
                 lru_w_a=d_wa, lru_b_a=d_ba, lru_w_x=d_wx, lru_b_x=d_bx, lru_lambda=d_lam, lru_norm=d_lru_norm,
                 post_mix_norm=d_post_mix, pre_mlp_norm=d_pre_mlp, post_mlp_norm=d_post_mlp)
    return grad_x, small


def _place():
    return lax.axis_index("x"), lax.axis_index("y"), lax.axis_index("c")


def _other_chips(x, y):
    return [(1 - x, y), (x, 1 - y), (1 - x, 1 - y)]


def _allgather8(blk, *, name):
    r, n = blk.shape

    def body(x_ref, out_ref, send_sems, recv_sems, local_sem):
        x, y, c = _place()
        me, sibling = (x, y, c), (x, y, 1 - c)
        chips = _other_chips(x, y)

        def rows(px, py, pc):
            return out_ref.at[pl.ds((4 * px + 2 * py + pc) * r, r), :]

        def copy(k, block, to, src=None):
            return pltpu.make_async_remote_copy(
                src_ref=rows(*block) if src is None else src, dst_ref=rows(*block),
                send_sem=send_sems.at[k], recv_sem=recv_sems.at[k], device_id=to, device_id_type=MESH)

        mine = pltpu.make_async_copy(x_ref, rows(*me), local_sem)
        mine.start()
        first = [copy(0, me, sibling, src=x_ref)]
        first += [copy(1 + k, me, (*chip, c), src=x_ref) for k, chip in enumerate(chips)]
        for cp in first:
            cp.start()
        passed = [copy(4 + k, (*chip, c), sibling) for k, chip in enumerate(chips)]
        for k, chip in enumerate(chips):
            copy(1 + k, (*chip, c), me).wait_recv()
            passed[k].start()
        copy(0, sibling, me).wait_recv()
        for k, chip in enumerate(chips):
            copy(4 + k, (*chip, 1 - c), me).wait_recv()
        for cp in first + passed:
            cp.wait_send()
        mine.wait()

    return pl.pallas_call(
        body, name=name,
        out_shape=jax.ShapeDtypeStruct((8 * r, n), blk.dtype),
        in_specs=[pl.BlockSpec(memory_space=pltpu.VMEM)], out_specs=pl.BlockSpec(memory_space=pltpu.VMEM),
        scratch_shapes=[pltpu.SemaphoreType.DMA((7,)), pltpu.SemaphoreType.DMA((7,)), pltpu.SemaphoreType.DMA],
        compiler_params=pltpu.CompilerParams(vmem_limit_bytes=VMEM_LIMIT),
    )(blk)


def _sum8(g, *, name, tr=232):
    _, r, n = g.shape
    tr = min(tr, r)
    assert r % tr == 0

    def body(g_ref, o_ref):
        s = g_ref[0]
        for k in range(1, 8):
            s = s + g_ref[k]
        o_ref[...] = s

    return pl.pallas_call(
        body, name=name, grid=(r // tr,),
        in_specs=[pl.BlockSpec((8, tr, n), lambda i: (0, i, 0))], out_specs=pl.BlockSpec((tr, n), lambda i: (i, 0)),
        out_shape=jax.ShapeDtypeStruct((r, n), g.dtype), compiler_params=_params(("parallel",)),
    )(g)


def _blocks(fn, ins, outs, *, grid, name, prefetch=None, aliases=None):
    n_in = len(ins)

    def body(*refs):
        if prefetch is not None:
            refs = refs[1:]
        res = fn(*[r[...] for r in refs[:n_in]])
        for o_ref, o in zip(refs[n_in:], res):
            o_ref[...] = o.astype(o_ref.dtype)

    in_specs = [pl.BlockSpec(b, m) for _, b, m in ins]
    out_specs = [pl.BlockSpec(b, m) for _, b, m in outs]
    kw = dict(name=name, out_shape=[s for s, _, _ in outs], input_output_aliases=aliases or {},
              compiler_params=_params(("arbitrary",) * len(grid)))
    arrs = [a for a, _, _ in ins]
    if prefetch is None:
        return pl.pallas_call(body, grid=grid, in_specs=in_specs, out_specs=out_specs, **kw)(*arrs)
    spec = pltpu.PrefetchScalarGridSpec(num_scalar_prefetch=1, grid=grid, in_specs=in_specs, out_specs=out_specs)
    return pl.pallas_call(body, grid_spec=spec, **kw)(prefetch, *arrs)


def _gather_weights(slots, *, name):
    n = len(slots)

    def body(*refs):
        i_refs, g_refs = refs[:n], refs[n:2 * n]
        send_sems, recv_sems = refs[2 * n:]
        x, y, c = _place()
        j, sibling = 2 * x + y, (x, y, 1 - c)
        chips = _other_chips(x, y)

        def cp(w, k, src, dst, to):
            return pltpu.make_async_remote_copy(src_ref=src, dst_ref=dst, send_sem=send_sems.at[6 * w + k],
                                                recv_sem=recv_sems.at[6 * w + k], device_id=to, device_id_type=MESH)

        def part(ref, chip_idx, hc):
            rh = ref.shape[1] // 2
            return ref.at[chip_idx, pl.ds(hc * rh, rh), :]

        sends = []
        for w in range(n):
            for k, chip in enumerate(chips):
                d = cp(w, k, part(i_refs[w], j, c), part(g_refs[w], j, c), (*chip, c))
                d.start()
                sends.append(d)
        for w in range(n):
            for k, (px, py) in enumerate(chips):
                landed = part(g_refs[w], 2 * px + py, c)
                cp(w, k, landed, landed, (px, py, c)).wait_recv()
                d = cp(w, 3 + k, landed, landed, sibling)
                d.start()
                sends.append(d)
        for w in range(n):
            for k, (px, py) in enumerate(chips):
                other = part(g_refs[w], 2 * px + py, 1 - c)
                cp(w, 3 + k, other, other, sibling).wait_recv()
        for d in sends:
            d.wait_send()

    return pl.pallas_call(
        body, name=name,
        out_shape=[jax.ShapeDtypeStruct(s.shape, s.dtype) for s in slots],
        in_specs=[ANY] * n, out_specs=[ANY] * n, input_output_aliases={w: w for w in range(n)},
        scratch_shapes=[pltpu.SemaphoreType.DMA((6 * n,)), pltpu.SemaphoreType.DMA((6 * n,))],
    )(*slots)


def _rs_sibling(gs, *, name):
    n = len(gs)

    def body(*refs):
        g_refs, got_refs = refs[:n], refs[n:2 * n]
        send_sems, recv_sems = refs[2 * n:]
        x, y, c = _place()
        copies = []
        for w in range(n):
            for s in range(4):
                d = pltpu.make_async_remote_copy(src_ref=g_refs[w].at[s, 1 - c], dst_ref=got_refs[w].at[s],
                                                 send_sem=send_sems.at[4 * w + s], recv_sem=recv_sems.at[4 * w + s],
                                                 device_id=(x, y, 1 - c), device_id_type=MESH)
                d.start()
                copies.append(d)
        for d in copies:
            d.wait_recv()
        for d in copies:
            d.wait_send()

    return pl.pallas_call(
        body, name=name, out_shape=[jax.ShapeDtypeStruct((4,) + g.shape[2:], g.dtype) for g in gs],
        in_specs=[ANY] * n, out_specs=[ANY] * n,
        scratch_shapes=[pltpu.SemaphoreType.DMA((4 * n,)), pltpu.SemaphoreType.DMA((4 * n,))],
    )(*gs)


def _pack(arrs):
    parts = []
    for v in arrs:
        f = v.reshape(-1)
        f = jnp.pad(f, (0, (-f.shape[0]) % 1024))
        parts.append(f.reshape(-1, 128))
    return jnp.concatenate(parts, axis=0)


def _unpack(packed, shapes):
    out, r0 = [], 0
    for s in shapes:
        size = 1
        for d in s:
            size *= d
        nr = (size + 1023) // 1024 * 8
        out.append(packed[r0:r0 + nr].reshape(-1)[:size].reshape(s))
        r0 += nr
    return out


SMALL_GRADS = [("loss", (1, 128)), ("pre_mix_norm", (1, 2048)), ("ssd_conv", (8, 4096)), ("ssd_par", (8, 128)),
               ("ssd_norm", (1, 2048)), ("lru_conv", (8, 2048)), ("lru_w_a", (16, 128, 128)), ("lru_b_a", (1, 2048)),
               ("lru_w_x", (16, 128, 128)), ("lru_b_x", (1, 2048)), ("lru_lambda", (1, 2048)), ("lru_norm", (1, 2048)),
               ("post_mix_norm", (1, 2048)), ("pre_mlp_norm", (1, 2048)), ("post_mlp_norm", (1, 2048))]

WEIGHTS = ['pre_mix_norm', 'w_in', 'ssd_conv_w', 'ssd_conv_b', 'ssd_dt_bias', 'ssd_a_log', 'ssd_d', 'ssd_norm', 'lru_conv_w',
           'lru_conv_b', 'lru_w_a', 'lru_b_a', 'lru_w_x', 'lru_b_x', 'lru_lambda', 'lru_norm', 'w_out', 'post_mix_norm',
           'pre_mlp_norm', 'w_mlp_in', 'w_mlp_out', 'post_mlp_norm']
LARGE = ['w_in', 'w_out', 'w_mlp_in', 'w_mlp_out']

D_SSD, D_XBC, DT_W = 2048, 4096, 32
TB = 256


def _half(ref, chip_idx, hc):
    rh = ref.shape[1] // 2
    return ref.at[chip_idx, pl.ds(hc * rh, rh), :]


def _job_gather_ici(i_ref, g_ref, send_sems, recv_sems, base):
    x, y, c = _place()
    j = 2 * x + y
    sends, recvs = [], []
    for k, (px, py) in enumerate(_other_chips(x, y)):
        kw = dict(send_sem=send_sems.at[base + k], recv_sem=recv_sems.at[base + k], device_id=(px, py, c), device_id_type=MESH)
        sends.append(pltpu.make_async_remote_copy(src_ref=_half(i_ref, j, c), dst_ref=_half(g_ref, j, c), **kw))
        landed = _half(g_ref, 2 * px + py, c)
        recvs.append(pltpu.make_async_remote_copy(src_ref=landed, dst_ref=landed, **kw))
    return sends, recvs


def _job_gather_sibling(i_ref, g_ref, send_sems, recv_sems, base):
    x, y, c = _place()
    sends, recvs = [], []
    for k, (px, py) in enumerate(_other_chips(x, y)):
        kw = dict(send_sem=send_sems.at[base + k], recv_sem=recv_sems.at[base + k], device_id=(x, y, 1 - c), device_id_type=MESH)
        sends.append(pltpu.make_async_remote_copy(src_ref=_half(i_ref, 2 * px + py, c), dst_ref=_half(g_ref, 2 * px + py, c), **kw))
        other = _half(g_ref, 2 * px + py, 1 - c)
        recvs.append(pltpu.make_async_remote_copy(src_ref=other, dst_ref=other, **kw))
    return sends, recvs


def _job_reduce_ici(s_ref, got_refs, send_sems, recv_sems, base):
    x, y, c = _place()
    sends = [pltpu.make_async_remote_copy(src_ref=s_ref.at[2 * px + py], dst_ref=got_refs[k], send_sem=send_sems.at[base + k],
                                          recv_sem=recv_sems.at[base + k], device_id=(px, py, c), device_id_type=MESH)
             for k, (px, py) in enumerate(_other_chips(x, y))]
    return sends, sends


class _DistPlan:
    def __init__(self, slots, w_main, w_dt, where, shapes):
        self.slots, self.w_main, self.w_dt, self.where, self.shapes = slots, w_main, w_dt, where, shapes
        self.pending, self.totals = [], {}

    def weight(self, name):
        d = self.w_main.shape[0]
        return {"w_main": lambda: self.w_main, "w_dt": lambda: self.w_dt, "w_out": lambda: self.slots["w_out"].reshape(-1, d),
                "w_mi": lambda: self.slots["w_mlp_in"], "w_mo": lambda: self.slots["w_mlp_out"].reshape(-1, d)}[name]()

    def _slot_side(self, jobs):
        names = []
        for n, _ in jobs:
            if n not in names:
                names.append(n)
        arrs = [self.slots[n] for n in names]

        def make(i_refs, o_refs, send_sems, recv_sems):
            sends, recvs = [], []
            for q, (n, job) in enumerate(jobs):
                s, r = job(i_refs[names.index(n)], o_refs[names.index(n)], send_sems, recv_sems, 3 * q)
                sends, recvs = sends + s, recvs + r
            return sends, recvs

        side = _Side(arrs, [jax.ShapeDtypeStruct(a.shape, a.dtype) for a in arrs], {i: i for i in range(len(arrs))}, 3 * len(jobs), make)
        return side, names

    def side(self, kernel_name):
        if kernel_name == "f_inproj":
            side, self._names = self._slot_side([("w_out", _job_gather_ici), ("w_mlp_in", _job_gather_ici)])
        elif kernel_name == "f_ssdconv":
            side, self._names = self._slot_side([("w_mlp_out", _job_gather_ici), ("w_out", _job_gather_sibling),
                                                 ("w_mlp_in", _job_gather_sibling)])
        elif kernel_name == "f_lruconv":
            side, self._names = self._slot_side([("w_mlp_out", _job_gather_sibling)])
        elif kernel_name in ("b_mlpin_dx", "b_ssd", "b_inproj_dx") and self.pending:
            sent = self._sent = list(self.pending)
            self.pending = []

            def make(i_refs, o_refs, send_sems, recv_sems):
                sends = []
                for q in range(len(sent)):
                    sends += _job_reduce_ici(i_refs[q], o_refs[3 * q:3 * q + 3], send_sems, recv_sems, 3 * q)[0]
                return sends, sends

            outs = [jax.ShapeDtypeStruct(s.shape[1:], s.dtype) for _, s in sent for _ in range(3)]
            side = _Side([s for _, s in sent], outs, {}, 3 * len(sent), make)
        elif kernel_name == "b_prenorm":
            names = self._share = list(self.totals)
            ts = [self.totals[n] for n in names]

            def make(i_refs, o_refs, send_sems, recv_sems):
                x, y, c = _place()
                sends, recvs = [], []
                for w in range(len(ts)):
                    kw = dict(send_sem=send_sems.at[w], recv_sem=recv_sems.at[w], device_id=(x, y, 1 - c), device_id_type=MESH)
                    sends.append(pltpu.make_async_remote_copy(src_ref=i_refs[w].at[c], dst_ref=o_refs[w].at[c], **kw))
                    recvs.append(pltpu.make_async_remote_copy(src_ref=i_refs[w].at[c], dst_ref=o_refs[w].at[1 - c], **kw))
                return sends, recvs

            side = _Side(ts, [jax.ShapeDtypeStruct(t.shape, t.dtype) for t in ts], {i: i for i in range(len(ts))}, len(ts), make)
        else:
            return None
        return side

    def done(self, kernel_name, side_outs):
        if kernel_name == "b_prenorm":
            self.shared = {n: o.reshape(self.shapes[n]) for n, o in zip(self._share, side_outs)}
            return
        if kernel_name.startswith("f_"):
            self.slots.update(zip(self._names, side_outs))
            return
        for q, (name, s4) in enumerate(self._sent):
            g3 = side_outs[3 * q:3 * q + 3]
            _, rh, cc = s4.shape
            row = ((TB, cc), lambda i, s: (i, 0))
            self.totals[name] = _blocks(
                lambda o, r0, r1, r2: (((o.astype(F32) + r0.astype(F32)) + r1.astype(F32)) + r2.astype(F32),),
                [(s4, (None, TB, cc), lambda i, s: (s[0], i, 0)), (g3[0],) + row, (g3[1],) + row, (g3[2],) + row],
                [(jax.ShapeDtypeStruct((2, rh, cc), F32), (None, TB, cc), lambda i, s: (s[1], i, 0))],
                grid=(rh // TB,), name="add_chips_" + name, prefetch=self.where)[0]

    def grad(self, name, g):
        if name == "w_in":
            dw_main, dw_dt = g
            c_dt = D_SSD + D_XBC
            g = jnp.concatenate([dw_main[:, :c_dt], dw_dt[:, :DT_W], dw_main[:, c_dt:]], axis=1)
            g = g.reshape(g.shape[0], 4, -1).transpose(1, 0, 2)
        g = g.reshape(4, 2, g.shape[1] // 2, g.shape[2])
        (t,) = _rs_sibling([g], name="reduce_sibling_" + name)
        _, rh, cc = t.shape
        s4 = _blocks(lambda u, v: (u.astype(F32) + v.astype(F32),),
                     [(g, (None, None, TB, cc), lambda q, i, s: (q, s[1], i, 0)), (t, (None, TB, cc), lambda q, i, s: (q, i, 0))],
                     [(jax.ShapeDtypeStruct(t.shape, BF16), (None, TB, cc), lambda q, i, s: (q, i, 0))],
                     grid=(4, rh // TB), name="add_sibling_" + name, prefetch=self.where)[0]
        self.pending.append((name, s4))

    def finish(self):
        assert not self.pending and len(self.shared) == len(self.shapes)
        return self.shared


def kernel(x, pre_mix_norm, w_in, ssd_conv_w, ssd_conv_b, ssd_dt_bias, ssd_a_log, ssd_d, ssd_norm, lru_conv_w, lru_conv_b, lru_w_a, lru_b_a, lru_w_x, lru_b_x, lru_lambda, lru_norm, w_out, post_mix_norm, pre_mlp_norm, w_mlp_in, w_mlp_out, post_mlp_norm, loss_target, m_pre_mix_norm, m_w_in, m_ssd_conv_w, m_ssd_conv_b, m_ssd_dt_bias, m_ssd_a_log, m_ssd_d, m_ssd_norm, m_lru_conv_w, m_lru_conv_b, m_lru_w_a, m_lru_b_a, m_lru_w_x, m_lru_b_x, m_lru_lambda, m_lru_norm, m_w_out, m_post_mix_norm, m_pre_mlp_norm, m_w_mlp_in, m_w_mlp_out, m_post_mlp_norm, v_pre_mix_norm, v_w_in, v_ssd_conv_w, v_ssd_conv_b, v_ssd_dt_bias, v_ssd_a_log, v_ssd_d, v_ssd_norm, v_lru_conv_w, v_lru_conv_b, v_lru_w_a, v_lru_b_a, v_lru_w_x, v_lru_b_x, v_lru_lambda, v_lru_norm, v_w_out, v_post_mix_norm, v_pre_mlp_norm, v_w_mlp_in, v_w_mlp_out, v_post_mlp_norm):
    a = dict(locals())
    j = 2 * lax.axis_index("x") + lax.axis_index("y")
    D = x.shape[-1]
    c_dt = D_SSD + D_XBC

    core = lax.axis_index("c")
    where = jnp.stack([j, core, 2 * j + core]).astype(jnp.int32)
    slots = {}
    for name in LARGE:
        w = a[name][0]
        r, cc = w.shape
        slots[name] = _blocks(lambda t: (t,), [(w, (TB, cc), lambda i, s: (i, 0))],
                              [(jax.ShapeDtypeStruct((4, r, cc), BF16), (None, TB, cc), lambda i, s: (s[0], i, 0))],
                              grid=(r // TB,), name="cast_" + name, prefetch=where)[0]
    (g_in,) = _gather_weights([slots.pop("w_in")], name="gather_w_in")
    w_full = jnp.transpose(g_in, (1, 0, 2)).reshape(D, -1)
    w_main = jnp.concatenate([w_full[:, :c_dt], w_full[:, c_dt + DT_W:]], axis=1)
    w_dt = jnp.pad(w_full[:, c_dt:c_dt + DT_W], ((0, 0), (0, 128 - DT_W)))
    taps = jnp.concatenate([ssd_conv_w[0].reshape(-1, 128), lru_conv_w[0].reshape(-1, 128)], axis=0)
    taps = _allgather8(taps, name="gather_taps").reshape(8, taps.shape[0], 128)[0::2]
    n_ssd = ssd_conv_w.shape[1] * ssd_conv_w.shape[2] // 128
    ssd_taps = taps[:, :n_ssd].reshape(4, CONV_W, -1).transpose(1, 0, 2).reshape(CONV_W, -1)
    lru_taps = taps[:, n_ssd:].reshape(4, CONV_W, -1).transpose(1, 0, 2).reshape(CONV_W, -1)

    def row128(v):
        return jnp.pad(v, ((0, 0), (0, 128 - v.shape[1])))

    p = dict(pre_mix_norm=pre_mix_norm, ssd_conv_w=ssd_taps, ssd_conv_b=ssd_conv_b,
             dtb=row128(ssd_dt_bias), alog=row128(ssd_a_log), drow=row128(ssd_d), ssd_norm=ssd_norm,
             lru_conv_w=lru_taps, lru_conv_b=lru_conv_b, lru_w_a=lru_w_a[0], lru_b_a=lru_b_a.reshape(1, -1),
             lru_w_x=lru_w_x[0], lru_b_x=lru_b_x.reshape(1, -1), lru_lambda=lru_lambda, lru_norm=lru_norm,
             post_mix_norm=post_mix_norm, pre_mlp_norm=pre_mlp_norm, post_mlp_norm=post_mlp_norm)
    plan = _DistPlan(slots, w_main, w_dt, where, {n: a[n].shape for n in LARGE})
    grad_x, small = _local_step(x[0], loss_target[0], p, plan)

    large_grads = plan.finish()

    packed = _pack([small[n] for n, _ in SMALL_GRADS])
    n_rows = packed.shape[0]
    gathered = _blocks(lambda t: (t,), [(packed, (n_rows, 128), lambda i, s: (0, 0))],
                       [(jax.ShapeDtypeStruct((8, n_rows, 128), F32), (None, n_rows, 128), lambda i, s: (s[2], 0, 0))],
                       grid=(1,), name="place_small", prefetch=where)[0]

    def gather_small_side(buf, stage):
        def make(i_refs, o_refs, send_sems, recv_sems):
            x, y, c = _place()
            src, dst = i_refs[0], o_refs[0]
            sends, recvs = [], []

            def add(k, block, to, frm):
                kw = dict(send_sem=send_sems.at[k], recv_sem=recv_sems.at[k], device_id=to, device_id_type=MESH)
                sends.append(pltpu.make_async_remote_copy(src_ref=src.at[block], dst_ref=dst.at[block], **kw))
                recvs.append(pltpu.make_async_remote_copy(src_ref=src.at[frm], dst_ref=dst.at[frm], **kw))

            if stage == 0:
                add(0, 4 * x + 2 * y + c, (x, y, 1 - c), 4 * x + 2 * y + 1 - c)
            for k, (px, py) in enumerate(_other_chips(x, y)):
                if stage == 0:
                    add(1 + k, 4 * x + 2 * y + c, (px, py, c), 4 * px + 2 * py + c)
                else:
                    add(k, 4 * px + 2 * py + c, (x, y, 1 - c), 4 * px + 2 * py + 1 - c)
            return sends, recvs

        return _Side([buf], [jax.ShapeDtypeStruct(buf.shape, buf.dtype)], {0: 0}, 4 if stage == 0 else 3, make)

    delta, new_m, new_v = {}, {}, {}
    for stage, name in enumerate(["w_mlp_in", "w_in", "w_mlp_out", "w_out"]):
        w = a[name][0]
        cc = w.shape[1]
        outs = _rowwise(_adamw, [w, large_grads[name][0], a["m_" + name][0], a["v_" + name][0]], [], [(cc, F32)] * 3, [],
                        name="adamw_" + name, tr=128, sub=8, side=gather_small_side(gathered, stage) if stage < 2 else None)
        if stage < 2:
            outs, (gathered,) = outs
        delta[name], new_m[name], new_v[name] = [o[None] for o in outs]

    total = _sum8(gathered, name="sum_small")
    tot = dict(zip([n for n, _ in SMALL_GRADS], _unpack(total, [s for _, s in SMALL_GRADS])))
    loss = tot["loss"][0, 0]
    n_sc, n_lc = ssd_conv_w.shape[2], lru_conv_w.shape[2]
    grads = dict(
        pre_mix_norm=tot["pre_mix_norm"],
        ssd_conv_w=lax.dynamic_slice(tot["ssd_conv"][:CONV_W], (0, j * n_sc), (CONV_W, n_sc))[None],
        ssd_conv_b=tot["ssd_conv"][CONV_W:CONV_W + 1],
        ssd_dt_bias=tot["ssd_par"][0:1, :DT_W], ssd_a_log=tot["ssd_par"][1:2, :DT_W], ssd_d=tot["ssd_par"][2:3, :DT_W],
        ssd_norm=tot["ssd_norm"],
        lru_conv_w=lax.dynamic_slice(tot["lru_conv"][:CONV_W], (0, j * n_lc), (CONV_W, n_lc))[None],
        lru_conv_b=tot["lru_conv"][CONV_W:CONV_W + 1],
        lru_w_a=tot["lru_w_a"][None], lru_b_a=tot["lru_b_a"].reshape(lru_b_a.shape),
        lru_w_x=tot["lru_w_x"][None], lru_b_x=tot["lru_b_x"].reshape(lru_b_x.shape),
        lru_lambda=tot["lru_lambda"], lru_norm=tot["lru_norm"], post_mix_norm=tot["post_mix_norm"],
        pre_mlp_norm=tot["pre_mlp_norm"], post_mlp_norm=tot["post_mlp_norm"])

    grads.update(large_grads)

    small_w = [n for n in WEIGHTS if n not in LARGE]
    packs = [_pack([d[n] for n in small_w]) for d in (a, grads, {n: a["m_" + n] for n in small_w}, {n: a["v_" + n] for n in small_w})]
    outs = _rowwise(_adamw, packs, [], [(128, F32)] * 3, [], name="adamw_small", tr=packs[0].shape[0], sub=8)
    for d, o in zip((delta, new_m, new_v), outs):
        d.update(zip(small_w, _unpack(o, [a[n].shape for n in small_w])))

    return (loss, grad_x[None], *[grads[n] for n in WEIGHTS], *[delta[n] for n in WEIGHTS],
            *[new_m[n] for n in WEIGHTS], *[new_v[n] for n in WEIGHTS])
```

```python
import functools

import jax
import jax.numpy as jnp
from jax import lax
from jax.experimental import pallas as pl
from jax.experimental.pallas import tpu as pltpu

F32 = jnp.float32
BF16 = jnp.bfloat16
MESH = pl.DeviceIdType.MESH

EPS = 1e-6
LRU_C = 8.0
ADAM_LR = 0.001
ADAM_B1 = 0.9
ADAM_B2 = 0.999
ADAM_EPS = 1e-08
ADAM_WD = 0.01
ADAM_STEP = 10

N_GROUPS = 8
HEADS_PER_GROUP = 4
HEAD_DIM = 64
GROUP_W = HEADS_PER_GROUP * HEAD_DIM
STATE = 128
LRU_HEADS = 16
LRU_BLOCK = 128
CONV_W = 4
SSD_CHUNK = 128
HALO = 8

VMEM_LIMIT = 48 * 1024 * 1024


def _params(sem=None):
    return pltpu.CompilerParams(dimension_semantics=sem, vmem_limit_bytes=VMEM_LIMIT)


@jax.custom_jvp
def _log1p(x):
    u = 1.0 + x
    d = u - 1.0
    return jnp.where(d == 0.0, x, jnp.log(u) * (x / jnp.where(d == 0.0, 1.0, d)))


@_log1p.defjvp
def _log1p_jvp(primals, tangents):
    (x,), (t,) = primals, tangents
    return _log1p(x), t / (1.0 + x)


@jax.custom_jvp
def _expm1(x):
    u = jnp.exp(x)
    lu = jnp.log(u)
    safe = jnp.where(lu == 0.0, 1.0, lu)
    y = (u - 1.0) * (x / safe)
    y = jnp.where(lu == 0.0, x, y)
    return jnp.where(u == 0.0, -1.0, y)


@_expm1.defjvp
def _expm1_jvp(primals, tangents):
    (x,), (t,) = primals, tangents
    return _expm1(x), t * jnp.exp(x)


def _softplus(x):
    return jnp.maximum(x, 0.0) + _log1p(jnp.exp(-jnp.abs(x)))


def _sigmoid(x):
    return 1.0 / (1.0 + jnp.exp(-x))


def _silu(x):
    return x * _sigmoid(x)


def _gelu(x):
    c = 0.7978845608028654
    return 0.5 * x * (1.0 + jnp.tanh(c * (x + 0.044715 * (x * x * x))))


def _rms(x, g):
    return x * lax.rsqrt(jnp.mean(x * x, axis=-1, keepdims=True) + EPS) * g


def _dot(a, b, dims):
    return lax.dot_general(a.astype(BF16), b.astype(BF16), (dims, ((), ())), preferred_element_type=F32)


_NN = ((1,), (0,))
_NT = ((1,), (1,))
_TN = ((0,), (0,))


ANY = pl.BlockSpec(memory_space=pl.ANY)


class _Side:
    def __init__(self, ins, outs, aliases, n_sems, make):
        self.ins, self.outs, self.aliases, self.n_sems, self.make = ins, outs, aliases, n_sems, make


def _call(body, args, *, name, grid, in_specs, out_specs, out_shape, scratch_shapes=(), sem=None, side=None):
    in_specs, out_specs, out_shape, scratch_shapes = list(in_specs), list(out_specs), list(out_shape), list(scratch_shapes)
    if side is None:
        outs = pl.pallas_call(body, name=name, grid=grid, in_specs=in_specs, out_specs=out_specs, out_shape=out_shape,
                              scratch_shapes=scratch_shapes, compiler_params=_params(sem))(*args)
        return list(outs), []
    n_in, n_out, n_scr, si, so = len(in_specs), len(out_specs), len(scratch_shapes), len(side.ins), len(side.outs)

    def full(*refs):
        s_in = refs[n_in:n_in + si]
        o0 = n_in + si
        s_out = refs[o0 + n_out:o0 + n_out + so]
        scr = refs[o0 + n_out + so:o0 + n_out + so + n_scr]
        send_sems, recv_sems = refs[-2], refs[-1]
        ids = [pl.program_id(d) for d in range(len(grid))]
        first = functools.reduce(jnp.logical_and, [i == 0 for i in ids])
        last = functools.reduce(jnp.logical_and, [i == g - 1 for i, g in zip(ids, grid)])

        @pl.when(first)
        def _():
            for d in side.make(s_in, s_out, send_sems, recv_sems)[0]:
                d.start()

        body(*refs[:n_in], *refs[o0:o0 + n_out], *scr)

        @pl.when(last)
        def _():
            sends, recvs = side.make(s_in, s_out, send_sems, recv_sems)
            for d in recvs:
                d.wait_recv()
            for d in sends:
                d.wait_send()

    outs = pl.pallas_call(
        full, name=name, grid=grid, in_specs=in_specs + [ANY] * si, out_specs=out_specs + [ANY] * so,
        out_shape=out_shape + list(side.outs),
        scratch_shapes=scratch_shapes + [pltpu.SemaphoreType.DMA((side.n_sems,)), pltpu.SemaphoreType.DMA((side.n_sems,))],
        input_output_aliases={n_in + i: n_out + o for i, o in side.aliases.items()},
        compiler_params=_params(("arbitrary",) * len(grid)),
    )(*args, *side.ins)
    return list(outs[:n_out]), list(outs[n_out:])


def _matmul(a, b, *, mode, m, n, k, tm, tn, tk, out_dtypes, name, a_spec=None, b_spec=None,
            out_specs=None, out_shapes=None, extras=(), epilogue=None, side=None):
    tm, tn, tk = min(tm, m), min(tn, n), min(tk, k)
    assert m % tm == 0 and n % tn == 0 and k % tk == 0, (name, m, n, k, tm, tn, tk)
    nk = k // tk
    dims = {"nn": _NN, "nt": _NT, "tn": _TN}[mode]
    if a_spec is None:
        a_spec = pl.BlockSpec((tk, tm), lambda i, j, kk: (kk, i)) if mode == "tn" else pl.BlockSpec((tm, tk), lambda i, j, kk: (i, kk))
    if b_spec is None:
        b_spec = pl.BlockSpec((tn, tk), lambda i, j, kk: (j, kk)) if mode == "nt" else pl.BlockSpec((tk, tn), lambda i, j, kk: (kk, j))
    tile = pl.BlockSpec((tm, tn), lambda i, j, kk: (i, j))
    if out_specs is None:
        out_specs = [tile for _ in out_dtypes]
    if out_shapes is None:
        out_shapes = [jax.ShapeDtypeStruct((m, n), d) for d in out_dtypes]
    n_ex, n_out = len(extras), len(out_dtypes)

    def body(*refs):
        a_ref, b_ref = refs[0], refs[1]
        ex_refs = refs[2:2 + n_ex]
        o_refs = refs[2 + n_ex:2 + n_ex + n_out]
        def finish(r):
            outs = epilogue(r, *[e[...] for e in ex_refs]) if epilogue is not None else (r,)
            for o_ref, o in zip(o_refs, outs):
                o_ref[...] = o.astype(o_ref.dtype)

        if nk == 1:
            finish(_dot(a_ref[...], b_ref[...], dims))
            return
        acc = refs[-1]
        kk = pl.program_id(2)

        @pl.when(kk == 0)
        def _():
            acc[...] = _dot(a_ref[...], b_ref[...], dims)

        @pl.when(kk > 0)
        def _():
            acc[...] += _dot(a_ref[...], b_ref[...], dims)

        @pl.when(kk == nk - 1)
        def _():
            finish(acc[...])

    outs, side_outs = _call(
        body, (a, b, *extras), name=name, grid=(m // tm, n // tn, nk),
        in_specs=[a_spec, b_spec] + [tile for _ in extras], out_specs=out_specs, out_shape=out_shapes,
        scratch_shapes=[] if nk == 1 else [pltpu.VMEM((tm, tn), F32)], sem=("parallel", "parallel", "arbitrary"), side=side)
    return outs if side is None else (outs, side_outs)


def _rowwise(fn, rows, bcast, out_rows, out_acc, *, name, tr, sub, side=None):
    rows = [r if isinstance(r, tuple) else (r, 0, r.shape[1]) for r in rows]
    row_specs = []
    for arr, c0, w in rows:
        assert c0 % w == 0, (name, c0, w)
        row_specs.append((w, c0 // w))
    rows = [r[0] for r in rows]
    L = rows[0].shape[0]
    tr = min(tr, L)
    sub = min(sub, tr)
    assert L % tr == 0 and tr % sub == 0, (name, L, tr, sub)
    n_r, n_b, n_or, n_oa = len(rows), len(bcast), len(out_rows), len(out_acc)

    def body(*refs):
        r_refs = refs[:n_r]
        b_refs = refs[n_r:n_r + n_b]
        or_refs = refs[n_r + n_b:n_r + n_b + n_or]
        oa_refs = refs[n_r + n_b + n_or:]
        i = pl.program_id(0)

        @pl.when(i == 0)
        def _():
            for o in oa_refs:
                o[...] = jnp.zeros_like(o)

        bvals = [b[...] for b in b_refs]

        def step(s, carry):
            r0 = pl.multiple_of(s * sub, sub)
            tiles = [r[pl.ds(r0, sub), :] for r in r_refs]
            outs = fn(*tiles, *bvals)
            for o_ref, o in zip(or_refs, outs[:n_or]):
                o_ref[pl.ds(r0, sub), :] = o.astype(o_ref.dtype)
            for o_ref, o in zip(oa_refs, outs[n_or:]):
                o_ref[...] += o
            return carry

        if tr == sub:
            step(0, 0)
        else:
            lax.fori_loop(0, tr // sub, step, 0)

    def whole(shape):
        nd = len(shape)
        return pl.BlockSpec(shape, lambda i, _n=nd: (0,) * _n)

    outs, side_outs = _call(
        body, (*rows, *bcast), name=name, grid=(L // tr,),
        in_specs=[pl.BlockSpec((tr, w), lambda i, _c=cb: (i, _c)) for w, cb in row_specs] + [whole(b.shape) for b in bcast],
        out_specs=[pl.BlockSpec((tr, c), lambda i: (i, 0)) for c, _ in out_rows] + [whole(s) for s in out_acc],
        out_shape=[jax.ShapeDtypeStruct((L, c), d) for c, d in out_rows] + [jax.ShapeDtypeStruct(s, F32) for s in out_acc],
        sem=("arbitrary",), side=side)
    return outs if side is None else (outs, side_outs)


def _colsum(x):
    return jnp.sum(x, axis=0, keepdims=True)


def _grouped(fn, rows, params, out_rows, out_acc, *, gw, name, tr, side=None):
    rows = [r if isinstance(r, tuple) else (r, 0) for r in rows]
    L = rows[0][0].shape[0]
    tr = min(tr, L)
    assert L % tr == 0
    G = None
    for p in params:
        G = p.shape[0] if p.ndim == 3 else p.shape[1] // gw
    cw = G * gw
    n_r, n_p, n_or, n_oa = len(rows), len(params), len(out_rows), len(out_acc)

    def pick(ref, g):
        return ref[g] if len(ref.shape) == 3 else ref[:, g * gw:(g + 1) * gw]

    def body(*refs):
        r_refs = refs[:n_r]
        p_refs = refs[n_r:n_r + n_p]
        or_refs = refs[n_r + n_p:n_r + n_p + n_or]
        oa_refs = refs[n_r + n_p + n_or:]

        @pl.when(pl.program_id(0) == 0)
        def _():
            for o in oa_refs:
                o[...] = jnp.zeros_like(o)

        for g in range(G):
            outs = fn(*[pick(r, g) for r in r_refs], *[pick(p, g) for p in p_refs])
            for o_ref, o in zip(or_refs, outs[:n_or]):
                o_ref[:, g * gw:(g + 1) * gw] = o.astype(o_ref.dtype)
            for o_ref, o in zip(oa_refs, outs[n_or:]):
                if len(o_ref.shape) == 3:
                    o_ref[g] += o
                else:
                    o_ref[:, g * gw:(g + 1) * gw] += o

    def whole(shape):
        nd = len(shape)
        return pl.BlockSpec(shape, lambda i, _n=nd: (0,) * _n)

    for _, c0 in rows:
        assert c0 % cw == 0
    outs, side_outs = _call(
        body, (*[r[0] for r in rows], *params), name=name, grid=(L // tr,),
        in_specs=[pl.BlockSpec((tr, cw), lambda i, _c=c0 // cw: (i, _c)) for _, c0 in rows] + [whole(p.shape) for p in params],
        out_specs=[pl.BlockSpec((tr, cw), lambda i: (i, 0)) for _ in out_rows] + [whole(s) for s in out_acc],
        out_shape=[jax.ShapeDtypeStruct((L, cw), d) for d in out_rows] + [jax.ShapeDtypeStruct(s, F32) for s in out_acc],
        sem=("arbitrary",), side=side)
    return outs if side is None else (outs, side_outs)


def _dsilu(p):
    s = _sigmoid(p)
    return s + p * s * (1.0 - s)


CONV_RC, CONV_CC = 32, 512


def _past_window(x_ref, head, r0, k, cs):
    if r0 == 0:
        return head[HALO - 3 + k:HALO - 3 + k + CONV_RC, cs]
    return x_ref[r0 - 3 + k:r0 - 3 + k + CONV_RC, cs]


def _conv_fwd(x, c0, C, w, b, *, silu, name, tr=512, cb=1024, side=None):
    L = x.shape[0]
    tr = min(tr, L)
    nb, hb = L // tr, tr // HALO
    assert L % tr == 0 and C % cb == 0 and c0 % cb == 0 and tr % CONV_RC == 0 and cb % CONV_CC == 0
    n_out = 2 if silu else 1

    def body(x_ref, h_ref, w_ref, b_ref, *rest):
        o_refs, head = rest[:n_out], rest[n_out]
        head[0:HALO, :] = h_ref[...]

        @pl.when(pl.program_id(0) == 0)
        def _():
            head[0:HALO, :] = jnp.zeros((HALO, cb), F32)

        head[HALO:, :] = x_ref[0:CONV_RC, :]
        for cc in range(cb // CONV_CC):
            cs = slice(cc * CONV_CC, (cc + 1) * CONV_CC)
            wv, bv = w_ref[:, cs], b_ref[:, cs]
            for r0 in range(0, tr, CONV_RC):
                y = bv
                for k in range(CONV_W):
                    y = y + wv[k:k + 1, :] * _past_window(x_ref, head, r0, k, cs)
                o_refs[0][r0:r0 + CONV_RC, cs] = y
                if silu:
                    o_refs[1][r0:r0 + CONV_RC, cs] = _silu(y)

    outs, side_outs = _call(
        body, (x, x, w, b), name=name, grid=(nb, C // cb),
        in_specs=[
            pl.BlockSpec((tr, cb), lambda i, j: (i, c0 // cb + j)),
            pl.BlockSpec((HALO, cb), lambda i, j: (jnp.maximum(i * hb - 1, 0), c0 // cb + j)),
            pl.BlockSpec((CONV_W, cb), lambda i, j: (0, j)),
            pl.BlockSpec((1, cb), lambda i, j: (0, j)),
        ],
        out_specs=[pl.BlockSpec((tr, cb), lambda i, j: (i, j)) for _ in range(n_out)],
        out_shape=[jax.ShapeDtypeStruct((L, C), F32) for _ in range(n_out)],
        scratch_shapes=[pltpu.VMEM((HALO + CONV_RC, cb), F32)], sem=("parallel", "parallel"), side=side)
    return outs if side is None else (outs, side_outs)


def _fold8(v):
    return (v[0:8] + v[8:16]) + (v[16:24] + v[24:32])


def _conv_bwd(dact, dc0, pre, x, xc0, C, w, *, silu, name, tr=512, cb=1024):
    L = x.shape[0]
    tr = min(tr, L)
    nb, hb = L // tr, tr // HALO
    last_h = L // HALO - 1
    assert L % tr == 0 and C % cb == 0 and tr % CONV_RC == 0 and cb % CONV_CC == 0

    def body(*refs):
        if silu:
            d_ref, dh_ref, p_ref, ph_ref, x_ref, xh_ref, w_ref, dx_ref, dwb_ref, dp, head = refs
        else:
            d_ref, dh_ref, x_ref, xh_ref, w_ref, dx_ref, dwb_ref, dp, head = refs
        i = pl.program_id(1)
        dp[tr:, :] = dh_ref[...] * _dsilu(ph_ref[...]) if silu else dh_ref[...]

        @pl.when(i == nb - 1)
        def _():
            dp[tr:, :] = jnp.zeros((HALO, cb), F32)

        head[0:HALO, :] = xh_ref[...]

        @pl.when(i == 0)
        def _():
            head[0:HALO, :] = jnp.zeros((HALO, cb), F32)
            dwb_ref[...] = jnp.zeros_like(dwb_ref)

        head[HALO:, :] = x_ref[0:CONV_RC, :]
        for cc in range(cb // CONV_CC):
            cs = slice(cc * CONV_CC, (cc + 1) * CONV_CC)
            for r0 in range(0, tr, CONV_RC):
                rs = slice(r0, r0 + CONV_RC)
                dp[rs, cs] = d_ref[rs, cs] * _dsilu(p_ref[rs, cs]) if silu else d_ref[rs, cs]
        for cc in range(cb // CONV_CC):
            cs = slice(cc * CONV_CC, (cc + 1) * CONV_CC)
            wv = w_ref[:, cs]
            acc = [jnp.zeros((8, CONV_CC), F32) for _ in range(CONV_W + 1)]
            for r0 in range(0, tr, CONV_RC):
                dm = dp[r0:r0 + CONV_RC, cs]
                dx = jnp.zeros((CONV_RC, CONV_CC), F32)
                for k in range(CONV_W):
                    dx = dx + wv[k:k + 1, :] * dp[r0 + 3 - k:r0 + 3 - k + CONV_RC, cs]
                    acc[k] = acc[k] + _fold8(dm * _past_window(x_ref, head, r0, k, cs))
                acc[CONV_W] = acc[CONV_W] + _fold8(dm)
                dx_ref[r0:r0 + CONV_RC, cs] = dx.astype(dx_ref.dtype)
            for k in range(CONV_W + 1):
                dwb_ref[k:k + 1, cs] += _colsum(acc[k])

    def main(c):
        return pl.BlockSpec((tr, cb), lambda j, i: (i, c // cb + j))

    def nxt(c):
        return pl.BlockSpec((HALO, cb), lambda j, i: (jnp.minimum((i + 1) * hb, last_h), c // cb + j))

    in_specs = [main(dc0), nxt(dc0)]
    args = [dact, dact]
    if silu:
        in_specs += [main(0), nxt(0)]
        args += [pre, pre]
    in_specs += [main(xc0), pl.BlockSpec((HALO, cb), lambda j, i: (jnp.maximum(i * hb - 1, 0), xc0 // cb + j)),
                 pl.BlockSpec((CONV_W, cb), lambda j, i: (0, j))]
    args += [x, x, w]
    return pl.pallas_call(
        body, name=name, grid=(C // cb, nb),
        in_specs=in_specs,
        out_specs=[pl.BlockSpec((tr, cb), lambda j, i: (i, j)), pl.BlockSpec((8, cb), lambda j, i: (0, j))],
        out_shape=[jax.ShapeDtypeStruct((L, C), BF16), jax.ShapeDtypeStruct((8, C), F32)],
        scratch_shapes=[pltpu.VMEM((tr + HALO, cb), F32), pltpu.VMEM((HALO + CONV_RC, cb), F32)],
        compiler_params=_params(("parallel", "arbitrary")),
    )(*args)


def _ssd_f1(dtraw, dtb, alog):
    q = dtraw.shape[0]
    dt = _softplus(dtraw + dtb)
    adt = dt * (-jnp.exp(alog))
    tril = (lax.broadcasted_iota(jnp.int32, (q, q), 0) >= lax.broadcasted_iota(jnp.int32, (q, q), 1)).astype(F32)
    acs = lax.dot_general(tril, adt, (_NN, ((), ())), precision=lax.Precision.HIGHEST, preferred_element_type=F32)
    return dt, acs


def _ssd_group(g, x, bm, cm, dt, acs, drow, hp):
    q = x.shape[0]
    lane = lax.broadcasted_iota(jnp.int32, (1, 128), 1)
    sub = lax.broadcasted_iota(jnp.int32, (128, 1), 0)
    head_of = lax.broadcasted_iota(jnp.int32, (1, GROUP_W), 1) // HEAD_DIM
    is_last = (lax.broadcasted_iota(jnp.int32, (q, 1), 0) == q - 1).astype(F32)
    causal = lax.broadcasted_iota(jnp.int32, (q, q), 0) >= lax.broadcasted_iota(jnp.int32, (q, q), 1)
    acs_end = jnp.sum(acs * is_last, axis=0, keepdims=True)
    acs_t = acs.T
    dt_exp = jnp.zeros((q, GROUP_W), F32)
    acs_exp = jnp.zeros((q, GROUP_W), F32)
    end_exp = jnp.zeros((1, GROUP_W), F32)
    d_exp = jnp.zeros((1, GROUP_W), F32)
    heads = []
    for k in range(HEADS_PER_GROUP):
        h = HEADS_PER_GROUP * g + k
        oh = (lane == h).astype(F32)
        mk = (head_of == k).astype(F32)
        acs_col = jnp.sum(acs * oh, axis=1, keepdims=True)
        acs_row = jnp.sum(acs_t * (sub == h).astype(F32), axis=0, keepdims=True)
        dt_exp = dt_exp + jnp.sum(dt * oh, axis=1, keepdims=True) * mk
        acs_exp = acs_exp + acs_col * mk
        end_exp = end_exp + jnp.sum(acs_end * oh, axis=1, keepdims=True) * mk
        d_exp = d_exp + jnp.sum(drow * oh, axis=1, keepdims=True) * mk
        heads.append((acs_col, acs_row, mk))
    xdt = x * dt_exp
    states = _dot(bm, xdt * jnp.exp(end_exp - acs_exp), _TN)
    y = _dot(cm, hp, _NN) * jnp.exp(acs_exp) + x * d_exp
    scores = _dot(cm, bm, _NT)
    for acs_col, acs_row, mk in heads:
        dec = jnp.exp(jnp.where(causal, acs_col - acs_row, -jnp.inf))
        y = y + _dot(scores * dec, xdt * mk, _NN)
    return y, hp * jnp.exp(end_exp) + states


def _ssd_fwd(act, dtraw, dtb, alog, drow, *, name, side=None):
    L = act.shape[0]
    q = min(SSD_CHUNK, L)
    nc = L // q
    d_ssd = N_GROUPS * GROUP_W

    def body(act_ref, dt_ref, dtb_ref, alog_ref, drow_ref, y_ref, hst_ref, h):
        @pl.when(pl.program_id(0) == 0)
        def _():
            h[...] = jnp.zeros_like(h)

        dt, acs = _ssd_f1(dt_ref[...], dtb_ref[...], alog_ref[...])
        drow = drow_ref[...]
        for g in range(N_GROUPS):
            hp = h[g]
            hst_ref[0, g] = hp
            y, hn = _ssd_group(g, act_ref[:, g * GROUP_W:(g + 1) * GROUP_W],
                               act_ref[:, d_ssd + g * STATE:d_ssd + (g + 1) * STATE],
                               act_ref[:, d_ssd + (N_GROUPS + g) * STATE:d_ssd + (N_GROUPS + g + 1) * STATE],
                               dt, acs, drow, hp)
            y_ref[:, g * GROUP_W:(g + 1) * GROUP_W] = y
            h[g] = hn

    row = pl.BlockSpec((1, 128), lambda i: (0, 0))
    outs, side_outs = _call(
        body, (act, dtraw, dtb, alog, drow), name=name, grid=(nc,),
        in_specs=[pl.BlockSpec((q, act.shape[1]), lambda i: (i, 0)), pl.BlockSpec((q, 128), lambda i: (i, 0)), row, row, row],
        out_specs=[pl.BlockSpec((q, d_ssd), lambda i: (i, 0)),
                   pl.BlockSpec((1, N_GROUPS, STATE, GROUP_W), lambda i: (i, 0, 0, 0))],
        out_shape=[jax.ShapeDtypeStruct((L, d_ssd), F32), jax.ShapeDtypeStruct((nc, N_GROUPS, STATE, GROUP_W), F32)],
        scratch_shapes=[pltpu.VMEM((N_GROUPS, STATE, GROUP_W), F32)], sem=("arbitrary",), side=side)
    return outs if side is None else (outs, side_outs)


def _ssd_bwd(act, dtraw, dy, hst, dtb, alog, drow, *, name, side=None):
    L = act.shape[0]
    q = min(SSD_CHUNK, L)
    nc = L // q
    d_ssd = N_GROUPS * GROUP_W

    def body(act_ref, dt_ref, dy_ref, hst_ref, dtb_ref, alog_ref, drow_ref, dact_ref, ddt_ref, dpar_ref, dh):
        @pl.when(pl.program_id(0) == 0)
        def _():
            dh[...] = jnp.zeros_like(dh)
            dpar_ref[...] = jnp.zeros_like(dpar_ref)

        (dt, acs), f1_vjp = jax.vjp(_ssd_f1, dt_ref[...], dtb_ref[...], alog_ref[...])
        drow = drow_ref[...]
        ddt = jnp.zeros_like(dt)
        dacs = jnp.zeros_like(acs)
        ddrow = jnp.zeros_like(drow)
        for g in range(N_GROUPS):
            xs = slice(g * GROUP_W, (g + 1) * GROUP_W)
            bs = slice(d_ssd + g * STATE, d_ssd + (g + 1) * STATE)
            cs = slice(d_ssd + (N_GROUPS + g) * STATE, d_ssd + (N_GROUPS + g + 1) * STATE)
            _, f2_vjp = jax.vjp(functools.partial(_ssd_group, g), act_ref[:, xs], act_ref[:, bs], act_ref[:, cs],
                                dt, acs, drow, hst_ref[0, g])
            dx, dbm, dcm, ddt_g, dacs_g, ddrow_g, dhp = f2_vjp((dy_ref[:, xs], dh[g]))
            dact_ref[:, xs] = dx
            dact_ref[:, bs] = dbm
            dact_ref[:, cs] = dcm
            dh[g] = dhp
            ddt, dacs, ddrow = ddt + ddt_g, dacs + dacs_g, ddrow + ddrow_g
        ddtraw, ddtb, dalog = f1_vjp((ddt, dacs))
        ddt_ref[...] = ddtraw
        dpar_ref[0:1, :] += ddtb
        dpar_ref[1:2, :] += dalog
        dpar_ref[2:3, :] += ddrow

    row = pl.BlockSpec((1, 128), lambda i: (0, 0))
    rev = lambda i: (nc - 1 - i, 0)
    outs, side_outs = _call(
        body, (act, dtraw, dy, hst, dtb, alog, drow), name=name, grid=(nc,),
        in_specs=[pl.BlockSpec((q, act.shape[1]), rev), pl.BlockSpec((q, 128), rev), pl.BlockSpec((q, d_ssd), rev),
                  pl.BlockSpec((1, N_GROUPS, STATE, GROUP_W), lambda i: (nc - 1 - i, 0, 0, 0)), row, row, row],
        out_specs=[pl.BlockSpec((q, act.shape[1]), rev), pl.BlockSpec((q, 128), rev), pl.BlockSpec((8, 128), lambda i: (0, 0))],
        out_shape=[jax.ShapeDtypeStruct(act.shape, F32), jax.ShapeDtypeStruct((L, 128), F32), jax.ShapeDtypeStruct((8, 128), F32)],
        scratch_shapes=[pltpu.VMEM((N_GROUPS, STATE, GROUP_W), F32)], sem=("arbitrary",), side=side)
    return outs if side is None else (outs, side_outs)


def _gate_head(xl, wa, ba, wx, bx, lam):
    r = _sigmoid(_dot(xl, wa, _NN) + ba)
    i = _sigmoid(_dot(xl, wx, _NN) + bx)
    log_a = -LRU_C * r * _softplus(-lam)
    return jnp.exp(log_a), jnp.sqrt(-_expm1(2.0 * log_a)) * (i * xl)


def _gate_head_bwd(xl, da, du, wa, ba, wx, bx, lam):
    _, vjp = jax.vjp(_gate_head, xl, wa, ba, wx, bx, lam)
    dxl, dwa, dba, dwx, dbx, dlam = vjp((da, du))
    return dxl, dwa, dba, dwx, dbx, dlam


def _scan_tile(a, b, rows, reverse):
    for d in (1, 2, 4):
        if reverse:
            keep = rows < 8 - d
            a_sh, b_sh = pltpu.roll(a, 8 - d, 0), pltpu.roll(b, 8 - d, 0)
        else:
            keep = rows >= d
            a_sh, b_sh = pltpu.roll(a, d, 0), pltpu.roll(b, d, 0)
        b = b + a * jnp.where(keep, b_sh, 0.0)
        a = a * jnp.where(keep, a_sh, 1.0)
    return a, b


def _lru_scan_fwd(a, u, *, name, tr=512, cb=1024):
    L, C = a.shape
    tr, cb = min(tr, L), min(cb, C)

    def body(a_ref, u_ref, h_ref, hp_ref, carry):
        @pl.when(pl.program_id(1) == 0)
        def _():
            carry[...] = jnp.zeros_like(carry)

        rows = lax.broadcasted_iota(jnp.int32, (8, cb), 0)

        def tile(t, hc):
            r0 = pl.multiple_of(t * 8, 8)
            pa, hb = _scan_tile(a_ref[pl.ds(r0, 8), :], u_ref[pl.ds(r0, 8), :], rows, False)
            h = hb + pa * hc
            h_ref[pl.ds(r0, 8), :] = h
            hp_ref[pl.ds(r0, 8), :] = jnp.where(rows >= 1, pltpu.roll(h, 1, 0), hc)
            return h[7:8, :]

        carry[...] = lax.fori_loop(0, tr // 8, tile, carry[...])

    blk = pl.BlockSpec((tr, cb), lambda j, i: (i, j))
    return pl.pallas_call(
        body, name=name, grid=(C // cb, L // tr),
        in_specs=[blk, blk], out_specs=[blk, blk],
        out_shape=[jax.ShapeDtypeStruct((L, C), F32), jax.ShapeDtypeStruct((L, C), F32)],
        scratch_shapes=[pltpu.VMEM((1, cb), F32)],
        compiler_params=_params(("parallel", "arbitrary")),
    )(a, u)


def _lru_scan_bwd(a, hprev, dh, *, name, tr=512, cb=1024):
    L, C = a.shape
    tr, cb = min(tr, L), min(cb, C)
    nb = L // tr

    def body(a_ref, hp_ref, dh_ref, da_ref, du_ref, carry):
        @pl.when(pl.program_id(1) == 0)
        def _():
            carry[...] = jnp.zeros_like(carry)

        rows = lax.broadcasted_iota(jnp.int32, (8, cb), 0)

        def tile(t, gc):
            r0 = pl.multiple_of((tr // 8 - 1 - t) * 8, 8)
            av, dv = a_ref[pl.ds(r0, 8), :], dh_ref[pl.ds(r0, 8), :]
            pa, gb = _scan_tile(av, av * dv, rows, True)
            big = gb + pa * gc
            g = dv + jnp.where(rows < 7, pltpu.roll(big, 7, 0), gc)
            du_ref[pl.ds(r0, 8), :] = g
            da_ref[pl.ds(r0, 8), :] = g * hp_ref[pl.ds(r0, 8), :]
            return big[0:1, :]

        carry[...] = lax.fori_loop(0, tr // 8, tile, carry[...])

    blk = pl.BlockSpec((tr, cb), lambda j, i: (nb - 1 - i, j))
    return pl.pallas_call(
        body, name=name, grid=(C // cb, nb),
        in_specs=[blk, blk, blk], out_specs=[blk, blk],
        out_shape=[jax.ShapeDtypeStruct((L, C), F32), jax.ShapeDtypeStruct((L, C), F32)],
        scratch_shapes=[pltpu.VMEM((1, cb), F32)],
        compiler_params=_params(("parallel", "arbitrary")),
    )(a, hprev, dh)


def _ssd_gate(y, z, n):
    v = y * _silu(z)
    return v * lax.rsqrt(jnp.mean(v * v, axis=-1, keepdims=True) + EPS) * n


def _ssd_gate_bwd(y, z, dy, n):
    _, vjp = jax.vjp(_ssd_gate, y, z, n)
    return vjp(dy)


def _lru_out(hl, gate, n):
    return _rms(hl * _gelu(gate), n)


def _lru_out_bwd(hl, gate, dy, n):
    _, vjp = jax.vjp(_lru_out, hl, gate, n)
    return vjp(dy)


def _mid(x, mix, pm, pmlp):
    x1 = x + _rms(mix, pm)
    return x1, _rms(x1, pmlp)


def _mid_bwd(x, mix, dx1p, dh2, pm, pmlp):
    _, vjp = jax.vjp(_mid, x, mix, pm, pmlp)
    dx, dmix, dpm, dpmlp = vjp((dx1p, dh2))
    return dmix, dx, dpm, dpmlp


def _loss_bwd(hm2, x1, tgt, g):
    def lossf(hm2, x1, g):
        e = x1 + _rms(hm2, g) - tgt
        return 0.5 * jnp.sum(jnp.mean(e * e, axis=-1, keepdims=True), axis=0, keepdims=True)

    val, vjp = jax.vjp(lossf, hm2, x1, g)
    dhm2, dx1, dg = vjp(jnp.ones((1, 1), F32))
    return dhm2, dx1, dg, val * jnp.ones((1, 128), F32)


def _in_bwd(x, dh_a, dh_b, dx1, g):
    _, vjp = jax.vjp(_rms, x, g)
    dx, dg = vjp(dh_a + dh_b)
    return dx + dx1, dg


def _adamw(w, g, m, v):
    m = ADAM_B1 * m + (1.0 - ADAM_B1) * g
    v = ADAM_B2 * v + (1.0 - ADAM_B2) * (g * g)
    m_hat = m / (1.0 - ADAM_B1 ** ADAM_STEP)
    v_hat = v / (1.0 - ADAM_B2 ** ADAM_STEP)
    return -ADAM_LR * (m_hat / (jnp.sqrt(v_hat) + ADAM_EPS) + ADAM_WD * w), m, v


class _LocalPlan:
    def __init__(self, p):
        self.p, self.large = p, {}

    def weight(self, name):
        return self.p[name]

    def side(self, kernel_name):
        return None

    def done(self, kernel_name, side_outs):
        pass

    def grad(self, name, g):
        self.large[name] = g


def _local_step(x, tgt, p, plan):
    L, D = x.shape

    def carry(fn, *args, name, **kw):
        side = plan.side(name)
        if side is None:
            return fn(*args, name=name, **kw)
        outs, side_outs = fn(*args, name=name, side=side, **kw)
        plan.done(name, side_outs)
        return outs

    d_ssd, d_xbc, d_lru, d_mix, d_ff = 2048, 4096, 2048, 4096, 8192
    n_main = d_ssd + d_xbc + 2 * d_lru
    c_xbc, c_gate, c_xl = d_ssd, d_ssd + d_xbc, d_ssd + d_xbc + d_lru
    TR, SUB = 256, 32
    mm = dict(tm=1024, tn=1024, tk=2048)

    (h,) = _rowwise(lambda xt, g: (_rms(xt, g),), [x], [p["pre_mix_norm"]], [(D, BF16)], [], name="f_prenorm", tr=TR, sub=SUB)
    w_main, w_dt = plan.weight("w_main"), plan.weight("w_dt")
    (proj,) = carry(_matmul, h, w_main, mode="nn", m=L, n=n_main, k=D, out_dtypes=[F32], name="f_inproj", **mm)
    (dtraw,) = _matmul(h, w_dt, mode="nn", m=L, n=128, k=D, out_dtypes=[F32], name="f_dtproj", **mm)
    pre, act = carry(_conv_fwd, proj, c_xbc, d_xbc, p["ssd_conv_w"], p["ssd_conv_b"], silu=True, name="f_ssdconv")
    (xl,) = carry(_conv_fwd, proj, c_xl, d_lru, p["lru_conv_w"], p["lru_conv_b"], silu=False, name="f_lruconv")
    yraw, hst = carry(_ssd_fwd, act, dtraw, p["dtb"], p["alog"], p["drow"], name="f_ssd")
    (yssd,) = _grouped(lambda y, z, n: (_ssd_gate(y, z, n),), [yraw, (proj, 0)], [p["ssd_norm"]], [BF16], [],
                       gw=GROUP_W, name="f_ssdgate", tr=TR)
    gate_p = [p["lru_w_a"], p["lru_b_a"], p["lru_w_x"], p["lru_b_x"], p["lru_lambda"]]
    a, u = carry(_grouped, _gate_head, [xl], gate_p, [F32, F32], [], gw=LRU_BLOCK, name="f_lrugates", tr=TR)
    hl, hprev = _lru_scan_fwd(a, u, name="f_lruscan")
    (ylru,) = _rowwise(lambda ht, gt, n: (_lru_out(ht, gt, n),), [hl, (proj, c_gate, d_lru)], [p["lru_norm"]],
                       [(d_lru, BF16)], [], name="f_lruout", tr=TR, sub=SUB)
    ycat = jnp.concatenate([yssd, ylru], axis=1)
    w_out = plan.weight("w_out")
    (mix,) = carry(_matmul, ycat, w_out, mode="nn", m=L, n=D, k=d_mix, out_dtypes=[F32], name="f_outproj", **mm)
    x1, h2 = _rowwise(_mid, [x, mix], [p["post_mix_norm"], p["pre_mlp_norm"]], [(D, F32), (D, BF16)], [],
                      name="f_mid", tr=TR, sub=SUB)
    nb_mi = (d_ff // 4) // mm["tn"]
    w_mi = plan.weight("w_mi")
    hm, act2 = carry(_matmul, h2, w_mi, mode="nn", m=L, n=d_ff, k=D, out_dtypes=[BF16, BF16], name="f_mlpin",
                     b_spec=pl.BlockSpec((None, mm["tk"], mm["tn"]), lambda i, j, kk: (j // nb_mi, kk, j % nb_mi)),
                     epilogue=lambda r: (r, jnp.square(jnp.maximum(r, 0.0))), **mm)
    w_mo = plan.weight("w_mo")
    (hm2,) = _matmul(act2, w_mo, mode="nn", m=L, n=D, k=d_ff, out_dtypes=[F32], name="f_mlpout", **mm)

    dhm2, dx1p, d_post_mlp, loss = _rowwise(_loss_bwd, [hm2, x1, tgt], [p["post_mlp_norm"]], [(D, BF16), (D, F32)],
                                            [(1, D), (1, 128)], name="b_loss", tr=TR, sub=SUB)
    (dhm,) = _matmul(dhm2, w_mo, mode="nt", m=L, n=d_ff, k=D, out_dtypes=[BF16], name="b_mlpout_dx", extras=[hm],
                     epilogue=lambda r, hmv: (r * (2.0 * jnp.maximum(hmv.astype(F32), 0.0)),), **mm)
    (dw_mo,) = _matmul(act2, dhm2, mode="tn", m=d_ff, n=D, k=L, out_dtypes=[BF16], name="b_mlpout_dw", **mm)
    plan.grad("w_mlp_out", dw_mo.reshape(4, -1, D))
    kb_mi = (d_ff // 4) // mm["tk"]
    (dh2,) = carry(_matmul, dhm, w_mi, mode="nt", m=L, n=D, k=d_ff, out_dtypes=[F32], name="b_mlpin_dx",
                   b_spec=pl.BlockSpec((None, mm["tn"], mm["tk"]), lambda i, j, kk: (kk // kb_mi, j, kk % kb_mi)), **mm)
    (dw_mi,) = carry(_matmul, h2, dhm, mode="tn", m=D, n=d_ff, k=L, out_dtypes=[BF16], name="b_mlpin_dw",
                     out_specs=[pl.BlockSpec((None, mm["tm"], mm["tn"]), lambda i, j, kk: (j // nb_mi, i, j % nb_mi))],
                     out_shapes=[jax.ShapeDtypeStruct((4, D, d_ff // 4), BF16)], **mm)
    plan.grad("w_mlp_in", dw_mi)
    dmix, dx1, d_post_mix, d_pre_mlp = _rowwise(_mid_bwd, [x, mix, dx1p, dh2], [p["post_mix_norm"], p["pre_mlp_norm"]],
                                                [(D, BF16), (D, F32)], [(1, D), (1, D)], name="b_mid", tr=TR, sub=SUB)
    (dw_out,) = _matmul(ycat, dmix, mode="tn", m=d_mix, n=D, k=L, out_dtypes=[BF16], name="b_outproj_dw", **mm)
    plan.grad("w_out", dw_out.reshape(4, -1, D))
    (dycat,) = _matmul(dmix, w_out, mode="nt", m=L, n=d_mix, k=D, out_dtypes=[F32], name="b_outproj_dx", **mm)
    dhl, dgate, d_lru_norm = _rowwise(_lru_out_bwd, [hl, (proj, c_gate, d_lru), (dycat, d_ssd, d_lru)], [p["lru_norm"]],
                                      [(d_lru, F32), (d_lru, BF16)], [(1, d_lru)], name="b_lruout", tr=TR, sub=SUB)
    da, du = _lru_scan_bwd(a, hprev, dhl, name="b_lruscan")
    dxl, d_wa, d_ba, d_wx, d_bx, d_lam = _grouped(
        _gate_head_bwd, [xl, da, du], gate_p, [F32],
        [(LRU_HEADS, LRU_BLOCK, LRU_BLOCK), (1, d_lru), (LRU_HEADS, LRU_BLOCK, LRU_BLOCK), (1, d_lru), (1, d_lru)],
        gw=LRU_BLOCK, name="b_lrugates", tr=TR)
    dxlru, dwb_lru = _conv_bwd(dxl, 0, None, proj, c_xl, d_lru, p["lru_conv_w"], silu=False, name="b_lruconv")
    dyraw, dz, d_ssd_norm = _grouped(_ssd_gate_bwd, [yraw, (proj, 0), (dycat, 0)], [p["ssd_norm"]], [F32, BF16], [(1, d_ssd)],
                                     gw=GROUP_W, name="b_ssdgate", tr=TR)
    dact, ddtraw, dpar = carry(_ssd_bwd, act, dtraw, dyraw, hst, p["dtb"], p["alog"], p["drow"], name="b_ssd")
    dxbc, dwb_ssd = _conv_bwd(dact, 0, pre, proj, c_xbc, d_xbc, p["ssd_conv_w"], silu=True, name="b_ssdconv")
    dproj = jnp.concatenate([dz, dxbc, dgate, dxlru], axis=1)
    (dw_main,) = _matmul(h, dproj, mode="tn", m=D, n=n_main, k=L, out_dtypes=[BF16], name="b_inproj_dw", **mm)
    (dw_dt,) = _matmul(h, ddtraw, mode="tn", m=D, n=128, k=L, out_dtypes=[BF16], name="b_dtproj_dw", **mm)
    plan.grad("w_in", (dw_main, dw_dt))
    (dh_a,) = carry(_matmul, dproj, w_main, mode="nt", m=L, n=D, k=n_main, out_dtypes=[F32], name="b_inproj_dx", **mm)
    (dh_b,) = _matmul(ddtraw, w_dt, mode="nt", m=L, n=D, k=128, out_dtypes=[F32], name="b_dtproj_dx", **mm)
    grad_x, d_pre_mix = carry(_rowwise, _in_bwd, [x, dh_a, dh_b, dx1], [p["pre_mix_norm"]], [(D, F32)], [(1, D)],
                              name="b_prenorm", tr=TR, sub=SUB)

    small = dict(loss=loss, pre_mix_norm=d_pre_mix, ssd_conv=dwb_ssd, ssd_par=dpar, ssd_norm=d_ssd_norm, lru_conv=dwb_lru,
                 lru_w_a=d_wa, lru_b_a=d_ba, lru_w_x=d_wx, lru_b_x=d_bx, lru_lambda=d_lam, lru_norm=d_lru_norm,
                 post_mix_norm=d_post_mix, pre_mlp_norm=d_pre_mlp, post_mlp_norm=d_post_mlp)
    return grad_x, small


def _place():
    return lax.axis_index("x"), lax.axis_index("y"), lax.axis_index("c")


def _other_chips(x, y):
    return [(1 - x, y), (x, 1 - y), (1 - x, 1 - y)]


def _allgather8(blk, *, name):
    r, n = blk.shape

    def body(x_ref, out_ref, send_sems, recv_sems, local_sem):
        x, y, c = _place()
        me, sibling = (x, y, c), (x, y, 1 - c)
        chips = _other_chips(x, y)

        def rows(px, py, pc):
            return out_ref.at[pl.ds((4 * px + 2 * py + pc) * r, r), :]

        def copy(k, block, to, src=None):
            return pltpu.make_async_remote_copy(
                src_ref=rows(*block) if src is None else src, dst_ref=rows(*block),
                send_sem=send_sems.at[k], recv_sem=recv_sems.at[k], device_id=to, device_id_type=MESH)

        mine = pltpu.make_async_copy(x_ref, rows(*me), local_sem)
        mine.start()
        first = [copy(0, me, sibling, src=x_ref)]
        first += [copy(1 + k, me, (*chip, c), src=x_ref) for k, chip in enumerate(chips)]
        for cp in first:
            cp.start()
        passed = [copy(4 + k, (*chip, c), sibling) for k, chip in enumerate(chips)]
        for k, chip in enumerate(chips):
            copy(1 + k, (*chip, c), me).wait_recv()
            passed[k].start()
        copy(0, sibling, me).wait_recv()
        for k, chip in enumerate(chips):
            copy(4 + k, (*chip, 1 - c), me).wait_recv()
        for cp in first + passed:
            cp.wait_send()
        mine.wait()

    return pl.pallas_call(
        body, name=name,
        out_shape=jax.ShapeDtypeStruct((8 * r, n), blk.dtype),
        in_specs=[pl.BlockSpec(memory_space=pltpu.VMEM)], out_specs=pl.BlockSpec(memory_space=pltpu.VMEM),
        scratch_shapes=[pltpu.SemaphoreType.DMA((7,)), pltpu.SemaphoreType.DMA((7,)), pltpu.SemaphoreType.DMA],
        compiler_params=pltpu.CompilerParams(vmem_limit_bytes=VMEM_LIMIT),
    )(blk)


def _sum8(g, *, name, tr=232):
    _, r, n = g.shape
    tr = min(tr, r)
    assert r % tr == 0

    def body(g_ref, o_ref):
        s = g_ref[0]
        for k in range(1, 8):
            s = s + g_ref[k]
        o_ref[...] = s

    return pl.pallas_call(
        body, name=name, grid=(r // tr,),
        in_specs=[pl.BlockSpec((8, tr, n), lambda i: (0, i, 0))], out_specs=pl.BlockSpec((tr, n), lambda i: (i, 0)),
        out_shape=jax.ShapeDtypeStruct((r, n), g.dtype), compiler_params=_params(("parallel",)),
    )(g)


def _blocks(fn, ins, outs, *, grid, name, prefetch=None, aliases=None):
    n_in = len(ins)

    def body(*refs):
        if prefetch is not None:
            refs = refs[1:]
        res = fn(*[r[...] for r in refs[:n_in]])
        for o_ref, o in zip(refs[n_in:], res):
            o_ref[...] = o.astype(o_ref.dtype)

    in_specs = [pl.BlockSpec(b, m) for _, b, m in ins]
    out_specs = [pl.BlockSpec(b, m) for _, b, m in outs]
    kw = dict(name=name, out_shape=[s for s, _, _ in outs], input_output_aliases=aliases or {},
              compiler_params=_params(("arbitrary",) * len(grid)))
    arrs = [a for a, _, _ in ins]
    if prefetch is None:
        return pl.pallas_call(body, grid=grid, in_specs=in_specs, out_specs=out_specs, **kw)(*arrs)
    spec = pltpu.PrefetchScalarGridSpec(num_scalar_prefetch=1, grid=grid, in_specs=in_specs, out_specs=out_specs)
    return pl.pallas_call(body, grid_spec=spec, **kw)(prefetch, *arrs)


def _gather_weights(slots, *, name):
    n = len(slots)

    def body(*refs):
        i_refs, g_refs = refs[:n], refs[n:2 * n]
        send_sems, recv_sems = refs[2 * n:]
        x, y, c = _place()
        j, sibling = 2 * x + y, (x, y, 1 - c)
        chips = _other_chips(x, y)

        def cp(w, k, src, dst, to):
            return pltpu.make_async_remote_copy(src_ref=src, dst_ref=dst, send_sem=send_sems.at[6 * w + k],
                                                recv_sem=recv_sems.at[6 * w + k], device_id=to, device_id_type=MESH)

        def part(ref, chip_idx, hc):
            rh = ref.shape[1] // 2
            return ref.at[chip_idx, pl.ds(hc * rh, rh), :]

        sends = []
        for w in range(n):
            for k, chip in enumerate(chips):
                d = cp(w, k, part(i_refs[w], j, c), part(g_refs[w], j, c), (*chip, c))
                d.start()
                sends.append(d)
        for w in range(n):
            for k, (px, py) in enumerate(chips):
                landed = part(g_refs[w], 2 * px + py, c)
                cp(w, k, landed, landed, (px, py, c)).wait_recv()
                d = cp(w, 3 + k, landed, landed, sibling)
                d.start()
                sends.append(d)
        for w in range(n):
            for k, (px, py) in enumerate(chips):
                other = part(g_refs[w], 2 * px + py, 1 - c)
                cp(w, 3 + k, other, other, sibling).wait_recv()
        for d in sends:
            d.wait_send()

    return pl.pallas_call(
        body, name=name,
        out_shape=[jax.ShapeDtypeStruct(s.shape, s.dtype) for s in slots],
        in_specs=[ANY] * n, out_specs=[ANY] * n, input_output_aliases={w: w for w in range(n)},
        scratch_shapes=[pltpu.SemaphoreType.DMA((6 * n,)), pltpu.SemaphoreType.DMA((6 * n,))],
    )(*slots)


def _rs_sibling(gs, *, name):
    n = len(gs)

    def body(*refs):
        g_refs, got_refs = refs[:n], refs[n:2 * n]
        send_sems, recv_sems = refs[2 * n:]
        x, y, c = _place()
        copies = []
        for w in range(n):
            for s in range(4):
                d = pltpu.make_async_remote_copy(src_ref=g_refs[w].at[s, 1 - c], dst_ref=got_refs[w].at[s],
                                                 send_sem=send_sems.at[4 * w + s], recv_sem=recv_sems.at[4 * w + s],
                                                 device_id=(x, y, 1 - c), device_id_type=MESH)
                d.start()
                copies.append(d)
        for d in copies:
            d.wait_recv()
        for d in copies:
            d.wait_send()

    return pl.pallas_call(
        body, name=name, out_shape=[jax.ShapeDtypeStruct((4,) + g.shape[2:], g.dtype) for g in gs],
        in_specs=[ANY] * n, out_specs=[ANY] * n,
        scratch_shapes=[pltpu.SemaphoreType.DMA((4 * n,)), pltpu.SemaphoreType.DMA((4 * n,))],
    )(*gs)


def _pack(arrs):
    parts = []
    for v in arrs:
        f = v.reshape(-1)
        f = jnp.pad(f, (0, (-f.shape[0]) % 1024))
        parts.append(f.reshape(-1, 128))
    return jnp.concatenate(parts, axis=0)


def _unpack(packed, shapes):
    out, r0 = [], 0
    for s in shapes:
        size = 1
        for d in s:
            size *= d
        nr = (size + 1023) // 1024 * 8
        out.append(packed[r0:r0 + nr].reshape(-1)[:size].reshape(s))
        r0 += nr
    return out


SMALL_GRADS = [("loss", (1, 128)), ("pre_mix_norm", (1, 2048)), ("ssd_conv", (8, 4096)), ("ssd_par", (8, 128)),
               ("ssd_norm", (1, 2048)), ("lru_conv", (8, 2048)), ("lru_w_a", (16, 128, 128)), ("lru_b_a", (1, 2048)),
               ("lru_w_x", (16, 128, 128)), ("lru_b_x", (1, 2048)), ("lru_lambda", (1, 2048)), ("lru_norm", (1, 2048)),
               ("post_mix_norm", (1, 2048)), ("pre_mlp_norm", (1, 2048)), ("post_mlp_norm", (1, 2048))]

WEIGHTS = ['pre_mix_norm', 'w_in', 'ssd_conv_w', 'ssd_conv_b', 'ssd_dt_bias', 'ssd_a_log', 'ssd_d', 'ssd_norm', 'lru_conv_w',
           'lru_conv_b', 'lru_w_a', 'lru_b_a', 'lru_w_x', 'lru_b_x', 'lru_lambda', 'lru_norm', 'w_out', 'post_mix_norm',
           'pre_mlp_norm', 'w_mlp_in', 'w_mlp_out', 'post_mlp_norm']
LARGE = ['w_in', 'w_out', 'w_mlp_in', 'w_mlp_out']

D_SSD, D_XBC, DT_W = 2048, 4096, 32
TB = 256


def _w_in_runs(n_shard, n_main):
    c_dt = D_SSD + D_XBC
    runs, p = [], 0
    while p < n_main:
        j, off = divmod(p if p < c_dt else p + DT_W, n_shard)
        ln = min(n_shard - off, (c_dt if p < c_dt else n_main) - p)
        runs.append((p, j, off, ln))
        p += ln
    jd, offd = divmod(c_dt, n_shard)
    assert offd + DT_W <= n_shard
    return runs, (jd, offd)


def _pack_w_in(slots, *, name):
    _, d, n_shard = slots.shape
    n_main = 4 * n_shard - DT_W
    runs, (jd, offd) = _w_in_runs(n_shard, n_main)

    def body(s_ref, main_ref, dt_ref):
        for p, j, off, ln in runs:
            main_ref[:, p:p + ln] = s_ref[j, :, off:off + ln]
        dt_ref[:, 0:DT_W] = s_ref[jd, :, offd:offd + DT_W]
        dt_ref[:, DT_W:] = jnp.zeros((TB, 128 - DT_W), dt_ref.dtype)

    return pl.pallas_call(
        body, name=name, grid=(d // TB,),
        in_specs=[pl.BlockSpec((4, TB, n_shard), lambda i: (0, i, 0))],
        out_specs=[pl.BlockSpec((TB, n_main), lambda i: (i, 0)), pl.BlockSpec((TB, 128), lambda i: (i, 0))],
        out_shape=[jax.ShapeDtypeStruct((d, n_main), slots.dtype), jax.ShapeDtypeStruct((d, 128), slots.dtype)],
        compiler_params=_params(("parallel",)),
    )(slots)


def _unpack_dw_in(dw_main, dw_dt, *, name):
    d, n_main = dw_main.shape
    n_shard = (n_main + DT_W) // 4
    runs, (jd, offd) = _w_in_runs(n_shard, n_main)

    def body(main_ref, dt_ref, o_ref):
        for p, j, off, ln in runs:
            o_ref[j, :, off:off + ln] = main_ref[:, p:p + ln]
        o_ref[jd, :, offd:offd + DT_W] = dt_ref[:, 0:DT_W]

    return pl.pallas_call(
        body, name=name, grid=(d // TB,),
        in_specs=[pl.BlockSpec((TB, n_main), lambda i: (i, 0)), pl.BlockSpec((TB, 128), lambda i: (i, 0))],
        out_specs=pl.BlockSpec((4, TB, n_shard), lambda i: (0, i, 0)),
        out_shape=jax.ShapeDtypeStruct((4, d, n_shard), dw_main.dtype),
        compiler_params=_params(("parallel",)),
    )(dw_main, dw_dt)


def _half(ref, chip_idx, hc, piece=(0, 1)):
    q, nq = piece
    rp = ref.shape[1] // (2 * nq)
    return ref.at[chip_idx, pl.ds((hc * nq + q) * rp, rp), :]


def _job_gather_ici(i_ref, g_ref, send_sems, recv_sems, base, piece):
    x, y, c = _place()
    j = 2 * x + y
    sends, recvs = [], []
    for k, (px, py) in enumerate(_other_chips(x, y)):
        kw = dict(send_sem=send_sems.at[base + k], recv_sem=recv_sems.at[base + k], device_id=(px, py, c), device_id_type=MESH)
        sends.append(pltpu.make_async_remote_copy(src_ref=_half(i_ref, j, c, piece), dst_ref=_half(g_ref, j, c, piece), **kw))
        landed = _half(g_ref, 2 * px + py, c, piece)
        recvs.append(pltpu.make_async_remote_copy(src_ref=landed, dst_ref=landed, **kw))
    return sends, recvs


def _job_gather_sibling(i_ref, g_ref, send_sems, recv_sems, base, piece):
    x, y, c = _place()
    sends, recvs = [], []
    for k, (px, py) in enumerate(_other_chips(x, y)):
        kw = dict(send_sem=send_sems.at[base + k], recv_sem=recv_sems.at[base + k], device_id=(x, y, 1 - c), device_id_type=MESH)
        sends.append(pltpu.make_async_remote_copy(src_ref=_half(i_ref, 2 * px + py, c, piece),
                                                  dst_ref=_half(g_ref, 2 * px + py, c, piece), **kw))
        other = _half(g_ref, 2 * px + py, 1 - c, piece)
        recvs.append(pltpu.make_async_remote_copy(src_ref=other, dst_ref=other, **kw))
    return sends, recvs


def _job_reduce_ici(s_ref, got_refs, send_sems, recv_sems, base, piece):
    x, y, c = _place()
    q, nq = piece
    rp = s_ref.shape[1] // nq
    rows = pl.ds(q * rp, rp)
    sends = [pltpu.make_async_remote_copy(src_ref=s_ref.at[2 * px + py, rows, :], dst_ref=got_refs[k].at[rows, :],
                                          send_sem=send_sems.at[base + k], recv_sem=recv_sems.at[base + k],
                                          device_id=(px, py, c), device_id_type=MESH)
             for k, (px, py) in enumerate(_other_chips(x, y))]
    return sends, sends


GATHER_PLAN = {
    "f_inproj": [("w_out", _job_gather_ici, (0, 1)), ("w_mlp_in", _job_gather_ici, (0, 2))],
    "f_ssdconv": [("w_mlp_in", _job_gather_ici, (1, 2)), ("w_out", _job_gather_sibling, (0, 1)),
                  ("w_mlp_in", _job_gather_sibling, (0, 2))],
    "f_lruconv": [("w_mlp_in", _job_gather_sibling, (1, 2))],
    "f_ssd": [("w_mlp_out", _job_gather_ici, (0, 2))],
    "f_lrugates": [("w_mlp_out", _job_gather_sibling, (0, 2))],
    "f_outproj": [("w_mlp_out", _job_gather_ici, (1, 2))],
    "f_mlpin": [("w_mlp_out", _job_gather_sibling, (1, 2))],
}
REDUCE_PLAN = {
    "b_mlpin_dx": [("w_mlp_out", (0, 2))],
    "b_mlpin_dw": [("w_mlp_out", (1, 2))],
    "b_ssd": [("w_mlp_in", (0, 1)), ("w_out", (0, 1))],
    "b_inproj_dx": [("w_in", (0, 1))],
}


class _DistPlan:
    def __init__(self, slots, w_main, w_dt, where, shapes):
        self.slots, self.w_main, self.w_dt, self.where, self.shapes = slots, w_main, w_dt, where, shapes
        self.partial, self.got, self.totals = {}, {}, {}

    def weight(self, name):
        d = self.w_main.shape[0]
        return {"w_main": lambda: self.w_main, "w_dt": lambda: self.w_dt, "w_out": lambda: self.slots["w_out"].reshape(-1, d),
                "w_mi": lambda: self.slots["w_mlp_in"], "w_mo": lambda: self.slots["w_mlp_out"].reshape(-1, d)}[name]()

    def _slot_side(self, jobs):
        names = []
        for n, _, _ in jobs:
            if n not in names:
                names.append(n)
        arrs = [self.slots[n] for n in names]

        def make(i_refs, o_refs, send_sems, recv_sems):
            sends, recvs = [], []
            for q, (n, job, piece) in enumerate(jobs):
                s, r = job(i_refs[names.index(n)], o_refs[names.index(n)], send_sems, recv_sems, 3 * q, piece)
                sends, recvs = sends + s, recvs + r
            return sends, recvs

        side = _Side(arrs, [jax.ShapeDtypeStruct(a.shape, a.dtype) for a in arrs], {i: i for i in range(len(arrs))}, 3 * len(jobs), make)
        return side, names

    def side(self, kernel_name):
        if kernel_name in GATHER_PLAN:
            side, self._names = self._slot_side(GATHER_PLAN[kernel_name])
        elif kernel_name in REDUCE_PLAN:
            items = self._sent = REDUCE_PLAN[kernel_name]
            parts = [self.partial[n] for n, _ in items]
            ins, aliases = list(parts), {}
            for q, (n, (piece, _)) in enumerate(items):
                if piece > 0:
                    for k in range(3):
                        aliases[len(ins)] = 3 * q + k
                        ins.append(self.got[n][k])

            def make(i_refs, o_refs, send_sems, recv_sems):
                sends = []
                for q, (_, piece) in enumerate(items):
                    sends += _job_reduce_ici(i_refs[q], o_refs[3 * q:3 * q + 3], send_sems, recv_sems, 3 * q, piece)[0]
                return sends, sends

            outs = [jax.ShapeDtypeStruct(s.shape[1:], s.dtype) for s in parts for _ in range(3)]
            side = _Side(ins, outs, aliases, 3 * len(items), make)
        elif kernel_name == "b_prenorm":
            names = self._share = list(self.totals)
            ts = [self.totals[n] for n in names]

            def make(i_refs, o_refs, send_sems, recv_sems):
                x, y, c = _place()
                sends, recvs = [], []
                for w in range(len(ts)):
                    kw = dict(send_sem=send_sems.at[w], recv_sem=recv_sems.at[w], device_id=(x, y, 1 - c), device_id_type=MESH)
                    sends.append(pltpu.make_async_remote_copy(src_ref=i_refs[w].at[c], dst_ref=o_refs[w].at[c], **kw))
                    recvs.append(pltpu.make_async_remote_copy(src_ref=i_refs[w].at[c], dst_ref=o_refs[w].at[1 - c], **kw))
                return sends, recvs

            side = _Side(ts, [jax.ShapeDtypeStruct(t.shape, t.dtype) for t in ts], {i: i for i in range(len(ts))}, len(ts), make)
        else:
            return None
        return side

    def done(self, kernel_name, side_outs):
        if kernel_name == "b_prenorm":
            self.shared = {n: o.reshape(self.shapes[n]) for n, o in zip(self._share, side_outs)}
            return
        if kernel_name.startswith("f_"):
            self.slots.update(zip(self._names, side_outs))
            return
        for q, (name, (piece, n_pieces)) in enumerate(self._sent):
            g3 = self.got[name] = side_outs[3 * q:3 * q + 3]
            if piece < n_pieces - 1:
                continue
            s4 = self.partial[name]
            _, rh, cc = s4.shape
            row = ((TB, cc), lambda i, s: (i, 0))
            self.totals[name] = _blocks(
                lambda o, r0, r1, r2: (((o.astype(F32) + r0.astype(F32)) + r1.astype(F32)) + r2.astype(F32),),
                [(s4, (None, TB, cc), lambda i, s: (s[0], i, 0)), (g3[0],) + row, (g3[1],) + row, (g3[2],) + row],
                [(jax.ShapeDtypeStruct((2, rh, cc), F32), (None, TB, cc), lambda i, s: (s[1], i, 0))],
                grid=(rh // TB,), name="add_chips_" + name, prefetch=self.where)[0]

    def grad(self, name, g):
        if name == "w_in":
            g = _unpack_dw_in(*g, name="unpack_dw_in")
        g = g.reshape(4, 2, g.shape[1] // 2, g.shape[2])
        (t,) = _rs_sibling([g], name="reduce_sibling_" + name)
        _, rh, cc = t.shape
        s4 = _blocks(lambda u, v: (u.astype(F32) + v.astype(F32),),
                     [(g, (None, None, TB, cc), lambda q, i, s: (q, s[1], i, 0)), (t, (None, TB, cc), lambda q, i, s: (q, i, 0))],
                     [(jax.ShapeDtypeStruct(t.shape, BF16), (None, TB, cc), lambda q, i, s: (q, i, 0))],
                     grid=(4, rh // TB), name="add_sibling_" + name, prefetch=self.where)[0]
        self.partial[name] = s4

    def finish(self):
        assert len(self.shared) == len(self.shapes)
        return self.shared


def kernel(x, pre_mix_norm, w_in, ssd_conv_w, ssd_conv_b, ssd_dt_bias, ssd_a_log, ssd_d, ssd_norm, lru_conv_w, lru_conv_b, lru_w_a, lru_b_a, lru_w_x, lru_b_x, lru_lambda, lru_norm, w_out, post_mix_norm, pre_mlp_norm, w_mlp_in, w_mlp_out, post_mlp_norm, loss_target, m_pre_mix_norm, m_w_in, m_ssd_conv_w, m_ssd_conv_b, m_ssd_dt_bias, m_ssd_a_log, m_ssd_d, m_ssd_norm, m_lru_conv_w, m_lru_conv_b, m_lru_w_a, m_lru_b_a, m_lru_w_x, m_lru_b_x, m_lru_lambda, m_lru_norm, m_w_out, m_post_mix_norm, m_pre_mlp_norm, m_w_mlp_in, m_w_mlp_out, m_post_mlp_norm, v_pre_mix_norm, v_w_in, v_ssd_conv_w, v_ssd_conv_b, v_ssd_dt_bias, v_ssd_a_log, v_ssd_d, v_ssd_norm, v_lru_conv_w, v_lru_conv_b, v_lru_w_a, v_lru_b_a, v_lru_w_x, v_lru_b_x, v_lru_lambda, v_lru_norm, v_w_out, v_post_mix_norm, v_pre_mlp_norm, v_w_mlp_in, v_w_mlp_out, v_post_mlp_norm):
    a = dict(locals())
    j = 2 * lax.axis_index("x") + lax.axis_index("y")
    D = x.shape[-1]
    c_dt = D_SSD + D_XBC

    core = lax.axis_index("c")
    where = jnp.stack([j, core, 2 * j + core]).astype(jnp.int32)
    slots = {}
    for name in LARGE:
        w = a[name][0]
        r, cc = w.shape
        slots[name] = _blocks(lambda t: (t,), [(w, (TB, cc), lambda i, s: (i, 0))],
                              [(jax.ShapeDtypeStruct((4, r, cc), BF16), (None, TB, cc), lambda i, s: (s[0], i, 0))],
                              grid=(r // TB,), name="cast_" + name, prefetch=where)[0]
    (g_in,) = _gather_weights([slots.pop("w_in")], name="gather_w_in")
    w_main, w_dt = _pack_w_in(g_in, name="pack_w_in")
    taps = jnp.concatenate([ssd_conv_w[0].reshape(-1, 128), lru_conv_w[0].reshape(-1, 128)], axis=0)
    taps = _allgather8(taps, name="gather_taps").reshape(8, taps.shape[0], 128)[0::2]
    n_ssd = ssd_conv_w.shape[1] * ssd_conv_w.shape[2] // 128
    ssd_taps = taps[:, :n_ssd].reshape(4, CONV_W, -1).transpose(1, 0, 2).reshape(CONV_W, -1)
    lru_taps = taps[:, n_ssd:].reshape(4, CONV_W, -1).transpose(1, 0, 2).reshape(CONV_W, -1)

    def row128(v):
        return jnp.pad(v, ((0, 0), (0, 128 - v.shape[1])))

    p = dict(pre_mix_norm=pre_mix_norm, ssd_conv_w=ssd_taps, ssd_conv_b=ssd_conv_b,
             dtb=row128(ssd_dt_bias), alog=row128(ssd_a_log), drow=row128(ssd_d), ssd_norm=ssd_norm,
             lru_conv_w=lru_taps, lru_conv_b=lru_conv_b, lru_w_a=lru_w_a[0], lru_b_a=lru_b_a.reshape(1, -1),
             lru_w_x=lru_w_x[0], lru_b_x=lru_b_x.reshape(1, -1), lru_lambda=lru_lambda, lru_norm=lru_norm,
             post_mix_norm=post_mix_norm, pre_mlp_norm=pre_mlp_norm, post_mlp_norm=post_mlp_norm)
    plan = _DistPlan(slots, w_main, w_dt, where, {n: a[n].shape for n in LARGE})
    grad_x, small = _local_step(x[0], loss_target[0], p, plan)

    large_grads = plan.finish()

    packed = _pack([small[n] for n, _ in SMALL_GRADS])
    n_rows = packed.shape[0]
    gathered = _blocks(lambda t: (t,), [(packed, (n_rows, 128), lambda i, s: (0, 0))],
                       [(jax.ShapeDtypeStruct((8, n_rows, 128), F32), (None, n_rows, 128), lambda i, s: (s[2], 0, 0))],
                       grid=(1,), name="place_small", prefetch=where)[0]

    def gather_small_side(buf, stage):
        rows = pl.ds(0, n_rows) if stage == 2 else pl.ds(stage * (n_rows // 2), n_rows // 2)

        def make(i_refs, o_refs, send_sems, recv_sems):
            x, y, c = _place()
            src, dst = i_refs[0], o_refs[0]
            sends, recvs = [], []

            def add(k, block, to, frm):
                kw = dict(send_sem=send_sems.at[k], recv_sem=recv_sems.at[k], device_id=to, device_id_type=MESH)
                sends.append(pltpu.make_async_remote_copy(src_ref=src.at[block, rows, :], dst_ref=dst.at[block, rows, :], **kw))
                recvs.append(pltpu.make_async_remote_copy(src_ref=src.at[frm, rows, :], dst_ref=dst.at[frm, rows, :], **kw))

            if stage < 2:
                add(0, 4 * x + 2 * y + c, (x, y, 1 - c), 4 * x + 2 * y + 1 - c)
            for k, (px, py) in enumerate(_other_chips(x, y)):
                if stage < 2:
                    add(1 + k, 4 * x + 2 * y + c, (px, py, c), 4 * px + 2 * py + c)
                else:
                    add(k, 4 * px + 2 * py + c, (x, y, 1 - c), 4 * px + 2 * py + 1 - c)
            return sends, recvs

        return _Side([buf], [jax.ShapeDtypeStruct(buf.shape, buf.dtype)], {0: 0}, 4 if stage < 2 else 3, make)

    delta, new_m, new_v = {}, {}, {}
    for stage, name in enumerate(["w_mlp_in", "w_mlp_out", "w_in", "w_out"]):
        w = a[name][0]
        cc = w.shape[1]
        outs = _rowwise(_adamw, [w, large_grads[name][0], a["m_" + name][0], a["v_" + name][0]], [], [(cc, F32)] * 3, [],
                        name="adamw_" + name, tr=128, sub=8, side=gather_small_side(gathered, stage) if stage < 3 else None)
        if stage < 3:
            outs, (gathered,) = outs
        delta[name], new_m[name], new_v[name] = [o[None] for o in outs]

    total = _sum8(gathered, name="sum_small")
    tot = dict(zip([n for n, _ in SMALL_GRADS], _unpack(total, [s for _, s in SMALL_GRADS])))
    loss = tot["loss"][0, 0]
    n_sc, n_lc = ssd_conv_w.shape[2], lru_conv_w.shape[2]
    grads = dict(
        pre_mix_norm=tot["pre_mix_norm"],
        ssd_conv_w=lax.dynamic_slice(tot["ssd_conv"][:CONV_W], (0, j * n_sc), (CONV_W, n_sc))[None],
        ssd_conv_b=tot["ssd_conv"][CONV_W:CONV_W + 1],
        ssd_dt_bias=tot["ssd_par"][0:1, :DT_W], ssd_a_log=tot["ssd_par"][1:2, :DT_W], ssd_d=tot["ssd_par"][2:3, :DT_W],
        ssd_norm=tot["ssd_norm"],
        lru_conv_w=lax.dynamic_slice(tot["lru_conv"][:CONV_W], (0, j * n_lc), (CONV_W, n_lc))[None],
        lru_conv_b=tot["lru_conv"][CONV_W:CONV_W + 1],
        lru_w_a=tot["lru_w_a"][None], lru_b_a=tot["lru_b_a"].reshape(lru_b_a.shape),
        lru_w_x=tot["lru_w_x"][None], lru_b_x=tot["lru_b_x"].reshape(lru_b_x.shape),
        lru_lambda=tot["lru_lambda"], lru_norm=tot["lru_norm"], post_mix_norm=tot["post_mix_norm"],
        pre_mlp_norm=tot["pre_mlp_norm"], post_mlp_norm=tot["post_mlp_norm"])

    grads.update(large_grads)

    small_w = [n for n in WEIGHTS if n not in LARGE]
    packs = [_pack([d[n] for n in small_w]) for d in (a, grads, {n: a["m_" + n] for n in small_w}, {n: a["v_" + n] for n in small_w})]
    outs = _rowwise(_adamw, packs, [], [(128, F32)] * 3, [], name="adamw_small", tr=packs[0].shape[0], sub=8)
    for d, o in zip((delta, new_m, new_v), outs):
        d.update(zip(small_w, _unpack(o, [a[n].shape for n in small_w])))

    return (loss, grad_x[None], *[grads[n] for n in WEIGHTS], *[delta[n] for n in WEIGHTS],
            *[new_m[n] for n in WEIGHTS], *[new_v[n] for n in WEIGHTS])
```

```python
import functools

import jax
import jax.numpy as jnp
from jax import lax
from jax.experimental import pallas as pl
from jax.experimental.pallas import tpu as pltpu

F32 = jnp.float32
BF16 = jnp.bfloat16
MESH = pl.DeviceIdType.MESH

EPS = 1e-6
LRU_C = 8.0
ADAM_LR = 0.001
ADAM_B1 = 0.9
ADAM_B2 = 0.999
ADAM_EPS = 1e-08
ADAM_WD = 0.01
ADAM_STEP = 10

N_GROUPS = 8
HEADS_PER_GROUP = 4
HEAD_DIM = 64
GROUP_W = HEADS_PER_GROUP * HEAD_DIM
STATE = 128
LRU_HEADS = 16
LRU_BLOCK = 128
CONV_W = 4
SSD_CHUNK = 256
HALO = 8

VMEM_LIMIT = 48 * 1024 * 1024


def _params(sem=None):
    return pltpu.CompilerParams(dimension_semantics=sem, vmem_limit_bytes=VMEM_LIMIT)


@jax.custom_jvp
def _log1p(x):
    u = 1.0 + x
    d = u - 1.0
    return jnp.where(d == 0.0, x, jnp.log(u) * (x / jnp.where(d == 0.0, 1.0, d)))


@_log1p.defjvp
def _log1p_jvp(primals, tangents):
    (x,), (t,) = primals, tangents
    return _log1p(x), t / (1.0 + x)


@jax.custom_jvp
def _expm1(x):
    u = jnp.exp(x)
    lu = jnp.log(u)
    safe = jnp.where(lu == 0.0, 1.0, lu)
    y = (u - 1.0) * (x / safe)
    y = jnp.where(lu == 0.0, x, y)
    return jnp.where(u == 0.0, -1.0, y)


@_expm1.defjvp
def _expm1_jvp(primals, tangents):
    (x,), (t,) = primals, tangents
    return _expm1(x), t * jnp.exp(x)


def _softplus(x):
    return jnp.maximum(x, 0.0) + _log1p(jnp.exp(-jnp.abs(x)))


def _sigmoid(x):
    return 1.0 / (1.0 + jnp.exp(-x))


def _silu(x):
    return x * _sigmoid(x)


def _gelu(x):
    c = 0.7978845608028654
    return 0.5 * x * (1.0 + jnp.tanh(c * (x + 0.044715 * (x * x * x))))


def _rms(x, g):
    return x * lax.rsqrt(jnp.mean(x * x, axis=-1, keepdims=True) + EPS) * g


def _dot(a, b, dims):
    return lax.dot_general(a.astype(BF16), b.astype(BF16), (dims, ((), ())), preferred_element_type=F32)


_NN = ((1,), (0,))
_NT = ((1,), (1,))
_TN = ((0,), (0,))


ANY = pl.BlockSpec(memory_space=pl.ANY)


class _Side:
    def __init__(self, ins, outs, aliases, n_sems, make):
        self.ins, self.outs, self.aliases, self.n_sems, self.make = ins, outs, aliases, n_sems, make


def _call(body, args, *, name, grid, in_specs, out_specs, out_shape, scratch_shapes=(), sem=None, side=None):
    in_specs, out_specs, out_shape, scratch_shapes = list(in_specs), list(out_specs), list(out_shape), list(scratch_shapes)
    if side is None:
        outs = pl.pallas_call(body, name=name, grid=grid, in_specs=in_specs, out_specs=out_specs, out_shape=out_shape,
                              scratch_shapes=scratch_shapes, compiler_params=_params(sem))(*args)
        return list(outs), []
    n_in, n_out, n_scr, si, so = len(in_specs), len(out_specs), len(scratch_shapes), len(side.ins), len(side.outs)

    def full(*refs):
        s_in = refs[n_in:n_in + si]
        o0 = n_in + si
        s_out = refs[o0 + n_out:o0 + n_out + so]
        scr = refs[o0 + n_out + so:o0 + n_out + so + n_scr]
        send_sems, recv_sems = refs[-2], refs[-1]
        ids = [pl.program_id(d) for d in range(len(grid))]
        first = functools.reduce(jnp.logical_and, [i == 0 for i in ids])
        last = functools.reduce(jnp.logical_and, [i == g - 1 for i, g in zip(ids, grid)])

        @pl.when(first)
        def _():
            for d in side.make(s_in, s_out, send_sems, recv_sems)[0]:
                d.start()

        body(*refs[:n_in], *refs[o0:o0 + n_out], *scr)

        @pl.when(last)
        def _():
            sends, recvs = side.make(s_in, s_out, send_sems, recv_sems)
            for d in recvs:
                d.wait_recv()
            for d in sends:
                d.wait_send()

    outs = pl.pallas_call(
        full, name=name, grid=grid, in_specs=in_specs + [ANY] * si, out_specs=out_specs + [ANY] * so,
        out_shape=out_shape + list(side.outs),
        scratch_shapes=scratch_shapes + [pltpu.SemaphoreType.DMA((side.n_sems,)), pltpu.SemaphoreType.DMA((side.n_sems,))],
        input_output_aliases={n_in + i: n_out + o for i, o in side.aliases.items()},
        compiler_params=_params(("arbitrary",) * len(grid)),
    )(*args, *side.ins)
    return list(outs[:n_out]), list(outs[n_out:])


def _matmul(a, b, *, mode, m, n, k, tm, tn, tk, out_dtypes, name, a_spec=None, b_spec=None,
            out_specs=None, out_shapes=None, extras=(), epilogue=None, side=None):
    tm, tn, tk = min(tm, m), min(tn, n), min(tk, k)
    assert m % tm == 0 and n % tn == 0 and k % tk == 0, (name, m, n, k, tm, tn, tk)
    nk = k // tk
    dims = {"nn": _NN, "nt": _NT, "tn": _TN}[mode]
    if a_spec is None:
        a_spec = pl.BlockSpec((tk, tm), lambda i, j, kk: (kk, i)) if mode == "tn" else pl.BlockSpec((tm, tk), lambda i, j, kk: (i, kk))
    if b_spec is None:
        b_spec = pl.BlockSpec((tn, tk), lambda i, j, kk: (j, kk)) if mode == "nt" else pl.BlockSpec((tk, tn), lambda i, j, kk: (kk, j))
    tile = pl.BlockSpec((tm, tn), lambda i, j, kk: (i, j))
    if out_specs is None:
        out_specs = [tile for _ in out_dtypes]
    if out_shapes is None:
        out_shapes = [jax.ShapeDtypeStruct((m, n), d) for d in out_dtypes]
    n_ex, n_out = len(extras), len(out_dtypes)

    def body(*refs):
        a_ref, b_ref = refs[0], refs[1]
        ex_refs = refs[2:2 + n_ex]
        o_refs = refs[2 + n_ex:2 + n_ex + n_out]
        def finish(r):
            outs = epilogue(r, *[e[...] for e in ex_refs]) if epilogue is not None else (r,)
            for o_ref, o in zip(o_refs, outs):
                o_ref[...] = o.astype(o_ref.dtype)

        if nk == 1:
            finish(_dot(a_ref[...], b_ref[...], dims))
            return
        acc = refs[-1]
        kk = pl.program_id(2)

        @pl.when(kk == 0)
        def _():
            acc[...] = _dot(a_ref[...], b_ref[...], dims)

        @pl.when(kk > 0)
        def _():
            acc[...] += _dot(a_ref[...], b_ref[...], dims)

        @pl.when(kk == nk - 1)
        def _():
            finish(acc[...])

    outs, side_outs = _call(
        body, (a, b, *extras), name=name, grid=(m // tm, n // tn, nk),
        in_specs=[a_spec, b_spec] + [tile for _ in extras], out_specs=out_specs, out_shape=out_shapes,
        scratch_shapes=[] if nk == 1 else [pltpu.VMEM((tm, tn), F32)], sem=("parallel", "parallel", "arbitrary"), side=side)
    return outs if side is None else (outs, side_outs)


def _rowwise(fn, rows, bcast, out_rows, out_acc, *, name, tr, sub, side=None):
    rows = [r if isinstance(r, tuple) else (r, 0, r.shape[1]) for r in rows]
    row_specs = []
    for arr, c0, w in rows:
        assert c0 % w == 0, (name, c0, w)
        row_specs.append((w, c0 // w))
    rows = [r[0] for r in rows]
    L = rows[0].shape[0]
    tr = min(tr, L)
    sub = min(sub, tr)
    assert L % tr == 0 and tr % sub == 0, (name, L, tr, sub)
    n_r, n_b, n_or, n_oa = len(rows), len(bcast), len(out_rows), len(out_acc)

    def body(*refs):
        r_refs = refs[:n_r]
        b_refs = refs[n_r:n_r + n_b]
        or_refs = refs[n_r + n_b:n_r + n_b + n_or]
        oa_refs = refs[n_r + n_b + n_or:]
        i = pl.program_id(0)

        @pl.when(i == 0)
        def _():
            for o in oa_refs:
                o[...] = jnp.zeros_like(o)

        bvals = [b[...] for b in b_refs]

        def step(s, carry):
            r0 = pl.multiple_of(s * sub, sub)
            tiles = [r[pl.ds(r0, sub), :] for r in r_refs]
            outs = fn(*tiles, *bvals)
            for o_ref, o in zip(or_refs, outs[:n_or]):
                o_ref[pl.ds(r0, sub), :] = o.astype(o_ref.dtype)
            for o_ref, o in zip(oa_refs, outs[n_or:]):
                o_ref[...] += o
            return carry

        if tr == sub:
            step(0, 0)
        else:
            lax.fori_loop(0, tr // sub, step, 0)

    def whole(shape):
        nd = len(shape)
        return pl.BlockSpec(shape, lambda i, _n=nd: (0,) * _n)

    outs, side_outs = _call(
        body, (*rows, *bcast), name=name, grid=(L // tr,),
        in_specs=[pl.BlockSpec((tr, w), lambda i, _c=cb: (i, _c)) for w, cb in row_specs] + [whole(b.shape) for b in bcast],
        out_specs=[pl.BlockSpec((tr, c), lambda i: (i, 0)) for c, _ in out_rows] + [whole(s) for s in out_acc],
        out_shape=[jax.ShapeDtypeStruct((L, c), d) for c, d in out_rows] + [jax.ShapeDtypeStruct(s, F32) for s in out_acc],
        sem=("arbitrary",), side=side)
    return outs if side is None else (outs, side_outs)


def _colsum(x):
    return jnp.sum(x, axis=0, keepdims=True)


def _grouped(fn, rows, params, out_rows, out_acc, *, gw, name, tr, side=None):
    rows = [r if isinstance(r, tuple) else (r, 0) for r in rows]
    L = rows[0][0].shape[0]
    tr = min(tr, L)
    assert L % tr == 0
    G = None
    for p in params:
        G = p.shape[0] if p.ndim == 3 else p.shape[1] // gw
    cw = G * gw
    n_r, n_p, n_or, n_oa = len(rows), len(params), len(out_rows), len(out_acc)

    def pick(ref, g):
        return ref[g] if len(ref.shape) == 3 else ref[:, g * gw:(g + 1) * gw]

    def body(*refs):
        r_refs = refs[:n_r]
        p_refs = refs[n_r:n_r + n_p]
        or_refs = refs[n_r + n_p:n_r + n_p + n_or]
        oa_refs = refs[n_r + n_p + n_or:]

        @pl.when(pl.program_id(0) == 0)
        def _():
            for o in oa_refs:
                o[...] = jnp.zeros_like(o)

        for g in range(G):
            outs = fn(*[pick(r, g) for r in r_refs], *[pick(p, g) for p in p_refs])
            for o_ref, o in zip(or_refs, outs[:n_or]):
                o_ref[:, g * gw:(g + 1) * gw] = o.astype(o_ref.dtype)
            for o_ref, o in zip(oa_refs, outs[n_or:]):
                if len(o_ref.shape) == 3:
                    o_ref[g] += o
                else:
                    o_ref[:, g * gw:(g + 1) * gw] += o

    def whole(shape):
        nd = len(shape)
        return pl.BlockSpec(shape, lambda i, _n=nd: (0,) * _n)

    for _, c0 in rows:
        assert c0 % cw == 0
    outs, side_outs = _call(
        body, (*[r[0] for r in rows], *params), name=name, grid=(L // tr,),
        in_specs=[pl.BlockSpec((tr, cw), lambda i, _c=c0 // cw: (i, _c)) for _, c0 in rows] + [whole(p.shape) for p in params],
        out_specs=[pl.BlockSpec((tr, cw), lambda i: (i, 0)) for _ in out_rows] + [whole(s) for s in out_acc],
        out_shape=[jax.ShapeDtypeStruct((L, cw), d) for d in out_rows] + [jax.ShapeDtypeStruct(s, F32) for s in out_acc],
        sem=("arbitrary",), side=side)
    return outs if side is None else (outs, side_outs)


def _dsilu(p):
    s = _sigmoid(p)
    return s + p * s * (1.0 - s)


CONV_RC, CONV_CC = 32, 512


def _past_window(x_ref, head, r0, k, cs):
    if r0 == 0:
        return head[HALO - 3 + k:HALO - 3 + k + CONV_RC, cs]
    return x_ref[r0 - 3 + k:r0 - 3 + k + CONV_RC, cs]


def _conv_fwd(x, c0, C, w, b, *, silu, name, tr=512, cb=1024, side=None):
    L = x.shape[0]
    tr = min(tr, L)
    nb, hb = L // tr, tr // HALO
    assert L % tr == 0 and C % cb == 0 and c0 % cb == 0 and tr % CONV_RC == 0 and cb % CONV_CC == 0
    n_out = 2 if silu else 1

    def body(x_ref, h_ref, w_ref, b_ref, *rest):
        o_refs, head = rest[:n_out], rest[n_out]
        head[0:HALO, :] = h_ref[...]

        @pl.when(pl.program_id(0) == 0)
        def _():
            head[0:HALO, :] = jnp.zeros((HALO, cb), F32)

        head[HALO:, :] = x_ref[0:CONV_RC, :]
        for cc in range(cb // CONV_CC):
            cs = slice(cc * CONV_CC, (cc + 1) * CONV_CC)
            wv, bv = w_ref[:, cs], b_ref[:, cs]
            for r0 in range(0, tr, CONV_RC):
                y = bv
                for k in range(CONV_W):
                    y = y + wv[k:k + 1, :] * _past_window(x_ref, head, r0, k, cs)
                o_refs[0][r0:r0 + CONV_RC, cs] = y
                if silu:
                    o_refs[1][r0:r0 + CONV_RC, cs] = _silu(y)

    outs, side_outs = _call(
        body, (x, x, w, b), name=name, grid=(nb, C // cb),
        in_specs=[
            pl.BlockSpec((tr, cb), lambda i, j: (i, c0 // cb + j)),
            pl.BlockSpec((HALO, cb), lambda i, j: (jnp.maximum(i * hb - 1, 0), c0 // cb + j)),
            pl.BlockSpec((CONV_W, cb), lambda i, j: (0, j)),
            pl.BlockSpec((1, cb), lambda i, j: (0, j)),
        ],
        out_specs=[pl.BlockSpec((tr, cb), lambda i, j: (i, j)) for _ in range(n_out)],
        out_shape=[jax.ShapeDtypeStruct((L, C), F32) for _ in range(n_out)],
        scratch_shapes=[pltpu.VMEM((HALO + CONV_RC, cb), F32)], sem=("parallel", "parallel"), side=side)
    return outs if side is None else (outs, side_outs)


def _fold8(v):
    return (v[0:8] + v[8:16]) + (v[16:24] + v[24:32])


def _conv_bwd(dact, dc0, pre, x, xc0, C, w, *, silu, name, tr=512, cb=1024, side=None):
    L = x.shape[0]
    tr = min(tr, L)
    nb, hb = L // tr, tr // HALO
    last_h = L // HALO - 1
    assert L % tr == 0 and C % cb == 0 and tr % CONV_RC == 0 and cb % CONV_CC == 0

    def body(*refs):
        if silu:
            d_ref, dh_ref, p_ref, ph_ref, x_ref, xh_ref, w_ref, dx_ref, dwb_ref, dp, head = refs
        else:
            d_ref, dh_ref, x_ref, xh_ref, w_ref, dx_ref, dwb_ref, dp, head = refs
        i = pl.program_id(1)
        dp[tr:, :] = dh_ref[...] * _dsilu(ph_ref[...]) if silu else dh_ref[...]

        @pl.when(i == nb - 1)
        def _():
            dp[tr:, :] = jnp.zeros((HALO, cb), F32)

        head[0:HALO, :] = xh_ref[...]

        @pl.when(i == 0)
        def _():
            head[0:HALO, :] = jnp.zeros((HALO, cb), F32)
            dwb_ref[...] = jnp.zeros_like(dwb_ref)

        head[HALO:, :] = x_ref[0:CONV_RC, :]
        for cc in range(cb // CONV_CC):
            cs = slice(cc * CONV_CC, (cc + 1) * CONV_CC)
            for r0 in range(0, tr, CONV_RC):
                rs = slice(r0, r0 + CONV_RC)
                dp[rs, cs] = d_ref[rs, cs] * _dsilu(p_ref[rs, cs]) if silu else d_ref[rs, cs]
        for cc in range(cb // CONV_CC):
            cs = slice(cc * CONV_CC, (cc + 1) * CONV_CC)
            wv = w_ref[:, cs]
            acc = [jnp.zeros((8, CONV_CC), F32) for _ in range(CONV_W + 1)]
            for r0 in range(0, tr, CONV_RC):
                dm = dp[r0:r0 + CONV_RC, cs]
                dx = jnp.zeros((CONV_RC, CONV_CC), F32)
                for k in range(CONV_W):
                    dx = dx + wv[k:k + 1, :] * dp[r0 + 3 - k:r0 + 3 - k + CONV_RC, cs]
                    acc[k] = acc[k] + _fold8(dm * _past_window(x_ref, head, r0, k, cs))
                acc[CONV_W] = acc[CONV_W] + _fold8(dm)
                dx_ref[r0:r0 + CONV_RC, cs] = dx.astype(dx_ref.dtype)
            for k in range(CONV_W + 1):
                dwb_ref[k:k + 1, cs] += _colsum(acc[k])

    def main(c):
        return pl.BlockSpec((tr, cb), lambda j, i: (i, c // cb + j))

    def nxt(c):
        return pl.BlockSpec((HALO, cb), lambda j, i: (jnp.minimum((i + 1) * hb, last_h), c // cb + j))

    in_specs = [main(dc0), nxt(dc0)]
    args = [dact, dact]
    if silu:
        in_specs += [main(0), nxt(0)]
        args += [pre, pre]
    in_specs += [main(xc0), pl.BlockSpec((HALO, cb), lambda j, i: (jnp.maximum(i * hb - 1, 0), xc0 // cb + j)),
                 pl.BlockSpec((CONV_W, cb), lambda j, i: (0, j))]
    args += [x, x, w]
    outs, side_outs = _call(
        body, args, name=name, grid=(C // cb, nb), in_specs=in_specs,
        out_specs=[pl.BlockSpec((tr, cb), lambda j, i: (i, j)), pl.BlockSpec((8, cb), lambda j, i: (0, j))],
        out_shape=[jax.ShapeDtypeStruct((L, C), BF16), jax.ShapeDtypeStruct((8, C), F32)],
        scratch_shapes=[pltpu.VMEM((tr + HALO, cb), F32), pltpu.VMEM((HALO + CONV_RC, cb), F32)],
        sem=("parallel", "arbitrary"), side=side)
    return outs if side is None else (outs, side_outs)


def _ssd_f1(dtraw, dtb, alog):
    q = dtraw.shape[0]
    dt = _softplus(dtraw + dtb)
    adt = dt * (-jnp.exp(alog))
    tril = (lax.broadcasted_iota(jnp.int32, (q, q), 0) >= lax.broadcasted_iota(jnp.int32, (q, q), 1)).astype(F32)
    acs = lax.dot_general(tril, adt, (_NN, ((), ())), precision=lax.Precision.HIGHEST, preferred_element_type=F32)
    return dt, acs


def _ssd_group(g, x, bm, cm, dt, acs, drow, hp):
    q = x.shape[0]
    lane = lax.broadcasted_iota(jnp.int32, (1, 128), 1)
    sub = lax.broadcasted_iota(jnp.int32, (128, 1), 0)
    head_of = lax.broadcasted_iota(jnp.int32, (1, GROUP_W), 1) // HEAD_DIM
    is_last = (lax.broadcasted_iota(jnp.int32, (q, 1), 0) == q - 1).astype(F32)
    causal = lax.broadcasted_iota(jnp.int32, (q, q), 0) >= lax.broadcasted_iota(jnp.int32, (q, q), 1)
    acs_end = jnp.sum(acs * is_last, axis=0, keepdims=True)
    acs_t = acs.T
    dt_exp = jnp.zeros((q, GROUP_W), F32)
    acs_exp = jnp.zeros((q, GROUP_W), F32)
    end_exp = jnp.zeros((1, GROUP_W), F32)
    d_exp = jnp.zeros((1, GROUP_W), F32)
    heads = []
    for k in range(HEADS_PER_GROUP):
        h = HEADS_PER_GROUP * g + k
        oh = (lane == h).astype(F32)
        mk = (head_of == k).astype(F32)
        acs_col = jnp.sum(acs * oh, axis=1, keepdims=True)
        acs_row = jnp.sum(acs_t * (sub == h).astype(F32), axis=0, keepdims=True)
        dt_exp = dt_exp + jnp.sum(dt * oh, axis=1, keepdims=True) * mk
        acs_exp = acs_exp + acs_col * mk
        end_exp = end_exp + jnp.sum(acs_end * oh, axis=1, keepdims=True) * mk
        d_exp = d_exp + jnp.sum(drow * oh, axis=1, keepdims=True) * mk
        heads.append((acs_col, acs_row, mk))
    xdt = x * dt_exp
    states = _dot(bm, xdt * jnp.exp(end_exp - acs_exp), _TN)
    y = _dot(cm, hp, _NN) * jnp.exp(acs_exp) + x * d_exp
    scores = _dot(cm, bm, _NT)
    for acs_col, acs_row, mk in heads:
        dec = jnp.exp(jnp.where(causal, acs_col - acs_row, -jnp.inf))
        y = y + _dot(scores * dec, xdt * mk, _NN)
    return y, hp * jnp.exp(end_exp) + states


def _ssd_fwd(act, dtraw, dtb, alog, drow, *, name, side=None):
    L = act.shape[0]
    q = min(SSD_CHUNK, L)
    nc = L // q
    d_ssd = N_GROUPS * GROUP_W

    def body(act_ref, dt_ref, dtb_ref, alog_ref, drow_ref, y_ref, hst_ref, h):
        @pl.when(pl.program_id(0) == 0)
        def _():
            h[...] = jnp.zeros_like(h)

        dt, acs = _ssd_f1(dt_ref[...], dtb_ref[...], alog_ref[...])
        drow = drow_ref[...]
        for g in range(N_GROUPS):
            hp = h[g]
            hst_ref[0, g] = hp
            y, hn = _ssd_group(g, act_ref[:, g * GROUP_W:(g + 1) * GROUP_W],
                               act_ref[:, d_ssd + g * STATE:d_ssd + (g + 1) * STATE],
                               act_ref[:, d_ssd + (N_GROUPS + g) * STATE:d_ssd + (N_GROUPS + g + 1) * STATE],
                               dt, acs, drow, hp)
            y_ref[:, g * GROUP_W:(g + 1) * GROUP_W] = y
            h[g] = hn

    row = pl.BlockSpec((1, 128), lambda i: (0, 0))
    outs, side_outs = _call(
        body, (act, dtraw, dtb, alog, drow), name=name, grid=(nc,),
        in_specs=[pl.BlockSpec((q, act.shape[1]), lambda i: (i, 0)), pl.BlockSpec((q, 128), lambda i: (i, 0)), row, row, row],
        out_specs=[pl.BlockSpec((q, d_ssd), lambda i: (i, 0)),
                   pl.BlockSpec((1, N_GROUPS, STATE, GROUP_W), lambda i: (i, 0, 0, 0))],
        out_shape=[jax.ShapeDtypeStruct((L, d_ssd), F32), jax.ShapeDtypeStruct((nc, N_GROUPS, STATE, GROUP_W), F32)],
        scratch_shapes=[pltpu.VMEM((N_GROUPS, STATE, GROUP_W), F32)], sem=("arbitrary",), side=side)
    return outs if side is None else (outs, side_outs)


def _ssd_bwd(act, dtraw, dy, hst, dtb, alog, drow, *, name, side=None):
    L = act.shape[0]
    q = min(SSD_CHUNK, L)
    nc = L // q
    d_ssd = N_GROUPS * GROUP_W

    def body(act_ref, dt_ref, dy_ref, hst_ref, dtb_ref, alog_ref, drow_ref, dact_ref, ddt_ref, dpar_ref, dh):
        @pl.when(pl.program_id(0) == 0)
        def _():
            dh[...] = jnp.zeros_like(dh)
            dpar_ref[...] = jnp.zeros_like(dpar_ref)

        (dt, acs), f1_vjp = jax.vjp(_ssd_f1, dt_ref[...], dtb_ref[...], alog_ref[...])
        drow = drow_ref[...]
        ddt = jnp.zeros_like(dt)
        dacs = jnp.zeros_like(acs)
        ddrow = jnp.zeros_like(drow)
        for g in range(N_GROUPS):
            xs = slice(g * GROUP_W, (g + 1) * GROUP_W)
            bs = slice(d_ssd + g * STATE, d_ssd + (g + 1) * STATE)
            cs = slice(d_ssd + (N_GROUPS + g) * STATE, d_ssd + (N_GROUPS + g + 1) * STATE)
            _, f2_vjp = jax.vjp(functools.partial(_ssd_group, g), act_ref[:, xs], act_ref[:, bs], act_ref[:, cs],
                                dt, acs, drow, hst_ref[0, g])
            dx, dbm, dcm, ddt_g, dacs_g, ddrow_g, dhp = f2_vjp((dy_ref[:, xs], dh[g]))
            dact_ref[:, xs] = dx
            dact_ref[:, bs] = dbm
            dact_ref[:, cs] = dcm
            dh[g] = dhp
            ddt, dacs, ddrow = ddt + ddt_g, dacs + dacs_g, ddrow + ddrow_g
        ddtraw, ddtb, dalog = f1_vjp((ddt, dacs))
        ddt_ref[...] = ddtraw
        dpar_ref[0:1, :] += ddtb
        dpar_ref[1:2, :] += dalog
        dpar_ref[2:3, :] += ddrow

    row = pl.BlockSpec((1, 128), lambda i: (0, 0))
    rev = lambda i: (nc - 1 - i, 0)
    outs, side_outs = _call(
        body, (act, dtraw, dy, hst, dtb, alog, drow), name=name, grid=(nc,),
        in_specs=[pl.BlockSpec((q, act.shape[1]), rev), pl.BlockSpec((q, 128), rev), pl.BlockSpec((q, d_ssd), rev),
                  pl.BlockSpec((1, N_GROUPS, STATE, GROUP_W), lambda i: (nc - 1 - i, 0, 0, 0)), row, row, row],
        out_specs=[pl.BlockSpec((q, act.shape[1]), rev), pl.BlockSpec((q, 128), rev), pl.BlockSpec((8, 128), lambda i: (0, 0))],
        out_shape=[jax.ShapeDtypeStruct(act.shape, F32), jax.ShapeDtypeStruct((L, 128), F32), jax.ShapeDtypeStruct((8, 128), F32)],
        scratch_shapes=[pltpu.VMEM((N_GROUPS, STATE, GROUP_W), F32)], sem=("arbitrary",), side=side)
    return outs if side is None else (outs, side_outs)


def _gate_head(xl, wa, ba, wx, bx, lam):
    r = _sigmoid(_dot(xl, wa, _NN) + ba)
    i = _sigmoid(_dot(xl, wx, _NN) + bx)
    log_a = -LRU_C * r * _softplus(-lam)
    return jnp.exp(log_a), jnp.sqrt(-_expm1(2.0 * log_a)) * (i * xl)


def _gate_head_bwd(xl, da, du, wa, ba, wx, bx, lam):
    _, vjp = jax.vjp(_gate_head, xl, wa, ba, wx, bx, lam)
    dxl, dwa, dba, dwx, dbx, dlam = vjp((da, du))
    return dxl, dwa, dba, dwx, dbx, dlam


def _scan_tile(a, b, rows, reverse):
    for d in (1, 2, 4):
        if reverse:
            keep = rows < 8 - d
            a_sh, b_sh = pltpu.roll(a, 8 - d, 0), pltpu.roll(b, 8 - d, 0)
        else:
            keep = rows >= d
            a_sh, b_sh = pltpu.roll(a, d, 0), pltpu.roll(b, d, 0)
        b = b + a * jnp.where(keep, b_sh, 0.0)
        a = a * jnp.where(keep, a_sh, 1.0)
    return a, b


def _lru_scan_fwd(a, u, *, name, tr=512, cb=1024):
    L, C = a.shape
    tr, cb = min(tr, L), min(cb, C)

    def body(a_ref, u_ref, h_ref, hp_ref, carry):
        @pl.when(pl.program_id(1) == 0)
        def _():
            carry[...] = jnp.zeros_like(carry)

        rows = lax.broadcasted_iota(jnp.int32, (8, cb), 0)

        def tile(t, hc):
            r0 = pl.multiple_of(t * 8, 8)
            pa, hb = _scan_tile(a_ref[pl.ds(r0, 8), :], u_ref[pl.ds(r0, 8), :], rows, False)
            h = hb + pa * hc
            h_ref[pl.ds(r0, 8), :] = h
            hp_ref[pl.ds(r0, 8), :] = jnp.where(rows >= 1, pltpu.roll(h, 1, 0), hc)
            return h[7:8, :]

        carry[...] = lax.fori_loop(0, tr // 8, tile, carry[...])

    blk = pl.BlockSpec((tr, cb), lambda j, i: (i, j))
    return pl.pallas_call(
        body, name=name, grid=(C // cb, L // tr),
        in_specs=[blk, blk], out_specs=[blk, blk],
        out_shape=[jax.ShapeDtypeStruct((L, C), F32), jax.ShapeDtypeStruct((L, C), F32)],
        scratch_shapes=[pltpu.VMEM((1, cb), F32)],
        compiler_params=_params(("parallel", "arbitrary")),
    )(a, u)


def _lru_scan_bwd(a, hprev, dh, *, name, tr=512, cb=1024):
    L, C = a.shape
    tr, cb = min(tr, L), min(cb, C)
    nb = L // tr

    def body(a_ref, hp_ref, dh_ref, da_ref, du_ref, carry):
        @pl.when(pl.program_id(1) == 0)
        def _():
            carry[...] = jnp.zeros_like(carry)

        rows = lax.broadcasted_iota(jnp.int32, (8, cb), 0)

        def tile(t, gc):
            r0 = pl.multiple_of((tr // 8 - 1 - t) * 8, 8)
            av, dv = a_ref[pl.ds(r0, 8), :], dh_ref[pl.ds(r0, 8), :]
            pa, gb = _scan_tile(av, av * dv, rows, True)
            big = gb + pa * gc
            g = dv + jnp.where(rows < 7, pltpu.roll(big, 7, 0), gc)
            du_ref[pl.ds(r0, 8), :] = g
            da_ref[pl.ds(r0, 8), :] = g * hp_ref[pl.ds(r0, 8), :]
            return big[0:1, :]

        carry[...] = lax.fori_loop(0, tr // 8, tile, carry[...])

    blk = pl.BlockSpec((tr, cb), lambda j, i: (nb - 1 - i, j))
    return pl.pallas_call(
        body, name=name, grid=(C // cb, nb),
        in_specs=[blk, blk, blk], out_specs=[blk, blk],
        out_shape=[jax.ShapeDtypeStruct((L, C), F32), jax.ShapeDtypeStruct((L, C), F32)],
        scratch_shapes=[pltpu.VMEM((1, cb), F32)],
        compiler_params=_params(("parallel", "arbitrary")),
    )(a, hprev, dh)


def _ssd_gate(y, z, n):
    v = y * _silu(z)
    return v * lax.rsqrt(jnp.mean(v * v, axis=-1, keepdims=True) + EPS) * n


def _ssd_gate_bwd(y, z, dy, n):
    _, vjp = jax.vjp(_ssd_gate, y, z, n)
    return vjp(dy)


def _lru_out(hl, gate, n):
    return _rms(hl * _gelu(gate), n)


def _lru_out_bwd(hl, gate, dy, n):
    _, vjp = jax.vjp(_lru_out, hl, gate, n)
    return vjp(dy)


def _mid(x, mix, pm, pmlp):
    x1 = x + _rms(mix, pm)
    return x1, _rms(x1, pmlp)


def _mid_bwd(x, mix, dx1p, dh2, pm, pmlp):
    _, vjp = jax.vjp(_mid, x, mix, pm, pmlp)
    dx, dmix, dpm, dpmlp = vjp((dx1p, dh2))
    return dmix, dx, dpm, dpmlp


def _loss_bwd(hm2, x1, tgt, g):
    def lossf(hm2, x1, g):
        e = x1 + _rms(hm2, g) - tgt
        return 0.5 * jnp.sum(jnp.mean(e * e, axis=-1, keepdims=True), axis=0, keepdims=True)

    val, vjp = jax.vjp(lossf, hm2, x1, g)
    dhm2, dx1, dg = vjp(jnp.ones((1, 1), F32))
    return dhm2, dx1, dg, val * jnp.ones((1, 128), F32)


def _in_bwd(x, dh_a, dh_b, dx1, g):
    _, vjp = jax.vjp(_rms, x, g)
    dx, dg = vjp(dh_a + dh_b)
    return dx + dx1, dg


def _adamw(w, g, m, v):
    m = ADAM_B1 * m + (1.0 - ADAM_B1) * g
    v = ADAM_B2 * v + (1.0 - ADAM_B2) * (g * g)
    m_hat = m / (1.0 - ADAM_B1 ** ADAM_STEP)
    v_hat = v / (1.0 - ADAM_B2 ** ADAM_STEP)
    return -ADAM_LR * (m_hat / (jnp.sqrt(v_hat) + ADAM_EPS) + ADAM_WD * w), m, v


class _LocalPlan:
    def __init__(self, p):
        self.p, self.large = p, {}

    def weight(self, name):
        return self.p[name]

    def side(self, kernel_name):
        return None

    def done(self, kernel_name, side_outs):
        pass

    def grad(self, name, g):
        self.large[name] = g

    def small(self, grads):
        pass


def _local_step(x, tgt, p, plan):
    L, D = x.shape

    def carry(fn, *args, name, **kw):
        side = plan.side(name)
        if side is None:
            return fn(*args, name=name, **kw)
        outs, side_outs = fn(*args, name=name, side=side, **kw)
        plan.done(name, side_outs)
        return outs

    d_ssd, d_xbc, d_lru, d_mix, d_ff = 2048, 4096, 2048, 4096, 8192
    n_main = d_ssd + d_xbc + 2 * d_lru
    c_xbc, c_gate, c_xl = d_ssd, d_ssd + d_xbc, d_ssd + d_xbc + d_lru
    TR, SUB = 256, 32
    mm = dict(tm=1024, tn=1024, tk=2048)

    (h,) = _rowwise(lambda xt, g: (_rms(xt, g),), [x], [p["pre_mix_norm"]], [(D, BF16)], [], name="f_prenorm", tr=TR, sub=SUB)
    w_main, w_dt = plan.weight("w_main"), plan.weight("w_dt")
    (proj,) = carry(_matmul, h, w_main, mode="nn", m=L, n=n_main, k=D, out_dtypes=[F32], name="f_inproj", **mm)
    (dtraw,) = _matmul(h, w_dt, mode="nn", m=L, n=128, k=D, out_dtypes=[F32], name="f_dtproj", **mm)
    pre, act = carry(_conv_fwd, proj, c_xbc, d_xbc, p["ssd_conv_w"], p["ssd_conv_b"], silu=True, name="f_ssdconv")
    (xl,) = carry(_conv_fwd, proj, c_xl, d_lru, p["lru_conv_w"], p["lru_conv_b"], silu=False, name="f_lruconv")
    yraw, hst = carry(_ssd_fwd, act, dtraw, p["dtb"], p["alog"], p["drow"], name="f_ssd")
    (yssd,) = _grouped(lambda y, z, n: (_ssd_gate(y, z, n),), [yraw, (proj, 0)], [p["ssd_norm"]], [BF16], [],
                       gw=GROUP_W, name="f_ssdgate", tr=TR)
    gate_p = [p["lru_w_a"], p["lru_b_a"], p["lru_w_x"], p["lru_b_x"], p["lru_lambda"]]
    a, u = carry(_grouped, _gate_head, [xl], gate_p, [F32, F32], [], gw=LRU_BLOCK, name="f_lrugates", tr=TR)
    hl, hprev = _lru_scan_fwd(a, u, name="f_lruscan")
    (ylru,) = _rowwise(lambda ht, gt, n: (_lru_out(ht, gt, n),), [hl, (proj, c_gate, d_lru)], [p["lru_norm"]],
                       [(d_lru, BF16)], [], name="f_lruout", tr=TR, sub=SUB)
    ycat = jnp.concatenate([yssd, ylru], axis=1)
    w_out = plan.weight("w_out")
    (mix,) = carry(_matmul, ycat, w_out, mode="nn", m=L, n=D, k=d_mix, out_dtypes=[F32], name="f_outproj", **mm)
    x1, h2 = _rowwise(_mid, [x, mix], [p["post_mix_norm"], p["pre_mlp_norm"]], [(D, F32), (D, BF16)], [],
                      name="f_mid", tr=TR, sub=SUB)
    nb_mi = (d_ff // 4) // mm["tn"]
    w_mi = plan.weight("w_mi")
    hm, act2 = carry(_matmul, h2, w_mi, mode="nn", m=L, n=d_ff, k=D, out_dtypes=[BF16, BF16], name="f_mlpin",
                     b_spec=pl.BlockSpec((None, mm["tk"], mm["tn"]), lambda i, j, kk: (j // nb_mi, kk, j % nb_mi)),
                     epilogue=lambda r: (r, jnp.square(jnp.maximum(r, 0.0))), **mm)
    w_mo = plan.weight("w_mo")
    (hm2,) = _matmul(act2, w_mo, mode="nn", m=L, n=D, k=d_ff, out_dtypes=[F32], name="f_mlpout", **mm)

    dhm2, dx1p, d_post_mlp, loss = _rowwise(_loss_bwd, [hm2, x1, tgt], [p["post_mlp_norm"]], [(D, BF16), (D, F32)],
                                            [(1, D), (1, 128)], name="b_loss", tr=TR, sub=SUB)
    (dhm,) = _matmul(dhm2, w_mo, mode="nt", m=L, n=d_ff, k=D, out_dtypes=[BF16], name="b_mlpout_dx", extras=[hm],
                     epilogue=lambda r, hmv: (r * (2.0 * jnp.maximum(hmv.astype(F32), 0.0)),), **mm)
    (dw_mo,) = _matmul(act2, dhm2, mode="tn", m=d_ff, n=D, k=L, out_dtypes=[BF16], name="b_mlpout_dw", **mm)
    plan.grad("w_mlp_out", dw_mo.reshape(4, -1, D))
    kb_mi = (d_ff // 4) // mm["tk"]
    (dh2,) = carry(_matmul, dhm, w_mi, mode="nt", m=L, n=D, k=d_ff, out_dtypes=[F32], name="b_mlpin_dx",
                   b_spec=pl.BlockSpec((None, mm["tn"], mm["tk"]), lambda i, j, kk: (kk // kb_mi, j, kk % kb_mi)), **mm)
    (dw_mi,) = carry(_matmul, h2, dhm, mode="tn", m=D, n=d_ff, k=L, out_dtypes=[BF16], name="b_mlpin_dw",
                     out_specs=[pl.BlockSpec((None, mm["tm"], mm["tn"]), lambda i, j, kk: (j // nb_mi, i, j % nb_mi))],
                     out_shapes=[jax.ShapeDtypeStruct((4, D, d_ff // 4), BF16)], **mm)
    plan.grad("w_mlp_in", dw_mi)
    dmix, dx1, d_post_mix, d_pre_mlp = carry(_rowwise, _mid_bwd, [x, mix, dx1p, dh2], [p["post_mix_norm"], p["pre_mlp_norm"]],
                                             [(D, BF16), (D, F32)], [(1, D), (1, D)], name="b_mid", tr=TR, sub=SUB)
    (dw_out,) = carry(_matmul, ycat, dmix, mode="tn", m=d_mix, n=D, k=L, out_dtypes=[BF16], name="b_outproj_dw", **mm)
    plan.grad("w_out", dw_out.reshape(4, -1, D))
    (dycat,) = carry(_matmul, dmix, w_out, mode="nt", m=L, n=d_mix, k=D, out_dtypes=[F32], name="b_outproj_dx", **mm)
    dhl, dgate, d_lru_norm = _rowwise(_lru_out_bwd, [hl, (proj, c_gate, d_lru), (dycat, d_ssd, d_lru)], [p["lru_norm"]],
                                      [(d_lru, F32), (d_lru, BF16)], [(1, d_lru)], name="b_lruout", tr=TR, sub=SUB)
    da, du = _lru_scan_bwd(a, hprev, dhl, name="b_lruscan")
    dxl, d_wa, d_ba, d_wx, d_bx, d_lam = _grouped(
        _gate_head_bwd, [xl, da, du], gate_p, [F32],
        [(LRU_HEADS, LRU_BLOCK, LRU_BLOCK), (1, d_lru), (LRU_HEADS, LRU_BLOCK, LRU_BLOCK), (1, d_lru), (1, d_lru)],
        gw=LRU_BLOCK, name="b_lrugates", tr=TR)
    dxlru, dwb_lru = _conv_bwd(dxl, 0, None, proj, c_xl, d_lru, p["lru_conv_w"], silu=False, name="b_lruconv")
    dyraw, dz, d_ssd_norm = _grouped(_ssd_gate_bwd, [yraw, (proj, 0), (dycat, 0)], [p["ssd_norm"]], [F32, BF16], [(1, d_ssd)],
                                     gw=GROUP_W, name="b_ssdgate", tr=TR)
    dact, ddtraw, dpar = carry(_ssd_bwd, act, dtraw, dyraw, hst, p["dtb"], p["alog"], p["drow"], name="b_ssd")
    dxbc, dwb_ssd = carry(_conv_bwd, dact, 0, pre, proj, c_xbc, d_xbc, p["ssd_conv_w"], silu=True, name="b_ssdconv")
    early = dict(loss=loss, ssd_conv=dwb_ssd, ssd_par=dpar, ssd_norm=d_ssd_norm, lru_conv=dwb_lru, lru_w_a=d_wa, lru_b_a=d_ba,
                 lru_w_x=d_wx, lru_b_x=d_bx, lru_lambda=d_lam, lru_norm=d_lru_norm, post_mix_norm=d_post_mix,
                 pre_mlp_norm=d_pre_mlp, post_mlp_norm=d_post_mlp)
    plan.small(early)
    dproj = jnp.concatenate([dz, dxbc, dgate, dxlru], axis=1)
    (dw_main,) = carry(_matmul, h, dproj, mode="tn", m=D, n=n_main, k=L, out_dtypes=[BF16], name="b_inproj_dw", **mm)
    (dw_dt,) = _matmul(h, ddtraw, mode="tn", m=D, n=128, k=L, out_dtypes=[BF16], name="b_dtproj_dw", **mm)
    plan.grad("w_in", (dw_main, dw_dt))
    (dh_a,) = carry(_matmul, dproj, w_main, mode="nt", m=L, n=D, k=n_main, out_dtypes=[F32], name="b_inproj_dx", **mm)
    (dh_b,) = _matmul(ddtraw, w_dt, mode="nt", m=L, n=D, k=128, out_dtypes=[F32], name="b_dtproj_dx", **mm)
    grad_x, d_pre_mix = carry(_rowwise, _in_bwd, [x, dh_a, dh_b, dx1], [p["pre_mix_norm"]], [(D, F32)], [(1, D)],
                              name="b_prenorm", tr=TR, sub=SUB)

    return grad_x, dict(early, pre_mix_norm=d_pre_mix)


def _place():
    return lax.axis_index("x"), lax.axis_index("y"), lax.axis_index("c")


def _other_chips(x, y):
    return [(1 - x, y), (x, 1 - y), (1 - x, 1 - y)]


def _allgather8(blk, *, name):
    r, n = blk.shape

    def body(x_ref, out_ref, send_sems, recv_sems, local_sem):
        x, y, c = _place()
        me, sibling = (x, y, c), (x, y, 1 - c)
        chips = _other_chips(x, y)

        def rows(px, py, pc):
            return out_ref.at[pl.ds((4 * px + 2 * py + pc) * r, r), :]

        def copy(k, block, to, src=None):
            return pltpu.make_async_remote_copy(
                src_ref=rows(*block) if src is None else src, dst_ref=rows(*block),
                send_sem=send_sems.at[k], recv_sem=recv_sems.at[k], device_id=to, device_id_type=MESH)

        mine = pltpu.make_async_copy(x_ref, rows(*me), local_sem)
        mine.start()
        first = [copy(0, me, sibling, src=x_ref)]
        first += [copy(1 + k, me, (*chip, c), src=x_ref) for k, chip in enumerate(chips)]
        for cp in first:
            cp.start()
        passed = [copy(4 + k, (*chip, c), sibling) for k, chip in enumerate(chips)]
        for k, chip in enumerate(chips):
            copy(1 + k, (*chip, c), me).wait_recv()
            passed[k].start()
        copy(0, sibling, me).wait_recv()
        for k, chip in enumerate(chips):
            copy(4 + k, (*chip, 1 - c), me).wait_recv()
        for cp in first + passed:
            cp.wait_send()
        mine.wait()

    return pl.pallas_call(
        body, name=name,
        out_shape=jax.ShapeDtypeStruct((8 * r, n), blk.dtype),
        in_specs=[pl.BlockSpec(memory_space=pltpu.VMEM)], out_specs=pl.BlockSpec(memory_space=pltpu.VMEM),
        scratch_shapes=[pltpu.SemaphoreType.DMA((7,)), pltpu.SemaphoreType.DMA((7,)), pltpu.SemaphoreType.DMA],
        compiler_params=pltpu.CompilerParams(vmem_limit_bytes=VMEM_LIMIT),
    )(blk)


def _sum8(g, *, name):
    _, r, n = g.shape
    tr = max(t for t in range(8, min(r, 512) + 1, 8) if r % t == 0)

    def body(g_ref, o_ref):
        s = g_ref[0]
        for k in range(1, 8):
            s = s + g_ref[k]
        o_ref[...] = s

    return pl.pallas_call(
        body, name=name, grid=(r // tr,),
        in_specs=[pl.BlockSpec((8, tr, n), lambda i: (0, i, 0))], out_specs=pl.BlockSpec((tr, n), lambda i: (i, 0)),
        out_shape=jax.ShapeDtypeStruct((r, n), g.dtype), compiler_params=_params(("parallel",)),
    )(g)


def _blocks(fn, ins, outs, *, grid, name, prefetch=None, aliases=None):
    n_in = len(ins)

    def body(*refs):
        if prefetch is not None:
            refs = refs[1:]
        res = fn(*[r[...] for r in refs[:n_in]])
        for o_ref, o in zip(refs[n_in:], res):
            o_ref[...] = o.astype(o_ref.dtype)

    in_specs = [pl.BlockSpec(b, m) for _, b, m in ins]
    out_specs = [pl.BlockSpec(b, m) for _, b, m in outs]
    kw = dict(name=name, out_shape=[s for s, _, _ in outs], input_output_aliases=aliases or {},
              compiler_params=_params(("arbitrary",) * len(grid)))
    arrs = [a for a, _, _ in ins]
    if prefetch is None:
        return pl.pallas_call(body, grid=grid, in_specs=in_specs, out_specs=out_specs, **kw)(*arrs)
    spec = pltpu.PrefetchScalarGridSpec(num_scalar_prefetch=1, grid=grid, in_specs=in_specs, out_specs=out_specs)
    return pl.pallas_call(body, grid_spec=spec, **kw)(prefetch, *arrs)


def _gather_weights(slots, *, name):
    n = len(slots)

    def body(*refs):
        i_refs, g_refs = refs[:n], refs[n:2 * n]
        send_sems, recv_sems = refs[2 * n:]
        x, y, c = _place()
        j, sibling = 2 * x + y, (x, y, 1 - c)
        chips = _other_chips(x, y)

        def cp(w, k, src, dst, to):
            return pltpu.make_async_remote_copy(src_ref=src, dst_ref=dst, send_sem=send_sems.at[6 * w + k],
                                                recv_sem=recv_sems.at[6 * w + k], device_id=to, device_id_type=MESH)

        def part(ref, chip_idx, hc):
            rh = ref.shape[1] // 2
            return ref.at[chip_idx, pl.ds(hc * rh, rh), :]

        sends = []
        for w in range(n):
            for k, chip in enumerate(chips):
                d = cp(w, k, part(i_refs[w], j, c), part(g_refs[w], j, c), (*chip, c))
                d.start()
                sends.append(d)
        for w in range(n):
            for k, (px, py) in enumerate(chips):
                landed = part(g_refs[w], 2 * px + py, c)
                cp(w, k, landed, landed, (px, py, c)).wait_recv()
                d = cp(w, 3 + k, landed, landed, sibling)
                d.start()
                sends.append(d)
        for w in range(n):
            for k, (px, py) in enumerate(chips):
                other = part(g_refs[w], 2 * px + py, 1 - c)
                cp(w, 3 + k, other, other, sibling).wait_recv()
        for d in sends:
            d.wait_send()

    return pl.pallas_call(
        body, name=name,
        out_shape=[jax.ShapeDtypeStruct(s.shape, s.dtype) for s in slots],
        in_specs=[ANY] * n, out_specs=[ANY] * n, input_output_aliases={w: w for w in range(n)},
        scratch_shapes=[pltpu.SemaphoreType.DMA((6 * n,)), pltpu.SemaphoreType.DMA((6 * n,))],
    )(*slots)


def _pack(arrs):
    parts = []
    for v in arrs:
        f = v.reshape(-1)
        f = jnp.pad(f, (0, (-f.shape[0]) % 1024))
        parts.append(f.reshape(-1, 128))
    return jnp.concatenate(parts, axis=0)


def _unpack(packed, shapes):
    out, r0 = [], 0
    for s in shapes:
        size = 1
        for d in s:
            size *= d
        nr = (size + 1023) // 1024 * 8
        out.append(packed[r0:r0 + nr].reshape(-1)[:size].reshape(s))
        r0 += nr
    return out


SMALL_GRADS = [("loss", (1, 128)), ("pre_mix_norm", (1, 2048)), ("ssd_conv", (8, 4096)), ("ssd_par", (8, 128)),
               ("ssd_norm", (1, 2048)), ("lru_conv", (8, 2048)), ("lru_w_a", (16, 128, 128)), ("lru_b_a", (1, 2048)),
               ("lru_w_x", (16, 128, 128)), ("lru_b_x", (1, 2048)), ("lru_lambda", (1, 2048)), ("lru_norm", (1, 2048)),
               ("post_mix_norm", (1, 2048)), ("pre_mlp_norm", (1, 2048)), ("post_mlp_norm", (1, 2048))]

SMALL_EARLY = [g for g in SMALL_GRADS if g[0] != "pre_mix_norm"]

WEIGHTS = ['pre_mix_norm', 'w_in', 'ssd_conv_w', 'ssd_conv_b', 'ssd_dt_bias', 'ssd_a_log', 'ssd_d', 'ssd_norm', 'lru_conv_w',
           'lru_conv_b', 'lru_w_a', 'lru_b_a', 'lru_w_x', 'lru_b_x', 'lru_lambda', 'lru_norm', 'w_out', 'post_mix_norm',
           'pre_mlp_norm', 'w_mlp_in', 'w_mlp_out', 'post_mlp_norm']
LARGE = ['w_in', 'w_out', 'w_mlp_in', 'w_mlp_out']

D_SSD, D_XBC, DT_W = 2048, 4096, 32
TB = 256


def _w_in_runs(n_shard, n_main):
    c_dt = D_SSD + D_XBC
    runs, p = [], 0
    while p < n_main:
        j, off = divmod(p if p < c_dt else p + DT_W, n_shard)
        ln = min(n_shard - off, (c_dt if p < c_dt else n_main) - p)
        runs.append((p, j, off, ln))
        p += ln
    jd, offd = divmod(c_dt, n_shard)
    assert offd + DT_W <= n_shard
    return runs, (jd, offd)


def _pack_w_in(slots, *, name):
    _, d, n_shard = slots.shape
    n_main = 4 * n_shard - DT_W
    runs, (jd, offd) = _w_in_runs(n_shard, n_main)

    def body(s_ref, main_ref, dt_ref):
        for p, j, off, ln in runs:
            main_ref[:, p:p + ln] = s_ref[j, :, off:off + ln]
        dt_ref[:, 0:DT_W] = s_ref[jd, :, offd:offd + DT_W]
        dt_ref[:, DT_W:] = jnp.zeros((TB, 128 - DT_W), dt_ref.dtype)

    return pl.pallas_call(
        body, name=name, grid=(d // TB,),
        in_specs=[pl.BlockSpec((4, TB, n_shard), lambda i: (0, i, 0))],
        out_specs=[pl.BlockSpec((TB, n_main), lambda i: (i, 0)), pl.BlockSpec((TB, 128), lambda i: (i, 0))],
        out_shape=[jax.ShapeDtypeStruct((d, n_main), slots.dtype), jax.ShapeDtypeStruct((d, 128), slots.dtype)],
        compiler_params=_params(("parallel",)),
    )(slots)


def _unpack_dw_in(dw_main, dw_dt, *, name):
    d, n_main = dw_main.shape
    n_shard = (n_main + DT_W) // 4
    runs, (jd, offd) = _w_in_runs(n_shard, n_main)

    def body(main_ref, dt_ref, o_ref):
        for p, j, off, ln in runs:
            o_ref[j, :, off:off + ln] = main_ref[:, p:p + ln]
        o_ref[jd, :, offd:offd + DT_W] = dt_ref[:, 0:DT_W]

    return pl.pallas_call(
        body, name=name, grid=(d // TB,),
        in_specs=[pl.BlockSpec((TB, n_main), lambda i: (i, 0)), pl.BlockSpec((TB, 128), lambda i: (i, 0))],
        out_specs=pl.BlockSpec((4, TB, n_shard), lambda i: (0, i, 0)),
        out_shape=jax.ShapeDtypeStruct((4, d, n_shard), dw_main.dtype),
        compiler_params=_params(("parallel",)),
    )(dw_main, dw_dt)


def _half(ref, chip_idx, hc, piece=(0, 1)):
    q, nq = piece
    rp = ref.shape[1] // (2 * nq)
    return ref.at[chip_idx, pl.ds((hc * nq + q) * rp, rp), :]


def _job_gather_ici(i_ref, g_ref, send_sems, recv_sems, base, piece):
    x, y, c = _place()
    j = 2 * x + y
    sends, recvs = [], []
    for k, (px, py) in enumerate(_other_chips(x, y)):
        kw = dict(send_sem=send_sems.at[base + k], recv_sem=recv_sems.at[base + k], device_id=(px, py, c), device_id_type=MESH)
        sends.append(pltpu.make_async_remote_copy(src_ref=_half(i_ref, j, c, piece), dst_ref=_half(g_ref, j, c, piece), **kw))
        landed = _half(g_ref, 2 * px + py, c, piece)
        recvs.append(pltpu.make_async_remote_copy(src_ref=landed, dst_ref=landed, **kw))
    return sends, recvs


def _job_gather_sibling(i_ref, g_ref, send_sems, recv_sems, base, piece):
    x, y, c = _place()
    sends, recvs = [], []
    for k, (px, py) in enumerate(_other_chips(x, y)):
        kw = dict(send_sem=send_sems.at[base + k], recv_sem=recv_sems.at[base + k], device_id=(x, y, 1 - c), device_id_type=MESH)
        sends.append(pltpu.make_async_remote_copy(src_ref=_half(i_ref, 2 * px + py, c, piece),
                                                  dst_ref=_half(g_ref, 2 * px + py, c, piece), **kw))
        other = _half(g_ref, 2 * px + py, 1 - c, piece)
        recvs.append(pltpu.make_async_remote_copy(src_ref=other, dst_ref=other, **kw))
    return sends, recvs


def _job_reduce_ici(s_ref, got_refs, send_sems, recv_sems, base, piece):
    x, y, c = _place()
    q, nq = piece
    rp = s_ref.shape[1] // nq
    rows = pl.ds(q * rp, rp)
    sends = [pltpu.make_async_remote_copy(src_ref=s_ref.at[2 * px + py, rows, :], dst_ref=got_refs[k].at[rows, :],
                                          send_sem=send_sems.at[base + k], recv_sem=recv_sems.at[base + k],
                                          device_id=(px, py, c), device_id_type=MESH)
             for k, (px, py) in enumerate(_other_chips(x, y))]
    return sends, sends


GATHER_PLAN = {
    "f_inproj": [("w_out", _job_gather_ici, (0, 1)), ("w_mlp_in", _job_gather_ici, (0, 2))],
    "f_ssdconv": [("w_mlp_in", _job_gather_ici, (1, 2)), ("w_out", _job_gather_sibling, (0, 1)),
                  ("w_mlp_in", _job_gather_sibling, (0, 2))],
    "f_lruconv": [("w_mlp_in", _job_gather_sibling, (1, 2))],
    "f_ssd": [("w_mlp_out", _job_gather_ici, (0, 2))],
    "f_lrugates": [("w_mlp_out", _job_gather_sibling, (0, 2))],
    "f_outproj": [("w_mlp_out", _job_gather_ici, (1, 2))],
    "f_mlpin": [("w_mlp_out", _job_gather_sibling, (1, 2))],
}
SIBLING_PLAN = {"b_mlpin_dx": "w_mlp_out", "b_mid": "w_mlp_in", "b_outproj_dx": "w_out"}
REDUCE_PLAN = {
    "b_mlpin_dw": [("w_mlp_out", (0, 2))],
    "b_outproj_dw": [("w_mlp_out", (1, 2))],
    "b_ssd": [("w_mlp_in", (0, 1)), ("w_out", (0, 1))],
    "b_inproj_dx": [("w_in", (0, 1))],
}
SHARE_PLAN = {"b_ssdconv": ["w_mlp_out", "w_mlp_in", "w_out"]}
SMALL_PLAN = {"b_inproj_dw": 0, "b_inproj_dx": 1}


class _Part:
    def __init__(self, ins, outs, aliases, n_sems, make, done):
        self.ins, self.outs, self.aliases, self.n_sems, self.make, self.done = ins, outs, aliases, n_sems, make, done


def _merge_parts(parts):
    ins, outs, aliases, offs, n = [], [], {}, [], 0
    for p in parts:
        offs.append((len(ins), len(outs), n))
        aliases.update({len(ins) + i: len(outs) + o for i, o in p.aliases.items()})
        ins, outs, n = ins + list(p.ins), outs + list(p.outs), n + p.n_sems

    def make(i_refs, o_refs, send_sems, recv_sems):
        sends, recvs = [], []
        for p, (io, oo, so) in zip(parts, offs):
            s, r = p.make(i_refs[io:io + len(p.ins)], o_refs[oo:oo + len(p.outs)], send_sems, recv_sems, so)
            sends, recvs = sends + s, recvs + r
        return sends, recvs

    return _Side(ins, outs, aliases, n, make)


def _parts_done(parts, outs):
    o = 0
    for p in parts:
        p.done(list(outs[o:o + len(p.outs)]))
        o += len(p.outs)


def _comm_call(parts, *, name):
    side = _merge_parts(parts)
    si, so = len(side.ins), len(side.outs)

    def body(*refs):
        sends, recvs = side.make(refs[:si], refs[si:si + so], refs[-2], refs[-1])
        for d in sends:
            d.start()
        for d in recvs:
            d.wait_recv()
        for d in sends:
            d.wait_send()

    outs = pl.pallas_call(
        body, name=name, out_shape=list(side.outs), in_specs=[ANY] * si, out_specs=[ANY] * so,
        scratch_shapes=[pltpu.SemaphoreType.DMA((side.n_sems,)), pltpu.SemaphoreType.DMA((side.n_sems,))],
        input_output_aliases=side.aliases,
    )(*side.ins)
    _parts_done(parts, outs)


class _DistPlan:
    def __init__(self, slots, w_main, w_dt, where, shapes):
        self.slots, self.w_main, self.w_dt, self.where, self.shapes = slots, w_main, w_dt, where, shapes
        self.sib_pending, self.partial, self.got, self.totals, self.shared = {}, {}, {}, {}, {}
        self.small_buf = None

    def weight(self, name):
        d = self.w_main.shape[0]
        return {"w_main": lambda: self.w_main, "w_dt": lambda: self.w_dt, "w_out": lambda: self.slots["w_out"].reshape(-1, d),
                "w_mi": lambda: self.slots["w_mlp_in"], "w_mo": lambda: self.slots["w_mlp_out"].reshape(-1, d)}[name]()

    def _gather_part(self, jobs):
        names = []
        for n, _, _ in jobs:
            if n not in names:
                names.append(n)
        arrs = [self.slots[n] for n in names]

        def make(i_refs, o_refs, send_sems, recv_sems, base):
            sends, recvs = [], []
            for q, (n, job, piece) in enumerate(jobs):
                s, r = job(i_refs[names.index(n)], o_refs[names.index(n)], send_sems, recv_sems, base + 3 * q, piece)
                sends, recvs = sends + s, recvs + r
            return sends, recvs

        return _Part(arrs, [jax.ShapeDtypeStruct(a.shape, a.dtype) for a in arrs], {i: i for i in range(len(arrs))},
                     3 * len(jobs), make, lambda outs: self.slots.update(zip(names, outs)))

    def _sibling_part(self, name):
        g = self.sib_pending.pop(name)
        _, _, rh, cc = g.shape

        def make(i_refs, o_refs, send_sems, recv_sems, base):
            x, y, c = _place()
            sends = [pltpu.make_async_remote_copy(src_ref=i_refs[0].at[s, 1 - c], dst_ref=o_refs[0].at[s],
                                                  send_sem=send_sems.at[base + s], recv_sem=recv_sems.at[base + s],
                                                  device_id=(x, y, 1 - c), device_id_type=MESH) for s in range(4)]
            return sends, sends

        def done(outs):
            (t,) = outs
            self.partial[name] = _blocks(
                lambda u, v: (u.astype(F32) + v.astype(F32),),
                [(g, (None, None, TB, cc), lambda q, i, s: (q, s[1], i, 0)), (t, (None, TB, cc), lambda q, i, s: (q, i, 0))],
                [(jax.ShapeDtypeStruct(t.shape, BF16), (None, TB, cc), lambda q, i, s: (q, i, 0))],
                grid=(4, rh // TB), name="add_sibling_" + name, prefetch=self.where)[0]

        return _Part([g], [jax.ShapeDtypeStruct((4, rh, cc), g.dtype)], {}, 4, make, done)

    def _reduce_part(self, items):
        parts = [self.partial[n] for n, _ in items]
        ins, aliases = list(parts), {}
        for q, (n, (piece, _)) in enumerate(items):
            if piece > 0:
                for k in range(3):
                    aliases[len(ins)] = 3 * q + k
                    ins.append(self.got[n][k])

        def make(i_refs, o_refs, send_sems, recv_sems, base):
            sends = []
            for q, (_, piece) in enumerate(items):
                sends += _job_reduce_ici(i_refs[q], o_refs[3 * q:3 * q + 3], send_sems, recv_sems, base + 3 * q, piece)[0]
            return sends, sends

        def done(outs):
            for q, (name, (piece, n_pieces)) in enumerate(items):
                g3 = self.got[name] = outs[3 * q:3 * q + 3]
                if piece < n_pieces - 1:
                    continue
                s4 = self.partial[name]
                _, rh, cc = s4.shape
                row = ((TB, cc), lambda i, s: (i, 0))
                self.totals[name] = _blocks(
                    lambda o, r0, r1, r2: (((o.astype(F32) + r0.astype(F32)) + r1.astype(F32)) + r2.astype(F32),),
                    [(s4, (None, TB, cc), lambda i, s: (s[0], i, 0)), (g3[0],) + row, (g3[1],) + row, (g3[2],) + row],
                    [(jax.ShapeDtypeStruct((2, rh, cc), F32), (None, TB, cc), lambda i, s: (s[1], i, 0))],
                    grid=(rh // TB,), name="add_chips_" + name, prefetch=self.where)[0]
                if name == "w_in":
                    _comm_call([self._share_part([name])], name="share_" + name)

        outs = [jax.ShapeDtypeStruct(s.shape[1:], s.dtype) for s in parts for _ in range(3)]
        return _Part(ins, outs, aliases, 3 * len(items), make, done)

    def _share_part(self, names):
        ts = [self.totals[n] for n in names]

        def make(i_refs, o_refs, send_sems, recv_sems, base):
            x, y, c = _place()
            sends, recvs = [], []
            for w in range(len(ts)):
                kw = dict(send_sem=send_sems.at[base + w], recv_sem=recv_sems.at[base + w], device_id=(x, y, 1 - c),
                          device_id_type=MESH)
                sends.append(pltpu.make_async_remote_copy(src_ref=i_refs[w].at[c], dst_ref=o_refs[w].at[c], **kw))
                recvs.append(pltpu.make_async_remote_copy(src_ref=i_refs[w].at[c], dst_ref=o_refs[w].at[1 - c], **kw))
            return sends, recvs

        def done(outs):
            self.shared.update({n: o.reshape(self.shapes[n]) for n, o in zip(names, outs)})

        return _Part(ts, [jax.ShapeDtypeStruct(t.shape, t.dtype) for t in ts], {i: i for i in range(len(ts))}, len(ts), make, done)

    def _small_part(self, stage):
        buf = self.small_buf

        def make(i_refs, o_refs, send_sems, recv_sems, base):
            x, y, c = _place()
            src, dst = i_refs[0], o_refs[0]
            sends, recvs = [], []

            def add(k, block, to, frm):
                kw = dict(send_sem=send_sems.at[base + k], recv_sem=recv_sems.at[base + k], device_id=to, device_id_type=MESH)
                sends.append(pltpu.make_async_remote_copy(src_ref=src.at[block], dst_ref=dst.at[block], **kw))
                recvs.append(pltpu.make_async_remote_copy(src_ref=src.at[frm], dst_ref=dst.at[frm], **kw))

            if stage == 0:
                add(0, 4 * x + 2 * y + c, (x, y, 1 - c), 4 * x + 2 * y + 1 - c)
            for k, (px, py) in enumerate(_other_chips(x, y)):
                if stage == 0:
                    add(1 + k, 4 * x + 2 * y + c, (px, py, c), 4 * px + 2 * py + c)
                else:
                    add(k, 4 * px + 2 * py + c, (x, y, 1 - c), 4 * px + 2 * py + 1 - c)
            return sends, recvs

        def done(outs):
            (self.small_buf,) = outs

        return _Part([buf], [jax.ShapeDtypeStruct(buf.shape, buf.dtype)], {0: 0}, 4 if stage == 0 else 3, make, done)

    def side(self, kernel_name):
        parts = []
        if kernel_name in GATHER_PLAN:
            parts.append(self._gather_part(GATHER_PLAN[kernel_name]))
        if kernel_name in SIBLING_PLAN:
            parts.append(self._sibling_part(SIBLING_PLAN[kernel_name]))
        if kernel_name in REDUCE_PLAN:
            parts.append(self._reduce_part(REDUCE_PLAN[kernel_name]))
        if kernel_name in SHARE_PLAN:
            parts.append(self._share_part(SHARE_PLAN[kernel_name]))
        if kernel_name in SMALL_PLAN:
            parts.append(self._small_part(SMALL_PLAN[kernel_name]))
        self._carried = parts
        return _merge_parts(parts) if parts else None

    def done(self, kernel_name, side_outs):
        _parts_done(self._carried, side_outs)

    def grad(self, name, g):
        if name == "w_in":
            g = _unpack_dw_in(*g, name="unpack_dw_in")
        self.sib_pending[name] = g.reshape(4, 2, g.shape[1] // 2, g.shape[2])
        if name == "w_in":
            _comm_call([self._sibling_part(name)], name="reduce_sibling_" + name)

    def small(self, grads):
        packed = _pack([grads[n] for n, _ in SMALL_EARLY])
        n_rows = packed.shape[0]
        self.small_buf = _blocks(lambda t: (t,), [(packed, (n_rows, 128), lambda i, s: (0, 0))],
                                 [(jax.ShapeDtypeStruct((8, n_rows, 128), F32), (None, n_rows, 128), lambda i, s: (s[2], 0, 0))],
                                 grid=(1,), name="place_small", prefetch=self.where)[0]

    def finish(self):
        assert len(self.shared) == len(self.shapes)
        return self.shared


def kernel(x, pre_mix_norm, w_in, ssd_conv_w, ssd_conv_b, ssd_dt_bias, ssd_a_log, ssd_d, ssd_norm, lru_conv_w, lru_conv_b, lru_w_a, lru_b_a, lru_w_x, lru_b_x, lru_lambda, lru_norm, w_out, post_mix_norm, pre_mlp_norm, w_mlp_in, w_mlp_out, post_mlp_norm, loss_target, m_pre_mix_norm, m_w_in, m_ssd_conv_w, m_ssd_conv_b, m_ssd_dt_bias, m_ssd_a_log, m_ssd_d, m_ssd_norm, m_lru_conv_w, m_lru_conv_b, m_lru_w_a, m_lru_b_a, m_lru_w_x, m_lru_b_x, m_lru_lambda, m_lru_norm, m_w_out, m_post_mix_norm, m_pre_mlp_norm, m_w_mlp_in, m_w_mlp_out, m_post_mlp_norm, v_pre_mix_norm, v_w_in, v_ssd_conv_w, v_ssd_conv_b, v_ssd_dt_bias, v_ssd_a_log, v_ssd_d, v_ssd_norm, v_lru_conv_w, v_lru_conv_b, v_lru_w_a, v_lru_b_a, v_lru_w_x, v_lru_b_x, v_lru_lambda, v_lru_norm, v_w_out, v_post_mix_norm, v_pre_mlp_norm, v_w_mlp_in, v_w_mlp_out, v_post_mlp_norm):
    a = dict(locals())
    j = 2 * lax.axis_index("x") + lax.axis_index("y")
    D = x.shape[-1]
    c_dt = D_SSD + D_XBC

    core = lax.axis_index("c")
    where = jnp.stack([j, core, 2 * j + core]).astype(jnp.int32)
    slots = {}
    for name in LARGE:
        w = a[name][0]
        r, cc = w.shape
        slots[name] = _blocks(lambda t: (t,), [(w, (TB, cc), lambda i, s: (i, 0))],
                              [(jax.ShapeDtypeStruct((4, r, cc), BF16), (None, TB, cc), lambda i, s: (s[0], i, 0))],
                              grid=(r // TB,), name="cast_" + name, prefetch=where)[0]
    (g_in,) = _gather_weights([slots.pop("w_in")], name="gather_w_in")
    w_main, w_dt = _pack_w_in(g_in, name="pack_w_in")
    taps = jnp.concatenate([ssd_conv_w[0].reshape(-1, 128), lru_conv_w[0].reshape(-1, 128)], axis=0)
    taps = _allgather8(taps, name="gather_taps").reshape(8, taps.shape[0], 128)[0::2]
    n_ssd = ssd_conv_w.shape[1] * ssd_conv_w.shape[2] // 128
    ssd_taps = taps[:, :n_ssd].reshape(4, CONV_W, -1).transpose(1, 0, 2).reshape(CONV_W, -1)
    lru_taps = taps[:, n_ssd:].reshape(4, CONV_W, -1).transpose(1, 0, 2).reshape(CONV_W, -1)

    def row128(v):
        return jnp.pad(v, ((0, 0), (0, 128 - v.shape[1])))

    p = dict(pre_mix_norm=pre_mix_norm, ssd_conv_w=ssd_taps, ssd_conv_b=ssd_conv_b,
             dtb=row128(ssd_dt_bias), alog=row128(ssd_a_log), drow=row128(ssd_d), ssd_norm=ssd_norm,
             lru_conv_w=lru_taps, lru_conv_b=lru_conv_b, lru_w_a=lru_w_a[0], lru_b_a=lru_b_a.reshape(1, -1),
             lru_w_x=lru_w_x[0], lru_b_x=lru_b_x.reshape(1, -1), lru_lambda=lru_lambda, lru_norm=lru_norm,
             post_mix_norm=post_mix_norm, pre_mlp_norm=pre_mlp_norm, post_mlp_norm=post_mlp_norm)
    plan = _DistPlan(slots, w_main, w_dt, where, {n: a[n].shape for n in LARGE})
    grad_x, small = _local_step(x[0], loss_target[0], p, plan)

    large_grads = plan.finish()

    delta, new_m, new_v = {}, {}, {}
    for name in LARGE:
        w = a[name][0]
        cc = w.shape[1]
        outs = _rowwise(_adamw, [w, large_grads[name][0], a["m_" + name][0], a["v_" + name][0]], [], [(cc, F32)] * 3, [],
                        name="adamw_" + name, tr=128, sub=8)
        delta[name], new_m[name], new_v[name] = [o[None] for o in outs]

    total = _sum8(plan.small_buf, name="sum_small")
    tot = dict(zip([n for n, _ in SMALL_EARLY], _unpack(total, [s for _, s in SMALL_EARLY])))
    late = small["pre_mix_norm"].reshape(-1, 128)
    late = _allgather8(late, name="gather_late").reshape(8, late.shape[0], 128)
    tot["pre_mix_norm"] = _sum8(late, name="sum_late").reshape(small["pre_mix_norm"].shape)
    loss = tot["loss"][0, 0]
    n_sc, n_lc = ssd_conv_w.shape[2], lru_conv_w.shape[2]
    grads = dict(
        pre_mix_norm=tot["pre_mix_norm"],
        ssd_conv_w=lax.dynamic_slice(tot["ssd_conv"][:CONV_W], (0, j * n_sc), (CONV_W, n_sc))[None],
        ssd_conv_b=tot["ssd_conv"][CONV_W:CONV_W + 1],
        ssd_dt_bias=tot["ssd_par"][0:1, :DT_W], ssd_a_log=tot["ssd_par"][1:2, :DT_W], ssd_d=tot["ssd_par"][2:3, :DT_W],
        ssd_norm=tot["ssd_norm"],
        lru_conv_w=lax.dynamic_slice(tot["lru_conv"][:CONV_W], (0, j * n_lc), (CONV_W, n_lc))[None],
        lru_conv_b=tot["lru_conv"][CONV_W:CONV_W + 1],
        lru_w_a=tot["lru_w_a"][None], lru_b_a=tot["lru_b_a"].reshape(lru_b_a.shape),
        lru_w_x=tot["lru_w_x"][None], lru_b_x=tot["lru_b_x"].reshape(lru_b_x.shape),
        lru_lambda=tot["lru_lambda"], lru_norm=tot["lru_norm"], post_mix_norm=tot["post_mix_norm"],
        pre_mlp_norm=tot["pre_mlp_norm"], post_mlp_norm=tot["post_mlp_norm"])

    grads.update(large_grads)

    small_w = [n for n in WEIGHTS if n not in LARGE]
    packs = [_pack([d[n] for n in small_w]) for d in (a, grads, {n: a["m_" + n] for n in small_w}, {n: a["v_" + n] for n in small_w})]
    outs = _rowwise(_adamw, packs, [], [(128, F32)] * 3, [], name="adamw_small", tr=packs[0].shape[0], sub=8)
    for d, o in zip((delta, new_m, new_v), outs):
        d.update(zip(small_w, _unpack(o, [a[n].shape for n in small_w])))

    return (loss, grad_x[None], *[grads[n] for n in WEIGHTS], *[delta[n] for n in WEIGHTS],
            *[new_m[n] for n in WEIGHTS], *[new_v[n] for n in WEIGHTS])
```

```python
import functools

import jax
import jax.numpy as jnp
from jax import lax
from jax.experimental import pallas as pl
from jax.experimental.pallas import tpu as pltpu

F32 = jnp.float32
BF16 = jnp.bfloat16
MESH = pl.DeviceIdType.MESH

EPS = 1e-6
LRU_C = 8.0
ADAM_LR = 0.001
ADAM_B1 = 0.9
ADAM_B2 = 0.999
ADAM_EPS = 1e-08
ADAM_WD = 0.01
ADAM_STEP = 10

N_GROUPS = 8
HEADS_PER_GROUP = 4
HEAD_DIM = 64
GROUP_W = HEADS_PER_GROUP * HEAD_DIM
STATE = 128
LRU_HEADS = 16
LRU_BLOCK = 128
CONV_W = 4
SSD_CHUNK = 256
HALO = 8

VMEM_LIMIT = 48 * 1024 * 1024


def _params(sem=None):
    return pltpu.CompilerParams(dimension_semantics=sem, vmem_limit_bytes=VMEM_LIMIT)


@jax.custom_jvp
def _log1p(x):
    u = 1.0 + x
    d = u - 1.0
    return jnp.where(d == 0.0, x, jnp.log(u) * (x / jnp.where(d == 0.0, 1.0, d)))


@_log1p.defjvp
def _log1p_jvp(primals, tangents):
    (x,), (t,) = primals, tangents
    return _log1p(x), t / (1.0 + x)


@jax.custom_jvp
def _expm1(x):
    u = jnp.exp(x)
    lu = jnp.log(u)
    safe = jnp.where(lu == 0.0, 1.0, lu)
    y = (u - 1.0) * (x / safe)
    y = jnp.where(lu == 0.0, x, y)
    return jnp.where(u == 0.0, -1.0, y)


@_expm1.defjvp
def _expm1_jvp(primals, tangents):
    (x,), (t,) = primals, tangents
    return _expm1(x), t * jnp.exp(x)


def _softplus(x):
    return jnp.maximum(x, 0.0) + _log1p(jnp.exp(-jnp.abs(x)))


def _sigmoid(x):
    return 1.0 / (1.0 + jnp.exp(-x))


def _silu(x):
    return x * _sigmoid(x)


def _gelu(x):
    c = 0.7978845608028654
    return 0.5 * x * (1.0 + jnp.tanh(c * (x + 0.044715 * (x * x * x))))


def _rms(x, g):
    return x * lax.rsqrt(jnp.mean(x * x, axis=-1, keepdims=True) + EPS) * g


def _dot(a, b, dims):
    return lax.dot_general(a.astype(BF16), b.astype(BF16), (dims, ((), ())), preferred_element_type=F32)


_NN = ((1,), (0,))
_NT = ((1,), (1,))
_TN = ((0,), (0,))


ANY = pl.BlockSpec(memory_space=pl.ANY)


class _Side:
    def __init__(self, ins, outs, aliases, n_sems, make):
        self.ins, self.outs, self.aliases, self.n_sems, self.make = ins, outs, aliases, n_sems, make


def _call(body, args, *, name, grid, in_specs, out_specs, out_shape, scratch_shapes=(), sem=None, side=None, into=None):
    in_specs, out_specs, out_shape, scratch_shapes = list(in_specs), list(out_specs), list(out_shape), list(scratch_shapes)
    held = []
    for oi, (buf, n_cols, off) in (into or {}).items():
        spec = out_specs[oi]
        out_shape[oi] = jax.ShapeDtypeStruct((out_shape[oi].shape[0], n_cols), out_shape[oi].dtype)
        out_specs[oi] = pl.BlockSpec(spec.block_shape, lambda *idx, _m=spec.index_map, _o=off: (_m(*idx)[0], _m(*idx)[1] + _o))
        if buf is not None:
            held.append((oi, buf))
    if side is None and not held:
        outs = pl.pallas_call(body, name=name, grid=grid, in_specs=in_specs, out_specs=out_specs, out_shape=out_shape,
                              scratch_shapes=scratch_shapes, compiler_params=_params(sem))(*args)
        return list(outs), []
    side_ins, side_outs = (list(side.ins), list(side.outs)) if side is not None else ([], [])
    n_in, n_out, n_scr, nh, si, so = len(in_specs), len(out_specs), len(scratch_shapes), len(held), len(side_ins), len(side_outs)

    def full(*refs):
        s_in = refs[n_in + nh:n_in + nh + si]
        o0 = n_in + nh + si
        s_out = refs[o0 + n_out:o0 + n_out + so]
        scr = refs[o0 + n_out + so:o0 + n_out + so + n_scr]
        if side is None:
            body(*refs[:n_in], *refs[o0:o0 + n_out], *scr)
            return
        send_sems, recv_sems = refs[-2], refs[-1]
        ids = [pl.program_id(d) for d in range(len(grid))]
        first = functools.reduce(jnp.logical_and, [i == 0 for i in ids])
        last = functools.reduce(jnp.logical_and, [i == g - 1 for i, g in zip(ids, grid)])

        @pl.when(first)
        def _():
            for d in side.make(s_in, s_out, send_sems, recv_sems)[0]:
                d.start()

        body(*refs[:n_in], *refs[o0:o0 + n_out], *scr)

        @pl.when(last)
        def _():
            sends, recvs = side.make(s_in, s_out, send_sems, recv_sems)
            for d in recvs:
                d.wait_recv()
            for d in sends:
                d.wait_send()

    aliases = {n_in + h: oi for h, (oi, _) in enumerate(held)}
    if side is not None:
        aliases.update({n_in + nh + i: n_out + o for i, o in side.aliases.items()})
        scratch_shapes = scratch_shapes + [pltpu.SemaphoreType.DMA((side.n_sems,)), pltpu.SemaphoreType.DMA((side.n_sems,))]
    outs = pl.pallas_call(
        full, name=name, grid=grid, in_specs=in_specs + [ANY] * (nh + si), out_specs=out_specs + [ANY] * so,
        out_shape=out_shape + side_outs, scratch_shapes=scratch_shapes, input_output_aliases=aliases,
        compiler_params=_params(("arbitrary",) * len(grid) if side is not None else sem),
    )(*args, *[b for _, b in held], *side_ins)
    return list(outs[:n_out]), list(outs[n_out:])


def _matmul(a, b, *, mode, m, n, k, tm, tn, tk, out_dtypes, name, a_spec=None, b_spec=None,
            out_specs=None, out_shapes=None, extras=(), epilogue=None, side=None):
    tm, tn, tk = min(tm, m), min(tn, n), min(tk, k)
    assert m % tm == 0 and n % tn == 0 and k % tk == 0, (name, m, n, k, tm, tn, tk)
    nk = k // tk
    dims = {"nn": _NN, "nt": _NT, "tn": _TN}[mode]
    if a_spec is None:
        a_spec = pl.BlockSpec((tk, tm), lambda i, j, kk: (kk, i)) if mode == "tn" else pl.BlockSpec((tm, tk), lambda i, j, kk: (i, kk))
    if b_spec is None:
        b_spec = pl.BlockSpec((tn, tk), lambda i, j, kk: (j, kk)) if mode == "nt" else pl.BlockSpec((tk, tn), lambda i, j, kk: (kk, j))
    tile = pl.BlockSpec((tm, tn), lambda i, j, kk: (i, j))
    if out_specs is None:
        out_specs = [tile for _ in out_dtypes]
    if out_shapes is None:
        out_shapes = [jax.ShapeDtypeStruct((m, n), d) for d in out_dtypes]
    n_ex, n_out = len(extras), len(out_dtypes)

    def body(*refs):
        a_ref, b_ref = refs[0], refs[1]
        ex_refs = refs[2:2 + n_ex]
        o_refs = refs[2 + n_ex:2 + n_ex + n_out]
        def finish(r):
            outs = epilogue(r, *[e[...] for e in ex_refs]) if epilogue is not None else (r,)
            for o_ref, o in zip(o_refs, outs):
                o_ref[...] = o.astype(o_ref.dtype)

        if nk == 1:
            finish(_dot(a_ref[...], b_ref[...], dims))
            return
        acc = refs[-1]
        kk = pl.program_id(2)

        @pl.when(kk == 0)
        def _():
            acc[...] = _dot(a_ref[...], b_ref[...], dims)

        @pl.when(kk > 0)
        def _():
            acc[...] += _dot(a_ref[...], b_ref[...], dims)

        @pl.when(kk == nk - 1)
        def _():
            finish(acc[...])

    outs, side_outs = _call(
        body, (a, b, *extras), name=name, grid=(m // tm, n // tn, nk),
        in_specs=[a_spec, b_spec] + [tile for _ in extras], out_specs=out_specs, out_shape=out_shapes,
        scratch_shapes=[] if nk == 1 else [pltpu.VMEM((tm, tn), F32)], sem=("parallel", "parallel", "arbitrary"), side=side)
    return outs if side is None else (outs, side_outs)


def _rowwise(fn, rows, bcast, out_rows, out_acc, *, name, tr, sub, side=None, into=None):
    rows = [r if isinstance(r, tuple) else (r, 0, r.shape[1]) for r in rows]
    row_specs = []
    for arr, c0, w in rows:
        assert c0 % w == 0, (name, c0, w)
        row_specs.append((w, c0 // w))
    rows = [r[0] for r in rows]
    L = rows[0].shape[0]
    tr = min(tr, L)
    sub = min(sub, tr)
    assert L % tr == 0 and tr % sub == 0, (name, L, tr, sub)
    n_r, n_b, n_or, n_oa = len(rows), len(bcast), len(out_rows), len(out_acc)

    def body(*refs):
        r_refs = refs[:n_r]
        b_refs = refs[n_r:n_r + n_b]
        or_refs = refs[n_r + n_b:n_r + n_b + n_or]
        oa_refs = refs[n_r + n_b + n_or:]
        i = pl.program_id(0)

        @pl.when(i == 0)
        def _():
            for o in oa_refs:
                o[...] = jnp.zeros_like(o)

        bvals = [b[...] for b in b_refs]

        def step(s, carry):
            r0 = pl.multiple_of(s * sub, sub)
            tiles = [r[pl.ds(r0, sub), :] for r in r_refs]
            outs = fn(*tiles, *bvals)
            for o_ref, o in zip(or_refs, outs[:n_or]):
                o_ref[pl.ds(r0, sub), :] = o.astype(o_ref.dtype)
            for o_ref, o in zip(oa_refs, outs[n_or:]):
                o_ref[...] += o
            return carry

        if tr == sub:
            step(0, 0)
        else:
            lax.fori_loop(0, tr // sub, step, 0)

    def whole(shape):
        nd = len(shape)
        return pl.BlockSpec(shape, lambda i, _n=nd: (0,) * _n)

    outs, side_outs = _call(
        body, (*rows, *bcast), name=name, grid=(L // tr,),
        in_specs=[pl.BlockSpec((tr, w), lambda i, _c=cb: (i, _c)) for w, cb in row_specs] + [whole(b.shape) for b in bcast],
        out_specs=[pl.BlockSpec((tr, c), lambda i: (i, 0)) for c, _ in out_rows] + [whole(s) for s in out_acc],
        out_shape=[jax.ShapeDtypeStruct((L, c), d) for c, d in out_rows] + [jax.ShapeDtypeStruct(s, F32) for s in out_acc],
        sem=("arbitrary",), side=side, into=into)
    return outs if side is None else (outs, side_outs)


def _colsum(x):
    return jnp.sum(x, axis=0, keepdims=True)


def _grouped(fn, rows, params, out_rows, out_acc, *, gw, name, tr, side=None, into=None):
    rows = [r if isinstance(r, tuple) else (r, 0) for r in rows]
    L = rows[0][0].shape[0]
    tr = min(tr, L)
    assert L % tr == 0
    G = None
    for p in params:
        G = p.shape[0] if p.ndim == 3 else p.shape[1] // gw
    cw = G * gw
    n_r, n_p, n_or, n_oa = len(rows), len(params), len(out_rows), len(out_acc)

    def pick(ref, g):
        return ref[g] if len(ref.shape) == 3 else ref[:, g * gw:(g + 1) * gw]

    def body(*refs):
        r_refs = refs[:n_r]
        p_refs = refs[n_r:n_r + n_p]
        or_refs = refs[n_r + n_p:n_r + n_p + n_or]
        oa_refs = refs[n_r + n_p + n_or:]

        @pl.when(pl.program_id(0) == 0)
        def _():
            for o in oa_refs:
                o[...] = jnp.zeros_like(o)

        for g in range(G):
            outs = fn(*[pick(r, g) for r in r_refs], *[pick(p, g) for p in p_refs])
            for o_ref, o in zip(or_refs, outs[:n_or]):
                o_ref[:, g * gw:(g + 1) * gw] = o.astype(o_ref.dtype)
            for o_ref, o in zip(oa_refs, outs[n_or:]):
                if len(o_ref.shape) == 3:
                    o_ref[g] += o
                else:
                    o_ref[:, g * gw:(g + 1) * gw] += o

    def whole(shape):
        nd = len(shape)
        return pl.BlockSpec(shape, lambda i, _n=nd: (0,) * _n)

    for _, c0 in rows:
        assert c0 % cw == 0
    outs, side_outs = _call(
        body, (*[r[0] for r in rows], *params), name=name, grid=(L // tr,),
        in_specs=[pl.BlockSpec((tr, cw), lambda i, _c=c0 // cw: (i, _c)) for _, c0 in rows] + [whole(p.shape) for p in params],
        out_specs=[pl.BlockSpec((tr, cw), lambda i: (i, 0)) for _ in out_rows] + [whole(s) for s in out_acc],
        out_shape=[jax.ShapeDtypeStruct((L, cw), d) for d in out_rows] + [jax.ShapeDtypeStruct(s, F32) for s in out_acc],
        sem=("arbitrary",), side=side, into=into)
    return outs if side is None else (outs, side_outs)


def _dsilu(p):
    s = _sigmoid(p)
    return s + p * s * (1.0 - s)


CONV_RC, CONV_CC = 32, 512


def _past_window(x_ref, head, r0, k, cs):
    if r0 == 0:
        return head[HALO - 3 + k:HALO - 3 + k + CONV_RC, cs]
    return x_ref[r0 - 3 + k:r0 - 3 + k + CONV_RC, cs]


def _conv_fwd(x, c0, C, w, b, *, silu, name, tr=512, cb=1024, side=None):
    L = x.shape[0]
    tr = min(tr, L)
    nb, hb = L // tr, tr // HALO
    assert L % tr == 0 and C % cb == 0 and c0 % cb == 0 and tr % CONV_RC == 0 and cb % CONV_CC == 0
    n_out = 2 if silu else 1

    def body(x_ref, h_ref, w_ref, b_ref, *rest):
        o_refs, head = rest[:n_out], rest[n_out]
        head[0:HALO, :] = h_ref[...]

        @pl.when(pl.program_id(0) == 0)
        def _():
            head[0:HALO, :] = jnp.zeros((HALO, cb), F32)

        head[HALO:, :] = x_ref[0:CONV_RC, :]
        for cc in range(cb // CONV_CC):
            cs = slice(cc * CONV_CC, (cc + 1) * CONV_CC)
            wv, bv = w_ref[:, cs], b_ref[:, cs]
            for r0 in range(0, tr, CONV_RC):
                y = bv
                for k in range(CONV_W):
                    y = y + wv[k:k + 1, :] * _past_window(x_ref, head, r0, k, cs)
                o_refs[0][r0:r0 + CONV_RC, cs] = y
                if silu:
                    o_refs[1][r0:r0 + CONV_RC, cs] = _silu(y)

    outs, side_outs = _call(
        body, (x, x, w, b), name=name, grid=(nb, C // cb),
        in_specs=[
            pl.BlockSpec((tr, cb), lambda i, j: (i, c0 // cb + j)),
            pl.BlockSpec((HALO, cb), lambda i, j: (jnp.maximum(i * hb - 1, 0), c0 // cb + j)),
            pl.BlockSpec((CONV_W, cb), lambda i, j: (0, j)),
            pl.BlockSpec((1, cb), lambda i, j: (0, j)),
        ],
        out_specs=[pl.BlockSpec((tr, cb), lambda i, j: (i, j)) for _ in range(n_out)],
        out_shape=[jax.ShapeDtypeStruct((L, C), F32) for _ in range(n_out)],
        scratch_shapes=[pltpu.VMEM((HALO + CONV_RC, cb), F32)], sem=("parallel", "parallel"), side=side)
    return outs if side is None else (outs, side_outs)


def _fold8(v):
    return (v[0:8] + v[8:16]) + (v[16:24] + v[24:32])


def _conv_bwd(dact, dc0, pre, x, xc0, C, w, *, silu, name, tr=512, cb=1024, side=None, into=None):
    L = x.shape[0]
    tr = min(tr, L)
    nb, hb = L // tr, tr // HALO
    last_h = L // HALO - 1
    assert L % tr == 0 and C % cb == 0 and tr % CONV_RC == 0 and cb % CONV_CC == 0

    def body(*refs):
        if silu:
            d_ref, dh_ref, p_ref, ph_ref, x_ref, xh_ref, w_ref, dx_ref, dwb_ref, dp, head = refs
        else:
            d_ref, dh_ref, x_ref, xh_ref, w_ref, dx_ref, dwb_ref, dp, head = refs
        i = pl.program_id(1)
        dp[tr:, :] = dh_ref[...] * _dsilu(ph_ref[...]) if silu else dh_ref[...]

        @pl.when(i == nb - 1)
        def _():
            dp[tr:, :] = jnp.zeros((HALO, cb), F32)

        head[0:HALO, :] = xh_ref[...]

        @pl.when(i == 0)
        def _():
            head[0:HALO, :] = jnp.zeros((HALO, cb), F32)
            dwb_ref[...] = jnp.zeros_like(dwb_ref)

        head[HALO:, :] = x_ref[0:CONV_RC, :]
        for cc in range(cb // CONV_CC):
            cs = slice(cc * CONV_CC, (cc + 1) * CONV_CC)
            for r0 in range(0, tr, CONV_RC):
                rs = slice(r0, r0 + CONV_RC)
                dp[rs, cs] = d_ref[rs, cs] * _dsilu(p_ref[rs, cs]) if silu else d_ref[rs, cs]
        for cc in range(cb // CONV_CC):
            cs = slice(cc * CONV_CC, (cc + 1) * CONV_CC)
            wv = w_ref[:, cs]
            acc = [jnp.zeros((8, CONV_CC), F32) for _ in range(CONV_W + 1)]
            for r0 in range(0, tr, CONV_RC):
                dm = dp[r0:r0 + CONV_RC, cs]
                dx = jnp.zeros((CONV_RC, CONV_CC), F32)
                for k in range(CONV_W):
                    dx = dx + wv[k:k + 1, :] * dp[r0 + 3 - k:r0 + 3 - k + CONV_RC, cs]
                    acc[k] = acc[k] + _fold8(dm * _past_window(x_ref, head, r0, k, cs))
                acc[CONV_W] = acc[CONV_W] + _fold8(dm)
                dx_ref[r0:r0 + CONV_RC, cs] = dx.astype(dx_ref.dtype)
            for k in range(CONV_W + 1):
                dwb_ref[k:k + 1, cs] += _colsum(acc[k])

    def main(c):
        return pl.BlockSpec((tr, cb), lambda j, i: (i, c // cb + j))

    def nxt(c):
        return pl.BlockSpec((HALO, cb), lambda j, i: (jnp.minimum((i + 1) * hb, last_h), c // cb + j))

    in_specs = [main(dc0), nxt(dc0)]
    args = [dact, dact]
    if silu:
        in_specs += [main(0), nxt(0)]
        args += [pre, pre]
    in_specs += [main(xc0), pl.BlockSpec((HALO, cb), lambda j, i: (jnp.maximum(i * hb - 1, 0), xc0 // cb + j)),
                 pl.BlockSpec((CONV_W, cb), lambda j, i: (0, j))]
    args += [x, x, w]
    outs, side_outs = _call(
        body, args, name=name, grid=(C // cb, nb), in_specs=in_specs,
        out_specs=[pl.BlockSpec((tr, cb), lambda j, i: (i, j)), pl.BlockSpec((8, cb), lambda j, i: (0, j))],
        out_shape=[jax.ShapeDtypeStruct((L, C), BF16), jax.ShapeDtypeStruct((8, C), F32)],
        scratch_shapes=[pltpu.VMEM((tr + HALO, cb), F32), pltpu.VMEM((HALO + CONV_RC, cb), F32)],
        sem=("parallel", "arbitrary"), side=side, into=into)
    return outs if side is None else (outs, side_outs)


def _ssd_f1(dtraw, dtb, alog):
    q = dtraw.shape[0]
    dt = _softplus(dtraw + dtb)
    adt = dt * (-jnp.exp(alog))
    tril = (lax.broadcasted_iota(jnp.int32, (q, q), 0) >= lax.broadcasted_iota(jnp.int32, (q, q), 1)).astype(F32)
    acs = lax.dot_general(tril, adt, (_NN, ((), ())), precision=lax.Precision.HIGHEST, preferred_element_type=F32)
    return dt, acs


def _ssd_group(g, x, bm, cm, dt, acs, drow, hp):
    q = x.shape[0]
    lane = lax.broadcasted_iota(jnp.int32, (1, 128), 1)
    sub = lax.broadcasted_iota(jnp.int32, (128, 1), 0)
    head_of = lax.broadcasted_iota(jnp.int32, (1, GROUP_W), 1) // HEAD_DIM
    is_last = (lax.broadcasted_iota(jnp.int32, (q, 1), 0) == q - 1).astype(F32)
    causal = lax.broadcasted_iota(jnp.int32, (q, q), 0) >= lax.broadcasted_iota(jnp.int32, (q, q), 1)
    acs_end = jnp.sum(acs * is_last, axis=0, keepdims=True)
    acs_t = acs.T
    dt_exp = jnp.zeros((q, GROUP_W), F32)
    acs_exp = jnp.zeros((q, GROUP_W), F32)
    end_exp = jnp.zeros((1, GROUP_W), F32)
    d_exp = jnp.zeros((1, GROUP_W), F32)
    heads = []
    for k in range(HEADS_PER_GROUP):
        h = HEADS_PER_GROUP * g + k
        oh = (lane == h).astype(F32)
        mk = (head_of == k).astype(F32)
        acs_col = jnp.sum(acs * oh, axis=1, keepdims=True)
        acs_row = jnp.sum(acs_t * (sub == h).astype(F32), axis=0, keepdims=True)
        dt_exp = dt_exp + jnp.sum(dt * oh, axis=1, keepdims=True) * mk
        acs_exp = acs_exp + acs_col * mk
        end_exp = end_exp + jnp.sum(acs_end * oh, axis=1, keepdims=True) * mk
        d_exp = d_exp + jnp.sum(drow * oh, axis=1, keepdims=True) * mk
        heads.append((acs_col, acs_row, mk))
    xdt = x * dt_exp
    states = _dot(bm, xdt * jnp.exp(end_exp - acs_exp), _TN)
    y = _dot(cm, hp, _NN) * jnp.exp(acs_exp) + x * d_exp
    scores = _dot(cm, bm, _NT)
    for acs_col, acs_row, mk in heads:
        dec = jnp.exp(jnp.where(causal, acs_col - acs_row, -jnp.inf))
        y = y + _dot(scores * dec, xdt * mk, _NN)
    return y, hp * jnp.exp(end_exp) + states


def _ssd_fwd(act, dtraw, dtb, alog, drow, *, name, side=None):
    L = act.shape[0]
    q = min(SSD_CHUNK, L)
    nc = L // q
    d_ssd = N_GROUPS * GROUP_W

    def body(act_ref, dt_ref, dtb_ref, alog_ref, drow_ref, y_ref, hst_ref, h):
        @pl.when(pl.program_id(0) == 0)
        def _():
            h[...] = jnp.zeros_like(h)

        dt, acs = _ssd_f1(dt_ref[...], dtb_ref[...], alog_ref[...])
        drow = drow_ref[...]
        for g in range(N_GROUPS):
            hp = h[g]
            hst_ref[0, g] = hp
            y, hn = _ssd_group(g, act_ref[:, g * GROUP_W:(g + 1) * GROUP_W],
                               act_ref[:, d_ssd + g * STATE:d_ssd + (g + 1) * STATE],
                               act_ref[:, d_ssd + (N_GROUPS + g) * STATE:d_ssd + (N_GROUPS + g + 1) * STATE],
                               dt, acs, drow, hp)
            y_ref[:, g * GROUP_W:(g + 1) * GROUP_W] = y
            h[g] = hn

    row = pl.BlockSpec((1, 128), lambda i: (0, 0))
    outs, side_outs = _call(
        body, (act, dtraw, dtb, alog, drow), name=name, grid=(nc,),
        in_specs=[pl.BlockSpec((q, act.shape[1]), lambda i: (i, 0)), pl.BlockSpec((q, 128), lambda i: (i, 0)), row, row, row],
        out_specs=[pl.BlockSpec((q, d_ssd), lambda i: (i, 0)),
                   pl.BlockSpec((1, N_GROUPS, STATE, GROUP_W), lambda i: (i, 0, 0, 0))],
        out_shape=[jax.ShapeDtypeStruct((L, d_ssd), F32), jax.ShapeDtypeStruct((nc, N_GROUPS, STATE, GROUP_W), F32)],
        scratch_shapes=[pltpu.VMEM((N_GROUPS, STATE, GROUP_W), F32)], sem=("arbitrary",), side=side)
    return outs if side is None else (outs, side_outs)


def _ssd_bwd(act, dtraw, dy, hst, dtb, alog, drow, *, name, side=None):
    L = act.shape[0]
    q = min(SSD_CHUNK, L)
    nc = L // q
    d_ssd = N_GROUPS * GROUP_W

    def body(act_ref, dt_ref, dy_ref, hst_ref, dtb_ref, alog_ref, drow_ref, dact_ref, ddt_ref, dpar_ref, dh):
        @pl.when(pl.program_id(0) == 0)
        def _():
            dh[...] = jnp.zeros_like(dh)
            dpar_ref[...] = jnp.zeros_like(dpar_ref)

        (dt, acs), f1_vjp = jax.vjp(_ssd_f1, dt_ref[...], dtb_ref[...], alog_ref[...])
        drow = drow_ref[...]
        ddt = jnp.zeros_like(dt)
        dacs = jnp.zeros_like(acs)
        ddrow = jnp.zeros_like(drow)
        for g in range(N_GROUPS):
            xs = slice(g * GROUP_W, (g + 1) * GROUP_W)
            bs = slice(d_ssd + g * STATE, d_ssd + (g + 1) * STATE)
            cs = slice(d_ssd + (N_GROUPS + g) * STATE, d_ssd + (N_GROUPS + g + 1) * STATE)
            _, f2_vjp = jax.vjp(functools.partial(_ssd_group, g), act_ref[:, xs], act_ref[:, bs], act_ref[:, cs],
                                dt, acs, drow, hst_ref[0, g])
            dx, dbm, dcm, ddt_g, dacs_g, ddrow_g, dhp = f2_vjp((dy_ref[:, xs], dh[g]))
            dact_ref[:, xs] = dx
            dact_ref[:, bs] = dbm
            dact_ref[:, cs] = dcm
            dh[g] = dhp
            ddt, dacs, ddrow = ddt + ddt_g, dacs + dacs_g, ddrow + ddrow_g
        ddtraw, ddtb, dalog = f1_vjp((ddt, dacs))
        ddt_ref[...] = ddtraw
        dpar_ref[0:1, :] += ddtb
        dpar_ref[1:2, :] += dalog
        dpar_ref[2:3, :] += ddrow

    row = pl.BlockSpec((1, 128), lambda i: (0, 0))
    rev = lambda i: (nc - 1 - i, 0)
    outs, side_outs = _call(
        body, (act, dtraw, dy, hst, dtb, alog, drow), name=name, grid=(nc,),
        in_specs=[pl.BlockSpec((q, act.shape[1]), rev), pl.BlockSpec((q, 128), rev), pl.BlockSpec((q, d_ssd), rev),
                  pl.BlockSpec((1, N_GROUPS, STATE, GROUP_W), lambda i: (nc - 1 - i, 0, 0, 0)), row, row, row],
        out_specs=[pl.BlockSpec((q, act.shape[1]), rev), pl.BlockSpec((q, 128), rev), pl.BlockSpec((8, 128), lambda i: (0, 0))],
        out_shape=[jax.ShapeDtypeStruct(act.shape, F32), jax.ShapeDtypeStruct((L, 128), F32), jax.ShapeDtypeStruct((8, 128), F32)],
        scratch_shapes=[pltpu.VMEM((N_GROUPS, STATE, GROUP_W), F32)], sem=("arbitrary",), side=side)
    return outs if side is None else (outs, side_outs)


def _gate_head(xl, wa, ba, wx, bx, lam):
    r = _sigmoid(_dot(xl, wa, _NN) + ba)
    i = _sigmoid(_dot(xl, wx, _NN) + bx)
    log_a = -LRU_C * r * _softplus(-lam)
    return jnp.exp(log_a), jnp.sqrt(-_expm1(2.0 * log_a)) * (i * xl)


def _gate_head_bwd(xl, da, du, wa, ba, wx, bx, lam):
    _, vjp = jax.vjp(_gate_head, xl, wa, ba, wx, bx, lam)
    dxl, dwa, dba, dwx, dbx, dlam = vjp((da, du))
    return dxl, dwa, dba, dwx, dbx, dlam


def _scan_tile(a, b, rows, reverse):
    for d in (1, 2, 4):
        if reverse:
            keep = rows < 8 - d
            a_sh, b_sh = pltpu.roll(a, 8 - d, 0), pltpu.roll(b, 8 - d, 0)
        else:
            keep = rows >= d
            a_sh, b_sh = pltpu.roll(a, d, 0), pltpu.roll(b, d, 0)
        b = b + a * jnp.where(keep, b_sh, 0.0)
        a = a * jnp.where(keep, a_sh, 1.0)
    return a, b


def _lru_scan_fwd(a, u, *, name, tr=512, cb=1024):
    L, C = a.shape
    tr, cb = min(tr, L), min(cb, C)

    def body(a_ref, u_ref, h_ref, hp_ref, carry):
        @pl.when(pl.program_id(1) == 0)
        def _():
            carry[...] = jnp.zeros_like(carry)

        rows = lax.broadcasted_iota(jnp.int32, (8, cb), 0)

        def tile(t, hc):
            r0 = pl.multiple_of(t * 8, 8)
            pa, hb = _scan_tile(a_ref[pl.ds(r0, 8), :], u_ref[pl.ds(r0, 8), :], rows, False)
            h = hb + pa * hc
            h_ref[pl.ds(r0, 8), :] = h
            hp_ref[pl.ds(r0, 8), :] = jnp.where(rows >= 1, pltpu.roll(h, 1, 0), hc)
            return h[7:8, :]

        carry[...] = lax.fori_loop(0, tr // 8, tile, carry[...])

    blk = pl.BlockSpec((tr, cb), lambda j, i: (i, j))
    return pl.pallas_call(
        body, name=name, grid=(C // cb, L // tr),
        in_specs=[blk, blk], out_specs=[blk, blk],
        out_shape=[jax.ShapeDtypeStruct((L, C), F32), jax.ShapeDtypeStruct((L, C), F32)],
        scratch_shapes=[pltpu.VMEM((1, cb), F32)],
        compiler_params=_params(("parallel", "arbitrary")),
    )(a, u)


def _lru_scan_bwd(a, hprev, dh, *, name, tr=512, cb=1024):
    L, C = a.shape
    tr, cb = min(tr, L), min(cb, C)
    nb = L // tr

    def body(a_ref, hp_ref, dh_ref, da_ref, du_ref, carry):
        @pl.when(pl.program_id(1) == 0)
        def _():
            carry[...] = jnp.zeros_like(carry)

        rows = lax.broadcasted_iota(jnp.int32, (8, cb), 0)

        def tile(t, gc):
            r0 = pl.multiple_of((tr // 8 - 1 - t) * 8, 8)
            av, dv = a_ref[pl.ds(r0, 8), :], dh_ref[pl.ds(r0, 8), :]
            pa, gb = _scan_tile(av, av * dv, rows, True)
            big = gb + pa * gc
            g = dv + jnp.where(rows < 7, pltpu.roll(big, 7, 0), gc)
            du_ref[pl.ds(r0, 8), :] = g
            da_ref[pl.ds(r0, 8), :] = g * hp_ref[pl.ds(r0, 8), :]
            return big[0:1, :]

        carry[...] = lax.fori_loop(0, tr // 8, tile, carry[...])

    blk = pl.BlockSpec((tr, cb), lambda j, i: (nb - 1 - i, j))
    return pl.pallas_call(
        body, name=name, grid=(C // cb, nb),
        in_specs=[blk, blk, blk], out_specs=[blk, blk],
        out_shape=[jax.ShapeDtypeStruct((L, C), F32), jax.ShapeDtypeStruct((L, C), F32)],
        scratch_shapes=[pltpu.VMEM((1, cb), F32)],
        compiler_params=_params(("parallel", "arbitrary")),
    )(a, hprev, dh)


def _ssd_gate(y, z, n):
    v = y * _silu(z)
    return v * lax.rsqrt(jnp.mean(v * v, axis=-1, keepdims=True) + EPS) * n


def _ssd_gate_bwd(y, z, dy, n):
    _, vjp = jax.vjp(_ssd_gate, y, z, n)
    return vjp(dy)


def _lru_out(hl, gate, n):
    return _rms(hl * _gelu(gate), n)


def _lru_out_bwd(hl, gate, dy, n):
    _, vjp = jax.vjp(_lru_out, hl, gate, n)
    return vjp(dy)


def _mid(x, mix, pm, pmlp):
    x1 = x + _rms(mix, pm)
    return x1, _rms(x1, pmlp)


def _mid_bwd(x, mix, dx1p, dh2, pm, pmlp):
    _, vjp = jax.vjp(_mid, x, mix, pm, pmlp)
    dx, dmix, dpm, dpmlp = vjp((dx1p, dh2))
    return dmix, dx, dpm, dpmlp


def _loss_bwd(hm2, x1, tgt, g):
    def lossf(hm2, x1, g):
        e = x1 + _rms(hm2, g) - tgt
        return 0.5 * jnp.sum(jnp.mean(e * e, axis=-1, keepdims=True), axis=0, keepdims=True)

    val, vjp = jax.vjp(lossf, hm2, x1, g)
    dhm2, dx1, dg = vjp(jnp.ones((1, 1), F32))
    return dhm2, dx1, dg, val * jnp.ones((1, 128), F32)


def _in_bwd(x, dh_a, dh_b, dx1, g):
    _, vjp = jax.vjp(_rms, x, g)
    dx, dg = vjp(dh_a + dh_b)
    return dx + dx1, dg


def _adamw(w, g, m, v):
    m = ADAM_B1 * m + (1.0 - ADAM_B1) * g
    v = ADAM_B2 * v + (1.0 - ADAM_B2) * (g * g)
    m_hat = m / (1.0 - ADAM_B1 ** ADAM_STEP)
    v_hat = v / (1.0 - ADAM_B2 ** ADAM_STEP)
    return -ADAM_LR * (m_hat / (jnp.sqrt(v_hat) + ADAM_EPS) + ADAM_WD * w), m, v


class _LocalPlan:
    def __init__(self, p):
        self.p, self.large = p, {}

    def weight(self, name):
        return self.p[name]

    def side(self, kernel_name):
        return None

    def done(self, kernel_name, side_outs):
        pass

    def grad(self, name, g):
        self.large[name] = g

    def small(self, grads):
        pass


def _local_step(x, tgt, p, plan):
    L, D = x.shape

    def carry(fn, *args, name, **kw):
        side = plan.side(name)
        if side is None:
            return fn(*args, name=name, **kw)
        outs, side_outs = fn(*args, name=name, side=side, **kw)
        plan.done(name, side_outs)
        return outs

    d_ssd, d_xbc, d_lru, d_mix, d_ff = 2048, 4096, 2048, 4096, 8192
    n_main = d_ssd + d_xbc + 2 * d_lru
    c_xbc, c_gate, c_xl = d_ssd, d_ssd + d_xbc, d_ssd + d_xbc + d_lru
    TR, SUB = 256, 32
    mm = dict(tm=1024, tn=1024, tk=2048)

    (h,) = _rowwise(lambda xt, g: (_rms(xt, g),), [x], [p["pre_mix_norm"]], [(D, BF16)], [], name="f_prenorm", tr=TR, sub=SUB)
    w_main, w_dt = plan.weight("w_main"), plan.weight("w_dt")
    (proj,) = carry(_matmul, h, w_main, mode="nn", m=L, n=n_main, k=D, out_dtypes=[F32], name="f_inproj", **mm)
    (dtraw,) = _matmul(h, w_dt, mode="nn", m=L, n=128, k=D, out_dtypes=[F32], name="f_dtproj", **mm)
    pre, act = carry(_conv_fwd, proj, c_xbc, d_xbc, p["ssd_conv_w"], p["ssd_conv_b"], silu=True, name="f_ssdconv")
    (xl,) = carry(_conv_fwd, proj, c_xl, d_lru, p["lru_conv_w"], p["lru_conv_b"], silu=False, name="f_lruconv")
    yraw, hst = carry(_ssd_fwd, act, dtraw, p["dtb"], p["alog"], p["drow"], name="f_ssd")
    (ycat,) = _grouped(lambda y, z, n: (_ssd_gate(y, z, n),), [yraw, (proj, 0)], [p["ssd_norm"]], [BF16], [],
                       gw=GROUP_W, name="f_ssdgate", tr=TR, into={0: (None, d_mix, 0)})
    gate_p = [p["lru_w_a"], p["lru_b_a"], p["lru_w_x"], p["lru_b_x"], p["lru_lambda"]]
    a, u = carry(_grouped, _gate_head, [xl], gate_p, [F32, F32], [], gw=LRU_BLOCK, name="f_lrugates", tr=TR)
    hl, hprev = _lru_scan_fwd(a, u, name="f_lruscan")
    (ycat,) = _rowwise(lambda ht, gt, n: (_lru_out(ht, gt, n),), [hl, (proj, c_gate, d_lru)], [p["lru_norm"]],
                       [(d_lru, BF16)], [], name="f_lruout", tr=TR, sub=SUB, into={0: (ycat, d_mix, d_ssd // d_lru)})
    w_out = plan.weight("w_out")
    (mix,) = carry(_matmul, ycat, w_out, mode="nn", m=L, n=D, k=d_mix, out_dtypes=[F32], name="f_outproj", **mm)
    x1, h2 = _rowwise(_mid, [x, mix], [p["post_mix_norm"], p["pre_mlp_norm"]], [(D, F32), (D, BF16)], [],
                      name="f_mid", tr=TR, sub=SUB)
    nb_mi = (d_ff // 4) // mm["tn"]
    w_mi = plan.weight("w_mi")
    hm, act2 = carry(_matmul, h2, w_mi, mode="nn", m=L, n=d_ff, k=D, out_dtypes=[BF16, BF16], name="f_mlpin",
                     b_spec=pl.BlockSpec((None, mm["tk"], mm["tn"]), lambda i, j, kk: (j // nb_mi, kk, j % nb_mi)),
                     epilogue=lambda r: (r, jnp.square(jnp.maximum(r, 0.0))), **mm)
    w_mo = plan.weight("w_mo")
    (hm2,) = _matmul(act2, w_mo, mode="nn", m=L, n=D, k=d_ff, out_dtypes=[F32], name="f_mlpout", **mm)

    dhm2, dx1p, d_post_mlp, loss = _rowwise(_loss_bwd, [hm2, x1, tgt], [p["post_mlp_norm"]], [(D, BF16), (D, F32)],
                                            [(1, D), (1, 128)], name="b_loss", tr=TR, sub=SUB)
    (dhm,) = _matmul(dhm2, w_mo, mode="nt", m=L, n=d_ff, k=D, out_dtypes=[BF16], name="b_mlpout_dx", extras=[hm],
                     epilogue=lambda r, hmv: (r * (2.0 * jnp.maximum(hmv.astype(F32), 0.0)),), **mm)
    (dw_mo,) = _matmul(act2, dhm2, mode="tn", m=d_ff, n=D, k=L, out_dtypes=[BF16], name="b_mlpout_dw", **mm)
    plan.grad("w_mlp_out", dw_mo.reshape(4, -1, D))
    kb_mi = (d_ff // 4) // mm["tk"]
    (dh2,) = carry(_matmul, dhm, w_mi, mode="nt", m=L, n=D, k=d_ff, out_dtypes=[F32], name="b_mlpin_dx",
                   b_spec=pl.BlockSpec((None, mm["tn"], mm["tk"]), lambda i, j, kk: (kk // kb_mi, j, kk % kb_mi)), **mm)
    (dw_mi,) = carry(_matmul, h2, dhm, mode="tn", m=D, n=d_ff, k=L, out_dtypes=[BF16], name="b_mlpin_dw",
                     out_specs=[pl.BlockSpec((None, mm["tm"], mm["tn"]), lambda i, j, kk: (j // nb_mi, i, j % nb_mi))],
                     out_shapes=[jax.ShapeDtypeStruct((4, D, d_ff // 4), BF16)], **mm)
    plan.grad("w_mlp_in", dw_mi)
    dmix, dx1, d_post_mix, d_pre_mlp = carry(_rowwise, _mid_bwd, [x, mix, dx1p, dh2], [p["post_mix_norm"], p["pre_mlp_norm"]],
                                             [(D, BF16), (D, F32)], [(1, D), (1, D)], name="b_mid", tr=TR, sub=SUB)
    (dw_out,) = carry(_matmul, ycat, dmix, mode="tn", m=d_mix, n=D, k=L, out_dtypes=[BF16], name="b_outproj_dw", **mm)
    plan.grad("w_out", dw_out.reshape(4, -1, D))
    (dycat,) = carry(_matmul, dmix, w_out, mode="nt", m=L, n=d_mix, k=D, out_dtypes=[F32], name="b_outproj_dx", **mm)
    dhl, dproj, d_lru_norm = _rowwise(_lru_out_bwd, [hl, (proj, c_gate, d_lru), (dycat, d_ssd, d_lru)], [p["lru_norm"]],
                                      [(d_lru, F32), (d_lru, BF16)], [(1, d_lru)], name="b_lruout", tr=TR, sub=SUB,
                                      into={1: (None, n_main, c_gate // d_lru)})
    da, du = _lru_scan_bwd(a, hprev, dhl, name="b_lruscan")
    dxl, d_wa, d_ba, d_wx, d_bx, d_lam = _grouped(
        _gate_head_bwd, [xl, da, du], gate_p, [F32],
        [(LRU_HEADS, LRU_BLOCK, LRU_BLOCK), (1, d_lru), (LRU_HEADS, LRU_BLOCK, LRU_BLOCK), (1, d_lru), (1, d_lru)],
        gw=LRU_BLOCK, name="b_lrugates", tr=TR)
    CB = 1024
    dproj, dwb_lru = _conv_bwd(dxl, 0, None, proj, c_xl, d_lru, p["lru_conv_w"], silu=False, name="b_lruconv", cb=CB,
                               into={0: (dproj, n_main, c_xl // CB)})
    dyraw, dproj, d_ssd_norm = _grouped(_ssd_gate_bwd, [yraw, (proj, 0), (dycat, 0)], [p["ssd_norm"]], [F32, BF16], [(1, d_ssd)],
                                        gw=GROUP_W, name="b_ssdgate", tr=TR, into={1: (dproj, n_main, 0)})
    dact, ddtraw, dpar = carry(_ssd_bwd, act, dtraw, dyraw, hst, p["dtb"], p["alog"], p["drow"], name="b_ssd")
    dproj, dwb_ssd = carry(_conv_bwd, dact, 0, pre, proj, c_xbc, d_xbc, p["ssd_conv_w"], silu=True, name="b_ssdconv", cb=CB,
                           into={0: (dproj, n_main, c_xbc // CB)})
    early = dict(loss=loss, ssd_conv=dwb_ssd, ssd_par=dpar, ssd_norm=d_ssd_norm, lru_conv=dwb_lru, lru_w_a=d_wa, lru_b_a=d_ba,
                 lru_w_x=d_wx, lru_b_x=d_bx, lru_lambda=d_lam, lru_norm=d_lru_norm, post_mix_norm=d_post_mix,
                 pre_mlp_norm=d_pre_mlp, post_mlp_norm=d_post_mlp)
    plan.small(early)
    (dw_main,) = carry(_matmul, h, dproj, mode="tn", m=D, n=n_main, k=L, out_dtypes=[BF16], name="b_inproj_dw", **mm)
    (dw_dt,) = _matmul(h, ddtraw, mode="tn", m=D, n=128, k=L, out_dtypes=[BF16], name="b_dtproj_dw", **mm)
    plan.grad("w_in", (dw_main, dw_dt))
    (dh_a,) = carry(_matmul, dproj, w_main, mode="nt", m=L, n=D, k=n_main, out_dtypes=[F32], name="b_inproj_dx", **mm)
    (dh_b,) = _matmul(ddtraw, w_dt, mode="nt", m=L, n=D, k=128, out_dtypes=[F32], name="b_dtproj_dx", **mm)
    grad_x, d_pre_mix = carry(_rowwise, _in_bwd, [x, dh_a, dh_b, dx1], [p["pre_mix_norm"]], [(D, F32)], [(1, D)],
                              name="b_prenorm", tr=TR, sub=SUB)

    return grad_x, dict(early, pre_mix_norm=d_pre_mix)


def _place():
    return lax.axis_index("x"), lax.axis_index("y"), lax.axis_index("c")


def _other_chips(x, y):
    return [(1 - x, y), (x, 1 - y), (1 - x, 1 - y)]


def _allgather8(blk, *, name):
    r, n = blk.shape

    def body(x_ref, out_ref, send_sems, recv_sems, local_sem):
        x, y, c = _place()
        me, sibling = (x, y, c), (x, y, 1 - c)
        chips = _other_chips(x, y)

        def rows(px, py, pc):
            return out_ref.at[pl.ds((4 * px + 2 * py + pc) * r, r), :]

        def copy(k, block, to, src=None):
            return pltpu.make_async_remote_copy(
                src_ref=rows(*block) if src is None else src, dst_ref=rows(*block),
                send_sem=send_sems.at[k], recv_sem=recv_sems.at[k], device_id=to, device_id_type=MESH)

        mine = pltpu.make_async_copy(x_ref, rows(*me), local_sem)
        mine.start()
        first = [copy(0, me, sibling, src=x_ref)]
        first += [copy(1 + k, me, (*chip, c), src=x_ref) for k, chip in enumerate(chips)]
        for cp in first:
            cp.start()
        passed = [copy(4 + k, (*chip, c), sibling) for k, chip in enumerate(chips)]
        for k, chip in enumerate(chips):
            copy(1 + k, (*chip, c), me).wait_recv()
            passed[k].start()
        copy(0, sibling, me).wait_recv()
        for k, chip in enumerate(chips):
            copy(4 + k, (*chip, 1 - c), me).wait_recv()
        for cp in first + passed:
            cp.wait_send()
        mine.wait()

    return pl.pallas_call(
        body, name=name,
        out_shape=jax.ShapeDtypeStruct((8 * r, n), blk.dtype),
        in_specs=[pl.BlockSpec(memory_space=pltpu.VMEM)], out_specs=pl.BlockSpec(memory_space=pltpu.VMEM),
        scratch_shapes=[pltpu.SemaphoreType.DMA((7,)), pltpu.SemaphoreType.DMA((7,)), pltpu.SemaphoreType.DMA],
        compiler_params=pltpu.CompilerParams(vmem_limit_bytes=VMEM_LIMIT),
    )(blk)


def _sum8(g, *, name):
    _, r, n = g.shape
    tr = max(t for t in range(8, min(r, 512) + 1, 8) if r % t == 0)

    def body(g_ref, o_ref):
        s = g_ref[0]
        for k in range(1, 8):
            s = s + g_ref[k]
        o_ref[...] = s

    return pl.pallas_call(
        body, name=name, grid=(r // tr,),
        in_specs=[pl.BlockSpec((8, tr, n), lambda i: (0, i, 0))], out_specs=pl.BlockSpec((tr, n), lambda i: (i, 0)),
        out_shape=jax.ShapeDtypeStruct((r, n), g.dtype), compiler_params=_params(("parallel",)),
    )(g)


def _blocks(fn, ins, outs, *, grid, name, prefetch=None, aliases=None):
    n_in = len(ins)

    def body(*refs):
        if prefetch is not None:
            refs = refs[1:]
        res = fn(*[r[...] for r in refs[:n_in]])
        for o_ref, o in zip(refs[n_in:], res):
            o_ref[...] = o.astype(o_ref.dtype)

    in_specs = [pl.BlockSpec(b, m) for _, b, m in ins]
    out_specs = [pl.BlockSpec(b, m) for _, b, m in outs]
    kw = dict(name=name, out_shape=[s for s, _, _ in outs], input_output_aliases=aliases or {},
              compiler_params=_params(("arbitrary",) * len(grid)))
    arrs = [a for a, _, _ in ins]
    if prefetch is None:
        return pl.pallas_call(body, grid=grid, in_specs=in_specs, out_specs=out_specs, **kw)(*arrs)
    spec = pltpu.PrefetchScalarGridSpec(num_scalar_prefetch=1, grid=grid, in_specs=in_specs, out_specs=out_specs)
    return pl.pallas_call(body, grid_spec=spec, **kw)(prefetch, *arrs)


def _gather_weights(slots, *, name):
    n = len(slots)

    def body(*refs):
        i_refs, g_refs = refs[:n], refs[n:2 * n]
        send_sems, recv_sems = refs[2 * n:]
        x, y, c = _place()
        j, sibling = 2 * x + y, (x, y, 1 - c)
        chips = _other_chips(x, y)

        def cp(w, k, src, dst, to):
            return pltpu.make_async_remote_copy(src_ref=src, dst_ref=dst, send_sem=send_sems.at[6 * w + k],
                                                recv_sem=recv_sems.at[6 * w + k], device_id=to, device_id_type=MESH)

        def part(ref, chip_idx, hc):
            rh = ref.shape[1] // 2
            return ref.at[chip_idx, pl.ds(hc * rh, rh), :]

        sends = []
        for w in range(n):
            for k, chip in enumerate(chips):
                d = cp(w, k, part(i_refs[w], j, c), part(g_refs[w], j, c), (*chip, c))
                d.start()
                sends.append(d)
        for w in range(n):
            for k, (px, py) in enumerate(chips):
                landed = part(g_refs[w], 2 * px + py, c)
                cp(w, k, landed, landed, (px, py, c)).wait_recv()
                d = cp(w, 3 + k, landed, landed, sibling)
                d.start()
                sends.append(d)
        for w in range(n):
            for k, (px, py) in enumerate(chips):
                other = part(g_refs[w], 2 * px + py, 1 - c)
                cp(w, 3 + k, other, other, sibling).wait_recv()
        for d in sends:
            d.wait_send()

    return pl.pallas_call(
        body, name=name,
        out_shape=[jax.ShapeDtypeStruct(s.shape, s.dtype) for s in slots],
        in_specs=[ANY] * n, out_specs=[ANY] * n, input_output_aliases={w: w for w in range(n)},
        scratch_shapes=[pltpu.SemaphoreType.DMA((6 * n,)), pltpu.SemaphoreType.DMA((6 * n,))],
    )(*slots)


def _pack(arrs):
    parts = []
    for v in arrs:
        f = v.reshape(-1)
        f = jnp.pad(f, (0, (-f.shape[0]) % 1024))
        parts.append(f.reshape(-1, 128))
    return jnp.concatenate(parts, axis=0)


def _unpack(packed, shapes):
    out, r0 = [], 0
    for s in shapes:
        size = 1
        for d in s:
            size *= d
        nr = (size + 1023) // 1024 * 8
        out.append(packed[r0:r0 + nr].reshape(-1)[:size].reshape(s))
        r0 += nr
    return out


SMALL_GRADS = [("loss", (1, 128)), ("pre_mix_norm", (1, 2048)), ("ssd_conv", (8, 4096)), ("ssd_par", (8, 128)),
               ("ssd_norm", (1, 2048)), ("lru_conv", (8, 2048)), ("lru_w_a", (16, 128, 128)), ("lru_b_a", (1, 2048)),
               ("lru_w_x", (16, 128, 128)), ("lru_b_x", (1, 2048)), ("lru_lambda", (1, 2048)), ("lru_norm", (1, 2048)),
               ("post_mix_norm", (1, 2048)), ("pre_mlp_norm", (1, 2048)), ("post_mlp_norm", (1, 2048))]

SMALL_EARLY = [g for g in SMALL_GRADS if g[0] != "pre_mix_norm"]

WEIGHTS = ['pre_mix_norm', 'w_in', 'ssd_conv_w', 'ssd_conv_b', 'ssd_dt_bias', 'ssd_a_log', 'ssd_d', 'ssd_norm', 'lru_conv_w',
           'lru_conv_b', 'lru_w_a', 'lru_b_a', 'lru_w_x', 'lru_b_x', 'lru_lambda', 'lru_norm', 'w_out', 'post_mix_norm',
           'pre_mlp_norm', 'w_mlp_in', 'w_mlp_out', 'post_mlp_norm']
LARGE = ['w_in', 'w_out', 'w_mlp_in', 'w_mlp_out']

D_SSD, D_XBC, DT_W = 2048, 4096, 32
TB = 256


def _w_in_runs(n_shard, n_main):
    c_dt = D_SSD + D_XBC
    runs, p = [], 0
    while p < n_main:
        j, off = divmod(p if p < c_dt else p + DT_W, n_shard)
        ln = min(n_shard - off, (c_dt if p < c_dt else n_main) - p)
        runs.append((p, j, off, ln))
        p += ln
    jd, offd = divmod(c_dt, n_shard)
    assert offd + DT_W <= n_shard
    return runs, (jd, offd)


def _pack_w_in(slots, *, name):
    _, d, n_shard = slots.shape
    n_main = 4 * n_shard - DT_W
    runs, (jd, offd) = _w_in_runs(n_shard, n_main)

    def body(s_ref, main_ref, dt_ref):
        for p, j, off, ln in runs:
            main_ref[:, p:p + ln] = s_ref[j, :, off:off + ln]
        dt_ref[:, 0:DT_W] = s_ref[jd, :, offd:offd + DT_W]
        dt_ref[:, DT_W:] = jnp.zeros((TB, 128 - DT_W), dt_ref.dtype)

    return pl.pallas_call(
        body, name=name, grid=(d // TB,),
        in_specs=[pl.BlockSpec((4, TB, n_shard), lambda i: (0, i, 0))],
        out_specs=[pl.BlockSpec((TB, n_main), lambda i: (i, 0)), pl.BlockSpec((TB, 128), lambda i: (i, 0))],
        out_shape=[jax.ShapeDtypeStruct((d, n_main), slots.dtype), jax.ShapeDtypeStruct((d, 128), slots.dtype)],
        compiler_params=_params(("parallel",)),
    )(slots)


def _unpack_dw_in(dw_main, dw_dt, *, name):
    d, n_main = dw_main.shape
    n_shard = (n_main + DT_W) // 4
    runs, (jd, offd) = _w_in_runs(n_shard, n_main)

    def body(main_ref, dt_ref, o_ref):
        for p, j, off, ln in runs:
            o_ref[j, :, off:off + ln] = main_ref[:, p:p + ln]
        o_ref[jd, :, offd:offd + DT_W] = dt_ref[:, 0:DT_W]

    return pl.pallas_call(
        body, name=name, grid=(d // TB,),
        in_specs=[pl.BlockSpec((TB, n_main), lambda i: (i, 0)), pl.BlockSpec((TB, 128), lambda i: (i, 0))],
        out_specs=pl.BlockSpec((4, TB, n_shard), lambda i: (0, i, 0)),
        out_shape=jax.ShapeDtypeStruct((4, d, n_shard), dw_main.dtype),
        compiler_params=_params(("parallel",)),
    )(dw_main, dw_dt)


def _half(ref, chip_idx, hc, piece=(0, 1)):
    q, nq = piece
    rp = ref.shape[1] // (2 * nq)
    return ref.at[chip_idx, pl.ds((hc * nq + q) * rp, rp), :]


def _job_gather_ici(i_ref, g_ref, send_sems, recv_sems, base, piece):
    x, y, c = _place()
    j = 2 * x + y
    sends, recvs = [], []
    for k, (px, py) in enumerate(_other_chips(x, y)):
        kw = dict(send_sem=send_sems.at[base + k], recv_sem=recv_sems.at[base + k], device_id=(px, py, c), device_id_type=MESH)
        sends.append(pltpu.make_async_remote_copy(src_ref=_half(i_ref, j, c, piece), dst_ref=_half(g_ref, j, c, piece), **kw))
        landed = _half(g_ref, 2 * px + py, c, piece)
        recvs.append(pltpu.make_async_remote_copy(src_ref=landed, dst_ref=landed, **kw))
    return sends, recvs


def _job_gather_sibling(i_ref, g_ref, send_sems, recv_sems, base, piece):
    x, y, c = _place()
    sends, recvs = [], []
    for k, (px, py) in enumerate(_other_chips(x, y)):
        kw = dict(send_sem=send_sems.at[base + k], recv_sem=recv_sems.at[base + k], device_id=(x, y, 1 - c), device_id_type=MESH)
        sends.append(pltpu.make_async_remote_copy(src_ref=_half(i_ref, 2 * px + py, c, piece),
                                                  dst_ref=_half(g_ref, 2 * px + py, c, piece), **kw))
        other = _half(g_ref, 2 * px + py, 1 - c, piece)
        recvs.append(pltpu.make_async_remote_copy(src_ref=other, dst_ref=other, **kw))
    return sends, recvs


def _job_reduce_ici(s_ref, got_refs, send_sems, recv_sems, base, piece):
    x, y, c = _place()
    q, nq = piece
    rp = s_ref.shape[1] // nq
    rows = pl.ds(q * rp, rp)
    sends = [pltpu.make_async_remote_copy(src_ref=s_ref.at[2 * px + py, rows, :], dst_ref=got_refs[k].at[rows, :],
                                          send_sem=send_sems.at[base + k], recv_sem=recv_sems.at[base + k],
                                          device_id=(px, py, c), device_id_type=MESH)
             for k, (px, py) in enumerate(_other_chips(x, y))]
    return sends, sends


GATHER_PLAN = {
    "f_inproj": [("w_out", _job_gather_ici, (0, 1)), ("w_mlp_in", _job_gather_ici, (0, 2))],
    "f_ssdconv": [("w_mlp_in", _job_gather_ici, (1, 2)), ("w_out", _job_gather_sibling, (0, 1)),
                  ("w_mlp_in", _job_gather_sibling, (0, 2))],
    "f_lruconv": [("w_mlp_in", _job_gather_sibling, (1, 2))],
    "f_ssd": [("w_mlp_out", _job_gather_ici, (0, 2))],
    "f_lrugates": [("w_mlp_out", _job_gather_sibling, (0, 2))],
    "f_outproj": [("w_mlp_out", _job_gather_ici, (1, 2))],
    "f_mlpin": [("w_mlp_out", _job_gather_sibling, (1, 2))],
}
SIBLING_PLAN = {"b_mlpin_dx": "w_mlp_out", "b_mid": "w_mlp_in", "b_outproj_dx": "w_out"}
REDUCE_PLAN = {
    "b_mlpin_dw": [("w_mlp_out", (0, 2))],
    "b_outproj_dw": [("w_mlp_out", (1, 2))],
    "b_ssd": [("w_mlp_in", (0, 1)), ("w_out", (0, 1))],
    "b_inproj_dx": [("w_in", (0, 1))],
}
SHARE_PLAN = {"b_ssdconv": ["w_mlp_out", "w_mlp_in", "w_out"]}
SMALL_PLAN = {"b_inproj_dw": 0, "b_inproj_dx": 1}


class _Part:
    def __init__(self, ins, outs, aliases, n_sems, make, done):
        self.ins, self.outs, self.aliases, self.n_sems, self.make, self.done = ins, outs, aliases, n_sems, make, done


def _merge_parts(parts):
    ins, outs, aliases, offs, n = [], [], {}, [], 0
    for p in parts:
        offs.append((len(ins), len(outs), n))
        aliases.update({len(ins) + i: len(outs) + o for i, o in p.aliases.items()})
        ins, outs, n = ins + list(p.ins), outs + list(p.outs), n + p.n_sems

    def make(i_refs, o_refs, send_sems, recv_sems):
        sends, recvs = [], []
        for p, (io, oo, so) in zip(parts, offs):
            s, r = p.make(i_refs[io:io + len(p.ins)], o_refs[oo:oo + len(p.outs)], send_sems, recv_sems, so)
            sends, recvs = sends + s, recvs + r
        return sends, recvs

    return _Side(ins, outs, aliases, n, make)


def _parts_done(parts, outs):
    o = 0
    for p in parts:
        p.done(list(outs[o:o + len(p.outs)]))
        o += len(p.outs)


def _comm_call(parts, *, name):
    side = _merge_parts(parts)
    si, so = len(side.ins), len(side.outs)

    def body(*refs):
        sends, recvs = side.make(refs[:si], refs[si:si + so], refs[-2], refs[-1])
        for d in sends:
            d.start()
        for d in recvs:
            d.wait_recv()
        for d in sends:
            d.wait_send()

    outs = pl.pallas_call(
        body, name=name, out_shape=list(side.outs), in_specs=[ANY] * si, out_specs=[ANY] * so,
        scratch_shapes=[pltpu.SemaphoreType.DMA((side.n_sems,)), pltpu.SemaphoreType.DMA((side.n_sems,))],
        input_output_aliases=side.aliases,
    )(*side.ins)
    _parts_done(parts, outs)


class _DistPlan:
    def __init__(self, slots, w_main, w_dt, where, shapes):
        self.slots, self.w_main, self.w_dt, self.where, self.shapes = slots, w_main, w_dt, where, shapes
        self.sib_pending, self.partial, self.got, self.totals, self.shared = {}, {}, {}, {}, {}
        self.small_buf = None

    def weight(self, name):
        d = self.w_main.shape[0]
        return {"w_main": lambda: self.w_main, "w_dt": lambda: self.w_dt, "w_out": lambda: self.slots["w_out"].reshape(-1, d),
                "w_mi": lambda: self.slots["w_mlp_in"], "w_mo": lambda: self.slots["w_mlp_out"].reshape(-1, d)}[name]()

    def _gather_part(self, jobs):
        names = []
        for n, _, _ in jobs:
            if n not in names:
                names.append(n)
        arrs = [self.slots[n] for n in names]

        def make(i_refs, o_refs, send_sems, recv_sems, base):
            sends, recvs = [], []
            for q, (n, job, piece) in enumerate(jobs):
                s, r = job(i_refs[names.index(n)], o_refs[names.index(n)], send_sems, recv_sems, base + 3 * q, piece)
                sends, recvs = sends + s, recvs + r
            return sends, recvs

        return _Part(arrs, [jax.ShapeDtypeStruct(a.shape, a.dtype) for a in arrs], {i: i for i in range(len(arrs))},
                     3 * len(jobs), make, lambda outs: self.slots.update(zip(names, outs)))

    def _sibling_part(self, name):
        g = self.sib_pending.pop(name)
        _, _, rh, cc = g.shape

        def make(i_refs, o_refs, send_sems, recv_sems, base):
            x, y, c = _place()
            sends = [pltpu.make_async_remote_copy(src_ref=i_refs[0].at[s, 1 - c], dst_ref=o_refs[0].at[s],
                                                  send_sem=send_sems.at[base + s], recv_sem=recv_sems.at[base + s],
                                                  device_id=(x, y, 1 - c), device_id_type=MESH) for s in range(4)]
            return sends, sends

        def done(outs):
            (t,) = outs
            self.partial[name] = _blocks(
                lambda u, v: (u.astype(F32) + v.astype(F32),),
                [(g, (None, None, TB, cc), lambda q, i, s: (q, s[1], i, 0)), (t, (None, TB, cc), lambda q, i, s: (q, i, 0))],
                [(jax.ShapeDtypeStruct(t.shape, BF16), (None, TB, cc), lambda q, i, s: (q, i, 0))],
                grid=(4, rh // TB), name="add_sibling_" + name, prefetch=self.where)[0]

        return _Part([g], [jax.ShapeDtypeStruct((4, rh, cc), g.dtype)], {}, 4, make, done)

    def _reduce_part(self, items):
        parts = [self.partial[n] for n, _ in items]
        ins, aliases = list(parts), {}
        for q, (n, (piece, _)) in enumerate(items):
            if piece > 0:
                for k in range(3):
                    aliases[len(ins)] = 3 * q + k
                    ins.append(self.got[n][k])

        def make(i_refs, o_refs, send_sems, recv_sems, base):
            sends = []
            for q, (_, piece) in enumerate(items):
                sends += _job_reduce_ici(i_refs[q], o_refs[3 * q:3 * q + 3], send_sems, recv_sems, base + 3 * q, piece)[0]
            return sends, sends

        def done(outs):
            for q, (name, (piece, n_pieces)) in enumerate(items):
                g3 = self.got[name] = outs[3 * q:3 * q + 3]
                if piece < n_pieces - 1:
                    continue
                s4 = self.partial[name]
                _, rh, cc = s4.shape
                row = ((TB, cc), lambda i, s: (i, 0))
                self.totals[name] = _blocks(
                    lambda o, r0, r1, r2: (((o.astype(F32) + r0.astype(F32)) + r1.astype(F32)) + r2.astype(F32),),
                    [(s4, (None, TB, cc), lambda i, s: (s[0], i, 0)), (g3[0],) + row, (g3[1],) + row, (g3[2],) + row],
                    [(jax.ShapeDtypeStruct((2, rh, cc), F32), (None, TB, cc), lambda i, s: (s[1], i, 0))],
                    grid=(rh // TB,), name="add_chips_" + name, prefetch=self.where)[0]
                if name == "w_in":
                    _comm_call([self._share_part([name])], name="share_" + name)

        outs = [jax.ShapeDtypeStruct(s.shape[1:], s.dtype) for s in parts for _ in range(3)]
        return _Part(ins, outs, aliases, 3 * len(items), make, done)

    def _share_part(self, names):
        ts = [self.totals[n] for n in names]

        def make(i_refs, o_refs, send_sems, recv_sems, base):
            x, y, c = _place()
            sends, recvs = [], []
            for w in range(len(ts)):
                kw = dict(send_sem=send_sems.at[base + w], recv_sem=recv_sems.at[base + w], device_id=(x, y, 1 - c),
                          device_id_type=MESH)
                sends.append(pltpu.make_async_remote_copy(src_ref=i_refs[w].at[c], dst_ref=o_refs[w].at[c], **kw))
                recvs.append(pltpu.make_async_remote_copy(src_ref=i_refs[w].at[c], dst_ref=o_refs[w].at[1 - c], **kw))
            return sends, recvs

        def done(outs):
            self.shared.update({n: o.reshape(self.shapes[n]) for n, o in zip(names, outs)})

        return _Part(ts, [jax.ShapeDtypeStruct(t.shape, t.dtype) for t in ts], {i: i for i in range(len(ts))}, len(ts), make, done)

    def _small_part(self, stage):
        buf = self.small_buf

        def make(i_refs, o_refs, send_sems, recv_sems, base):
            x, y, c = _place()
            src, dst = i_refs[0], o_refs[0]
            sends, recvs = [], []

            def add(k, block, to, frm):
                kw = dict(send_sem=send_sems.at[base + k], recv_sem=recv_sems.at[base + k], device_id=to, device_id_type=MESH)
                sends.append(pltpu.make_async_remote_copy(src_ref=src.at[block], dst_ref=dst.at[block], **kw))
                recvs.append(pltpu.make_async_remote_copy(src_ref=src.at[frm], dst_ref=dst.at[frm], **kw))

            if stage == 0:
                add(0, 4 * x + 2 * y + c, (x, y, 1 - c), 4 * x + 2 * y + 1 - c)
            for k, (px, py) in enumerate(_other_chips(x, y)):
                if stage == 0:
                    add(1 + k, 4 * x + 2 * y + c, (px, py, c), 4 * px + 2 * py + c)
                else:
                    add(k, 4 * px + 2 * py + c, (x, y, 1 - c), 4 * px + 2 * py + 1 - c)
            return sends, recvs

        def done(outs):
            (self.small_buf,) = outs

        return _Part([buf], [jax.ShapeDtypeStruct(buf.shape, buf.dtype)], {0: 0}, 4 if stage == 0 else 3, make, done)

    def side(self, kernel_name):
        parts = []
        if kernel_name in GATHER_PLAN:
            parts.append(self._gather_part(GATHER_PLAN[kernel_name]))
        if kernel_name in SIBLING_PLAN:
            parts.append(self._sibling_part(SIBLING_PLAN[kernel_name]))
        if kernel_name in REDUCE_PLAN:
            parts.append(self._reduce_part(REDUCE_PLAN[kernel_name]))
        if kernel_name in SHARE_PLAN:
            parts.append(self._share_part(SHARE_PLAN[kernel_name]))
        if kernel_name in SMALL_PLAN:
            parts.append(self._small_part(SMALL_PLAN[kernel_name]))
        self._carried = parts
        return _merge_parts(parts) if parts else None

    def done(self, kernel_name, side_outs):
        _parts_done(self._carried, side_outs)

    def grad(self, name, g):
        if name == "w_in":
            g = _unpack_dw_in(*g, name="unpack_dw_in")
        self.sib_pending[name] = g.reshape(4, 2, g.shape[1] // 2, g.shape[2])
        if name == "w_in":
            _comm_call([self._sibling_part(name)], name="reduce_sibling_" + name)

    def small(self, grads):
        packed = _pack([grads[n] for n, _ in SMALL_EARLY])
        n_rows = packed.shape[0]
        self.small_buf = _blocks(lambda t: (t,), [(packed, (n_rows, 128), lambda i, s: (0, 0))],
                                 [(jax.ShapeDtypeStruct((8, n_rows, 128), F32), (None, n_rows, 128), lambda i, s: (s[2], 0, 0))],
                                 grid=(1,), name="place_small", prefetch=self.where)[0]

    def finish(self):
        assert len(self.shared) == len(self.shapes)
        return self.shared


def kernel(x, pre_mix_norm, w_in, ssd_conv_w, ssd_conv_b, ssd_dt_bias, ssd_a_log, ssd_d, ssd_norm, lru_conv_w, lru_conv_b, lru_w_a, lru_b_a, lru_w_x, lru_b_x, lru_lambda, lru_norm, w_out, post_mix_norm, pre_mlp_norm, w_mlp_in, w_mlp_out, post_mlp_norm, loss_target, m_pre_mix_norm, m_w_in, m_ssd_conv_w, m_ssd_conv_b, m_ssd_dt_bias, m_ssd_a_log, m_ssd_d, m_ssd_norm, m_lru_conv_w, m_lru_conv_b, m_lru_w_a, m_lru_b_a, m_lru_w_x, m_lru_b_x, m_lru_lambda, m_lru_norm, m_w_out, m_post_mix_norm, m_pre_mlp_norm, m_w_mlp_in, m_w_mlp_out, m_post_mlp_norm, v_pre_mix_norm, v_w_in, v_ssd_conv_w, v_ssd_conv_b, v_ssd_dt_bias, v_ssd_a_log, v_ssd_d, v_ssd_norm, v_lru_conv_w, v_lru_conv_b, v_lru_w_a, v_lru_b_a, v_lru_w_x, v_lru_b_x, v_lru_lambda, v_lru_norm, v_w_out, v_post_mix_norm, v_pre_mlp_norm, v_w_mlp_in, v_w_mlp_out, v_post_mlp_norm):
    a = dict(locals())
    j = 2 * lax.axis_index("x") + lax.axis_index("y")
    D = x.shape[-1]
    c_dt = D_SSD + D_XBC

    core = lax.axis_index("c")
    where = jnp.stack([j, core, 2 * j + core]).astype(jnp.int32)
    slots = {}
    for name in LARGE:
        w = a[name][0]
        r, cc = w.shape
        slots[name] = _blocks(lambda t: (t,), [(w, (TB, cc), lambda i, s: (i, 0))],
                              [(jax.ShapeDtypeStruct((4, r, cc), BF16), (None, TB, cc), lambda i, s: (s[0], i, 0))],
                              grid=(r // TB,), name="cast_" + name, prefetch=where)[0]
    (g_in,) = _gather_weights([slots.pop("w_in")], name="gather_w_in")
    w_main, w_dt = _pack_w_in(g_in, name="pack_w_in")
    taps = jnp.concatenate([ssd_conv_w[0].reshape(-1, 128), lru_conv_w[0].reshape(-1, 128)], axis=0)
    taps = _allgather8(taps, name="gather_taps").reshape(8, taps.shape[0], 128)[0::2]
    n_ssd = ssd_conv_w.shape[1] * ssd_conv_w.shape[2] // 128
    ssd_taps = taps[:, :n_ssd].reshape(4, CONV_W, -1).transpose(1, 0, 2).reshape(CONV_W, -1)
    lru_taps = taps[:, n_ssd:].reshape(4, CONV_W, -1).transpose(1, 0, 2).reshape(CONV_W, -1)

    def row128(v):
        return jnp.pad(v, ((0, 0), (0, 128 - v.shape[1])))

    p = dict(pre_mix_norm=pre_mix_norm, ssd_conv_w=ssd_taps, ssd_conv_b=ssd_conv_b,
             dtb=row128(ssd_dt_bias), alog=row128(ssd_a_log), drow=row128(ssd_d), ssd_norm=ssd_norm,
             lru_conv_w=lru_taps, lru_conv_b=lru_conv_b, lru_w_a=lru_w_a[0], lru_b_a=lru_b_a.reshape(1, -1),
             lru_w_x=lru_w_x[0], lru_b_x=lru_b_x.reshape(1, -1), lru_lambda=lru_lambda, lru_norm=lru_norm,
             post_mix_norm=post_mix_norm, pre_mlp_norm=pre_mlp_norm, post_mlp_norm=post_mlp_norm)
    plan = _DistPlan(slots, w_main, w_dt, where, {n: a[n].shape for n in LARGE})
    grad_x, small = _local_step(x[0], loss_target[0], p, plan)

    large_grads = plan.finish()

    delta, new_m, new_v = {}, {}, {}
    for name in LARGE:
        w = a[name][0]
        cc = w.shape[1]
        outs = _rowwise(_adamw, [w, large_grads[name][0], a["m_" + name][0], a["v_" + name][0]], [], [(cc, F32)] * 3, [],
                        name="adamw_" + name, tr=128, sub=8)
        delta[name], new_m[name], new_v[name] = [o[None] for o in outs]

    total = _sum8(plan.small_buf, name="sum_small")
    tot = dict(zip([n for n, _ in SMALL_EARLY], _unpack(total, [s for _, s in SMALL_EARLY])))
    late = small["pre_mix_norm"].reshape(-1, 128)
    late = _allgather8(late, name="gather_late").reshape(8, late.shape[0], 128)
    tot["pre_mix_norm"] = _sum8(late, name="sum_late").reshape(small["pre_mix_norm"].shape)
    loss = tot["loss"][0, 0]
    n_sc, n_lc = ssd_conv_w.shape[2], lru_conv_w.shape[2]
    grads = dict(
        pre_mix_norm=tot["pre_mix_norm"],
        ssd_conv_w=lax.dynamic_slice(tot["ssd_conv"][:CONV_W], (0, j * n_sc), (CONV_W, n_sc))[None],
        ssd_conv_b=tot["ssd_conv"][CONV_W:CONV_W + 1],
        ssd_dt_bias=tot["ssd_par"][0:1, :DT_W], ssd_a_log=tot["ssd_par"][1:2, :DT_W], ssd_d=tot["ssd_par"][2:3, :DT_W],
        ssd_norm=tot["ssd_norm"],
        lru_conv_w=lax.dynamic_slice(tot["lru_conv"][:CONV_W], (0, j * n_lc), (CONV_W, n_lc))[None],
        lru_conv_b=tot["lru_conv"][CONV_W:CONV_W + 1],
        lru_w_a=tot["lru_w_a"][None], lru_b_a=tot["lru_b_a"].reshape(lru_b_a.shape),
        lru_w_x=tot["lru_w_x"][None], lru_b_x=tot["lru_b_x"].reshape(lru_b_x.shape),
        lru_lambda=tot["lru_lambda"], lru_norm=tot["lru_norm"], post_mix_norm=tot["post_mix_norm"],
        pre_mlp_norm=tot["pre_mlp_norm"], post_mlp_norm=tot["post_mlp_norm"])

    grads.update(large_grads)

    small_w = [n for n in WEIGHTS if n not in LARGE]
    n_sw = len(small_w)

    def flat(v):
        return v.reshape(-1, v.shape[-1])

    def adamw_all(*refs):
        ins, outs = refs[:4 * n_sw], refs[4 * n_sw:]
        for q in range(n_sw):
            res = _adamw(*[ins[4 * q + t][...] for t in range(4)])
            for t in range(3):
                outs[3 * q + t][...] = res[t]

    operands = [flat(d[n]) for n in small_w for d in (a, grads, {k: a["m_" + k] for k in small_w}, {k: a["v_" + k] for k in small_w})]
    outs = pl.pallas_call(
        adamw_all, name="adamw_small",
        out_shape=[jax.ShapeDtypeStruct(flat(a[n]).shape, F32) for n in small_w for _ in range(3)],
        in_specs=[pl.BlockSpec(memory_space=pltpu.VMEM)] * (4 * n_sw), out_specs=[pl.BlockSpec(memory_space=pltpu.VMEM)] * (3 * n_sw),
        compiler_params=pltpu.CompilerParams(vmem_limit_bytes=VMEM_LIMIT),
    )(*operands)
    for q, n in enumerate(small_w):
        delta[n], new_m[n], new_v[n] = [o.reshape(a[n].shape) for o in outs[3 * q:3 * q + 3]]

    return (loss, grad_x[None], *[grads[n] for n in WEIGHTS], *[delta[n] for n in WEIGHTS],
            *[new_m[n] for n in WEIGHTS], *[new_v[n] for n in WEIGHTS])
```

```python
import functools

import jax
import jax.numpy as jnp
from jax import lax
from jax.experimental import pallas as pl
from jax.experimental.pallas import tpu as pltpu

F32 = jnp.float32
BF16 = jnp.bfloat16
MESH = pl.DeviceIdType.MESH

EPS = 1e-6
LRU_C = 8.0
ADAM_LR = 0.001
ADAM_B1 = 0.9
ADAM_B2 = 0.999
ADAM_EPS = 1e-08
ADAM_WD = 0.01
ADAM_STEP = 10

N_GROUPS = 8
HEADS_PER_GROUP = 4
HEAD_DIM = 64
GROUP_W = HEADS_PER_GROUP * HEAD_DIM
STATE = 128
LRU_HEADS = 16
LRU_BLOCK = 128
CONV_W = 4
SSD_CHUNK = 256
HALO = 8

VMEM_LIMIT = 48 * 1024 * 1024


def _params(sem=None):
    return pltpu.CompilerParams(dimension_semantics=sem, vmem_limit_bytes=VMEM_LIMIT)


@jax.custom_jvp
def _log1p(x):
    u = 1.0 + x
    d = u - 1.0
    return jnp.where(d == 0.0, x, jnp.log(u) * (x / jnp.where(d == 0.0, 1.0, d)))


@_log1p.defjvp
def _log1p_jvp(primals, tangents):
    (x,), (t,) = primals, tangents
    return _log1p(x), t / (1.0 + x)


@jax.custom_jvp
def _expm1(x):
    u = jnp.exp(x)
    lu = jnp.log(u)
    safe = jnp.where(lu == 0.0, 1.0, lu)
    y = (u - 1.0) * (x / safe)
    y = jnp.where(lu == 0.0, x, y)
    return jnp.where(u == 0.0, -1.0, y)


@_expm1.defjvp
def _expm1_jvp(primals, tangents):
    (x,), (t,) = primals, tangents
    return _expm1(x), t * jnp.exp(x)


def _softplus(x):
    return jnp.maximum(x, 0.0) + _log1p(jnp.exp(-jnp.abs(x)))


def _sigmoid(x):
    return 1.0 / (1.0 + jnp.exp(-x))


def _silu(x):
    return x * _sigmoid(x)


def _gelu(x):
    c = 0.7978845608028654
    return 0.5 * x * (1.0 + jnp.tanh(c * (x + 0.044715 * (x * x * x))))


def _rms(x, g):
    return x * lax.rsqrt(jnp.mean(x * x, axis=-1, keepdims=True) + EPS) * g


def _dot(a, b, dims):
    return lax.dot_general(a.astype(BF16), b.astype(BF16), (dims, ((), ())), preferred_element_type=F32)


_NN = ((1,), (0,))
_NT = ((1,), (1,))
_TN = ((0,), (0,))


ANY = pl.BlockSpec(memory_space=pl.ANY)


class _Side:
    def __init__(self, ins, outs, aliases, n_sems, make):
        self.ins, self.outs, self.aliases, self.n_sems, self.make = ins, outs, aliases, n_sems, make


def _call(body, args, *, name, grid, in_specs, out_specs, out_shape, scratch_shapes=(), sem=None, side=None, into=None):
    in_specs, out_specs, out_shape, scratch_shapes = list(in_specs), list(out_specs), list(out_shape), list(scratch_shapes)
    held = []
    for oi, (buf, n_cols, off) in (into or {}).items():
        spec = out_specs[oi]
        out_shape[oi] = jax.ShapeDtypeStruct((out_shape[oi].shape[0], n_cols), out_shape[oi].dtype)
        out_specs[oi] = pl.BlockSpec(spec.block_shape, lambda *idx, _m=spec.index_map, _o=off: (_m(*idx)[0], _m(*idx)[1] + _o))
        if buf is not None:
            held.append((oi, buf))
    if side is None and not held:
        outs = pl.pallas_call(body, name=name, grid=grid, in_specs=in_specs, out_specs=out_specs, out_shape=out_shape,
                              scratch_shapes=scratch_shapes, compiler_params=_params(sem))(*args)
        return list(outs), []
    side_ins, side_outs = (list(side.ins), list(side.outs)) if side is not None else ([], [])
    n_in, n_out, n_scr, nh, si, so = len(in_specs), len(out_specs), len(scratch_shapes), len(held), len(side_ins), len(side_outs)

    def full(*refs):
        s_in = refs[n_in + nh:n_in + nh + si]
        o0 = n_in + nh + si
        s_out = refs[o0 + n_out:o0 + n_out + so]
        scr = refs[o0 + n_out + so:o0 + n_out + so + n_scr]
        if side is None:
            body(*refs[:n_in], *refs[o0:o0 + n_out], *scr)
            return
        send_sems, recv_sems = refs[-2], refs[-1]
        ids = [pl.program_id(d) for d in range(len(grid))]
        first = functools.reduce(jnp.logical_and, [i == 0 for i in ids])
        last = functools.reduce(jnp.logical_and, [i == g - 1 for i, g in zip(ids, grid)])

        @pl.when(first)
        def _():
            for d in side.make(s_in, s_out, send_sems, recv_sems)[0]:
                d.start()

        body(*refs[:n_in], *refs[o0:o0 + n_out], *scr)

        @pl.when(last)
        def _():
            sends, recvs = side.make(s_in, s_out, send_sems, recv_sems)
            for d in recvs:
                d.wait_recv()
            for d in sends:
                d.wait_send()

    aliases = {n_in + h: oi for h, (oi, _) in enumerate(held)}
    if side is not None:
        aliases.update({n_in + nh + i: n_out + o for i, o in side.aliases.items()})
        scratch_shapes = scratch_shapes + [pltpu.SemaphoreType.DMA((side.n_sems,)), pltpu.SemaphoreType.DMA((side.n_sems,))]
    outs = pl.pallas_call(
        full, name=name, grid=grid, in_specs=in_specs + [ANY] * (nh + si), out_specs=out_specs + [ANY] * so,
        out_shape=out_shape + side_outs, scratch_shapes=scratch_shapes, input_output_aliases=aliases,
        compiler_params=_params(("arbitrary",) * len(grid) if side is not None else sem),
    )(*args, *[b for _, b in held], *side_ins)
    return list(outs[:n_out]), list(outs[n_out:])


def _matmul(a, b, *, mode, m, n, k, tm, tn, tk, out_dtypes, name, a_spec=None, b_spec=None,
            out_specs=None, out_shapes=None, extras=(), epilogue=None, side=None):
    tm, tn, tk = min(tm, m), min(tn, n), min(tk, k)
    assert m % tm == 0 and n % tn == 0 and k % tk == 0, (name, m, n, k, tm, tn, tk)
    nk = k // tk
    dims = {"nn": _NN, "nt": _NT, "tn": _TN}[mode]
    if a_spec is None:
        a_spec = pl.BlockSpec((tk, tm), lambda i, j, kk: (kk, i)) if mode == "tn" else pl.BlockSpec((tm, tk), lambda i, j, kk: (i, kk))
    if b_spec is None:
        b_spec = pl.BlockSpec((tn, tk), lambda i, j, kk: (j, kk)) if mode == "nt" else pl.BlockSpec((tk, tn), lambda i, j, kk: (kk, j))
    tile = pl.BlockSpec((tm, tn), lambda i, j, kk: (i, j))
    if out_specs is None:
        out_specs = [tile for _ in out_dtypes]
    if out_shapes is None:
        out_shapes = [jax.ShapeDtypeStruct((m, n), d) for d in out_dtypes]
    n_ex, n_out = len(extras), len(out_dtypes)

    def body(*refs):
        a_ref, b_ref = refs[0], refs[1]
        ex_refs = refs[2:2 + n_ex]
        o_refs = refs[2 + n_ex:2 + n_ex + n_out]
        def finish(r):
            outs = epilogue(r, *[e[...] for e in ex_refs]) if epilogue is not None else (r,)
            for o_ref, o in zip(o_refs, outs):
                o_ref[...] = o.astype(o_ref.dtype)

        if nk == 1:
            finish(_dot(a_ref[...], b_ref[...], dims))
            return
        acc = refs[-1]
        kk = pl.program_id(2)

        @pl.when(kk == 0)
        def _():
            acc[...] = _dot(a_ref[...], b_ref[...], dims)

        @pl.when(kk > 0)
        def _():
            acc[...] += _dot(a_ref[...], b_ref[...], dims)

        @pl.when(kk == nk - 1)
        def _():
            finish(acc[...])

    outs, side_outs = _call(
        body, (a, b, *extras), name=name, grid=(m // tm, n // tn, nk),
        in_specs=[a_spec, b_spec] + [tile for _ in extras], out_specs=out_specs, out_shape=out_shapes,
        scratch_shapes=[] if nk == 1 else [pltpu.VMEM((tm, tn), F32)], sem=("parallel", "parallel", "arbitrary"), side=side)
    return outs if side is None else (outs, side_outs)


def _rowwise(fn, rows, bcast, out_rows, out_acc, *, name, tr, sub, side=None, into=None):
    rows = [r if isinstance(r, tuple) else (r, 0, r.shape[1]) for r in rows]
    row_specs = []
    for arr, c0, w in rows:
        assert c0 % w == 0, (name, c0, w)
        row_specs.append((w, c0 // w))
    rows = [r[0] for r in rows]
    L = rows[0].shape[0]
    tr = min(tr, L)
    sub = min(sub, tr)
    assert L % tr == 0 and tr % sub == 0, (name, L, tr, sub)
    n_r, n_b, n_or, n_oa = len(rows), len(bcast), len(out_rows), len(out_acc)

    def body(*refs):
        r_refs = refs[:n_r]
        b_refs = refs[n_r:n_r + n_b]
        or_refs = refs[n_r + n_b:n_r + n_b + n_or]
        oa_refs = refs[n_r + n_b + n_or:]
        i = pl.program_id(0)

        @pl.when(i == 0)
        def _():
            for o in oa_refs:
                o[...] = jnp.zeros_like(o)

        bvals = [b[...] for b in b_refs]

        def step(s, carry):
            r0 = pl.multiple_of(s * sub, sub)
            tiles = [r[pl.ds(r0, sub), :] for r in r_refs]
            outs = fn(*tiles, *bvals)
            for o_ref, o in zip(or_refs, outs[:n_or]):
                o_ref[pl.ds(r0, sub), :] = o.astype(o_ref.dtype)
            for o_ref, o in zip(oa_refs, outs[n_or:]):
                o_ref[...] += o
            return carry

        if tr == sub:
            step(0, 0)
        else:
            lax.fori_loop(0, tr // sub, step, 0)

    def whole(shape):
        nd = len(shape)
        return pl.BlockSpec(shape, lambda i, _n=nd: (0,) * _n)

    outs, side_outs = _call(
        body, (*rows, *bcast), name=name, grid=(L // tr,),
        in_specs=[pl.BlockSpec((tr, w), lambda i, _c=cb: (i, _c)) for w, cb in row_specs] + [whole(b.shape) for b in bcast],
        out_specs=[pl.BlockSpec((tr, c), lambda i: (i, 0)) for c, _ in out_rows] + [whole(s) for s in out_acc],
        out_shape=[jax.ShapeDtypeStruct((L, c), d) for c, d in out_rows] + [jax.ShapeDtypeStruct(s, F32) for s in out_acc],
        sem=("arbitrary",), side=side, into=into)
    return outs if side is None else (outs, side_outs)


def _colsum(x):
    return jnp.sum(x, axis=0, keepdims=True)


def _grouped(fn, rows, params, out_rows, out_acc, *, gw, name, tr, side=None, into=None):
    rows = [r if isinstance(r, tuple) else (r, 0) for r in rows]
    L = rows[0][0].shape[0]
    tr = min(tr, L)
    assert L % tr == 0
    G = None
    for p in params:
        G = p.shape[0] if p.ndim == 3 else p.shape[1] // gw
    cw = G * gw
    n_r, n_p, n_or, n_oa = len(rows), len(params), len(out_rows), len(out_acc)

    def pick(ref, g):
        return ref[g] if len(ref.shape) == 3 else ref[:, g * gw:(g + 1) * gw]

    def body(*refs):
        r_refs = refs[:n_r]
        p_refs = refs[n_r:n_r + n_p]
        or_refs = refs[n_r + n_p:n_r + n_p + n_or]
        oa_refs = refs[n_r + n_p + n_or:]

        @pl.when(pl.program_id(0) == 0)
        def _():
            for o in oa_refs:
                o[...] = jnp.zeros_like(o)

        for g in range(G):
            outs = fn(*[pick(r, g) for r in r_refs], *[pick(p, g) for p in p_refs])
            for o_ref, o in zip(or_refs, outs[:n_or]):
                o_ref[:, g * gw:(g + 1) * gw] = o.astype(o_ref.dtype)
            for o_ref, o in zip(oa_refs, outs[n_or:]):
                if len(o_ref.shape) == 3:
                    o_ref[g] += o
                else:
                    o_ref[:, g * gw:(g + 1) * gw] += o

    def whole(shape):
        nd = len(shape)
        return pl.BlockSpec(shape, lambda i, _n=nd: (0,) * _n)

    for _, c0 in rows:
        assert c0 % cw == 0
    outs, side_outs = _call(
        body, (*[r[0] for r in rows], *params), name=name, grid=(L // tr,),
        in_specs=[pl.BlockSpec((tr, cw), lambda i, _c=c0 // cw: (i, _c)) for _, c0 in rows] + [whole(p.shape) for p in params],
        out_specs=[pl.BlockSpec((tr, cw), lambda i: (i, 0)) for _ in out_rows] + [whole(s) for s in out_acc],
        out_shape=[jax.ShapeDtypeStruct((L, cw), d) for d in out_rows] + [jax.ShapeDtypeStruct(s, F32) for s in out_acc],
        sem=("arbitrary",), side=side, into=into)
    return outs if side is None else (outs, side_outs)


def _dsilu(p):
    s = _sigmoid(p)
    return s + p * s * (1.0 - s)


CONV_RC, CONV_CC = 32, 512


def _past_window(x_ref, head, r0, k, cs):
    if r0 == 0:
        return head[HALO - 3 + k:HALO - 3 + k + CONV_RC, cs]
    return x_ref[r0 - 3 + k:r0 - 3 + k + CONV_RC, cs]


def _conv_fwd(x, c0, C, w, b, *, silu, name, tr=512, cb=1024, side=None):
    L = x.shape[0]
    tr = min(tr, L)
    nb, hb = L // tr, tr // HALO
    assert L % tr == 0 and C % cb == 0 and c0 % cb == 0 and tr % CONV_RC == 0 and cb % CONV_CC == 0
    n_out = 2 if silu else 1

    def body(x_ref, h_ref, w_ref, b_ref, *rest):
        o_refs, head = rest[:n_out], rest[n_out]
        head[0:HALO, :] = h_ref[...]

        @pl.when(pl.program_id(0) == 0)
        def _():
            head[0:HALO, :] = jnp.zeros((HALO, cb), F32)

        head[HALO:, :] = x_ref[0:CONV_RC, :]
        for cc in range(cb // CONV_CC):
            cs = slice(cc * CONV_CC, (cc + 1) * CONV_CC)
            wv, bv = w_ref[:, cs], b_ref[:, cs]
            for r0 in range(0, tr, CONV_RC):
                y = bv
                for k in range(CONV_W):
                    y = y + wv[k:k + 1, :] * _past_window(x_ref, head, r0, k, cs)
                o_refs[0][r0:r0 + CONV_RC, cs] = y
                if silu:
                    o_refs[1][r0:r0 + CONV_RC, cs] = _silu(y)

    outs, side_outs = _call(
        body, (x, x, w, b), name=name, grid=(nb, C // cb),
        in_specs=[
            pl.BlockSpec((tr, cb), lambda i, j: (i, c0 // cb + j)),
            pl.BlockSpec((HALO, cb), lambda i, j: (jnp.maximum(i * hb - 1, 0), c0 // cb + j)),
            pl.BlockSpec((CONV_W, cb), lambda i, j: (0, j)),
            pl.BlockSpec((1, cb), lambda i, j: (0, j)),
        ],
        out_specs=[pl.BlockSpec((tr, cb), lambda i, j: (i, j)) for _ in range(n_out)],
        out_shape=[jax.ShapeDtypeStruct((L, C), F32) for _ in range(n_out)],
        scratch_shapes=[pltpu.VMEM((HALO + CONV_RC, cb), F32)], sem=("parallel", "parallel"), side=side)
    return outs if side is None else (outs, side_outs)


def _fold8(v):
    return (v[0:8] + v[8:16]) + (v[16:24] + v[24:32])


def _conv_bwd(dact, dc0, pre, x, xc0, C, w, *, silu, name, tr=512, cb=1024, side=None, into=None):
    L = x.shape[0]
    tr = min(tr, L)
    nb, hb = L // tr, tr // HALO
    last_h = L // HALO - 1
    assert L % tr == 0 and C % cb == 0 and tr % CONV_RC == 0 and cb % CONV_CC == 0

    def body(*refs):
        if silu:
            d_ref, dh_ref, p_ref, ph_ref, x_ref, xh_ref, w_ref, dx_ref, dwb_ref, dp, head = refs
        else:
            d_ref, dh_ref, x_ref, xh_ref, w_ref, dx_ref, dwb_ref, dp, head = refs
        i = pl.program_id(1)
        dp[tr:, :] = dh_ref[...] * _dsilu(ph_ref[...]) if silu else dh_ref[...]

        @pl.when(i == nb - 1)
        def _():
            dp[tr:, :] = jnp.zeros((HALO, cb), F32)

        head[0:HALO, :] = xh_ref[...]

        @pl.when(i == 0)
        def _():
            head[0:HALO, :] = jnp.zeros((HALO, cb), F32)
            dwb_ref[...] = jnp.zeros_like(dwb_ref)

        head[HALO:, :] = x_ref[0:CONV_RC, :]
        for cc in range(cb // CONV_CC):
            cs = slice(cc * CONV_CC, (cc + 1) * CONV_CC)
            for r0 in range(0, tr, CONV_RC):
                rs = slice(r0, r0 + CONV_RC)
                dp[rs, cs] = d_ref[rs, cs] * _dsilu(p_ref[rs, cs]) if silu else d_ref[rs, cs]
        for cc in range(cb // CONV_CC):
            cs = slice(cc * CONV_CC, (cc + 1) * CONV_CC)
            wv = w_ref[:, cs]
            acc = [jnp.zeros((8, CONV_CC), F32) for _ in range(CONV_W + 1)]
            for r0 in range(0, tr, CONV_RC):
                dm = dp[r0:r0 + CONV_RC, cs]
                dx = jnp.zeros((CONV_RC, CONV_CC), F32)
                for k in range(CONV_W):
                    dx = dx + wv[k:k + 1, :] * dp[r0 + 3 - k:r0 + 3 - k + CONV_RC, cs]
                    acc[k] = acc[k] + _fold8(dm * _past_window(x_ref, head, r0, k, cs))
                acc[CONV_W] = acc[CONV_W] + _fold8(dm)
                dx_ref[r0:r0 + CONV_RC, cs] = dx.astype(dx_ref.dtype)
            for k in range(CONV_W + 1):
                dwb_ref[k:k + 1, cs] += _colsum(acc[k])

    def main(c):
        return pl.BlockSpec((tr, cb), lambda j, i: (i, c // cb + j))

    def nxt(c):
        return pl.BlockSpec((HALO, cb), lambda j, i: (jnp.minimum((i + 1) * hb, last_h), c // cb + j))

    in_specs = [main(dc0), nxt(dc0)]
    args = [dact, dact]
    if silu:
        in_specs += [main(0), nxt(0)]
        args += [pre, pre]
    in_specs += [main(xc0), pl.BlockSpec((HALO, cb), lambda j, i: (jnp.maximum(i * hb - 1, 0), xc0 // cb + j)),
                 pl.BlockSpec((CONV_W, cb), lambda j, i: (0, j))]
    args += [x, x, w]
    outs, side_outs = _call(
        body, args, name=name, grid=(C // cb, nb), in_specs=in_specs,
        out_specs=[pl.BlockSpec((tr, cb), lambda j, i: (i, j)), pl.BlockSpec((8, cb), lambda j, i: (0, j))],
        out_shape=[jax.ShapeDtypeStruct((L, C), BF16), jax.ShapeDtypeStruct((8, C), F32)],
        scratch_shapes=[pltpu.VMEM((tr + HALO, cb), F32), pltpu.VMEM((HALO + CONV_RC, cb), F32)],
        sem=("parallel", "arbitrary"), side=side, into=into)
    return outs if side is None else (outs, side_outs)


def _ssd_f1(dtraw, dtb, alog):
    q = dtraw.shape[0]
    dt = _softplus(dtraw + dtb)
    adt = dt * (-jnp.exp(alog))
    tril = (lax.broadcasted_iota(jnp.int32, (q, q), 0) >= lax.broadcasted_iota(jnp.int32, (q, q), 1)).astype(F32)
    acs = lax.dot_general(tril, adt, (_NN, ((), ())), precision=lax.Precision.HIGHEST, preferred_element_type=F32)
    return dt, acs


def _ssd_group(g, x, bm, cm, dt, acs, drow, hp):
    q = x.shape[0]
    lane = lax.broadcasted_iota(jnp.int32, (1, 128), 1)
    sub = lax.broadcasted_iota(jnp.int32, (128, 1), 0)
    head_of = lax.broadcasted_iota(jnp.int32, (1, GROUP_W), 1) // HEAD_DIM
    is_last = (lax.broadcasted_iota(jnp.int32, (q, 1), 0) == q - 1).astype(F32)
    causal = lax.broadcasted_iota(jnp.int32, (q, q), 0) >= lax.broadcasted_iota(jnp.int32, (q, q), 1)
    acs_end = jnp.sum(acs * is_last, axis=0, keepdims=True)
    acs_t = acs.T
    dt_exp = jnp.zeros((q, GROUP_W), F32)
    acs_exp = jnp.zeros((q, GROUP_W), F32)
    end_exp = jnp.zeros((1, GROUP_W), F32)
    d_exp = jnp.zeros((1, GROUP_W), F32)
    heads = []
    for k in range(HEADS_PER_GROUP):
        h = HEADS_PER_GROUP * g + k
        oh = (lane == h).astype(F32)
        mk = (head_of == k).astype(F32)
        acs_col = jnp.sum(acs * oh, axis=1, keepdims=True)
        acs_row = jnp.sum(acs_t * (sub == h).astype(F32), axis=0, keepdims=True)
        dt_exp = dt_exp + jnp.sum(dt * oh, axis=1, keepdims=True) * mk
        acs_exp = acs_exp + acs_col * mk
        end_exp = end_exp + jnp.sum(acs_end * oh, axis=1, keepdims=True) * mk
        d_exp = d_exp + jnp.sum(drow * oh, axis=1, keepdims=True) * mk
        heads.append((acs_col, acs_row, mk))
    xdt = x * dt_exp
    states = _dot(bm, xdt * jnp.exp(end_exp - acs_exp), _TN)
    y = _dot(cm, hp, _NN) * jnp.exp(acs_exp) + x * d_exp
    scores = _dot(cm, bm, _NT)
    for acs_col, acs_row, mk in heads:
        dec = jnp.exp(jnp.where(causal, acs_col - acs_row, -jnp.inf))
        y = y + _dot(scores * dec, xdt * mk, _NN)
    return y, hp * jnp.exp(end_exp) + states


def _ssd_fwd(act, dtraw, dtb, alog, drow, *, name, side=None):
    L = act.shape[0]
    q = min(SSD_CHUNK, L)
    nc = L // q
    d_ssd = N_GROUPS * GROUP_W

    def body(act_ref, dt_ref, dtb_ref, alog_ref, drow_ref, y_ref, hst_ref, h):
        @pl.when(pl.program_id(0) == 0)
        def _():
            h[...] = jnp.zeros_like(h)

        dt, acs = _ssd_f1(dt_ref[...], dtb_ref[...], alog_ref[...])
        drow = drow_ref[...]
        for g in range(N_GROUPS):
            hp = h[g]
            hst_ref[0, g] = hp
            y, hn = _ssd_group(g, act_ref[:, g * GROUP_W:(g + 1) * GROUP_W],
                               act_ref[:, d_ssd + g * STATE:d_ssd + (g + 1) * STATE],
                               act_ref[:, d_ssd + (N_GROUPS + g) * STATE:d_ssd + (N_GROUPS + g + 1) * STATE],
                               dt, acs, drow, hp)
            y_ref[:, g * GROUP_W:(g + 1) * GROUP_W] = y
            h[g] = hn

    row = pl.BlockSpec((1, 128), lambda i: (0, 0))
    outs, side_outs = _call(
        body, (act, dtraw, dtb, alog, drow), name=name, grid=(nc,),
        in_specs=[pl.BlockSpec((q, act.shape[1]), lambda i: (i, 0)), pl.BlockSpec((q, 128), lambda i: (i, 0)), row, row, row],
        out_specs=[pl.BlockSpec((q, d_ssd), lambda i: (i, 0)),
                   pl.BlockSpec((1, N_GROUPS, STATE, GROUP_W), lambda i: (i, 0, 0, 0))],
        out_shape=[jax.ShapeDtypeStruct((L, d_ssd), F32), jax.ShapeDtypeStruct((nc, N_GROUPS, STATE, GROUP_W), F32)],
        scratch_shapes=[pltpu.VMEM((N_GROUPS, STATE, GROUP_W), F32)], sem=("arbitrary",), side=side)
    return outs if side is None else (outs, side_outs)


def _ssd_bwd(act, dtraw, dy, hst, dtb, alog, drow, *, name, side=None):
    L = act.shape[0]
    q = min(SSD_CHUNK, L)
    nc = L // q
    d_ssd = N_GROUPS * GROUP_W

    def body(act_ref, dt_ref, dy_ref, hst_ref, dtb_ref, alog_ref, drow_ref, dact_ref, ddt_ref, dpar_ref, dh):
        @pl.when(pl.program_id(0) == 0)
        def _():
            dh[...] = jnp.zeros_like(dh)
            dpar_ref[...] = jnp.zeros_like(dpar_ref)

        (dt, acs), f1_vjp = jax.vjp(_ssd_f1, dt_ref[...], dtb_ref[...], alog_ref[...])
        drow = drow_ref[...]
        ddt = jnp.zeros_like(dt)
        dacs = jnp.zeros_like(acs)
        ddrow = jnp.zeros_like(drow)
        for g in range(N_GROUPS):
            xs = slice(g * GROUP_W, (g + 1) * GROUP_W)
            bs = slice(d_ssd + g * STATE, d_ssd + (g + 1) * STATE)
            cs = slice(d_ssd + (N_GROUPS + g) * STATE, d_ssd + (N_GROUPS + g + 1) * STATE)
            _, f2_vjp = jax.vjp(functools.partial(_ssd_group, g), act_ref[:, xs], act_ref[:, bs], act_ref[:, cs],
                                dt, acs, drow, hst_ref[0, g])
            dx, dbm, dcm, ddt_g, dacs_g, ddrow_g, dhp = f2_vjp((dy_ref[:, xs], dh[g]))
            dact_ref[:, xs] = dx
            dact_ref[:, bs] = dbm
            dact_ref[:, cs] = dcm
            dh[g] = dhp
            ddt, dacs, ddrow = ddt + ddt_g, dacs + dacs_g, ddrow + ddrow_g
        ddtraw, ddtb, dalog = f1_vjp((ddt, dacs))
        ddt_ref[...] = ddtraw
        dpar_ref[0:1, :] += ddtb
        dpar_ref[1:2, :] += dalog
        dpar_ref[2:3, :] += ddrow

    row = pl.BlockSpec((1, 128), lambda i: (0, 0))
    rev = lambda i: (nc - 1 - i, 0)
    outs, side_outs = _call(
        body, (act, dtraw, dy, hst, dtb, alog, drow), name=name, grid=(nc,),
        in_specs=[pl.BlockSpec((q, act.shape[1]), rev), pl.BlockSpec((q, 128), rev), pl.BlockSpec((q, d_ssd), rev),
                  pl.BlockSpec((1, N_GROUPS, STATE, GROUP_W), lambda i: (nc - 1 - i, 0, 0, 0)), row, row, row],
        out_specs=[pl.BlockSpec((q, act.shape[1]), rev), pl.BlockSpec((q, 128), rev), pl.BlockSpec((8, 128), lambda i: (0, 0))],
        out_shape=[jax.ShapeDtypeStruct(act.shape, F32), jax.ShapeDtypeStruct((L, 128), F32), jax.ShapeDtypeStruct((8, 128), F32)],
        scratch_shapes=[pltpu.VMEM((N_GROUPS, STATE, GROUP_W), F32)], sem=("arbitrary",), side=side)
    return outs if side is None else (outs, side_outs)


def _gate_head(xl, wa, ba, wx, bx, lam):
    r = _sigmoid(_dot(xl, wa, _NN) + ba)
    i = _sigmoid(_dot(xl, wx, _NN) + bx)
    log_a = -LRU_C * r * _softplus(-lam)
    return jnp.exp(log_a), jnp.sqrt(-_expm1(2.0 * log_a)) * (i * xl)


def _gate_head_bwd(xl, da, du, wa, ba, wx, bx, lam):
    _, vjp = jax.vjp(_gate_head, xl, wa, ba, wx, bx, lam)
    dxl, dwa, dba, dwx, dbx, dlam = vjp((da, du))
    return dxl, dwa, dba, dwx, dbx, dlam


def _scan_tile(a, b, rows, reverse):
    for d in (1, 2, 4):
        if reverse:
            keep = rows < 8 - d
            a_sh, b_sh = pltpu.roll(a, 8 - d, 0), pltpu.roll(b, 8 - d, 0)
        else:
            keep = rows >= d
            a_sh, b_sh = pltpu.roll(a, d, 0), pltpu.roll(b, d, 0)
        b = b + a * jnp.where(keep, b_sh, 0.0)
        a = a * jnp.where(keep, a_sh, 1.0)
    return a, b


def _lru_scan_fwd(a, u, *, name, tr=512, cb=1024):
    L, C = a.shape
    tr, cb = min(tr, L), min(cb, C)

    def body(a_ref, u_ref, h_ref, hp_ref, carry):
        @pl.when(pl.program_id(1) == 0)
        def _():
            carry[...] = jnp.zeros_like(carry)

        rows = lax.broadcasted_iota(jnp.int32, (8, cb), 0)

        def tile(t, hc):
            r0 = pl.multiple_of(t * 8, 8)
            pa, hb = _scan_tile(a_ref[pl.ds(r0, 8), :], u_ref[pl.ds(r0, 8), :], rows, False)
            h = hb + pa * hc
            h_ref[pl.ds(r0, 8), :] = h
            hp_ref[pl.ds(r0, 8), :] = jnp.where(rows >= 1, pltpu.roll(h, 1, 0), hc)
            return h[7:8, :]

        carry[...] = lax.fori_loop(0, tr // 8, tile, carry[...], unroll=4)

    blk = pl.BlockSpec((tr, cb), lambda j, i: (i, j))
    return pl.pallas_call(
        body, name=name, grid=(C // cb, L // tr),
        in_specs=[blk, blk], out_specs=[blk, blk],
        out_shape=[jax.ShapeDtypeStruct((L, C), F32), jax.ShapeDtypeStruct((L, C), F32)],
        scratch_shapes=[pltpu.VMEM((1, cb), F32)],
        compiler_params=_params(("parallel", "arbitrary")),
    )(a, u)


def _lru_scan_bwd(a, hprev, dh, *, name, tr=512, cb=1024):
    L, C = a.shape
    tr, cb = min(tr, L), min(cb, C)
    nb = L // tr

    def body(a_ref, hp_ref, dh_ref, da_ref, du_ref, carry):
        @pl.when(pl.program_id(1) == 0)
        def _():
            carry[...] = jnp.zeros_like(carry)

        rows = lax.broadcasted_iota(jnp.int32, (8, cb), 0)

        def tile(t, gc):
            r0 = pl.multiple_of((tr // 8 - 1 - t) * 8, 8)
            av, dv = a_ref[pl.ds(r0, 8), :], dh_ref[pl.ds(r0, 8), :]
            pa, gb = _scan_tile(av, av * dv, rows, True)
            big = gb + pa * gc
            g = dv + jnp.where(rows < 7, pltpu.roll(big, 7, 0), gc)
            du_ref[pl.ds(r0, 8), :] = g
            da_ref[pl.ds(r0, 8), :] = g * hp_ref[pl.ds(r0, 8), :]
            return big[0:1, :]

        carry[...] = lax.fori_loop(0, tr // 8, tile, carry[...], unroll=4)

    blk = pl.BlockSpec((tr, cb), lambda j, i: (nb - 1 - i, j))
    return pl.pallas_call(
        body, name=name, grid=(C // cb, nb),
        in_specs=[blk, blk, blk], out_specs=[blk, blk],
        out_shape=[jax.ShapeDtypeStruct((L, C), F32), jax.ShapeDtypeStruct((L, C), F32)],
        scratch_shapes=[pltpu.VMEM((1, cb), F32)],
        compiler_params=_params(("parallel", "arbitrary")),
    )(a, hprev, dh)


def _ssd_gate(y, z, n):
    v = y * _silu(z)
    return v * lax.rsqrt(jnp.mean(v * v, axis=-1, keepdims=True) + EPS) * n


def _ssd_gate_bwd(y, z, dy, n):
    _, vjp = jax.vjp(_ssd_gate, y, z, n)
    return vjp(dy)


def _lru_out(hl, gate, n):
    return _rms(hl * _gelu(gate), n)


def _lru_out_bwd(hl, gate, dy, n):
    _, vjp = jax.vjp(_lru_out, hl, gate, n)
    return vjp(dy)


def _mid(x, mix, pm, pmlp):
    x1 = x + _rms(mix, pm)
    return x1, _rms(x1, pmlp)


def _mid_bwd(x, mix, dx1p, dh2, pm, pmlp):
    _, vjp = jax.vjp(_mid, x, mix, pm, pmlp)
    dx, dmix, dpm, dpmlp = vjp((dx1p, dh2))
    return dmix, dx, dpm, dpmlp


def _loss_bwd(hm2, x1, tgt, g):
    def lossf(hm2, x1, g):
        e = x1 + _rms(hm2, g) - tgt
        return 0.5 * jnp.sum(jnp.mean(e * e, axis=-1, keepdims=True), axis=0, keepdims=True)

    val, vjp = jax.vjp(lossf, hm2, x1, g)
    dhm2, dx1, dg = vjp(jnp.ones((1, 1), F32))
    return dhm2, dx1, dg, val * jnp.ones((1, 128), F32)


def _in_bwd(x, dh_a, dh_b, dx1, g):
    _, vjp = jax.vjp(_rms, x, g)
    dx, dg = vjp(dh_a + dh_b)
    return dx + dx1, dg


def _adamw(w, g, m, v):
    m = ADAM_B1 * m + (1.0 - ADAM_B1) * g
    v = ADAM_B2 * v + (1.0 - ADAM_B2) * (g * g)
    m_hat = m / (1.0 - ADAM_B1 ** ADAM_STEP)
    v_hat = v / (1.0 - ADAM_B2 ** ADAM_STEP)
    return -ADAM_LR * (m_hat / (jnp.sqrt(v_hat) + ADAM_EPS) + ADAM_WD * w), m, v


class _LocalPlan:
    def __init__(self, p):
        self.p, self.large = p, {}

    def weight(self, name):
        return self.p[name]

    def side(self, kernel_name):
        return None

    def done(self, kernel_name, side_outs):
        pass

    def grad(self, name, g):
        self.large[name] = g

    def small(self, grads):
        pass


def _local_step(x, tgt, p, plan):
    L, D = x.shape

    def carry(fn, *args, name, **kw):
        side = plan.side(name)
        if side is None:
            return fn(*args, name=name, **kw)
        outs, side_outs = fn(*args, name=name, side=side, **kw)
        plan.done(name, side_outs)
        return outs

    d_ssd, d_xbc, d_lru, d_mix, d_ff = 2048, 4096, 2048, 4096, 8192
    n_main = d_ssd + d_xbc + 2 * d_lru
    c_xbc, c_gate, c_xl = d_ssd, d_ssd + d_xbc, d_ssd + d_xbc + d_lru
    TR, SUB = 256, 32
    mm = dict(tm=1024, tn=1024, tk=2048)

    (h,) = _rowwise(lambda xt, g: (_rms(xt, g),), [x], [p["pre_mix_norm"]], [(D, BF16)], [], name="f_prenorm", tr=TR, sub=SUB)
    w_main, w_dt = plan.weight("w_main"), plan.weight("w_dt")
    (proj,) = carry(_matmul, h, w_main, mode="nn", m=L, n=n_main, k=D, out_dtypes=[F32], name="f_inproj", **mm)
    (dtraw,) = _matmul(h, w_dt, mode="nn", m=L, n=128, k=D, out_dtypes=[F32], name="f_dtproj", **mm)
    pre, act = carry(_conv_fwd, proj, c_xbc, d_xbc, p["ssd_conv_w"], p["ssd_conv_b"], silu=True, name="f_ssdconv")
    (xl,) = carry(_conv_fwd, proj, c_xl, d_lru, p["lru_conv_w"], p["lru_conv_b"], silu=False, name="f_lruconv")
    yraw, hst = carry(_ssd_fwd, act, dtraw, p["dtb"], p["alog"], p["drow"], name="f_ssd")
    (ycat,) = _grouped(lambda y, z, n: (_ssd_gate(y, z, n),), [yraw, (proj, 0)], [p["ssd_norm"]], [BF16], [],
                       gw=GROUP_W, name="f_ssdgate", tr=TR, into={0: (None, d_mix, 0)})
    gate_p = [p["lru_w_a"], p["lru_b_a"], p["lru_w_x"], p["lru_b_x"], p["lru_lambda"]]
    a, u = carry(_grouped, _gate_head, [xl], gate_p, [F32, F32], [], gw=LRU_BLOCK, name="f_lrugates", tr=TR)
    hl, hprev = _lru_scan_fwd(a, u, name="f_lruscan")
    (ycat,) = _rowwise(lambda ht, gt, n: (_lru_out(ht, gt, n),), [hl, (proj, c_gate, d_lru)], [p["lru_norm"]],
                       [(d_lru, BF16)], [], name="f_lruout", tr=TR, sub=SUB, into={0: (ycat, d_mix, d_ssd // d_lru)})
    w_out = plan.weight("w_out")
    (mix,) = carry(_matmul, ycat, w_out, mode="nn", m=L, n=D, k=d_mix, out_dtypes=[F32], name="f_outproj", **mm)
    x1, h2 = _rowwise(_mid, [x, mix], [p["post_mix_norm"], p["pre_mlp_norm"]], [(D, F32), (D, BF16)], [],
                      name="f_mid", tr=TR, sub=SUB)
    nb_mi = (d_ff // 4) // mm["tn"]
    w_mi = plan.weight("w_mi")
    hm, act2 = carry(_matmul, h2, w_mi, mode="nn", m=L, n=d_ff, k=D, out_dtypes=[BF16, BF16], name="f_mlpin",
                     b_spec=pl.BlockSpec((None, mm["tk"], mm["tn"]), lambda i, j, kk: (j // nb_mi, kk, j % nb_mi)),
                     epilogue=lambda r: (r, jnp.square(jnp.maximum(r, 0.0))), **mm)
    w_mo = plan.weight("w_mo")
    (hm2,) = _matmul(act2, w_mo, mode="nn", m=L, n=D, k=d_ff, out_dtypes=[F32], name="f_mlpout", **mm)

    dhm2, dx1p, d_post_mlp, loss = _rowwise(_loss_bwd, [hm2, x1, tgt], [p["post_mlp_norm"]], [(D, BF16), (D, F32)],
                                            [(1, D), (1, 128)], name="b_loss", tr=TR, sub=SUB)
    (dhm,) = _matmul(dhm2, w_mo, mode="nt", m=L, n=d_ff, k=D, out_dtypes=[BF16], name="b_mlpout_dx", extras=[hm],
                     epilogue=lambda r, hmv: (r * (2.0 * jnp.maximum(hmv.astype(F32), 0.0)),), **mm)
    (dw_mo,) = _matmul(act2, dhm2, mode="tn", m=d_ff, n=D, k=L, out_dtypes=[BF16], name="b_mlpout_dw", **mm)
    plan.grad("w_mlp_out", dw_mo.reshape(4, -1, D))
    kb_mi = (d_ff // 4) // mm["tk"]
    (dh2,) = carry(_matmul, dhm, w_mi, mode="nt", m=L, n=D, k=d_ff, out_dtypes=[F32], name="b_mlpin_dx",
                   b_spec=pl.BlockSpec((None, mm["tn"], mm["tk"]), lambda i, j, kk: (kk // kb_mi, j, kk % kb_mi)), **mm)
    (dw_mi,) = carry(_matmul, h2, dhm, mode="tn", m=D, n=d_ff, k=L, out_dtypes=[BF16], name="b_mlpin_dw",
                     out_specs=[pl.BlockSpec((None, mm["tm"], mm["tn"]), lambda i, j, kk: (j // nb_mi, i, j % nb_mi))],
                     out_shapes=[jax.ShapeDtypeStruct((4, D, d_ff // 4), BF16)], **mm)
    plan.grad("w_mlp_in", dw_mi)
    dmix, dx1, d_post_mix, d_pre_mlp = carry(_rowwise, _mid_bwd, [x, mix, dx1p, dh2], [p["post_mix_norm"], p["pre_mlp_norm"]],
                                             [(D, BF16), (D, F32)], [(1, D), (1, D)], name="b_mid", tr=TR, sub=SUB)
    (dw_out,) = carry(_matmul, ycat, dmix, mode="tn", m=d_mix, n=D, k=L, out_dtypes=[BF16], name="b_outproj_dw", **mm)
    plan.grad("w_out", dw_out.reshape(4, -1, D))
    (dycat,) = carry(_matmul, dmix, w_out, mode="nt", m=L, n=d_mix, k=D, out_dtypes=[F32], name="b_outproj_dx", **mm)
    dhl, dproj, d_lru_norm = _rowwise(_lru_out_bwd, [hl, (proj, c_gate, d_lru), (dycat, d_ssd, d_lru)], [p["lru_norm"]],
                                      [(d_lru, F32), (d_lru, BF16)], [(1, d_lru)], name="b_lruout", tr=TR, sub=SUB,
                                      into={1: (None, n_main, c_gate // d_lru)})
    da, du = _lru_scan_bwd(a, hprev, dhl, name="b_lruscan")
    dxl, d_wa, d_ba, d_wx, d_bx, d_lam = _grouped(
        _gate_head_bwd, [xl, da, du], gate_p, [F32],
        [(LRU_HEADS, LRU_BLOCK, LRU_BLOCK), (1, d_lru), (LRU_HEADS, LRU_BLOCK, LRU_BLOCK), (1, d_lru), (1, d_lru)],
        gw=LRU_BLOCK, name="b_lrugates", tr=TR)
    CB = 1024
    dproj, dwb_lru = _conv_bwd(dxl, 0, None, proj, c_xl, d_lru, p["lru_conv_w"], silu=False, name="b_lruconv", cb=CB,
                               into={0: (dproj, n_main, c_xl // CB)})
    dyraw, dproj, d_ssd_norm = _grouped(_ssd_gate_bwd, [yraw, (proj, 0), (dycat, 0)], [p["ssd_norm"]], [F32, BF16], [(1, d_ssd)],
                                        gw=GROUP_W, name="b_ssdgate", tr=TR, into={1: (dproj, n_main, 0)})
    dact, ddtraw, dpar = carry(_ssd_bwd, act, dtraw, dyraw, hst, p["dtb"], p["alog"], p["drow"], name="b_ssd")
    dproj, dwb_ssd = carry(_conv_bwd, dact, 0, pre, proj, c_xbc, d_xbc, p["ssd_conv_w"], silu=True, name="b_ssdconv", cb=CB,
                           into={0: (dproj, n_main, c_xbc // CB)})
    early = dict(loss=loss, ssd_conv=dwb_ssd, ssd_par=dpar, ssd_norm=d_ssd_norm, lru_conv=dwb_lru, lru_w_a=d_wa, lru_b_a=d_ba,
                 lru_w_x=d_wx, lru_b_x=d_bx, lru_lambda=d_lam, lru_norm=d_lru_norm, post_mix_norm=d_post_mix,
                 pre_mlp_norm=d_pre_mlp, post_mlp_norm=d_post_mlp)
    plan.small(early)
    (dw_main,) = carry(_matmul, h, dproj, mode="tn", m=D, n=n_main, k=L, out_dtypes=[BF16], name="b_inproj_dw", **mm)
    (dw_dt,) = _matmul(h, ddtraw, mode="tn", m=D, n=128, k=L, out_dtypes=[BF16], name="b_dtproj_dw", **mm)
    plan.grad("w_in", (dw_main, dw_dt))
    (dh_a,) = carry(_matmul, dproj, w_main, mode="nt", m=L, n=D, k=n_main, out_dtypes=[F32], name="b_inproj_dx", **mm)
    (dh_b,) = _matmul(ddtraw, w_dt, mode="nt", m=L, n=D, k=128, out_dtypes=[F32], name="b_dtproj_dx", **mm)
    grad_x, d_pre_mix = carry(_rowwise, _in_bwd, [x, dh_a, dh_b, dx1], [p["pre_mix_norm"]], [(D, F32)], [(1, D)],
                              name="b_prenorm", tr=TR, sub=SUB)

    return grad_x, dict(early, pre_mix_norm=d_pre_mix)


def _place():
    return lax.axis_index("x"), lax.axis_index("y"), lax.axis_index("c")


def _other_chips(x, y):
    return [(1 - x, y), (x, 1 - y), (1 - x, 1 - y)]


def _allgather8(blk, *, name):
    r, n = blk.shape

    def body(x_ref, out_ref, send_sems, recv_sems, local_sem):
        x, y, c = _place()
        me, sibling = (x, y, c), (x, y, 1 - c)
        chips = _other_chips(x, y)

        def rows(px, py, pc):
            return out_ref.at[pl.ds((4 * px + 2 * py + pc) * r, r), :]

        def copy(k, block, to, src=None):
            return pltpu.make_async_remote_copy(
                src_ref=rows(*block) if src is None else src, dst_ref=rows(*block),
                send_sem=send_sems.at[k], recv_sem=recv_sems.at[k], device_id=to, device_id_type=MESH)

        mine = pltpu.make_async_copy(x_ref, rows(*me), local_sem)
        mine.start()
        first = [copy(0, me, sibling, src=x_ref)]
        first += [copy(1 + k, me, (*chip, c), src=x_ref) for k, chip in enumerate(chips)]
        for cp in first:
            cp.start()
        passed = [copy(4 + k, (*chip, c), sibling) for k, chip in enumerate(chips)]
        for k, chip in enumerate(chips):
            copy(1 + k, (*chip, c), me).wait_recv()
            passed[k].start()
        copy(0, sibling, me).wait_recv()
        for k, chip in enumerate(chips):
            copy(4 + k, (*chip, 1 - c), me).wait_recv()
        for cp in first + passed:
            cp.wait_send()
        mine.wait()

    return pl.pallas_call(
        body, name=name,
        out_shape=jax.ShapeDtypeStruct((8 * r, n), blk.dtype),
        in_specs=[pl.BlockSpec(memory_space=pltpu.VMEM)], out_specs=pl.BlockSpec(memory_space=pltpu.VMEM),
        scratch_shapes=[pltpu.SemaphoreType.DMA((7,)), pltpu.SemaphoreType.DMA((7,)), pltpu.SemaphoreType.DMA],
        compiler_params=pltpu.CompilerParams(vmem_limit_bytes=VMEM_LIMIT),
    )(blk)


def _sum8(g, *, name):
    _, r, n = g.shape
    tr = max(t for t in range(8, min(r, 512) + 1, 8) if r % t == 0)

    def body(g_ref, o_ref):
        s = g_ref[0]
        for k in range(1, 8):
            s = s + g_ref[k]
        o_ref[...] = s

    return pl.pallas_call(
        body, name=name, grid=(r // tr,),
        in_specs=[pl.BlockSpec((8, tr, n), lambda i: (0, i, 0))], out_specs=pl.BlockSpec((tr, n), lambda i: (i, 0)),
        out_shape=jax.ShapeDtypeStruct((r, n), g.dtype), compiler_params=_params(("parallel",)),
    )(g)


def _blocks(fn, ins, outs, *, grid, name, prefetch=None, aliases=None):
    n_in = len(ins)

    def body(*refs):
        if prefetch is not None:
            refs = refs[1:]
        res = fn(*[r[...] for r in refs[:n_in]])
        for o_ref, o in zip(refs[n_in:], res):
            o_ref[...] = o.astype(o_ref.dtype)

    in_specs = [pl.BlockSpec(b, m) for _, b, m in ins]
    out_specs = [pl.BlockSpec(b, m) for _, b, m in outs]
    kw = dict(name=name, out_shape=[s for s, _, _ in outs], input_output_aliases=aliases or {},
              compiler_params=_params(("arbitrary",) * len(grid)))
    arrs = [a for a, _, _ in ins]
    if prefetch is None:
        return pl.pallas_call(body, grid=grid, in_specs=in_specs, out_specs=out_specs, **kw)(*arrs)
    spec = pltpu.PrefetchScalarGridSpec(num_scalar_prefetch=1, grid=grid, in_specs=in_specs, out_specs=out_specs)
    return pl.pallas_call(body, grid_spec=spec, **kw)(prefetch, *arrs)


def _gather_tree(slot, *, name):
    def body(i_ref, g_ref, send_sems, recv_sems):
        x, y, c = _place()
        j, jx, jy, jd = 2 * x + y, 2 * (1 - x) + y, 2 * x + (1 - y), 2 * (1 - x) + (1 - y)
        xn, yn, sibling = (1 - x, y, c), (x, 1 - y, c), (x, y, 1 - c)
        rp = i_ref.shape[1] // 4

        def piece(ref, chip, hc, q):
            return ref.at[chip, pl.ds((2 * hc + q) * rp, rp), :]

        def cp(k, src, dst, to):
            return pltpu.make_async_remote_copy(src_ref=src, dst_ref=dst, send_sem=send_sems.at[k], recv_sem=recv_sems.at[k],
                                                device_id=to, device_id_type=MESH)

        def landed(k, chip, q):
            return cp(k, piece(g_ref, chip, c, q), piece(g_ref, chip, c, q), xn)

        sends = []

        def go(d):
            d.start()
            sends.append(d)

        go(cp(0, piece(i_ref, j, c, 0), piece(g_ref, j, c, 0), xn))
        go(cp(2, piece(i_ref, j, c, 1), piece(g_ref, j, c, 1), yn))
        go(cp(1, piece(i_ref, j, c, 1), piece(g_ref, j, c, 1), xn))
        go(cp(3, piece(i_ref, j, c, 0), piece(g_ref, j, c, 0), yn))
        arrivals = [(0, jx, 0, 4, yn), (2, jy, 1, 5, xn), (1, jx, 1, None, None), (3, jy, 0, None, None),
                    (4, jd, 0, None, None), (5, jd, 1, None, None)]
        for n_arr, (k, chip, q, k_fwd, to) in enumerate(arrivals):
            landed(k, chip, q).wait_recv()
            if k_fwd is not None:
                go(cp(k_fwd, piece(g_ref, chip, c, q), piece(g_ref, chip, c, q), to))
            go(cp(6 + n_arr, piece(g_ref, chip, c, q), piece(g_ref, chip, c, q), sibling))
        for n_arr, (_, chip, q, _, _) in enumerate(arrivals):
            other = piece(g_ref, chip, 1 - c, q)
            cp(6 + n_arr, other, other, sibling).wait_recv()
        for d in sends:
            d.wait_send()

    return pl.pallas_call(
        body, name=name, out_shape=jax.ShapeDtypeStruct(slot.shape, slot.dtype),
        in_specs=[ANY], out_specs=ANY, input_output_aliases={0: 0},
        scratch_shapes=[pltpu.SemaphoreType.DMA((12,)), pltpu.SemaphoreType.DMA((12,))],
    )(slot)


def _pack(arrs):
    parts = []
    for v in arrs:
        f = v.reshape(-1)
        f = jnp.pad(f, (0, (-f.shape[0]) % 1024))
        parts.append(f.reshape(-1, 128))
    return jnp.concatenate(parts, axis=0)


def _unpack(packed, shapes):
    out, r0 = [], 0
    for s in shapes:
        size = 1
        for d in s:
            size *= d
        nr = (size + 1023) // 1024 * 8
        out.append(packed[r0:r0 + nr].reshape(-1)[:size].reshape(s))
        r0 += nr
    return out


SMALL_GRADS = [("loss", (1, 128)), ("pre_mix_norm", (1, 2048)), ("ssd_conv", (8, 4096)), ("ssd_par", (8, 128)),
               ("ssd_norm", (1, 2048)), ("lru_conv", (8, 2048)), ("lru_w_a", (16, 128, 128)), ("lru_b_a", (1, 2048)),
               ("lru_w_x", (16, 128, 128)), ("lru_b_x", (1, 2048)), ("lru_lambda", (1, 2048)), ("lru_norm", (1, 2048)),
               ("post_mix_norm", (1, 2048)), ("pre_mlp_norm", (1, 2048)), ("post_mlp_norm", (1, 2048))]

SMALL_EARLY = [g for g in SMALL_GRADS if g[0] != "pre_mix_norm"]

WEIGHTS = ['pre_mix_norm', 'w_in', 'ssd_conv_w', 'ssd_conv_b', 'ssd_dt_bias', 'ssd_a_log', 'ssd_d', 'ssd_norm', 'lru_conv_w',
           'lru_conv_b', 'lru_w_a', 'lru_b_a', 'lru_w_x', 'lru_b_x', 'lru_lambda', 'lru_norm', 'w_out', 'post_mix_norm',
           'pre_mlp_norm', 'w_mlp_in', 'w_mlp_out', 'post_mlp_norm']
LARGE = ['w_in', 'w_out', 'w_mlp_in', 'w_mlp_out']

D_SSD, D_XBC, DT_W = 2048, 4096, 32
TB = 256


def _w_in_runs(n_shard, n_main):
    c_dt = D_SSD + D_XBC
    runs, p = [], 0
    while p < n_main:
        j, off = divmod(p if p < c_dt else p + DT_W, n_shard)
        ln = min(n_shard - off, (c_dt if p < c_dt else n_main) - p)
        runs.append((p, j, off, ln))
        p += ln
    jd, offd = divmod(c_dt, n_shard)
    assert offd + DT_W <= n_shard
    return runs, (jd, offd)


def _pack_w_in(slots, *, name):
    _, d, n_shard = slots.shape
    n_main = 4 * n_shard - DT_W
    runs, (jd, offd) = _w_in_runs(n_shard, n_main)

    def body(s_ref, main_ref, dt_ref):
        for p, j, off, ln in runs:
            main_ref[:, p:p + ln] = s_ref[j, :, off:off + ln]
        dt_ref[:, 0:DT_W] = s_ref[jd, :, offd:offd + DT_W]
        dt_ref[:, DT_W:] = jnp.zeros((TB, 128 - DT_W), dt_ref.dtype)

    return pl.pallas_call(
        body, name=name, grid=(d // TB,),
        in_specs=[pl.BlockSpec((4, TB, n_shard), lambda i: (0, i, 0))],
        out_specs=[pl.BlockSpec((TB, n_main), lambda i: (i, 0)), pl.BlockSpec((TB, 128), lambda i: (i, 0))],
        out_shape=[jax.ShapeDtypeStruct((d, n_main), slots.dtype), jax.ShapeDtypeStruct((d, 128), slots.dtype)],
        compiler_params=_params(("parallel",)),
    )(slots)


def _unpack_dw_in(dw_main, dw_dt, *, name):
    d, n_main = dw_main.shape
    n_shard = (n_main + DT_W) // 4
    runs, (jd, offd) = _w_in_runs(n_shard, n_main)

    def body(main_ref, dt_ref, o_ref):
        for p, j, off, ln in runs:
            o_ref[j, :, off:off + ln] = main_ref[:, p:p + ln]
        o_ref[jd, :, offd:offd + DT_W] = dt_ref[:, 0:DT_W]

    return pl.pallas_call(
        body, name=name, grid=(d // TB,),
        in_specs=[pl.BlockSpec((TB, n_main), lambda i: (i, 0)), pl.BlockSpec((TB, 128), lambda i: (i, 0))],
        out_specs=pl.BlockSpec((4, TB, n_shard), lambda i: (0, i, 0)),
        out_shape=jax.ShapeDtypeStruct((4, d, n_shard), dw_main.dtype),
        compiler_params=_params(("parallel",)),
    )(dw_main, dw_dt)


def _half(ref, chip_idx, hc, piece=(0, 1)):
    q, nq = piece
    rp = ref.shape[1] // (2 * nq)
    return ref.at[chip_idx, pl.ds((hc * nq + q) * rp, rp), :]


def _job_gather_ici(i_ref, g_ref, send_sems, recv_sems, base, piece):
    x, y, c = _place()
    j = 2 * x + y
    sends, recvs = [], []
    for k, (px, py) in enumerate(_other_chips(x, y)):
        kw = dict(send_sem=send_sems.at[base + k], recv_sem=recv_sems.at[base + k], device_id=(px, py, c), device_id_type=MESH)
        sends.append(pltpu.make_async_remote_copy(src_ref=_half(i_ref, j, c, piece), dst_ref=_half(g_ref, j, c, piece), **kw))
        landed = _half(g_ref, 2 * px + py, c, piece)
        recvs.append(pltpu.make_async_remote_copy(src_ref=landed, dst_ref=landed, **kw))
    return sends, recvs


def _job_gather_sibling(i_ref, g_ref, send_sems, recv_sems, base, piece):
    x, y, c = _place()
    sends, recvs = [], []
    for k, (px, py) in enumerate(_other_chips(x, y)):
        kw = dict(send_sem=send_sems.at[base + k], recv_sem=recv_sems.at[base + k], device_id=(x, y, 1 - c), device_id_type=MESH)
        sends.append(pltpu.make_async_remote_copy(src_ref=_half(i_ref, 2 * px + py, c, piece),
                                                  dst_ref=_half(g_ref, 2 * px + py, c, piece), **kw))
        other = _half(g_ref, 2 * px + py, 1 - c, piece)
        recvs.append(pltpu.make_async_remote_copy(src_ref=other, dst_ref=other, **kw))
    return sends, recvs


def _job_reduce_ici(s_ref, got_refs, send_sems, recv_sems, base, piece):
    x, y, c = _place()
    q, nq = piece
    rp = s_ref.shape[1] // nq
    rows = pl.ds(q * rp, rp)
    sends = [pltpu.make_async_remote_copy(src_ref=s_ref.at[2 * px + py, rows, :], dst_ref=got_refs[k].at[rows, :],
                                          send_sem=send_sems.at[base + k], recv_sem=recv_sems.at[base + k],
                                          device_id=(px, py, c), device_id_type=MESH)
             for k, (px, py) in enumerate(_other_chips(x, y))]
    return sends, sends


GATHER_PLAN = {
    "f_inproj": [("w_out", _job_gather_ici, (0, 1)), ("w_mlp_in", _job_gather_ici, (0, 2))],
    "f_ssdconv": [("w_mlp_in", _job_gather_ici, (1, 2)), ("w_out", _job_gather_sibling, (0, 1)),
                  ("w_mlp_in", _job_gather_sibling, (0, 2))],
    "f_lruconv": [("w_mlp_in", _job_gather_sibling, (1, 2))],
    "f_ssd": [("w_mlp_out", _job_gather_ici, (0, 2))],
    "f_lrugates": [("w_mlp_out", _job_gather_sibling, (0, 2))],
    "f_outproj": [("w_mlp_out", _job_gather_ici, (1, 2))],
    "f_mlpin": [("w_mlp_out", _job_gather_sibling, (1, 2))],
}
SIBLING_PLAN = {"b_mlpin_dx": "w_mlp_out", "b_mid": "w_mlp_in", "b_outproj_dx": "w_out"}
REDUCE_PLAN = {
    "b_mlpin_dw": [("w_mlp_out", (0, 2))],
    "b_outproj_dw": [("w_mlp_out", (1, 2))],
    "b_ssd": [("w_mlp_in", (0, 1)), ("w_out", (0, 1))],
    "b_inproj_dx": [("w_in", (0, 1))],
}
SHARE_PLAN = {"b_ssdconv": ["w_mlp_out", "w_mlp_in", "w_out"]}
SMALL_PLAN = {"b_inproj_dw": 0, "b_inproj_dx": 1}


class _Part:
    def __init__(self, ins, outs, aliases, n_sems, make, done):
        self.ins, self.outs, self.aliases, self.n_sems, self.make, self.done = ins, outs, aliases, n_sems, make, done


def _merge_parts(parts):
    ins, outs, aliases, offs, n = [], [], {}, [], 0
    for p in parts:
        offs.append((len(ins), len(outs), n))
        aliases.update({len(ins) + i: len(outs) + o for i, o in p.aliases.items()})
        ins, outs, n = ins + list(p.ins), outs + list(p.outs), n + p.n_sems

    def make(i_refs, o_refs, send_sems, recv_sems):
        sends, recvs = [], []
        for p, (io, oo, so) in zip(parts, offs):
            s, r = p.make(i_refs[io:io + len(p.ins)], o_refs[oo:oo + len(p.outs)], send_sems, recv_sems, so)
            sends, recvs = sends + s, recvs + r
        return sends, recvs

    return _Side(ins, outs, aliases, n, make)


def _parts_done(parts, outs):
    o = 0
    for p in parts:
        p.done(list(outs[o:o + len(p.outs)]))
        o += len(p.outs)


def _comm_call(parts, *, name):
    side = _merge_parts(parts)
    si, so = len(side.ins), len(side.outs)

    def body(*refs):
        sends, recvs = side.make(refs[:si], refs[si:si + so], refs[-2], refs[-1])
        for d in sends:
            d.start()
        for d in recvs:
            d.wait_recv()
        for d in sends:
            d.wait_send()

    outs = pl.pallas_call(
        body, name=name, out_shape=list(side.outs), in_specs=[ANY] * si, out_specs=[ANY] * so,
        scratch_shapes=[pltpu.SemaphoreType.DMA((side.n_sems,)), pltpu.SemaphoreType.DMA((side.n_sems,))],
        input_output_aliases=side.aliases,
    )(*side.ins)
    _parts_done(parts, outs)


class _DistPlan:
    def __init__(self, slots, w_main, w_dt, where, shapes):
        self.slots, self.w_main, self.w_dt, self.where, self.shapes = slots, w_main, w_dt, where, shapes
        self.sib_pending, self.partial, self.got, self.totals, self.shared = {}, {}, {}, {}, {}
        self.small_buf = None

    def weight(self, name):
        d = self.w_main.shape[0]
        return {"w_main": lambda: self.w_main, "w_dt": lambda: self.w_dt, "w_out": lambda: self.slots["w_out"].reshape(-1, d),
                "w_mi": lambda: self.slots["w_mlp_in"], "w_mo": lambda: self.slots["w_mlp_out"].reshape(-1, d)}[name]()

    def _gather_part(self, jobs):
        names = []
        for n, _, _ in jobs:
            if n not in names:
                names.append(n)
        arrs = [self.slots[n] for n in names]

        def make(i_refs, o_refs, send_sems, recv_sems, base):
            sends, recvs = [], []
            for q, (n, job, piece) in enumerate(jobs):
                s, r = job(i_refs[names.index(n)], o_refs[names.index(n)], send_sems, recv_sems, base + 3 * q, piece)
                sends, recvs = sends + s, recvs + r
            return sends, recvs

        return _Part(arrs, [jax.ShapeDtypeStruct(a.shape, a.dtype) for a in arrs], {i: i for i in range(len(arrs))},
                     3 * len(jobs), make, lambda outs: self.slots.update(zip(names, outs)))

    def _sibling_part(self, name):
        g = self.sib_pending.pop(name)
        _, _, rh, cc = g.shape

        def make(i_refs, o_refs, send_sems, recv_sems, base):
            x, y, c = _place()
            sends = [pltpu.make_async_remote_copy(src_ref=i_refs[0].at[s, 1 - c], dst_ref=o_refs[0].at[s],
                                                  send_sem=send_sems.at[base + s], recv_sem=recv_sems.at[base + s],
                                                  device_id=(x, y, 1 - c), device_id_type=MESH) for s in range(4)]
            return sends, sends

        def done(outs):
            (t,) = outs
            self.partial[name] = _blocks(
                lambda u, v: (u.astype(F32) + v.astype(F32),),
                [(g, (None, None, TB, cc), lambda q, i, s: (q, s[1], i, 0)), (t, (None, TB, cc), lambda q, i, s: (q, i, 0))],
                [(jax.ShapeDtypeStruct(t.shape, BF16), (None, TB, cc), lambda q, i, s: (q, i, 0))],
                grid=(4, rh // TB), name="add_sibling_" + name, prefetch=self.where)[0]

        return _Part([g], [jax.ShapeDtypeStruct((4, rh, cc), g.dtype)], {}, 4, make, done)

    def _reduce_part(self, items):
        parts = [self.partial[n] for n, _ in items]
        ins, aliases = list(parts), {}
        for q, (n, (piece, _)) in enumerate(items):
            if piece > 0:
                for k in range(3):
                    aliases[len(ins)] = 3 * q + k
                    ins.append(self.got[n][k])

        def make(i_refs, o_refs, send_sems, recv_sems, base):
            sends = []
            for q, (_, piece) in enumerate(items):
                sends += _job_reduce_ici(i_refs[q], o_refs[3 * q:3 * q + 3], send_sems, recv_sems, base + 3 * q, piece)[0]
            return sends, sends

        def done(outs):
            for q, (name, (piece, n_pieces)) in enumerate(items):
                g3 = self.got[name] = outs[3 * q:3 * q + 3]
                if piece < n_pieces - 1:
                    continue
                s4 = self.partial[name]
                _, rh, cc = s4.shape
                row = ((TB, cc), lambda i, s: (i, 0))
                self.totals[name] = _blocks(
                    lambda o, r0, r1, r2: (((o.astype(F32) + r0.astype(F32)) + r1.astype(F32)) + r2.astype(F32),),
                    [(s4, (None, TB, cc), lambda i, s: (s[0], i, 0)), (g3[0],) + row, (g3[1],) + row, (g3[2],) + row],
                    [(jax.ShapeDtypeStruct((2, rh, cc), F32), (None, TB, cc), lambda i, s: (s[1], i, 0))],
                    grid=(rh // TB,), name="add_chips_" + name, prefetch=self.where)[0]
                if name == "w_in":
                    _comm_call([self._share_part([name])], name="share_" + name)

        outs = [jax.ShapeDtypeStruct(s.shape[1:], s.dtype) for s in parts for _ in range(3)]
        return _Part(ins, outs, aliases, 3 * len(items), make, done)

    def _share_part(self, names):
        ts = [self.totals[n] for n in names]

        def make(i_refs, o_refs, send_sems, recv_sems, base):
            x, y, c = _place()
            sends, recvs = [], []
            for w in range(len(ts)):
                kw = dict(send_sem=send_sems.at[base + w], recv_sem=recv_sems.at[base + w], device_id=(x, y, 1 - c),
                          device_id_type=MESH)
                sends.append(pltpu.make_async_remote_copy(src_ref=i_refs[w].at[c], dst_ref=o_refs[w].at[c], **kw))
                recvs.append(pltpu.make_async_remote_copy(src_ref=i_refs[w].at[c], dst_ref=o_refs[w].at[1 - c], **kw))
            return sends, recvs

        def done(outs):
            self.shared.update({n: o.reshape(self.shapes[n]) for n, o in zip(names, outs)})

        return _Part(ts, [jax.ShapeDtypeStruct(t.shape, t.dtype) for t in ts], {i: i for i in range(len(ts))}, len(ts), make, done)

    def _small_part(self, stage):
        buf = self.small_buf

        def make(i_refs, o_refs, send_sems, recv_sems, base):
            x, y, c = _place()
            src, dst = i_refs[0], o_refs[0]
            sends, recvs = [], []

            def add(k, block, to, frm):
                kw = dict(send_sem=send_sems.at[base + k], recv_sem=recv_sems.at[base + k], device_id=to, device_id_type=MESH)
                sends.append(pltpu.make_async_remote_copy(src_ref=src.at[block], dst_ref=dst.at[block], **kw))
                recvs.append(pltpu.make_async_remote_copy(src_ref=src.at[frm], dst_ref=dst.at[frm], **kw))

            if stage == 0:
                add(0, 4 * x + 2 * y + c, (x, y, 1 - c), 4 * x + 2 * y + 1 - c)
            for k, (px, py) in enumerate(_other_chips(x, y)):
                if stage == 0:
                    add(1 + k, 4 * x + 2 * y + c, (px, py, c), 4 * px + 2 * py + c)
                else:
                    add(k, 4 * px + 2 * py + c, (x, y, 1 - c), 4 * px + 2 * py + 1 - c)
            return sends, recvs

        def done(outs):
            (self.small_buf,) = outs

        return _Part([buf], [jax.ShapeDtypeStruct(buf.shape, buf.dtype)], {0: 0}, 4 if stage == 0 else 3, make, done)

    def side(self, kernel_name):
        parts = []
        if kernel_name in GATHER_PLAN:
            parts.append(self._gather_part(GATHER_PLAN[kernel_name]))
        if kernel_name in SIBLING_PLAN:
            parts.append(self._sibling_part(SIBLING_PLAN[kernel_name]))
        if kernel_name in REDUCE_PLAN:
            parts.append(self._reduce_part(REDUCE_PLAN[kernel_name]))
        if kernel_name in SHARE_PLAN:
            parts.append(self._share_part(SHARE_PLAN[kernel_name]))
        if kernel_name in SMALL_PLAN:
            parts.append(self._small_part(SMALL_PLAN[kernel_name]))
        self._carried = parts
        return _merge_parts(parts) if parts else None

    def done(self, kernel_name, side_outs):
        _parts_done(self._carried, side_outs)

    def grad(self, name, g):
        if name == "w_in":
            g = _unpack_dw_in(*g, name="unpack_dw_in")
        self.sib_pending[name] = g.reshape(4, 2, g.shape[1] // 2, g.shape[2])
        if name == "w_in":
            _comm_call([self._sibling_part(name)], name="reduce_sibling_" + name)

    def small(self, grads):
        packed = _pack([grads[n] for n, _ in SMALL_EARLY])
        n_rows = packed.shape[0]
        self.small_buf = _blocks(lambda t: (t,), [(packed, (n_rows, 128), lambda i, s: (0, 0))],
                                 [(jax.ShapeDtypeStruct((8, n_rows, 128), F32), (None, n_rows, 128), lambda i, s: (s[2], 0, 0))],
                                 grid=(1,), name="place_small", prefetch=self.where)[0]

    def finish(self):
        assert len(self.shared) == len(self.shapes)
        return self.shared


def kernel(x, pre_mix_norm, w_in, ssd_conv_w, ssd_conv_b, ssd_dt_bias, ssd_a_log, ssd_d, ssd_norm, lru_conv_w, lru_conv_b, lru_w_a, lru_b_a, lru_w_x, lru_b_x, lru_lambda, lru_norm, w_out, post_mix_norm, pre_mlp_norm, w_mlp_in, w_mlp_out, post_mlp_norm, loss_target, m_pre_mix_norm, m_w_in, m_ssd_conv_w, m_ssd_conv_b, m_ssd_dt_bias, m_ssd_a_log, m_ssd_d, m_ssd_norm, m_lru_conv_w, m_lru_conv_b, m_lru_w_a, m_lru_b_a, m_lru_w_x, m_lru_b_x, m_lru_lambda, m_lru_norm, m_w_out, m_post_mix_norm, m_pre_mlp_norm, m_w_mlp_in, m_w_mlp_out, m_post_mlp_norm, v_pre_mix_norm, v_w_in, v_ssd_conv_w, v_ssd_conv_b, v_ssd_dt_bias, v_ssd_a_log, v_ssd_d, v_ssd_norm, v_lru_conv_w, v_lru_conv_b, v_lru_w_a, v_lru_b_a, v_lru_w_x, v_lru_b_x, v_lru_lambda, v_lru_norm, v_w_out, v_post_mix_norm, v_pre_mlp_norm, v_w_mlp_in, v_w_mlp_out, v_post_mlp_norm):
    a = dict(locals())
    j = 2 * lax.axis_index("x") + lax.axis_index("y")
    D = x.shape[-1]
    c_dt = D_SSD + D_XBC

    core = lax.axis_index("c")
    where = jnp.stack([j, core, 2 * j + core]).astype(jnp.int32)
    slots = {}
    for name in LARGE:
        w = a[name][0]
        r, cc = w.shape
        slots[name] = _blocks(lambda t: (t,), [(w, (TB, cc), lambda i, s: (i, 0))],
                              [(jax.ShapeDtypeStruct((4, r, cc), BF16), (None, TB, cc), lambda i, s: (s[0], i, 0))],
                              grid=(r // TB,), name="cast_" + name, prefetch=where)[0]
    g_in = _gather_tree(slots.pop("w_in"), name="gather_w_in")
    w_main, w_dt = _pack_w_in(g_in, name="pack_w_in")
    taps = jnp.concatenate([ssd_conv_w[0].reshape(-1, 128), lru_conv_w[0].reshape(-1, 128)], axis=0)
    taps = _allgather8(taps, name="gather_taps").reshape(8, taps.shape[0], 128)[0::2]
    n_ssd = ssd_conv_w.shape[1] * ssd_conv_w.shape[2] // 128
    ssd_taps = taps[:, :n_ssd].reshape(4, CONV_W, -1).transpose(1, 0, 2).reshape(CONV_W, -1)
    lru_taps = taps[:, n_ssd:].reshape(4, CONV_W, -1).transpose(1, 0, 2).reshape(CONV_W, -1)

    def row128(v):
        return jnp.pad(v, ((0, 0), (0, 128 - v.shape[1])))

    p = dict(pre_mix_norm=pre_mix_norm, ssd_conv_w=ssd_taps, ssd_conv_b=ssd_conv_b,
             dtb=row128(ssd_dt_bias), alog=row128(ssd_a_log), drow=row128(ssd_d), ssd_norm=ssd_norm,
             lru_conv_w=lru_taps, lru_conv_b=lru_conv_b, lru_w_a=lru_w_a[0], lru_b_a=lru_b_a.reshape(1, -1),
             lru_w_x=lru_w_x[0], lru_b_x=lru_b_x.reshape(1, -1), lru_lambda=lru_lambda, lru_norm=lru_norm,
             post_mix_norm=post_mix_norm, pre_mlp_norm=pre_mlp_norm, post_mlp_norm=post_mlp_norm)
    plan = _DistPlan(slots, w_main, w_dt, where, {n: a[n].shape for n in LARGE})
    grad_x, small = _local_step(x[0], loss_target[0], p, plan)

    large_grads = plan.finish()

    delta, new_m, new_v = {}, {}, {}
    for name in LARGE:
        w = a[name][0]
        cc = w.shape[1]
        outs = _rowwise(_adamw, [w, large_grads[name][0], a["m_" + name][0], a["v_" + name][0]], [], [(cc, F32)] * 3, [],
                        name="adamw_" + name, tr=128, sub=8)
        delta[name], new_m[name], new_v[name] = [o[None] for o in outs]

    total = _sum8(plan.small_buf, name="sum_small")
    tot = dict(zip([n for n, _ in SMALL_EARLY], _unpack(total, [s for _, s in SMALL_EARLY])))
    late = small["pre_mix_norm"].reshape(-1, 128)
    late = _allgather8(late, name="gather_late").reshape(8, late.shape[0], 128)
    tot["pre_mix_norm"] = _sum8(late, name="sum_late").reshape(small["pre_mix_norm"].shape)
    loss = tot["loss"][0, 0]
    n_sc, n_lc = ssd_conv_w.shape[2], lru_conv_w.shape[2]
    grads = dict(
        pre_mix_norm=tot["pre_mix_norm"],
        ssd_conv_w=lax.dynamic_slice(tot["ssd_conv"][:CONV_W], (0, j * n_sc), (CONV_W, n_sc))[None],
        ssd_conv_b=tot["ssd_conv"][CONV_W:CONV_W + 1],
        ssd_dt_bias=tot["ssd_par"][0:1, :DT_W], ssd_a_log=tot["ssd_par"][1:2, :DT_W], ssd_d=tot["ssd_par"][2:3, :DT_W],
        ssd_norm=tot["ssd_norm"],
        lru_conv_w=lax.dynamic_slice(tot["lru_conv"][:CONV_W], (0, j * n_lc), (CONV_W, n_lc))[None],
        lru_conv_b=tot["lru_conv"][CONV_W:CONV_W + 1],
        lru_w_a=tot["lru_w_a"][None], lru_b_a=tot["lru_b_a"].reshape(lru_b_a.shape),
        lru_w_x=tot["lru_w_x"][None], lru_b_x=tot["lru_b_x"].reshape(lru_b_x.shape),
        lru_lambda=tot["lru_lambda"], lru_norm=tot["lru_norm"], post_mix_norm=tot["post_mix_norm"],
        pre_mlp_norm=tot["pre_mlp_norm"], post_mlp_norm=tot["post_mlp_norm"])

    grads.update(large_grads)

    small_w = [n for n in WEIGHTS if n not in LARGE]
    n_sw = len(small_w)

    def flat(v):
        return v.reshape(-1, v.shape[-1])

    def adamw_all(*refs):
        ins, outs = refs[:4 * n_sw], refs[4 * n_sw:]
        for q in range(n_sw):
            res = _adamw(*[ins[4 * q + t][...] for t in range(4)])
            for t in range(3):
                outs[3 * q + t][...] = res[t]

    operands = [flat(d[n]) for n in small_w for d in (a, grads, {k: a["m_" + k] for k in small_w}, {k: a["v_" + k] for k in small_w})]
    outs = pl.pallas_call(
        adamw_all, name="adamw_small",
        out_shape=[jax.ShapeDtypeStruct(flat(a[n]).shape, F32) for n in small_w for _ in range(3)],
        in_specs=[pl.BlockSpec(memory_space=pltpu.VMEM)] * (4 * n_sw), out_specs=[pl.BlockSpec(memory_space=pltpu.VMEM)] * (3 * n_sw),
        compiler_params=pltpu.CompilerParams(vmem_limit_bytes=VMEM_LIMIT),
    )(*operands)
    for q, n in enumerate(small_w):
        delta[n], new_m[n], new_v[n] = [o.reshape(a[n].shape) for o in outs[3 * q:3 * q + 3]]

    return (loss, grad_x[None], *[grads[n] for n in WEIGHTS], *[delta[n] for n in WEIGHTS],
            *[new_m[n] for n in WEIGHTS], *[new_v[n] for n in WEIGHTS])
```

```python
import functools

import jax
import jax.numpy as jnp
from jax import lax
from jax.experimental import pallas as pl
from jax.experimental.pallas import tpu as pltpu

F32 = jnp.float32
BF16 = jnp.bfloat16
MESH = pl.DeviceIdType.MESH

EPS = 1e-6
LRU_C = 8.0
ADAM_LR = 0.001
ADAM_B1 = 0.9
ADAM_B2 = 0.999
ADAM_EPS = 1e-08
ADAM_WD = 0.01
ADAM_STEP = 10

N_GROUPS = 8
HEADS_PER_GROUP = 4
HEAD_DIM = 64
GROUP_W = HEADS_PER_GROUP * HEAD_DIM
STATE = 128
LRU_HEADS = 16
LRU_BLOCK = 128
CONV_W = 4
SSD_CHUNK = 256
HALO = 8

VMEM_LIMIT = 48 * 1024 * 1024


def _params(sem=None):
    return pltpu.CompilerParams(dimension_semantics=sem, vmem_limit_bytes=VMEM_LIMIT)


@jax.custom_jvp
def _log1p(x):
    u = 1.0 + x
    d = u - 1.0
    return jnp.where(d == 0.0, x, jnp.log(u) * (x / jnp.where(d == 0.0, 1.0, d)))


@_log1p.defjvp
def _log1p_jvp(primals, tangents):
    (x,), (t,) = primals, tangents
    return _log1p(x), t / (1.0 + x)


@jax.custom_jvp
def _expm1(x):
    u = jnp.exp(x)
    lu = jnp.log(u)
    safe = jnp.where(lu == 0.0, 1.0, lu)
    y = (u - 1.0) * (x / safe)
    y = jnp.where(lu == 0.0, x, y)
    return jnp.where(u == 0.0, -1.0, y)


@_expm1.defjvp
def _expm1_jvp(primals, tangents):
    (x,), (t,) = primals, tangents
    return _expm1(x), t * jnp.exp(x)


def _softplus(x):
    return jnp.maximum(x, 0.0) + _log1p(jnp.exp(-jnp.abs(x)))


def _sigmoid(x):
    return 1.0 / (1.0 + jnp.exp(-x))


def _silu(x):
    return x * _sigmoid(x)


def _gelu(x):
    c = 0.7978845608028654
    return 0.5 * x * (1.0 + jnp.tanh(c * (x + 0.044715 * (x * x * x))))


def _rms(x, g):
    return x * lax.rsqrt(jnp.mean(x * x, axis=-1, keepdims=True) + EPS) * g


def _dot(a, b, dims):
    return lax.dot_general(a.astype(BF16), b.astype(BF16), (dims, ((), ())), preferred_element_type=F32)


_NN = ((1,), (0,))
_NT = ((1,), (1,))
_TN = ((0,), (0,))


ANY = pl.BlockSpec(memory_space=pl.ANY)


class _Side:
    def __init__(self, ins, outs, aliases, n_sems, make):
        self.ins, self.outs, self.aliases, self.n_sems, self.make = ins, outs, aliases, n_sems, make


def _call(body, args, *, name, grid, in_specs, out_specs, out_shape, scratch_shapes=(), sem=None, side=None, into=None):
    in_specs, out_specs, out_shape, scratch_shapes = list(in_specs), list(out_specs), list(out_shape), list(scratch_shapes)
    held = []
    for oi, (buf, n_cols, off) in (into or {}).items():
        spec = out_specs[oi]
        out_shape[oi] = jax.ShapeDtypeStruct((out_shape[oi].shape[0], n_cols), out_shape[oi].dtype)
        out_specs[oi] = pl.BlockSpec(spec.block_shape, lambda *idx, _m=spec.index_map, _o=off: (_m(*idx)[0], _m(*idx)[1] + _o))
        if buf is not None:
            held.append((oi, buf))
    if side is None and not held:
        outs = pl.pallas_call(body, name=name, grid=grid, in_specs=in_specs, out_specs=out_specs, out_shape=out_shape,
                              scratch_shapes=scratch_shapes, compiler_params=_params(sem))(*args)
        return list(outs), []
    side_ins, side_outs = (list(side.ins), list(side.outs)) if side is not None else ([], [])
    n_in, n_out, n_scr, nh, si, so = len(in_specs), len(out_specs), len(scratch_shapes), len(held), len(side_ins), len(side_outs)

    def full(*refs):
        s_in = refs[n_in + nh:n_in + nh + si]
        o0 = n_in + nh + si
        s_out = refs[o0 + n_out:o0 + n_out + so]
        scr = refs[o0 + n_out + so:o0 + n_out + so + n_scr]
        if side is None:
            body(*refs[:n_in], *refs[o0:o0 + n_out], *scr)
            return
        send_sems, recv_sems = refs[-2], refs[-1]
        ids = [pl.program_id(d) for d in range(len(grid))]
        first = functools.reduce(jnp.logical_and, [i == 0 for i in ids])
        last = functools.reduce(jnp.logical_and, [i == g - 1 for i, g in zip(ids, grid)])

        @pl.when(first)
        def _():
            for d in side.make(s_in, s_out, send_sems, recv_sems)[0]:
                d.start()

        body(*refs[:n_in], *refs[o0:o0 + n_out], *scr)

        @pl.when(last)
        def _():
            sends, recvs = side.make(s_in, s_out, send_sems, recv_sems)
            for d in recvs:
                d.wait_recv()
            for d in sends:
                d.wait_send()

    aliases = {n_in + h: oi for h, (oi, _) in enumerate(held)}
    if side is not None:
        aliases.update({n_in + nh + i: n_out + o for i, o in side.aliases.items()})
        scratch_shapes = scratch_shapes + [pltpu.SemaphoreType.DMA((side.n_sems,)), pltpu.SemaphoreType.DMA((side.n_sems,))]
    outs = pl.pallas_call(
        full, name=name, grid=grid, in_specs=in_specs + [ANY] * (nh + si), out_specs=out_specs + [ANY] * so,
        out_shape=out_shape + side_outs, scratch_shapes=scratch_shapes, input_output_aliases=aliases,
        compiler_params=_params(("arbitrary",) * len(grid) if side is not None else sem),
    )(*args, *[b for _, b in held], *side_ins)
    return list(outs[:n_out]), list(outs[n_out:])


def _matmul(a, b, *, mode, m, n, k, tm, tn, tk, out_dtypes, name, a_spec=None, b_spec=None,
            out_specs=None, out_shapes=None, extras=(), epilogue=None, side=None):
    tm, tn, tk = min(tm, m), min(tn, n), min(tk, k)
    assert m % tm == 0 and n % tn == 0 and k % tk == 0, (name, m, n, k, tm, tn, tk)
    nk = k // tk
    dims = {"nn": _NN, "nt": _NT, "tn": _TN}[mode]
    if a_spec is None:
        a_spec = pl.BlockSpec((tk, tm), lambda i, j, kk: (kk, i)) if mode == "tn" else pl.BlockSpec((tm, tk), lambda i, j, kk: (i, kk))
    if b_spec is None:
        b_spec = pl.BlockSpec((tn, tk), lambda i, j, kk: (j, kk)) if mode == "nt" else pl.BlockSpec((tk, tn), lambda i, j, kk: (kk, j))
    tile = pl.BlockSpec((tm, tn), lambda i, j, kk: (i, j))
    if out_specs is None:
        out_specs = [tile for _ in out_dtypes]
    if out_shapes is None:
        out_shapes = [jax.ShapeDtypeStruct((m, n), d) for d in out_dtypes]
    n_ex, n_out = len(extras), len(out_dtypes)

    def body(*refs):
        a_ref, b_ref = refs[0], refs[1]
        ex_refs = refs[2:2 + n_ex]
        o_refs = refs[2 + n_ex:2 + n_ex + n_out]
        def finish(r):
            outs = epilogue(r, *[e[...] for e in ex_refs]) if epilogue is not None else (r,)
            for o_ref, o in zip(o_refs, outs):
                o_ref[...] = o.astype(o_ref.dtype)

        if nk == 1:
            finish(_dot(a_ref[...], b_ref[...], dims))
            return
        acc = refs[-1]
        kk = pl.program_id(2)

        @pl.when(kk == 0)
        def _():
            acc[...] = _dot(a_ref[...], b_ref[...], dims)

        @pl.when(kk > 0)
        def _():
            acc[...] += _dot(a_ref[...], b_ref[...], dims)

        @pl.when(kk == nk - 1)
        def _():
            finish(acc[...])

    outs, side_outs = _call(
        body, (a, b, *extras), name=name, grid=(m // tm, n // tn, nk),
        in_specs=[a_spec, b_spec] + [tile for _ in extras], out_specs=out_specs, out_shape=out_shapes,
        scratch_shapes=[] if nk == 1 else [pltpu.VMEM((tm, tn), F32)], sem=("parallel", "parallel", "arbitrary"), side=side)
    return outs if side is None else (outs, side_outs)


def _rowwise(fn, rows, bcast, out_rows, out_acc, *, name, tr, sub, side=None, into=None):
    rows = [r if isinstance(r, tuple) else (r, 0, r.shape[1]) for r in rows]
    row_specs = []
    for arr, c0, w in rows:
        assert c0 % w == 0, (name, c0, w)
        row_specs.append((w, c0 // w))
    rows = [r[0] for r in rows]
    L = rows[0].shape[0]
    tr = min(tr, L)
    sub = min(sub, tr)
    assert L % tr == 0 and tr % sub == 0, (name, L, tr, sub)
    n_r, n_b, n_or, n_oa = len(rows), len(bcast), len(out_rows), len(out_acc)

    def body(*refs):
        r_refs = refs[:n_r]
        b_refs = refs[n_r:n_r + n_b]
        or_refs = refs[n_r + n_b:n_r + n_b + n_or]
        oa_refs = refs[n_r + n_b + n_or:]
        i = pl.program_id(0)

        @pl.when(i == 0)
        def _():
            for o in oa_refs:
                o[...] = jnp.zeros_like(o)

        bvals = [b[...] for b in b_refs]

        def step(s, carry):
            r0 = pl.multiple_of(s * sub, sub)
            tiles = [r[pl.ds(r0, sub), :] for r in r_refs]
            outs = fn(*tiles, *bvals)
            for o_ref, o in zip(or_refs, outs[:n_or]):
                o_ref[pl.ds(r0, sub), :] = o.astype(o_ref.dtype)
            for o_ref, o in zip(oa_refs, outs[n_or:]):
                o_ref[...] += o
            return carry

        if tr == sub:
            step(0, 0)
        else:
            lax.fori_loop(0, tr // sub, step, 0, unroll=8)

    def whole(shape):
        nd = len(shape)
        return pl.BlockSpec(shape, lambda i, _n=nd: (0,) * _n)

    outs, side_outs = _call(
        body, (*rows, *bcast), name=name, grid=(L // tr,),
        in_specs=[pl.BlockSpec((tr, w), lambda i, _c=cb: (i, _c)) for w, cb in row_specs] + [whole(b.shape) for b in bcast],
        out_specs=[pl.BlockSpec((tr, c), lambda i: (i, 0)) for c, _ in out_rows] + [whole(s) for s in out_acc],
        out_shape=[jax.ShapeDtypeStruct((L, c), d) for c, d in out_rows] + [jax.ShapeDtypeStruct(s, F32) for s in out_acc],
        sem=("arbitrary",), side=side, into=into)
    return outs if side is None else (outs, side_outs)


def _colsum(x):
    return jnp.sum(x, axis=0, keepdims=True)


def _grouped(fn, rows, params, out_rows, out_acc, *, gw, name, tr, side=None, into=None):
    rows = [r if isinstance(r, tuple) else (r, 0) for r in rows]
    L = rows[0][0].shape[0]
    tr = min(tr, L)
    assert L % tr == 0
    G = None
    for p in params:
        G = p.shape[0] if p.ndim == 3 else p.shape[1] // gw
    cw = G * gw
    n_r, n_p, n_or, n_oa = len(rows), len(params), len(out_rows), len(out_acc)

    def pick(ref, g):
        return ref[g] if len(ref.shape) == 3 else ref[:, g * gw:(g + 1) * gw]

    def body(*refs):
        r_refs = refs[:n_r]
        p_refs = refs[n_r:n_r + n_p]
        or_refs = refs[n_r + n_p:n_r + n_p + n_or]
        oa_refs = refs[n_r + n_p + n_or:]

        @pl.when(pl.program_id(0) == 0)
        def _():
            for o in oa_refs:
                o[...] = jnp.zeros_like(o)

        for g in range(G):
            outs = fn(*[pick(r, g) for r in r_refs], *[pick(p, g) for p in p_refs])
            for o_ref, o in zip(or_refs, outs[:n_or]):
                o_ref[:, g * gw:(g + 1) * gw] = o.astype(o_ref.dtype)
            for o_ref, o in zip(oa_refs, outs[n_or:]):
                if len(o_ref.shape) == 3:
                    o_ref[g] += o
                else:
                    o_ref[:, g * gw:(g + 1) * gw] += o

    def whole(shape):
        nd = len(shape)
        return pl.BlockSpec(shape, lambda i, _n=nd: (0,) * _n)

    for _, c0 in rows:
        assert c0 % cw == 0
    outs, side_outs = _call(
        body, (*[r[0] for r in rows], *params), name=name, grid=(L // tr,),
        in_specs=[pl.BlockSpec((tr, cw), lambda i, _c=c0 // cw: (i, _c)) for _, c0 in rows] + [whole(p.shape) for p in params],
        out_specs=[pl.BlockSpec((tr, cw), lambda i: (i, 0)) for _ in out_rows] + [whole(s) for s in out_acc],
        out_shape=[jax.ShapeDtypeStruct((L, cw), d) for d in out_rows] + [jax.ShapeDtypeStruct(s, F32) for s in out_acc],
        sem=("arbitrary",), side=side, into=into)
    return outs if side is None else (outs, side_outs)


def _dsilu(p):
    s = _sigmoid(p)
    return s + p * s * (1.0 - s)


CONV_RC, CONV_CC = 32, 512


def _past_window(x_ref, head, r0, k, cs):
    if r0 == 0:
        return head[HALO - 3 + k:HALO - 3 + k + CONV_RC, cs]
    return x_ref[r0 - 3 + k:r0 - 3 + k + CONV_RC, cs]


def _conv_fwd(x, c0, C, w, b, *, silu, name, tr=512, cb=1024, side=None):
    L = x.shape[0]
    tr = min(tr, L)
    nb, hb = L // tr, tr // HALO
    assert L % tr == 0 and C % cb == 0 and c0 % cb == 0 and tr % CONV_RC == 0 and cb % CONV_CC == 0
    n_out = 2 if silu else 1

    def body(x_ref, h_ref, w_ref, b_ref, *rest):
        o_refs, head = rest[:n_out], rest[n_out]
        head[0:HALO, :] = h_ref[...]

        @pl.when(pl.program_id(0) == 0)
        def _():
            head[0:HALO, :] = jnp.zeros((HALO, cb), F32)

        head[HALO:, :] = x_ref[0:CONV_RC, :]
        for cc in range(cb // CONV_CC):
            cs = slice(cc * CONV_CC, (cc + 1) * CONV_CC)
            wv, bv = w_ref[:, cs], b_ref[:, cs]
            for r0 in range(0, tr, CONV_RC):
                y = bv
                for k in range(CONV_W):
                    y = y + wv[k:k + 1, :] * _past_window(x_ref, head, r0, k, cs)
                o_refs[0][r0:r0 + CONV_RC, cs] = y
                if silu:
                    o_refs[1][r0:r0 + CONV_RC, cs] = _silu(y)

    outs, side_outs = _call(
        body, (x, x, w, b), name=name, grid=(nb, C // cb),
        in_specs=[
            pl.BlockSpec((tr, cb), lambda i, j: (i, c0 // cb + j)),
            pl.BlockSpec((HALO, cb), lambda i, j: (jnp.maximum(i * hb - 1, 0), c0 // cb + j)),
            pl.BlockSpec((CONV_W, cb), lambda i, j: (0, j)),
            pl.BlockSpec((1, cb), lambda i, j: (0, j)),
        ],
        out_specs=[pl.BlockSpec((tr, cb), lambda i, j: (i, j)) for _ in range(n_out)],
        out_shape=[jax.ShapeDtypeStruct((L, C), F32) for _ in range(n_out)],
        scratch_shapes=[pltpu.VMEM((HALO + CONV_RC, cb), F32)], sem=("parallel", "parallel"), side=side)
    return outs if side is None else (outs, side_outs)


def _fold8(v):
    return (v[0:8] + v[8:16]) + (v[16:24] + v[24:32])


def _conv_bwd(dact, dc0, pre, x, xc0, C, w, *, silu, name, tr=512, cb=1024, side=None, into=None):
    L = x.shape[0]
    tr = min(tr, L)
    nb, hb = L // tr, tr // HALO
    last_h = L // HALO - 1
    assert L % tr == 0 and C % cb == 0 and tr % CONV_RC == 0 and cb % CONV_CC == 0

    def body(*refs):
        if silu:
            d_ref, dh_ref, p_ref, ph_ref, x_ref, xh_ref, w_ref, dx_ref, dwb_ref, dp, head = refs
        else:
            d_ref, dh_ref, x_ref, xh_ref, w_ref, dx_ref, dwb_ref, dp, head = refs
        i = pl.program_id(1)
        dp[tr:, :] = dh_ref[...] * _dsilu(ph_ref[...]) if silu else dh_ref[...]

        @pl.when(i == nb - 1)
        def _():
            dp[tr:, :] = jnp.zeros((HALO, cb), F32)

        head[0:HALO, :] = xh_ref[...]

        @pl.when(i == 0)
        def _():
            head[0:HALO, :] = jnp.zeros((HALO, cb), F32)
            dwb_ref[...] = jnp.zeros_like(dwb_ref)

        head[HALO:, :] = x_ref[0:CONV_RC, :]
        for cc in range(cb // CONV_CC):
            cs = slice(cc * CONV_CC, (cc + 1) * CONV_CC)
            for r0 in range(0, tr, CONV_RC):
                rs = slice(r0, r0 + CONV_RC)
                dp[rs, cs] = d_ref[rs, cs] * _dsilu(p_ref[rs, cs]) if silu else d_ref[rs, cs]
        for cc in range(cb // CONV_CC):
            cs = slice(cc * CONV_CC, (cc + 1) * CONV_CC)
            wv = w_ref[:, cs]
            acc = [jnp.zeros((8, CONV_CC), F32) for _ in range(CONV_W + 1)]
            for r0 in range(0, tr, CONV_RC):
                dm = dp[r0:r0 + CONV_RC, cs]
                dx = jnp.zeros((CONV_RC, CONV_CC), F32)
                for k in range(CONV_W):
                    dx = dx + wv[k:k + 1, :] * dp[r0 + 3 - k:r0 + 3 - k + CONV_RC, cs]
                    acc[k] = acc[k] + _fold8(dm * _past_window(x_ref, head, r0, k, cs))
                acc[CONV_W] = acc[CONV_W] + _fold8(dm)
                dx_ref[r0:r0 + CONV_RC, cs] = dx.astype(dx_ref.dtype)
            for k in range(CONV_W + 1):
                dwb_ref[k:k + 1, cs] += _colsum(acc[k])

    def main(c):
        return pl.BlockSpec((tr, cb), lambda j, i: (i, c // cb + j))

    def nxt(c):
        return pl.BlockSpec((HALO, cb), lambda j, i: (jnp.minimum((i + 1) * hb, last_h), c // cb + j))

    in_specs = [main(dc0), nxt(dc0)]
    args = [dact, dact]
    if silu:
        in_specs += [main(0), nxt(0)]
        args += [pre, pre]
    in_specs += [main(xc0), pl.BlockSpec((HALO, cb), lambda j, i: (jnp.maximum(i * hb - 1, 0), xc0 // cb + j)),
                 pl.BlockSpec((CONV_W, cb), lambda j, i: (0, j))]
    args += [x, x, w]
    outs, side_outs = _call(
        body, args, name=name, grid=(C // cb, nb), in_specs=in_specs,
        out_specs=[pl.BlockSpec((tr, cb), lambda j, i: (i, j)), pl.BlockSpec((8, cb), lambda j, i: (0, j))],
        out_shape=[jax.ShapeDtypeStruct((L, C), BF16), jax.ShapeDtypeStruct((8, C), F32)],
        scratch_shapes=[pltpu.VMEM((tr + HALO, cb), F32), pltpu.VMEM((HALO + CONV_RC, cb), F32)],
        sem=("parallel", "arbitrary"), side=side, into=into)
    return outs if side is None else (outs, side_outs)


def _ssd_f1(dtraw, dtb, alog):
    q = dtraw.shape[0]
    dt = _softplus(dtraw + dtb)
    adt = dt * (-jnp.exp(alog))
    tril = (lax.broadcasted_iota(jnp.int32, (q, q), 0) >= lax.broadcasted_iota(jnp.int32, (q, q), 1)).astype(F32)
    acs = lax.dot_general(tril, adt, (_NN, ((), ())), precision=lax.Precision.HIGHEST, preferred_element_type=F32)
    return dt, acs


def _ssd_group(g, x, bm, cm, dt, acs, drow, hp):
    q = x.shape[0]
    lane = lax.broadcasted_iota(jnp.int32, (1, 128), 1)
    sub = lax.broadcasted_iota(jnp.int32, (128, 1), 0)
    head_of = lax.broadcasted_iota(jnp.int32, (1, GROUP_W), 1) // HEAD_DIM
    is_last = (lax.broadcasted_iota(jnp.int32, (q, 1), 0) == q - 1).astype(F32)
    causal = lax.broadcasted_iota(jnp.int32, (q, q), 0) >= lax.broadcasted_iota(jnp.int32, (q, q), 1)
    acs_end = jnp.sum(acs * is_last, axis=0, keepdims=True)
    acs_t = acs.T
    dt_exp = jnp.zeros((q, GROUP_W), F32)
    acs_exp = jnp.zeros((q, GROUP_W), F32)
    end_exp = jnp.zeros((1, GROUP_W), F32)
    d_exp = jnp.zeros((1, GROUP_W), F32)
    heads = []
    for k in range(HEADS_PER_GROUP):
        h = HEADS_PER_GROUP * g + k
        oh = (lane == h).astype(F32)
        mk = (head_of == k).astype(F32)
        acs_col = jnp.sum(acs * oh, axis=1, keepdims=True)
        acs_row = jnp.sum(acs_t * (sub == h).astype(F32), axis=0, keepdims=True)
        dt_exp = dt_exp + jnp.sum(dt * oh, axis=1, keepdims=True) * mk
        acs_exp = acs_exp + acs_col * mk
        end_exp = end_exp + jnp.sum(acs_end * oh, axis=1, keepdims=True) * mk
        d_exp = d_exp + jnp.sum(drow * oh, axis=1, keepdims=True) * mk
        heads.append((acs_col, acs_row, mk))
    xdt = x * dt_exp
    states = _dot(bm, xdt * jnp.exp(end_exp - acs_exp), _TN)
    y = _dot(cm, hp, _NN) * jnp.exp(acs_exp) + x * d_exp
    scores = _dot(cm, bm, _NT)
    for acs_col, acs_row, mk in heads:
        dec = jnp.exp(jnp.where(causal, acs_col - acs_row, -jnp.inf))
        y = y + _dot(scores * dec, xdt * mk, _NN)
    return y, hp * jnp.exp(end_exp) + states


def _ssd_fwd(act, dtraw, dtb, alog, drow, *, name, side=None):
    L = act.shape[0]
    q = min(SSD_CHUNK, L)
    nc = L // q
    d_ssd = N_GROUPS * GROUP_W

    def body(act_ref, dt_ref, dtb_ref, alog_ref, drow_ref, y_ref, hst_ref, h):
        @pl.when(pl.program_id(0) == 0)
        def _():
            h[...] = jnp.zeros_like(h)

        dt, acs = _ssd_f1(dt_ref[...], dtb_ref[...], alog_ref[...])
        drow = drow_ref[...]
        for g in range(N_GROUPS):
            hp = h[g]
            hst_ref[0, g] = hp
            y, hn = _ssd_group(g, act_ref[:, g * GROUP_W:(g + 1) * GROUP_W],
                               act_ref[:, d_ssd + g * STATE:d_ssd + (g + 1) * STATE],
                               act_ref[:, d_ssd + (N_GROUPS + g) * STATE:d_ssd + (N_GROUPS + g + 1) * STATE],
                               dt, acs, drow, hp)
            y_ref[:, g * GROUP_W:(g + 1) * GROUP_W] = y
            h[g] = hn

    row = pl.BlockSpec((1, 128), lambda i: (0, 0))
    outs, side_outs = _call(
        body, (act, dtraw, dtb, alog, drow), name=name, grid=(nc,),
        in_specs=[pl.BlockSpec((q, act.shape[1]), lambda i: (i, 0)), pl.BlockSpec((q, 128), lambda i: (i, 0)), row, row, row],
        out_specs=[pl.BlockSpec((q, d_ssd), lambda i: (i, 0)),
                   pl.BlockSpec((1, N_GROUPS, STATE, GROUP_W), lambda i: (i, 0, 0, 0))],
        out_shape=[jax.ShapeDtypeStruct((L, d_ssd), F32), jax.ShapeDtypeStruct((nc, N_GROUPS, STATE, GROUP_W), F32)],
        scratch_shapes=[pltpu.VMEM((N_GROUPS, STATE, GROUP_W), F32)], sem=("arbitrary",), side=side)
    return outs if side is None else (outs, side_outs)


def _ssd_bwd(act, dtraw, dy, hst, dtb, alog, drow, *, name, side=None):
    L = act.shape[0]
    q = min(SSD_CHUNK, L)
    nc = L // q
    d_ssd = N_GROUPS * GROUP_W

    def body(act_ref, dt_ref, dy_ref, hst_ref, dtb_ref, alog_ref, drow_ref, dact_ref, ddt_ref, dpar_ref, dh):
        @pl.when(pl.program_id(0) == 0)
        def _():
            dh[...] = jnp.zeros_like(dh)
            dpar_ref[...] = jnp.zeros_like(dpar_ref)

        (dt, acs), f1_vjp = jax.vjp(_ssd_f1, dt_ref[...], dtb_ref[...], alog_ref[...])
        drow = drow_ref[...]
        ddt = jnp.zeros_like(dt)
        dacs = jnp.zeros_like(acs)
        ddrow = jnp.zeros_like(drow)
        for g in range(N_GROUPS):
            xs = slice(g * GROUP_W, (g + 1) * GROUP_W)
            bs = slice(d_ssd + g * STATE, d_ssd + (g + 1) * STATE)
            cs = slice(d_ssd + (N_GROUPS + g) * STATE, d_ssd + (N_GROUPS + g + 1) * STATE)
            _, f2_vjp = jax.vjp(functools.partial(_ssd_group, g), act_ref[:, xs], act_ref[:, bs], act_ref[:, cs],
                                dt, acs, drow, hst_ref[0, g])
            dx, dbm, dcm, ddt_g, dacs_g, ddrow_g, dhp = f2_vjp((dy_ref[:, xs], dh[g]))
            dact_ref[:, xs] = dx
            dact_ref[:, bs] = dbm
            dact_ref[:, cs] = dcm
            dh[g] = dhp
            ddt, dacs, ddrow = ddt + ddt_g, dacs + dacs_g, ddrow + ddrow_g
        ddtraw, ddtb, dalog = f1_vjp((ddt, dacs))
        ddt_ref[...] = ddtraw
        dpar_ref[0:1, :] += ddtb
        dpar_ref[1:2, :] += dalog
        dpar_ref[2:3, :] += ddrow

    row = pl.BlockSpec((1, 128), lambda i: (0, 0))
    rev = lambda i: (nc - 1 - i, 0)
    outs, side_outs = _call(
        body, (act, dtraw, dy, hst, dtb, alog, drow), name=name, grid=(nc,),
        in_specs=[pl.BlockSpec((q, act.shape[1]), rev), pl.BlockSpec((q, 128), rev), pl.BlockSpec((q, d_ssd), rev),
                  pl.BlockSpec((1, N_GROUPS, STATE, GROUP_W), lambda i: (nc - 1 - i, 0, 0, 0)), row, row, row],
        out_specs=[pl.BlockSpec((q, act.shape[1]), rev), pl.BlockSpec((q, 128), rev), pl.BlockSpec((8, 128), lambda i: (0, 0))],
        out_shape=[jax.ShapeDtypeStruct(act.shape, F32), jax.ShapeDtypeStruct((L, 128), F32), jax.ShapeDtypeStruct((8, 128), F32)],
        scratch_shapes=[pltpu.VMEM((N_GROUPS, STATE, GROUP_W), F32)], sem=("arbitrary",), side=side)
    return outs if side is None else (outs, side_outs)


def _gate_head(xl, wa, ba, wx, bx, lam):
    r = _sigmoid(_dot(xl, wa, _NN) + ba)
    i = _sigmoid(_dot(xl, wx, _NN) + bx)
    log_a = -LRU_C * r * _softplus(-lam)
    return jnp.exp(log_a), jnp.sqrt(-_expm1(2.0 * log_a)) * (i * xl)


def _gate_head_bwd(xl, da, du, wa, ba, wx, bx, lam):
    _, vjp = jax.vjp(_gate_head, xl, wa, ba, wx, bx, lam)
    dxl, dwa, dba, dwx, dbx, dlam = vjp((da, du))
    return dxl, dwa, dba, dwx, dbx, dlam


def _scan_tile(a, b, rows, reverse):
    for d in (1, 2, 4):
        if reverse:
            keep = rows < 8 - d
            a_sh, b_sh = pltpu.roll(a, 8 - d, 0), pltpu.roll(b, 8 - d, 0)
        else:
            keep = rows >= d
            a_sh, b_sh = pltpu.roll(a, d, 0), pltpu.roll(b, d, 0)
        b = b + a * jnp.where(keep, b_sh, 0.0)
        a = a * jnp.where(keep, a_sh, 1.0)
    return a, b


def _lru_scan_fwd(a, u, *, name, tr=512, cb=1024):
    L, C = a.shape
    tr, cb = min(tr, L), min(cb, C)

    def body(a_ref, u_ref, h_ref, hp_ref, carry):
        @pl.when(pl.program_id(1) == 0)
        def _():
            carry[...] = jnp.zeros_like(carry)

        rows = lax.broadcasted_iota(jnp.int32, (8, cb), 0)

        def tile(t, hc):
            r0 = pl.multiple_of(t * 8, 8)
            pa, hb = _scan_tile(a_ref[pl.ds(r0, 8), :], u_ref[pl.ds(r0, 8), :], rows, False)
            h = hb + pa * hc
            h_ref[pl.ds(r0, 8), :] = h
            hp_ref[pl.ds(r0, 8), :] = jnp.where(rows >= 1, pltpu.roll(h, 1, 0), hc)
            return h[7:8, :]

        carry[...] = lax.fori_loop(0, tr // 8, tile, carry[...], unroll=4)

    blk = pl.BlockSpec((tr, cb), lambda j, i: (i, j))
    return pl.pallas_call(
        body, name=name, grid=(C // cb, L // tr),
        in_specs=[blk, blk], out_specs=[blk, blk],
        out_shape=[jax.ShapeDtypeStruct((L, C), F32), jax.ShapeDtypeStruct((L, C), F32)],
        scratch_shapes=[pltpu.VMEM((1, cb), F32)],
        compiler_params=_params(("parallel", "arbitrary")),
    )(a, u)


def _lru_scan_bwd(a, hprev, dh, *, name, tr=512, cb=1024):
    L, C = a.shape
    tr, cb = min(tr, L), min(cb, C)
    nb = L // tr

    def body(a_ref, hp_ref, dh_ref, da_ref, du_ref, carry):
        @pl.when(pl.program_id(1) == 0)
        def _():
            carry[...] = jnp.zeros_like(carry)

        rows = lax.broadcasted_iota(jnp.int32, (8, cb), 0)

        def tile(t, gc):
            r0 = pl.multiple_of((tr // 8 - 1 - t) * 8, 8)
            av, dv = a_ref[pl.ds(r0, 8), :], dh_ref[pl.ds(r0, 8), :]
            pa, gb = _scan_tile(av, av * dv, rows, True)
            big = gb + pa * gc
            g = dv + jnp.where(rows < 7, pltpu.roll(big, 7, 0), gc)
            du_ref[pl.ds(r0, 8), :] = g
            da_ref[pl.ds(r0, 8), :] = g * hp_ref[pl.ds(r0, 8), :]
            return big[0:1, :]

        carry[...] = lax.fori_loop(0, tr // 8, tile, carry[...], unroll=4)

    blk = pl.BlockSpec((tr, cb), lambda j, i: (nb - 1 - i, j))
    return pl.pallas_call(
        body, name=name, grid=(C // cb, nb),
        in_specs=[blk, blk, blk], out_specs=[blk, blk],
        out_shape=[jax.ShapeDtypeStruct((L, C), F32), jax.ShapeDtypeStruct((L, C), F32)],
        scratch_shapes=[pltpu.VMEM((1, cb), F32)],
        compiler_params=_params(("parallel", "arbitrary")),
    )(a, hprev, dh)


def _ssd_gate(y, z, n):
    v = y * _silu(z)
    return v * lax.rsqrt(jnp.mean(v * v, axis=-1, keepdims=True) + EPS) * n


def _ssd_gate_bwd(y, z, dy, n):
    _, vjp = jax.vjp(_ssd_gate, y, z, n)
    return vjp(dy.astype(F32))


def _lru_out(hl, gate, n):
    return _rms(hl * _gelu(gate), n)


def _lru_out_bwd(hl, gate, dy, n):
    _, vjp = jax.vjp(_lru_out, hl, gate, n)
    return vjp(dy.astype(F32))


def _mid(x, mix, pm, pmlp):
    x1 = x + _rms(mix, pm)
    return x1, _rms(x1, pmlp)


def _mid_bwd(x, mix, dx1p, dh2, pm, pmlp):
    _, vjp = jax.vjp(_mid, x, mix, pm, pmlp)
    dx, dmix, dpm, dpmlp = vjp((dx1p, dh2))
    return dmix, dx, dpm, dpmlp


def _loss_bwd(hm2, x1, tgt, g):
    def lossf(hm2, x1, g):
        e = x1 + _rms(hm2, g) - tgt
        return 0.5 * jnp.sum(jnp.mean(e * e, axis=-1, keepdims=True), axis=0, keepdims=True)

    val, vjp = jax.vjp(lossf, hm2, x1, g)
    dhm2, dx1, dg = vjp(jnp.ones((1, 1), F32))
    return dhm2, dx1, dg, val * jnp.ones((1, 128), F32)


def _in_bwd(x, dh_a, dh_b, dx1, g):
    _, vjp = jax.vjp(_rms, x, g)
    dx, dg = vjp(dh_a + dh_b)
    return dx + dx1, dg


def _adamw(w, g, m, v):
    m = ADAM_B1 * m + (1.0 - ADAM_B1) * g
    v = ADAM_B2 * v + (1.0 - ADAM_B2) * (g * g)
    m_hat = m / (1.0 - ADAM_B1 ** ADAM_STEP)
    v_hat = v / (1.0 - ADAM_B2 ** ADAM_STEP)
    return -ADAM_LR * (m_hat / (jnp.sqrt(v_hat) + ADAM_EPS) + ADAM_WD * w), m, v


class _LocalPlan:
    def __init__(self, p):
        self.p, self.large = p, {}

    def weight(self, name):
        return self.p[name]

    def side(self, kernel_name):
        return None

    def done(self, kernel_name, side_outs):
        pass

    def grad(self, name, g):
        self.large[name] = g

    def small(self, grads):
        pass


def _local_step(x, tgt, p, plan):
    L, D = x.shape

    def carry(fn, *args, name, **kw):
        side = plan.side(name)
        if side is None:
            return fn(*args, name=name, **kw)
        outs, side_outs = fn(*args, name=name, side=side, **kw)
        plan.done(name, side_outs)
        return outs

    d_ssd, d_xbc, d_lru, d_mix, d_ff = 2048, 4096, 2048, 4096, 8192
    n_main = d_ssd + d_xbc + 2 * d_lru
    c_xbc, c_gate, c_xl = d_ssd, d_ssd + d_xbc, d_ssd + d_xbc + d_lru
    TR, SUB = 256, 32
    mm = dict(tm=1024, tn=1024, tk=2048)

    (h,) = _rowwise(lambda xt, g: (_rms(xt, g),), [x], [p["pre_mix_norm"]], [(D, BF16)], [], name="f_prenorm", tr=TR, sub=SUB)
    w_main, w_dt = plan.weight("w_main"), plan.weight("w_dt")
    (proj,) = carry(_matmul, h, w_main, mode="nn", m=L, n=n_main, k=D, out_dtypes=[F32], name="f_inproj", **mm)
    (dtraw,) = _matmul(h, w_dt, mode="nn", m=L, n=128, k=D, out_dtypes=[F32], name="f_dtproj", **mm)
    pre, act = carry(_conv_fwd, proj, c_xbc, d_xbc, p["ssd_conv_w"], p["ssd_conv_b"], silu=True, name="f_ssdconv")
    (xl,) = carry(_conv_fwd, proj, c_xl, d_lru, p["lru_conv_w"], p["lru_conv_b"], silu=False, name="f_lruconv")
    yraw, hst = carry(_ssd_fwd, act, dtraw, p["dtb"], p["alog"], p["drow"], name="f_ssd")
    (ycat,) = _grouped(lambda y, z, n: (_ssd_gate(y, z, n),), [yraw, (proj, 0)], [p["ssd_norm"]], [BF16], [],
                       gw=GROUP_W, name="f_ssdgate", tr=TR, into={0: (None, d_mix, 0)})
    gate_p = [p["lru_w_a"], p["lru_b_a"], p["lru_w_x"], p["lru_b_x"], p["lru_lambda"]]
    a, u = carry(_grouped, _gate_head, [xl], gate_p, [F32, F32], [], gw=LRU_BLOCK, name="f_lrugates", tr=TR)
    hl, hprev = _lru_scan_fwd(a, u, name="f_lruscan")
    (ycat,) = _rowwise(lambda ht, gt, n: (_lru_out(ht, gt, n),), [hl, (proj, c_gate, d_lru)], [p["lru_norm"]],
                       [(d_lru, BF16)], [], name="f_lruout", tr=TR, sub=SUB, into={0: (ycat, d_mix, d_ssd // d_lru)})
    w_out = plan.weight("w_out")
    (mix,) = carry(_matmul, ycat, w_out, mode="nn", m=L, n=D, k=d_mix, out_dtypes=[F32], name="f_outproj", **mm)
    x1, h2 = _rowwise(_mid, [x, mix], [p["post_mix_norm"], p["pre_mlp_norm"]], [(D, F32), (D, BF16)], [],
                      name="f_mid", tr=TR, sub=SUB)
    nb_mi = (d_ff // 4) // mm["tn"]
    w_mi = plan.weight("w_mi")
    hm, act2 = carry(_matmul, h2, w_mi, mode="nn", m=L, n=d_ff, k=D, out_dtypes=[BF16, BF16], name="f_mlpin",
                     b_spec=pl.BlockSpec((None, mm["tk"], mm["tn"]), lambda i, j, kk: (j // nb_mi, kk, j % nb_mi)),
                     epilogue=lambda r: (r, jnp.square(jnp.maximum(r, 0.0))), **mm)
    w_mo = plan.weight("w_mo")
    (hm2,) = _matmul(act2, w_mo, mode="nn", m=L, n=D, k=d_ff, out_dtypes=[F32], name="f_mlpout", **mm)

    dhm2, dx1p, d_post_mlp, loss = _rowwise(_loss_bwd, [hm2, x1, tgt], [p["post_mlp_norm"]], [(D, BF16), (D, F32)],
                                            [(1, D), (1, 128)], name="b_loss", tr=TR, sub=SUB)
    (dhm,) = _matmul(dhm2, w_mo, mode="nt", m=L, n=d_ff, k=D, out_dtypes=[BF16], name="b_mlpout_dx", extras=[hm],
                     epilogue=lambda r, hmv: (r * (2.0 * jnp.maximum(hmv.astype(F32), 0.0)),), **mm)
    (dw_mo,) = _matmul(act2, dhm2, mode="tn", m=d_ff, n=D, k=L, out_dtypes=[BF16], name="b_mlpout_dw", **mm)
    plan.grad("w_mlp_out", dw_mo.reshape(4, -1, D))
    kb_mi = (d_ff // 4) // mm["tk"]
    (dh2,) = carry(_matmul, dhm, w_mi, mode="nt", m=L, n=D, k=d_ff, out_dtypes=[F32], name="b_mlpin_dx",
                   b_spec=pl.BlockSpec((None, mm["tn"], mm["tk"]), lambda i, j, kk: (kk // kb_mi, j, kk % kb_mi)), **mm)
    (dw_mi,) = carry(_matmul, h2, dhm, mode="tn", m=D, n=d_ff, k=L, out_dtypes=[BF16], name="b_mlpin_dw",
                     out_specs=[pl.BlockSpec((None, mm["tm"], mm["tn"]), lambda i, j, kk: (j // nb_mi, i, j % nb_mi))],
                     out_shapes=[jax.ShapeDtypeStruct((4, D, d_ff // 4), BF16)], **mm)
    plan.grad("w_mlp_in", dw_mi)
    dmix, dx1, d_post_mix, d_pre_mlp = carry(_rowwise, _mid_bwd, [x, mix, dx1p, dh2], [p["post_mix_norm"], p["pre_mlp_norm"]],
                                             [(D, BF16), (D, F32)], [(1, D), (1, D)], name="b_mid", tr=TR, sub=SUB)
    (dw_out,) = carry(_matmul, ycat, dmix, mode="tn", m=d_mix, n=D, k=L, out_dtypes=[BF16], name="b_outproj_dw", **mm)
    plan.grad("w_out", dw_out.reshape(4, -1, D))
    (dycat,) = carry(_matmul, dmix, w_out, mode="nt", m=L, n=d_mix, k=D, out_dtypes=[BF16], name="b_outproj_dx", **mm)
    dhl, dproj, d_lru_norm = _rowwise(_lru_out_bwd, [hl, (proj, c_gate, d_lru), (dycat, d_ssd, d_lru)], [p["lru_norm"]],
                                      [(d_lru, F32), (d_lru, BF16)], [(1, d_lru)], name="b_lruout", tr=TR, sub=SUB,
                                      into={1: (None, n_main, c_gate // d_lru)})
    da, du = _lru_scan_bwd(a, hprev, dhl, name="b_lruscan")
    dxl, d_wa, d_ba, d_wx, d_bx, d_lam = _grouped(
        _gate_head_bwd, [xl, da, du], gate_p, [F32],
        [(LRU_HEADS, LRU_BLOCK, LRU_BLOCK), (1, d_lru), (LRU_HEADS, LRU_BLOCK, LRU_BLOCK), (1, d_lru), (1, d_lru)],
        gw=LRU_BLOCK, name="b_lrugates", tr=TR)
    CB = 1024
    dproj, dwb_lru = _conv_bwd(dxl, 0, None, proj, c_xl, d_lru, p["lru_conv_w"], silu=False, name="b_lruconv", cb=CB,
                               into={0: (dproj, n_main, c_xl // CB)})
    dyraw, dproj, d_ssd_norm = _grouped(_ssd_gate_bwd, [yraw, (proj, 0), (dycat, 0)], [p["ssd_norm"]], [F32, BF16], [(1, d_ssd)],
                                        gw=GROUP_W, name="b_ssdgate", tr=TR, into={1: (dproj, n_main, 0)})
    dact, ddtraw, dpar = carry(_ssd_bwd, act, dtraw, dyraw, hst, p["dtb"], p["alog"], p["drow"], name="b_ssd")
    dproj, dwb_ssd = carry(_conv_bwd, dact, 0, pre, proj, c_xbc, d_xbc, p["ssd_conv_w"], silu=True, name="b_ssdconv", cb=CB,
                           into={0: (dproj, n_main, c_xbc // CB)})
    early = dict(loss=loss, ssd_conv=dwb_ssd, ssd_par=dpar, ssd_norm=d_ssd_norm, lru_conv=dwb_lru, lru_w_a=d_wa, lru_b_a=d_ba,
                 lru_w_x=d_wx, lru_b_x=d_bx, lru_lambda=d_lam, lru_norm=d_lru_norm, post_mix_norm=d_post_mix,
                 pre_mlp_norm=d_pre_mlp, post_mlp_norm=d_post_mlp)
    plan.small(early)
    (dw_main,) = carry(_matmul, h, dproj, mode="tn", m=D, n=n_main, k=L, out_dtypes=[BF16], name="b_inproj_dw", **mm)
    (dw_dt,) = _matmul(h, ddtraw, mode="tn", m=D, n=128, k=L, out_dtypes=[BF16], name="b_dtproj_dw", **mm)
    plan.grad("w_in", (dw_main, dw_dt))
    (dh_a,) = carry(_matmul, dproj, w_main, mode="nt", m=L, n=D, k=n_main, out_dtypes=[F32], name="b_inproj_dx", **mm)
    (dh_b,) = _matmul(ddtraw, w_dt, mode="nt", m=L, n=D, k=128, out_dtypes=[F32], name="b_dtproj_dx", **mm)
    grad_x, d_pre_mix = carry(_rowwise, _in_bwd, [x, dh_a, dh_b, dx1], [p["pre_mix_norm"]], [(D, F32)], [(1, D)],
                              name="b_prenorm", tr=TR, sub=SUB)

    return grad_x, dict(early, pre_mix_norm=d_pre_mix)


def _place():
    return lax.axis_index("x"), lax.axis_index("y"), lax.axis_index("c")


def _other_chips(x, y):
    return [(1 - x, y), (x, 1 - y), (1 - x, 1 - y)]


def _allgather8(blk, *, name):
    r, n = blk.shape

    def body(x_ref, out_ref, send_sems, recv_sems, local_sem):
        x, y, c = _place()
        me, sibling = (x, y, c), (x, y, 1 - c)
        chips = _other_chips(x, y)

        def rows(px, py, pc):
            return out_ref.at[pl.ds((4 * px + 2 * py + pc) * r, r), :]

        def copy(k, block, to, src=None):
            return pltpu.make_async_remote_copy(
                src_ref=rows(*block) if src is None else src, dst_ref=rows(*block),
                send_sem=send_sems.at[k], recv_sem=recv_sems.at[k], device_id=to, device_id_type=MESH)

        mine = pltpu.make_async_copy(x_ref, rows(*me), local_sem)
        mine.start()
        first = [copy(0, me, sibling, src=x_ref)]
        first += [copy(1 + k, me, (*chip, c), src=x_ref) for k, chip in enumerate(chips)]
        for cp in first:
            cp.start()
        passed = [copy(4 + k, (*chip, c), sibling) for k, chip in enumerate(chips)]
        for k, chip in enumerate(chips):
            copy(1 + k, (*chip, c), me).wait_recv()
            passed[k].start()
        copy(0, sibling, me).wait_recv()
        for k, chip in enumerate(chips):
            copy(4 + k, (*chip, 1 - c), me).wait_recv()
        for cp in first + passed:
            cp.wait_send()
        mine.wait()

    return pl.pallas_call(
        body, name=name,
        out_shape=jax.ShapeDtypeStruct((8 * r, n), blk.dtype),
        in_specs=[pl.BlockSpec(memory_space=pltpu.VMEM)], out_specs=pl.BlockSpec(memory_space=pltpu.VMEM),
        scratch_shapes=[pltpu.SemaphoreType.DMA((7,)), pltpu.SemaphoreType.DMA((7,)), pltpu.SemaphoreType.DMA],
        compiler_params=pltpu.CompilerParams(vmem_limit_bytes=VMEM_LIMIT),
    )(blk)


def _sum8(g, *, name):
    _, r, n = g.shape
    tr = max(t for t in range(8, min(r, 512) + 1, 8) if r % t == 0)

    def body(g_ref, o_ref):
        s = g_ref[0]
        for k in range(1, 8):
            s = s + g_ref[k]
        o_ref[...] = s

    return pl.pallas_call(
        body, name=name, grid=(r // tr,),
        in_specs=[pl.BlockSpec((8, tr, n), lambda i: (0, i, 0))], out_specs=pl.BlockSpec((tr, n), lambda i: (i, 0)),
        out_shape=jax.ShapeDtypeStruct((r, n), g.dtype), compiler_params=_params(("parallel",)),
    )(g)


def _blocks(fn, ins, outs, *, grid, name, prefetch=None, aliases=None):
    n_in = len(ins)

    def body(*refs):
        if prefetch is not None:
            refs = refs[1:]
        res = fn(*[r[...] for r in refs[:n_in]])
        for o_ref, o in zip(refs[n_in:], res):
            o_ref[...] = o.astype(o_ref.dtype)

    in_specs = [pl.BlockSpec(b, m) for _, b, m in ins]
    out_specs = [pl.BlockSpec(b, m) for _, b, m in outs]
    kw = dict(name=name, out_shape=[s for s, _, _ in outs], input_output_aliases=aliases or {},
              compiler_params=_params(("arbitrary",) * len(grid)))
    arrs = [a for a, _, _ in ins]
    if prefetch is None:
        return pl.pallas_call(body, grid=grid, in_specs=in_specs, out_specs=out_specs, **kw)(*arrs)
    spec = pltpu.PrefetchScalarGridSpec(num_scalar_prefetch=1, grid=grid, in_specs=in_specs, out_specs=out_specs)
    return pl.pallas_call(body, grid_spec=spec, **kw)(prefetch, *arrs)


def _gather_tree(slot, *, name):
    def body(i_ref, g_ref, send_sems, recv_sems):
        x, y, c = _place()
        j, jx, jy, jd = 2 * x + y, 2 * (1 - x) + y, 2 * x + (1 - y), 2 * (1 - x) + (1 - y)
        xn, yn, sibling = (1 - x, y, c), (x, 1 - y, c), (x, y, 1 - c)
        rp = i_ref.shape[1] // 4

        def piece(ref, chip, hc, q):
            return ref.at[chip, pl.ds((2 * hc + q) * rp, rp), :]

        def cp(k, src, dst, to):
            return pltpu.make_async_remote_copy(src_ref=src, dst_ref=dst, send_sem=send_sems.at[k], recv_sem=recv_sems.at[k],
                                                device_id=to, device_id_type=MESH)

        def landed(k, chip, q):
            return cp(k, piece(g_ref, chip, c, q), piece(g_ref, chip, c, q), xn)

        sends = []

        def go(d):
            d.start()
            sends.append(d)

        go(cp(0, piece(i_ref, j, c, 0), piece(g_ref, j, c, 0), xn))
        go(cp(2, piece(i_ref, j, c, 1), piece(g_ref, j, c, 1), yn))
        go(cp(1, piece(i_ref, j, c, 1), piece(g_ref, j, c, 1), xn))
        go(cp(3, piece(i_ref, j, c, 0), piece(g_ref, j, c, 0), yn))
        arrivals = [(0, jx, 0, 4, yn), (2, jy, 1, 5, xn), (1, jx, 1, None, None), (3, jy, 0, None, None),
                    (4, jd, 0, None, None), (5, jd, 1, None, None)]
        for n_arr, (k, chip, q, k_fwd, to) in enumerate(arrivals):
            landed(k, chip, q).wait_recv()
            if k_fwd is not None:
                go(cp(k_fwd, piece(g_ref, chip, c, q), piece(g_ref, chip, c, q), to))
            go(cp(6 + n_arr, piece(g_ref, chip, c, q), piece(g_ref, chip, c, q), sibling))
        for n_arr, (_, chip, q, _, _) in enumerate(arrivals):
            other = piece(g_ref, chip, 1 - c, q)
            cp(6 + n_arr, other, other, sibling).wait_recv()
        for d in sends:
            d.wait_send()

    return pl.pallas_call(
        body, name=name, out_shape=jax.ShapeDtypeStruct(slot.shape, slot.dtype),
        in_specs=[ANY], out_specs=ANY, input_output_aliases={0: 0},
        scratch_shapes=[pltpu.SemaphoreType.DMA((12,)), pltpu.SemaphoreType.DMA((12,))],
    )(slot)


def _pack(arrs):
    parts = []
    for v in arrs:
        f = v.reshape(-1)
        f = jnp.pad(f, (0, (-f.shape[0]) % 1024))
        parts.append(f.reshape(-1, 128))
    return jnp.concatenate(parts, axis=0)


def _unpack(packed, shapes):
    out, r0 = [], 0
    for s in shapes:
        size = 1
        for d in s:
            size *= d
        nr = (size + 1023) // 1024 * 8
        out.append(packed[r0:r0 + nr].reshape(-1)[:size].reshape(s))
        r0 += nr
    return out


SMALL_GRADS = [("loss", (1, 128)), ("pre_mix_norm", (1, 2048)), ("ssd_conv", (8, 4096)), ("ssd_par", (8, 128)),
               ("ssd_norm", (1, 2048)), ("lru_conv", (8, 2048)), ("lru_w_a", (16, 128, 128)), ("lru_b_a", (1, 2048)),
               ("lru_w_x", (16, 128, 128)), ("lru_b_x", (1, 2048)), ("lru_lambda", (1, 2048)), ("lru_norm", (1, 2048)),
               ("post_mix_norm", (1, 2048)), ("pre_mlp_norm", (1, 2048)), ("post_mlp_norm", (1, 2048))]

SMALL_EARLY = [g for g in SMALL_GRADS if g[0] != "pre_mix_norm"]

WEIGHTS = ['pre_mix_norm', 'w_in', 'ssd_conv_w', 'ssd_conv_b', 'ssd_dt_bias', 'ssd_a_log', 'ssd_d', 'ssd_norm', 'lru_conv_w',
           'lru_conv_b', 'lru_w_a', 'lru_b_a', 'lru_w_x', 'lru_b_x', 'lru_lambda', 'lru_norm', 'w_out', 'post_mix_norm',
           'pre_mlp_norm', 'w_mlp_in', 'w_mlp_out', 'post_mlp_norm']
LARGE = ['w_in', 'w_out', 'w_mlp_in', 'w_mlp_out']

D_SSD, D_XBC, DT_W = 2048, 4096, 32
TB = 256


def _w_in_runs(n_shard, n_main):
    c_dt = D_SSD + D_XBC
    runs, p = [], 0
    while p < n_main:
        j, off = divmod(p if p < c_dt else p + DT_W, n_shard)
        ln = min(n_shard - off, (c_dt if p < c_dt else n_main) - p)
        runs.append((p, j, off, ln))
        p += ln
    jd, offd = divmod(c_dt, n_shard)
    assert offd + DT_W <= n_shard
    return runs, (jd, offd)


def _pack_w_in(slots, *, name):
    _, d, n_shard = slots.shape
    n_main = 4 * n_shard - DT_W
    runs, (jd, offd) = _w_in_runs(n_shard, n_main)

    def body(s_ref, main_ref, dt_ref):
        for p, j, off, ln in runs:
            main_ref[:, p:p + ln] = s_ref[j, :, off:off + ln]
        dt_ref[:, 0:DT_W] = s_ref[jd, :, offd:offd + DT_W]
        dt_ref[:, DT_W:] = jnp.zeros((TB, 128 - DT_W), dt_ref.dtype)

    return pl.pallas_call(
        body, name=name, grid=(d // TB,),
        in_specs=[pl.BlockSpec((4, TB, n_shard), lambda i: (0, i, 0))],
        out_specs=[pl.BlockSpec((TB, n_main), lambda i: (i, 0)), pl.BlockSpec((TB, 128), lambda i: (i, 0))],
        out_shape=[jax.ShapeDtypeStruct((d, n_main), slots.dtype), jax.ShapeDtypeStruct((d, 128), slots.dtype)],
        compiler_params=_params(("parallel",)),
    )(slots)


def _unpack_dw_in(dw_main, dw_dt, *, name):
    d, n_main = dw_main.shape
    n_shard = (n_main + DT_W) // 4
    runs, (jd, offd) = _w_in_runs(n_shard, n_main)

    def body(main_ref, dt_ref, o_ref):
        for p, j, off, ln in runs:
            o_ref[j, :, off:off + ln] = main_ref[:, p:p + ln]
        o_ref[jd, :, offd:offd + DT_W] = dt_ref[:, 0:DT_W]

    return pl.pallas_call(
        body, name=name, grid=(d // TB,),
        in_specs=[pl.BlockSpec((TB, n_main), lambda i: (i, 0)), pl.BlockSpec((TB, 128), lambda i: (i, 0))],
        out_specs=pl.BlockSpec((4, TB, n_shard), lambda i: (0, i, 0)),
        out_shape=jax.ShapeDtypeStruct((4, d, n_shard), dw_main.dtype),
        compiler_params=_params(("parallel",)),
    )(dw_main, dw_dt)


def _half(ref, chip_idx, hc, piece=(0, 1)):
    q, nq = piece
    rp = ref.shape[1] // (2 * nq)
    return ref.at[chip_idx, pl.ds((hc * nq + q) * rp, rp), :]


def _job_gather_ici(i_ref, g_ref, send_sems, recv_sems, base, piece):
    x, y, c = _place()
    j = 2 * x + y
    sends, recvs = [], []
    for k, (px, py) in enumerate(_other_chips(x, y)):
        kw = dict(send_sem=send_sems.at[base + k], recv_sem=recv_sems.at[base + k], device_id=(px, py, c), device_id_type=MESH)
        sends.append(pltpu.make_async_remote_copy(src_ref=_half(i_ref, j, c, piece), dst_ref=_half(g_ref, j, c, piece), **kw))
        landed = _half(g_ref, 2 * px + py, c, piece)
        recvs.append(pltpu.make_async_remote_copy(src_ref=landed, dst_ref=landed, **kw))
    return sends, recvs


def _job_gather_sibling(i_ref, g_ref, send_sems, recv_sems, base, piece):
    x, y, c = _place()
    sends, recvs = [], []
    for k, (px, py) in enumerate(_other_chips(x, y)):
        kw = dict(send_sem=send_sems.at[base + k], recv_sem=recv_sems.at[base + k], device_id=(x, y, 1 - c), device_id_type=MESH)
        sends.append(pltpu.make_async_remote_copy(src_ref=_half(i_ref, 2 * px + py, c, piece),
                                                  dst_ref=_half(g_ref, 2 * px + py, c, piece), **kw))
        other = _half(g_ref, 2 * px + py, 1 - c, piece)
        recvs.append(pltpu.make_async_remote_copy(src_ref=other, dst_ref=other, **kw))
    return sends, recvs


def _job_reduce_ici(s_ref, got_refs, send_sems, recv_sems, base, piece):
    x, y, c = _place()
    q, nq = piece
    rp = s_ref.shape[1] // nq
    rows = pl.ds(q * rp, rp)
    sends = [pltpu.make_async_remote_copy(src_ref=s_ref.at[2 * px + py, rows, :], dst_ref=got_refs[k].at[rows, :],
                                          send_sem=send_sems.at[base + k], recv_sem=recv_sems.at[base + k],
                                          device_id=(px, py, c), device_id_type=MESH)
             for k, (px, py) in enumerate(_other_chips(x, y))]
    return sends, sends


GATHER_PLAN = {
    "f_inproj": [("w_out", _job_gather_ici, (0, 1)), ("w_mlp_in", _job_gather_ici, (0, 2))],
    "f_ssdconv": [("w_mlp_in", _job_gather_ici, (1, 2)), ("w_out", _job_gather_sibling, (0, 1)),
                  ("w_mlp_in", _job_gather_sibling, (0, 2))],
    "f_lruconv": [("w_mlp_in", _job_gather_sibling, (1, 2))],
    "f_ssd": [("w_mlp_out", _job_gather_ici, (0, 2))],
    "f_lrugates": [("w_mlp_out", _job_gather_sibling, (0, 2))],
    "f_outproj": [("w_mlp_out", _job_gather_ici, (1, 2))],
    "f_mlpin": [("w_mlp_out", _job_gather_sibling, (1, 2))],
}
SIBLING_PLAN = {"b_mlpin_dx": "w_mlp_out", "b_mid": "w_mlp_in", "b_outproj_dx": "w_out"}
REDUCE_PLAN = {
    "b_mlpin_dw": [("w_mlp_out", (0, 2))],
    "b_outproj_dw": [("w_mlp_out", (1, 2))],
    "b_ssd": [("w_mlp_in", (0, 1)), ("w_out", (0, 1))],
    "b_inproj_dx": [("w_in", (0, 1))],
}
SHARE_PLAN = {"b_ssdconv": ["w_mlp_out", "w_mlp_in", "w_out"]}
SMALL_PLAN = {"b_inproj_dw": 0, "b_inproj_dx": 1}


class _Part:
    def __init__(self, ins, outs, aliases, n_sems, make, done):
        self.ins, self.outs, self.aliases, self.n_sems, self.make, self.done = ins, outs, aliases, n_sems, make, done


def _merge_parts(parts):
    ins, outs, aliases, offs, n = [], [], {}, [], 0
    for p in parts:
        offs.append((len(ins), len(outs), n))
        aliases.update({len(ins) + i: len(outs) + o for i, o in p.aliases.items()})
        ins, outs, n = ins + list(p.ins), outs + list(p.outs), n + p.n_sems

    def make(i_refs, o_refs, send_sems, recv_sems):
        sends, recvs = [], []
        for p, (io, oo, so) in zip(parts, offs):
            s, r = p.make(i_refs[io:io + len(p.ins)], o_refs[oo:oo + len(p.outs)], send_sems, recv_sems, so)
            sends, recvs = sends + s, recvs + r
        return sends, recvs

    return _Side(ins, outs, aliases, n, make)


def _parts_done(parts, outs):
    o = 0
    for p in parts:
        p.done(list(outs[o:o + len(p.outs)]))
        o += len(p.outs)


def _comm_call(parts, *, name):
    side = _merge_parts(parts)
    si, so = len(side.ins), len(side.outs)

    def body(*refs):
        sends, recvs = side.make(refs[:si], refs[si:si + so], refs[-2], refs[-1])
        for d in sends:
            d.start()
        for d in recvs:
            d.wait_recv()
        for d in sends:
            d.wait_send()

    outs = pl.pallas_call(
        body, name=name, out_shape=list(side.outs), in_specs=[ANY] * si, out_specs=[ANY] * so,
        scratch_shapes=[pltpu.SemaphoreType.DMA((side.n_sems,)), pltpu.SemaphoreType.DMA((side.n_sems,))],
        input_output_aliases=side.aliases,
    )(*side.ins)
    _parts_done(parts, outs)


class _DistPlan:
    def __init__(self, slots, w_main, w_dt, where, shapes):
        self.slots, self.w_main, self.w_dt, self.where, self.shapes = slots, w_main, w_dt, where, shapes
        self.sib_pending, self.partial, self.got, self.totals, self.shared = {}, {}, {}, {}, {}
        self.small_buf = None

    def weight(self, name):
        d = self.w_main.shape[0]
        return {"w_main": lambda: self.w_main, "w_dt": lambda: self.w_dt, "w_out": lambda: self.slots["w_out"].reshape(-1, d),
                "w_mi": lambda: self.slots["w_mlp_in"], "w_mo": lambda: self.slots["w_mlp_out"].reshape(-1, d)}[name]()

    def _gather_part(self, jobs):
        names = []
        for n, _, _ in jobs:
            if n not in names:
                names.append(n)
        arrs = [self.slots[n] for n in names]

        def make(i_refs, o_refs, send_sems, recv_sems, base):
            sends, recvs = [], []
            for q, (n, job, piece) in enumerate(jobs):
                s, r = job(i_refs[names.index(n)], o_refs[names.index(n)], send_sems, recv_sems, base + 3 * q, piece)
                sends, recvs = sends + s, recvs + r
            return sends, recvs

        return _Part(arrs, [jax.ShapeDtypeStruct(a.shape, a.dtype) for a in arrs], {i: i for i in range(len(arrs))},
                     3 * len(jobs), make, lambda outs: self.slots.update(zip(names, outs)))

    def _sibling_part(self, name):
        g = self.sib_pending.pop(name)
        _, _, rh, cc = g.shape

        def make(i_refs, o_refs, send_sems, recv_sems, base):
            x, y, c = _place()
            sends = [pltpu.make_async_remote_copy(src_ref=i_refs[0].at[s, 1 - c], dst_ref=o_refs[0].at[s],
                                                  send_sem=send_sems.at[base + s], recv_sem=recv_sems.at[base + s],
                                                  device_id=(x, y, 1 - c), device_id_type=MESH) for s in range(4)]
            return sends, sends

        def done(outs):
            (t,) = outs
            self.partial[name] = _blocks(
                lambda u, v: (u.astype(F32) + v.astype(F32),),
                [(g, (None, None, TB, cc), lambda q, i, s: (q, s[1], i, 0)), (t, (None, TB, cc), lambda q, i, s: (q, i, 0))],
                [(jax.ShapeDtypeStruct(t.shape, BF16), (None, TB, cc), lambda q, i, s: (q, i, 0))],
                grid=(4, rh // TB), name="add_sibling_" + name, prefetch=self.where)[0]

        return _Part([g], [jax.ShapeDtypeStruct((4, rh, cc), g.dtype)], {}, 4, make, done)

    def _reduce_part(self, items):
        parts = [self.partial[n] for n, _ in items]
        ins, aliases = list(parts), {}
        for q, (n, (piece, _)) in enumerate(items):
            if piece > 0:
                for k in range(3):
                    aliases[len(ins)] = 3 * q + k
                    ins.append(self.got[n][k])

        def make(i_refs, o_refs, send_sems, recv_sems, base):
            sends = []
            for q, (_, piece) in enumerate(items):
                sends += _job_reduce_ici(i_refs[q], o_refs[3 * q:3 * q + 3], send_sems, recv_sems, base + 3 * q, piece)[0]
            return sends, sends

        def done(outs):
            for q, (name, (piece, n_pieces)) in enumerate(items):
                g3 = self.got[name] = outs[3 * q:3 * q + 3]
                if piece < n_pieces - 1:
                    continue
                s4 = self.partial[name]
                _, rh, cc = s4.shape
                row = ((TB, cc), lambda i, s: (i, 0))
                self.totals[name] = _blocks(
                    lambda o, r0, r1, r2: (((o.astype(F32) + r0.astype(F32)) + r1.astype(F32)) + r2.astype(F32),),
                    [(s4, (None, TB, cc), lambda i, s: (s[0], i, 0)), (g3[0],) + row, (g3[1],) + row, (g3[2],) + row],
                    [(jax.ShapeDtypeStruct((2, rh, cc), F32), (None, TB, cc), lambda i, s: (s[1], i, 0))],
                    grid=(rh // TB,), name="add_chips_" + name, prefetch=self.where)[0]
                if name == "w_in":
                    _comm_call([self._share_part([name])], name="share_" + name)

        outs = [jax.ShapeDtypeStruct(s.shape[1:], s.dtype) for s in parts for _ in range(3)]
        return _Part(ins, outs, aliases, 3 * len(items), make, done)

    def _share_part(self, names):
        ts = [self.totals[n] for n in names]

        def make(i_refs, o_refs, send_sems, recv_sems, base):
            x, y, c = _place()
            sends, recvs = [], []
            for w in range(len(ts)):
                kw = dict(send_sem=send_sems.at[base + w], recv_sem=recv_sems.at[base + w], device_id=(x, y, 1 - c),
                          device_id_type=MESH)
                sends.append(pltpu.make_async_remote_copy(src_ref=i_refs[w].at[c], dst_ref=o_refs[w].at[c], **kw))
                recvs.append(pltpu.make_async_remote_copy(src_ref=i_refs[w].at[c], dst_ref=o_refs[w].at[1 - c], **kw))
            return sends, recvs

        def done(outs):
            self.shared.update({n: o.reshape(self.shapes[n]) for n, o in zip(names, outs)})

        return _Part(ts, [jax.ShapeDtypeStruct(t.shape, t.dtype) for t in ts], {i: i for i in range(len(ts))}, len(ts), make, done)

    def _small_part(self, stage):
        buf = self.small_buf

        def make(i_refs, o_refs, send_sems, recv_sems, base):
            x, y, c = _place()
            src, dst = i_refs[0], o_refs[0]
            sends, recvs = [], []

            def add(k, block, to, frm):
                kw = dict(send_sem=send_sems.at[base + k], recv_sem=recv_sems.at[base + k], device_id=to, device_id_type=MESH)
                sends.append(pltpu.make_async_remote_copy(src_ref=src.at[block], dst_ref=dst.at[block], **kw))
                recvs.append(pltpu.make_async_remote_copy(src_ref=src.at[frm], dst_ref=dst.at[frm], **kw))

            if stage == 0:
                add(0, 4 * x + 2 * y + c, (x, y, 1 - c), 4 * x + 2 * y + 1 - c)
            for k, (px, py) in enumerate(_other_chips(x, y)):
                if stage == 0:
                    add(1 + k, 4 * x + 2 * y + c, (px, py, c), 4 * px + 2 * py + c)
                else:
                    add(k, 4 * px + 2 * py + c, (x, y, 1 - c), 4 * px + 2 * py + 1 - c)
            return sends, recvs

        def done(outs):
            (self.small_buf,) = outs

        return _Part([buf], [jax.ShapeDtypeStruct(buf.shape, buf.dtype)], {0: 0}, 4 if stage == 0 else 3, make, done)

    def side(self, kernel_name):
        parts = []
        if kernel_name in GATHER_PLAN:
            parts.append(self._gather_part(GATHER_PLAN[kernel_name]))
        if kernel_name in SIBLING_PLAN:
            parts.append(self._sibling_part(SIBLING_PLAN[kernel_name]))
        if kernel_name in REDUCE_PLAN:
            parts.append(self._reduce_part(REDUCE_PLAN[kernel_name]))
        if kernel_name in SHARE_PLAN:
            parts.append(self._share_part(SHARE_PLAN[kernel_name]))
        if kernel_name in SMALL_PLAN:
            parts.append(self._small_part(SMALL_PLAN[kernel_name]))
        self._carried = parts
        return _merge_parts(parts) if parts else None

    def done(self, kernel_name, side_outs):
        _parts_done(self._carried, side_outs)

    def grad(self, name, g):
        if name == "w_in":
            g = _unpack_dw_in(*g, name="unpack_dw_in")
        self.sib_pending[name] = g.reshape(4, 2, g.shape[1] // 2, g.shape[2])
        if name == "w_in":
            _comm_call([self._sibling_part(name)], name="reduce_sibling_" + name)

    def small(self, grads):
        packed = _pack([grads[n] for n, _ in SMALL_EARLY])
        n_rows = packed.shape[0]
        self.small_buf = _blocks(lambda t: (t,), [(packed, (n_rows, 128), lambda i, s: (0, 0))],
                                 [(jax.ShapeDtypeStruct((8, n_rows, 128), F32), (None, n_rows, 128), lambda i, s: (s[2], 0, 0))],
                                 grid=(1,), name="place_small", prefetch=self.where)[0]

    def finish(self):
        assert len(self.shared) == len(self.shapes)
        return self.shared


def kernel(x, pre_mix_norm, w_in, ssd_conv_w, ssd_conv_b, ssd_dt_bias, ssd_a_log, ssd_d, ssd_norm, lru_conv_w, lru_conv_b, lru_w_a, lru_b_a, lru_w_x, lru_b_x, lru_lambda, lru_norm, w_out, post_mix_norm, pre_mlp_norm, w_mlp_in, w_mlp_out, post_mlp_norm, loss_target, m_pre_mix_norm, m_w_in, m_ssd_conv_w, m_ssd_conv_b, m_ssd_dt_bias, m_ssd_a_log, m_ssd_d, m_ssd_norm, m_lru_conv_w, m_lru_conv_b, m_lru_w_a, m_lru_b_a, m_lru_w_x, m_lru_b_x, m_lru_lambda, m_lru_norm, m_w_out, m_post_mix_norm, m_pre_mlp_norm, m_w_mlp_in, m_w_mlp_out, m_post_mlp_norm, v_pre_mix_norm, v_w_in, v_ssd_conv_w, v_ssd_conv_b, v_ssd_dt_bias, v_ssd_a_log, v_ssd_d, v_ssd_norm, v_lru_conv_w, v_lru_conv_b, v_lru_w_a, v_lru_b_a, v_lru_w_x, v_lru_b_x, v_lru_lambda, v_lru_norm, v_w_out, v_post_mix_norm, v_pre_mlp_norm, v_w_mlp_in, v_w_mlp_out, v_post_mlp_norm):
    a = dict(locals())
    j = 2 * lax.axis_index("x") + lax.axis_index("y")
    D = x.shape[-1]
    c_dt = D_SSD + D_XBC

    core = lax.axis_index("c")
    where = jnp.stack([j, core, 2 * j + core]).astype(jnp.int32)
    slots = {}
    for name in LARGE:
        w = a[name][0]
        r, cc = w.shape
        slots[name] = _blocks(lambda t: (t,), [(w, (TB, cc), lambda i, s: (i, 0))],
                              [(jax.ShapeDtypeStruct((4, r, cc), BF16), (None, TB, cc), lambda i, s: (s[0], i, 0))],
                              grid=(r // TB,), name="cast_" + name, prefetch=where)[0]
    g_in = _gather_tree(slots.pop("w_in"), name="gather_w_in")
    w_main, w_dt = _pack_w_in(g_in, name="pack_w_in")
    taps = jnp.concatenate([ssd_conv_w[0].reshape(-1, 128), lru_conv_w[0].reshape(-1, 128)], axis=0)
    taps = _allgather8(taps, name="gather_taps").reshape(8, taps.shape[0], 128)[0::2]
    n_ssd = ssd_conv_w.shape[1] * ssd_conv_w.shape[2] // 128
    ssd_taps = taps[:, :n_ssd].reshape(4, CONV_W, -1).transpose(1, 0, 2).reshape(CONV_W, -1)
    lru_taps = taps[:, n_ssd:].reshape(4, CONV_W, -1).transpose(1, 0, 2).reshape(CONV_W, -1)

    def row128(v):
        return jnp.pad(v, ((0, 0), (0, 128 - v.shape[1])))

    p = dict(pre_mix_norm=pre_mix_norm, ssd_conv_w=ssd_taps, ssd_conv_b=ssd_conv_b,
             dtb=row128(ssd_dt_bias), alog=row128(ssd_a_log), drow=row128(ssd_d), ssd_norm=ssd_norm,
             lru_conv_w=lru_taps, lru_conv_b=lru_conv_b, lru_w_a=lru_w_a[0], lru_b_a=lru_b_a.reshape(1, -1),
             lru_w_x=lru_w_x[0], lru_b_x=lru_b_x.reshape(1, -1), lru_lambda=lru_lambda, lru_norm=lru_norm,
             post_mix_norm=post_mix_norm, pre_mlp_norm=pre_mlp_norm, post_mlp_norm=post_mlp_norm)
    plan = _DistPlan(slots, w_main, w_dt, where, {n: a[n].shape for n in LARGE})
    grad_x, small = _local_step(x[0], loss_target[0], p, plan)

    large_grads = plan.finish()

    delta, new_m, new_v = {}, {}, {}
    for name in LARGE:
        w = a[name][0]
        cc = w.shape[1]
        outs = _rowwise(_adamw, [w, large_grads[name][0], a["m_" + name][0], a["v_" + name][0]], [], [(cc, F32)] * 3, [],
                        name="adamw_" + name, tr=128, sub=8)
        delta[name], new_m[name], new_v[name] = [o[None] for o in outs]

    total = _sum8(plan.small_buf, name="sum_small")
    tot = dict(zip([n for n, _ in SMALL_EARLY], _unpack(total, [s for _, s in SMALL_EARLY])))
    late = small["pre_mix_norm"].reshape(-1, 128)
    late = _allgather8(late, name="gather_late").reshape(8, late.shape[0], 128)
    tot["pre_mix_norm"] = _sum8(late, name="sum_late").reshape(small["pre_mix_norm"].shape)
    loss = tot["loss"][0, 0]
    n_sc, n_lc = ssd_conv_w.shape[2], lru_conv_w.shape[2]
    grads = dict(
        pre_mix_norm=tot["pre_mix_norm"],
        ssd_conv_w=lax.dynamic_slice(tot["ssd_conv"][:CONV_W], (0, j * n_sc), (CONV_W, n_sc))[None],
        ssd_conv_b=tot["ssd_conv"][CONV_W:CONV_W + 1],
        ssd_dt_bias=tot["ssd_par"][0:1, :DT_W], ssd_a_log=tot["ssd_par"][1:2, :DT_W], ssd_d=tot["ssd_par"][2:3, :DT_W],
        ssd_norm=tot["ssd_norm"],
        lru_conv_w=lax.dynamic_slice(tot["lru_conv"][:CONV_W], (0, j * n_lc), (CONV_W, n_lc))[None],
        lru_conv_b=tot["lru_conv"][CONV_W:CONV_W + 1],
        lru_w_a=tot["lru_w_a"][None], lru_b_a=tot["lru_b_a"].reshape(lru_b_a.shape),
        lru_w_x=tot["lru_w_x"][None], lru_b_x=tot["lru_b_x"].reshape(lru_b_x.shape),
        lru_lambda=tot["lru_lambda"], lru_norm=tot["lru_norm"], post_mix_norm=tot["post_mix_norm"],
        pre_mlp_norm=tot["pre_mlp_norm"], post_mlp_norm=tot["post_mlp_norm"])

    grads.update(large_grads)

    small_w = [n for n in WEIGHTS if n not in LARGE]
    n_sw = len(small_w)

    def flat(v):
        return v.reshape(-1, v.shape[-1])

    def adamw_all(*refs):
        ins, outs = refs[:4 * n_sw], refs[4 * n_sw:]
        for q in range(n_sw):
            res = _adamw(*[ins[4 * q + t][...] for t in range(4)])
            for t in range(3):
                outs[3 * q + t][...] = res[t]

    operands = [flat(d[n]) for n in small_w for d in (a, grads, {k: a["m_" + k] for k in small_w}, {k: a["v_" + k] for k in small_w})]
    outs = pl.pallas_call(
        adamw_all, name="adamw_small",
        out_shape=[jax.ShapeDtypeStruct(flat(a[n]).shape, F32) for n in small_w for _ in range(3)],
        in_specs=[pl.BlockSpec(memory_space=pltpu.VMEM)] * (4 * n_sw), out_specs=[pl.BlockSpec(memory_space=pltpu.VMEM)] * (3 * n_sw),
        compiler_params=pltpu.CompilerParams(vmem_limit_bytes=VMEM_LIMIT),
    )(*operands)
    for q, n in enumerate(small_w):
        delta[n], new_m[n], new_v[n] = [o.reshape(a[n].shape) for o in outs[3 * q:3 * q + 3]]

    return (loss, grad_x[None], *[grads[n] for n in WEIGHTS], *[delta[n] for n in WEIGHTS],
            *[new_m[n] for n in WEIGHTS], *[new_v[n] for n in WEIGHTS])
```

```python
import functools

import jax
import jax.numpy as jnp
from jax import lax
from jax.experimental import pallas as pl
from jax.experimental.pallas import tpu as pltpu

F32 = jnp.float32
BF16 = jnp.bfloat16
MESH = pl.DeviceIdType.MESH

EPS = 1e-6
LRU_C = 8.0
ADAM_LR = 0.001
ADAM_B1 = 0.9
ADAM_B2 = 0.999
ADAM_EPS = 1e-08
ADAM_WD = 0.01
ADAM_STEP = 10

N_GROUPS = 8
HEADS_PER_GROUP = 4
HEAD_DIM = 64
GROUP_W = HEADS_PER_GROUP * HEAD_DIM
STATE = 128
LRU_HEADS = 16
LRU_BLOCK = 128
CONV_W = 4
SSD_CHUNK = 256
HALO = 8

VMEM_LIMIT = 48 * 1024 * 1024


def _params(sem=None):
    return pltpu.CompilerParams(dimension_semantics=sem, vmem_limit_bytes=VMEM_LIMIT)


@jax.custom_jvp
def _log1p(x):
    u = 1.0 + x
    d = u - 1.0
    return jnp.where(d == 0.0, x, jnp.log(u) * (x / jnp.where(d == 0.0, 1.0, d)))


@_log1p.defjvp
def _log1p_jvp(primals, tangents):
    (x,), (t,) = primals, tangents
    return _log1p(x), t / (1.0 + x)


@jax.custom_jvp
def _expm1(x):
    u = jnp.exp(x)
    lu = jnp.log(u)
    safe = jnp.where(lu == 0.0, 1.0, lu)
    y = (u - 1.0) * (x / safe)
    y = jnp.where(lu == 0.0, x, y)
    return jnp.where(u == 0.0, -1.0, y)


@_expm1.defjvp
def _expm1_jvp(primals, tangents):
    (x,), (t,) = primals, tangents
    return _expm1(x), t * jnp.exp(x)


def _softplus(x):
    return jnp.maximum(x, 0.0) + _log1p(jnp.exp(-jnp.abs(x)))


def _sigmoid(x):
    return 1.0 / (1.0 + jnp.exp(-x))


def _silu(x):
    return x * _sigmoid(x)


def _gelu(x):
    c = 0.7978845608028654
    return 0.5 * x * (1.0 + jnp.tanh(c * (x + 0.044715 * (x * x * x))))


def _rms(x, g):
    return x * lax.rsqrt(jnp.mean(x * x, axis=-1, keepdims=True) + EPS) * g


def _dot(a, b, dims):
    return lax.dot_general(a.astype(BF16), b.astype(BF16), (dims, ((), ())), preferred_element_type=F32)


_NN = ((1,), (0,))
_NT = ((1,), (1,))
_TN = ((0,), (0,))


ANY = pl.BlockSpec(memory_space=pl.ANY)


class _Side:
    def __init__(self, ins, outs, aliases, n_sems, make):
        self.ins, self.outs, self.aliases, self.n_sems, self.make = ins, outs, aliases, n_sems, make


def _call(body, args, *, name, grid, in_specs, out_specs, out_shape, scratch_shapes=(), sem=None, side=None, into=None):
    in_specs, out_specs, out_shape, scratch_shapes = list(in_specs), list(out_specs), list(out_shape), list(scratch_shapes)
    held = []
    for oi, (buf, n_cols, off) in (into or {}).items():
        spec = out_specs[oi]
        out_shape[oi] = jax.ShapeDtypeStruct((out_shape[oi].shape[0], n_cols), out_shape[oi].dtype)
        out_specs[oi] = pl.BlockSpec(spec.block_shape, lambda *idx, _m=spec.index_map, _o=off: (_m(*idx)[0], _m(*idx)[1] + _o))
        if buf is not None:
            held.append((oi, buf))
    if side is None and not held:
        outs = pl.pallas_call(body, name=name, grid=grid, in_specs=in_specs, out_specs=out_specs, out_shape=out_shape,
                              scratch_shapes=scratch_shapes, compiler_params=_params(sem))(*args)
        return list(outs), []
    side_ins, side_outs = (list(side.ins), list(side.outs)) if side is not None else ([], [])
    n_in, n_out, n_scr, nh, si, so = len(in_specs), len(out_specs), len(scratch_shapes), len(held), len(side_ins), len(side_outs)

    def full(*refs):
        s_in = refs[n_in + nh:n_in + nh + si]
        o0 = n_in + nh + si
        s_out = refs[o0 + n_out:o0 + n_out + so]
        scr = refs[o0 + n_out + so:o0 + n_out + so + n_scr]
        if side is None:
            body(*refs[:n_in], *refs[o0:o0 + n_out], *scr)
            return
        send_sems, recv_sems = refs[-2], refs[-1]
        ids = [pl.program_id(d) for d in range(len(grid))]
        first = functools.reduce(jnp.logical_and, [i == 0 for i in ids])
        last = functools.reduce(jnp.logical_and, [i == g - 1 for i, g in zip(ids, grid)])

        @pl.when(first)
        def _():
            for d in side.make(s_in, s_out, send_sems, recv_sems)[0]:
                d.start()

        body(*refs[:n_in], *refs[o0:o0 + n_out], *scr)

        @pl.when(last)
        def _():
            sends, recvs = side.make(s_in, s_out, send_sems, recv_sems)
            for d in recvs:
                d.wait_recv()
            for d in sends:
                d.wait_send()

    aliases = {n_in + h: oi for h, (oi, _) in enumerate(held)}
    if side is not None:
        aliases.update({n_in + nh + i: n_out + o for i, o in side.aliases.items()})
        scratch_shapes = scratch_shapes + [pltpu.SemaphoreType.DMA((side.n_sems,)), pltpu.SemaphoreType.DMA((side.n_sems,))]
    outs = pl.pallas_call(
        full, name=name, grid=grid, in_specs=in_specs + [ANY] * (nh + si), out_specs=out_specs + [ANY] * so,
        out_shape=out_shape + side_outs, scratch_shapes=scratch_shapes, input_output_aliases=aliases,
        compiler_params=_params(("arbitrary",) * len(grid) if side is not None else sem),
    )(*args, *[b for _, b in held], *side_ins)
    return list(outs[:n_out]), list(outs[n_out:])


def _matmul(a, b, *, mode, m, n, k, tm, tn, tk, out_dtypes, name, a_spec=None, b_spec=None,
            out_specs=None, out_shapes=None, extras=(), epilogue=None, side=None):
    tm, tn, tk = min(tm, m), min(tn, n), min(tk, k)
    assert m % tm == 0 and n % tn == 0 and k % tk == 0, (name, m, n, k, tm, tn, tk)
    nk = k // tk
    dims = {"nn": _NN, "nt": _NT, "tn": _TN}[mode]
    if a_spec is None:
        a_spec = pl.BlockSpec((tk, tm), lambda i, j, kk: (kk, i)) if mode == "tn" else pl.BlockSpec((tm, tk), lambda i, j, kk: (i, kk))
    if b_spec is None:
        b_spec = pl.BlockSpec((tn, tk), lambda i, j, kk: (j, kk)) if mode == "nt" else pl.BlockSpec((tk, tn), lambda i, j, kk: (kk, j))
    tile = pl.BlockSpec((tm, tn), lambda i, j, kk: (i, j))
    if out_specs is None:
        out_specs = [tile for _ in out_dtypes]
    if out_shapes is None:
        out_shapes = [jax.ShapeDtypeStruct((m, n), d) for d in out_dtypes]
    n_ex, n_out = len(extras), len(out_dtypes)

    def body(*refs):
        a_ref, b_ref = refs[0], refs[1]
        ex_refs = refs[2:2 + n_ex]
        o_refs = refs[2 + n_ex:2 + n_ex + n_out]
        def finish(r):
            outs = epilogue(r, *[e[...] for e in ex_refs]) if epilogue is not None else (r,)
            for o_ref, o in zip(o_refs, outs):
                o_ref[...] = o.astype(o_ref.dtype)

        if nk == 1:
            finish(_dot(a_ref[...], b_ref[...], dims))
            return
        acc = refs[-1]
        kk = pl.program_id(2)

        @pl.when(kk == 0)
        def _():
            acc[...] = _dot(a_ref[...], b_ref[...], dims)

        @pl.when(kk > 0)
        def _():
            acc[...] += _dot(a_ref[...], b_ref[...], dims)

        @pl.when(kk == nk - 1)
        def _():
            finish(acc[...])

    outs, side_outs = _call(
        body, (a, b, *extras), name=name, grid=(m // tm, n // tn, nk),
        in_specs=[a_spec, b_spec] + [tile for _ in extras], out_specs=out_specs, out_shape=out_shapes,
        scratch_shapes=[] if nk == 1 else [pltpu.VMEM((tm, tn), F32)], sem=("parallel", "parallel", "arbitrary"), side=side)
    return outs if side is None else (outs, side_outs)


def _rowwise(fn, rows, bcast, out_rows, out_acc, *, name, tr, sub, side=None, into=None):
    rows = [r if isinstance(r, tuple) else (r, 0, r.shape[1]) for r in rows]
    row_specs = []
    for arr, c0, w in rows:
        assert c0 % w == 0, (name, c0, w)
        row_specs.append((w, c0 // w))
    rows = [r[0] for r in rows]
    L = rows[0].shape[0]
    tr = min(tr, L)
    sub = min(sub, tr)
    assert L % tr == 0 and tr % sub == 0, (name, L, tr, sub)
    n_r, n_b, n_or, n_oa = len(rows), len(bcast), len(out_rows), len(out_acc)

    def body(*refs):
        r_refs = refs[:n_r]
        b_refs = refs[n_r:n_r + n_b]
        or_refs = refs[n_r + n_b:n_r + n_b + n_or]
        oa_refs = refs[n_r + n_b + n_or:]
        i = pl.program_id(0)

        @pl.when(i == 0)
        def _():
            for o in oa_refs:
                o[...] = jnp.zeros_like(o)

        bvals = [b[...] for b in b_refs]

        def step(s, carry):
            r0 = pl.multiple_of(s * sub, sub)
            tiles = [r[pl.ds(r0, sub), :] for r in r_refs]
            outs = fn(*tiles, *bvals)
            for o_ref, o in zip(or_refs, outs[:n_or]):
                o_ref[pl.ds(r0, sub), :] = o.astype(o_ref.dtype)
            for o_ref, o in zip(oa_refs, outs[n_or:]):
                o_ref[...] += o
            return carry

        if tr == sub:
            step(0, 0)
        else:
            lax.fori_loop(0, tr // sub, step, 0, unroll=8)

    def whole(shape):
        nd = len(shape)
        return pl.BlockSpec(shape, lambda i, _n=nd: (0,) * _n)

    outs, side_outs = _call(
        body, (*rows, *bcast), name=name, grid=(L // tr,),
        in_specs=[pl.BlockSpec((tr, w), lambda i, _c=cb: (i, _c)) for w, cb in row_specs] + [whole(b.shape) for b in bcast],
        out_specs=[pl.BlockSpec((tr, c), lambda i: (i, 0)) for c, _ in out_rows] + [whole(s) for s in out_acc],
        out_shape=[jax.ShapeDtypeStruct((L, c), d) for c, d in out_rows] + [jax.ShapeDtypeStruct(s, F32) for s in out_acc],
        sem=("arbitrary",), side=side, into=into)
    return outs if side is None else (outs, side_outs)


def _colsum(x):
    return jnp.sum(x, axis=0, keepdims=True)


def _grouped(fn, rows, params, out_rows, out_acc, *, gw, name, tr, side=None, into=None):
    rows = [r if isinstance(r, tuple) else (r, 0) for r in rows]
    L = rows[0][0].shape[0]
    tr = min(tr, L)
    assert L % tr == 0
    G = None
    for p in params:
        G = p.shape[0] if p.ndim == 3 else p.shape[1] // gw
    cw = G * gw
    n_r, n_p, n_or, n_oa = len(rows), len(params), len(out_rows), len(out_acc)

    def pick(ref, g):
        return ref[g] if len(ref.shape) == 3 else ref[:, g * gw:(g + 1) * gw]

    def body(*refs):
        r_refs = refs[:n_r]
        p_refs = refs[n_r:n_r + n_p]
        or_refs = refs[n_r + n_p:n_r + n_p + n_or]
        oa_refs = refs[n_r + n_p + n_or:]

        @pl.when(pl.program_id(0) == 0)
        def _():
            for o in oa_refs:
                o[...] = jnp.zeros_like(o)

        for g in range(G):
            outs = fn(*[pick(r, g) for r in r_refs], *[pick(p, g) for p in p_refs])
            for o_ref, o in zip(or_refs, outs[:n_or]):
                o_ref[:, g * gw:(g + 1) * gw] = o.astype(o_ref.dtype)
            for o_ref, o in zip(oa_refs, outs[n_or:]):
                if len(o_ref.shape) == 3:
                    o_ref[g] += o
                else:
                    o_ref[:, g * gw:(g + 1) * gw] += o

    def whole(shape):
        nd = len(shape)
        return pl.BlockSpec(shape, lambda i, _n=nd: (0,) * _n)

    for _, c0 in rows:
        assert c0 % cw == 0
    outs, side_outs = _call(
        body, (*[r[0] for r in rows], *params), name=name, grid=(L // tr,),
        in_specs=[pl.BlockSpec((tr, cw), lambda i, _c=c0 // cw: (i, _c)) for _, c0 in rows] + [whole(p.shape) for p in params],
        out_specs=[pl.BlockSpec((tr, cw), lambda i: (i, 0)) for _ in out_rows] + [whole(s) for s in out_acc],
        out_shape=[jax.ShapeDtypeStruct((L, cw), d) for d in out_rows] + [jax.ShapeDtypeStruct(s, F32) for s in out_acc],
        sem=("arbitrary",), side=side, into=into)
    return outs if side is None else (outs, side_outs)


def _dsilu(p):
    s = _sigmoid(p)
    return s + p * s * (1.0 - s)


CONV_RC, CONV_CC = 32, 512


def _past_window(x_ref, head, r0, k, cs):
    if r0 == 0:
        return head[HALO - 3 + k:HALO - 3 + k + CONV_RC, cs]
    return x_ref[r0 - 3 + k:r0 - 3 + k + CONV_RC, cs]


def _conv_fwd(x, c0, C, w, b, *, silu, name, tr=512, cb=1024, side=None):
    L = x.shape[0]
    tr = min(tr, L)
    nb, hb = L // tr, tr // HALO
    assert L % tr == 0 and C % cb == 0 and c0 % cb == 0 and tr % CONV_RC == 0 and cb % CONV_CC == 0
    n_out = 2 if silu else 1

    def body(x_ref, h_ref, w_ref, b_ref, *rest):
        o_refs, head = rest[:n_out], rest[n_out]
        head[0:HALO, :] = h_ref[...]

        @pl.when(pl.program_id(0) == 0)
        def _():
            head[0:HALO, :] = jnp.zeros((HALO, cb), F32)

        head[HALO:, :] = x_ref[0:CONV_RC, :]
        for cc in range(cb // CONV_CC):
            cs = slice(cc * CONV_CC, (cc + 1) * CONV_CC)
            wv, bv = w_ref[:, cs], b_ref[:, cs]
            for r0 in range(0, tr, CONV_RC):
                y = bv
                for k in range(CONV_W):
                    y = y + wv[k:k + 1, :] * _past_window(x_ref, head, r0, k, cs)
                o_refs[0][r0:r0 + CONV_RC, cs] = y
                if silu:
                    o_refs[1][r0:r0 + CONV_RC, cs] = _silu(y)

    outs, side_outs = _call(
        body, (x, x, w, b), name=name, grid=(nb, C // cb),
        in_specs=[
            pl.BlockSpec((tr, cb), lambda i, j: (i, c0 // cb + j)),
            pl.BlockSpec((HALO, cb), lambda i, j: (jnp.maximum(i * hb - 1, 0), c0 // cb + j)),
            pl.BlockSpec((CONV_W, cb), lambda i, j: (0, j)),
            pl.BlockSpec((1, cb), lambda i, j: (0, j)),
        ],
        out_specs=[pl.BlockSpec((tr, cb), lambda i, j: (i, j)) for _ in range(n_out)],
        out_shape=[jax.ShapeDtypeStruct((L, C), F32) for _ in range(n_out)],
        scratch_shapes=[pltpu.VMEM((HALO + CONV_RC, cb), F32)], sem=("parallel", "parallel"), side=side)
    return outs if side is None else (outs, side_outs)


def _fold8(v):
    return (v[0:8] + v[8:16]) + (v[16:24] + v[24:32])


def _conv_bwd(dact, dc0, pre, x, xc0, C, w, *, silu, name, tr=512, cb=1024, side=None, into=None):
    L = x.shape[0]
    tr = min(tr, L)
    nb, hb = L // tr, tr // HALO
    last_h = L // HALO - 1
    assert L % tr == 0 and C % cb == 0 and tr % CONV_RC == 0 and cb % CONV_CC == 0

    def body(*refs):
        if silu:
            d_ref, dh_ref, p_ref, ph_ref, x_ref, xh_ref, w_ref, dx_ref, dwb_ref, dp, head = refs
        else:
            d_ref, dh_ref, x_ref, xh_ref, w_ref, dx_ref, dwb_ref, dp, head = refs
        i = pl.program_id(1)
        dp[tr:, :] = dh_ref[...] * _dsilu(ph_ref[...]) if silu else dh_ref[...]

        @pl.when(i == nb - 1)
        def _():
            dp[tr:, :] = jnp.zeros((HALO, cb), F32)

        head[0:HALO, :] = xh_ref[...]

        @pl.when(i == 0)
        def _():
            head[0:HALO, :] = jnp.zeros((HALO, cb), F32)
            dwb_ref[...] = jnp.zeros_like(dwb_ref)

        head[HALO:, :] = x_ref[0:CONV_RC, :]
        for cc in range(cb // CONV_CC):
            cs = slice(cc * CONV_CC, (cc + 1) * CONV_CC)
            for r0 in range(0, tr, CONV_RC):
                rs = slice(r0, r0 + CONV_RC)
                dp[rs, cs] = d_ref[rs, cs] * _dsilu(p_ref[rs, cs]) if silu else d_ref[rs, cs]
        for cc in range(cb // CONV_CC):
            cs = slice(cc * CONV_CC, (cc + 1) * CONV_CC)
            wv = w_ref[:, cs]
            acc = [jnp.zeros((8, CONV_CC), F32) for _ in range(CONV_W + 1)]
            for r0 in range(0, tr, CONV_RC):
                dm = dp[r0:r0 + CONV_RC, cs]
                dx = jnp.zeros((CONV_RC, CONV_CC), F32)
                for k in range(CONV_W):
                    dx = dx + wv[k:k + 1, :] * dp[r0 + 3 - k:r0 + 3 - k + CONV_RC, cs]
                    acc[k] = acc[k] + _fold8(dm * _past_window(x_ref, head, r0, k, cs))
                acc[CONV_W] = acc[CONV_W] + _fold8(dm)
                dx_ref[r0:r0 + CONV_RC, cs] = dx.astype(dx_ref.dtype)
            for k in range(CONV_W + 1):
                dwb_ref[k:k + 1, cs] += _colsum(acc[k])

    def main(c):
        return pl.BlockSpec((tr, cb), lambda j, i: (i, c // cb + j))

    def nxt(c):
        return pl.BlockSpec((HALO, cb), lambda j, i: (jnp.minimum((i + 1) * hb, last_h), c // cb + j))

    in_specs = [main(dc0), nxt(dc0)]
    args = [dact, dact]
    if silu:
        in_specs += [main(0), nxt(0)]
        args += [pre, pre]
    in_specs += [main(xc0), pl.BlockSpec((HALO, cb), lambda j, i: (jnp.maximum(i * hb - 1, 0), xc0 // cb + j)),
                 pl.BlockSpec((CONV_W, cb), lambda j, i: (0, j))]
    args += [x, x, w]
    outs, side_outs = _call(
        body, args, name=name, grid=(C // cb, nb), in_specs=in_specs,
        out_specs=[pl.BlockSpec((tr, cb), lambda j, i: (i, j)), pl.BlockSpec((8, cb), lambda j, i: (0, j))],
        out_shape=[jax.ShapeDtypeStruct((L, C), BF16), jax.ShapeDtypeStruct((8, C), F32)],
        scratch_shapes=[pltpu.VMEM((tr + HALO, cb), F32), pltpu.VMEM((HALO + CONV_RC, cb), F32)],
        sem=("parallel", "arbitrary"), side=side, into=into)
    return outs if side is None else (outs, side_outs)


def _ssd_f1(dtraw, dtb, alog):
    q = dtraw.shape[0]
    dt = _softplus(dtraw + dtb)
    adt = dt * (-jnp.exp(alog))
    tril = (lax.broadcasted_iota(jnp.int32, (q, q), 0) >= lax.broadcasted_iota(jnp.int32, (q, q), 1)).astype(F32)
    acs = lax.dot_general(tril, adt, (_NN, ((), ())), precision=lax.Precision.HIGHEST, preferred_element_type=F32)
    return dt, acs


def _ssd_group(g, x, bm, cm, dt, acs, drow, hp):
    q = x.shape[0]
    lane = lax.broadcasted_iota(jnp.int32, (1, 128), 1)
    sub = lax.broadcasted_iota(jnp.int32, (128, 1), 0)
    head_of = lax.broadcasted_iota(jnp.int32, (1, GROUP_W), 1) // HEAD_DIM
    is_last = (lax.broadcasted_iota(jnp.int32, (q, 1), 0) == q - 1).astype(F32)
    causal = lax.broadcasted_iota(jnp.int32, (q, q), 0) >= lax.broadcasted_iota(jnp.int32, (q, q), 1)
    acs_end = jnp.sum(acs * is_last, axis=0, keepdims=True)
    acs_t = acs.T
    dt_exp = jnp.zeros((q, GROUP_W), F32)
    acs_exp = jnp.zeros((q, GROUP_W), F32)
    end_exp = jnp.zeros((1, GROUP_W), F32)
    d_exp = jnp.zeros((1, GROUP_W), F32)
    heads = []
    for k in range(HEADS_PER_GROUP):
        h = HEADS_PER_GROUP * g + k
        oh = (lane == h).astype(F32)
        mk = (head_of == k).astype(F32)
        acs_col = jnp.sum(acs * oh, axis=1, keepdims=True)
        acs_row = jnp.sum(acs_t * (sub == h).astype(F32), axis=0, keepdims=True)
        dt_exp = dt_exp + jnp.sum(dt * oh, axis=1, keepdims=True) * mk
        acs_exp = acs_exp + acs_col * mk
        end_exp = end_exp + jnp.sum(acs_end * oh, axis=1, keepdims=True) * mk
        d_exp = d_exp + jnp.sum(drow * oh, axis=1, keepdims=True) * mk
        heads.append((acs_col, acs_row, mk))
    xdt = x * dt_exp
    states = _dot(bm, xdt * jnp.exp(end_exp - acs_exp), _TN)
    y = _dot(cm, hp, _NN) * jnp.exp(acs_exp) + x * d_exp
    scores = _dot(cm, bm, _NT)
    for acs_col, acs_row, mk in heads:
        dec = jnp.exp(jnp.where(causal, acs_col - acs_row, -jnp.inf))
        y = y + _dot(scores * dec, xdt * mk, _NN)
    return y, hp * jnp.exp(end_exp) + states


def _ssd_fwd(act, dtraw, dtb, alog, drow, *, name, side=None):
    L = act.shape[0]
    q = min(SSD_CHUNK, L)
    nc = L // q
    d_ssd = N_GROUPS * GROUP_W

    def body(act_ref, dt_ref, dtb_ref, alog_ref, drow_ref, y_ref, hst_ref, h):
        @pl.when(pl.program_id(0) == 0)
        def _():
            h[...] = jnp.zeros_like(h)

        dt, acs = _ssd_f1(dt_ref[...], dtb_ref[...], alog_ref[...])
        drow = drow_ref[...]
        for g in range(N_GROUPS):
            hp = h[g]
            hst_ref[0, g] = hp
            y, hn = _ssd_group(g, act_ref[:, g * GROUP_W:(g + 1) * GROUP_W],
                               act_ref[:, d_ssd + g * STATE:d_ssd + (g + 1) * STATE],
                               act_ref[:, d_ssd + (N_GROUPS + g) * STATE:d_ssd + (N_GROUPS + g + 1) * STATE],
                               dt, acs, drow, hp)
            y_ref[:, g * GROUP_W:(g + 1) * GROUP_W] = y
            h[g] = hn

    row = pl.BlockSpec((1, 128), lambda i: (0, 0))
    outs, side_outs = _call(
        body, (act, dtraw, dtb, alog, drow), name=name, grid=(nc,),
        in_specs=[pl.BlockSpec((q, act.shape[1]), lambda i: (i, 0)), pl.BlockSpec((q, 128), lambda i: (i, 0)), row, row, row],
        out_specs=[pl.BlockSpec((q, d_ssd), lambda i: (i, 0)),
                   pl.BlockSpec((1, N_GROUPS, STATE, GROUP_W), lambda i: (i, 0, 0, 0))],
        out_shape=[jax.ShapeDtypeStruct((L, d_ssd), F32), jax.ShapeDtypeStruct((nc, N_GROUPS, STATE, GROUP_W), F32)],
        scratch_shapes=[pltpu.VMEM((N_GROUPS, STATE, GROUP_W), F32)], sem=("arbitrary",), side=side)
    return outs if side is None else (outs, side_outs)


def _ssd_bwd(act, dtraw, dy, hst, dtb, alog, drow, *, name, side=None):
    L = act.shape[0]
    q = min(SSD_CHUNK, L)
    nc = L // q
    d_ssd = N_GROUPS * GROUP_W

    def body(act_ref, dt_ref, dy_ref, hst_ref, dtb_ref, alog_ref, drow_ref, dact_ref, ddt_ref, dpar_ref, dh):
        @pl.when(pl.program_id(0) == 0)
        def _():
            dh[...] = jnp.zeros_like(dh)
            dpar_ref[...] = jnp.zeros_like(dpar_ref)

        (dt, acs), f1_vjp = jax.vjp(_ssd_f1, dt_ref[...], dtb_ref[...], alog_ref[...])
        drow = drow_ref[...]
        ddt = jnp.zeros_like(dt)
        dacs = jnp.zeros_like(acs)
        ddrow = jnp.zeros_like(drow)
        for g in range(N_GROUPS):
            xs = slice(g * GROUP_W, (g + 1) * GROUP_W)
            bs = slice(d_ssd + g * STATE, d_ssd + (g + 1) * STATE)
            cs = slice(d_ssd + (N_GROUPS + g) * STATE, d_ssd + (N_GROUPS + g + 1) * STATE)
            _, f2_vjp = jax.vjp(functools.partial(_ssd_group, g), act_ref[:, xs], act_ref[:, bs], act_ref[:, cs],
                                dt, acs, drow, hst_ref[0, g])
            dx, dbm, dcm, ddt_g, dacs_g, ddrow_g, dhp = f2_vjp((dy_ref[:, xs], dh[g]))
            dact_ref[:, xs] = dx
            dact_ref[:, bs] = dbm
            dact_ref[:, cs] = dcm
            dh[g] = dhp
            ddt, dacs, ddrow = ddt + ddt_g, dacs + dacs_g, ddrow + ddrow_g
        ddtraw, ddtb, dalog = f1_vjp((ddt, dacs))
        ddt_ref[...] = ddtraw
        dpar_ref[0:1, :] += ddtb
        dpar_ref[1:2, :] += dalog
        dpar_ref[2:3, :] += ddrow

    row = pl.BlockSpec((1, 128), lambda i: (0, 0))
    rev = lambda i: (nc - 1 - i, 0)
    outs, side_outs = _call(
        body, (act, dtraw, dy, hst, dtb, alog, drow), name=name, grid=(nc,),
        in_specs=[pl.BlockSpec((q, act.shape[1]), rev), pl.BlockSpec((q, 128), rev), pl.BlockSpec((q, d_ssd), rev),
                  pl.BlockSpec((1, N_GROUPS, STATE, GROUP_W), lambda i: (nc - 1 - i, 0, 0, 0)), row, row, row],
        out_specs=[pl.BlockSpec((q, act.shape[1]), rev), pl.BlockSpec((q, 128), rev), pl.BlockSpec((8, 128), lambda i: (0, 0))],
        out_shape=[jax.ShapeDtypeStruct(act.shape, F32), jax.ShapeDtypeStruct((L, 128), F32), jax.ShapeDtypeStruct((8, 128), F32)],
        scratch_shapes=[pltpu.VMEM((N_GROUPS, STATE, GROUP_W), F32)], sem=("arbitrary",), side=side)
    return outs if side is None else (outs, side_outs)


def _gate_head(xl, wa, ba, wx, bx, lam):
    r = _sigmoid(_dot(xl, wa, _NN) + ba)
    i = _sigmoid(_dot(xl, wx, _NN) + bx)
    log_a = -LRU_C * r * _softplus(-lam)
    return jnp.exp(log_a), jnp.sqrt(-_expm1(2.0 * log_a)) * (i * xl)


def _gate_head_bwd(xl, da, du, wa, ba, wx, bx, lam):
    _, vjp = jax.vjp(_gate_head, xl, wa, ba, wx, bx, lam)
    dxl, dwa, dba, dwx, dbx, dlam = vjp((da, du))
    return dxl, dwa, dba, dwx, dbx, dlam


def _scan_tile(a, b, rows, reverse):
    for d in (1, 2, 4):
        if reverse:
            keep = rows < 8 - d
            a_sh, b_sh = pltpu.roll(a, 8 - d, 0), pltpu.roll(b, 8 - d, 0)
        else:
            keep = rows >= d
            a_sh, b_sh = pltpu.roll(a, d, 0), pltpu.roll(b, d, 0)
        b = b + a * jnp.where(keep, b_sh, 0.0)
        a = a * jnp.where(keep, a_sh, 1.0)
    return a, b


def _lru_scan_fwd(a, u, *, name, tr=512, cb=1024):
    L, C = a.shape
    tr, cb = min(tr, L), min(cb, C)

    def body(a_ref, u_ref, h_ref, hp_ref, carry):
        @pl.when(pl.program_id(1) == 0)
        def _():
            carry[...] = jnp.zeros_like(carry)

        rows = lax.broadcasted_iota(jnp.int32, (8, cb), 0)

        def tile(t, hc):
            r0 = pl.multiple_of(t * 8, 8)
            pa, hb = _scan_tile(a_ref[pl.ds(r0, 8), :], u_ref[pl.ds(r0, 8), :], rows, False)
            h = hb + pa * hc
            h_ref[pl.ds(r0, 8), :] = h
            hp_ref[pl.ds(r0, 8), :] = jnp.where(rows >= 1, pltpu.roll(h, 1, 0), hc)
            return h[7:8, :]

        carry[...] = lax.fori_loop(0, tr // 8, tile, carry[...], unroll=4)

    blk = pl.BlockSpec((tr, cb), lambda j, i: (i, j))
    return pl.pallas_call(
        body, name=name, grid=(C // cb, L // tr),
        in_specs=[blk, blk], out_specs=[blk, blk],
        out_shape=[jax.ShapeDtypeStruct((L, C), F32), jax.ShapeDtypeStruct((L, C), F32)],
        scratch_shapes=[pltpu.VMEM((1, cb), F32)],
        compiler_params=_params(("parallel", "arbitrary")),
    )(a, u)


def _lru_scan_bwd(a, hprev, dh, *, name, tr=512, cb=1024):
    L, C = a.shape
    tr, cb = min(tr, L), min(cb, C)
    nb = L // tr

    def body(a_ref, hp_ref, dh_ref, da_ref, du_ref, carry):
        @pl.when(pl.program_id(1) == 0)
        def _():
            carry[...] = jnp.zeros_like(carry)

        rows = lax.broadcasted_iota(jnp.int32, (8, cb), 0)

        def tile(t, gc):
            r0 = pl.multiple_of((tr // 8 - 1 - t) * 8, 8)
            av, dv = a_ref[pl.ds(r0, 8), :], dh_ref[pl.ds(r0, 8), :]
            pa, gb = _scan_tile(av, av * dv, rows, True)
            big = gb + pa * gc
            g = dv + jnp.where(rows < 7, pltpu.roll(big, 7, 0), gc)
            du_ref[pl.ds(r0, 8), :] = g
            da_ref[pl.ds(r0, 8), :] = g * hp_ref[pl.ds(r0, 8), :]
            return big[0:1, :]

        carry[...] = lax.fori_loop(0, tr // 8, tile, carry[...], unroll=4)

    blk = pl.BlockSpec((tr, cb), lambda j, i: (nb - 1 - i, j))
    return pl.pallas_call(
        body, name=name, grid=(C // cb, nb),
        in_specs=[blk, blk, blk], out_specs=[blk, blk],
        out_shape=[jax.ShapeDtypeStruct((L, C), F32), jax.ShapeDtypeStruct((L, C), F32)],
        scratch_shapes=[pltpu.VMEM((1, cb), F32)],
        compiler_params=_params(("parallel", "arbitrary")),
    )(a, hprev, dh)


def _ssd_gate(y, z, n):
    v = y * _silu(z)
    return v * lax.rsqrt(jnp.mean(v * v, axis=-1, keepdims=True) + EPS) * n


def _ssd_gate_bwd(y, z, dy, n):
    _, vjp = jax.vjp(_ssd_gate, y, z, n)
    return vjp(dy.astype(F32))


def _lru_out(hl, gate, n):
    return _rms(hl * _gelu(gate), n)


def _lru_out_bwd(hl, gate, dy, n):
    _, vjp = jax.vjp(_lru_out, hl, gate, n)
    return vjp(dy.astype(F32))


def _mid(x, mix, pm, pmlp):
    x1 = x + _rms(mix, pm)
    return x1, _rms(x1, pmlp)


def _mid_bwd(x, mix, dx1p, dh2, pm, pmlp):
    _, vjp = jax.vjp(_mid, x, mix, pm, pmlp)
    dx, dmix, dpm, dpmlp = vjp((dx1p, dh2))
    return dmix, dx, dpm, dpmlp


def _loss_bwd(hm2, x1, tgt, g):
    def lossf(hm2, x1, g):
        e = x1 + _rms(hm2, g) - tgt
        return 0.5 * jnp.sum(jnp.mean(e * e, axis=-1, keepdims=True), axis=0, keepdims=True)

    val, vjp = jax.vjp(lossf, hm2, x1, g)
    dhm2, dx1, dg = vjp(jnp.ones((1, 1), F32))
    return dhm2, dx1, dg, val * jnp.ones((1, 128), F32)


def _in_bwd(x, dh_a, dh_b, dx1, g):
    _, vjp = jax.vjp(_rms, x, g)
    dx, dg = vjp(dh_a + dh_b)
    return dx + dx1, dg


def _adamw(w, g, m, v):
    m = ADAM_B1 * m + (1.0 - ADAM_B1) * g
    v = ADAM_B2 * v + (1.0 - ADAM_B2) * (g * g)
    m_hat = m / (1.0 - ADAM_B1 ** ADAM_STEP)
    v_hat = v / (1.0 - ADAM_B2 ** ADAM_STEP)
    return -ADAM_LR * (m_hat / (jnp.sqrt(v_hat) + ADAM_EPS) + ADAM_WD * w), m, v


class _LocalPlan:
    def __init__(self, p):
        self.p, self.large = p, {}

    def weight(self, name):
        return self.p[name]

    def side(self, kernel_name):
        return None

    def done(self, kernel_name, side_outs):
        pass

    def grad(self, name, g):
        self.large[name] = g

    def small(self, grads):
        pass


def _local_step(x, tgt, p, plan):
    L, D = x.shape

    def carry(fn, *args, name, **kw):
        side = plan.side(name)
        if side is None:
            return fn(*args, name=name, **kw)
        outs, side_outs = fn(*args, name=name, side=side, **kw)
        plan.done(name, side_outs)
        return outs

    d_ssd, d_xbc, d_lru, d_mix, d_ff = 2048, 4096, 2048, 4096, 8192
    n_main = d_ssd + d_xbc + 2 * d_lru
    c_xbc, c_gate, c_xl = d_ssd, d_ssd + d_xbc, d_ssd + d_xbc + d_lru
    TR, TRW, SUB = 256, 512, 32
    mm = dict(tm=1024, tn=1024, tk=2048)

    (h,) = _rowwise(lambda xt, g: (_rms(xt, g),), [x], [p["pre_mix_norm"]], [(D, BF16)], [], name="f_prenorm", tr=TRW, sub=SUB)
    w_main, w_dt = plan.weight("w_main"), plan.weight("w_dt")
    (proj,) = carry(_matmul, h, w_main, mode="nn", m=L, n=n_main, k=D, out_dtypes=[F32], name="f_inproj", **mm)
    (dtraw,) = _matmul(h, w_dt, mode="nn", m=L, n=128, k=D, out_dtypes=[F32], name="f_dtproj", **mm)
    pre, act = carry(_conv_fwd, proj, c_xbc, d_xbc, p["ssd_conv_w"], p["ssd_conv_b"], silu=True, name="f_ssdconv")
    (xl,) = carry(_conv_fwd, proj, c_xl, d_lru, p["lru_conv_w"], p["lru_conv_b"], silu=False, name="f_lruconv")
    yraw, hst = carry(_ssd_fwd, act, dtraw, p["dtb"], p["alog"], p["drow"], name="f_ssd")
    (ycat,) = _grouped(lambda y, z, n: (_ssd_gate(y, z, n),), [yraw, (proj, 0)], [p["ssd_norm"]], [BF16], [],
                       gw=GROUP_W, name="f_ssdgate", tr=TR, into={0: (None, d_mix, 0)})
    gate_p = [p["lru_w_a"], p["lru_b_a"], p["lru_w_x"], p["lru_b_x"], p["lru_lambda"]]
    a, u = carry(_grouped, _gate_head, [xl], gate_p, [F32, F32], [], gw=LRU_BLOCK, name="f_lrugates", tr=TR)
    hl, hprev = _lru_scan_fwd(a, u, name="f_lruscan")
    (ycat,) = _rowwise(lambda ht, gt, n: (_lru_out(ht, gt, n),), [hl, (proj, c_gate, d_lru)], [p["lru_norm"]],
                       [(d_lru, BF16)], [], name="f_lruout", tr=TRW, sub=SUB, into={0: (ycat, d_mix, d_ssd // d_lru)})
    w_out = plan.weight("w_out")
    (mix,) = carry(_matmul, ycat, w_out, mode="nn", m=L, n=D, k=d_mix, out_dtypes=[F32], name="f_outproj", **mm)
    x1, h2 = _rowwise(_mid, [x, mix], [p["post_mix_norm"], p["pre_mlp_norm"]], [(D, F32), (D, BF16)], [],
                      name="f_mid", tr=TRW, sub=SUB)
    nb_mi = (d_ff // 4) // mm["tn"]
    w_mi = plan.weight("w_mi")
    hm, act2 = carry(_matmul, h2, w_mi, mode="nn", m=L, n=d_ff, k=D, out_dtypes=[BF16, BF16], name="f_mlpin",
                     b_spec=pl.BlockSpec((None, mm["tk"], mm["tn"]), lambda i, j, kk: (j // nb_mi, kk, j % nb_mi)),
                     epilogue=lambda r: (r, jnp.square(jnp.maximum(r, 0.0))), **mm)
    w_mo = plan.weight("w_mo")
    (hm2,) = _matmul(act2, w_mo, mode="nn", m=L, n=D, k=d_ff, out_dtypes=[F32], name="f_mlpout", **mm)

    dhm2, dx1p, d_post_mlp, loss = _rowwise(_loss_bwd, [hm2, x1, tgt], [p["post_mlp_norm"]], [(D, BF16), (D, F32)],
                                            [(1, D), (1, 128)], name="b_loss", tr=TRW, sub=SUB)
    (dhm,) = _matmul(dhm2, w_mo, mode="nt", m=L, n=d_ff, k=D, out_dtypes=[BF16], name="b_mlpout_dx", extras=[hm],
                     epilogue=lambda r, hmv: (r * (2.0 * jnp.maximum(hmv.astype(F32), 0.0)),), **mm)
    (dw_mo,) = _matmul(act2, dhm2, mode="tn", m=d_ff, n=D, k=L, out_dtypes=[BF16], name="b_mlpout_dw", **mm)
    plan.grad("w_mlp_out", dw_mo.reshape(4, -1, D))
    kb_mi = (d_ff // 4) // mm["tk"]
    (dh2,) = carry(_matmul, dhm, w_mi, mode="nt", m=L, n=D, k=d_ff, out_dtypes=[F32], name="b_mlpin_dx",
                   b_spec=pl.BlockSpec((None, mm["tn"], mm["tk"]), lambda i, j, kk: (kk // kb_mi, j, kk % kb_mi)), **mm)
    (dw_mi,) = carry(_matmul, h2, dhm, mode="tn", m=D, n=d_ff, k=L, out_dtypes=[BF16], name="b_mlpin_dw",
                     out_specs=[pl.BlockSpec((None, mm["tm"], mm["tn"]), lambda i, j, kk: (j // nb_mi, i, j % nb_mi))],
                     out_shapes=[jax.ShapeDtypeStruct((4, D, d_ff // 4), BF16)], **mm)
    plan.grad("w_mlp_in", dw_mi)
    dmix, dx1, d_post_mix, d_pre_mlp = carry(_rowwise, _mid_bwd, [x, mix, dx1p, dh2], [p["post_mix_norm"], p["pre_mlp_norm"]],
                                             [(D, BF16), (D, F32)], [(1, D), (1, D)], name="b_mid", tr=TR, sub=SUB)
    (dw_out,) = carry(_matmul, ycat, dmix, mode="tn", m=d_mix, n=D, k=L, out_dtypes=[BF16], name="b_outproj_dw", **mm)
    plan.grad("w_out", dw_out.reshape(4, -1, D))
    (dycat,) = carry(_matmul, dmix, w_out, mode="nt", m=L, n=d_mix, k=D, out_dtypes=[BF16], name="b_outproj_dx", **mm)
    dhl, dproj, d_lru_norm = _rowwise(_lru_out_bwd, [hl, (proj, c_gate, d_lru), (dycat, d_ssd, d_lru)], [p["lru_norm"]],
                                      [(d_lru, F32), (d_lru, BF16)], [(1, d_lru)], name="b_lruout", tr=TRW, sub=SUB,
                                      into={1: (None, n_main, c_gate // d_lru)})
    da, du = _lru_scan_bwd(a, hprev, dhl, name="b_lruscan")
    dxl, d_wa, d_ba, d_wx, d_bx, d_lam = _grouped(
        _gate_head_bwd, [xl, da, du], gate_p, [F32],
        [(LRU_HEADS, LRU_BLOCK, LRU_BLOCK), (1, d_lru), (LRU_HEADS, LRU_BLOCK, LRU_BLOCK), (1, d_lru), (1, d_lru)],
        gw=LRU_BLOCK, name="b_lrugates", tr=TR)
    CB = 1024
    dproj, dwb_lru = _conv_bwd(dxl, 0, None, proj, c_xl, d_lru, p["lru_conv_w"], silu=False, name="b_lruconv", cb=CB,
                               into={0: (dproj, n_main, c_xl // CB)})
    dyraw, dproj, d_ssd_norm = _grouped(_ssd_gate_bwd, [yraw, (proj, 0), (dycat, 0)], [p["ssd_norm"]], [F32, BF16], [(1, d_ssd)],
                                        gw=GROUP_W, name="b_ssdgate", tr=TR, into={1: (dproj, n_main, 0)})
    dact, ddtraw, dpar = carry(_ssd_bwd, act, dtraw, dyraw, hst, p["dtb"], p["alog"], p["drow"], name="b_ssd")
    dproj, dwb_ssd = carry(_conv_bwd, dact, 0, pre, proj, c_xbc, d_xbc, p["ssd_conv_w"], silu=True, name="b_ssdconv", cb=CB,
                           into={0: (dproj, n_main, c_xbc // CB)})
    early = dict(loss=loss, ssd_conv=dwb_ssd, ssd_par=dpar, ssd_norm=d_ssd_norm, lru_conv=dwb_lru, lru_w_a=d_wa, lru_b_a=d_ba,
                 lru_w_x=d_wx, lru_b_x=d_bx, lru_lambda=d_lam, lru_norm=d_lru_norm, post_mix_norm=d_post_mix,
                 pre_mlp_norm=d_pre_mlp, post_mlp_norm=d_post_mlp)
    plan.small(early)
    (dw_main,) = carry(_matmul, h, dproj, mode="tn", m=D, n=n_main, k=L, out_dtypes=[BF16], name="b_inproj_dw", **mm)
    (dw_dt,) = _matmul(h, ddtraw, mode="tn", m=D, n=128, k=L, out_dtypes=[BF16], name="b_dtproj_dw", **mm)
    plan.grad("w_in", (dw_main, dw_dt))
    (dh_a,) = carry(_matmul, dproj, w_main, mode="nt", m=L, n=D, k=n_main, out_dtypes=[F32], name="b_inproj_dx", **mm)
    (dh_b,) = _matmul(ddtraw, w_dt, mode="nt", m=L, n=D, k=128, out_dtypes=[F32], name="b_dtproj_dx", **mm)
    grad_x, d_pre_mix = carry(_rowwise, _in_bwd, [x, dh_a, dh_b, dx1], [p["pre_mix_norm"]], [(D, F32)], [(1, D)],
                              name="b_prenorm", tr=TRW, sub=SUB)

    return grad_x, dict(early, pre_mix_norm=d_pre_mix)


def _place():
    return lax.axis_index("x"), lax.axis_index("y"), lax.axis_index("c")


def _other_chips(x, y):
    return [(1 - x, y), (x, 1 - y), (1 - x, 1 - y)]


def _allgather8(blk, *, name):
    r, n = blk.shape

    def body(x_ref, out_ref, send_sems, recv_sems, local_sem):
        x, y, c = _place()
        me, sibling = (x, y, c), (x, y, 1 - c)
        chips = _other_chips(x, y)

        def rows(px, py, pc):
            return out_ref.at[pl.ds((4 * px + 2 * py + pc) * r, r), :]

        def copy(k, block, to, src=None):
            return pltpu.make_async_remote_copy(
                src_ref=rows(*block) if src is None else src, dst_ref=rows(*block),
                send_sem=send_sems.at[k], recv_sem=recv_sems.at[k], device_id=to, device_id_type=MESH)

        mine = pltpu.make_async_copy(x_ref, rows(*me), local_sem)
        mine.start()
        first = [copy(0, me, sibling, src=x_ref)]
        first += [copy(1 + k, me, (*chip, c), src=x_ref) for k, chip in enumerate(chips)]
        for cp in first:
            cp.start()
        passed = [copy(4 + k, (*chip, c), sibling) for k, chip in enumerate(chips)]
        for k, chip in enumerate(chips):
            copy(1 + k, (*chip, c), me).wait_recv()
            passed[k].start()
        copy(0, sibling, me).wait_recv()
        for k, chip in enumerate(chips):
            copy(4 + k, (*chip, 1 - c), me).wait_recv()
        for cp in first + passed:
            cp.wait_send()
        mine.wait()

    return pl.pallas_call(
        body, name=name,
        out_shape=jax.ShapeDtypeStruct((8 * r, n), blk.dtype),
        in_specs=[pl.BlockSpec(memory_space=pltpu.VMEM)], out_specs=pl.BlockSpec(memory_space=pltpu.VMEM),
        scratch_shapes=[pltpu.SemaphoreType.DMA((7,)), pltpu.SemaphoreType.DMA((7,)), pltpu.SemaphoreType.DMA],
        compiler_params=pltpu.CompilerParams(vmem_limit_bytes=VMEM_LIMIT),
    )(blk)


def _sum8(g, *, name):
    _, r, n = g.shape
    tr = max(t for t in range(8, min(r, 512) + 1, 8) if r % t == 0)

    def body(g_ref, o_ref):
        s = g_ref[0]
        for k in range(1, 8):
            s = s + g_ref[k]
        o_ref[...] = s

    return pl.pallas_call(
        body, name=name, grid=(r // tr,),
        in_specs=[pl.BlockSpec((8, tr, n), lambda i: (0, i, 0))], out_specs=pl.BlockSpec((tr, n), lambda i: (i, 0)),
        out_shape=jax.ShapeDtypeStruct((r, n), g.dtype), compiler_params=_params(("parallel",)),
    )(g)


def _blocks(fn, ins, outs, *, grid, name, prefetch=None, aliases=None):
    n_in = len(ins)

    def body(*refs):
        if prefetch is not None:
            refs = refs[1:]
        res = fn(*[r[...] for r in refs[:n_in]])
        for o_ref, o in zip(refs[n_in:], res):
            o_ref[...] = o.astype(o_ref.dtype)

    in_specs = [pl.BlockSpec(b, m) for _, b, m in ins]
    out_specs = [pl.BlockSpec(b, m) for _, b, m in outs]
    kw = dict(name=name, out_shape=[s for s, _, _ in outs], input_output_aliases=aliases or {},
              compiler_params=_params(("arbitrary",) * len(grid)))
    arrs = [a for a, _, _ in ins]
    if prefetch is None:
        return pl.pallas_call(body, grid=grid, in_specs=in_specs, out_specs=out_specs, **kw)(*arrs)
    spec = pltpu.PrefetchScalarGridSpec(num_scalar_prefetch=1, grid=grid, in_specs=in_specs, out_specs=out_specs)
    return pl.pallas_call(body, grid_spec=spec, **kw)(prefetch, *arrs)


def _gather_tree(slot, *, name):
    def body(i_ref, g_ref, send_sems, recv_sems):
        x, y, c = _place()
        j, jx, jy, jd = 2 * x + y, 2 * (1 - x) + y, 2 * x + (1 - y), 2 * (1 - x) + (1 - y)
        xn, yn, sibling = (1 - x, y, c), (x, 1 - y, c), (x, y, 1 - c)
        rp = i_ref.shape[1] // 4

        def piece(ref, chip, hc, q):
            return ref.at[chip, pl.ds((2 * hc + q) * rp, rp), :]

        def cp(k, src, dst, to):
            return pltpu.make_async_remote_copy(src_ref=src, dst_ref=dst, send_sem=send_sems.at[k], recv_sem=recv_sems.at[k],
                                                device_id=to, device_id_type=MESH)

        def landed(k, chip, q):
            return cp(k, piece(g_ref, chip, c, q), piece(g_ref, chip, c, q), xn)

        sends = []

        def go(d):
            d.start()
            sends.append(d)

        go(cp(0, piece(i_ref, j, c, 0), piece(g_ref, j, c, 0), xn))
        go(cp(2, piece(i_ref, j, c, 1), piece(g_ref, j, c, 1), yn))
        go(cp(1, piece(i_ref, j, c, 1), piece(g_ref, j, c, 1), xn))
        go(cp(3, piece(i_ref, j, c, 0), piece(g_ref, j, c, 0), yn))
        arrivals = [(0, jx, 0, 4, yn), (2, jy, 1, 5, xn), (1, jx, 1, None, None), (3, jy, 0, None, None),
                    (4, jd, 0, None, None), (5, jd, 1, None, None)]
        for n_arr, (k, chip, q, k_fwd, to) in enumerate(arrivals):
            landed(k, chip, q).wait_recv()
            if k_fwd is not None:
                go(cp(k_fwd, piece(g_ref, chip, c, q), piece(g_ref, chip, c, q), to))
            go(cp(6 + n_arr, piece(g_ref, chip, c, q), piece(g_ref, chip, c, q), sibling))
        for n_arr, (_, chip, q, _, _) in enumerate(arrivals):
            other = piece(g_ref, chip, 1 - c, q)
            cp(6 + n_arr, other, other, sibling).wait_recv()
        for d in sends:
            d.wait_send()

    return pl.pallas_call(
        body, name=name, out_shape=jax.ShapeDtypeStruct(slot.shape, slot.dtype),
        in_specs=[ANY], out_specs=ANY, input_output_aliases={0: 0},
        scratch_shapes=[pltpu.SemaphoreType.DMA((12,)), pltpu.SemaphoreType.DMA((12,))],
    )(slot)


def _pack(arrs):
    parts = []
    for v in arrs:
        f = v.reshape(-1)
        f = jnp.pad(f, (0, (-f.shape[0]) % 1024))
        parts.append(f.reshape(-1, 128))
    return jnp.concatenate(parts, axis=0)


def _unpack(packed, shapes):
    out, r0 = [], 0
    for s in shapes:
        size = 1
        for d in s:
            size *= d
        nr = (size + 1023) // 1024 * 8
        out.append(packed[r0:r0 + nr].reshape(-1)[:size].reshape(s))
        r0 += nr
    return out


SMALL_GRADS = [("loss", (1, 128)), ("pre_mix_norm", (1, 2048)), ("ssd_conv", (8, 4096)), ("ssd_par", (8, 128)),
               ("ssd_norm", (1, 2048)), ("lru_conv", (8, 2048)), ("lru_w_a", (16, 128, 128)), ("lru_b_a", (1, 2048)),
               ("lru_w_x", (16, 128, 128)), ("lru_b_x", (1, 2048)), ("lru_lambda", (1, 2048)), ("lru_norm", (1, 2048)),
               ("post_mix_norm", (1, 2048)), ("pre_mlp_norm", (1, 2048)), ("post_mlp_norm", (1, 2048))]

SMALL_EARLY = [g for g in SMALL_GRADS if g[0] != "pre_mix_norm"]

WEIGHTS = ['pre_mix_norm', 'w_in', 'ssd_conv_w', 'ssd_conv_b', 'ssd_dt_bias', 'ssd_a_log', 'ssd_d', 'ssd_norm', 'lru_conv_w',
           'lru_conv_b', 'lru_w_a', 'lru_b_a', 'lru_w_x', 'lru_b_x', 'lru_lambda', 'lru_norm', 'w_out', 'post_mix_norm',
           'pre_mlp_norm', 'w_mlp_in', 'w_mlp_out', 'post_mlp_norm']
LARGE = ['w_in', 'w_out', 'w_mlp_in', 'w_mlp_out']

D_SSD, D_XBC, DT_W = 2048, 4096, 32
TB = 256


def _w_in_runs(n_shard, n_main):
    c_dt = D_SSD + D_XBC
    runs, p = [], 0
    while p < n_main:
        j, off = divmod(p if p < c_dt else p + DT_W, n_shard)
        ln = min(n_shard - off, (c_dt if p < c_dt else n_main) - p)
        runs.append((p, j, off, ln))
        p += ln
    jd, offd = divmod(c_dt, n_shard)
    assert offd + DT_W <= n_shard
    return runs, (jd, offd)


def _pack_w_in(slots, *, name):
    _, d, n_shard = slots.shape
    n_main = 4 * n_shard - DT_W
    runs, (jd, offd) = _w_in_runs(n_shard, n_main)

    def body(s_ref, main_ref, dt_ref):
        for p, j, off, ln in runs:
            main_ref[:, p:p + ln] = s_ref[j, :, off:off + ln]
        dt_ref[:, 0:DT_W] = s_ref[jd, :, offd:offd + DT_W]
        dt_ref[:, DT_W:] = jnp.zeros((TB, 128 - DT_W), dt_ref.dtype)

    return pl.pallas_call(
        body, name=name, grid=(d // TB,),
        in_specs=[pl.BlockSpec((4, TB, n_shard), lambda i: (0, i, 0))],
        out_specs=[pl.BlockSpec((TB, n_main), lambda i: (i, 0)), pl.BlockSpec((TB, 128), lambda i: (i, 0))],
        out_shape=[jax.ShapeDtypeStruct((d, n_main), slots.dtype), jax.ShapeDtypeStruct((d, 128), slots.dtype)],
        compiler_params=_params(("parallel",)),
    )(slots)


def _unpack_dw_in(dw_main, dw_dt, *, name):
    d, n_main = dw_main.shape
    n_shard = (n_main + DT_W) // 4
    runs, (jd, offd) = _w_in_runs(n_shard, n_main)

    def body(main_ref, dt_ref, o_ref):
        for p, j, off, ln in runs:
            o_ref[j, :, off:off + ln] = main_ref[:, p:p + ln]
        o_ref[jd, :, offd:offd + DT_W] = dt_ref[:, 0:DT_W]

    return pl.pallas_call(
        body, name=name, grid=(d // TB,),
        in_specs=[pl.BlockSpec((TB, n_main), lambda i: (i, 0)), pl.BlockSpec((TB, 128), lambda i: (i, 0))],
        out_specs=pl.BlockSpec((4, TB, n_shard), lambda i: (0, i, 0)),
        out_shape=jax.ShapeDtypeStruct((4, d, n_shard), dw_main.dtype),
        compiler_params=_params(("parallel",)),
    )(dw_main, dw_dt)


def _half(ref, chip_idx, hc, piece=(0, 1)):
    q, nq = piece
    rp = ref.shape[1] // (2 * nq)
    return ref.at[chip_idx, pl.ds((hc * nq + q) * rp, rp), :]


def _job_gather_ici(i_ref, g_ref, send_sems, recv_sems, base, piece):
    x, y, c = _place()
    j = 2 * x + y
    sends, recvs = [], []
    for k, (px, py) in enumerate(_other_chips(x, y)):
        kw = dict(send_sem=send_sems.at[base + k], recv_sem=recv_sems.at[base + k], device_id=(px, py, c), device_id_type=MESH)
        sends.append(pltpu.make_async_remote_copy(src_ref=_half(i_ref, j, c, piece), dst_ref=_half(g_ref, j, c, piece), **kw))
        landed = _half(g_ref, 2 * px + py, c, piece)
        recvs.append(pltpu.make_async_remote_copy(src_ref=landed, dst_ref=landed, **kw))
    return sends, recvs


def _job_gather_sibling(i_ref, g_ref, send_sems, recv_sems, base, piece):
    x, y, c = _place()
    sends, recvs = [], []
    for k, (px, py) in enumerate(_other_chips(x, y)):
        kw = dict(send_sem=send_sems.at[base + k], recv_sem=recv_sems.at[base + k], device_id=(x, y, 1 - c), device_id_type=MESH)
        sends.append(pltpu.make_async_remote_copy(src_ref=_half(i_ref, 2 * px + py, c, piece),
                                                  dst_ref=_half(g_ref, 2 * px + py, c, piece), **kw))
        other = _half(g_ref, 2 * px + py, 1 - c, piece)
        recvs.append(pltpu.make_async_remote_copy(src_ref=other, dst_ref=other, **kw))
    return sends, recvs


def _job_reduce_ici(s_ref, got_refs, send_sems, recv_sems, base, piece):
    x, y, c = _place()
    q, nq = piece
    rp = s_ref.shape[1] // nq
    rows = pl.ds(q * rp, rp)
    sends = [pltpu.make_async_remote_copy(src_ref=s_ref.at[2 * px + py, rows, :], dst_ref=got_refs[k].at[rows, :],
                                          send_sem=send_sems.at[base + k], recv_sem=recv_sems.at[base + k],
                                          device_id=(px, py, c), device_id_type=MESH)
             for k, (px, py) in enumerate(_other_chips(x, y))]
    return sends, sends


GATHER_PLAN = {
    "f_inproj": [("w_out", _job_gather_ici, (0, 1)), ("w_mlp_in", _job_gather_ici, (0, 2))],
    "f_ssdconv": [("w_mlp_in", _job_gather_ici, (1, 2)), ("w_out", _job_gather_sibling, (0, 1)),
                  ("w_mlp_in", _job_gather_sibling, (0, 2))],
    "f_lruconv": [("w_mlp_in", _job_gather_sibling, (1, 2))],
    "f_ssd": [("w_mlp_out", _job_gather_ici, (0, 2))],
    "f_lrugates": [("w_mlp_out", _job_gather_sibling, (0, 2))],
    "f_outproj": [("w_mlp_out", _job_gather_ici, (1, 2))],
    "f_mlpin": [("w_mlp_out", _job_gather_sibling, (1, 2))],
}
SIBLING_PLAN = {"b_mlpin_dx": "w_mlp_out", "b_mid": "w_mlp_in", "b_outproj_dx": "w_out"}
REDUCE_PLAN = {
    "b_mlpin_dw": [("w_mlp_out", (0, 2))],
    "b_outproj_dw": [("w_mlp_out", (1, 2))],
    "b_ssd": [("w_mlp_in", (0, 1)), ("w_out", (0, 1))],
    "b_inproj_dx": [("w_in", (0, 1))],
}
SHARE_PLAN = {"b_ssdconv": ["w_mlp_out", "w_mlp_in", "w_out"]}
SMALL_PLAN = {"b_inproj_dw": 0, "b_inproj_dx": 1}


class _Part:
    def __init__(self, ins, outs, aliases, n_sems, make, done):
        self.ins, self.outs, self.aliases, self.n_sems, self.make, self.done = ins, outs, aliases, n_sems, make, done


def _merge_parts(parts):
    ins, outs, aliases, offs, n = [], [], {}, [], 0
    for p in parts:
        offs.append((len(ins), len(outs), n))
        aliases.update({len(ins) + i: len(outs) + o for i, o in p.aliases.items()})
        ins, outs, n = ins + list(p.ins), outs + list(p.outs), n + p.n_sems

    def make(i_refs, o_refs, send_sems, recv_sems):
        sends, recvs = [], []
        for p, (io, oo, so) in zip(parts, offs):
            s, r = p.make(i_refs[io:io + len(p.ins)], o_refs[oo:oo + len(p.outs)], send_sems, recv_sems, so)
            sends, recvs = sends + s, recvs + r
        return sends, recvs

    return _Side(ins, outs, aliases, n, make)


def _parts_done(parts, outs):
    o = 0
    for p in parts:
        p.done(list(outs[o:o + len(p.outs)]))
        o += len(p.outs)


def _comm_call(parts, *, name):
    side = _merge_parts(parts)
    si, so = len(side.ins), len(side.outs)

    def body(*refs):
        sends, recvs = side.make(refs[:si], refs[si:si + so], refs[-2], refs[-1])
        for d in sends:
            d.start()
        for d in recvs:
            d.wait_recv()
        for d in sends:
            d.wait_send()

    outs = pl.pallas_call(
        body, name=name, out_shape=list(side.outs), in_specs=[ANY] * si, out_specs=[ANY] * so,
        scratch_shapes=[pltpu.SemaphoreType.DMA((side.n_sems,)), pltpu.SemaphoreType.DMA((side.n_sems,))],
        input_output_aliases=side.aliases,
    )(*side.ins)
    _parts_done(parts, outs)


class _DistPlan:
    def __init__(self, slots, w_main, w_dt, where, shapes):
        self.slots, self.w_main, self.w_dt, self.where, self.shapes = slots, w_main, w_dt, where, shapes
        self.sib_pending, self.partial, self.got, self.totals, self.shared = {}, {}, {}, {}, {}
        self.small_buf = None

    def weight(self, name):
        d = self.w_main.shape[0]
        return {"w_main": lambda: self.w_main, "w_dt": lambda: self.w_dt, "w_out": lambda: self.slots["w_out"].reshape(-1, d),
                "w_mi": lambda: self.slots["w_mlp_in"], "w_mo": lambda: self.slots["w_mlp_out"].reshape(-1, d)}[name]()

    def _gather_part(self, jobs):
        names = []
        for n, _, _ in jobs:
            if n not in names:
                names.append(n)
        arrs = [self.slots[n] for n in names]

        def make(i_refs, o_refs, send_sems, recv_sems, base):
            sends, recvs = [], []
            for q, (n, job, piece) in enumerate(jobs):
                s, r = job(i_refs[names.index(n)], o_refs[names.index(n)], send_sems, recv_sems, base + 3 * q, piece)
                sends, recvs = sends + s, recvs + r
            return sends, recvs

        return _Part(arrs, [jax.ShapeDtypeStruct(a.shape, a.dtype) for a in arrs], {i: i for i in range(len(arrs))},
                     3 * len(jobs), make, lambda outs: self.slots.update(zip(names, outs)))

    def _sibling_part(self, name):
        g = self.sib_pending.pop(name)
        _, _, rh, cc = g.shape

        def make(i_refs, o_refs, send_sems, recv_sems, base):
            x, y, c = _place()
            sends = [pltpu.make_async_remote_copy(src_ref=i_refs[0].at[s, 1 - c], dst_ref=o_refs[0].at[s],
                                                  send_sem=send_sems.at[base + s], recv_sem=recv_sems.at[base + s],
                                                  device_id=(x, y, 1 - c), device_id_type=MESH) for s in range(4)]
            return sends, sends

        def done(outs):
            (t,) = outs
            self.partial[name] = _blocks(
                lambda u, v: (u.astype(F32) + v.astype(F32),),
                [(g, (None, None, TB, cc), lambda q, i, s: (q, s[1], i, 0)), (t, (None, TB, cc), lambda q, i, s: (q, i, 0))],
                [(jax.ShapeDtypeStruct(t.shape, BF16), (None, TB, cc), lambda q, i, s: (q, i, 0))],
                grid=(4, rh // TB), name="add_sibling_" + name, prefetch=self.where)[0]

        return _Part([g], [jax.ShapeDtypeStruct((4, rh, cc), g.dtype)], {}, 4, make, done)

    def _reduce_part(self, items):
        parts = [self.partial[n] for n, _ in items]
        ins, aliases = list(parts), {}
        for q, (n, (piece, _)) in enumerate(items):
            if piece > 0:
                for k in range(3):
                    aliases[len(ins)] = 3 * q + k
                    ins.append(self.got[n][k])

        def make(i_refs, o_refs, send_sems, recv_sems, base):
            sends = []
            for q, (_, piece) in enumerate(items):
                sends += _job_reduce_ici(i_refs[q], o_refs[3 * q:3 * q + 3], send_sems, recv_sems, base + 3 * q, piece)[0]
            return sends, sends

        def done(outs):
            for q, (name, (piece, n_pieces)) in enumerate(items):
                g3 = self.got[name] = outs[3 * q:3 * q + 3]
                if piece < n_pieces - 1:
                    continue
                s4 = self.partial[name]
                _, rh, cc = s4.shape
                row = ((TB, cc), lambda i, s: (i, 0))
                self.totals[name] = _blocks(
                    lambda o, r0, r1, r2: (((o.astype(F32) + r0.astype(F32)) + r1.astype(F32)) + r2.astype(F32),),
                    [(s4, (None, TB, cc), lambda i, s: (s[0], i, 0)), (g3[0],) + row, (g3[1],) + row, (g3[2],) + row],
                    [(jax.ShapeDtypeStruct((2, rh, cc), F32), (None, TB, cc), lambda i, s: (s[1], i, 0))],
                    grid=(rh // TB,), name="add_chips_" + name, prefetch=self.where)[0]
                if name == "w_in":
                    _comm_call([self._share_part([name])], name="share_" + name)

        outs = [jax.ShapeDtypeStruct(s.shape[1:], s.dtype) for s in parts for _ in range(3)]
        return _Part(ins, outs, aliases, 3 * len(items), make, done)

    def _share_part(self, names):
        ts = [self.totals[n] for n in names]

        def make(i_refs, o_refs, send_sems, recv_sems, base):
            x, y, c = _place()
            sends, recvs = [], []
            for w in range(len(ts)):
                kw = dict(send_sem=send_sems.at[base + w], recv_sem=recv_sems.at[base + w], device_id=(x, y, 1 - c),
                          device_id_type=MESH)
                sends.append(pltpu.make_async_remote_copy(src_ref=i_refs[w].at[c], dst_ref=o_refs[w].at[c], **kw))
                recvs.append(pltpu.make_async_remote_copy(src_ref=i_refs[w].at[c], dst_ref=o_refs[w].at[1 - c], **kw))
            return sends, recvs

        def done(outs):
            self.shared.update({n: o.reshape(self.shapes[n]) for n, o in zip(names, outs)})

        return _Part(ts, [jax.ShapeDtypeStruct(t.shape, t.dtype) for t in ts], {i: i for i in range(len(ts))}, len(ts), make, done)

    def _small_part(self, stage):
        buf = self.small_buf

        def make(i_refs, o_refs, send_sems, recv_sems, base):
            x, y, c = _place()
            src, dst = i_refs[0], o_refs[0]
            sends, recvs = [], []

            def add(k, block, to, frm):
                kw = dict(send_sem=send_sems.at[base + k], recv_sem=recv_sems.at[base + k], device_id=to, device_id_type=MESH)
                sends.append(pltpu.make_async_remote_copy(src_ref=src.at[block], dst_ref=dst.at[block], **kw))
                recvs.append(pltpu.make_async_remote_copy(src_ref=src.at[frm], dst_ref=dst.at[frm], **kw))

            if stage == 0:
                add(0, 4 * x + 2 * y + c, (x, y, 1 - c), 4 * x + 2 * y + 1 - c)
            for k, (px, py) in enumerate(_other_chips(x, y)):
                if stage == 0:
                    add(1 + k, 4 * x + 2 * y + c, (px, py, c), 4 * px + 2 * py + c)
                else:
                    add(k, 4 * px + 2 * py + c, (x, y, 1 - c), 4 * px + 2 * py + 1 - c)
            return sends, recvs

        def done(outs):
            (self.small_buf,) = outs

        return _Part([buf], [jax.ShapeDtypeStruct(buf.shape, buf.dtype)], {0: 0}, 4 if stage == 0 else 3, make, done)

    def side(self, kernel_name):
        parts = []
        if kernel_name in GATHER_PLAN:
            parts.append(self._gather_part(GATHER_PLAN[kernel_name]))
        if kernel_name in SIBLING_PLAN:
            parts.append(self._sibling_part(SIBLING_PLAN[kernel_name]))
        if kernel_name in REDUCE_PLAN:
            parts.append(self._reduce_part(REDUCE_PLAN[kernel_name]))
        if kernel_name in SHARE_PLAN:
            parts.append(self._share_part(SHARE_PLAN[kernel_name]))
        if kernel_name in SMALL_PLAN:
            parts.append(self._small_part(SMALL_PLAN[kernel_name]))
        self._carried = parts
        return _merge_parts(parts) if parts else None

    def done(self, kernel_name, side_outs):
        _parts_done(self._carried, side_outs)

    def grad(self, name, g):
        if name == "w_in":
            g = _unpack_dw_in(*g, name="unpack_dw_in")
        self.sib_pending[name] = g.reshape(4, 2, g.shape[1] // 2, g.shape[2])
        if name == "w_in":
            _comm_call([self._sibling_part(name)], name="reduce_sibling_" + name)

    def small(self, grads):
        packed = _pack([grads[n] for n, _ in SMALL_EARLY])
        n_rows = packed.shape[0]
        self.small_buf = _blocks(lambda t: (t,), [(packed, (n_rows, 128), lambda i, s: (0, 0))],
                                 [(jax.ShapeDtypeStruct((8, n_rows, 128), F32), (None, n_rows, 128), lambda i, s: (s[2], 0, 0))],
                                 grid=(1,), name="place_small", prefetch=self.where)[0]

    def finish(self):
        assert len(self.shared) == len(self.shapes)
        return self.shared


def kernel(x, pre_mix_norm, w_in, ssd_conv_w, ssd_conv_b, ssd_dt_bias, ssd_a_log, ssd_d, ssd_norm, lru_conv_w, lru_conv_b, lru_w_a, lru_b_a, lru_w_x, lru_b_x, lru_lambda, lru_norm, w_out, post_mix_norm, pre_mlp_norm, w_mlp_in, w_mlp_out, post_mlp_norm, loss_target, m_pre_mix_norm, m_w_in, m_ssd_conv_w, m_ssd_conv_b, m_ssd_dt_bias, m_ssd_a_log, m_ssd_d, m_ssd_norm, m_lru_conv_w, m_lru_conv_b, m_lru_w_a, m_lru_b_a, m_lru_w_x, m_lru_b_x, m_lru_lambda, m_lru_norm, m_w_out, m_post_mix_norm, m_pre_mlp_norm, m_w_mlp_in, m_w_mlp_out, m_post_mlp_norm, v_pre_mix_norm, v_w_in, v_ssd_conv_w, v_ssd_conv_b, v_ssd_dt_bias, v_ssd_a_log, v_ssd_d, v_ssd_norm, v_lru_conv_w, v_lru_conv_b, v_lru_w_a, v_lru_b_a, v_lru_w_x, v_lru_b_x, v_lru_lambda, v_lru_norm, v_w_out, v_post_mix_norm, v_pre_mlp_norm, v_w_mlp_in, v_w_mlp_out, v_post_mlp_norm):
    a = dict(locals())
    j = 2 * lax.axis_index("x") + lax.axis_index("y")
    D = x.shape[-1]
    c_dt = D_SSD + D_XBC

    core = lax.axis_index("c")
    where = jnp.stack([j, core, 2 * j + core]).astype(jnp.int32)
    slots = {}
    for name in LARGE:
        w = a[name][0]
        r, cc = w.shape
        slots[name] = _blocks(lambda t: (t,), [(w, (TB, cc), lambda i, s: (i, 0))],
                              [(jax.ShapeDtypeStruct((4, r, cc), BF16), (None, TB, cc), lambda i, s: (s[0], i, 0))],
                              grid=(r // TB,), name="cast_" + name, prefetch=where)[0]
    g_in = _gather_tree(slots.pop("w_in"), name="gather_w_in")
    w_main, w_dt = _pack_w_in(g_in, name="pack_w_in")
    taps = jnp.concatenate([ssd_conv_w[0].reshape(-1, 128), lru_conv_w[0].reshape(-1, 128)], axis=0)
    taps = _allgather8(taps, name="gather_taps").reshape(8, taps.shape[0], 128)[0::2]
    n_ssd = ssd_conv_w.shape[1] * ssd_conv_w.shape[2] // 128
    ssd_taps = taps[:, :n_ssd].reshape(4, CONV_W, -1).transpose(1, 0, 2).reshape(CONV_W, -1)
    lru_taps = taps[:, n_ssd:].reshape(4, CONV_W, -1).transpose(1, 0, 2).reshape(CONV_W, -1)

    def row128(v):
        return jnp.pad(v, ((0, 0), (0, 128 - v.shape[1])))

    p = dict(pre_mix_norm=pre_mix_norm, ssd_conv_w=ssd_taps, ssd_conv_b=ssd_conv_b,
             dtb=row128(ssd_dt_bias), alog=row128(ssd_a_log), drow=row128(ssd_d), ssd_norm=ssd_norm,
             lru_conv_w=lru_taps, lru_conv_b=lru_conv_b, lru_w_a=lru_w_a[0], lru_b_a=lru_b_a.reshape(1, -1),
             lru_w_x=lru_w_x[0], lru_b_x=lru_b_x.reshape(1, -1), lru_lambda=lru_lambda, lru_norm=lru_norm,
             post_mix_norm=post_mix_norm, pre_mlp_norm=pre_mlp_norm, post_mlp_norm=post_mlp_norm)
    plan = _DistPlan(slots, w_main, w_dt, where, {n: a[n].shape for n in LARGE})
    grad_x, small = _local_step(x[0], loss_target[0], p, plan)

    large_grads = plan.finish()

    delta, new_m, new_v = {}, {}, {}
    for name in LARGE:
        w = a[name][0]
        cc = w.shape[1]
        outs = _rowwise(_adamw, [w, large_grads[name][0], a["m_" + name][0], a["v_" + name][0]], [], [(cc, F32)] * 3, [],
                        name="adamw_" + name, tr=128, sub=8)
        delta[name], new_m[name], new_v[name] = [o[None] for o in outs]

    total = _sum8(plan.small_buf, name="sum_small")
    tot = dict(zip([n for n, _ in SMALL_EARLY], _unpack(total, [s for _, s in SMALL_EARLY])))
    late = small["pre_mix_norm"].reshape(-1, 128)
    late = _allgather8(late, name="gather_late").reshape(8, late.shape[0], 128)
    tot["pre_mix_norm"] = _sum8(late, name="sum_late").reshape(small["pre_mix_norm"].shape)
    loss = tot["loss"][0, 0]
    n_sc, n_lc = ssd_conv_w.shape[2], lru_conv_w.shape[2]
    grads = dict(
        pre_mix_norm=tot["pre_mix_norm"],
        ssd_conv_w=lax.dynamic_slice(tot["ssd_conv"][:CONV_W], (0, j * n_sc), (CONV_W, n_sc))[None],
        ssd_conv_b=tot["ssd_conv"][CONV_W:CONV_W + 1],
        ssd_dt_bias=tot["ssd_par"][0:1, :DT_W], ssd_a_log=tot["ssd_par"][1:2, :DT_W], ssd_d=tot["ssd_par"][2:3, :DT_W],
        ssd_norm=tot["ssd_norm"],
        lru_conv_w=lax.dynamic_slice(tot["lru_conv"][:CONV_W], (0, j * n_lc), (CONV_W, n_lc))[None],
        lru_conv_b=tot["lru_conv"][CONV_W:CONV_W + 1],
        lru_w_a=tot["lru_w_a"][None], lru_b_a=tot["lru_b_a"].reshape(lru_b_a.shape),
        lru_w_x=tot["lru_w_x"][None], lru_b_x=tot["lru_b_x"].reshape(lru_b_x.shape),
        lru_lambda=tot["lru_lambda"], lru_norm=tot["lru_norm"], post_mix_norm=tot["post_mix_norm"],
        pre_mlp_norm=tot["pre_mlp_norm"], post_mlp_norm=tot["post_mlp_norm"])

    grads.update(large_grads)

    small_w = [n for n in WEIGHTS if n not in LARGE]
    n_sw = len(small_w)

    def flat(v):
        return v.reshape(-1, v.shape[-1])

    def adamw_all(*refs):
        ins, outs = refs[:4 * n_sw], refs[4 * n_sw:]
        for q in range(n_sw):
            res = _adamw(*[ins[4 * q + t][...] for t in range(4)])
            for t in range(3):
                outs[3 * q + t][...] = res[t]

    operands = [flat(d[n]) for n in small_w for d in (a, grads, {k: a["m_" + k] for k in small_w}, {k: a["v_" + k] for k in small_w})]
    outs = pl.pallas_call(
        adamw_all, name="adamw_small",
        out_shape=[jax.ShapeDtypeStruct(flat(a[n]).shape, F32) for n in small_w for _ in range(3)],
        in_specs=[pl.BlockSpec(memory_space=pltpu.VMEM)] * (4 * n_sw), out_specs=[pl.BlockSpec(memory_space=pltpu.VMEM)] * (3 * n_sw),
        compiler_params=pltpu.CompilerParams(vmem_limit_bytes=VMEM_LIMIT),
    )(*operands)
    for q, n in enumerate(small_w):
        delta[n], new_m[n], new_v[n] = [o.reshape(a[n].shape) for o in outs[3 * q:3 * q + 3]]

    return (loss, grad_x[None], *[grads[n] for n in WEIGHTS], *[delta[n] for n in WEIGHTS],
            *[new_m[n] for n in WEIGHTS], *[new_v[n] for n in WEIGHTS])
```

```python
import functools

import jax
import jax.numpy as jnp
from jax import lax
from jax.experimental import pallas as pl
from jax.experimental.pallas import tpu as pltpu

F32 = jnp.float32
BF16 = jnp.bfloat16
MESH = pl.DeviceIdType.MESH

EPS = 1e-6
LRU_C = 8.0
ADAM_LR = 0.001
ADAM_B1 = 0.9
ADAM_B2 = 0.999
ADAM_EPS = 1e-08
ADAM_WD = 0.01
ADAM_STEP = 10

N_GROUPS = 8
HEADS_PER_GROUP = 4
HEAD_DIM = 64
GROUP_W = HEADS_PER_GROUP * HEAD_DIM
STATE = 128
LRU_HEADS = 16
LRU_BLOCK = 128
CONV_W = 4
SSD_CHUNK = 256
HALO = 8

VMEM_LIMIT = 48 * 1024 * 1024


def _params(sem=None):
    return pltpu.CompilerParams(dimension_semantics=sem, vmem_limit_bytes=VMEM_LIMIT)


@jax.custom_jvp
def _log1p(x):
    u = 1.0 + x
    d = u - 1.0
    return jnp.where(d == 0.0, x, jnp.log(u) * (x / jnp.where(d == 0.0, 1.0, d)))


@_log1p.defjvp
def _log1p_jvp(primals, tangents):
    (x,), (t,) = primals, tangents
    return _log1p(x), t / (1.0 + x)


@jax.custom_jvp
def _expm1(x):
    u = jnp.exp(x)
    lu = jnp.log(u)
    safe = jnp.where(lu == 0.0, 1.0, lu)
    y = (u - 1.0) * (x / safe)
    y = jnp.where(lu == 0.0, x, y)
    return jnp.where(u == 0.0, -1.0, y)


@_expm1.defjvp
def _expm1_jvp(primals, tangents):
    (x,), (t,) = primals, tangents
    return _expm1(x), t * jnp.exp(x)


def _softplus(x):
    return jnp.maximum(x, 0.0) + _log1p(jnp.exp(-jnp.abs(x)))


def _sigmoid(x):
    return 1.0 / (1.0 + jnp.exp(-x))


def _silu(x):
    return x * _sigmoid(x)


def _gelu(x):
    c = 0.7978845608028654
    return 0.5 * x * (1.0 + jnp.tanh(c * (x + 0.044715 * (x * x * x))))


def _rms(x, g):
    return x * lax.rsqrt(jnp.mean(x * x, axis=-1, keepdims=True) + EPS) * g


def _dot(a, b, dims):
    return lax.dot_general(a.astype(BF16), b.astype(BF16), (dims, ((), ())), preferred_element_type=F32)


_NN = ((1,), (0,))
_NT = ((1,), (1,))
_TN = ((0,), (0,))


ANY = pl.BlockSpec(memory_space=pl.ANY)


class _Side:
    def __init__(self, ins, outs, aliases, n_sems, make):
        self.ins, self.outs, self.aliases, self.n_sems, self.make = ins, outs, aliases, n_sems, make


def _call(body, args, *, name, grid, in_specs, out_specs, out_shape, scratch_shapes=(), sem=None, side=None, into=None):
    in_specs, out_specs, out_shape, scratch_shapes = list(in_specs), list(out_specs), list(out_shape), list(scratch_shapes)
    held = []
    for oi, (buf, n_cols, off) in (into or {}).items():
        spec = out_specs[oi]
        out_shape[oi] = jax.ShapeDtypeStruct((out_shape[oi].shape[0], n_cols), out_shape[oi].dtype)
        out_specs[oi] = pl.BlockSpec(spec.block_shape, lambda *idx, _m=spec.index_map, _o=off: (_m(*idx)[0], _m(*idx)[1] + _o))
        if buf is not None:
            held.append((oi, buf))
    if side is None and not held:
        outs = pl.pallas_call(body, name=name, grid=grid, in_specs=in_specs, out_specs=out_specs, out_shape=out_shape,
                              scratch_shapes=scratch_shapes, compiler_params=_params(sem))(*args)
        return list(outs), []
    side_ins, side_outs = (list(side.ins), list(side.outs)) if side is not None else ([], [])
    n_in, n_out, n_scr, nh, si, so = len(in_specs), len(out_specs), len(scratch_shapes), len(held), len(side_ins), len(side_outs)

    def full(*refs):
        s_in = refs[n_in + nh:n_in + nh + si]
        o0 = n_in + nh + si
        s_out = refs[o0 + n_out:o0 + n_out + so]
        scr = refs[o0 + n_out + so:o0 + n_out + so + n_scr]
        if side is None:
            body(*refs[:n_in], *refs[o0:o0 + n_out], *scr)
            return
        send_sems, recv_sems = refs[-2], refs[-1]
        ids = [pl.program_id(d) for d in range(len(grid))]
        first = functools.reduce(jnp.logical_and, [i == 0 for i in ids])
        last = functools.reduce(jnp.logical_and, [i == g - 1 for i, g in zip(ids, grid)])

        @pl.when(first)
        def _():
            for d in side.make(s_in, s_out, send_sems, recv_sems)[0]:
                d.start()

        body(*refs[:n_in], *refs[o0:o0 + n_out], *scr)

        @pl.when(last)
        def _():
            sends, recvs = side.make(s_in, s_out, send_sems, recv_sems)
            for d in recvs:
                d.wait_recv()
            for d in sends:
                d.wait_send()

    aliases = {n_in + h: oi for h, (oi, _) in enumerate(held)}
    if side is not None:
        aliases.update({n_in + nh + i: n_out + o for i, o in side.aliases.items()})
        scratch_shapes = scratch_shapes + [pltpu.SemaphoreType.DMA((side.n_sems,)), pltpu.SemaphoreType.DMA((side.n_sems,))]
    outs = pl.pallas_call(
        full, name=name, grid=grid, in_specs=in_specs + [ANY] * (nh + si), out_specs=out_specs + [ANY] * so,
        out_shape=out_shape + side_outs, scratch_shapes=scratch_shapes, input_output_aliases=aliases,
        compiler_params=_params(("arbitrary",) * len(grid) if side is not None else sem),
    )(*args, *[b for _, b in held], *side_ins)
    return list(outs[:n_out]), list(outs[n_out:])


def _matmul(a, b, *, mode, m, n, k, tm, tn, tk, out_dtypes, name, a_spec=None, b_spec=None,
            out_specs=None, out_shapes=None, extras=(), epilogue=None, side=None):
    tm, tn, tk = min(tm, m), min(tn, n), min(tk, k)
    assert m % tm == 0 and n % tn == 0 and k % tk == 0, (name, m, n, k, tm, tn, tk)
    nk = k // tk
    dims = {"nn": _NN, "nt": _NT, "tn": _TN}[mode]
    if a_spec is None:
        a_spec = pl.BlockSpec((tk, tm), lambda i, j, kk: (kk, i)) if mode == "tn" else pl.BlockSpec((tm, tk), lambda i, j, kk: (i, kk))
    if b_spec is None:
        b_spec = pl.BlockSpec((tn, tk), lambda i, j, kk: (j, kk)) if mode == "nt" else pl.BlockSpec((tk, tn), lambda i, j, kk: (kk, j))
    tile = pl.BlockSpec((tm, tn), lambda i, j, kk: (i, j))
    if out_specs is None:
        out_specs = [tile for _ in out_dtypes]
    if out_shapes is None:
        out_shapes = [jax.ShapeDtypeStruct((m, n), d) for d in out_dtypes]
    n_ex, n_out = len(extras), len(out_dtypes)

    def body(*refs):
        a_ref, b_ref = refs[0], refs[1]
        ex_refs = refs[2:2 + n_ex]
        o_refs = refs[2 + n_ex:2 + n_ex + n_out]
        def finish(r):
            outs = epilogue(r, *[e[...] for e in ex_refs]) if epilogue is not None else (r,)
            for o_ref, o in zip(o_refs, outs):
                o_ref[...] = o.astype(o_ref.dtype)

        if nk == 1:
            finish(_dot(a_ref[...], b_ref[...], dims))
            return
        acc = refs[-1]
        kk = pl.program_id(2)

        @pl.when(kk == 0)
        def _():
            acc[...] = _dot(a_ref[...], b_ref[...], dims)

        @pl.when(kk > 0)
        def _():
            acc[...] += _dot(a_ref[...], b_ref[...], dims)

        @pl.when(kk == nk - 1)
        def _():
            finish(acc[...])

    outs, side_outs = _call(
        body, (a, b, *extras), name=name, grid=(m // tm, n // tn, nk),
        in_specs=[a_spec, b_spec] + [tile for _ in extras], out_specs=out_specs, out_shape=out_shapes,
        scratch_shapes=[] if nk == 1 else [pltpu.VMEM((tm, tn), F32)], sem=("parallel", "parallel", "arbitrary"), side=side)
    return outs if side is None else (outs, side_outs)


def _rowwise(fn, rows, bcast, out_rows, out_acc, *, name, tr, sub, side=None, into=None):
    rows = [r if isinstance(r, tuple) else (r, 0, r.shape[1]) for r in rows]
    row_specs = []
    for arr, c0, w in rows:
        assert c0 % w == 0, (name, c0, w)
        row_specs.append((w, c0 // w))
    rows = [r[0] for r in rows]
    L = rows[0].shape[0]
    tr = min(tr, L)
    sub = min(sub, tr)
    assert L % tr == 0 and tr % sub == 0, (name, L, tr, sub)
    n_r, n_b, n_or, n_oa = len(rows), len(bcast), len(out_rows), len(out_acc)

    def body(*refs):
        r_refs = refs[:n_r]
        b_refs = refs[n_r:n_r + n_b]
        or_refs = refs[n_r + n_b:n_r + n_b + n_or]
        oa_refs = refs[n_r + n_b + n_or:]
        i = pl.program_id(0)

        @pl.when(i == 0)
        def _():
            for o in oa_refs:
                o[...] = jnp.zeros_like(o)

        bvals = [b[...] for b in b_refs]

        def step(s, carry):
            r0 = pl.multiple_of(s * sub, sub)
            tiles = [r[pl.ds(r0, sub), :] for r in r_refs]
            outs = fn(*tiles, *bvals)
            for o_ref, o in zip(or_refs, outs[:n_or]):
                o_ref[pl.ds(r0, sub), :] = o.astype(o_ref.dtype)
            for o_ref, o in zip(oa_refs, outs[n_or:]):
                o_ref[...] += o
            return carry

        if tr == sub:
            step(0, 0)
        else:
            lax.fori_loop(0, tr // sub, step, 0, unroll=8)

    def whole(shape):
        nd = len(shape)
        return pl.BlockSpec(shape, lambda i, _n=nd: (0,) * _n)

    outs, side_outs = _call(
        body, (*rows, *bcast), name=name, grid=(L // tr,),
        in_specs=[pl.BlockSpec((tr, w), lambda i, _c=cb: (i, _c)) for w, cb in row_specs] + [whole(b.shape) for b in bcast],
        out_specs=[pl.BlockSpec((tr, c), lambda i: (i, 0)) for c, _ in out_rows] + [whole(s) for s in out_acc],
        out_shape=[jax.ShapeDtypeStruct((L, c), d) for c, d in out_rows] + [jax.ShapeDtypeStruct(s, F32) for s in out_acc],
        sem=("arbitrary",), side=side, into=into)
    return outs if side is None else (outs, side_outs)


def _colsum(x):
    return jnp.sum(x, axis=0, keepdims=True)


def _grouped(fn, rows, params, out_rows, out_acc, *, gw, name, tr, side=None, into=None):
    rows = [r if isinstance(r, tuple) else (r, 0) for r in rows]
    L = rows[0][0].shape[0]
    tr = min(tr, L)
    assert L % tr == 0
    G = None
    for p in params:
        G = p.shape[0] if p.ndim == 3 else p.shape[1] // gw
    cw = G * gw
    n_r, n_p, n_or, n_oa = len(rows), len(params), len(out_rows), len(out_acc)

    def pick(ref, g):
        return ref[g] if len(ref.shape) == 3 else ref[:, g * gw:(g + 1) * gw]

    def body(*refs):
        r_refs = refs[:n_r]
        p_refs = refs[n_r:n_r + n_p]
        or_refs = refs[n_r + n_p:n_r + n_p + n_or]
        oa_refs = refs[n_r + n_p + n_or:]

        @pl.when(pl.program_id(0) == 0)
        def _():
            for o in oa_refs:
                o[...] = jnp.zeros_like(o)

        for g in range(G):
            outs = fn(*[pick(r, g) for r in r_refs], *[pick(p, g) for p in p_refs])
            for o_ref, o in zip(or_refs, outs[:n_or]):
                o_ref[:, g * gw:(g + 1) * gw] = o.astype(o_ref.dtype)
            for o_ref, o in zip(oa_refs, outs[n_or:]):
                if len(o_ref.shape) == 3:
                    o_ref[g] += o
                else:
                    o_ref[:, g * gw:(g + 1) * gw] += o

    def whole(shape):
        nd = len(shape)
        return pl.BlockSpec(shape, lambda i, _n=nd: (0,) * _n)

    for _, c0 in rows:
        assert c0 % cw == 0
    outs, side_outs = _call(
        body, (*[r[0] for r in rows], *params), name=name, grid=(L // tr,),
        in_specs=[pl.BlockSpec((tr, cw), lambda i, _c=c0 // cw: (i, _c)) for _, c0 in rows] + [whole(p.shape) for p in params],
        out_specs=[pl.BlockSpec((tr, cw), lambda i: (i, 0)) for _ in out_rows] + [whole(s) for s in out_acc],
        out_shape=[jax.ShapeDtypeStruct((L, cw), d) for d in out_rows] + [jax.ShapeDtypeStruct(s, F32) for s in out_acc],
        sem=("arbitrary",), side=side, into=into)
    return outs if side is None else (outs, side_outs)


def _dsilu(p):
    s = _sigmoid(p)
    return s + p * s * (1.0 - s)


CONV_RC, CONV_CC = 32, 512


def _past_window(x_ref, head, r0, k, cs):
    if r0 == 0:
        return head[HALO - 3 + k:HALO - 3 + k + CONV_RC, cs]
    return x_ref[r0 - 3 + k:r0 - 3 + k + CONV_RC, cs]


def _conv_fwd(x, c0, C, w, b, *, silu, name, tr=512, cb=1024, side=None):
    L = x.shape[0]
    tr = min(tr, L)
    nb, hb = L // tr, tr // HALO
    assert L % tr == 0 and C % cb == 0 and c0 % cb == 0 and tr % CONV_RC == 0 and cb % CONV_CC == 0
    n_out = 2 if silu else 1

    def body(x_ref, h_ref, w_ref, b_ref, *rest):
        o_refs, head = rest[:n_out], rest[n_out]
        head[0:HALO, :] = h_ref[...]

        @pl.when(pl.program_id(0) == 0)
        def _():
            head[0:HALO, :] = jnp.zeros((HALO, cb), F32)

        head[HALO:, :] = x_ref[0:CONV_RC, :]
        for cc in range(cb // CONV_CC):
            cs = slice(cc * CONV_CC, (cc + 1) * CONV_CC)
            wv, bv = w_ref[:, cs], b_ref[:, cs]
            for r0 in range(0, tr, CONV_RC):
                y = bv
                for k in range(CONV_W):
                    y = y + wv[k:k + 1, :] * _past_window(x_ref, head, r0, k, cs)
                o_refs[0][r0:r0 + CONV_RC, cs] = y
                if silu:
                    o_refs[1][r0:r0 + CONV_RC, cs] = _silu(y)

    outs, side_outs = _call(
        body, (x, x, w, b), name=name, grid=(nb, C // cb),
        in_specs=[
            pl.BlockSpec((tr, cb), lambda i, j: (i, c0 // cb + j)),
            pl.BlockSpec((HALO, cb), lambda i, j: (jnp.maximum(i * hb - 1, 0), c0 // cb + j)),
            pl.BlockSpec((CONV_W, cb), lambda i, j: (0, j)),
            pl.BlockSpec((1, cb), lambda i, j: (0, j)),
        ],
        out_specs=[pl.BlockSpec((tr, cb), lambda i, j: (i, j)) for _ in range(n_out)],
        out_shape=[jax.ShapeDtypeStruct((L, C), F32) for _ in range(n_out)],
        scratch_shapes=[pltpu.VMEM((HALO + CONV_RC, cb), F32)], sem=("parallel", "parallel"), side=side)
    return outs if side is None else (outs, side_outs)


def _fold8(v):
    return (v[0:8] + v[8:16]) + (v[16:24] + v[24:32])


def _conv_bwd(dact, dc0, pre, x, xc0, C, w, *, silu, name, tr=512, cb=1024, side=None, into=None):
    L = x.shape[0]
    tr = min(tr, L)
    nb, hb = L // tr, tr // HALO
    last_h = L // HALO - 1
    assert L % tr == 0 and C % cb == 0 and tr % CONV_RC == 0 and cb % CONV_CC == 0

    def body(*refs):
        if silu:
            d_ref, dh_ref, p_ref, ph_ref, x_ref, xh_ref, w_ref, dx_ref, dwb_ref, dp, head = refs
        else:
            d_ref, dh_ref, x_ref, xh_ref, w_ref, dx_ref, dwb_ref, dp, head = refs
        i = pl.program_id(1)
        dp[tr:, :] = dh_ref[...] * _dsilu(ph_ref[...]) if silu else dh_ref[...]

        @pl.when(i == nb - 1)
        def _():
            dp[tr:, :] = jnp.zeros((HALO, cb), F32)

        head[0:HALO, :] = xh_ref[...]

        @pl.when(i == 0)
        def _():
            head[0:HALO, :] = jnp.zeros((HALO, cb), F32)
            dwb_ref[...] = jnp.zeros_like(dwb_ref)

        head[HALO:, :] = x_ref[0:CONV_RC, :]
        for cc in range(cb // CONV_CC):
            cs = slice(cc * CONV_CC, (cc + 1) * CONV_CC)
            for r0 in range(0, tr, CONV_RC):
                rs = slice(r0, r0 + CONV_RC)
                dp[rs, cs] = d_ref[rs, cs] * _dsilu(p_ref[rs, cs]) if silu else d_ref[rs, cs]
        for cc in range(cb // CONV_CC):
            cs = slice(cc * CONV_CC, (cc + 1) * CONV_CC)
            wv = w_ref[:, cs]
            acc = [jnp.zeros((8, CONV_CC), F32) for _ in range(CONV_W + 1)]
            for r0 in range(0, tr, CONV_RC):
                dm = dp[r0:r0 + CONV_RC, cs]
                dx = jnp.zeros((CONV_RC, CONV_CC), F32)
                for k in range(CONV_W):
                    dx = dx + wv[k:k + 1, :] * dp[r0 + 3 - k:r0 + 3 - k + CONV_RC, cs]
                    acc[k] = acc[k] + _fold8(dm * _past_window(x_ref, head, r0, k, cs))
                acc[CONV_W] = acc[CONV_W] + _fold8(dm)
                dx_ref[r0:r0 + CONV_RC, cs] = dx.astype(dx_ref.dtype)
            for k in range(CONV_W + 1):
                dwb_ref[k:k + 1, cs] += _colsum(acc[k])

    def main(c):
        return pl.BlockSpec((tr, cb), lambda j, i: (i, c // cb + j))

    def nxt(c):
        return pl.BlockSpec((HALO, cb), lambda j, i: (jnp.minimum((i + 1) * hb, last_h), c // cb + j))

    in_specs = [main(dc0), nxt(dc0)]
    args = [dact, dact]
    if silu:
        in_specs += [main(0), nxt(0)]
        args += [pre, pre]
    in_specs += [main(xc0), pl.BlockSpec((HALO, cb), lambda j, i: (jnp.maximum(i * hb - 1, 0), xc0 // cb + j)),
                 pl.BlockSpec((CONV_W, cb), lambda j, i: (0, j))]
    args += [x, x, w]
    outs, side_outs = _call(
        body, args, name=name, grid=(C // cb, nb), in_specs=in_specs,
        out_specs=[pl.BlockSpec((tr, cb), lambda j, i: (i, j)), pl.BlockSpec((8, cb), lambda j, i: (0, j))],
        out_shape=[jax.ShapeDtypeStruct((L, C), BF16), jax.ShapeDtypeStruct((8, C), F32)],
        scratch_shapes=[pltpu.VMEM((tr + HALO, cb), F32), pltpu.VMEM((HALO + CONV_RC, cb), F32)],
        sem=("parallel", "arbitrary"), side=side, into=into)
    return outs if side is None else (outs, side_outs)


def _ssd_f1(dtraw, dtb, alog):
    q = dtraw.shape[0]
    dt = _softplus(dtraw + dtb)
    adt = dt * (-jnp.exp(alog))
    tril = (lax.broadcasted_iota(jnp.int32, (q, q), 0) >= lax.broadcasted_iota(jnp.int32, (q, q), 1)).astype(F32)
    acs = lax.dot_general(tril, adt, (_NN, ((), ())), precision=lax.Precision.HIGHEST, preferred_element_type=F32)
    return dt, acs


def _ssd_group(g, x, bm, cm, dt, acs, drow, hp):
    q = x.shape[0]
    lane = lax.broadcasted_iota(jnp.int32, (1, 128), 1)
    sub = lax.broadcasted_iota(jnp.int32, (128, 1), 0)
    head_of = lax.broadcasted_iota(jnp.int32, (1, GROUP_W), 1) // HEAD_DIM
    is_last = (lax.broadcasted_iota(jnp.int32, (q, 1), 0) == q - 1).astype(F32)
    causal = lax.broadcasted_iota(jnp.int32, (q, q), 0) >= lax.broadcasted_iota(jnp.int32, (q, q), 1)
    acs_end = jnp.sum(acs * is_last, axis=0, keepdims=True)
    acs_t = acs.T
    dt_exp = jnp.zeros((q, GROUP_W), F32)
    acs_exp = jnp.zeros((q, GROUP_W), F32)
    end_exp = jnp.zeros((1, GROUP_W), F32)
    d_exp = jnp.zeros((1, GROUP_W), F32)
    heads = []
    for k in range(HEADS_PER_GROUP):
        h = HEADS_PER_GROUP * g + k
        oh = (lane == h).astype(F32)
        mk = (head_of == k).astype(F32)
        acs_col = jnp.sum(acs * oh, axis=1, keepdims=True)
        acs_row = jnp.sum(acs_t * (sub == h).astype(F32), axis=0, keepdims=True)
        dt_exp = dt_exp + jnp.sum(dt * oh, axis=1, keepdims=True) * mk
        acs_exp = acs_exp + acs_col * mk
        end_exp = end_exp + jnp.sum(acs_end * oh, axis=1, keepdims=True) * mk
        d_exp = d_exp + jnp.sum(drow * oh, axis=1, keepdims=True) * mk
        heads.append((acs_col, acs_row, mk))
    xdt = x * dt_exp
    states = _dot(bm, xdt * jnp.exp(end_exp - acs_exp), _TN)
    y = _dot(cm, hp, _NN) * jnp.exp(acs_exp) + x * d_exp
    scores = _dot(cm, bm, _NT)
    for acs_col, acs_row, mk in heads:
        dec = jnp.exp(jnp.where(causal, acs_col - acs_row, -jnp.inf))
        y = y + _dot(scores * dec, xdt * mk, _NN)
    return y, hp * jnp.exp(end_exp) + states


def _ssd_fwd(act, dtraw, dtb, alog, drow, *, name, side=None):
    L = act.shape[0]
    q = min(SSD_CHUNK, L)
    nc = L // q
    d_ssd = N_GROUPS * GROUP_W

    def body(act_ref, dt_ref, dtb_ref, alog_ref, drow_ref, y_ref, hst_ref, h):
        @pl.when(pl.program_id(0) == 0)
        def _():
            h[...] = jnp.zeros_like(h)

        dt, acs = _ssd_f1(dt_ref[...], dtb_ref[...], alog_ref[...])
        drow = drow_ref[...]
        for g in range(N_GROUPS):
            hp = h[g]
            hst_ref[0, g] = hp
            y, hn = _ssd_group(g, act_ref[:, g * GROUP_W:(g + 1) * GROUP_W],
                               act_ref[:, d_ssd + g * STATE:d_ssd + (g + 1) * STATE],
                               act_ref[:, d_ssd + (N_GROUPS + g) * STATE:d_ssd + (N_GROUPS + g + 1) * STATE],
                               dt, acs, drow, hp)
            y_ref[:, g * GROUP_W:(g + 1) * GROUP_W] = y
            h[g] = hn

    row = pl.BlockSpec((1, 128), lambda i: (0, 0))
    outs, side_outs = _call(
        body, (act, dtraw, dtb, alog, drow), name=name, grid=(nc,),
        in_specs=[pl.BlockSpec((q, act.shape[1]), lambda i: (i, 0)), pl.BlockSpec((q, 128), lambda i: (i, 0)), row, row, row],
        out_specs=[pl.BlockSpec((q, d_ssd), lambda i: (i, 0)),
                   pl.BlockSpec((1, N_GROUPS, STATE, GROUP_W), lambda i: (i, 0, 0, 0))],
        out_shape=[jax.ShapeDtypeStruct((L, d_ssd), F32), jax.ShapeDtypeStruct((nc, N_GROUPS, STATE, GROUP_W), F32)],
        scratch_shapes=[pltpu.VMEM((N_GROUPS, STATE, GROUP_W), F32)], sem=("arbitrary",), side=side)
    return outs if side is None else (outs, side_outs)


def _ssd_bwd(act, dtraw, dy, hst, dtb, alog, drow, *, name, side=None):
    L = act.shape[0]
    q = min(SSD_CHUNK, L)
    nc = L // q
    d_ssd = N_GROUPS * GROUP_W

    def body(act_ref, dt_ref, dy_ref, hst_ref, dtb_ref, alog_ref, drow_ref, dact_ref, ddt_ref, dpar_ref, dh):
        @pl.when(pl.program_id(0) == 0)
        def _():
            dh[...] = jnp.zeros_like(dh)
            dpar_ref[...] = jnp.zeros_like(dpar_ref)

        (dt, acs), f1_vjp = jax.vjp(_ssd_f1, dt_ref[...], dtb_ref[...], alog_ref[...])
        drow = drow_ref[...]
        ddt = jnp.zeros_like(dt)
        dacs = jnp.zeros_like(acs)
        ddrow = jnp.zeros_like(drow)
        for g in range(N_GROUPS):
            xs = slice(g * GROUP_W, (g + 1) * GROUP_W)
            bs = slice(d_ssd + g * STATE, d_ssd + (g + 1) * STATE)
            cs = slice(d_ssd + (N_GROUPS + g) * STATE, d_ssd + (N_GROUPS + g + 1) * STATE)
            _, f2_vjp = jax.vjp(functools.partial(_ssd_group, g), act_ref[:, xs], act_ref[:, bs], act_ref[:, cs],
                                dt, acs, drow, hst_ref[0, g])
            dx, dbm, dcm, ddt_g, dacs_g, ddrow_g, dhp = f2_vjp((dy_ref[:, xs], dh[g]))
            dact_ref[:, xs] = dx
            dact_ref[:, bs] = dbm
            dact_ref[:, cs] = dcm
            dh[g] = dhp
            ddt, dacs, ddrow = ddt + ddt_g, dacs + dacs_g, ddrow + ddrow_g
        ddtraw, ddtb, dalog = f1_vjp((ddt, dacs))
        ddt_ref[...] = ddtraw
        dpar_ref[0:1, :] += ddtb
        dpar_ref[1:2, :] += dalog
        dpar_ref[2:3, :] += ddrow

    row = pl.BlockSpec((1, 128), lambda i: (0, 0))
    rev = lambda i: (nc - 1 - i, 0)
    outs, side_outs = _call(
        body, (act, dtraw, dy, hst, dtb, alog, drow), name=name, grid=(nc,),
        in_specs=[pl.BlockSpec((q, act.shape[1]), rev), pl.BlockSpec((q, 128), rev), pl.BlockSpec((q, d_ssd), rev),
                  pl.BlockSpec((1, N_GROUPS, STATE, GROUP_W), lambda i: (nc - 1 - i, 0, 0, 0)), row, row, row],
        out_specs=[pl.BlockSpec((q, act.shape[1]), rev), pl.BlockSpec((q, 128), rev), pl.BlockSpec((8, 128), lambda i: (0, 0))],
        out_shape=[jax.ShapeDtypeStruct(act.shape, F32), jax.ShapeDtypeStruct((L, 128), F32), jax.ShapeDtypeStruct((8, 128), F32)],
        scratch_shapes=[pltpu.VMEM((N_GROUPS, STATE, GROUP_W), F32)], sem=("arbitrary",), side=side)
    return outs if side is None else (outs, side_outs)


def _gate_head(xl, wa, ba, wx, bx, lam):
    r = _sigmoid(_dot(xl, wa, _NN) + ba)
    i = _sigmoid(_dot(xl, wx, _NN) + bx)
    log_a = -LRU_C * r * _softplus(-lam)
    return jnp.exp(log_a), jnp.sqrt(-_expm1(2.0 * log_a)) * (i * xl)


def _gate_head_bwd(xl, da, du, wa, ba, wx, bx, lam):
    _, vjp = jax.vjp(_gate_head, xl, wa, ba, wx, bx, lam)
    dxl, dwa, dba, dwx, dbx, dlam = vjp((da, du))
    return dxl, dwa, dba, dwx, dbx, dlam


def _scan_tile(a, b, rows, reverse):
    for d in (1, 2, 4):
        if reverse:
            keep = rows < 8 - d
            a_sh, b_sh = pltpu.roll(a, 8 - d, 0), pltpu.roll(b, 8 - d, 0)
        else:
            keep = rows >= d
            a_sh, b_sh = pltpu.roll(a, d, 0), pltpu.roll(b, d, 0)
        b = b + a * jnp.where(keep, b_sh, 0.0)
        a = a * jnp.where(keep, a_sh, 1.0)
    return a, b


def _lru_scan_fwd(a, u, *, name, tr=512, cb=1024):
    L, C = a.shape
    tr, cb = min(tr, L), min(cb, C)

    def body(a_ref, u_ref, h_ref, hp_ref, carry):
        @pl.when(pl.program_id(1) == 0)
        def _():
            carry[...] = jnp.zeros_like(carry)

        rows = lax.broadcasted_iota(jnp.int32, (8, cb), 0)

        def tile(t, hc):
            r0 = pl.multiple_of(t * 8, 8)
            pa, hb = _scan_tile(a_ref[pl.ds(r0, 8), :], u_ref[pl.ds(r0, 8), :], rows, False)
            h = hb + pa * hc
            h_ref[pl.ds(r0, 8), :] = h
            hp_ref[pl.ds(r0, 8), :] = jnp.where(rows >= 1, pltpu.roll(h, 1, 0), hc)
            return h[7:8, :]

        carry[...] = lax.fori_loop(0, tr // 8, tile, carry[...], unroll=4)

    blk = pl.BlockSpec((tr, cb), lambda j, i: (i, j))
    return pl.pallas_call(
        body, name=name, grid=(C // cb, L // tr),
        in_specs=[blk, blk], out_specs=[blk, blk],
        out_shape=[jax.ShapeDtypeStruct((L, C), F32), jax.ShapeDtypeStruct((L, C), F32)],
        scratch_shapes=[pltpu.VMEM((1, cb), F32)],
        compiler_params=_params(("parallel", "arbitrary")),
    )(a, u)


def _lru_scan_bwd(a, hprev, dh, *, name, tr=512, cb=1024):
    L, C = a.shape
    tr, cb = min(tr, L), min(cb, C)
    nb = L // tr

    def body(a_ref, hp_ref, dh_ref, da_ref, du_ref, carry):
        @pl.when(pl.program_id(1) == 0)
        def _():
            carry[...] = jnp.zeros_like(carry)

        rows = lax.broadcasted_iota(jnp.int32, (8, cb), 0)

        def tile(t, gc):
            r0 = pl.multiple_of((tr // 8 - 1 - t) * 8, 8)
            av, dv = a_ref[pl.ds(r0, 8), :], dh_ref[pl.ds(r0, 8), :]
            pa, gb = _scan_tile(av, av * dv, rows, True)
            big = gb + pa * gc
            g = dv + jnp.where(rows < 7, pltpu.roll(big, 7, 0), gc)
            du_ref[pl.ds(r0, 8), :] = g
            da_ref[pl.ds(r0, 8), :] = g * hp_ref[pl.ds(r0, 8), :]
            return big[0:1, :]

        carry[...] = lax.fori_loop(0, tr // 8, tile, carry[...], unroll=4)

    blk = pl.BlockSpec((tr, cb), lambda j, i: (nb - 1 - i, j))
    return pl.pallas_call(
        body, name=name, grid=(C // cb, nb),
        in_specs=[blk, blk, blk], out_specs=[blk, blk],
        out_shape=[jax.ShapeDtypeStruct((L, C), F32), jax.ShapeDtypeStruct((L, C), F32)],
        scratch_shapes=[pltpu.VMEM((1, cb), F32)],
        compiler_params=_params(("parallel", "arbitrary")),
    )(a, hprev, dh)


def _ssd_gate(y, z, n):
    v = y * _silu(z)
    return v * lax.rsqrt(jnp.mean(v * v, axis=-1, keepdims=True) + EPS) * n


def _ssd_gate_bwd(y, z, dy, n):
    _, vjp = jax.vjp(_ssd_gate, y, z, n)
    return vjp(dy.astype(F32))


def _lru_out(hl, gate, n):
    return _rms(hl * _gelu(gate), n)


def _lru_out_bwd(hl, gate, dy, n):
    _, vjp = jax.vjp(_lru_out, hl, gate, n)
    return vjp(dy.astype(F32))


def _mid(x, mix, pm, pmlp):
    x1 = x + _rms(mix, pm)
    return x1, _rms(x1, pmlp)


def _mid_bwd(x, mix, dx1p, dh2, pm, pmlp):
    _, vjp = jax.vjp(_mid, x, mix, pm, pmlp)
    dx, dmix, dpm, dpmlp = vjp((dx1p, dh2))
    return dmix, dx, dpm, dpmlp


def _loss_bwd(hm2, x1, tgt, g):
    def lossf(hm2, x1, g):
        e = x1 + _rms(hm2, g) - tgt
        return 0.5 * jnp.sum(jnp.mean(e * e, axis=-1, keepdims=True), axis=0, keepdims=True)

    val, vjp = jax.vjp(lossf, hm2, x1, g)
    dhm2, dx1, dg = vjp(jnp.ones((1, 1), F32))
    return dhm2, dx1, dg, val * jnp.ones((1, 128), F32)


def _in_bwd(x, dh_a, dh_b, dx1, g):
    _, vjp = jax.vjp(_rms, x, g)
    dx, dg = vjp(dh_a + dh_b)
    return dx + dx1, dg


def _adamw(w, g, m, v):
    m = ADAM_B1 * m + (1.0 - ADAM_B1) * g
    v = ADAM_B2 * v + (1.0 - ADAM_B2) * (g * g)
    m_hat = m / (1.0 - ADAM_B1 ** ADAM_STEP)
    v_hat = v / (1.0 - ADAM_B2 ** ADAM_STEP)
    return -ADAM_LR * (m_hat / (jnp.sqrt(v_hat) + ADAM_EPS) + ADAM_WD * w), m, v


class _LocalPlan:
    def __init__(self, p):
        self.p, self.large = p, {}

    def weight(self, name):
        return self.p[name]

    def side(self, kernel_name):
        return None

    def done(self, kernel_name, side_outs):
        pass

    def grad(self, name, g):
        self.large[name] = g

    def small(self, grads):
        pass


def _local_step(x, tgt, p, plan):
    L, D = x.shape

    def carry(fn, *args, name, **kw):
        side = plan.side(name)
        if side is None:
            return fn(*args, name=name, **kw)
        outs, side_outs = fn(*args, name=name, side=side, **kw)
        plan.done(name, side_outs)
        return outs

    d_ssd, d_xbc, d_lru, d_mix, d_ff = 2048, 4096, 2048, 4096, 8192
    n_main = d_ssd + d_xbc + 2 * d_lru
    c_xbc, c_gate, c_xl = d_ssd, d_ssd + d_xbc, d_ssd + d_xbc + d_lru
    TR, TRW, SUB = 256, 512, 32
    mm = dict(tm=1024, tn=1024, tk=2048)

    (h,) = _rowwise(lambda xt, g: (_rms(xt, g),), [x], [p["pre_mix_norm"]], [(D, BF16)], [], name="f_prenorm", tr=TRW, sub=SUB)
    w_main, w_dt = plan.weight("w_main"), plan.weight("w_dt")
    (proj,) = carry(_matmul, h, w_main, mode="nn", m=L, n=n_main, k=D, out_dtypes=[F32], name="f_inproj", **mm)
    (dtraw,) = _matmul(h, w_dt, mode="nn", m=L, n=128, k=D, out_dtypes=[F32], name="f_dtproj", **mm)
    pre, act = carry(_conv_fwd, proj, c_xbc, d_xbc, p["ssd_conv_w"], p["ssd_conv_b"], silu=True, name="f_ssdconv")
    (xl,) = carry(_conv_fwd, proj, c_xl, d_lru, p["lru_conv_w"], p["lru_conv_b"], silu=False, name="f_lruconv")
    yraw, hst = carry(_ssd_fwd, act, dtraw, p["dtb"], p["alog"], p["drow"], name="f_ssd")
    (ycat,) = _grouped(lambda y, z, n: (_ssd_gate(y, z, n),), [yraw, (proj, 0)], [p["ssd_norm"]], [BF16], [],
                       gw=GROUP_W, name="f_ssdgate", tr=TR, into={0: (None, d_mix, 0)})
    gate_p = [p["lru_w_a"], p["lru_b_a"], p["lru_w_x"], p["lru_b_x"], p["lru_lambda"]]
    a, u = carry(_grouped, _gate_head, [xl], gate_p, [F32, F32], [], gw=LRU_BLOCK, name="f_lrugates", tr=TR)
    hl, hprev = _lru_scan_fwd(a, u, name="f_lruscan")
    (ycat,) = _rowwise(lambda ht, gt, n: (_lru_out(ht, gt, n),), [hl, (proj, c_gate, d_lru)], [p["lru_norm"]],
                       [(d_lru, BF16)], [], name="f_lruout", tr=TRW, sub=SUB, into={0: (ycat, d_mix, d_ssd // d_lru)})
    w_out = plan.weight("w_out")
    (mix,) = carry(_matmul, ycat, w_out, mode="nn", m=L, n=D, k=d_mix, out_dtypes=[F32], name="f_outproj", **mm)
    x1, h2 = _rowwise(_mid, [x, mix], [p["post_mix_norm"], p["pre_mlp_norm"]], [(D, F32), (D, BF16)], [],
                      name="f_mid", tr=TRW, sub=SUB)
    nb_mi = (d_ff // 4) // mm["tn"]
    w_mi = plan.weight("w_mi")
    hm, act2 = carry(_matmul, h2, w_mi, mode="nn", m=L, n=d_ff, k=D, out_dtypes=[BF16, BF16], name="f_mlpin",
                     b_spec=pl.BlockSpec((None, mm["tk"], mm["tn"]), lambda i, j, kk: (j // nb_mi, kk, j % nb_mi)),
                     epilogue=lambda r: (r, jnp.square(jnp.maximum(r, 0.0))), **mm)
    w_mo = plan.weight("w_mo")
    (hm2,) = _matmul(act2, w_mo, mode="nn", m=L, n=D, k=d_ff, out_dtypes=[F32], name="f_mlpout", **mm)

    dhm2, dx1p, d_post_mlp, loss = _rowwise(_loss_bwd, [hm2, x1, tgt], [p["post_mlp_norm"]], [(D, BF16), (D, F32)],
                                            [(1, D), (1, 128)], name="b_loss", tr=TR, sub=SUB)
    (dhm,) = _matmul(dhm2, w_mo, mode="nt", m=L, n=d_ff, k=D, out_dtypes=[BF16], name="b_mlpout_dx", extras=[hm],
                     epilogue=lambda r, hmv: (r * (2.0 * jnp.maximum(hmv.astype(F32), 0.0)),), **mm)
    (dw_mo,) = _matmul(act2, dhm2, mode="tn", m=d_ff, n=D, k=L, out_dtypes=[BF16], name="b_mlpout_dw", **mm)
    plan.grad("w_mlp_out", dw_mo.reshape(4, -1, D))
    kb_mi = (d_ff // 4) // mm["tk"]
    (dh2,) = carry(_matmul, dhm, w_mi, mode="nt", m=L, n=D, k=d_ff, out_dtypes=[F32], name="b_mlpin_dx",
                   b_spec=pl.BlockSpec((None, mm["tn"], mm["tk"]), lambda i, j, kk: (kk // kb_mi, j, kk % kb_mi)), **mm)
    (dw_mi,) = carry(_matmul, h2, dhm, mode="tn", m=D, n=d_ff, k=L, out_dtypes=[BF16], name="b_mlpin_dw",
                     out_specs=[pl.BlockSpec((None, mm["tm"], mm["tn"]), lambda i, j, kk: (j // nb_mi, i, j % nb_mi))],
                     out_shapes=[jax.ShapeDtypeStruct((4, D, d_ff // 4), BF16)], **mm)
    plan.grad("w_mlp_in", dw_mi)
    dmix, dx1, d_post_mix, d_pre_mlp = carry(_rowwise, _mid_bwd, [x, mix, dx1p, dh2], [p["post_mix_norm"], p["pre_mlp_norm"]],
                                             [(D, BF16), (D, F32)], [(1, D), (1, D)], name="b_mid", tr=TR, sub=SUB)
    (dw_out,) = carry(_matmul, ycat, dmix, mode="tn", m=d_mix, n=D, k=L, out_dtypes=[BF16], name="b_outproj_dw", **mm)
    plan.grad("w_out", dw_out.reshape(4, -1, D))
    (dycat,) = carry(_matmul, dmix, w_out, mode="nt", m=L, n=d_mix, k=D, out_dtypes=[BF16], name="b_outproj_dx", **mm)
    dhl, dproj, d_lru_norm = _rowwise(_lru_out_bwd, [hl, (proj, c_gate, d_lru), (dycat, d_ssd, d_lru)], [p["lru_norm"]],
                                      [(d_lru, F32), (d_lru, BF16)], [(1, d_lru)], name="b_lruout", tr=TR, sub=SUB,
                                      into={1: (None, n_main, c_gate // d_lru)})
    da, du = _lru_scan_bwd(a, hprev, dhl, name="b_lruscan")
    dxl, d_wa, d_ba, d_wx, d_bx, d_lam = _grouped(
        _gate_head_bwd, [xl, da, du], gate_p, [F32],
        [(LRU_HEADS, LRU_BLOCK, LRU_BLOCK), (1, d_lru), (LRU_HEADS, LRU_BLOCK, LRU_BLOCK), (1, d_lru), (1, d_lru)],
        gw=LRU_BLOCK, name="b_lrugates", tr=TR)
    CB = 1024
    dproj, dwb_lru = _conv_bwd(dxl, 0, None, proj, c_xl, d_lru, p["lru_conv_w"], silu=False, name="b_lruconv", cb=CB,
                               into={0: (dproj, n_main, c_xl // CB)})
    dyraw, dproj, d_ssd_norm = _grouped(_ssd_gate_bwd, [yraw, (proj, 0), (dycat, 0)], [p["ssd_norm"]], [F32, BF16], [(1, d_ssd)],
                                        gw=GROUP_W, name="b_ssdgate", tr=TR, into={1: (dproj, n_main, 0)})
    dact, ddtraw, dpar = carry(_ssd_bwd, act, dtraw, dyraw, hst, p["dtb"], p["alog"], p["drow"], name="b_ssd")
    dproj, dwb_ssd = carry(_conv_bwd, dact, 0, pre, proj, c_xbc, d_xbc, p["ssd_conv_w"], silu=True, name="b_ssdconv", cb=CB,
                           into={0: (dproj, n_main, c_xbc // CB)})
    early = dict(loss=loss, ssd_conv=dwb_ssd, ssd_par=dpar, ssd_norm=d_ssd_norm, lru_conv=dwb_lru, lru_w_a=d_wa, lru_b_a=d_ba,
                 lru_w_x=d_wx, lru_b_x=d_bx, lru_lambda=d_lam, lru_norm=d_lru_norm, post_mix_norm=d_post_mix,
                 pre_mlp_norm=d_pre_mlp, post_mlp_norm=d_post_mlp)
    plan.small(early)
    (dw_main,) = carry(_matmul, h, dproj, mode="tn", m=D, n=n_main, k=L, out_dtypes=[BF16], name="b_inproj_dw", **mm)
    (dw_dt,) = _matmul(h, ddtraw, mode="tn", m=D, n=128, k=L, out_dtypes=[BF16], name="b_dtproj_dw", **mm)
    plan.grad("w_in", (dw_main, dw_dt))
    (dh_a,) = carry(_matmul, dproj, w_main, mode="nt", m=L, n=D, k=n_main, out_dtypes=[F32], name="b_inproj_dx", **mm)
    (dh_b,) = carry(_matmul, ddtraw, w_dt, mode="nt", m=L, n=D, k=128, out_dtypes=[F32], name="b_dtproj_dx", **mm)
    grad_x, d_pre_mix = carry(_rowwise, _in_bwd, [x, dh_a, dh_b, dx1], [p["pre_mix_norm"]], [(D, F32)], [(1, D)],
                              name="b_prenorm", tr=TRW, sub=SUB)

    return grad_x, dict(early, pre_mix_norm=d_pre_mix)


def _place():
    return lax.axis_index("x"), lax.axis_index("y"), lax.axis_index("c")


def _other_chips(x, y):
    return [(1 - x, y), (x, 1 - y), (1 - x, 1 - y)]


def _allgather8(blk, *, name):
    r, n = blk.shape

    def body(x_ref, out_ref, send_sems, recv_sems, local_sem):
        x, y, c = _place()
        me, sibling = (x, y, c), (x, y, 1 - c)
        chips = _other_chips(x, y)

        def rows(px, py, pc):
            return out_ref.at[pl.ds((4 * px + 2 * py + pc) * r, r), :]

        def copy(k, block, to, src=None):
            return pltpu.make_async_remote_copy(
                src_ref=rows(*block) if src is None else src, dst_ref=rows(*block),
                send_sem=send_sems.at[k], recv_sem=recv_sems.at[k], device_id=to, device_id_type=MESH)

        mine = pltpu.make_async_copy(x_ref, rows(*me), local_sem)
        mine.start()
        first = [copy(0, me, sibling, src=x_ref)]
        first += [copy(1 + k, me, (*chip, c), src=x_ref) for k, chip in enumerate(chips)]
        for cp in first:
            cp.start()
        passed = [copy(4 + k, (*chip, c), sibling) for k, chip in enumerate(chips)]
        for k, chip in enumerate(chips):
            copy(1 + k, (*chip, c), me).wait_recv()
            passed[k].start()
        copy(0, sibling, me).wait_recv()
        for k, chip in enumerate(chips):
            copy(4 + k, (*chip, 1 - c), me).wait_recv()
        for cp in first + passed:
            cp.wait_send()
        mine.wait()

    return pl.pallas_call(
        body, name=name,
        out_shape=jax.ShapeDtypeStruct((8 * r, n), blk.dtype),
        in_specs=[pl.BlockSpec(memory_space=pltpu.VMEM)], out_specs=pl.BlockSpec(memory_space=pltpu.VMEM),
        scratch_shapes=[pltpu.SemaphoreType.DMA((7,)), pltpu.SemaphoreType.DMA((7,)), pltpu.SemaphoreType.DMA],
        compiler_params=pltpu.CompilerParams(vmem_limit_bytes=VMEM_LIMIT),
    )(blk)


def _sum8(g, *, name):
    _, r, n = g.shape
    tr = max(t for t in range(8, min(r, 512) + 1, 8) if r % t == 0)

    def body(g_ref, o_ref):
        s = g_ref[0]
        for k in range(1, 8):
            s = s + g_ref[k]
        o_ref[...] = s

    return pl.pallas_call(
        body, name=name, grid=(r // tr,),
        in_specs=[pl.BlockSpec((8, tr, n), lambda i: (0, i, 0))], out_specs=pl.BlockSpec((tr, n), lambda i: (i, 0)),
        out_shape=jax.ShapeDtypeStruct((r, n), g.dtype), compiler_params=_params(("parallel",)),
    )(g)


def _blocks(fn, ins, outs, *, grid, name, prefetch=None, aliases=None):
    n_in = len(ins)

    def body(*refs):
        if prefetch is not None:
            refs = refs[1:]
        res = fn(*[r[...] for r in refs[:n_in]])
        for o_ref, o in zip(refs[n_in:], res):
            o_ref[...] = o.astype(o_ref.dtype)

    in_specs = [pl.BlockSpec(b, m) for _, b, m in ins]
    out_specs = [pl.BlockSpec(b, m) for _, b, m in outs]
    kw = dict(name=name, out_shape=[s for s, _, _ in outs], input_output_aliases=aliases or {},
              compiler_params=_params(("arbitrary",) * len(grid)))
    arrs = [a for a, _, _ in ins]
    if prefetch is None:
        return pl.pallas_call(body, grid=grid, in_specs=in_specs, out_specs=out_specs, **kw)(*arrs)
    spec = pltpu.PrefetchScalarGridSpec(num_scalar_prefetch=1, grid=grid, in_specs=in_specs, out_specs=out_specs)
    return pl.pallas_call(body, grid_spec=spec, **kw)(prefetch, *arrs)


def _gather_tree(slot, *, name):
    def body(i_ref, g_ref, send_sems, recv_sems):
        x, y, c = _place()
        j, jx, jy, jd = 2 * x + y, 2 * (1 - x) + y, 2 * x + (1 - y), 2 * (1 - x) + (1 - y)
        xn, yn, sibling = (1 - x, y, c), (x, 1 - y, c), (x, y, 1 - c)
        rp = i_ref.shape[1] // 4

        def piece(ref, chip, hc, q):
            return ref.at[chip, pl.ds((2 * hc + q) * rp, rp), :]

        def cp(k, src, dst, to):
            return pltpu.make_async_remote_copy(src_ref=src, dst_ref=dst, send_sem=send_sems.at[k], recv_sem=recv_sems.at[k],
                                                device_id=to, device_id_type=MESH)

        def landed(k, chip, q):
            return cp(k, piece(g_ref, chip, c, q), piece(g_ref, chip, c, q), xn)

        sends = []

        def go(d):
            d.start()
            sends.append(d)

        go(cp(0, piece(i_ref, j, c, 0), piece(g_ref, j, c, 0), xn))
        go(cp(2, piece(i_ref, j, c, 1), piece(g_ref, j, c, 1), yn))
        go(cp(1, piece(i_ref, j, c, 1), piece(g_ref, j, c, 1), xn))
        go(cp(3, piece(i_ref, j, c, 0), piece(g_ref, j, c, 0), yn))
        arrivals = [(0, jx, 0, 4, yn), (2, jy, 1, 5, xn), (1, jx, 1, None, None), (3, jy, 0, None, None),
                    (4, jd, 0, None, None), (5, jd, 1, None, None)]
        for n_arr, (k, chip, q, k_fwd, to) in enumerate(arrivals):
            landed(k, chip, q).wait_recv()
            if k_fwd is not None:
                go(cp(k_fwd, piece(g_ref, chip, c, q), piece(g_ref, chip, c, q), to))
            go(cp(6 + n_arr, piece(g_ref, chip, c, q), piece(g_ref, chip, c, q), sibling))
        for n_arr, (_, chip, q, _, _) in enumerate(arrivals):
            other = piece(g_ref, chip, 1 - c, q)
            cp(6 + n_arr, other, other, sibling).wait_recv()
        for d in sends:
            d.wait_send()

    return pl.pallas_call(
        body, name=name, out_shape=jax.ShapeDtypeStruct(slot.shape, slot.dtype),
        in_specs=[ANY], out_specs=ANY, input_output_aliases={0: 0},
        scratch_shapes=[pltpu.SemaphoreType.DMA((12,)), pltpu.SemaphoreType.DMA((12,))],
    )(slot)


def _pack(arrs):
    parts = []
    for v in arrs:
        f = v.reshape(-1)
        f = jnp.pad(f, (0, (-f.shape[0]) % 1024))
        parts.append(f.reshape(-1, 128))
    return jnp.concatenate(parts, axis=0)


def _unpack(packed, shapes):
    out, r0 = [], 0
    for s in shapes:
        size = 1
        for d in s:
            size *= d
        nr = (size + 1023) // 1024 * 8
        out.append(packed[r0:r0 + nr].reshape(-1)[:size].reshape(s))
        r0 += nr
    return out


SMALL_GRADS = [("loss", (1, 128)), ("pre_mix_norm", (1, 2048)), ("ssd_conv", (8, 4096)), ("ssd_par", (8, 128)),
               ("ssd_norm", (1, 2048)), ("lru_conv", (8, 2048)), ("lru_w_a", (16, 128, 128)), ("lru_b_a", (1, 2048)),
               ("lru_w_x", (16, 128, 128)), ("lru_b_x", (1, 2048)), ("lru_lambda", (1, 2048)), ("lru_norm", (1, 2048)),
               ("post_mix_norm", (1, 2048)), ("pre_mlp_norm", (1, 2048)), ("post_mlp_norm", (1, 2048))]

SMALL_EARLY = [g for g in SMALL_GRADS if g[0] != "pre_mix_norm"]

WEIGHTS = ['pre_mix_norm', 'w_in', 'ssd_conv_w', 'ssd_conv_b', 'ssd_dt_bias', 'ssd_a_log', 'ssd_d', 'ssd_norm', 'lru_conv_w',
           'lru_conv_b', 'lru_w_a', 'lru_b_a', 'lru_w_x', 'lru_b_x', 'lru_lambda', 'lru_norm', 'w_out', 'post_mix_norm',
           'pre_mlp_norm', 'w_mlp_in', 'w_mlp_out', 'post_mlp_norm']
LARGE = ['w_in', 'w_out', 'w_mlp_in', 'w_mlp_out']

D_SSD, D_XBC, DT_W = 2048, 4096, 32
TB = 256


def _w_in_runs(n_shard, n_main):
    c_dt = D_SSD + D_XBC
    runs, p = [], 0
    while p < n_main:
        j, off = divmod(p if p < c_dt else p + DT_W, n_shard)
        ln = min(n_shard - off, (c_dt if p < c_dt else n_main) - p)
        runs.append((p, j, off, ln))
        p += ln
    jd, offd = divmod(c_dt, n_shard)
    assert offd + DT_W <= n_shard
    return runs, (jd, offd)


def _pack_w_in(slots, *, name):
    _, d, n_shard = slots.shape
    n_main = 4 * n_shard - DT_W
    runs, (jd, offd) = _w_in_runs(n_shard, n_main)

    def body(s_ref, main_ref, dt_ref):
        for p, j, off, ln in runs:
            main_ref[:, p:p + ln] = s_ref[j, :, off:off + ln]
        dt_ref[:, 0:DT_W] = s_ref[jd, :, offd:offd + DT_W]
        dt_ref[:, DT_W:] = jnp.zeros((TB, 128 - DT_W), dt_ref.dtype)

    return pl.pallas_call(
        body, name=name, grid=(d // TB,),
        in_specs=[pl.BlockSpec((4, TB, n_shard), lambda i: (0, i, 0))],
        out_specs=[pl.BlockSpec((TB, n_main), lambda i: (i, 0)), pl.BlockSpec((TB, 128), lambda i: (i, 0))],
        out_shape=[jax.ShapeDtypeStruct((d, n_main), slots.dtype), jax.ShapeDtypeStruct((d, 128), slots.dtype)],
        compiler_params=_params(("parallel",)),
    )(slots)


def _unpack_dw_in(dw_main, dw_dt, *, name):
    d, n_main = dw_main.shape
    n_shard = (n_main + DT_W) // 4
    runs, (jd, offd) = _w_in_runs(n_shard, n_main)

    def body(main_ref, dt_ref, o_ref):
        for p, j, off, ln in runs:
            o_ref[j, :, off:off + ln] = main_ref[:, p:p + ln]
        o_ref[jd, :, offd:offd + DT_W] = dt_ref[:, 0:DT_W]

    return pl.pallas_call(
        body, name=name, grid=(d // TB,),
        in_specs=[pl.BlockSpec((TB, n_main), lambda i: (i, 0)), pl.BlockSpec((TB, 128), lambda i: (i, 0))],
        out_specs=pl.BlockSpec((4, TB, n_shard), lambda i: (0, i, 0)),
        out_shape=jax.ShapeDtypeStruct((4, d, n_shard), dw_main.dtype),
        compiler_params=_params(("parallel",)),
    )(dw_main, dw_dt)


def _half(ref, chip_idx, hc, piece=(0, 1)):
    q, nq = piece
    rp = ref.shape[1] // (2 * nq)
    return ref.at[chip_idx, pl.ds((hc * nq + q) * rp, rp), :]


def _job_gather_ici(i_ref, g_ref, send_sems, recv_sems, base, piece):
    x, y, c = _place()
    j = 2 * x + y
    sends, recvs = [], []
    for k, (px, py) in enumerate(_other_chips(x, y)):
        kw = dict(send_sem=send_sems.at[base + k], recv_sem=recv_sems.at[base + k], device_id=(px, py, c), device_id_type=MESH)
        sends.append(pltpu.make_async_remote_copy(src_ref=_half(i_ref, j, c, piece), dst_ref=_half(g_ref, j, c, piece), **kw))
        landed = _half(g_ref, 2 * px + py, c, piece)
        recvs.append(pltpu.make_async_remote_copy(src_ref=landed, dst_ref=landed, **kw))
    return sends, recvs


def _job_gather_sibling(i_ref, g_ref, send_sems, recv_sems, base, piece):
    x, y, c = _place()
    sends, recvs = [], []
    for k, (px, py) in enumerate(_other_chips(x, y)):
        kw = dict(send_sem=send_sems.at[base + k], recv_sem=recv_sems.at[base + k], device_id=(x, y, 1 - c), device_id_type=MESH)
        sends.append(pltpu.make_async_remote_copy(src_ref=_half(i_ref, 2 * px + py, c, piece),
                                                  dst_ref=_half(g_ref, 2 * px + py, c, piece), **kw))
        other = _half(g_ref, 2 * px + py, 1 - c, piece)
        recvs.append(pltpu.make_async_remote_copy(src_ref=other, dst_ref=other, **kw))
    return sends, recvs


def _job_reduce_ici(s_ref, got_refs, send_sems, recv_sems, base, piece):
    x, y, c = _place()
    q, nq, count = piece
    rp = s_ref.shape[1] // nq
    rows = pl.ds(q * rp, count * rp)
    sends = [pltpu.make_async_remote_copy(src_ref=s_ref.at[2 * px + py, rows, :], dst_ref=got_refs[k].at[rows, :],
                                          send_sem=send_sems.at[base + k], recv_sem=recv_sems.at[base + k],
                                          device_id=(px, py, c), device_id_type=MESH)
             for k, (px, py) in enumerate(_other_chips(x, y))]
    return sends, sends


GATHER_PLAN = {
    "f_inproj": [("w_out", _job_gather_ici, (0, 1)), ("w_mlp_in", _job_gather_ici, (0, 2))],
    "f_ssdconv": [("w_mlp_in", _job_gather_ici, (1, 2)), ("w_out", _job_gather_sibling, (0, 1)),
                  ("w_mlp_in", _job_gather_sibling, (0, 2))],
    "f_lruconv": [("w_mlp_in", _job_gather_sibling, (1, 2))],
    "f_ssd": [("w_mlp_out", _job_gather_ici, (0, 2))],
    "f_lrugates": [("w_mlp_out", _job_gather_sibling, (0, 2))],
    "f_outproj": [("w_mlp_out", _job_gather_ici, (1, 2))],
    "f_mlpin": [("w_mlp_out", _job_gather_sibling, (1, 2))],
}
SIBLING_PLAN = {"b_mlpin_dx": "w_mlp_out", "b_mid": "w_mlp_in", "b_outproj_dx": "w_out"}
REDUCE_PLAN = {
    "b_mlpin_dw": [("w_mlp_out", (0, 2, 1))],
    "b_outproj_dw": [("w_mlp_out", (1, 2, 1))],
    "b_ssd": [("w_mlp_in", (0, 1, 1)), ("w_out", (0, 1, 1))],
    "b_inproj_dx": [("w_in", (0, 8, 7))],
    "b_dtproj_dx": [("w_in", (7, 8, 1))],
}
SHARE_PLAN = {"b_ssdconv": ["w_mlp_out", "w_mlp_in", "w_out"]}
SMALL_PLAN = {"b_inproj_dw": 0, "b_inproj_dx": 1}


class _Part:
    def __init__(self, ins, outs, aliases, n_sems, make, done):
        self.ins, self.outs, self.aliases, self.n_sems, self.make, self.done = ins, outs, aliases, n_sems, make, done


def _merge_parts(parts):
    ins, outs, aliases, offs, n = [], [], {}, [], 0
    for p in parts:
        offs.append((len(ins), len(outs), n))
        aliases.update({len(ins) + i: len(outs) + o for i, o in p.aliases.items()})
        ins, outs, n = ins + list(p.ins), outs + list(p.outs), n + p.n_sems

    def make(i_refs, o_refs, send_sems, recv_sems):
        sends, recvs = [], []
        for p, (io, oo, so) in zip(parts, offs):
            s, r = p.make(i_refs[io:io + len(p.ins)], o_refs[oo:oo + len(p.outs)], send_sems, recv_sems, so)
            sends, recvs = sends + s, recvs + r
        return sends, recvs

    return _Side(ins, outs, aliases, n, make)


def _parts_done(parts, outs):
    o = 0
    for p in parts:
        p.done(list(outs[o:o + len(p.outs)]))
        o += len(p.outs)


def _comm_call(parts, *, name):
    side = _merge_parts(parts)
    si, so = len(side.ins), len(side.outs)

    def body(*refs):
        sends, recvs = side.make(refs[:si], refs[si:si + so], refs[-2], refs[-1])
        for d in sends:
            d.start()
        for d in recvs:
            d.wait_recv()
        for d in sends:
            d.wait_send()

    outs = pl.pallas_call(
        body, name=name, out_shape=list(side.outs), in_specs=[ANY] * si, out_specs=[ANY] * so,
        scratch_shapes=[pltpu.SemaphoreType.DMA((side.n_sems,)), pltpu.SemaphoreType.DMA((side.n_sems,))],
        input_output_aliases=side.aliases,
    )(*side.ins)
    _parts_done(parts, outs)


class _DistPlan:
    def __init__(self, slots, w_main, w_dt, where, shapes):
        self.slots, self.w_main, self.w_dt, self.where, self.shapes = slots, w_main, w_dt, where, shapes
        self.sib_pending, self.partial, self.got, self.totals, self.shared = {}, {}, {}, {}, {}
        self.small_buf = None

    def weight(self, name):
        d = self.w_main.shape[0]
        return {"w_main": lambda: self.w_main, "w_dt": lambda: self.w_dt, "w_out": lambda: self.slots["w_out"].reshape(-1, d),
                "w_mi": lambda: self.slots["w_mlp_in"], "w_mo": lambda: self.slots["w_mlp_out"].reshape(-1, d)}[name]()

    def _gather_part(self, jobs):
        names = []
        for n, _, _ in jobs:
            if n not in names:
                names.append(n)
        arrs = [self.slots[n] for n in names]

        def make(i_refs, o_refs, send_sems, recv_sems, base):
            sends, recvs = [], []
            for q, (n, job, piece) in enumerate(jobs):
                s, r = job(i_refs[names.index(n)], o_refs[names.index(n)], send_sems, recv_sems, base + 3 * q, piece)
                sends, recvs = sends + s, recvs + r
            return sends, recvs

        return _Part(arrs, [jax.ShapeDtypeStruct(a.shape, a.dtype) for a in arrs], {i: i for i in range(len(arrs))},
                     3 * len(jobs), make, lambda outs: self.slots.update(zip(names, outs)))

    def _sibling_part(self, name):
        g = self.sib_pending.pop(name)
        _, _, rh, cc = g.shape

        def make(i_refs, o_refs, send_sems, recv_sems, base):
            x, y, c = _place()
            sends = [pltpu.make_async_remote_copy(src_ref=i_refs[0].at[s, 1 - c], dst_ref=o_refs[0].at[s],
                                                  send_sem=send_sems.at[base + s], recv_sem=recv_sems.at[base + s],
                                                  device_id=(x, y, 1 - c), device_id_type=MESH) for s in range(4)]
            return sends, sends

        def done(outs):
            (t,) = outs
            self.partial[name] = _blocks(
                lambda u, v: (u.astype(F32) + v.astype(F32),),
                [(g, (None, None, TB, cc), lambda q, i, s: (q, s[1], i, 0)), (t, (None, TB, cc), lambda q, i, s: (q, i, 0))],
                [(jax.ShapeDtypeStruct(t.shape, BF16), (None, TB, cc), lambda q, i, s: (q, i, 0))],
                grid=(4, rh // TB), name="add_sibling_" + name, prefetch=self.where)[0]

        return _Part([g], [jax.ShapeDtypeStruct((4, rh, cc), g.dtype)], {}, 4, make, done)

    def _reduce_part(self, items):
        parts = [self.partial[n] for n, _ in items]
        ins, aliases = list(parts), {}
        for q, (n, (first, _, _)) in enumerate(items):
            if first > 0:
                for k in range(3):
                    aliases[len(ins)] = 3 * q + k
                    ins.append(self.got[n][k])

        def make(i_refs, o_refs, send_sems, recv_sems, base):
            sends = []
            for q, (_, piece) in enumerate(items):
                sends += _job_reduce_ici(i_refs[q], o_refs[3 * q:3 * q + 3], send_sems, recv_sems, base + 3 * q, piece)[0]
            return sends, sends

        def done(outs):
            for q, (name, (first, n_pieces, count)) in enumerate(items):
                g3 = self.got[name] = outs[3 * q:3 * q + 3]
                if first + count < n_pieces:
                    continue
                s4 = self.partial[name]
                _, rh, cc = s4.shape
                row = ((TB, cc), lambda i, s: (i, 0))
                self.totals[name] = _blocks(
                    lambda o, r0, r1, r2: (((o.astype(F32) + r0.astype(F32)) + r1.astype(F32)) + r2.astype(F32),),
                    [(s4, (None, TB, cc), lambda i, s: (s[0], i, 0)), (g3[0],) + row, (g3[1],) + row, (g3[2],) + row],
                    [(jax.ShapeDtypeStruct((2, rh, cc), F32), (None, TB, cc), lambda i, s: (s[1], i, 0))],
                    grid=(rh // TB,), name="add_chips_" + name, prefetch=self.where)[0]
                if name == "w_in":
                    _comm_call([self._share_part([name])], name="share_" + name)

        outs = [jax.ShapeDtypeStruct(s.shape[1:], s.dtype) for s in parts for _ in range(3)]
        return _Part(ins, outs, aliases, 3 * len(items), make, done)

    def _share_part(self, names):
        ts = [self.totals[n] for n in names]

        def make(i_refs, o_refs, send_sems, recv_sems, base):
            x, y, c = _place()
            sends, recvs = [], []
            for w in range(len(ts)):
                kw = dict(send_sem=send_sems.at[base + w], recv_sem=recv_sems.at[base + w], device_id=(x, y, 1 - c),
                          device_id_type=MESH)
                sends.append(pltpu.make_async_remote_copy(src_ref=i_refs[w].at[c], dst_ref=o_refs[w].at[c], **kw))
                recvs.append(pltpu.make_async_remote_copy(src_ref=i_refs[w].at[c], dst_ref=o_refs[w].at[1 - c], **kw))
            return sends, recvs

        def done(outs):
            self.shared.update({n: o.reshape(self.shapes[n]) for n, o in zip(names, outs)})

        return _Part(ts, [jax.ShapeDtypeStruct(t.shape, t.dtype) for t in ts], {i: i for i in range(len(ts))}, len(ts), make, done)

    def _small_part(self, stage):
        buf = self.small_buf

        def make(i_refs, o_refs, send_sems, recv_sems, base):
            x, y, c = _place()
            src, dst = i_refs[0], o_refs[0]
            sends, recvs = [], []

            def add(k, block, to, frm):
                kw = dict(send_sem=send_sems.at[base + k], recv_sem=recv_sems.at[base + k], device_id=to, device_id_type=MESH)
                sends.append(pltpu.make_async_remote_copy(src_ref=src.at[block], dst_ref=dst.at[block], **kw))
                recvs.append(pltpu.make_async_remote_copy(src_ref=src.at[frm], dst_ref=dst.at[frm], **kw))

            if stage == 0:
                add(0, 4 * x + 2 * y + c, (x, y, 1 - c), 4 * x + 2 * y + 1 - c)
            for k, (px, py) in enumerate(_other_chips(x, y)):
                if stage == 0:
                    add(1 + k, 4 * x + 2 * y + c, (px, py, c), 4 * px + 2 * py + c)
                else:
                    add(k, 4 * px + 2 * py + c, (x, y, 1 - c), 4 * px + 2 * py + 1 - c)
            return sends, recvs

        def done(outs):
            (self.small_buf,) = outs

        return _Part([buf], [jax.ShapeDtypeStruct(buf.shape, buf.dtype)], {0: 0}, 4 if stage == 0 else 3, make, done)

    def side(self, kernel_name):
        parts = []
        if kernel_name in GATHER_PLAN:
            parts.append(self._gather_part(GATHER_PLAN[kernel_name]))
        if kernel_name in SIBLING_PLAN:
            parts.append(self._sibling_part(SIBLING_PLAN[kernel_name]))
        if kernel_name in REDUCE_PLAN:
            parts.append(self._reduce_part(REDUCE_PLAN[kernel_name]))
        if kernel_name in SHARE_PLAN:
            parts.append(self._share_part(SHARE_PLAN[kernel_name]))
        if kernel_name in SMALL_PLAN:
            parts.append(self._small_part(SMALL_PLAN[kernel_name]))
        self._carried = parts
        return _merge_parts(parts) if parts else None

    def done(self, kernel_name, side_outs):
        _parts_done(self._carried, side_outs)

    def grad(self, name, g):
        if name == "w_in":
            g = _unpack_dw_in(*g, name="unpack_dw_in")
        self.sib_pending[name] = g.reshape(4, 2, g.shape[1] // 2, g.shape[2])
        if name == "w_in":
            _comm_call([self._sibling_part(name)], name="reduce_sibling_" + name)

    def small(self, grads):
        packed = _pack([grads[n] for n, _ in SMALL_EARLY])
        n_rows = packed.shape[0]
        self.small_buf = _blocks(lambda t: (t,), [(packed, (n_rows, 128), lambda i, s: (0, 0))],
                                 [(jax.ShapeDtypeStruct((8, n_rows, 128), F32), (None, n_rows, 128), lambda i, s: (s[2], 0, 0))],
                                 grid=(1,), name="place_small", prefetch=self.where)[0]

    def finish(self):
        assert len(self.shared) == len(self.shapes)
        return self.shared


def kernel(x, pre_mix_norm, w_in, ssd_conv_w, ssd_conv_b, ssd_dt_bias, ssd_a_log, ssd_d, ssd_norm, lru_conv_w, lru_conv_b, lru_w_a, lru_b_a, lru_w_x, lru_b_x, lru_lambda, lru_norm, w_out, post_mix_norm, pre_mlp_norm, w_mlp_in, w_mlp_out, post_mlp_norm, loss_target, m_pre_mix_norm, m_w_in, m_ssd_conv_w, m_ssd_conv_b, m_ssd_dt_bias, m_ssd_a_log, m_ssd_d, m_ssd_norm, m_lru_conv_w, m_lru_conv_b, m_lru_w_a, m_lru_b_a, m_lru_w_x, m_lru_b_x, m_lru_lambda, m_lru_norm, m_w_out, m_post_mix_norm, m_pre_mlp_norm, m_w_mlp_in, m_w_mlp_out, m_post_mlp_norm, v_pre_mix_norm, v_w_in, v_ssd_conv_w, v_ssd_conv_b, v_ssd_dt_bias, v_ssd_a_log, v_ssd_d, v_ssd_norm, v_lru_conv_w, v_lru_conv_b, v_lru_w_a, v_lru_b_a, v_lru_w_x, v_lru_b_x, v_lru_lambda, v_lru_norm, v_w_out, v_post_mix_norm, v_pre_mlp_norm, v_w_mlp_in, v_w_mlp_out, v_post_mlp_norm):
    a = dict(locals())
    j = 2 * lax.axis_index("x") + lax.axis_index("y")
    D = x.shape[-1]
    c_dt = D_SSD + D_XBC

    core = lax.axis_index("c")
    where = jnp.stack([j, core, 2 * j + core]).astype(jnp.int32)
    slots = {}
    for name in LARGE:
        w = a[name][0]
        r, cc = w.shape
        slots[name] = _blocks(lambda t: (t,), [(w, (TB, cc), lambda i, s: (i, 0))],
                              [(jax.ShapeDtypeStruct((4, r, cc), BF16), (None, TB, cc), lambda i, s: (s[0], i, 0))],
                              grid=(r // TB,), name="cast_" + name, prefetch=where)[0]
    g_in = _gather_tree(slots.pop("w_in"), name="gather_w_in")
    w_main, w_dt = _pack_w_in(g_in, name="pack_w_in")
    taps = jnp.concatenate([ssd_conv_w[0].reshape(-1, 128), lru_conv_w[0].reshape(-1, 128)], axis=0)
    taps = _allgather8(taps, name="gather_taps").reshape(8, taps.shape[0], 128)[0::2]
    n_ssd = ssd_conv_w.shape[1] * ssd_conv_w.shape[2] // 128
    ssd_taps = taps[:, :n_ssd].reshape(4, CONV_W, -1).transpose(1, 0, 2).reshape(CONV_W, -1)
    lru_taps = taps[:, n_ssd:].reshape(4, CONV_W, -1).transpose(1, 0, 2).reshape(CONV_W, -1)

    def row128(v):
        return jnp.pad(v, ((0, 0), (0, 128 - v.shape[1])))

    p = dict(pre_mix_norm=pre_mix_norm, ssd_conv_w=ssd_taps, ssd_conv_b=ssd_conv_b,
             dtb=row128(ssd_dt_bias), alog=row128(ssd_a_log), drow=row128(ssd_d), ssd_norm=ssd_norm,
             lru_conv_w=lru_taps, lru_conv_b=lru_conv_b, lru_w_a=lru_w_a[0], lru_b_a=lru_b_a.reshape(1, -1),
             lru_w_x=lru_w_x[0], lru_b_x=lru_b_x.reshape(1, -1), lru_lambda=lru_lambda, lru_norm=lru_norm,
             post_mix_norm=post_mix_norm, pre_mlp_norm=pre_mlp_norm, post_mlp_norm=post_mlp_norm)
    plan = _DistPlan(slots, w_main, w_dt, where, {n: a[n].shape for n in LARGE})
    grad_x, small = _local_step(x[0], loss_target[0], p, plan)

    large_grads = plan.finish()

    delta, new_m, new_v = {}, {}, {}
    for name in LARGE:
        w = a[name][0]
        cc = w.shape[1]
        outs = _rowwise(_adamw, [w, large_grads[name][0], a["m_" + name][0], a["v_" + name][0]], [], [(cc, F32)] * 3, [],
                        name="adamw_" + name, tr=128, sub=8)
        delta[name], new_m[name], new_v[name] = [o[None] for o in outs]

    total = _sum8(plan.small_buf, name="sum_small")
    tot = dict(zip([n for n, _ in SMALL_EARLY], _unpack(total, [s for _, s in SMALL_EARLY])))
    late = small["pre_mix_norm"].reshape(-1, 128)
    late = _allgather8(late, name="gather_late").reshape(8, late.shape[0], 128)
    tot["pre_mix_norm"] = _sum8(late, name="sum_late").reshape(small["pre_mix_norm"].shape)
    loss = tot["loss"][0, 0]
    n_sc, n_lc = ssd_conv_w.shape[2], lru_conv_w.shape[2]
    grads = dict(
        pre_mix_norm=tot["pre_mix_norm"],
        ssd_conv_w=lax.dynamic_slice(tot["ssd_conv"][:CONV_W], (0, j * n_sc), (CONV_W, n_sc))[None],
        ssd_conv_b=tot["ssd_conv"][CONV_W:CONV_W + 1],
        ssd_dt_bias=tot["ssd_par"][0:1, :DT_W], ssd_a_log=tot["ssd_par"][1:2, :DT_W], ssd_d=tot["ssd_par"][2:3, :DT_W],
        ssd_norm=tot["ssd_norm"],
        lru_conv_w=lax.dynamic_slice(tot["lru_conv"][:CONV_W], (0, j * n_lc), (CONV_W, n_lc))[None],
        lru_conv_b=tot["lru_conv"][CONV_W:CONV_W + 1],
        lru_w_a=tot["lru_w_a"][None], lru_b_a=tot["lru_b_a"].reshape(lru_b_a.shape),
        lru_w_x=tot["lru_w_x"][None], lru_b_x=tot["lru_b_x"].reshape(lru_b_x.shape),
        lru_lambda=tot["lru_lambda"], lru_norm=tot["lru_norm"], post_mix_norm=tot["post_mix_norm"],
        pre_mlp_norm=tot["pre_mlp_norm"], post_mlp_norm=tot["post_mlp_norm"])

    grads.update(large_grads)

    small_w = [n for n in WEIGHTS if n not in LARGE]
    n_sw = len(small_w)

    def flat(v):
        return v.reshape(-1, v.shape[-1])

    def adamw_all(*refs):
        ins, outs = refs[:4 * n_sw], refs[4 * n_sw:]
        for q in range(n_sw):
            res = _adamw(*[ins[4 * q + t][...] for t in range(4)])
            for t in range(3):
                outs[3 * q + t][...] = res[t]

    operands = [flat(d[n]) for n in small_w for d in (a, grads, {k: a["m_" + k] for k in small_w}, {k: a["v_" + k] for k in small_w})]
    outs = pl.pallas_call(
        adamw_all, name="adamw_small",
        out_shape=[jax.ShapeDtypeStruct(flat(a[n]).shape, F32) for n in small_w for _ in range(3)],
        in_specs=[pl.BlockSpec(memory_space=pltpu.VMEM)] * (4 * n_sw), out_specs=[pl.BlockSpec(memory_space=pltpu.VMEM)] * (3 * n_sw),
        compiler_params=pltpu.CompilerParams(vmem_limit_bytes=VMEM_LIMIT),
    )(*operands)
    for q, n in enumerate(small_w):
        delta[n], new_m[n], new_v[n] = [o.reshape(a[n].shape) for o in outs[3 * q:3 * q + 3]]

    return (loss, grad_x[None], *[grads[n] for n in WEIGHTS], *[delta[n] for n in WEIGHTS],
            *[new_m[n] for n in WEIGHTS], *[new_v[n] for n in WEIGHTS])
```

```python
import functools

import jax
import jax.numpy as jnp
from jax import lax
from jax.experimental import pallas as pl
from jax.experimental.pallas import tpu as pltpu

F32 = jnp.float32
BF16 = jnp.bfloat16
MESH = pl.DeviceIdType.MESH

EPS = 1e-6
LRU_C = 8.0
ADAM_LR = 0.001
ADAM_B1 = 0.9
ADAM_B2 = 0.999
ADAM_EPS = 1e-08
ADAM_WD = 0.01
ADAM_STEP = 10

N_GROUPS = 8
HEADS_PER_GROUP = 4
HEAD_DIM = 64
GROUP_W = HEADS_PER_GROUP * HEAD_DIM
STATE = 128
LRU_HEADS = 16
LRU_BLOCK = 128
CONV_W = 4
SSD_CHUNK = 256
HALO = 8

VMEM_LIMIT = 56 * 1024 * 1024


def _params(sem=None):
    return pltpu.CompilerParams(dimension_semantics=sem, vmem_limit_bytes=VMEM_LIMIT)


@jax.custom_jvp
def _log1p(x):
    u = 1.0 + x
    d = u - 1.0
    return jnp.where(d == 0.0, x, jnp.log(u) * (x / jnp.where(d == 0.0, 1.0, d)))


@_log1p.defjvp
def _log1p_jvp(primals, tangents):
    (x,), (t,) = primals, tangents
    return _log1p(x), t / (1.0 + x)


@jax.custom_jvp
def _expm1(x):
    u = jnp.exp(x)
    lu = jnp.log(u)
    safe = jnp.where(lu == 0.0, 1.0, lu)
    y = (u - 1.0) * (x / safe)
    y = jnp.where(lu == 0.0, x, y)
    return jnp.where(u == 0.0, -1.0, y)


@_expm1.defjvp
def _expm1_jvp(primals, tangents):
    (x,), (t,) = primals, tangents
    return _expm1(x), t * jnp.exp(x)


def _softplus(x):
    return jnp.maximum(x, 0.0) + _log1p(jnp.exp(-jnp.abs(x)))


def _sigmoid(x):
    return 1.0 / (1.0 + jnp.exp(-x))


def _silu(x):
    return x * _sigmoid(x)


def _gelu(x):
    c = 0.7978845608028654
    return 0.5 * x * (1.0 + jnp.tanh(c * (x + 0.044715 * (x * x * x))))


def _rms(x, g):
    return x * lax.rsqrt(jnp.mean(x * x, axis=-1, keepdims=True) + EPS) * g


def _dot(a, b, dims):
    return lax.dot_general(a.astype(BF16), b.astype(BF16), (dims, ((), ())), preferred_element_type=F32)


_NN = ((1,), (0,))
_NT = ((1,), (1,))
_TN = ((0,), (0,))


ANY = pl.BlockSpec(memory_space=pl.ANY)


class _Side:
    def __init__(self, ins, outs, aliases, n_sems, make):
        self.ins, self.outs, self.aliases, self.n_sems, self.make = ins, outs, aliases, n_sems, make


def _call(body, args, *, name, grid, in_specs, out_specs, out_shape, scratch_shapes=(), sem=None, side=None, into=None):
    in_specs, out_specs, out_shape, scratch_shapes = list(in_specs), list(out_specs), list(out_shape), list(scratch_shapes)
    held = []
    for oi, (buf, n_cols, off) in (into or {}).items():
        spec = out_specs[oi]
        out_shape[oi] = jax.ShapeDtypeStruct((out_shape[oi].shape[0], n_cols), out_shape[oi].dtype)
        out_specs[oi] = pl.BlockSpec(spec.block_shape, lambda *idx, _m=spec.index_map, _o=off: (_m(*idx)[0], _m(*idx)[1] + _o))
        if buf is not None:
            held.append((oi, buf))
    if side is None and not held:
        outs = pl.pallas_call(body, name=name, grid=grid, in_specs=in_specs, out_specs=out_specs, out_shape=out_shape,
                              scratch_shapes=scratch_shapes, compiler_params=_params(sem))(*args)
        return list(outs), []
    side_ins, side_outs = (list(side.ins), list(side.outs)) if side is not None else ([], [])
    n_in, n_out, n_scr, nh, si, so = len(in_specs), len(out_specs), len(scratch_shapes), len(held), len(side_ins), len(side_outs)

    def full(*refs):
        s_in = refs[n_in + nh:n_in + nh + si]
        o0 = n_in + nh + si
        s_out = refs[o0 + n_out:o0 + n_out + so]
        scr = refs[o0 + n_out + so:o0 + n_out + so + n_scr]
        if side is None:
            body(*refs[:n_in], *refs[o0:o0 + n_out], *scr)
            return
        send_sems, recv_sems = refs[-2], refs[-1]
        ids = [pl.program_id(d) for d in range(len(grid))]
        first = functools.reduce(jnp.logical_and, [i == 0 for i in ids])
        last = functools.reduce(jnp.logical_and, [i == g - 1 for i, g in zip(ids, grid)])

        @pl.when(first)
        def _():
            for d in side.make(s_in, s_out, send_sems, recv_sems)[0]:
                d.start()

        body(*refs[:n_in], *refs[o0:o0 + n_out], *scr)

        @pl.when(last)
        def _():
            sends, recvs = side.make(s_in, s_out, send_sems, recv_sems)
            for d in recvs:
                d.wait_recv()
            for d in sends:
                d.wait_send()

    aliases = {n_in + h: oi for h, (oi, _) in enumerate(held)}
    if side is not None:
        aliases.update({n_in + nh + i: n_out + o for i, o in side.aliases.items()})
        scratch_shapes = scratch_shapes + [pltpu.SemaphoreType.DMA((side.n_sems,)), pltpu.SemaphoreType.DMA((side.n_sems,))]
    outs = pl.pallas_call(
        full, name=name, grid=grid, in_specs=in_specs + [ANY] * (nh + si), out_specs=out_specs + [ANY] * so,
        out_shape=out_shape + side_outs, scratch_shapes=scratch_shapes, input_output_aliases=aliases,
        compiler_params=_params(("arbitrary",) * len(grid) if side is not None else sem),
    )(*args, *[b for _, b in held], *side_ins)
    return list(outs[:n_out]), list(outs[n_out:])


def _matmul(a, b, *, mode, m, n, k, tm, tn, tk, out_dtypes, name, a_spec=None, b_spec=None,
            out_specs=None, out_shapes=None, extras=(), epilogue=None, side=None):
    tm, tn, tk = min(tm, m), min(tn, n), min(tk, k)
    assert m % tm == 0 and n % tn == 0 and k % tk == 0, (name, m, n, k, tm, tn, tk)
    nk = k // tk
    dims = {"nn": _NN, "nt": _NT, "tn": _TN}[mode]
    if a_spec is None:
        a_spec = pl.BlockSpec((tk, tm), lambda i, j, kk: (kk, i)) if mode == "tn" else pl.BlockSpec((tm, tk), lambda i, j, kk: (i, kk))
    if b_spec is None:
        b_spec = pl.BlockSpec((tn, tk), lambda i, j, kk: (j, kk)) if mode == "nt" else pl.BlockSpec((tk, tn), lambda i, j, kk: (kk, j))
    tile = pl.BlockSpec((tm, tn), lambda i, j, kk: (i, j))
    if out_specs is None:
        out_specs = [tile for _ in out_dtypes]
    if out_shapes is None:
        out_shapes = [jax.ShapeDtypeStruct((m, n), d) for d in out_dtypes]
    n_ex, n_out = len(extras), len(out_dtypes)

    def body(*refs):
        a_ref, b_ref = refs[0], refs[1]
        ex_refs = refs[2:2 + n_ex]
        o_refs = refs[2 + n_ex:2 + n_ex + n_out]
        def finish(r):
            outs = epilogue(r, *[e[...] for e in ex_refs]) if epilogue is not None else (r,)
            for o_ref, o in zip(o_refs, outs):
                o_ref[...] = o.astype(o_ref.dtype)

        if nk == 1:
            finish(_dot(a_ref[...], b_ref[...], dims))
            return
        acc = refs[-1]
        kk = pl.program_id(2)

        @pl.when(kk == 0)
        def _():
            acc[...] = _dot(a_ref[...], b_ref[...], dims)

        @pl.when(kk > 0)
        def _():
            acc[...] += _dot(a_ref[...], b_ref[...], dims)

        @pl.when(kk == nk - 1)
        def _():
            finish(acc[...])

    outs, side_outs = _call(
        body, (a, b, *extras), name=name, grid=(m // tm, n // tn, nk),
        in_specs=[a_spec, b_spec] + [tile for _ in extras], out_specs=out_specs, out_shape=out_shapes,
        scratch_shapes=[] if nk == 1 else [pltpu.VMEM((tm, tn), F32)], sem=("parallel", "parallel", "arbitrary"), side=side)
    return outs if side is None else (outs, side_outs)


def _rowwise(fn, rows, bcast, out_rows, out_acc, *, name, tr, sub, side=None, into=None):
    rows = [r if isinstance(r, tuple) else (r, 0, r.shape[1]) for r in rows]
    row_specs = []
    for arr, c0, w in rows:
        assert c0 % w == 0, (name, c0, w)
        row_specs.append((w, c0 // w))
    rows = [r[0] for r in rows]
    L = rows[0].shape[0]
    tr = min(tr, L)
    sub = min(sub, tr)
    assert L % tr == 0 and tr % sub == 0, (name, L, tr, sub)
    n_r, n_b, n_or, n_oa = len(rows), len(bcast), len(out_rows), len(out_acc)

    def body(*refs):
        r_refs = refs[:n_r]
        b_refs = refs[n_r:n_r + n_b]
        or_refs = refs[n_r + n_b:n_r + n_b + n_or]
        oa_refs = refs[n_r + n_b + n_or:]
        i = pl.program_id(0)

        @pl.when(i == 0)
        def _():
            for o in oa_refs:
                o[...] = jnp.zeros_like(o)

        bvals = [b[...] for b in b_refs]

        def step(s, carry):
            r0 = pl.multiple_of(s * sub, sub)
            tiles = [r[pl.ds(r0, sub), :] for r in r_refs]
            outs = fn(*tiles, *bvals)
            for o_ref, o in zip(or_refs, outs[:n_or]):
                o_ref[pl.ds(r0, sub), :] = o.astype(o_ref.dtype)
            for o_ref, o in zip(oa_refs, outs[n_or:]):
                o_ref[...] += o
            return carry

        if tr == sub:
            step(0, 0)
        else:
            lax.fori_loop(0, tr // sub, step, 0, unroll=8)

    def whole(shape):
        nd = len(shape)
        return pl.BlockSpec(shape, lambda i, _n=nd: (0,) * _n)

    outs, side_outs = _call(
        body, (*rows, *bcast), name=name, grid=(L // tr,),
        in_specs=[pl.BlockSpec((tr, w), lambda i, _c=cb: (i, _c)) for w, cb in row_specs] + [whole(b.shape) for b in bcast],
        out_specs=[pl.BlockSpec((tr, c), lambda i: (i, 0)) for c, _ in out_rows] + [whole(s) for s in out_acc],
        out_shape=[jax.ShapeDtypeStruct((L, c), d) for c, d in out_rows] + [jax.ShapeDtypeStruct(s, F32) for s in out_acc],
        sem=("arbitrary",), side=side, into=into)
    return outs if side is None else (outs, side_outs)


def _colsum(x):
    return jnp.sum(x, axis=0, keepdims=True)


def _grouped(fn, rows, params, out_rows, out_acc, *, gw, name, tr, side=None, into=None):
    rows = [r if isinstance(r, tuple) else (r, 0) for r in rows]
    L = rows[0][0].shape[0]
    tr = min(tr, L)
    assert L % tr == 0
    G = None
    for p in params:
        G = p.shape[0] if p.ndim == 3 else p.shape[1] // gw
    cw = G * gw
    n_r, n_p, n_or, n_oa = len(rows), len(params), len(out_rows), len(out_acc)

    def pick(ref, g):
        return ref[g] if len(ref.shape) == 3 else ref[:, g * gw:(g + 1) * gw]

    def body(*refs):
        r_refs = refs[:n_r]
        p_refs = refs[n_r:n_r + n_p]
        or_refs = refs[n_r + n_p:n_r + n_p + n_or]
        oa_refs = refs[n_r + n_p + n_or:]

        @pl.when(pl.program_id(0) == 0)
        def _():
            for o in oa_refs:
                o[...] = jnp.zeros_like(o)

        for g in range(G):
            outs = fn(*[pick(r, g) for r in r_refs], *[pick(p, g) for p in p_refs])
            for o_ref, o in zip(or_refs, outs[:n_or]):
                o_ref[:, g * gw:(g + 1) * gw] = o.astype(o_ref.dtype)
            for o_ref, o in zip(oa_refs, outs[n_or:]):
                if len(o_ref.shape) == 3:
                    o_ref[g] += o
                else:
                    o_ref[:, g * gw:(g + 1) * gw] += o

    def whole(shape):
        nd = len(shape)
        return pl.BlockSpec(shape, lambda i, _n=nd: (0,) * _n)

    for _, c0 in rows:
        assert c0 % cw == 0
    outs, side_outs = _call(
        body, (*[r[0] for r in rows], *params), name=name, grid=(L // tr,),
        in_specs=[pl.BlockSpec((tr, cw), lambda i, _c=c0 // cw: (i, _c)) for _, c0 in rows] + [whole(p.shape) for p in params],
        out_specs=[pl.BlockSpec((tr, cw), lambda i: (i, 0)) for _ in out_rows] + [whole(s) for s in out_acc],
        out_shape=[jax.ShapeDtypeStruct((L, cw), d) for d in out_rows] + [jax.ShapeDtypeStruct(s, F32) for s in out_acc],
        sem=("arbitrary",), side=side, into=into)
    return outs if side is None else (outs, side_outs)


def _dsilu(p):
    s = _sigmoid(p)
    return s + p * s * (1.0 - s)


CONV_RC, CONV_CC = 32, 512


def _past_window(x_ref, head, r0, k, cs):
    if r0 == 0:
        return head[HALO - 3 + k:HALO - 3 + k + CONV_RC, cs]
    return x_ref[r0 - 3 + k:r0 - 3 + k + CONV_RC, cs]


def _conv_fwd(x, c0, C, w, b, *, silu, name, tr=512, cb=1024, side=None):
    L = x.shape[0]
    tr = min(tr, L)
    nb, hb = L // tr, tr // HALO
    assert L % tr == 0 and C % cb == 0 and c0 % cb == 0 and tr % CONV_RC == 0 and cb % CONV_CC == 0
    n_out = 2 if silu else 1

    def body(x_ref, h_ref, w_ref, b_ref, *rest):
        o_refs, head = rest[:n_out], rest[n_out]
        head[0:HALO, :] = h_ref[...]

        @pl.when(pl.program_id(0) == 0)
        def _():
            head[0:HALO, :] = jnp.zeros((HALO, cb), F32)

        head[HALO:, :] = x_ref[0:CONV_RC, :]
        for cc in range(cb // CONV_CC):
            cs = slice(cc * CONV_CC, (cc + 1) * CONV_CC)
            wv, bv = w_ref[:, cs], b_ref[:, cs]
            for r0 in range(0, tr, CONV_RC):
                y = bv
                for k in range(CONV_W):
                    y = y + wv[k:k + 1, :] * _past_window(x_ref, head, r0, k, cs)
                o_refs[0][r0:r0 + CONV_RC, cs] = y
                if silu:
                    o_refs[1][r0:r0 + CONV_RC, cs] = _silu(y)

    outs, side_outs = _call(
        body, (x, x, w, b), name=name, grid=(nb, C // cb),
        in_specs=[
            pl.BlockSpec((tr, cb), lambda i, j: (i, c0 // cb + j)),
            pl.BlockSpec((HALO, cb), lambda i, j: (jnp.maximum(i * hb - 1, 0), c0 // cb + j)),
            pl.BlockSpec((CONV_W, cb), lambda i, j: (0, j)),
            pl.BlockSpec((1, cb), lambda i, j: (0, j)),
        ],
        out_specs=[pl.BlockSpec((tr, cb), lambda i, j: (i, j)) for _ in range(n_out)],
        out_shape=[jax.ShapeDtypeStruct((L, C), F32) for _ in range(n_out)],
        scratch_shapes=[pltpu.VMEM((HALO + CONV_RC, cb), F32)], sem=("parallel", "parallel"), side=side)
    return outs if side is None else (outs, side_outs)


def _fold8(v):
    return (v[0:8] + v[8:16]) + (v[16:24] + v[24:32])


def _conv_bwd(dact, dc0, pre, x, xc0, C, w, *, silu, name, tr=512, cb=1024, side=None, into=None):
    L = x.shape[0]
    tr = min(tr, L)
    nb, hb = L // tr, tr // HALO
    last_h = L // HALO - 1
    assert L % tr == 0 and C % cb == 0 and tr % CONV_RC == 0 and cb % CONV_CC == 0

    def body(*refs):
        if silu:
            d_ref, dh_ref, p_ref, ph_ref, x_ref, xh_ref, w_ref, dx_ref, dwb_ref, dp, head = refs
        else:
            d_ref, dh_ref, x_ref, xh_ref, w_ref, dx_ref, dwb_ref, dp, head = refs
        i = pl.program_id(1)
        dp[tr:, :] = dh_ref[...] * _dsilu(ph_ref[...]) if silu else dh_ref[...]

        @pl.when(i == nb - 1)
        def _():
            dp[tr:, :] = jnp.zeros((HALO, cb), F32)

        head[0:HALO, :] = xh_ref[...]

        @pl.when(i == 0)
        def _():
            head[0:HALO, :] = jnp.zeros((HALO, cb), F32)
            dwb_ref[...] = jnp.zeros_like(dwb_ref)

        head[HALO:, :] = x_ref[0:CONV_RC, :]
        for cc in range(cb // CONV_CC):
            cs = slice(cc * CONV_CC, (cc + 1) * CONV_CC)
            for r0 in range(0, tr, CONV_RC):
                rs = slice(r0, r0 + CONV_RC)
                dp[rs, cs] = d_ref[rs, cs] * _dsilu(p_ref[rs, cs]) if silu else d_ref[rs, cs]
        for cc in range(cb // CONV_CC):
            cs = slice(cc * CONV_CC, (cc + 1) * CONV_CC)
            wv = w_ref[:, cs]
            acc = [jnp.zeros((8, CONV_CC), F32) for _ in range(CONV_W + 1)]
            for r0 in range(0, tr, CONV_RC):
                dm = dp[r0:r0 + CONV_RC, cs]
                dx = jnp.zeros((CONV_RC, CONV_CC), F32)
                for k in range(CONV_W):
                    dx = dx + wv[k:k + 1, :] * dp[r0 + 3 - k:r0 + 3 - k + CONV_RC, cs]
                    acc[k] = acc[k] + _fold8(dm * _past_window(x_ref, head, r0, k, cs))
                acc[CONV_W] = acc[CONV_W] + _fold8(dm)
                dx_ref[r0:r0 + CONV_RC, cs] = dx.astype(dx_ref.dtype)
            for k in range(CONV_W + 1):
                dwb_ref[k:k + 1, cs] += _colsum(acc[k])

    def main(c):
        return pl.BlockSpec((tr, cb), lambda j, i: (i, c // cb + j))

    def nxt(c):
        return pl.BlockSpec((HALO, cb), lambda j, i: (jnp.minimum((i + 1) * hb, last_h), c // cb + j))

    in_specs = [main(dc0), nxt(dc0)]
    args = [dact, dact]
    if silu:
        in_specs += [main(0), nxt(0)]
        args += [pre, pre]
    in_specs += [main(xc0), pl.BlockSpec((HALO, cb), lambda j, i: (jnp.maximum(i * hb - 1, 0), xc0 // cb + j)),
                 pl.BlockSpec((CONV_W, cb), lambda j, i: (0, j))]
    args += [x, x, w]
    outs, side_outs = _call(
        body, args, name=name, grid=(C // cb, nb), in_specs=in_specs,
        out_specs=[pl.BlockSpec((tr, cb), lambda j, i: (i, j)), pl.BlockSpec((8, cb), lambda j, i: (0, j))],
        out_shape=[jax.ShapeDtypeStruct((L, C), BF16), jax.ShapeDtypeStruct((8, C), F32)],
        scratch_shapes=[pltpu.VMEM((tr + HALO, cb), F32), pltpu.VMEM((HALO + CONV_RC, cb), F32)],
        sem=("parallel", "arbitrary"), side=side, into=into)
    return outs if side is None else (outs, side_outs)


def _ssd_f1(dtraw, dtb, alog):
    q = dtraw.shape[0]
    dt = _softplus(dtraw + dtb)
    adt = dt * (-jnp.exp(alog))
    tril = (lax.broadcasted_iota(jnp.int32, (q, q), 0) >= lax.broadcasted_iota(jnp.int32, (q, q), 1)).astype(F32)
    acs = lax.dot_general(tril, adt, (_NN, ((), ())), precision=lax.Precision.HIGHEST, preferred_element_type=F32)
    return dt, acs


def _ssd_group(g, x, bm, cm, dt, acs, drow, hp):
    q = x.shape[0]
    lane = lax.broadcasted_iota(jnp.int32, (1, 128), 1)
    sub = lax.broadcasted_iota(jnp.int32, (128, 1), 0)
    head_of = lax.broadcasted_iota(jnp.int32, (1, GROUP_W), 1) // HEAD_DIM
    is_last = (lax.broadcasted_iota(jnp.int32, (q, 1), 0) == q - 1).astype(F32)
    causal = lax.broadcasted_iota(jnp.int32, (q, q), 0) >= lax.broadcasted_iota(jnp.int32, (q, q), 1)
    acs_end = jnp.sum(acs * is_last, axis=0, keepdims=True)
    acs_t = acs.T
    dt_exp = jnp.zeros((q, GROUP_W), F32)
    acs_exp = jnp.zeros((q, GROUP_W), F32)
    end_exp = jnp.zeros((1, GROUP_W), F32)
    d_exp = jnp.zeros((1, GROUP_W), F32)
    heads = []
    for k in range(HEADS_PER_GROUP):
        h = HEADS_PER_GROUP * g + k
        oh = (lane == h).astype(F32)
        mk = (head_of == k).astype(F32)
        acs_col = jnp.sum(acs * oh, axis=1, keepdims=True)
        acs_row = jnp.sum(acs_t * (sub == h).astype(F32), axis=0, keepdims=True)
        dt_exp = dt_exp + jnp.sum(dt * oh, axis=1, keepdims=True) * mk
        acs_exp = acs_exp + acs_col * mk
        end_exp = end_exp + jnp.sum(acs_end * oh, axis=1, keepdims=True) * mk
        d_exp = d_exp + jnp.sum(drow * oh, axis=1, keepdims=True) * mk
        heads.append((acs_col, acs_row, mk))
    xdt = x * dt_exp
    states = _dot(bm, xdt * jnp.exp(end_exp - acs_exp), _TN)
    y = _dot(cm, hp, _NN) * jnp.exp(acs_exp) + x * d_exp
    scores = _dot(cm, bm, _NT)
    for acs_col, acs_row, mk in heads:
        dec = jnp.exp(jnp.where(causal, acs_col - acs_row, -jnp.inf))
        y = y + _dot(scores * dec, xdt * mk, _NN)
    return y, hp * jnp.exp(end_exp) + states


def _ssd_fwd(act, dtraw, dtb, alog, drow, *, name, side=None):
    L = act.shape[0]
    q = min(SSD_CHUNK, L)
    nc = L // q
    d_ssd = N_GROUPS * GROUP_W

    def body(act_ref, dt_ref, dtb_ref, alog_ref, drow_ref, y_ref, hst_ref, h):
        @pl.when(pl.program_id(0) == 0)
        def _():
            h[...] = jnp.zeros_like(h)

        dt, acs = _ssd_f1(dt_ref[...], dtb_ref[...], alog_ref[...])
        drow = drow_ref[...]
        for g in range(N_GROUPS):
            hp = h[g]
            hst_ref[0, g] = hp
            y, hn = _ssd_group(g, act_ref[:, g * GROUP_W:(g + 1) * GROUP_W],
                               act_ref[:, d_ssd + g * STATE:d_ssd + (g + 1) * STATE],
                               act_ref[:, d_ssd + (N_GROUPS + g) * STATE:d_ssd + (N_GROUPS + g + 1) * STATE],
                               dt, acs, drow, hp)
            y_ref[:, g * GROUP_W:(g + 1) * GROUP_W] = y
            h[g] = hn

    row = pl.BlockSpec((1, 128), lambda i: (0, 0))
    outs, side_outs = _call(
        body, (act, dtraw, dtb, alog, drow), name=name, grid=(nc,),
        in_specs=[pl.BlockSpec((q, act.shape[1]), lambda i: (i, 0)), pl.BlockSpec((q, 128), lambda i: (i, 0)), row, row, row],
        out_specs=[pl.BlockSpec((q, d_ssd), lambda i: (i, 0)),
                   pl.BlockSpec((1, N_GROUPS, STATE, GROUP_W), lambda i: (i, 0, 0, 0))],
        out_shape=[jax.ShapeDtypeStruct((L, d_ssd), F32), jax.ShapeDtypeStruct((nc, N_GROUPS, STATE, GROUP_W), F32)],
        scratch_shapes=[pltpu.VMEM((N_GROUPS, STATE, GROUP_W), F32)], sem=("arbitrary",), side=side)
    return outs if side is None else (outs, side_outs)


def _ssd_bwd(act, dtraw, dy, hst, dtb, alog, drow, *, name, side=None):
    L = act.shape[0]
    q = min(SSD_CHUNK, L)
    nc = L // q
    d_ssd = N_GROUPS * GROUP_W

    def body(act_ref, dt_ref, dy_ref, hst_ref, dtb_ref, alog_ref, drow_ref, dact_ref, ddt_ref, dpar_ref, dh):
        @pl.when(pl.program_id(0) == 0)
        def _():
            dh[...] = jnp.zeros_like(dh)
            dpar_ref[...] = jnp.zeros_like(dpar_ref)

        (dt, acs), f1_vjp = jax.vjp(_ssd_f1, dt_ref[...], dtb_ref[...], alog_ref[...])
        drow = drow_ref[...]
        ddt = jnp.zeros_like(dt)
        dacs = jnp.zeros_like(acs)
        ddrow = jnp.zeros_like(drow)
        for g in range(N_GROUPS):
            xs = slice(g * GROUP_W, (g + 1) * GROUP_W)
            bs = slice(d_ssd + g * STATE, d_ssd + (g + 1) * STATE)
            cs = slice(d_ssd + (N_GROUPS + g) * STATE, d_ssd + (N_GROUPS + g + 1) * STATE)
            _, f2_vjp = jax.vjp(functools.partial(_ssd_group, g), act_ref[:, xs], act_ref[:, bs], act_ref[:, cs],
                                dt, acs, drow, hst_ref[0, g])
            dx, dbm, dcm, ddt_g, dacs_g, ddrow_g, dhp = f2_vjp((dy_ref[:, xs], dh[g]))
            dact_ref[:, xs] = dx
            dact_ref[:, bs] = dbm
            dact_ref[:, cs] = dcm
            dh[g] = dhp
            ddt, dacs, ddrow = ddt + ddt_g, dacs + dacs_g, ddrow + ddrow_g
        ddtraw, ddtb, dalog = f1_vjp((ddt, dacs))
        ddt_ref[...] = ddtraw
        dpar_ref[0:1, :] += ddtb
        dpar_ref[1:2, :] += dalog
        dpar_ref[2:3, :] += ddrow

    row = pl.BlockSpec((1, 128), lambda i: (0, 0))
    rev = lambda i: (nc - 1 - i, 0)
    outs, side_outs = _call(
        body, (act, dtraw, dy, hst, dtb, alog, drow), name=name, grid=(nc,),
        in_specs=[pl.BlockSpec((q, act.shape[1]), rev), pl.BlockSpec((q, 128), rev), pl.BlockSpec((q, d_ssd), rev),
                  pl.BlockSpec((1, N_GROUPS, STATE, GROUP_W), lambda i: (nc - 1 - i, 0, 0, 0)), row, row, row],
        out_specs=[pl.BlockSpec((q, act.shape[1]), rev), pl.BlockSpec((q, 128), rev), pl.BlockSpec((8, 128), lambda i: (0, 0))],
        out_shape=[jax.ShapeDtypeStruct(act.shape, F32), jax.ShapeDtypeStruct((L, 128), F32), jax.ShapeDtypeStruct((8, 128), F32)],
        scratch_shapes=[pltpu.VMEM((N_GROUPS, STATE, GROUP_W), F32)], sem=("arbitrary",), side=side)
    return outs if side is None else (outs, side_outs)


def _gate_head(xl, wa, ba, wx, bx, lam):
    r = _sigmoid(_dot(xl, wa, _NN) + ba)
    i = _sigmoid(_dot(xl, wx, _NN) + bx)
    log_a = -LRU_C * r * _softplus(-lam)
    return jnp.exp(log_a), jnp.sqrt(-_expm1(2.0 * log_a)) * (i * xl)


def _gate_head_bwd(xl, da, du, wa, ba, wx, bx, lam):
    _, vjp = jax.vjp(_gate_head, xl, wa, ba, wx, bx, lam)
    dxl, dwa, dba, dwx, dbx, dlam = vjp((da, du))
    return dxl, dwa, dba, dwx, dbx, dlam


def _scan_tile(a, b, rows, reverse):
    for d in (1, 2, 4):
        if reverse:
            keep = rows < 8 - d
            a_sh, b_sh = pltpu.roll(a, 8 - d, 0), pltpu.roll(b, 8 - d, 0)
        else:
            keep = rows >= d
            a_sh, b_sh = pltpu.roll(a, d, 0), pltpu.roll(b, d, 0)
        b = b + a * jnp.where(keep, b_sh, 0.0)
        a = a * jnp.where(keep, a_sh, 1.0)
    return a, b


def _lru_scan_fwd(a, u, *, name, tr=512, cb=1024):
    L, C = a.shape
    tr, cb = min(tr, L), min(cb, C)

    def body(a_ref, u_ref, h_ref, hp_ref, carry):
        @pl.when(pl.program_id(1) == 0)
        def _():
            carry[...] = jnp.zeros_like(carry)

        rows = lax.broadcasted_iota(jnp.int32, (8, cb), 0)

        def tile(t, hc):
            r0 = pl.multiple_of(t * 8, 8)
            pa, hb = _scan_tile(a_ref[pl.ds(r0, 8), :], u_ref[pl.ds(r0, 8), :], rows, False)
            h = hb + pa * hc
            h_ref[pl.ds(r0, 8), :] = h
            hp_ref[pl.ds(r0, 8), :] = jnp.where(rows >= 1, pltpu.roll(h, 1, 0), hc)
            return h[7:8, :]

        carry[...] = lax.fori_loop(0, tr // 8, tile, carry[...], unroll=4)

    blk = pl.BlockSpec((tr, cb), lambda j, i: (i, j))
    return pl.pallas_call(
        body, name=name, grid=(C // cb, L // tr),
        in_specs=[blk, blk], out_specs=[blk, blk],
        out_shape=[jax.ShapeDtypeStruct((L, C), F32), jax.ShapeDtypeStruct((L, C), F32)],
        scratch_shapes=[pltpu.VMEM((1, cb), F32)],
        compiler_params=_params(("parallel", "arbitrary")),
    )(a, u)


def _lru_scan_bwd(a, hprev, dh, *, name, tr=512, cb=1024):
    L, C = a.shape
    tr, cb = min(tr, L), min(cb, C)
    nb = L // tr

    def body(a_ref, hp_ref, dh_ref, da_ref, du_ref, carry):
        @pl.when(pl.program_id(1) == 0)
        def _():
            carry[...] = jnp.zeros_like(carry)

        rows = lax.broadcasted_iota(jnp.int32, (8, cb), 0)

        def tile(t, gc):
            r0 = pl.multiple_of((tr // 8 - 1 - t) * 8, 8)
            av, dv = a_ref[pl.ds(r0, 8), :], dh_ref[pl.ds(r0, 8), :]
            pa, gb = _scan_tile(av, av * dv, rows, True)
            big = gb + pa * gc
            g = dv + jnp.where(rows < 7, pltpu.roll(big, 7, 0), gc)
            du_ref[pl.ds(r0, 8), :] = g
            da_ref[pl.ds(r0, 8), :] = g * hp_ref[pl.ds(r0, 8), :]
            return big[0:1, :]

        carry[...] = lax.fori_loop(0, tr // 8, tile, carry[...], unroll=4)

    blk = pl.BlockSpec((tr, cb), lambda j, i: (nb - 1 - i, j))
    return pl.pallas_call(
        body, name=name, grid=(C // cb, nb),
        in_specs=[blk, blk, blk], out_specs=[blk, blk],
        out_shape=[jax.ShapeDtypeStruct((L, C), F32), jax.ShapeDtypeStruct((L, C), F32)],
        scratch_shapes=[pltpu.VMEM((1, cb), F32)],
        compiler_params=_params(("parallel", "arbitrary")),
    )(a, hprev, dh)


def _ssd_gate(y, z, n):
    v = y * _silu(z)
    return v * lax.rsqrt(jnp.mean(v * v, axis=-1, keepdims=True) + EPS) * n


def _ssd_gate_bwd(y, z, dy, n):
    _, vjp = jax.vjp(_ssd_gate, y, z, n)
    return vjp(dy.astype(F32))


def _lru_out(hl, gate, n):
    return _rms(hl * _gelu(gate), n)


def _lru_out_bwd(hl, gate, dy, n):
    _, vjp = jax.vjp(_lru_out, hl, gate, n)
    return vjp(dy.astype(F32))


def _mid(x, mix, pm, pmlp):
    x1 = x + _rms(mix, pm)
    return x1, _rms(x1, pmlp)


def _mid_bwd(x, mix, dx1p, dh2, pm, pmlp):
    _, vjp = jax.vjp(_mid, x, mix, pm, pmlp)
    dx, dmix, dpm, dpmlp = vjp((dx1p, dh2))
    return dmix, dx, dpm, dpmlp


def _loss_bwd(hm2, x1, tgt, g):
    def lossf(hm2, x1, g):
        e = x1 + _rms(hm2, g) - tgt
        return 0.5 * jnp.sum(jnp.mean(e * e, axis=-1, keepdims=True), axis=0, keepdims=True)

    val, vjp = jax.vjp(lossf, hm2, x1, g)
    dhm2, dx1, dg = vjp(jnp.ones((1, 1), F32))
    return dhm2, dx1, dg, val * jnp.ones((1, 128), F32)


def _in_bwd(x, dh_a, dh_b, dx1, g):
    _, vjp = jax.vjp(_rms, x, g)
    dx, dg = vjp(dh_a + dh_b)
    return dx + dx1, dg


def _adamw(w, g, m, v):
    m = ADAM_B1 * m + (1.0 - ADAM_B1) * g
    v = ADAM_B2 * v + (1.0 - ADAM_B2) * (g * g)
    m_hat = m / (1.0 - ADAM_B1 ** ADAM_STEP)
    v_hat = v / (1.0 - ADAM_B2 ** ADAM_STEP)
    return -ADAM_LR * (m_hat / (jnp.sqrt(v_hat) + ADAM_EPS) + ADAM_WD * w), m, v


class _LocalPlan:
    def __init__(self, p):
        self.p, self.large = p, {}

    def weight(self, name):
        return self.p[name]

    def side(self, kernel_name):
        return None

    def done(self, kernel_name, side_outs):
        pass

    def grad(self, name, g):
        self.large[name] = g

    def small(self, grads):
        pass


def _local_step(x, tgt, p, plan):
    L, D = x.shape

    def carry(fn, *args, name, **kw):
        side = plan.side(name)
        if side is None:
            return fn(*args, name=name, **kw)
        outs, side_outs = fn(*args, name=name, side=side, **kw)
        plan.done(name, side_outs)
        return outs

    d_ssd, d_xbc, d_lru, d_mix, d_ff = 2048, 4096, 2048, 4096, 8192
    n_main = d_ssd + d_xbc + 2 * d_lru
    c_xbc, c_gate, c_xl = d_ssd, d_ssd + d_xbc, d_ssd + d_xbc + d_lru
    TR, TRW, SUB = 256, 512, 32
    mm = dict(tm=1024, tn=1024, tk=2048)
    mmt = dict(tm=2048, tn=1024, tk=2048)

    (h,) = _rowwise(lambda xt, g: (_rms(xt, g),), [x], [p["pre_mix_norm"]], [(D, BF16)], [], name="f_prenorm", tr=TRW, sub=SUB)
    w_main, w_dt = plan.weight("w_main"), plan.weight("w_dt")
    (proj,) = carry(_matmul, h, w_main, mode="nn", m=L, n=n_main, k=D, out_dtypes=[F32], name="f_inproj", **mmt)
    (dtraw,) = _matmul(h, w_dt, mode="nn", m=L, n=128, k=D, out_dtypes=[F32], name="f_dtproj", **mm)
    pre, act = carry(_conv_fwd, proj, c_xbc, d_xbc, p["ssd_conv_w"], p["ssd_conv_b"], silu=True, name="f_ssdconv")
    (xl,) = carry(_conv_fwd, proj, c_xl, d_lru, p["lru_conv_w"], p["lru_conv_b"], silu=False, name="f_lruconv")
    yraw, hst = carry(_ssd_fwd, act, dtraw, p["dtb"], p["alog"], p["drow"], name="f_ssd")
    (ycat,) = _grouped(lambda y, z, n: (_ssd_gate(y, z, n),), [yraw, (proj, 0)], [p["ssd_norm"]], [BF16], [],
                       gw=GROUP_W, name="f_ssdgate", tr=TR, into={0: (None, d_mix, 0)})
    gate_p = [p["lru_w_a"], p["lru_b_a"], p["lru_w_x"], p["lru_b_x"], p["lru_lambda"]]
    a, u = carry(_grouped, _gate_head, [xl], gate_p, [F32, F32], [], gw=LRU_BLOCK, name="f_lrugates", tr=TR)
    hl, hprev = _lru_scan_fwd(a, u, name="f_lruscan")
    (ycat,) = _rowwise(lambda ht, gt, n: (_lru_out(ht, gt, n),), [hl, (proj, c_gate, d_lru)], [p["lru_norm"]],
                       [(d_lru, BF16)], [], name="f_lruout", tr=TRW, sub=SUB, into={0: (ycat, d_mix, d_ssd // d_lru)})
    w_out = plan.weight("w_out")
    (mix,) = carry(_matmul, ycat, w_out, mode="nn", m=L, n=D, k=d_mix, out_dtypes=[F32], name="f_outproj", **mm)
    x1, h2 = _rowwise(_mid, [x, mix], [p["post_mix_norm"], p["pre_mlp_norm"]], [(D, F32), (D, BF16)], [],
                      name="f_mid", tr=TRW, sub=SUB)
    nb_mi = (d_ff // 4) // mm["tn"]
    w_mi = plan.weight("w_mi")
    hm, act2 = carry(_matmul, h2, w_mi, mode="nn", m=L, n=d_ff, k=D, out_dtypes=[BF16, BF16], name="f_mlpin",
                     b_spec=pl.BlockSpec((None, mm["tk"], mm["tn"]), lambda i, j, kk: (j // nb_mi, kk, j % nb_mi)),
                     epilogue=lambda r: (r, jnp.square(jnp.maximum(r, 0.0))), **mmt)
    w_mo = plan.weight("w_mo")
    (hm2,) = _matmul(act2, w_mo, mode="nn", m=L, n=D, k=d_ff, out_dtypes=[F32], name="f_mlpout", **mm)

    dhm2, dx1p, d_post_mlp, loss = _rowwise(_loss_bwd, [hm2, x1, tgt], [p["post_mlp_norm"]], [(D, BF16), (D, F32)],
                                            [(1, D), (1, 128)], name="b_loss", tr=TR, sub=SUB)
    (dhm,) = _matmul(dhm2, w_mo, mode="nt", m=L, n=d_ff, k=D, out_dtypes=[BF16], name="b_mlpout_dx", extras=[hm],
                     epilogue=lambda r, hmv: (r * (2.0 * jnp.maximum(hmv.astype(F32), 0.0)),), **mmt)
    (dw_mo,) = _matmul(act2, dhm2, mode="tn", m=d_ff, n=D, k=L, out_dtypes=[BF16], name="b_mlpout_dw", **mmt)
    plan.grad("w_mlp_out", dw_mo.reshape(4, -1, D))
    kb_mi = (d_ff // 4) // mm["tk"]
    (dh2,) = carry(_matmul, dhm, w_mi, mode="nt", m=L, n=D, k=d_ff, out_dtypes=[F32], name="b_mlpin_dx",
                   b_spec=pl.BlockSpec((None, mm["tn"], mm["tk"]), lambda i, j, kk: (kk // kb_mi, j, kk % kb_mi)), **mm)
    (dw_mi,) = carry(_matmul, h2, dhm, mode="tn", m=D, n=d_ff, k=L, out_dtypes=[BF16], name="b_mlpin_dw",
                     out_specs=[pl.BlockSpec((None, mmt["tm"], mmt["tn"]), lambda i, j, kk: (j // nb_mi, i, j % nb_mi))],
                     out_shapes=[jax.ShapeDtypeStruct((4, D, d_ff // 4), BF16)], **mmt)
    plan.grad("w_mlp_in", dw_mi)
    dmix, dx1, d_post_mix, d_pre_mlp = carry(_rowwise, _mid_bwd, [x, mix, dx1p, dh2], [p["post_mix_norm"], p["pre_mlp_norm"]],
                                             [(D, BF16), (D, F32)], [(1, D), (1, D)], name="b_mid", tr=TR, sub=SUB)
    (dw_out,) = carry(_matmul, ycat, dmix, mode="tn", m=d_mix, n=D, k=L, out_dtypes=[BF16], name="b_outproj_dw", **mmt)
    plan.grad("w_out", dw_out.reshape(4, -1, D))
    (dycat,) = carry(_matmul, dmix, w_out, mode="nt", m=L, n=d_mix, k=D, out_dtypes=[BF16], name="b_outproj_dx", **mmt)
    dhl, dproj, d_lru_norm = _rowwise(_lru_out_bwd, [hl, (proj, c_gate, d_lru), (dycat, d_ssd, d_lru)], [p["lru_norm"]],
                                      [(d_lru, F32), (d_lru, BF16)], [(1, d_lru)], name="b_lruout", tr=TR, sub=SUB,
                                      into={1: (None, n_main, c_gate // d_lru)})
    da, du = _lru_scan_bwd(a, hprev, dhl, name="b_lruscan")
    dxl, d_wa, d_ba, d_wx, d_bx, d_lam = _grouped(
        _gate_head_bwd, [xl, da, du], gate_p, [F32],
        [(LRU_HEADS, LRU_BLOCK, LRU_BLOCK), (1, d_lru), (LRU_HEADS, LRU_BLOCK, LRU_BLOCK), (1, d_lru), (1, d_lru)],
        gw=LRU_BLOCK, name="b_lrugates", tr=TR)
    CB = 1024
    dproj, dwb_lru = _conv_bwd(dxl, 0, None, proj, c_xl, d_lru, p["lru_conv_w"], silu=False, name="b_lruconv", cb=CB,
                               into={0: (dproj, n_main, c_xl // CB)})
    dyraw, dproj, d_ssd_norm = _grouped(_ssd_gate_bwd, [yraw, (proj, 0), (dycat, 0)], [p["ssd_norm"]], [F32, BF16], [(1, d_ssd)],
                                        gw=GROUP_W, name="b_ssdgate", tr=TR, into={1: (dproj, n_main, 0)})
    dact, ddtraw, dpar = carry(_ssd_bwd, act, dtraw, dyraw, hst, p["dtb"], p["alog"], p["drow"], name="b_ssd")
    dproj, dwb_ssd = carry(_conv_bwd, dact, 0, pre, proj, c_xbc, d_xbc, p["ssd_conv_w"], silu=True, name="b_ssdconv", cb=CB,
                           into={0: (dproj, n_main, c_xbc // CB)})
    early = dict(loss=loss, ssd_conv=dwb_ssd, ssd_par=dpar, ssd_norm=d_ssd_norm, lru_conv=dwb_lru, lru_w_a=d_wa, lru_b_a=d_ba,
                 lru_w_x=d_wx, lru_b_x=d_bx, lru_lambda=d_lam, lru_norm=d_lru_norm, post_mix_norm=d_post_mix,
                 pre_mlp_norm=d_pre_mlp, post_mlp_norm=d_post_mlp)
    plan.small(early)
    (dw_main,) = carry(_matmul, h, dproj, mode="tn", m=D, n=n_main, k=L, out_dtypes=[BF16], name="b_inproj_dw", **mmt)
    (dw_dt,) = _matmul(h, ddtraw, mode="tn", m=D, n=128, k=L, out_dtypes=[BF16], name="b_dtproj_dw", **mm)
    plan.grad("w_in", (dw_main, dw_dt))
    (dh_a,) = carry(_matmul, dproj, w_main, mode="nt", m=L, n=D, k=n_main, out_dtypes=[F32], name="b_inproj_dx", **mm)
    (dh_b,) = carry(_matmul, ddtraw, w_dt, mode="nt", m=L, n=D, k=128, out_dtypes=[F32], name="b_dtproj_dx", **mm)
    grad_x, d_pre_mix = carry(_rowwise, _in_bwd, [x, dh_a, dh_b, dx1], [p["pre_mix_norm"]], [(D, F32)], [(1, D)],
                              name="b_prenorm", tr=TRW, sub=SUB)

    return grad_x, dict(early, pre_mix_norm=d_pre_mix)


def _place():
    return lax.axis_index("x"), lax.axis_index("y"), lax.axis_index("c")


def _other_chips(x, y):
    return [(1 - x, y), (x, 1 - y), (1 - x, 1 - y)]


def _allgather8(blk, *, name):
    r, n = blk.shape

    def body(x_ref, out_ref, send_sems, recv_sems, local_sem):
        x, y, c = _place()
        me, sibling = (x, y, c), (x, y, 1 - c)
        chips = _other_chips(x, y)

        def rows(px, py, pc):
            return out_ref.at[pl.ds((4 * px + 2 * py + pc) * r, r), :]

        def copy(k, block, to, src=None):
            return pltpu.make_async_remote_copy(
                src_ref=rows(*block) if src is None else src, dst_ref=rows(*block),
                send_sem=send_sems.at[k], recv_sem=recv_sems.at[k], device_id=to, device_id_type=MESH)

        mine = pltpu.make_async_copy(x_ref, rows(*me), local_sem)
        mine.start()
        first = [copy(0, me, sibling, src=x_ref)]
        first += [copy(1 + k, me, (*chip, c), src=x_ref) for k, chip in enumerate(chips)]
        for cp in first:
            cp.start()
        passed = [copy(4 + k, (*chip, c), sibling) for k, chip in enumerate(chips)]
        for k, chip in enumerate(chips):
            copy(1 + k, (*chip, c), me).wait_recv()
            passed[k].start()
        copy(0, sibling, me).wait_recv()
        for k, chip in enumerate(chips):
            copy(4 + k, (*chip, 1 - c), me).wait_recv()
        for cp in first + passed:
            cp.wait_send()
        mine.wait()

    return pl.pallas_call(
        body, name=name,
        out_shape=jax.ShapeDtypeStruct((8 * r, n), blk.dtype),
        in_specs=[pl.BlockSpec(memory_space=pltpu.VMEM)], out_specs=pl.BlockSpec(memory_space=pltpu.VMEM),
        scratch_shapes=[pltpu.SemaphoreType.DMA((7,)), pltpu.SemaphoreType.DMA((7,)), pltpu.SemaphoreType.DMA],
        compiler_params=pltpu.CompilerParams(vmem_limit_bytes=VMEM_LIMIT),
    )(blk)


def _sum8(g, *, name):
    _, r, n = g.shape
    tr = max(t for t in range(8, min(r, 512) + 1, 8) if r % t == 0)

    def body(g_ref, o_ref):
        s = g_ref[0]
        for k in range(1, 8):
            s = s + g_ref[k]
        o_ref[...] = s

    return pl.pallas_call(
        body, name=name, grid=(r // tr,),
        in_specs=[pl.BlockSpec((8, tr, n), lambda i: (0, i, 0))], out_specs=pl.BlockSpec((tr, n), lambda i: (i, 0)),
        out_shape=jax.ShapeDtypeStruct((r, n), g.dtype), compiler_params=_params(("parallel",)),
    )(g)


def _blocks(fn, ins, outs, *, grid, name, prefetch=None, aliases=None):
    n_in = len(ins)

    def body(*refs):
        if prefetch is not None:
            refs = refs[1:]
        res = fn(*[r[...] for r in refs[:n_in]])
        for o_ref, o in zip(refs[n_in:], res):
            o_ref[...] = o.astype(o_ref.dtype)

    in_specs = [pl.BlockSpec(b, m) for _, b, m in ins]
    out_specs = [pl.BlockSpec(b, m) for _, b, m in outs]
    kw = dict(name=name, out_shape=[s for s, _, _ in outs], input_output_aliases=aliases or {},
              compiler_params=_params(("arbitrary",) * len(grid)))
    arrs = [a for a, _, _ in ins]
    if prefetch is None:
        return pl.pallas_call(body, grid=grid, in_specs=in_specs, out_specs=out_specs, **kw)(*arrs)
    spec = pltpu.PrefetchScalarGridSpec(num_scalar_prefetch=1, grid=grid, in_specs=in_specs, out_specs=out_specs)
    return pl.pallas_call(body, grid_spec=spec, **kw)(prefetch, *arrs)


def _gather_tree(slot, *, name):
    def body(i_ref, g_ref, send_sems, recv_sems):
        x, y, c = _place()
        j, jx, jy, jd = 2 * x + y, 2 * (1 - x) + y, 2 * x + (1 - y), 2 * (1 - x) + (1 - y)
        xn, yn, sibling = (1 - x, y, c), (x, 1 - y, c), (x, y, 1 - c)
        rp = i_ref.shape[1] // 4

        def piece(ref, chip, hc, q):
            return ref.at[chip, pl.ds((2 * hc + q) * rp, rp), :]

        def cp(k, src, dst, to):
            return pltpu.make_async_remote_copy(src_ref=src, dst_ref=dst, send_sem=send_sems.at[k], recv_sem=recv_sems.at[k],
                                                device_id=to, device_id_type=MESH)

        def landed(k, chip, q):
            return cp(k, piece(g_ref, chip, c, q), piece(g_ref, chip, c, q), xn)

        sends = []

        def go(d):
            d.start()
            sends.append(d)

        go(cp(0, piece(i_ref, j, c, 0), piece(g_ref, j, c, 0), xn))
        go(cp(2, piece(i_ref, j, c, 1), piece(g_ref, j, c, 1), yn))
        go(cp(1, piece(i_ref, j, c, 1), piece(g_ref, j, c, 1), xn))
        go(cp(3, piece(i_ref, j, c, 0), piece(g_ref, j, c, 0), yn))
        arrivals = [(0, jx, 0, 4, yn), (2, jy, 1, 5, xn), (1, jx, 1, None, None), (3, jy, 0, None, None),
                    (4, jd, 0, None, None), (5, jd, 1, None, None)]
        for n_arr, (k, chip, q, k_fwd, to) in enumerate(arrivals):
            landed(k, chip, q).wait_recv()
            if k_fwd is not None:
                go(cp(k_fwd, piece(g_ref, chip, c, q), piece(g_ref, chip, c, q), to))
            go(cp(6 + n_arr, piece(g_ref, chip, c, q), piece(g_ref, chip, c, q), sibling))
        for n_arr, (_, chip, q, _, _) in enumerate(arrivals):
            other = piece(g_ref, chip, 1 - c, q)
            cp(6 + n_arr, other, other, sibling).wait_recv()
        for d in sends:
            d.wait_send()

    return pl.pallas_call(
        body, name=name, out_shape=jax.ShapeDtypeStruct(slot.shape, slot.dtype),
        in_specs=[ANY], out_specs=ANY, input_output_aliases={0: 0},
        scratch_shapes=[pltpu.SemaphoreType.DMA((12,)), pltpu.SemaphoreType.DMA((12,))],
    )(slot)


def _pack(arrs):
    parts = []
    for v in arrs:
        f = v.reshape(-1)
        f = jnp.pad(f, (0, (-f.shape[0]) % 1024))
        parts.append(f.reshape(-1, 128))
    return jnp.concatenate(parts, axis=0)


def _unpack(packed, shapes):
    out, r0 = [], 0
    for s in shapes:
        size = 1
        for d in s:
            size *= d
        nr = (size + 1023) // 1024 * 8
        out.append(packed[r0:r0 + nr].reshape(-1)[:size].reshape(s))
        r0 += nr
    return out


SMALL_GRADS = [("loss", (1, 128)), ("pre_mix_norm", (1, 2048)), ("ssd_conv", (8, 4096)), ("ssd_par", (8, 128)),
               ("ssd_norm", (1, 2048)), ("lru_conv", (8, 2048)), ("lru_w_a", (16, 128, 128)), ("lru_b_a", (1, 2048)),
               ("lru_w_x", (16, 128, 128)), ("lru_b_x", (1, 2048)), ("lru_lambda", (1, 2048)), ("lru_norm", (1, 2048)),
               ("post_mix_norm", (1, 2048)), ("pre_mlp_norm", (1, 2048)), ("post_mlp_norm", (1, 2048))]

SMALL_EARLY = [g for g in SMALL_GRADS if g[0] != "pre_mix_norm"]

WEIGHTS = ['pre_mix_norm', 'w_in', 'ssd_conv_w', 'ssd_conv_b', 'ssd_dt_bias', 'ssd_a_log', 'ssd_d', 'ssd_norm', 'lru_conv_w',
           'lru_conv_b', 'lru_w_a', 'lru_b_a', 'lru_w_x', 'lru_b_x', 'lru_lambda', 'lru_norm', 'w_out', 'post_mix_norm',
           'pre_mlp_norm', 'w_mlp_in', 'w_mlp_out', 'post_mlp_norm']
LARGE = ['w_in', 'w_out', 'w_mlp_in', 'w_mlp_out']

D_SSD, D_XBC, DT_W = 2048, 4096, 32
TB = 256


def _w_in_runs(n_shard, n_main):
    c_dt = D_SSD + D_XBC
    runs, p = [], 0
    while p < n_main:
        j, off = divmod(p if p < c_dt else p + DT_W, n_shard)
        ln = min(n_shard - off, (c_dt if p < c_dt else n_main) - p)
        runs.append((p, j, off, ln))
        p += ln
    jd, offd = divmod(c_dt, n_shard)
    assert offd + DT_W <= n_shard
    return runs, (jd, offd)


def _pack_w_in(slots, *, name):
    _, d, n_shard = slots.shape
    n_main = 4 * n_shard - DT_W
    runs, (jd, offd) = _w_in_runs(n_shard, n_main)

    def body(s_ref, main_ref, dt_ref):
        for p, j, off, ln in runs:
            main_ref[:, p:p + ln] = s_ref[j, :, off:off + ln]
        dt_ref[:, 0:DT_W] = s_ref[jd, :, offd:offd + DT_W]
        dt_ref[:, DT_W:] = jnp.zeros((TB, 128 - DT_W), dt_ref.dtype)

    return pl.pallas_call(
        body, name=name, grid=(d // TB,),
        in_specs=[pl.BlockSpec((4, TB, n_shard), lambda i: (0, i, 0))],
        out_specs=[pl.BlockSpec((TB, n_main), lambda i: (i, 0)), pl.BlockSpec((TB, 128), lambda i: (i, 0))],
        out_shape=[jax.ShapeDtypeStruct((d, n_main), slots.dtype), jax.ShapeDtypeStruct((d, 128), slots.dtype)],
        compiler_params=_params(("parallel",)),
    )(slots)


def _unpack_dw_in(dw_main, dw_dt, *, name):
    d, n_main = dw_main.shape
    n_shard = (n_main + DT_W) // 4
    runs, (jd, offd) = _w_in_runs(n_shard, n_main)

    def body(main_ref, dt_ref, o_ref):
        for p, j, off, ln in runs:
            o_ref[j, :, off:off + ln] = main_ref[:, p:p + ln]
        o_ref[jd, :, offd:offd + DT_W] = dt_ref[:, 0:DT_W]

    return pl.pallas_call(
        body, name=name, grid=(d // TB,),
        in_specs=[pl.BlockSpec((TB, n_main), lambda i: (i, 0)), pl.BlockSpec((TB, 128), lambda i: (i, 0))],
        out_specs=pl.BlockSpec((4, TB, n_shard), lambda i: (0, i, 0)),
        out_shape=jax.ShapeDtypeStruct((4, d, n_shard), dw_main.dtype),
        compiler_params=_params(("parallel",)),
    )(dw_main, dw_dt)


def _half(ref, chip_idx, hc, piece=(0, 1)):
    q, nq = piece
    rp = ref.shape[1] // (2 * nq)
    return ref.at[chip_idx, pl.ds((hc * nq + q) * rp, rp), :]


def _job_gather_ici(i_ref, g_ref, send_sems, recv_sems, base, piece):
    x, y, c = _place()
    j = 2 * x + y
    sends, recvs = [], []
    for k, (px, py) in enumerate(_other_chips(x, y)):
        kw = dict(send_sem=send_sems.at[base + k], recv_sem=recv_sems.at[base + k], device_id=(px, py, c), device_id_type=MESH)
        sends.append(pltpu.make_async_remote_copy(src_ref=_half(i_ref, j, c, piece), dst_ref=_half(g_ref, j, c, piece), **kw))
        landed = _half(g_ref, 2 * px + py, c, piece)
        recvs.append(pltpu.make_async_remote_copy(src_ref=landed, dst_ref=landed, **kw))
    return sends, recvs


def _job_gather_sibling(i_ref, g_ref, send_sems, recv_sems, base, piece):
    x, y, c = _place()
    sends, recvs = [], []
    for k, (px, py) in enumerate(_other_chips(x, y)):
        kw = dict(send_sem=send_sems.at[base + k], recv_sem=recv_sems.at[base + k], device_id=(x, y, 1 - c), device_id_type=MESH)
        sends.append(pltpu.make_async_remote_copy(src_ref=_half(i_ref, 2 * px + py, c, piece),
                                                  dst_ref=_half(g_ref, 2 * px + py, c, piece), **kw))
        other = _half(g_ref, 2 * px + py, 1 - c, piece)
        recvs.append(pltpu.make_async_remote_copy(src_ref=other, dst_ref=other, **kw))
    return sends, recvs


def _job_reduce_ici(s_ref, got_refs, send_sems, recv_sems, base, piece):
    x, y, c = _place()
    q, nq, count = piece
    rp = s_ref.shape[1] // nq
    rows = pl.ds(q * rp, count * rp)
    sends = [pltpu.make_async_remote_copy(src_ref=s_ref.at[2 * px + py, rows, :], dst_ref=got_refs[k].at[rows, :],
                                          send_sem=send_sems.at[base + k], recv_sem=recv_sems.at[base + k],
                                          device_id=(px, py, c), device_id_type=MESH)
             for k, (px, py) in enumerate(_other_chips(x, y))]
    return sends, sends


GATHER_PLAN = {
    "f_inproj": [("w_out", _job_gather_ici, (0, 1)), ("w_mlp_in", _job_gather_ici, (0, 2))],
    "f_ssdconv": [("w_mlp_in", _job_gather_ici, (1, 2)), ("w_out", _job_gather_sibling, (0, 1)),
                  ("w_mlp_in", _job_gather_sibling, (0, 2))],
    "f_lruconv": [("w_mlp_in", _job_gather_sibling, (1, 2))],
    "f_ssd": [("w_mlp_out", _job_gather_ici, (0, 2))],
    "f_lrugates": [("w_mlp_out", _job_gather_sibling, (0, 2))],
    "f_outproj": [("w_mlp_out", _job_gather_ici, (1, 2))],
    "f_mlpin": [("w_mlp_out", _job_gather_sibling, (1, 2))],
}
SIBLING_PLAN = {"b_mlpin_dx": "w_mlp_out", "b_mid": "w_mlp_in", "b_outproj_dx": "w_out"}
REDUCE_PLAN = {
    "b_mlpin_dw": [("w_mlp_out", (0, 2, 1))],
    "b_outproj_dw": [("w_mlp_out", (1, 2, 1))],
    "b_ssd": [("w_mlp_in", (0, 1, 1)), ("w_out", (0, 1, 1))],
    "b_inproj_dx": [("w_in", (0, 8, 7))],
    "b_dtproj_dx": [("w_in", (7, 8, 1))],
}
SHARE_PLAN = {"b_ssdconv": ["w_mlp_out", "w_mlp_in", "w_out"]}
SMALL_PLAN = {"b_inproj_dw": 0, "b_inproj_dx": 1}


class _Part:
    def __init__(self, ins, outs, aliases, n_sems, make, done):
        self.ins, self.outs, self.aliases, self.n_sems, self.make, self.done = ins, outs, aliases, n_sems, make, done


def _merge_parts(parts):
    ins, outs, aliases, offs, n = [], [], {}, [], 0
    for p in parts:
        offs.append((len(ins), len(outs), n))
        aliases.update({len(ins) + i: len(outs) + o for i, o in p.aliases.items()})
        ins, outs, n = ins + list(p.ins), outs + list(p.outs), n + p.n_sems

    def make(i_refs, o_refs, send_sems, recv_sems):
        sends, recvs = [], []
        for p, (io, oo, so) in zip(parts, offs):
            s, r = p.make(i_refs[io:io + len(p.ins)], o_refs[oo:oo + len(p.outs)], send_sems, recv_sems, so)
            sends, recvs = sends + s, recvs + r
        return sends, recvs

    return _Side(ins, outs, aliases, n, make)


def _parts_done(parts, outs):
    o = 0
    for p in parts:
        p.done(list(outs[o:o + len(p.outs)]))
        o += len(p.outs)


def _comm_call(parts, *, name):
    side = _merge_parts(parts)
    si, so = len(side.ins), len(side.outs)

    def body(*refs):
        sends, recvs = side.make(refs[:si], refs[si:si + so], refs[-2], refs[-1])
        for d in sends:
            d.start()
        for d in recvs:
            d.wait_recv()
        for d in sends:
            d.wait_send()

    outs = pl.pallas_call(
        body, name=name, out_shape=list(side.outs), in_specs=[ANY] * si, out_specs=[ANY] * so,
        scratch_shapes=[pltpu.SemaphoreType.DMA((side.n_sems,)), pltpu.SemaphoreType.DMA((side.n_sems,))],
        input_output_aliases=side.aliases,
    )(*side.ins)
    _parts_done(parts, outs)


class _DistPlan:
    def __init__(self, slots, w_main, w_dt, where, shapes):
        self.slots, self.w_main, self.w_dt, self.where, self.shapes = slots, w_main, w_dt, where, shapes
        self.sib_pending, self.partial, self.got, self.totals, self.shared = {}, {}, {}, {}, {}
        self.small_buf = None

    def weight(self, name):
        d = self.w_main.shape[0]
        return {"w_main": lambda: self.w_main, "w_dt": lambda: self.w_dt, "w_out": lambda: self.slots["w_out"].reshape(-1, d),
                "w_mi": lambda: self.slots["w_mlp_in"], "w_mo": lambda: self.slots["w_mlp_out"].reshape(-1, d)}[name]()

    def _gather_part(self, jobs):
        names = []
        for n, _, _ in jobs:
            if n not in names:
                names.append(n)
        arrs = [self.slots[n] for n in names]

        def make(i_refs, o_refs, send_sems, recv_sems, base):
            sends, recvs = [], []
            for q, (n, job, piece) in enumerate(jobs):
                s, r = job(i_refs[names.index(n)], o_refs[names.index(n)], send_sems, recv_sems, base + 3 * q, piece)
                sends, recvs = sends + s, recvs + r
            return sends, recvs

        return _Part(arrs, [jax.ShapeDtypeStruct(a.shape, a.dtype) for a in arrs], {i: i for i in range(len(arrs))},
                     3 * len(jobs), make, lambda outs: self.slots.update(zip(names, outs)))

    def _sibling_part(self, name):
        g = self.sib_pending.pop(name)
        _, _, rh, cc = g.shape

        def make(i_refs, o_refs, send_sems, recv_sems, base):
            x, y, c = _place()
            sends = [pltpu.make_async_remote_copy(src_ref=i_refs[0].at[s, 1 - c], dst_ref=o_refs[0].at[s],
                                                  send_sem=send_sems.at[base + s], recv_sem=recv_sems.at[base + s],
                                                  device_id=(x, y, 1 - c), device_id_type=MESH) for s in range(4)]
            return sends, sends

        def done(outs):
            (t,) = outs
            self.partial[name] = _blocks(
                lambda u, v: (u.astype(F32) + v.astype(F32),),
                [(g, (None, None, TB, cc), lambda q, i, s: (q, s[1], i, 0)), (t, (None, TB, cc), lambda q, i, s: (q, i, 0))],
                [(jax.ShapeDtypeStruct(t.shape, BF16), (None, TB, cc), lambda q, i, s: (q, i, 0))],
                grid=(4, rh // TB), name="add_sibling_" + name, prefetch=self.where)[0]

        return _Part([g], [jax.ShapeDtypeStruct((4, rh, cc), g.dtype)], {}, 4, make, done)

    def _reduce_part(self, items):
        parts = [self.partial[n] for n, _ in items]
        ins, aliases = list(parts), {}
        for q, (n, (first, _, _)) in enumerate(items):
            if first > 0:
                for k in range(3):
                    aliases[len(ins)] = 3 * q + k
                    ins.append(self.got[n][k])

        def make(i_refs, o_refs, send_sems, recv_sems, base):
            sends = []
            for q, (_, piece) in enumerate(items):
                sends += _job_reduce_ici(i_refs[q], o_refs[3 * q:3 * q + 3], send_sems, recv_sems, base + 3 * q, piece)[0]
            return sends, sends

        def done(outs):
            for q, (name, (first, n_pieces, count)) in enumerate(items):
                g3 = self.got[name] = outs[3 * q:3 * q + 3]
                if first + count < n_pieces:
                    continue
                s4 = self.partial[name]
                _, rh, cc = s4.shape
                row = ((TB, cc), lambda i, s: (i, 0))
                self.totals[name] = _blocks(
                    lambda o, r0, r1, r2: (((o.astype(F32) + r0.astype(F32)) + r1.astype(F32)) + r2.astype(F32),),
                    [(s4, (None, TB, cc), lambda i, s: (s[0], i, 0)), (g3[0],) + row, (g3[1],) + row, (g3[2],) + row],
                    [(jax.ShapeDtypeStruct((2, rh, cc), F32), (None, TB, cc), lambda i, s: (s[1], i, 0))],
                    grid=(rh // TB,), name="add_chips_" + name, prefetch=self.where)[0]
                if name == "w_in":
                    _comm_call([self._share_part([name])], name="share_" + name)

        outs = [jax.ShapeDtypeStruct(s.shape[1:], s.dtype) for s in parts for _ in range(3)]
        return _Part(ins, outs, aliases, 3 * len(items), make, done)

    def _share_part(self, names):
        ts = [self.totals[n] for n in names]

        def make(i_refs, o_refs, send_sems, recv_sems, base):
            x, y, c = _place()
            sends, recvs = [], []
            for w in range(len(ts)):
                kw = dict(send_sem=send_sems.at[base + w], recv_sem=recv_sems.at[base + w], device_id=(x, y, 1 - c),
                          device_id_type=MESH)
                sends.append(pltpu.make_async_remote_copy(src_ref=i_refs[w].at[c], dst_ref=o_refs[w].at[c], **kw))
                recvs.append(pltpu.make_async_remote_copy(src_ref=i_refs[w].at[c], dst_ref=o_refs[w].at[1 - c], **kw))
            return sends, recvs

        def done(outs):
            self.shared.update({n: o.reshape(self.shapes[n]) for n, o in zip(names, outs)})

        return _Part(ts, [jax.ShapeDtypeStruct(t.shape, t.dtype) for t in ts], {i: i for i in range(len(ts))}, len(ts), make, done)

    def _small_part(self, stage):
        buf = self.small_buf

        def make(i_refs, o_refs, send_sems, recv_sems, base):
            x, y, c = _place()
            src, dst = i_refs[0], o_refs[0]
            sends, recvs = [], []

            def add(k, block, to, frm):
                kw = dict(send_sem=send_sems.at[base + k], recv_sem=recv_sems.at[base + k], device_id=to, device_id_type=MESH)
                sends.append(pltpu.make_async_remote_copy(src_ref=src.at[block], dst_ref=dst.at[block], **kw))
                recvs.append(pltpu.make_async_remote_copy(src_ref=src.at[frm], dst_ref=dst.at[frm], **kw))

            if stage == 0:
                add(0, 4 * x + 2 * y + c, (x, y, 1 - c), 4 * x + 2 * y + 1 - c)
            for k, (px, py) in enumerate(_other_chips(x, y)):
                if stage == 0:
                    add(1 + k, 4 * x + 2 * y + c, (px, py, c), 4 * px + 2 * py + c)
                else:
                    add(k, 4 * px + 2 * py + c, (x, y, 1 - c), 4 * px + 2 * py + 1 - c)
            return sends, recvs

        def done(outs):
            (self.small_buf,) = outs

        return _Part([buf], [jax.ShapeDtypeStruct(buf.shape, buf.dtype)], {0: 0}, 4 if stage == 0 else 3, make, done)

    def side(self, kernel_name):
        parts = []
        if kernel_name in GATHER_PLAN:
            parts.append(self._gather_part(GATHER_PLAN[kernel_name]))
        if kernel_name in SIBLING_PLAN:
            parts.append(self._sibling_part(SIBLING_PLAN[kernel_name]))
        if kernel_name in REDUCE_PLAN:
            parts.append(self._reduce_part(REDUCE_PLAN[kernel_name]))
        if kernel_name in SHARE_PLAN:
            parts.append(self._share_part(SHARE_PLAN[kernel_name]))
        if kernel_name in SMALL_PLAN:
            parts.append(self._small_part(SMALL_PLAN[kernel_name]))
        self._carried = parts
        return _merge_parts(parts) if parts else None

    def done(self, kernel_name, side_outs):
        _parts_done(self._carried, side_outs)

    def grad(self, name, g):
        if name == "w_in":
            g = _unpack_dw_in(*g, name="unpack_dw_in")
        self.sib_pending[name] = g.reshape(4, 2, g.shape[1] // 2, g.shape[2])
        if name == "w_in":
            _comm_call([self._sibling_part(name)], name="reduce_sibling_" + name)

    def small(self, grads):
        packed = _pack([grads[n] for n, _ in SMALL_EARLY])
        n_rows = packed.shape[0]
        self.small_buf = _blocks(lambda t: (t,), [(packed, (n_rows, 128), lambda i, s: (0, 0))],
                                 [(jax.ShapeDtypeStruct((8, n_rows, 128), F32), (None, n_rows, 128), lambda i, s: (s[2], 0, 0))],
                                 grid=(1,), name="place_small", prefetch=self.where)[0]

    def finish(self):
        assert len(self.shared) == len(self.shapes)
        return self.shared


def kernel(x, pre_mix_norm, w_in, ssd_conv_w, ssd_conv_b, ssd_dt_bias, ssd_a_log, ssd_d, ssd_norm, lru_conv_w, lru_conv_b, lru_w_a, lru_b_a, lru_w_x, lru_b_x, lru_lambda, lru_norm, w_out, post_mix_norm, pre_mlp_norm, w_mlp_in, w_mlp_out, post_mlp_norm, loss_target, m_pre_mix_norm, m_w_in, m_ssd_conv_w, m_ssd_conv_b, m_ssd_dt_bias, m_ssd_a_log, m_ssd_d, m_ssd_norm, m_lru_conv_w, m_lru_conv_b, m_lru_w_a, m_lru_b_a, m_lru_w_x, m_lru_b_x, m_lru_lambda, m_lru_norm, m_w_out, m_post_mix_norm, m_pre_mlp_norm, m_w_mlp_in, m_w_mlp_out, m_post_mlp_norm, v_pre_mix_norm, v_w_in, v_ssd_conv_w, v_ssd_conv_b, v_ssd_dt_bias, v_ssd_a_log, v_ssd_d, v_ssd_norm, v_lru_conv_w, v_lru_conv_b, v_lru_w_a, v_lru_b_a, v_lru_w_x, v_lru_b_x, v_lru_lambda, v_lru_norm, v_w_out, v_post_mix_norm, v_pre_mlp_norm, v_w_mlp_in, v_w_mlp_out, v_post_mlp_norm):
    a = dict(locals())
    j = 2 * lax.axis_index("x") + lax.axis_index("y")
    D = x.shape[-1]
    c_dt = D_SSD + D_XBC

    core = lax.axis_index("c")
    where = jnp.stack([j, core, 2 * j + core]).astype(jnp.int32)
    slots = {}
    for name in LARGE:
        w = a[name][0]
        r, cc = w.shape
        slots[name] = _blocks(lambda t: (t,), [(w, (TB, cc), lambda i, s: (i, 0))],
                              [(jax.ShapeDtypeStruct((4, r, cc), BF16), (None, TB, cc), lambda i, s: (s[0], i, 0))],
                              grid=(r // TB,), name="cast_" + name, prefetch=where)[0]
    g_in = _gather_tree(slots.pop("w_in"), name="gather_w_in")
    w_main, w_dt = _pack_w_in(g_in, name="pack_w_in")
    taps = jnp.concatenate([ssd_conv_w[0].reshape(-1, 128), lru_conv_w[0].reshape(-1, 128)], axis=0)
    taps = _allgather8(taps, name="gather_taps").reshape(8, taps.shape[0], 128)[0::2]
    n_ssd = ssd_conv_w.shape[1] * ssd_conv_w.shape[2] // 128
    ssd_taps = taps[:, :n_ssd].reshape(4, CONV_W, -1).transpose(1, 0, 2).reshape(CONV_W, -1)
    lru_taps = taps[:, n_ssd:].reshape(4, CONV_W, -1).transpose(1, 0, 2).reshape(CONV_W, -1)

    def row128(v):
        return jnp.pad(v, ((0, 0), (0, 128 - v.shape[1])))

    p = dict(pre_mix_norm=pre_mix_norm, ssd_conv_w=ssd_taps, ssd_conv_b=ssd_conv_b,
             dtb=row128(ssd_dt_bias), alog=row128(ssd_a_log), drow=row128(ssd_d), ssd_norm=ssd_norm,
             lru_conv_w=lru_taps, lru_conv_b=lru_conv_b, lru_w_a=lru_w_a[0], lru_b_a=lru_b_a.reshape(1, -1),
             lru_w_x=lru_w_x[0], lru_b_x=lru_b_x.reshape(1, -1), lru_lambda=lru_lambda, lru_norm=lru_norm,
             post_mix_norm=post_mix_norm, pre_mlp_norm=pre_mlp_norm, post_mlp_norm=post_mlp_norm)
    plan = _DistPlan(slots, w_main, w_dt, where, {n: a[n].shape for n in LARGE})
    grad_x, small = _local_step(x[0], loss_target[0], p, plan)

    large_grads = plan.finish()

    delta, new_m, new_v = {}, {}, {}
    for name in LARGE:
        w = a[name][0]
        cc = w.shape[1]
        outs = _rowwise(_adamw, [w, large_grads[name][0], a["m_" + name][0], a["v_" + name][0]], [], [(cc, F32)] * 3, [],
                        name="adamw_" + name, tr=128, sub=8)
        delta[name], new_m[name], new_v[name] = [o[None] for o in outs]

    total = _sum8(plan.small_buf, name="sum_small")
    tot = dict(zip([n for n, _ in SMALL_EARLY], _unpack(total, [s for _, s in SMALL_EARLY])))
    late = small["pre_mix_norm"].reshape(-1, 128)
    late = _allgather8(late, name="gather_late").reshape(8, late.shape[0], 128)
    tot["pre_mix_norm"] = _sum8(late, name="sum_late").reshape(small["pre_mix_norm"].shape)
    loss = tot["loss"][0, 0]
    n_sc, n_lc = ssd_conv_w.shape[2], lru_conv_w.shape[2]
    grads = dict(
        pre_mix_norm=tot["pre_mix_norm"],
        ssd_conv_w=lax.dynamic_slice(tot["ssd_conv"][:CONV_W], (0, j * n_sc), (CONV_W, n_sc))[None],
        ssd_conv_b=tot["ssd_conv"][CONV_W:CONV_W + 1],
        ssd_dt_bias=tot["ssd_par"][0:1, :DT_W], ssd_a_log=tot["ssd_par"][1:2, :DT_W], ssd_d=tot["ssd_par"][2:3, :DT_W],
        ssd_norm=tot["ssd_norm"],
        lru_conv_w=lax.dynamic_slice(tot["lru_conv"][:CONV_W], (0, j * n_lc), (CONV_W, n_lc))[None],
        lru_conv_b=tot["lru_conv"][CONV_W:CONV_W + 1],
        lru_w_a=tot["lru_w_a"][None], lru_b_a=tot["lru_b_a"].reshape(lru_b_a.shape),
        lru_w_x=tot["lru_w_x"][None], lru_b_x=tot["lru_b_x"].reshape(lru_b_x.shape),
        lru_lambda=tot["lru_lambda"], lru_norm=tot["lru_norm"], post_mix_norm=tot["post_mix_norm"],
        pre_mlp_norm=tot["pre_mlp_norm"], post_mlp_norm=tot["post_mlp_norm"])

    grads.update(large_grads)

    small_w = [n for n in WEIGHTS if n not in LARGE]
    n_sw = len(small_w)

    def flat(v):
        return v.reshape(-1, v.shape[-1])

    def adamw_all(*refs):
        ins, outs = refs[:4 * n_sw], refs[4 * n_sw:]
        for q in range(n_sw):
            res = _adamw(*[ins[4 * q + t][...] for t in range(4)])
            for t in range(3):
                outs[3 * q + t][...] = res[t]

    operands = [flat(d[n]) for n in small_w for d in (a, grads, {k: a["m_" + k] for k in small_w}, {k: a["v_" + k] for k in small_w})]
    outs = pl.pallas_call(
        adamw_all, name="adamw_small",
        out_shape=[jax.ShapeDtypeStruct(flat(a[n]).shape, F32) for n in small_w for _ in range(3)],
        in_specs=[pl.BlockSpec(memory_space=pltpu.VMEM)] * (4 * n_sw), out_specs=[pl.BlockSpec(memory_space=pltpu.VMEM)] * (3 * n_sw),
        compiler_params=pltpu.CompilerParams(vmem_limit_bytes=VMEM_LIMIT),
    )(*operands)
    for q, n in enumerate(small_w):
        delta[n], new_m[n], new_v[n] = [o.reshape(a[n].shape) for o in outs[3 * q:3 * q + 3]]

    return (loss, grad_x[None], *[grads[n] for n in WEIGHTS], *[delta[n] for n in WEIGHTS],
            *[new_m[n] for n in WEIGHTS], *[new_v[n] for n in WEIGHTS])
```

```python
import functools

import jax
import jax.numpy as jnp
from jax import lax
from jax.experimental import pallas as pl
from jax.experimental.pallas import tpu as pltpu

F32 = jnp.float32
BF16 = jnp.bfloat16
MESH = pl.DeviceIdType.MESH

EPS = 1e-6
LRU_C = 8.0
ADAM_LR = 0.001
ADAM_B1 = 0.9
ADAM_B2 = 0.999
ADAM_EPS = 1e-08
ADAM_WD = 0.01
ADAM_STEP = 10

N_GROUPS = 8
HEADS_PER_GROUP = 4
HEAD_DIM = 64
GROUP_W = HEADS_PER_GROUP * HEAD_DIM
STATE = 128
LRU_HEADS = 16
LRU_BLOCK = 128
CONV_W = 4
SSD_CHUNK = 256
HALO = 8

VMEM_LIMIT = 56 * 1024 * 1024


def _params(sem=None):
    return pltpu.CompilerParams(dimension_semantics=sem, vmem_limit_bytes=VMEM_LIMIT)


@jax.custom_jvp
def _log1p(x):
    u = 1.0 + x
    d = u - 1.0
    return jnp.where(d == 0.0, x, jnp.log(u) * (x / jnp.where(d == 0.0, 1.0, d)))


@_log1p.defjvp
def _log1p_jvp(primals, tangents):
    (x,), (t,) = primals, tangents
    return _log1p(x), t / (1.0 + x)


@jax.custom_jvp
def _expm1(x):
    u = jnp.exp(x)
    lu = jnp.log(u)
    safe = jnp.where(lu == 0.0, 1.0, lu)
    y = (u - 1.0) * (x / safe)
    y = jnp.where(lu == 0.0, x, y)
    return jnp.where(u == 0.0, -1.0, y)


@_expm1.defjvp
def _expm1_jvp(primals, tangents):
    (x,), (t,) = primals, tangents
    return _expm1(x), t * jnp.exp(x)


def _softplus(x):
    return jnp.maximum(x, 0.0) + _log1p(jnp.exp(-jnp.abs(x)))


def _sigmoid(x):
    return 1.0 / (1.0 + jnp.exp(-x))


def _silu(x):
    return x * _sigmoid(x)


def _gelu(x):
    c = 0.7978845608028654
    return 0.5 * x * (1.0 + jnp.tanh(c * (x + 0.044715 * (x * x * x))))


def _rms(x, g):
    return x * lax.rsqrt(jnp.mean(x * x, axis=-1, keepdims=True) + EPS) * g


def _dot(a, b, dims):
    return lax.dot_general(a.astype(BF16), b.astype(BF16), (dims, ((), ())), preferred_element_type=F32)


_NN = ((1,), (0,))
_NT = ((1,), (1,))
_TN = ((0,), (0,))


ANY = pl.BlockSpec(memory_space=pl.ANY)


class _Side:
    def __init__(self, ins, outs, aliases, n_sems, make):
        self.ins, self.outs, self.aliases, self.n_sems, self.make = ins, outs, aliases, n_sems, make


def _call(body, args, *, name, grid, in_specs, out_specs, out_shape, scratch_shapes=(), sem=None, side=None, into=None):
    in_specs, out_specs, out_shape, scratch_shapes = list(in_specs), list(out_specs), list(out_shape), list(scratch_shapes)
    held = []
    for oi, (buf, n_cols, off) in (into or {}).items():
        spec = out_specs[oi]
        out_shape[oi] = jax.ShapeDtypeStruct((out_shape[oi].shape[0], n_cols), out_shape[oi].dtype)
        out_specs[oi] = pl.BlockSpec(spec.block_shape, lambda *idx, _m=spec.index_map, _o=off: (_m(*idx)[0], _m(*idx)[1] + _o))
        if buf is not None:
            held.append((oi, buf))
    if side is None and not held:
        outs = pl.pallas_call(body, name=name, grid=grid, in_specs=in_specs, out_specs=out_specs, out_shape=out_shape,
                              scratch_shapes=scratch_shapes, compiler_params=_params(sem))(*args)
        return list(outs), []
    side_ins, side_outs = (list(side.ins), list(side.outs)) if side is not None else ([], [])
    n_in, n_out, n_scr, nh, si, so = len(in_specs), len(out_specs), len(scratch_shapes), len(held), len(side_ins), len(side_outs)

    def full(*refs):
        s_in = refs[n_in + nh:n_in + nh + si]
        o0 = n_in + nh + si
        s_out = refs[o0 + n_out:o0 + n_out + so]
        scr = refs[o0 + n_out + so:o0 + n_out + so + n_scr]
        if side is None:
            body(*refs[:n_in], *refs[o0:o0 + n_out], *scr)
            return
        send_sems, recv_sems = refs[-2], refs[-1]
        ids = [pl.program_id(d) for d in range(len(grid))]
        first = functools.reduce(jnp.logical_and, [i == 0 for i in ids])
        last = functools.reduce(jnp.logical_and, [i == g - 1 for i, g in zip(ids, grid)])

        @pl.when(first)
        def _():
            for d in side.make(s_in, s_out, send_sems, recv_sems)[0]:
                d.start()

        body(*refs[:n_in], *refs[o0:o0 + n_out], *scr)

        @pl.when(last)
        def _():
            sends, recvs = side.make(s_in, s_out, send_sems, recv_sems)
            for d in recvs:
                d.wait_recv()
            for d in sends:
                d.wait_send()

    aliases = {n_in + h: oi for h, (oi, _) in enumerate(held)}
    if side is not None:
        aliases.update({n_in + nh + i: n_out + o for i, o in side.aliases.items()})
        scratch_shapes = scratch_shapes + [pltpu.SemaphoreType.DMA((side.n_sems,)), pltpu.SemaphoreType.DMA((side.n_sems,))]
    outs = pl.pallas_call(
        full, name=name, grid=grid, in_specs=in_specs + [ANY] * (nh + si), out_specs=out_specs + [ANY] * so,
        out_shape=out_shape + side_outs, scratch_shapes=scratch_shapes, input_output_aliases=aliases,
        compiler_params=_params(("arbitrary",) * len(grid) if side is not None else sem),
    )(*args, *[b for _, b in held], *side_ins)
    return list(outs[:n_out]), list(outs[n_out:])


def _matmul(a, b, *, mode, m, n, k, tm, tn, tk, out_dtypes, name, a_spec=None, b_spec=None,
            out_specs=None, out_shapes=None, extras=(), epilogue=None, side=None):
    tm, tn, tk = min(tm, m), min(tn, n), min(tk, k)
    assert m % tm == 0 and n % tn == 0 and k % tk == 0, (name, m, n, k, tm, tn, tk)
    nk = k // tk
    dims = {"nn": _NN, "nt": _NT, "tn": _TN}[mode]
    if a_spec is None:
        a_spec = pl.BlockSpec((tk, tm), lambda i, j, kk: (kk, i)) if mode == "tn" else pl.BlockSpec((tm, tk), lambda i, j, kk: (i, kk))
    if b_spec is None:
        b_spec = pl.BlockSpec((tn, tk), lambda i, j, kk: (j, kk)) if mode == "nt" else pl.BlockSpec((tk, tn), lambda i, j, kk: (kk, j))
    tile = pl.BlockSpec((tm, tn), lambda i, j, kk: (i, j))
    if out_specs is None:
        out_specs = [tile for _ in out_dtypes]
    if out_shapes is None:
        out_shapes = [jax.ShapeDtypeStruct((m, n), d) for d in out_dtypes]
    n_ex, n_out = len(extras), len(out_dtypes)

    def body(*refs):
        a_ref, b_ref = refs[0], refs[1]
        ex_refs = refs[2:2 + n_ex]
        o_refs = refs[2 + n_ex:2 + n_ex + n_out]
        def finish(r):
            outs = epilogue(r, *[e[...] for e in ex_refs]) if epilogue is not None else (r,)
            for o_ref, o in zip(o_refs, outs):
                o_ref[...] = o.astype(o_ref.dtype)

        if nk == 1:
            finish(_dot(a_ref[...], b_ref[...], dims))
            return
        acc = refs[-1]
        kk = pl.program_id(2)

        @pl.when(kk == 0)
        def _():
            acc[...] = _dot(a_ref[...], b_ref[...], dims)

        @pl.when(kk > 0)
        def _():
            acc[...] += _dot(a_ref[...], b_ref[...], dims)

        @pl.when(kk == nk - 1)
        def _():
            finish(acc[...])

    outs, side_outs = _call(
        body, (a, b, *extras), name=name, grid=(m // tm, n // tn, nk),
        in_specs=[a_spec, b_spec] + [tile for _ in extras], out_specs=out_specs, out_shape=out_shapes,
        scratch_shapes=[] if nk == 1 else [pltpu.VMEM((tm, tn), F32)], sem=("parallel", "parallel", "arbitrary"), side=side)
    return outs if side is None else (outs, side_outs)


def _rowwise(fn, rows, bcast, out_rows, out_acc, *, name, tr, sub, side=None, into=None):
    rows = [r if isinstance(r, tuple) else (r, 0, r.shape[1]) for r in rows]
    row_specs = []
    for arr, c0, w in rows:
        assert c0 % w == 0, (name, c0, w)
        row_specs.append((w, c0 // w))
    rows = [r[0] for r in rows]
    L = rows[0].shape[0]
    tr = min(tr, L)
    sub = min(sub, tr)
    assert L % tr == 0 and tr % sub == 0, (name, L, tr, sub)
    n_r, n_b, n_or, n_oa = len(rows), len(bcast), len(out_rows), len(out_acc)

    def body(*refs):
        r_refs = refs[:n_r]
        b_refs = refs[n_r:n_r + n_b]
        or_refs = refs[n_r + n_b:n_r + n_b + n_or]
        oa_refs = refs[n_r + n_b + n_or:]
        i = pl.program_id(0)

        @pl.when(i == 0)
        def _():
            for o in oa_refs:
                o[...] = jnp.zeros_like(o)

        bvals = [b[...] for b in b_refs]

        def step(s, carry):
            r0 = pl.multiple_of(s * sub, sub)
            tiles = [r[pl.ds(r0, sub), :] for r in r_refs]
            outs = fn(*tiles, *bvals)
            for o_ref, o in zip(or_refs, outs[:n_or]):
                o_ref[pl.ds(r0, sub), :] = o.astype(o_ref.dtype)
            for o_ref, o in zip(oa_refs, outs[n_or:]):
                o_ref[...] += o
            return carry

        if tr == sub:
            step(0, 0)
        else:
            lax.fori_loop(0, tr // sub, step, 0, unroll=8)

    def whole(shape):
        nd = len(shape)
        return pl.BlockSpec(shape, lambda i, _n=nd: (0,) * _n)

    outs, side_outs = _call(
        body, (*rows, *bcast), name=name, grid=(L // tr,),
        in_specs=[pl.BlockSpec((tr, w), lambda i, _c=cb: (i, _c)) for w, cb in row_specs] + [whole(b.shape) for b in bcast],
        out_specs=[pl.BlockSpec((tr, c), lambda i: (i, 0)) for c, _ in out_rows] + [whole(s) for s in out_acc],
        out_shape=[jax.ShapeDtypeStruct((L, c), d) for c, d in out_rows] + [jax.ShapeDtypeStruct(s, F32) for s in out_acc],
        sem=("arbitrary",), side=side, into=into)
    return outs if side is None else (outs, side_outs)


def _colsum(x):
    return jnp.sum(x, axis=0, keepdims=True)


def _grouped(fn, rows, params, out_rows, out_acc, *, gw, name, tr, side=None, into=None):
    rows = [r if isinstance(r, tuple) else (r, 0) for r in rows]
    L = rows[0][0].shape[0]
    tr = min(tr, L)
    assert L % tr == 0
    G = None
    for p in params:
        G = p.shape[0] if p.ndim == 3 else p.shape[1] // gw
    cw = G * gw
    n_r, n_p, n_or, n_oa = len(rows), len(params), len(out_rows), len(out_acc)

    def pick(ref, g):
        return ref[g] if len(ref.shape) == 3 else ref[:, g * gw:(g + 1) * gw]

    def body(*refs):
        r_refs = refs[:n_r]
        p_refs = refs[n_r:n_r + n_p]
        or_refs = refs[n_r + n_p:n_r + n_p + n_or]
        oa_refs = refs[n_r + n_p + n_or:]

        @pl.when(pl.program_id(0) == 0)
        def _():
            for o in oa_refs:
                o[...] = jnp.zeros_like(o)

        for g in range(G):
            outs = fn(*[pick(r, g) for r in r_refs], *[pick(p, g) for p in p_refs])
            for o_ref, o in zip(or_refs, outs[:n_or]):
                o_ref[:, g * gw:(g + 1) * gw] = o.astype(o_ref.dtype)
            for o_ref, o in zip(oa_refs, outs[n_or:]):
                if len(o_ref.shape) == 3:
                    o_ref[g] += o
                else:
                    o_ref[:, g * gw:(g + 1) * gw] += o

    def whole(shape):
        nd = len(shape)
        return pl.BlockSpec(shape, lambda i, _n=nd: (0,) * _n)

    for _, c0 in rows:
        assert c0 % cw == 0
    outs, side_outs = _call(
        body, (*[r[0] for r in rows], *params), name=name, grid=(L // tr,),
        in_specs=[pl.BlockSpec((tr, cw), lambda i, _c=c0 // cw: (i, _c)) for _, c0 in rows] + [whole(p.shape) for p in params],
        out_specs=[pl.BlockSpec((tr, cw), lambda i: (i, 0)) for _ in out_rows] + [whole(s) for s in out_acc],
        out_shape=[jax.ShapeDtypeStruct((L, cw), d) for d in out_rows] + [jax.ShapeDtypeStruct(s, F32) for s in out_acc],
        sem=("arbitrary",), side=side, into=into)
    return outs if side is None else (outs, side_outs)


def _dsilu(p):
    s = _sigmoid(p)
    return s + p * s * (1.0 - s)


CONV_RC, CONV_CC = 32, 512


def _past_window(x_ref, head, r0, k, cs):
    if r0 == 0:
        return head[HALO - 3 + k:HALO - 3 + k + CONV_RC, cs]
    return x_ref[r0 - 3 + k:r0 - 3 + k + CONV_RC, cs]


def _conv_fwd(x, c0, C, w, b, *, silu, name, tr=512, cb=1024, side=None):
    L = x.shape[0]
    tr = min(tr, L)
    nb, hb = L // tr, tr // HALO
    assert L % tr == 0 and C % cb == 0 and c0 % cb == 0 and tr % CONV_RC == 0 and cb % CONV_CC == 0
    n_out = 2 if silu else 1

    def body(x_ref, h_ref, w_ref, b_ref, *rest):
        o_refs, head = rest[:n_out], rest[n_out]
        head[0:HALO, :] = h_ref[...]

        @pl.when(pl.program_id(0) == 0)
        def _():
            head[0:HALO, :] = jnp.zeros((HALO, cb), F32)

        head[HALO:, :] = x_ref[0:CONV_RC, :]
        for cc in range(cb // CONV_CC):
            cs = slice(cc * CONV_CC, (cc + 1) * CONV_CC)
            wv, bv = w_ref[:, cs], b_ref[:, cs]
            for r0 in range(0, tr, CONV_RC):
                y = bv
                for k in range(CONV_W):
                    y = y + wv[k:k + 1, :] * _past_window(x_ref, head, r0, k, cs)
                o_refs[0][r0:r0 + CONV_RC, cs] = y
                if silu:
                    o_refs[1][r0:r0 + CONV_RC, cs] = _silu(y)

    outs, side_outs = _call(
        body, (x, x, w, b), name=name, grid=(nb, C // cb),
        in_specs=[
            pl.BlockSpec((tr, cb), lambda i, j: (i, c0 // cb + j)),
            pl.BlockSpec((HALO, cb), lambda i, j: (jnp.maximum(i * hb - 1, 0), c0 // cb + j)),
            pl.BlockSpec((CONV_W, cb), lambda i, j: (0, j)),
            pl.BlockSpec((1, cb), lambda i, j: (0, j)),
        ],
        out_specs=[pl.BlockSpec((tr, cb), lambda i, j: (i, j)) for _ in range(n_out)],
        out_shape=[jax.ShapeDtypeStruct((L, C), F32) for _ in range(n_out)],
        scratch_shapes=[pltpu.VMEM((HALO + CONV_RC, cb), F32)], sem=("parallel", "parallel"), side=side)
    return outs if side is None else (outs, side_outs)


def _fold8(v):
    return (v[0:8] + v[8:16]) + (v[16:24] + v[24:32])


def _conv_bwd(dact, dc0, pre, x, xc0, C, w, *, silu, name, tr=512, cb=1024, side=None, into=None):
    L = x.shape[0]
    tr = min(tr, L)
    nb, hb = L // tr, tr // HALO
    last_h = L // HALO - 1
    assert L % tr == 0 and C % cb == 0 and tr % CONV_RC == 0 and cb % CONV_CC == 0

    def body(*refs):
        if silu:
            d_ref, dh_ref, p_ref, ph_ref, x_ref, xh_ref, w_ref, dx_ref, dwb_ref, dp, head = refs
        else:
            d_ref, dh_ref, x_ref, xh_ref, w_ref, dx_ref, dwb_ref, dp, head = refs
        i = pl.program_id(1)
        dp[tr:, :] = dh_ref[...] * _dsilu(ph_ref[...]) if silu else dh_ref[...]

        @pl.when(i == nb - 1)
        def _():
            dp[tr:, :] = jnp.zeros((HALO, cb), F32)

        head[0:HALO, :] = xh_ref[...]

        @pl.when(i == 0)
        def _():
            head[0:HALO, :] = jnp.zeros((HALO, cb), F32)
            dwb_ref[...] = jnp.zeros_like(dwb_ref)

        head[HALO:, :] = x_ref[0:CONV_RC, :]
        for cc in range(cb // CONV_CC):
            cs = slice(cc * CONV_CC, (cc + 1) * CONV_CC)
            for r0 in range(0, tr, CONV_RC):
                rs = slice(r0, r0 + CONV_RC)
                dp[rs, cs] = d_ref[rs, cs] * _dsilu(p_ref[rs, cs]) if silu else d_ref[rs, cs]
        for cc in range(cb // CONV_CC):
            cs = slice(cc * CONV_CC, (cc + 1) * CONV_CC)
            wv = w_ref[:, cs]
            acc = [jnp.zeros((8, CONV_CC), F32) for _ in range(CONV_W + 1)]
            for r0 in range(0, tr, CONV_RC):
                dm = dp[r0:r0 + CONV_RC, cs]
                dx = jnp.zeros((CONV_RC, CONV_CC), F32)
                for k in range(CONV_W):
                    dx = dx + wv[k:k + 1, :] * dp[r0 + 3 - k:r0 + 3 - k + CONV_RC, cs]
                    acc[k] = acc[k] + _fold8(dm * _past_window(x_ref, head, r0, k, cs))
                acc[CONV_W] = acc[CONV_W] + _fold8(dm)
                dx_ref[r0:r0 + CONV_RC, cs] = dx.astype(dx_ref.dtype)
            for k in range(CONV_W + 1):
                dwb_ref[k:k + 1, cs] += _colsum(acc[k])

    def main(c):
        return pl.BlockSpec((tr, cb), lambda j, i: (i, c // cb + j))

    def nxt(c):
        return pl.BlockSpec((HALO, cb), lambda j, i: (jnp.minimum((i + 1) * hb, last_h), c // cb + j))

    in_specs = [main(dc0), nxt(dc0)]
    args = [dact, dact]
    if silu:
        in_specs += [main(0), nxt(0)]
        args += [pre, pre]
    in_specs += [main(xc0), pl.BlockSpec((HALO, cb), lambda j, i: (jnp.maximum(i * hb - 1, 0), xc0 // cb + j)),
                 pl.BlockSpec((CONV_W, cb), lambda j, i: (0, j))]
    args += [x, x, w]
    outs, side_outs = _call(
        body, args, name=name, grid=(C // cb, nb), in_specs=in_specs,
        out_specs=[pl.BlockSpec((tr, cb), lambda j, i: (i, j)), pl.BlockSpec((8, cb), lambda j, i: (0, j))],
        out_shape=[jax.ShapeDtypeStruct((L, C), BF16), jax.ShapeDtypeStruct((8, C), F32)],
        scratch_shapes=[pltpu.VMEM((tr + HALO, cb), F32), pltpu.VMEM((HALO + CONV_RC, cb), F32)],
        sem=("parallel", "arbitrary"), side=side, into=into)
    return outs if side is None else (outs, side_outs)


def _ssd_f1(dtraw, dtb, alog):
    q = dtraw.shape[0]
    dt = _softplus(dtraw + dtb)
    adt = dt * (-jnp.exp(alog))
    tril = (lax.broadcasted_iota(jnp.int32, (q, q), 0) >= lax.broadcasted_iota(jnp.int32, (q, q), 1)).astype(F32)
    acs = lax.dot_general(tril, adt, (_NN, ((), ())), precision=lax.Precision.HIGHEST, preferred_element_type=F32)
    return dt, acs


def _ssd_group(g, x, bm, cm, dt, acs, drow, hp):
    q = x.shape[0]
    lane = lax.broadcasted_iota(jnp.int32, (1, 128), 1)
    sub = lax.broadcasted_iota(jnp.int32, (128, 1), 0)
    head_of = lax.broadcasted_iota(jnp.int32, (1, GROUP_W), 1) // HEAD_DIM
    is_last = (lax.broadcasted_iota(jnp.int32, (q, 1), 0) == q - 1).astype(F32)
    causal = lax.broadcasted_iota(jnp.int32, (q, q), 0) >= lax.broadcasted_iota(jnp.int32, (q, q), 1)
    acs_end = jnp.sum(acs * is_last, axis=0, keepdims=True)
    acs_t = acs.T
    dt_exp = jnp.zeros((q, GROUP_W), F32)
    acs_exp = jnp.zeros((q, GROUP_W), F32)
    end_exp = jnp.zeros((1, GROUP_W), F32)
    d_exp = jnp.zeros((1, GROUP_W), F32)
    heads = []
    for k in range(HEADS_PER_GROUP):
        h = HEADS_PER_GROUP * g + k
        oh = (lane == h).astype(F32)
        mk = (head_of == k).astype(F32)
        acs_col = jnp.sum(acs * oh, axis=1, keepdims=True)
        acs_row = jnp.sum(acs_t * (sub == h).astype(F32), axis=0, keepdims=True)
        dt_exp = dt_exp + jnp.sum(dt * oh, axis=1, keepdims=True) * mk
        acs_exp = acs_exp + acs_col * mk
        end_exp = end_exp + jnp.sum(acs_end * oh, axis=1, keepdims=True) * mk
        d_exp = d_exp + jnp.sum(drow * oh, axis=1, keepdims=True) * mk
        heads.append((acs_col, acs_row, mk))
    xdt = x * dt_exp
    states = _dot(bm, xdt * jnp.exp(end_exp - acs_exp), _TN)
    y = _dot(cm, hp, _NN) * jnp.exp(acs_exp) + x * d_exp
    scores = _dot(cm, bm, _NT)
    for acs_col, acs_row, mk in heads:
        dec = jnp.exp(jnp.where(causal, acs_col - acs_row, -jnp.inf))
        y = y + _dot(scores * dec, xdt * mk, _NN)
    return y, hp * jnp.exp(end_exp) + states


def _ssd_fwd(act, dtraw, dtb, alog, drow, *, name, side=None):
    L = act.shape[0]
    q = min(SSD_CHUNK, L)
    nc = L // q
    d_ssd = N_GROUPS * GROUP_W

    def body(act_ref, dt_ref, dtb_ref, alog_ref, drow_ref, y_ref, hst_ref, h):
        @pl.when(pl.program_id(0) == 0)
        def _():
            h[...] = jnp.zeros_like(h)

        dt, acs = _ssd_f1(dt_ref[...], dtb_ref[...], alog_ref[...])
        drow = drow_ref[...]
        for g in range(N_GROUPS):
            hp = h[g]
            hst_ref[0, g] = hp
            y, hn = _ssd_group(g, act_ref[:, g * GROUP_W:(g + 1) * GROUP_W],
                               act_ref[:, d_ssd + g * STATE:d_ssd + (g + 1) * STATE],
                               act_ref[:, d_ssd + (N_GROUPS + g) * STATE:d_ssd + (N_GROUPS + g + 1) * STATE],
                               dt, acs, drow, hp)
            y_ref[:, g * GROUP_W:(g + 1) * GROUP_W] = y
            h[g] = hn

    row = pl.BlockSpec((1, 128), lambda i: (0, 0))
    outs, side_outs = _call(
        body, (act, dtraw, dtb, alog, drow), name=name, grid=(nc,),
        in_specs=[pl.BlockSpec((q, act.shape[1]), lambda i: (i, 0)), pl.BlockSpec((q, 128), lambda i: (i, 0)), row, row, row],
        out_specs=[pl.BlockSpec((q, d_ssd), lambda i: (i, 0)),
                   pl.BlockSpec((1, N_GROUPS, STATE, GROUP_W), lambda i: (i, 0, 0, 0))],
        out_shape=[jax.ShapeDtypeStruct((L, d_ssd), F32), jax.ShapeDtypeStruct((nc, N_GROUPS, STATE, GROUP_W), F32)],
        scratch_shapes=[pltpu.VMEM((N_GROUPS, STATE, GROUP_W), F32)], sem=("arbitrary",), side=side)
    return outs if side is None else (outs, side_outs)


def _ssd_bwd(act, dtraw, dy, hst, dtb, alog, drow, *, name, side=None):
    L = act.shape[0]
    q = min(SSD_CHUNK, L)
    nc = L // q
    d_ssd = N_GROUPS * GROUP_W

    def body(act_ref, dt_ref, dy_ref, hst_ref, dtb_ref, alog_ref, drow_ref, dact_ref, ddt_ref, dpar_ref, dh):
        @pl.when(pl.program_id(0) == 0)
        def _():
            dh[...] = jnp.zeros_like(dh)
            dpar_ref[...] = jnp.zeros_like(dpar_ref)

        (dt, acs), f1_vjp = jax.vjp(_ssd_f1, dt_ref[...], dtb_ref[...], alog_ref[...])
        drow = drow_ref[...]
        ddt = jnp.zeros_like(dt)
        dacs = jnp.zeros_like(acs)
        ddrow = jnp.zeros_like(drow)
        for g in range(N_GROUPS):
            xs = slice(g * GROUP_W, (g + 1) * GROUP_W)
            bs = slice(d_ssd + g * STATE, d_ssd + (g + 1) * STATE)
            cs = slice(d_ssd + (N_GROUPS + g) * STATE, d_ssd + (N_GROUPS + g + 1) * STATE)
            _, f2_vjp = jax.vjp(functools.partial(_ssd_group, g), act_ref[:, xs], act_ref[:, bs], act_ref[:, cs],
                                dt, acs, drow, hst_ref[0, g])
            dx, dbm, dcm, ddt_g, dacs_g, ddrow_g, dhp = f2_vjp((dy_ref[:, xs], dh[g]))
            dact_ref[:, xs] = dx
            dact_ref[:, bs] = dbm
            dact_ref[:, cs] = dcm
            dh[g] = dhp
            ddt, dacs, ddrow = ddt + ddt_g, dacs + dacs_g, ddrow + ddrow_g
        ddtraw, ddtb, dalog = f1_vjp((ddt, dacs))
        ddt_ref[...] = ddtraw
        dpar_ref[0:1, :] += ddtb
        dpar_ref[1:2, :] += dalog
        dpar_ref[2:3, :] += ddrow

    row = pl.BlockSpec((1, 128), lambda i: (0, 0))
    rev = lambda i: (nc - 1 - i, 0)
    outs, side_outs = _call(
        body, (act, dtraw, dy, hst, dtb, alog, drow), name=name, grid=(nc,),
        in_specs=[pl.BlockSpec((q, act.shape[1]), rev), pl.BlockSpec((q, 128), rev), pl.BlockSpec((q, d_ssd), rev),
                  pl.BlockSpec((1, N_GROUPS, STATE, GROUP_W), lambda i: (nc - 1 - i, 0, 0, 0)), row, row, row],
        out_specs=[pl.BlockSpec((q, act.shape[1]), rev), pl.BlockSpec((q, 128), rev), pl.BlockSpec((8, 128), lambda i: (0, 0))],
        out_shape=[jax.ShapeDtypeStruct(act.shape, F32), jax.ShapeDtypeStruct((L, 128), F32), jax.ShapeDtypeStruct((8, 128), F32)],
        scratch_shapes=[pltpu.VMEM((N_GROUPS, STATE, GROUP_W), F32)], sem=("arbitrary",), side=side)
    return outs if side is None else (outs, side_outs)


def _gate_head(xl, wa, ba, wx, bx, lam):
    r = _sigmoid(_dot(xl, wa, _NN) + ba)
    i = _sigmoid(_dot(xl, wx, _NN) + bx)
    log_a = -LRU_C * r * _softplus(-lam)
    return jnp.exp(log_a), jnp.sqrt(-_expm1(2.0 * log_a)) * (i * xl)


def _gate_head_bwd(xl, da, du, wa, ba, wx, bx, lam):
    _, vjp = jax.vjp(_gate_head, xl, wa, ba, wx, bx, lam)
    dxl, dwa, dba, dwx, dbx, dlam = vjp((da, du))
    return dxl, dwa, dba, dwx, dbx, dlam


def _scan_tile(a, b, rows, reverse):
    for d in (1, 2, 4):
        if reverse:
            keep = rows < 8 - d
            a_sh, b_sh = pltpu.roll(a, 8 - d, 0), pltpu.roll(b, 8 - d, 0)
        else:
            keep = rows >= d
            a_sh, b_sh = pltpu.roll(a, d, 0), pltpu.roll(b, d, 0)
        b = b + a * jnp.where(keep, b_sh, 0.0)
        a = a * jnp.where(keep, a_sh, 1.0)
    return a, b


def _lru_scan_fwd(a, u, *, name, tr=512, cb=1024):
    L, C = a.shape
    tr, cb = min(tr, L), min(cb, C)

    def body(a_ref, u_ref, h_ref, hp_ref, carry):
        @pl.when(pl.program_id(1) == 0)
        def _():
            carry[...] = jnp.zeros_like(carry)

        rows = lax.broadcasted_iota(jnp.int32, (8, cb), 0)

        def tile(t, hc):
            r0 = pl.multiple_of(t * 8, 8)
            pa, hb = _scan_tile(a_ref[pl.ds(r0, 8), :], u_ref[pl.ds(r0, 8), :], rows, False)
            h = hb + pa * hc
            h_ref[pl.ds(r0, 8), :] = h
            hp_ref[pl.ds(r0, 8), :] = jnp.where(rows >= 1, pltpu.roll(h, 1, 0), hc)
            return h[7:8, :]

        carry[...] = lax.fori_loop(0, tr // 8, tile, carry[...], unroll=4)

    blk = pl.BlockSpec((tr, cb), lambda j, i: (i, j))
    return pl.pallas_call(
        body, name=name, grid=(C // cb, L // tr),
        in_specs=[blk, blk], out_specs=[blk, blk],
        out_shape=[jax.ShapeDtypeStruct((L, C), F32), jax.ShapeDtypeStruct((L, C), F32)],
        scratch_shapes=[pltpu.VMEM((1, cb), F32)],
        compiler_params=_params(("parallel", "arbitrary")),
    )(a, u)


def _lru_scan_bwd(a, hprev, dh, *, name, tr=512, cb=1024):
    L, C = a.shape
    tr, cb = min(tr, L), min(cb, C)
    nb = L // tr

    def body(a_ref, hp_ref, dh_ref, da_ref, du_ref, carry):
        @pl.when(pl.program_id(1) == 0)
        def _():
            carry[...] = jnp.zeros_like(carry)

        rows = lax.broadcasted_iota(jnp.int32, (8, cb), 0)

        def tile(t, gc):
            r0 = pl.multiple_of((tr // 8 - 1 - t) * 8, 8)
            av, dv = a_ref[pl.ds(r0, 8), :], dh_ref[pl.ds(r0, 8), :]
            pa, gb = _scan_tile(av, av * dv, rows, True)
            big = gb + pa * gc
            g = dv + jnp.where(rows < 7, pltpu.roll(big, 7, 0), gc)
            du_ref[pl.ds(r0, 8), :] = g
            da_ref[pl.ds(r0, 8), :] = g * hp_ref[pl.ds(r0, 8), :]
            return big[0:1, :]

        carry[...] = lax.fori_loop(0, tr // 8, tile, carry[...], unroll=4)

    blk = pl.BlockSpec((tr, cb), lambda j, i: (nb - 1 - i, j))
    return pl.pallas_call(
        body, name=name, grid=(C // cb, nb),
        in_specs=[blk, blk, blk], out_specs=[blk, blk],
        out_shape=[jax.ShapeDtypeStruct((L, C), F32), jax.ShapeDtypeStruct((L, C), F32)],
        scratch_shapes=[pltpu.VMEM((1, cb), F32)],
        compiler_params=_params(("parallel", "arbitrary")),
    )(a, hprev, dh)


def _ssd_gate(y, z, n):
    v = y * _silu(z)
    return v * lax.rsqrt(jnp.mean(v * v, axis=-1, keepdims=True) + EPS) * n


def _ssd_gate_bwd(y, z, dy, n):
    _, vjp = jax.vjp(_ssd_gate, y, z, n)
    return vjp(dy.astype(F32))


def _lru_out(hl, gate, n):
    return _rms(hl * _gelu(gate), n)


def _lru_out_bwd(hl, gate, dy, n):
    _, vjp = jax.vjp(_lru_out, hl, gate, n)
    return vjp(dy.astype(F32))


def _mid(x, mix, pm, pmlp):
    x1 = x + _rms(mix, pm)
    return x1, _rms(x1, pmlp)


def _mid_bwd(x, mix, dx1p, dh2, pm, pmlp):
    _, vjp = jax.vjp(_mid, x, mix, pm, pmlp)
    dx, dmix, dpm, dpmlp = vjp((dx1p, dh2))
    return dmix, dx, dpm, dpmlp


def _loss_bwd(hm2, x1, tgt, g):
    def lossf(hm2, x1, g):
        e = x1 + _rms(hm2, g) - tgt
        return 0.5 * jnp.sum(jnp.mean(e * e, axis=-1, keepdims=True), axis=0, keepdims=True)

    val, vjp = jax.vjp(lossf, hm2, x1, g)
    dhm2, dx1, dg = vjp(jnp.ones((1, 1), F32))
    return dhm2, dx1, dg, val * jnp.ones((1, 128), F32)


def _in_bwd(x, dh_a, dh_b, dx1, g):
    _, vjp = jax.vjp(_rms, x, g)
    dx, dg = vjp(dh_a + dh_b)
    return dx + dx1, dg


def _adamw(w, g, m, v):
    m = ADAM_B1 * m + (1.0 - ADAM_B1) * g
    v = ADAM_B2 * v + (1.0 - ADAM_B2) * (g * g)
    m_hat = m / (1.0 - ADAM_B1 ** ADAM_STEP)
    v_hat = v / (1.0 - ADAM_B2 ** ADAM_STEP)
    return -ADAM_LR * (m_hat / (jnp.sqrt(v_hat) + ADAM_EPS) + ADAM_WD * w), m, v


class _LocalPlan:
    def __init__(self, p):
        self.p, self.large = p, {}

    def weight(self, name):
        return self.p[name]

    def side(self, kernel_name):
        return None

    def done(self, kernel_name, side_outs):
        pass

    def grad(self, name, g):
        self.large[name] = g

    def small(self, grads):
        pass


def _local_step(x, tgt, p, plan):
    L, D = x.shape

    def carry(fn, *args, name, **kw):
        side = plan.side(name)
        if side is None:
            return fn(*args, name=name, **kw)
        outs, side_outs = fn(*args, name=name, side=side, **kw)
        plan.done(name, side_outs)
        return outs

    d_ssd, d_xbc, d_lru, d_mix, d_ff = 2048, 4096, 2048, 4096, 8192
    n_main = d_ssd + d_xbc + 2 * d_lru
    c_xbc, c_gate, c_xl = d_ssd, d_ssd + d_xbc, d_ssd + d_xbc + d_lru
    TR, TRW, SUB = 256, 512, 32
    mm = dict(tm=1024, tn=1024, tk=2048)
    mmt = dict(tm=2048, tn=1024, tk=2048)

    (h,) = _rowwise(lambda xt, g: (_rms(xt, g),), [x], [p["pre_mix_norm"]], [(D, BF16)], [], name="f_prenorm", tr=TRW, sub=SUB)
    w_main, w_dt = plan.weight("w_main"), plan.weight("w_dt")
    (proj,) = carry(_matmul, h, w_main, mode="nn", m=L, n=n_main, k=D, out_dtypes=[F32], name="f_inproj", **mmt)
    (dtraw,) = _matmul(h, w_dt, mode="nn", m=L, n=128, k=D, out_dtypes=[F32], name="f_dtproj", **mm)
    pre, act = carry(_conv_fwd, proj, c_xbc, d_xbc, p["ssd_conv_w"], p["ssd_conv_b"], silu=True, name="f_ssdconv")
    (xl,) = carry(_conv_fwd, proj, c_xl, d_lru, p["lru_conv_w"], p["lru_conv_b"], silu=False, name="f_lruconv")
    yraw, hst = carry(_ssd_fwd, act, dtraw, p["dtb"], p["alog"], p["drow"], name="f_ssd")
    (ycat,) = _grouped(lambda y, z, n: (_ssd_gate(y, z, n),), [yraw, (proj, 0)], [p["ssd_norm"]], [BF16], [],
                       gw=GROUP_W, name="f_ssdgate", tr=TR, into={0: (None, d_mix, 0)})
    gate_p = [p["lru_w_a"], p["lru_b_a"], p["lru_w_x"], p["lru_b_x"], p["lru_lambda"]]
    a, u = carry(_grouped, _gate_head, [xl], gate_p, [F32, F32], [], gw=LRU_BLOCK, name="f_lrugates", tr=TR)
    hl, hprev = _lru_scan_fwd(a, u, name="f_lruscan")
    (ycat,) = _rowwise(lambda ht, gt, n: (_lru_out(ht, gt, n),), [hl, (proj, c_gate, d_lru)], [p["lru_norm"]],
                       [(d_lru, BF16)], [], name="f_lruout", tr=TRW, sub=SUB, into={0: (ycat, d_mix, d_ssd // d_lru)})
    w_out = plan.weight("w_out")
    (mix,) = carry(_matmul, ycat, w_out, mode="nn", m=L, n=D, k=d_mix, out_dtypes=[F32], name="f_outproj", **mm)
    x1, h2 = _rowwise(_mid, [x, mix], [p["post_mix_norm"], p["pre_mlp_norm"]], [(D, F32), (D, BF16)], [],
                      name="f_mid", tr=TRW, sub=SUB)
    nb_mi = (d_ff // 4) // mm["tn"]
    w_mi = plan.weight("w_mi")
    hm, act2 = carry(_matmul, h2, w_mi, mode="nn", m=L, n=d_ff, k=D, out_dtypes=[BF16, BF16], name="f_mlpin",
                     b_spec=pl.BlockSpec((None, mm["tk"], mm["tn"]), lambda i, j, kk: (j // nb_mi, kk, j % nb_mi)),
                     epilogue=lambda r: (r, jnp.square(jnp.maximum(r, 0.0))), **mmt)
    w_mo = plan.weight("w_mo")
    (hm2,) = _matmul(act2, w_mo, mode="nn", m=L, n=D, k=d_ff, out_dtypes=[F32], name="f_mlpout", tm=2048, tn=1024, tk=1024)

    dhm2, dx1p, d_post_mlp, loss = _rowwise(_loss_bwd, [hm2, x1, tgt], [p["post_mlp_norm"]], [(D, BF16), (D, F32)],
                                            [(1, D), (1, 128)], name="b_loss", tr=TR, sub=SUB)
    (dhm,) = _matmul(dhm2, w_mo, mode="nt", m=L, n=d_ff, k=D, out_dtypes=[BF16], name="b_mlpout_dx", extras=[hm],
                     epilogue=lambda r, hmv: (r * (2.0 * jnp.maximum(hmv.astype(F32), 0.0)),), **mmt)
    (dw_mo,) = _matmul(act2, dhm2, mode="tn", m=d_ff, n=D, k=L, out_dtypes=[BF16], name="b_mlpout_dw", **mmt)
    plan.grad("w_mlp_out", dw_mo.reshape(4, -1, D))
    kb_mi = (d_ff // 4) // mm["tk"]
    (dh2,) = carry(_matmul, dhm, w_mi, mode="nt", m=L, n=D, k=d_ff, out_dtypes=[F32], name="b_mlpin_dx",
                   b_spec=pl.BlockSpec((None, mm["tn"], mm["tk"]), lambda i, j, kk: (kk // kb_mi, j, kk % kb_mi)), **mm)
    (dw_mi,) = carry(_matmul, h2, dhm, mode="tn", m=D, n=d_ff, k=L, out_dtypes=[BF16], name="b_mlpin_dw",
                     out_specs=[pl.BlockSpec((None, mmt["tm"], mmt["tn"]), lambda i, j, kk: (j // nb_mi, i, j % nb_mi))],
                     out_shapes=[jax.ShapeDtypeStruct((4, D, d_ff // 4), BF16)], **mmt)
    plan.grad("w_mlp_in", dw_mi)
    dmix, dx1, d_post_mix, d_pre_mlp = carry(_rowwise, _mid_bwd, [x, mix, dx1p, dh2], [p["post_mix_norm"], p["pre_mlp_norm"]],
                                             [(D, BF16), (D, F32)], [(1, D), (1, D)], name="b_mid", tr=TR, sub=SUB)
    (dw_out,) = carry(_matmul, ycat, dmix, mode="tn", m=d_mix, n=D, k=L, out_dtypes=[BF16], name="b_outproj_dw", **mmt)
    plan.grad("w_out", dw_out.reshape(4, -1, D))
    (dycat,) = carry(_matmul, dmix, w_out, mode="nt", m=L, n=d_mix, k=D, out_dtypes=[BF16], name="b_outproj_dx", **mmt)
    dhl, dproj, d_lru_norm = _rowwise(_lru_out_bwd, [hl, (proj, c_gate, d_lru), (dycat, d_ssd, d_lru)], [p["lru_norm"]],
                                      [(d_lru, F32), (d_lru, BF16)], [(1, d_lru)], name="b_lruout", tr=TR, sub=SUB,
                                      into={1: (None, n_main, c_gate // d_lru)})
    da, du = _lru_scan_bwd(a, hprev, dhl, name="b_lruscan")
    dxl, d_wa, d_ba, d_wx, d_bx, d_lam = _grouped(
        _gate_head_bwd, [xl, da, du], gate_p, [F32],
        [(LRU_HEADS, LRU_BLOCK, LRU_BLOCK), (1, d_lru), (LRU_HEADS, LRU_BLOCK, LRU_BLOCK), (1, d_lru), (1, d_lru)],
        gw=LRU_BLOCK, name="b_lrugates", tr=TR)
    CB = 1024
    dproj, dwb_lru = _conv_bwd(dxl, 0, None, proj, c_xl, d_lru, p["lru_conv_w"], silu=False, name="b_lruconv", cb=CB,
                               into={0: (dproj, n_main, c_xl // CB)})
    dyraw, dproj, d_ssd_norm = _grouped(_ssd_gate_bwd, [yraw, (proj, 0), (dycat, 0)], [p["ssd_norm"]], [F32, BF16], [(1, d_ssd)],
                                        gw=GROUP_W, name="b_ssdgate", tr=TR, into={1: (dproj, n_main, 0)})
    dact, ddtraw, dpar = carry(_ssd_bwd, act, dtraw, dyraw, hst, p["dtb"], p["alog"], p["drow"], name="b_ssd")
    dproj, dwb_ssd = carry(_conv_bwd, dact, 0, pre, proj, c_xbc, d_xbc, p["ssd_conv_w"], silu=True, name="b_ssdconv", cb=CB,
                           into={0: (dproj, n_main, c_xbc // CB)})
    early = dict(loss=loss, ssd_conv=dwb_ssd, ssd_par=dpar, ssd_norm=d_ssd_norm, lru_conv=dwb_lru, lru_w_a=d_wa, lru_b_a=d_ba,
                 lru_w_x=d_wx, lru_b_x=d_bx, lru_lambda=d_lam, lru_norm=d_lru_norm, post_mix_norm=d_post_mix,
                 pre_mlp_norm=d_pre_mlp, post_mlp_norm=d_post_mlp)
    plan.small(early)
    (dw_main,) = carry(_matmul, h, dproj, mode="tn", m=D, n=n_main, k=L, out_dtypes=[BF16], name="b_inproj_dw", **mmt)
    (dw_dt,) = _matmul(h, ddtraw, mode="tn", m=D, n=128, k=L, out_dtypes=[BF16], name="b_dtproj_dw", **mm)
    plan.grad("w_in", (dw_main, dw_dt))
    (dh_a,) = carry(_matmul, dproj, w_main, mode="nt", m=L, n=D, k=n_main, out_dtypes=[F32], name="b_inproj_dx", **mm)
    (dh_b,) = carry(_matmul, ddtraw, w_dt, mode="nt", m=L, n=D, k=128, out_dtypes=[F32], name="b_dtproj_dx", **mm)
    grad_x, d_pre_mix = carry(_rowwise, _in_bwd, [x, dh_a, dh_b, dx1], [p["pre_mix_norm"]], [(D, F32)], [(1, D)],
                              name="b_prenorm", tr=TRW, sub=SUB)

    return grad_x, dict(early, pre_mix_norm=d_pre_mix)


def _place():
    return lax.axis_index("x"), lax.axis_index("y"), lax.axis_index("c")


def _other_chips(x, y):
    return [(1 - x, y), (x, 1 - y), (1 - x, 1 - y)]


def _allgather8(blk, *, name):
    r, n = blk.shape

    def body(x_ref, out_ref, send_sems, recv_sems, local_sem):
        x, y, c = _place()
        me, sibling = (x, y, c), (x, y, 1 - c)
        chips = _other_chips(x, y)

        def rows(px, py, pc):
            return out_ref.at[pl.ds((4 * px + 2 * py + pc) * r, r), :]

        def copy(k, block, to, src=None):
            return pltpu.make_async_remote_copy(
                src_ref=rows(*block) if src is None else src, dst_ref=rows(*block),
                send_sem=send_sems.at[k], recv_sem=recv_sems.at[k], device_id=to, device_id_type=MESH)

        mine = pltpu.make_async_copy(x_ref, rows(*me), local_sem)
        mine.start()
        first = [copy(0, me, sibling, src=x_ref)]
        first += [copy(1 + k, me, (*chip, c), src=x_ref) for k, chip in enumerate(chips)]
        for cp in first:
            cp.start()
        passed = [copy(4 + k, (*chip, c), sibling) for k, chip in enumerate(chips)]
        for k, chip in enumerate(chips):
            copy(1 + k, (*chip, c), me).wait_recv()
            passed[k].start()
        copy(0, sibling, me).wait_recv()
        for k, chip in enumerate(chips):
            copy(4 + k, (*chip, 1 - c), me).wait_recv()
        for cp in first + passed:
            cp.wait_send()
        mine.wait()

    return pl.pallas_call(
        body, name=name,
        out_shape=jax.ShapeDtypeStruct((8 * r, n), blk.dtype),
        in_specs=[pl.BlockSpec(memory_space=pltpu.VMEM)], out_specs=pl.BlockSpec(memory_space=pltpu.VMEM),
        scratch_shapes=[pltpu.SemaphoreType.DMA((7,)), pltpu.SemaphoreType.DMA((7,)), pltpu.SemaphoreType.DMA],
        compiler_params=pltpu.CompilerParams(vmem_limit_bytes=VMEM_LIMIT),
    )(blk)


def _sum8(g, *, name):
    _, r, n = g.shape
    tr = max(t for t in range(8, min(r, 512) + 1, 8) if r % t == 0)

    def body(g_ref, o_ref):
        s = g_ref[0]
        for k in range(1, 8):
            s = s + g_ref[k]
        o_ref[...] = s

    return pl.pallas_call(
        body, name=name, grid=(r // tr,),
        in_specs=[pl.BlockSpec((8, tr, n), lambda i: (0, i, 0))], out_specs=pl.BlockSpec((tr, n), lambda i: (i, 0)),
        out_shape=jax.ShapeDtypeStruct((r, n), g.dtype), compiler_params=_params(("parallel",)),
    )(g)


def _blocks(fn, ins, outs, *, grid, name, prefetch=None, aliases=None):
    n_in = len(ins)

    def body(*refs):
        if prefetch is not None:
            refs = refs[1:]
        res = fn(*[r[...] for r in refs[:n_in]])
        for o_ref, o in zip(refs[n_in:], res):
            o_ref[...] = o.astype(o_ref.dtype)

    in_specs = [pl.BlockSpec(b, m) for _, b, m in ins]
    out_specs = [pl.BlockSpec(b, m) for _, b, m in outs]
    kw = dict(name=name, out_shape=[s for s, _, _ in outs], input_output_aliases=aliases or {},
              compiler_params=_params(("arbitrary",) * len(grid)))
    arrs = [a for a, _, _ in ins]
    if prefetch is None:
        return pl.pallas_call(body, grid=grid, in_specs=in_specs, out_specs=out_specs, **kw)(*arrs)
    spec = pltpu.PrefetchScalarGridSpec(num_scalar_prefetch=1, grid=grid, in_specs=in_specs, out_specs=out_specs)
    return pl.pallas_call(body, grid_spec=spec, **kw)(prefetch, *arrs)


def _gather_tree(slot, *, name):
    def body(i_ref, g_ref, send_sems, recv_sems):
        x, y, c = _place()
        j, jx, jy, jd = 2 * x + y, 2 * (1 - x) + y, 2 * x + (1 - y), 2 * (1 - x) + (1 - y)
        xn, yn, sibling = (1 - x, y, c), (x, 1 - y, c), (x, y, 1 - c)
        rp = i_ref.shape[1] // 4

        def piece(ref, chip, hc, q):
            return ref.at[chip, pl.ds((2 * hc + q) * rp, rp), :]

        def cp(k, src, dst, to):
            return pltpu.make_async_remote_copy(src_ref=src, dst_ref=dst, send_sem=send_sems.at[k], recv_sem=recv_sems.at[k],
                                                device_id=to, device_id_type=MESH)

        def landed(k, chip, q):
            return cp(k, piece(g_ref, chip, c, q), piece(g_ref, chip, c, q), xn)

        sends = []

        def go(d):
            d.start()
            sends.append(d)

        go(cp(0, piece(i_ref, j, c, 0), piece(g_ref, j, c, 0), xn))
        go(cp(2, piece(i_ref, j, c, 1), piece(g_ref, j, c, 1), yn))
        go(cp(1, piece(i_ref, j, c, 1), piece(g_ref, j, c, 1), xn))
        go(cp(3, piece(i_ref, j, c, 0), piece(g_ref, j, c, 0), yn))
        arrivals = [(0, jx, 0, 4, yn), (2, jy, 1, 5, xn), (1, jx, 1, None, None), (3, jy, 0, None, None),
                    (4, jd, 0, None, None), (5, jd, 1, None, None)]
        for n_arr, (k, chip, q, k_fwd, to) in enumerate(arrivals):
            landed(k, chip, q).wait_recv()
            if k_fwd is not None:
                go(cp(k_fwd, piece(g_ref, chip, c, q), piece(g_ref, chip, c, q), to))
            go(cp(6 + n_arr, piece(g_ref, chip, c, q), piece(g_ref, chip, c, q), sibling))
        for n_arr, (_, chip, q, _, _) in enumerate(arrivals):
            other = piece(g_ref, chip, 1 - c, q)
            cp(6 + n_arr, other, other, sibling).wait_recv()
        for d in sends:
            d.wait_send()

    return pl.pallas_call(
        body, name=name, out_shape=jax.ShapeDtypeStruct(slot.shape, slot.dtype),
        in_specs=[ANY], out_specs=ANY, input_output_aliases={0: 0},
        scratch_shapes=[pltpu.SemaphoreType.DMA((12,)), pltpu.SemaphoreType.DMA((12,))],
    )(slot)


def _pack(arrs):
    parts = []
    for v in arrs:
        f = v.reshape(-1)
        f = jnp.pad(f, (0, (-f.shape[0]) % 1024))
        parts.append(f.reshape(-1, 128))
    return jnp.concatenate(parts, axis=0)


def _unpack(packed, shapes):
    out, r0 = [], 0
    for s in shapes:
        size = 1
        for d in s:
            size *= d
        nr = (size + 1023) // 1024 * 8
        out.append(packed[r0:r0 + nr].reshape(-1)[:size].reshape(s))
        r0 += nr
    return out


SMALL_GRADS = [("loss", (1, 128)), ("pre_mix_norm", (1, 2048)), ("ssd_conv", (8, 4096)), ("ssd_par", (8, 128)),
               ("ssd_norm", (1, 2048)), ("lru_conv", (8, 2048)), ("lru_w_a", (16, 128, 128)), ("lru_b_a", (1, 2048)),
               ("lru_w_x", (16, 128, 128)), ("lru_b_x", (1, 2048)), ("lru_lambda", (1, 2048)), ("lru_norm", (1, 2048)),
               ("post_mix_norm", (1, 2048)), ("pre_mlp_norm", (1, 2048)), ("post_mlp_norm", (1, 2048))]

SMALL_EARLY = [g for g in SMALL_GRADS if g[0] != "pre_mix_norm"]

WEIGHTS = ['pre_mix_norm', 'w_in', 'ssd_conv_w', 'ssd_conv_b', 'ssd_dt_bias', 'ssd_a_log', 'ssd_d', 'ssd_norm', 'lru_conv_w',
           'lru_conv_b', 'lru_w_a', 'lru_b_a', 'lru_w_x', 'lru_b_x', 'lru_lambda', 'lru_norm', 'w_out', 'post_mix_norm',
           'pre_mlp_norm', 'w_mlp_in', 'w_mlp_out', 'post_mlp_norm']
LARGE = ['w_in', 'w_out', 'w_mlp_in', 'w_mlp_out']

D_SSD, D_XBC, DT_W = 2048, 4096, 32
TB = 256


def _w_in_runs(n_shard, n_main):
    c_dt = D_SSD + D_XBC
    runs, p = [], 0
    while p < n_main:
        j, off = divmod(p if p < c_dt else p + DT_W, n_shard)
        ln = min(n_shard - off, (c_dt if p < c_dt else n_main) - p)
        runs.append((p, j, off, ln))
        p += ln
    jd, offd = divmod(c_dt, n_shard)
    assert offd + DT_W <= n_shard
    return runs, (jd, offd)


def _pack_w_in(slots, *, name):
    _, d, n_shard = slots.shape
    n_main = 4 * n_shard - DT_W
    runs, (jd, offd) = _w_in_runs(n_shard, n_main)

    def body(s_ref, main_ref, dt_ref):
        for p, j, off, ln in runs:
            main_ref[:, p:p + ln] = s_ref[j, :, off:off + ln]
        dt_ref[:, 0:DT_W] = s_ref[jd, :, offd:offd + DT_W]
        dt_ref[:, DT_W:] = jnp.zeros((TB, 128 - DT_W), dt_ref.dtype)

    return pl.pallas_call(
        body, name=name, grid=(d // TB,),
        in_specs=[pl.BlockSpec((4, TB, n_shard), lambda i: (0, i, 0))],
        out_specs=[pl.BlockSpec((TB, n_main), lambda i: (i, 0)), pl.BlockSpec((TB, 128), lambda i: (i, 0))],
        out_shape=[jax.ShapeDtypeStruct((d, n_main), slots.dtype), jax.ShapeDtypeStruct((d, 128), slots.dtype)],
        compiler_params=_params(("parallel",)),
    )(slots)


def _unpack_dw_in(dw_main, dw_dt, *, name):
    d, n_main = dw_main.shape
    n_shard = (n_main + DT_W) // 4
    runs, (jd, offd) = _w_in_runs(n_shard, n_main)

    def body(main_ref, dt_ref, o_ref):
        for p, j, off, ln in runs:
            o_ref[j, :, off:off + ln] = main_ref[:, p:p + ln]
        o_ref[jd, :, offd:offd + DT_W] = dt_ref[:, 0:DT_W]

    return pl.pallas_call(
        body, name=name, grid=(d // TB,),
        in_specs=[pl.BlockSpec((TB, n_main), lambda i: (i, 0)), pl.BlockSpec((TB, 128), lambda i: (i, 0))],
        out_specs=pl.BlockSpec((4, TB, n_shard), lambda i: (0, i, 0)),
        out_shape=jax.ShapeDtypeStruct((4, d, n_shard), dw_main.dtype),
        compiler_params=_params(("parallel",)),
    )(dw_main, dw_dt)


def _half(ref, chip_idx, hc, piece=(0, 1)):
    q, nq = piece
    rp = ref.shape[1] // (2 * nq)
    return ref.at[chip_idx, pl.ds((hc * nq + q) * rp, rp), :]


def _job_gather_ici(i_ref, g_ref, send_sems, recv_sems, base, piece):
    x, y, c = _place()
    j = 2 * x + y
    sends, recvs = [], []
    for k, (px, py) in enumerate(_other_chips(x, y)):
        kw = dict(send_sem=send_sems.at[base + k], recv_sem=recv_sems.at[base + k], device_id=(px, py, c), device_id_type=MESH)
        sends.append(pltpu.make_async_remote_copy(src_ref=_half(i_ref, j, c, piece), dst_ref=_half(g_ref, j, c, piece), **kw))
        landed = _half(g_ref, 2 * px + py, c, piece)
        recvs.append(pltpu.make_async_remote_copy(src_ref=landed, dst_ref=landed, **kw))
    return sends, recvs


def _job_gather_sibling(i_ref, g_ref, send_sems, recv_sems, base, piece):
    x, y, c = _place()
    sends, recvs = [], []
    for k, (px, py) in enumerate(_other_chips(x, y)):
        kw = dict(send_sem=send_sems.at[base + k], recv_sem=recv_sems.at[base + k], device_id=(x, y, 1 - c), device_id_type=MESH)
        sends.append(pltpu.make_async_remote_copy(src_ref=_half(i_ref, 2 * px + py, c, piece),
                                                  dst_ref=_half(g_ref, 2 * px + py, c, piece), **kw))
        other = _half(g_ref, 2 * px + py, 1 - c, piece)
        recvs.append(pltpu.make_async_remote_copy(src_ref=other, dst_ref=other, **kw))
    return sends, recvs


def _job_reduce_ici(s_ref, got_refs, send_sems, recv_sems, base, piece):
    x, y, c = _place()
    q, nq, count = piece
    rp = s_ref.shape[1] // nq
    rows = pl.ds(q * rp, count * rp)
    sends = [pltpu.make_async_remote_copy(src_ref=s_ref.at[2 * px + py, rows, :], dst_ref=got_refs[k].at[rows, :],
                                          send_sem=send_sems.at[base + k], recv_sem=recv_sems.at[base + k],
                                          device_id=(px, py, c), device_id_type=MESH)
             for k, (px, py) in enumerate(_other_chips(x, y))]
    return sends, sends


GATHER_PLAN = {
    "f_inproj": [("w_out", _job_gather_ici, (0, 1)), ("w_mlp_in", _job_gather_ici, (0, 2))],
    "f_ssdconv": [("w_mlp_in", _job_gather_ici, (1, 2)), ("w_out", _job_gather_sibling, (0, 1)),
                  ("w_mlp_in", _job_gather_sibling, (0, 2))],
    "f_lruconv": [("w_mlp_in", _job_gather_sibling, (1, 2))],
    "f_ssd": [("w_mlp_out", _job_gather_ici, (0, 2))],
    "f_lrugates": [("w_mlp_out", _job_gather_sibling, (0, 2))],
    "f_outproj": [("w_mlp_out", _job_gather_ici, (1, 2))],
    "f_mlpin": [("w_mlp_out", _job_gather_sibling, (1, 2))],
}
SIBLING_PLAN = {"b_mlpin_dx": "w_mlp_out", "b_mid": "w_mlp_in", "b_outproj_dx": "w_out"}
REDUCE_PLAN = {
    "b_mlpin_dw": [("w_mlp_out", (0, 2, 1))],
    "b_outproj_dw": [("w_mlp_out", (1, 2, 1))],
    "b_ssd": [("w_mlp_in", (0, 1, 1)), ("w_out", (0, 1, 1))],
    "b_inproj_dx": [("w_in", (0, 8, 7))],
    "b_dtproj_dx": [("w_in", (7, 8, 1))],
}
SHARE_PLAN = {"b_ssdconv": ["w_mlp_out", "w_mlp_in", "w_out"]}
SMALL_PLAN = {"b_inproj_dw": 0, "b_inproj_dx": 1}


class _Part:
    def __init__(self, ins, outs, aliases, n_sems, make, done):
        self.ins, self.outs, self.aliases, self.n_sems, self.make, self.done = ins, outs, aliases, n_sems, make, done


def _merge_parts(parts):
    ins, outs, aliases, offs, n = [], [], {}, [], 0
    for p in parts:
        offs.append((len(ins), len(outs), n))
        aliases.update({len(ins) + i: len(outs) + o for i, o in p.aliases.items()})
        ins, outs, n = ins + list(p.ins), outs + list(p.outs), n + p.n_sems

    def make(i_refs, o_refs, send_sems, recv_sems):
        sends, recvs = [], []
        for p, (io, oo, so) in zip(parts, offs):
            s, r = p.make(i_refs[io:io + len(p.ins)], o_refs[oo:oo + len(p.outs)], send_sems, recv_sems, so)
            sends, recvs = sends + s, recvs + r
        return sends, recvs

    return _Side(ins, outs, aliases, n, make)


def _parts_done(parts, outs):
    o = 0
    for p in parts:
        p.done(list(outs[o:o + len(p.outs)]))
        o += len(p.outs)


def _comm_call(parts, *, name):
    side = _merge_parts(parts)
    si, so = len(side.ins), len(side.outs)

    def body(*refs):
        sends, recvs = side.make(refs[:si], refs[si:si + so], refs[-2], refs[-1])
        for d in sends:
            d.start()
        for d in recvs:
            d.wait_recv()
        for d in sends:
            d.wait_send()

    outs = pl.pallas_call(
        body, name=name, out_shape=list(side.outs), in_specs=[ANY] * si, out_specs=[ANY] * so,
        scratch_shapes=[pltpu.SemaphoreType.DMA((side.n_sems,)), pltpu.SemaphoreType.DMA((side.n_sems,))],
        input_output_aliases=side.aliases,
    )(*side.ins)
    _parts_done(parts, outs)


class _DistPlan:
    def __init__(self, slots, w_main, w_dt, where, shapes):
        self.slots, self.w_main, self.w_dt, self.where, self.shapes = slots, w_main, w_dt, where, shapes
        self.sib_pending, self.partial, self.got, self.totals, self.shared = {}, {}, {}, {}, {}
        self.small_buf = None

    def weight(self, name):
        d = self.w_main.shape[0]
        return {"w_main": lambda: self.w_main, "w_dt": lambda: self.w_dt, "w_out": lambda: self.slots["w_out"].reshape(-1, d),
                "w_mi": lambda: self.slots["w_mlp_in"], "w_mo": lambda: self.slots["w_mlp_out"].reshape(-1, d)}[name]()

    def _gather_part(self, jobs):
        names = []
        for n, _, _ in jobs:
            if n not in names:
                names.append(n)
        arrs = [self.slots[n] for n in names]

        def make(i_refs, o_refs, send_sems, recv_sems, base):
            sends, recvs = [], []
            for q, (n, job, piece) in enumerate(jobs):
                s, r = job(i_refs[names.index(n)], o_refs[names.index(n)], send_sems, recv_sems, base + 3 * q, piece)
                sends, recvs = sends + s, recvs + r
            return sends, recvs

        return _Part(arrs, [jax.ShapeDtypeStruct(a.shape, a.dtype) for a in arrs], {i: i for i in range(len(arrs))},
                     3 * len(jobs), make, lambda outs: self.slots.update(zip(names, outs)))

    def _sibling_part(self, name):
        g = self.sib_pending.pop(name)
        _, _, rh, cc = g.shape

        def make(i_refs, o_refs, send_sems, recv_sems, base):
            x, y, c = _place()
            sends = [pltpu.make_async_remote_copy(src_ref=i_refs[0].at[s, 1 - c], dst_ref=o_refs[0].at[s],
                                                  send_sem=send_sems.at[base + s], recv_sem=recv_sems.at[base + s],
                                                  device_id=(x, y, 1 - c), device_id_type=MESH) for s in range(4)]
            return sends, sends

        def done(outs):
            (t,) = outs
            self.partial[name] = _blocks(
                lambda u, v: (u.astype(F32) + v.astype(F32),),
                [(g, (None, None, TB, cc), lambda q, i, s: (q, s[1], i, 0)), (t, (None, TB, cc), lambda q, i, s: (q, i, 0))],
                [(jax.ShapeDtypeStruct(t.shape, BF16), (None, TB, cc), lambda q, i, s: (q, i, 0))],
                grid=(4, rh // TB), name="add_sibling_" + name, prefetch=self.where)[0]

        return _Part([g], [jax.ShapeDtypeStruct((4, rh, cc), g.dtype)], {}, 4, make, done)

    def _reduce_part(self, items):
        parts = [self.partial[n] for n, _ in items]
        ins, aliases = list(parts), {}
        for q, (n, (first, _, _)) in enumerate(items):
            if first > 0:
                for k in range(3):
                    aliases[len(ins)] = 3 * q + k
                    ins.append(self.got[n][k])

        def make(i_refs, o_refs, send_sems, recv_sems, base):
            sends = []
            for q, (_, piece) in enumerate(items):
                sends += _job_reduce_ici(i_refs[q], o_refs[3 * q:3 * q + 3], send_sems, recv_sems, base + 3 * q, piece)[0]
            return sends, sends

        def done(outs):
            for q, (name, (first, n_pieces, count)) in enumerate(items):
                g3 = self.got[name] = outs[3 * q:3 * q + 3]
                if first + count < n_pieces:
                    continue
                s4 = self.partial[name]
                _, rh, cc = s4.shape
                row = ((TB, cc), lambda i, s: (i, 0))
                self.totals[name] = _blocks(
                    lambda o, r0, r1, r2: (((o.astype(F32) + r0.astype(F32)) + r1.astype(F32)) + r2.astype(F32),),
                    [(s4, (None, TB, cc), lambda i, s: (s[0], i, 0)), (g3[0],) + row, (g3[1],) + row, (g3[2],) + row],
                    [(jax.ShapeDtypeStruct((2, rh, cc), F32), (None, TB, cc), lambda i, s: (s[1], i, 0))],
                    grid=(rh // TB,), name="add_chips_" + name, prefetch=self.where)[0]
                if name == "w_in":
                    _comm_call([self._share_part([name])], name="share_" + name)

        outs = [jax.ShapeDtypeStruct(s.shape[1:], s.dtype) for s in parts for _ in range(3)]
        return _Part(ins, outs, aliases, 3 * len(items), make, done)

    def _share_part(self, names):
        ts = [self.totals[n] for n in names]

        def make(i_refs, o_refs, send_sems, recv_sems, base):
            x, y, c = _place()
            sends, recvs = [], []
            for w in range(len(ts)):
                kw = dict(send_sem=send_sems.at[base + w], recv_sem=recv_sems.at[base + w], device_id=(x, y, 1 - c),
                          device_id_type=MESH)
                sends.append(pltpu.make_async_remote_copy(src_ref=i_refs[w].at[c], dst_ref=o_refs[w].at[c], **kw))
                recvs.append(pltpu.make_async_remote_copy(src_ref=i_refs[w].at[c], dst_ref=o_refs[w].at[1 - c], **kw))
            return sends, recvs

        def done(outs):
            self.shared.update({n: o.reshape(self.shapes[n]) for n, o in zip(names, outs)})

        return _Part(ts, [jax.ShapeDtypeStruct(t.shape, t.dtype) for t in ts], {i: i for i in range(len(ts))}, len(ts), make, done)

    def _small_part(self, stage):
        buf = self.small_buf

        def make(i_refs, o_refs, send_sems, recv_sems, base):
            x, y, c = _place()
            src, dst = i_refs[0], o_refs[0]
            sends, recvs = [], []

            def add(k, block, to, frm):
                kw = dict(send_sem=send_sems.at[base + k], recv_sem=recv_sems.at[base + k], device_id=to, device_id_type=MESH)
                sends.append(pltpu.make_async_remote_copy(src_ref=src.at[block], dst_ref=dst.at[block], **kw))
                recvs.append(pltpu.make_async_remote_copy(src_ref=src.at[frm], dst_ref=dst.at[frm], **kw))

            if stage == 0:
                add(0, 4 * x + 2 * y + c, (x, y, 1 - c), 4 * x + 2 * y + 1 - c)
            for k, (px, py) in enumerate(_other_chips(x, y)):
                if stage == 0:
                    add(1 + k, 4 * x + 2 * y + c, (px, py, c), 4 * px + 2 * py + c)
                else:
                    add(k, 4 * px + 2 * py + c, (x, y, 1 - c), 4 * px + 2 * py + 1 - c)
            return sends, recvs

        def done(outs):
            (self.small_buf,) = outs

        return _Part([buf], [jax.ShapeDtypeStruct(buf.shape, buf.dtype)], {0: 0}, 4 if stage == 0 else 3, make, done)

    def side(self, kernel_name):
        parts = []
        if kernel_name in GATHER_PLAN:
            parts.append(self._gather_part(GATHER_PLAN[kernel_name]))
        if kernel_name in SIBLING_PLAN:
            parts.append(self._sibling_part(SIBLING_PLAN[kernel_name]))
        if kernel_name in REDUCE_PLAN:
            parts.append(self._reduce_part(REDUCE_PLAN[kernel_name]))
        if kernel_name in SHARE_PLAN:
            parts.append(self._share_part(SHARE_PLAN[kernel_name]))
        if kernel_name in SMALL_PLAN:
            parts.append(self._small_part(SMALL_PLAN[kernel_name]))
        self._carried = parts
        return _merge_parts(parts) if parts else None

    def done(self, kernel_name, side_outs):
        _parts_done(self._carried, side_outs)

    def grad(self, name, g):
        if name == "w_in":
            g = _unpack_dw_in(*g, name="unpack_dw_in")
        self.sib_pending[name] = g.reshape(4, 2, g.shape[1] // 2, g.shape[2])
        if name == "w_in":
            _comm_call([self._sibling_part(name)], name="reduce_sibling_" + name)

    def small(self, grads):
        packed = _pack([grads[n] for n, _ in SMALL_EARLY])
        n_rows = packed.shape[0]
        self.small_buf = _blocks(lambda t: (t,), [(packed, (n_rows, 128), lambda i, s: (0, 0))],
                                 [(jax.ShapeDtypeStruct((8, n_rows, 128), F32), (None, n_rows, 128), lambda i, s: (s[2], 0, 0))],
                                 grid=(1,), name="place_small", prefetch=self.where)[0]

    def finish(self):
        assert len(self.shared) == len(self.shapes)
        return self.shared


def kernel(x, pre_mix_norm, w_in, ssd_conv_w, ssd_conv_b, ssd_dt_bias, ssd_a_log, ssd_d, ssd_norm, lru_conv_w, lru_conv_b, lru_w_a, lru_b_a, lru_w_x, lru_b_x, lru_lambda, lru_norm, w_out, post_mix_norm, pre_mlp_norm, w_mlp_in, w_mlp_out, post_mlp_norm, loss_target, m_pre_mix_norm, m_w_in, m_ssd_conv_w, m_ssd_conv_b, m_ssd_dt_bias, m_ssd_a_log, m_ssd_d, m_ssd_norm, m_lru_conv_w, m_lru_conv_b, m_lru_w_a, m_lru_b_a, m_lru_w_x, m_lru_b_x, m_lru_lambda, m_lru_norm, m_w_out, m_post_mix_norm, m_pre_mlp_norm, m_w_mlp_in, m_w_mlp_out, m_post_mlp_norm, v_pre_mix_norm, v_w_in, v_ssd_conv_w, v_ssd_conv_b, v_ssd_dt_bias, v_ssd_a_log, v_ssd_d, v_ssd_norm, v_lru_conv_w, v_lru_conv_b, v_lru_w_a, v_lru_b_a, v_lru_w_x, v_lru_b_x, v_lru_lambda, v_lru_norm, v_w_out, v_post_mix_norm, v_pre_mlp_norm, v_w_mlp_in, v_w_mlp_out, v_post_mlp_norm):
    a = dict(locals())
    j = 2 * lax.axis_index("x") + lax.axis_index("y")
    D = x.shape[-1]
    c_dt = D_SSD + D_XBC

    core = lax.axis_index("c")
    where = jnp.stack([j, core, 2 * j + core]).astype(jnp.int32)
    slots = {}
    for name in LARGE:
        w = a[name][0]
        r, cc = w.shape
        slots[name] = _blocks(lambda t: (t,), [(w, (TB, cc), lambda i, s: (i, 0))],
                              [(jax.ShapeDtypeStruct((4, r, cc), BF16), (None, TB, cc), lambda i, s: (s[0], i, 0))],
                              grid=(r // TB,), name="cast_" + name, prefetch=where)[0]
    g_in = _gather_tree(slots.pop("w_in"), name="gather_w_in")
    w_main, w_dt = _pack_w_in(g_in, name="pack_w_in")
    taps = jnp.concatenate([ssd_conv_w[0].reshape(-1, 128), lru_conv_w[0].reshape(-1, 128)], axis=0)
    taps = _allgather8(taps, name="gather_taps").reshape(8, taps.shape[0], 128)[0::2]
    n_ssd = ssd_conv_w.shape[1] * ssd_conv_w.shape[2] // 128
    ssd_taps = taps[:, :n_ssd].reshape(4, CONV_W, -1).transpose(1, 0, 2).reshape(CONV_W, -1)
    lru_taps = taps[:, n_ssd:].reshape(4, CONV_W, -1).transpose(1, 0, 2).reshape(CONV_W, -1)

    def row128(v):
        return jnp.pad(v, ((0, 0), (0, 128 - v.shape[1])))

    p = dict(pre_mix_norm=pre_mix_norm, ssd_conv_w=ssd_taps, ssd_conv_b=ssd_conv_b,
             dtb=row128(ssd_dt_bias), alog=row128(ssd_a_log), drow=row128(ssd_d), ssd_norm=ssd_norm,
             lru_conv_w=lru_taps, lru_conv_b=lru_conv_b, lru_w_a=lru_w_a[0], lru_b_a=lru_b_a.reshape(1, -1),
             lru_w_x=lru_w_x[0], lru_b_x=lru_b_x.reshape(1, -1), lru_lambda=lru_lambda, lru_norm=lru_norm,
             post_mix_norm=post_mix_norm, pre_mlp_norm=pre_mlp_norm, post_mlp_norm=post_mlp_norm)
    plan = _DistPlan(slots, w_main, w_dt, where, {n: a[n].shape for n in LARGE})
    grad_x, small = _local_step(x[0], loss_target[0], p, plan)

    large_grads = plan.finish()

    delta, new_m, new_v = {}, {}, {}
    for name in LARGE:
        w = a[name][0]
        cc = w.shape[1]
        outs = _rowwise(_adamw, [w, large_grads[name][0], a["m_" + name][0], a["v_" + name][0]], [], [(cc, F32)] * 3, [],
                        name="adamw_" + name, tr=128, sub=8)
        delta[name], new_m[name], new_v[name] = [o[None] for o in outs]

    total = _sum8(plan.small_buf, name="sum_small")
    tot = dict(zip([n for n, _ in SMALL_EARLY], _unpack(total, [s for _, s in SMALL_EARLY])))
    late = small["pre_mix_norm"].reshape(-1, 128)
    late = _allgather8(late, name="gather_late").reshape(8, late.shape[0], 128)
    tot["pre_mix_norm"] = _sum8(late, name="sum_late").reshape(small["pre_mix_norm"].shape)
    loss = tot["loss"][0, 0]
    n_sc, n_lc = ssd_conv_w.shape[2], lru_conv_w.shape[2]
    grads = dict(
        pre_mix_norm=tot["pre_mix_norm"],
        ssd_conv_w=lax.dynamic_slice(tot["ssd_conv"][:CONV_W], (0, j * n_sc), (CONV_W, n_sc))[None],
        ssd_conv_b=tot["ssd_conv"][CONV_W:CONV_W + 1],
        ssd_dt_bias=tot["ssd_par"][0:1, :DT_W], ssd_a_log=tot["ssd_par"][1:2, :DT_W], ssd_d=tot["ssd_par"][2:3, :DT_W],
        ssd_norm=tot["ssd_norm"],
        lru_conv_w=lax.dynamic_slice(tot["lru_conv"][:CONV_W], (0, j * n_lc), (CONV_W, n_lc))[None],
        lru_conv_b=tot["lru_conv"][CONV_W:CONV_W + 1],
        lru_w_a=tot["lru_w_a"][None], lru_b_a=tot["lru_b_a"].reshape(lru_b_a.shape),
        lru_w_x=tot["lru_w_x"][None], lru_b_x=tot["lru_b_x"].reshape(lru_b_x.shape),
        lru_lambda=tot["lru_lambda"], lru_norm=tot["lru_norm"], post_mix_norm=tot["post_mix_norm"],
        pre_mlp_norm=tot["pre_mlp_norm"], post_mlp_norm=tot["post_mlp_norm"])

    grads.update(large_grads)

    small_w = [n for n in WEIGHTS if n not in LARGE]
    n_sw = len(small_w)

    def flat(v):
        return v.reshape(-1, v.shape[-1])

    def adamw_all(*refs):
        ins, outs = refs[:4 * n_sw], refs[4 * n_sw:]
        for q in range(n_sw):
            res = _adamw(*[ins[4 * q + t][...] for t in range(4)])
            for t in range(3):
                outs[3 * q + t][...] = res[t]

    operands = [flat(d[n]) for n in small_w for d in (a, grads, {k: a["m_" + k] for k in small_w}, {k: a["v_" + k] for k in small_w})]
    outs = pl.pallas_call(
        adamw_all, name="adamw_small",
        out_shape=[jax.ShapeDtypeStruct(flat(a[n]).shape, F32) for n in small_w for _ in range(3)],
        in_specs=[pl.BlockSpec(memory_space=pltpu.VMEM)] * (4 * n_sw), out_specs=[pl.BlockSpec(memory_space=pltpu.VMEM)] * (3 * n_sw),
        compiler_params=pltpu.CompilerParams(vmem_limit_bytes=VMEM_LIMIT),
    )(*operands)
    for q, n in enumerate(small_w):
        delta[n], new_m[n], new_v[n] = [o.reshape(a[n].shape) for o in outs[3 * q:3 * q + 3]]

    return (loss, grad_x[None], *[grads[n] for n in WEIGHTS], *[delta[n] for n in WEIGHTS],
            *[new_m[n] for n in WEIGHTS], *[new_v[n] for n in WEIGHTS])
```
